```python
import jax, jax.numpy as jnp
from jax import lax
import numpy as np

D_MODEL = 2048
BATCH = 8
SEQ = 4096
DEPTH = 2

GRID_W = 64
CTX_LEN = 256
HEAD_DIM = 128
AXIS_DIM = HEAD_DIM // 2
ATT_Q_HEADS = 12
ATT_KV_HEADS = 4
ATT_GROUPS = ATT_Q_HEADS // ATT_KV_HEADS
ATT_WIDTH = ATT_Q_HEADS * HEAD_DIM
KV_WIDTH = ATT_KV_HEADS * HEAD_DIM
AUX_WIDTH = D_MODEL - ATT_WIDTH
AUX_GROUPS = 4
AUX_GROUP_DIM = AUX_WIDTH // AUX_GROUPS
CONV_WIDTH = 3
POOL_WINDOWS = (2, 4, 8, 16)
WINDOW = 128
Q_BLOCK = 128
BAND = Q_BLOCK + 2 * WINDOW
ROPE_THETA = 10000.0
FFN_HIDDEN = -(-8 * D_MODEL // (3 * 256)) * 256
N_MOD = 6
EPS = 1e-6
NEG_INF = -1e30
ATT_SCALE = HEAD_DIM ** -0.5
IN_WIDTHS_AB = (ATT_WIDTH, KV_WIDTH, KV_WIDTH, AUX_WIDTH, AUX_WIDTH, AUX_WIDTH)
IN_WIDTHS_CD = (ATT_WIDTH, KV_WIDTH, KV_WIDTH, AUX_WIDTH)

kernel_name = "hybrid_prefix_dit_block"


def _split_points(widths):
    return [int(v) for v in np.cumsum(widths)[:-1]]


def _rms_norm(x, g):
    xf = x.astype(jnp.float32)
    y = xf * lax.rsqrt(jnp.mean(xf * xf, axis=-1, keepdims=True) + EPS)
    return (y * g.astype(jnp.float32)).astype(x.dtype)


def _modulate(h, shift, scale):
    return h * (1 + scale) + shift


def _axial_rope(n):
    rows = n // GRID_W
    row = jnp.broadcast_to(jnp.arange(rows, dtype=jnp.float32)[:, None], (rows, GRID_W)).reshape(-1)
    col = jnp.broadcast_to(jnp.arange(GRID_W, dtype=jnp.float32)[None, :], (rows, GRID_W)).reshape(-1)
    inv = jnp.power(ROPE_THETA, -jnp.arange(0, AXIS_DIM, 2, dtype=jnp.float32) / AXIS_DIM)
    ang_r = row[:, None] * inv
    ang_c = col[:, None] * inv
    return (jnp.cos(ang_r), jnp.sin(ang_r), jnp.cos(ang_c), jnp.sin(ang_c))


def _rotate(x, cos, sin):
    x1, x2 = jnp.split(x, 2, axis=-1)
    c = cos[:, None, :].astype(x.dtype)
    s = sin[:, None, :].astype(x.dtype)
    return jnp.concatenate([x1 * c - x2 * s, x2 * c + x1 * s], axis=-1)


def _apply_axial_rope(x, rope):
    cr, sr, cc, sc = rope
    return jnp.concatenate([_rotate(x[..., :AXIS_DIM], cr, sr), _rotate(x[..., AXIS_DIM:], cc, sc)], axis=-1)


def _softmax_attend(q, k, v, bias=None, sink=None):
    s = jnp.einsum('bqkgd,bnkd->bkgqn', q, k).astype(jnp.float32) * ATT_SCALE
    if bias is not None:
        s = s + bias
    if sink is not None:
        sink_col = jnp.broadcast_to(sink.astype(jnp.float32)[None, :, :, None, None], s.shape[:-1] + (1,))
        p = jax.nn.softmax(jnp.concatenate([s, sink_col], axis=-1), axis=-1)[..., :-1]
    else:
        p = jax.nn.softmax(s, axis=-1)
    return jnp.einsum('bkgqn,bnkd->bqkgd', p.astype(v.dtype), v)


def _dense_latent_attention(q, k, v, kc, vc):
    B, S = q.shape[0], q.shape[1]
    nb = S // Q_BLOCK
    k_all = jnp.concatenate([kc, k], axis=1)
    v_all = jnp.concatenate([vc, v], axis=1)
    qb = jnp.moveaxis(q.reshape(B, nb, Q_BLOCK, ATT_KV_HEADS, ATT_GROUPS, HEAD_DIM), 1, 0)
    o = lax.map(lambda qi: _softmax_attend(qi, k_all, v_all), qb)
    return jnp.moveaxis(o, 0, 1).reshape(B, S, ATT_WIDTH)


def _window_latent_attention(q, k, v, kc, vc, sink):
    B, S = q.shape[0], q.shape[1]
    nb = S // Q_BLOCK
    pad = ((0, 0), (WINDOW, WINDOW), (0, 0), (0, 0))
    kp = jnp.pad(k, pad)
    vp = jnp.pad(v, pad)
    qb = jnp.moveaxis(q.reshape(B, nb, Q_BLOCK, ATT_KV_HEADS, ATT_GROUPS, HEAD_DIM), 1, 0)
    ctx_bias = jnp.zeros((Q_BLOCK, kc.shape[1]), jnp.float32)

    def block(args):
        qi, bi = args
        start = bi * Q_BLOCK
        kb = lax.dynamic_slice_in_dim(kp, start, BAND, axis=1)
        vb = lax.dynamic_slice_in_dim(vp, start, BAND, axis=1)
        kpos = start - WINDOW + jnp.arange(BAND, dtype=jnp.int32)
        qpos = start + jnp.arange(Q_BLOCK, dtype=jnp.int32)
        ok = (jnp.abs(kpos[None, :] - qpos[:, None]) <= WINDOW) & (kpos[None, :] >= 0) & (kpos[None, :] < S)
        band_bias = jnp.where(ok, jnp.float32(0.0), jnp.float32(NEG_INF))
        bias = jnp.concatenate([ctx_bias, band_bias], axis=-1)
        return _softmax_attend(qi, jnp.concatenate([kc, kb], axis=1), jnp.concatenate([vc, vb], axis=1), bias, sink)

    o = lax.map(block, (qb, jnp.arange(nb, dtype=jnp.int32)))
    return jnp.moveaxis(o, 0, 1).reshape(B, S, ATT_WIDTH)


def _short_conv(u, w):
    L = u.shape[1]
    half = CONV_WIDTH // 2
    up = jnp.pad(u, ((0, 0), (half, half), (0, 0)))
    out = w[0] * up[:, 0:L]
    for j in range(1, CONV_WIDTH):
        out = out + w[j] * up[:, j:j + L]
    return out


def _multiscale_pool(u, pool_w, pool_scale):
    B, L, _ = u.shape
    ug = u.reshape(B, L, AUX_GROUPS, AUX_GROUP_DIM)
    t = jnp.arange(L, dtype=jnp.int32)
    outs = []
    for g, w in enumerate(POOL_WINDOWS):
        ui = ug[:, :, g].astype(jnp.float32)
        cs = jnp.pad(jnp.cumsum(ui, axis=1), ((0, 0), (1, 0), (0, 0)))
        lo = jnp.clip(t - w // 2, 0, L)
        hi = jnp.clip(t - w // 2 + w, 0, L)
        mean = (cs[:, hi] - cs[:, lo]) / (hi - lo).astype(jnp.float32)[None, :, None]
        outs.append((mean - ui).astype(u.dtype))
    pooled = jnp.stack(outs, axis=2)
    mixed = jnp.einsum('blgc,gcd->blgd', pooled, pool_w).reshape(B, L, AUX_WIDTH)
    return mixed * pool_scale


def _heads(t, n_heads):
    return t.reshape(t.shape[0], t.shape[1], n_heads, HEAD_DIM)


def _ctx_kv(cn, w_in, k_g):
    kv = cn @ w_in[:, ATT_WIDTH:ATT_WIDTH + 2 * KV_WIDTH]
    kc, vc = jnp.split(kv, 2, axis=-1)
    return _rms_norm(_heads(kc, ATT_KV_HEADS), k_g), _heads(vc, ATT_KV_HEADS)


def _mixer_ab(xn, cn, p, rope, need_ctx):
    B, S, _ = xn.shape
    Lc = cn.shape[1]
    q, k, v, gb, gc, u = jnp.split(xn @ p['w_in'], _split_points(IN_WIDTHS_AB), axis=-1)
    q = _apply_axial_rope(_rms_norm(_heads(q, ATT_Q_HEADS), p['q_g']), rope)
    k = _apply_axial_rope(_rms_norm(_heads(k, ATT_KV_HEADS), p['k_g']), rope)
    v = _heads(v, ATT_KV_HEADS)
    kc, vc = _ctx_kv(cn, p['w_in'], p['k_g'])
    attn = _dense_latent_attention(q, k, v, kc, vc)
    conv = gb * _short_conv(gc * u, p['conv_w'])
    y = jnp.concatenate([attn, conv], axis=-1) @ p['w_out']
    yc = None
    if need_ctx:
        qc, _, _, gbc, gcc, uc = jnp.split(cn @ p['w_in'], _split_points(IN_WIDTHS_AB), axis=-1)
        qc = _rms_norm(_heads(qc, ATT_Q_HEADS), p['q_g']).reshape(B, Lc, ATT_KV_HEADS, ATT_GROUPS, HEAD_DIM)
        attn_c = _softmax_attend(qc, kc, vc).reshape(B, Lc, ATT_WIDTH)
        conv_c = gbc * _short_conv(gcc * uc, p['conv_w'])
        yc = jnp.concatenate([attn_c, conv_c], axis=-1) @ p['w_out']
    return y, yc


def _mixer_cd(xn, cn, p, rope, need_ctx):
    B, S, _ = xn.shape
    Lc = cn.shape[1]
    sink = p['sink'].reshape(ATT_KV_HEADS, ATT_GROUPS)
    q, k, v, u = jnp.split(xn @ p['w_in'], _split_points(IN_WIDTHS_CD), axis=-1)
    q = _apply_axial_rope(_rms_norm(_heads(q, ATT_Q_HEADS), p['q_g']), rope)
    k = _apply_axial_rope(_rms_norm(_heads(k, ATT_KV_HEADS), p['k_g']), rope)
    v = _heads(v, ATT_KV_HEADS)
    kc, vc = _ctx_kv(cn, p['w_in'], p['k_g'])
    attn = _window_latent_attention(q, k, v, kc, vc, sink)
    pool = _multiscale_pool(u, p['pool_w'], p['pool_scale'])
    y = jnp.concatenate([attn, pool], axis=-1) @ p['w_out']
    yc = None
    if need_ctx:
        qc, _, _, uc = jnp.split(cn @ p['w_in'], _split_points(IN_WIDTHS_CD), axis=-1)
        qc = _rms_norm(_heads(qc, ATT_Q_HEADS), p['q_g']).reshape(B, Lc, ATT_KV_HEADS, ATT_GROUPS, HEAD_DIM)
        attn_c = _softmax_attend(qc, kc, vc, None, sink).reshape(B, Lc, ATT_WIDTH)
        pool_c = _multiscale_pool(uc, p['pool_w'], p['pool_scale'])
        yc = jnp.concatenate([attn_c, pool_c], axis=-1) @ p['w_out']
    return y, yc


def _swiglu(h, w_gate, w_up, w_down):
    return (jax.nn.silu(h @ w_gate) * (h @ w_up)) @ w_down


def _fwd_setup_inputs(seed: int = 0) -> dict:
    key = jax.random.key(seed)
    keys = iter(jax.random.split(key, 64))
    f32 = jnp.float32

    def nrm(shape, scale):
        return jax.random.normal(next(keys), shape, f32) * scale

    def gain(n):
        return jnp.ones((n,), f32) + nrm((n,), 0.02)

    d_in_ab = sum(IN_WIDTHS_AB)
    d_in_cd = sum(IN_WIDTHS_CD)
    inp = {}
    inp['x'] = nrm((BATCH, SEQ, D_MODEL), 1.0)
    inp['c'] = nrm((BATCH, D_MODEL), 1.0)
    inp['ctx'] = nrm((BATCH, CTX_LEN, D_MODEL), 1.0)
    inp['c_ctx'] = nrm((D_MODEL,), 1.0)
    inp['l0_norm1_g'] = gain(D_MODEL)
    inp['l0_w_mod'] = nrm((D_MODEL, N_MOD * D_MODEL), 0.5 * D_MODEL ** -0.5)
    inp['l0_b_mod'] = nrm((N_MOD * D_MODEL,), 0.02)
    inp['l0_w_in'] = nrm((D_MODEL, d_in_ab), D_MODEL ** -0.5)
    inp['l0_q_norm_g'] = gain(HEAD_DIM)
    inp['l0_k_norm_g'] = gain(HEAD_DIM)
    inp['l0_conv_w'] = nrm((CONV_WIDTH, AUX_WIDTH), CONV_WIDTH ** -0.5)
    inp['l0_w_out'] = nrm((D_MODEL, D_MODEL), D_MODEL ** -0.5)
    inp['l0_norm2_g'] = gain(D_MODEL)
    inp['l0_w_gate'] = nrm((D_MODEL, FFN_HIDDEN), D_MODEL ** -0.5)
    inp['l0_w_up'] = nrm((D_MODEL, FFN_HIDDEN), D_MODEL ** -0.5)
    inp['l0_w_down'] = nrm((FFN_HIDDEN, D_MODEL), FFN_HIDDEN ** -0.5)
    inp['l1_norm1_g'] = gain(D_MODEL)
    inp['l1_w_mod'] = nrm((D_MODEL, N_MOD * D_MODEL), 0.5 * D_MODEL ** -0.5)
    inp['l1_b_mod'] = nrm((N_MOD * D_MODEL,), 0.02)
    inp['l1_w_in'] = nrm((D_MODEL, d_in_cd), D_MODEL ** -0.5)
    inp['l1_q_norm_g'] = gain(HEAD_DIM)
    inp['l1_k_norm_g'] = gain(HEAD_DIM)
    inp['l1_sink'] = nrm((ATT_Q_HEADS,), 0.5)
    inp['l1_pool_w'] = nrm((AUX_GROUPS, AUX_GROUP_DIM, AUX_GROUP_DIM), AUX_GROUP_DIM ** -0.5)
    inp['l1_pool_scale'] = gain(AUX_WIDTH)
    inp['l1_w_out'] = nrm((D_MODEL, D_MODEL), D_MODEL ** -0.5)
    inp['l1_norm2_g'] = gain(D_MODEL)
    inp['l1_w_gate'] = nrm((D_MODEL, FFN_HIDDEN), D_MODEL ** -0.5)
    inp['l1_w_up'] = nrm((D_MODEL, FFN_HIDDEN), D_MODEL ** -0.5)
    inp['l1_w_down'] = nrm((FFN_HIDDEN, D_MODEL), FFN_HIDDEN ** -0.5)
    inp['final_norm_g'] = gain(D_MODEL)
    return inp


def _fwd_reference(x, c, ctx, c_ctx,
              l0_norm1_g, l0_w_mod, l0_b_mod, l0_w_in, l0_q_norm_g, l0_k_norm_g, l0_conv_w, l0_w_out,
              l0_norm2_g, l0_w_gate, l0_w_up, l0_w_down,
              l1_norm1_g, l1_w_mod, l1_b_mod, l1_w_in, l1_q_norm_g, l1_k_norm_g, l1_sink, l1_pool_w,
              l1_pool_scale, l1_w_out, l1_norm2_g, l1_w_gate, l1_w_up, l1_w_down,
              final_norm_g):
    layers = [
        dict(norm1_g=l0_norm1_g, w_mod=l0_w_mod, b_mod=l0_b_mod, w_in=l0_w_in, q_g=l0_q_norm_g,
             k_g=l0_k_norm_g, conv_w=l0_conv_w, w_out=l0_w_out, norm2_g=l0_norm2_g,
             w_gate=l0_w_gate, w_up=l0_w_up, w_down=l0_w_down),
        dict(norm1_g=l1_norm1_g, w_mod=l1_w_mod, b_mod=l1_b_mod, w_in=l1_w_in, q_g=l1_q_norm_g,
             k_g=l1_k_norm_g, sink=l1_sink, pool_w=l1_pool_w, pool_scale=l1_pool_scale, w_out=l1_w_out,
             norm2_g=l1_norm2_g, w_gate=l1_w_gate, w_up=l1_w_up, w_down=l1_w_down),
    ]
    rope = _axial_rope(x.shape[1])
    h = x
    hc = ctx
    sc = jax.nn.silu(c)
    sc_ctx = jax.nn.silu(c_ctx)
    for i in range(DEPTH):
        p = layers[i]
        need_ctx = i < DEPTH - 1
        shift1, scale1, gate1, shift2, scale2, gate2 = jnp.split(sc @ p['w_mod'] + p['b_mod'], N_MOD, axis=-1)
        cshift1, cscale1, cgate1, cshift2, cscale2, cgate2 = jnp.split(sc_ctx @ p['w_mod'] + p['b_mod'], N_MOD, axis=-1)
        xn = _modulate(_rms_norm(h, p['norm1_g']), shift1[:, None], scale1[:, None])
        cn = _modulate(_rms_norm(hc, p['norm1_g']), cshift1, cscale1)
        if i % 2 == 0:
            y, yc = _mixer_ab(xn, cn, p, rope, need_ctx)
        else:
            y, yc = _mixer_cd(xn, cn, p, rope, need_ctx)
        h = h + gate1[:, None] * y
        hn = _modulate(_rms_norm(h, p['norm2_g']), shift2[:, None], scale2[:, None])
        h = h + gate2[:, None] * _swiglu(hn, p['w_gate'], p['w_up'], p['w_down'])
        if need_ctx:
            hc = hc + cgate1 * yc
            hcn = _modulate(_rms_norm(hc, p['norm2_g']), cshift2, cscale2)
            hc = hc + cgate2 * _swiglu(hcn, p['w_gate'], p['w_up'], p['w_down'])
    return _rms_norm(h, final_norm_g)


import jax as _jax
import jax.numpy as _jnp

TWIN_FORMAT = 'train_step'
FWD_PARAMS = ['x', 'c', 'ctx', 'c_ctx', 'l0_norm1_g', 'l0_w_mod', 'l0_b_mod', 'l0_w_in', 'l0_q_norm_g', 'l0_k_norm_g', 'l0_conv_w', 'l0_w_out', 'l0_norm2_g', 'l0_w_gate', 'l0_w_up', 'l0_w_down', 'l1_norm1_g', 'l1_w_mod', 'l1_b_mod', 'l1_w_in', 'l1_q_norm_g', 'l1_k_norm_g', 'l1_sink', 'l1_pool_w', 'l1_pool_scale', 'l1_w_out', 'l1_norm2_g', 'l1_w_gate', 'l1_w_up', 'l1_w_down', 'final_norm_g']
TWIN_WEIGHTS = ['c_ctx', 'l0_norm1_g', 'l0_w_mod', 'l0_b_mod', 'l0_w_in', 'l0_q_norm_g', 'l0_k_norm_g', 'l0_conv_w', 'l0_w_out', 'l0_norm2_g', 'l0_w_gate', 'l0_w_up', 'l0_w_down', 'l1_norm1_g', 'l1_w_mod', 'l1_b_mod', 'l1_w_in', 'l1_q_norm_g', 'l1_k_norm_g', 'l1_sink', 'l1_pool_w', 'l1_pool_scale', 'l1_w_out', 'l1_norm2_g', 'l1_w_gate', 'l1_w_up', 'l1_w_down', 'final_norm_g']
TWIN_DIFF_INPUT = 'x'
TWIN_INPUTS = ['x', 'c', 'ctx', 'c_ctx', 'l0_norm1_g', 'l0_w_mod', 'l0_b_mod', 'l0_w_in', 'l0_q_norm_g', 'l0_k_norm_g', 'l0_conv_w', 'l0_w_out', 'l0_norm2_g', 'l0_w_gate', 'l0_w_up', 'l0_w_down', 'l1_norm1_g', 'l1_w_mod', 'l1_b_mod', 'l1_w_in', 'l1_q_norm_g', 'l1_k_norm_g', 'l1_sink', 'l1_pool_w', 'l1_pool_scale', 'l1_w_out', 'l1_norm2_g', 'l1_w_gate', 'l1_w_up', 'l1_w_down', 'final_norm_g', 'loss_target', 'm_c_ctx', 'm_l0_norm1_g', 'm_l0_w_mod', 'm_l0_b_mod', 'm_l0_w_in', 'm_l0_q_norm_g', 'm_l0_k_norm_g', 'm_l0_conv_w', 'm_l0_w_out', 'm_l0_norm2_g', 'm_l0_w_gate', 'm_l0_w_up', 'm_l0_w_down', 'm_l1_norm1_g', 'm_l1_w_mod', 'm_l1_b_mod', 'm_l1_w_in', 'm_l1_q_norm_g', 'm_l1_k_norm_g', 'm_l1_sink', 'm_l1_pool_w', 'm_l1_pool_scale', 'm_l1_w_out', 'm_l1_norm2_g', 'm_l1_w_gate', 'm_l1_w_up', 'm_l1_w_down', 'm_final_norm_g', 'v_c_ctx', 'v_l0_norm1_g', 'v_l0_w_mod', 'v_l0_b_mod', 'v_l0_w_in', 'v_l0_q_norm_g', 'v_l0_k_norm_g', 'v_l0_conv_w', 'v_l0_w_out', 'v_l0_norm2_g', 'v_l0_w_gate', 'v_l0_w_up', 'v_l0_w_down', 'v_l1_norm1_g', 'v_l1_w_mod', 'v_l1_b_mod', 'v_l1_w_in', 'v_l1_q_norm_g', 'v_l1_k_norm_g', 'v_l1_sink', 'v_l1_pool_w', 'v_l1_pool_scale', 'v_l1_w_out', 'v_l1_norm2_g', 'v_l1_w_gate', 'v_l1_w_up', 'v_l1_w_down', 'v_final_norm_g']
TWIN_OUTPUTS = ['loss', 'grad_x', 'grad_c_ctx', 'grad_l0_norm1_g', 'grad_l0_w_mod', 'grad_l0_b_mod', 'grad_l0_w_in', 'grad_l0_q_norm_g', 'grad_l0_k_norm_g', 'grad_l0_conv_w', 'grad_l0_w_out', 'grad_l0_norm2_g', 'grad_l0_w_gate', 'grad_l0_w_up', 'grad_l0_w_down', 'grad_l1_norm1_g', 'grad_l1_w_mod', 'grad_l1_b_mod', 'grad_l1_w_in', 'grad_l1_q_norm_g', 'grad_l1_k_norm_g', 'grad_l1_sink', 'grad_l1_pool_w', 'grad_l1_pool_scale', 'grad_l1_w_out', 'grad_l1_norm2_g', 'grad_l1_w_gate', 'grad_l1_w_up', 'grad_l1_w_down', 'grad_final_norm_g', 'delta_c_ctx', 'delta_l0_norm1_g', 'delta_l0_w_mod', 'delta_l0_b_mod', 'delta_l0_w_in', 'delta_l0_q_norm_g', 'delta_l0_k_norm_g', 'delta_l0_conv_w', 'delta_l0_w_out', 'delta_l0_norm2_g', 'delta_l0_w_gate', 'delta_l0_w_up', 'delta_l0_w_down', 'delta_l1_norm1_g', 'delta_l1_w_mod', 'delta_l1_b_mod', 'delta_l1_w_in', 'delta_l1_q_norm_g', 'delta_l1_k_norm_g', 'delta_l1_sink', 'delta_l1_pool_w', 'delta_l1_pool_scale', 'delta_l1_w_out', 'delta_l1_norm2_g', 'delta_l1_w_gate', 'delta_l1_w_up', 'delta_l1_w_down', 'delta_final_norm_g', 'new_m_c_ctx', 'new_m_l0_norm1_g', 'new_m_l0_w_mod', 'new_m_l0_b_mod', 'new_m_l0_w_in', 'new_m_l0_q_norm_g', 'new_m_l0_k_norm_g', 'new_m_l0_conv_w', 'new_m_l0_w_out', 'new_m_l0_norm2_g', 'new_m_l0_w_gate', 'new_m_l0_w_up', 'new_m_l0_w_down', 'new_m_l1_norm1_g', 'new_m_l1_w_mod', 'new_m_l1_b_mod', 'new_m_l1_w_in', 'new_m_l1_q_norm_g', 'new_m_l1_k_norm_g', 'new_m_l1_sink', 'new_m_l1_pool_w', 'new_m_l1_pool_scale', 'new_m_l1_w_out', 'new_m_l1_norm2_g', 'new_m_l1_w_gate', 'new_m_l1_w_up', 'new_m_l1_w_down', 'new_m_final_norm_g', 'new_v_c_ctx', 'new_v_l0_norm1_g', 'new_v_l0_w_mod', 'new_v_l0_b_mod', 'new_v_l0_w_in', 'new_v_l0_q_norm_g', 'new_v_l0_k_norm_g', 'new_v_l0_conv_w', 'new_v_l0_w_out', 'new_v_l0_norm2_g', 'new_v_l0_w_gate', 'new_v_l0_w_up', 'new_v_l0_w_down', 'new_v_l1_norm1_g', 'new_v_l1_w_mod', 'new_v_l1_b_mod', 'new_v_l1_w_in', 'new_v_l1_q_norm_g', 'new_v_l1_k_norm_g', 'new_v_l1_sink', 'new_v_l1_pool_w', 'new_v_l1_pool_scale', 'new_v_l1_w_out', 'new_v_l1_norm2_g', 'new_v_l1_w_gate', 'new_v_l1_w_up', 'new_v_l1_w_down', 'new_v_final_norm_g']
TWIN_LEAF_KINDS = {'loss': 'loss', 'grad_x': 'grad_x', 'grad_c_ctx': 'grad_w', 'grad_l0_norm1_g': 'grad_w', 'grad_l0_w_mod': 'grad_w', 'grad_l0_b_mod': 'grad_w', 'grad_l0_w_in': 'grad_w', 'grad_l0_q_norm_g': 'grad_w', 'grad_l0_k_norm_g': 'grad_w', 'grad_l0_conv_w': 'grad_w', 'grad_l0_w_out': 'grad_w', 'grad_l0_norm2_g': 'grad_w', 'grad_l0_w_gate': 'grad_w', 'grad_l0_w_up': 'grad_w', 'grad_l0_w_down': 'grad_w', 'grad_l1_norm1_g': 'grad_w', 'grad_l1_w_mod': 'grad_w', 'grad_l1_b_mod': 'grad_w', 'grad_l1_w_in': 'grad_w', 'grad_l1_q_norm_g': 'grad_w', 'grad_l1_k_norm_g': 'grad_w', 'grad_l1_sink': 'grad_w', 'grad_l1_pool_w': 'grad_w', 'grad_l1_pool_scale': 'grad_w', 'grad_l1_w_out': 'grad_w', 'grad_l1_norm2_g': 'grad_w', 'grad_l1_w_gate': 'grad_w', 'grad_l1_w_up': 'grad_w', 'grad_l1_w_down': 'grad_w', 'grad_final_norm_g': 'grad_w', 'delta_c_ctx': 'delta_w', 'delta_l0_norm1_g': 'delta_w', 'delta_l0_w_mod': 'delta_w', 'delta_l0_b_mod': 'delta_w', 'delta_l0_w_in': 'delta_w', 'delta_l0_q_norm_g': 'delta_w', 'delta_l0_k_norm_g': 'delta_w', 'delta_l0_conv_w': 'delta_w', 'delta_l0_w_out': 'delta_w', 'delta_l0_norm2_g': 'delta_w', 'delta_l0_w_gate': 'delta_w', 'delta_l0_w_up': 'delta_w', 'delta_l0_w_down': 'delta_w', 'delta_l1_norm1_g': 'delta_w', 'delta_l1_w_mod': 'delta_w', 'delta_l1_b_mod': 'delta_w', 'delta_l1_w_in': 'delta_w', 'delta_l1_q_norm_g': 'delta_w', 'delta_l1_k_norm_g': 'delta_w', 'delta_l1_sink': 'delta_w', 'delta_l1_pool_w': 'delta_w', 'delta_l1_pool_scale': 'delta_w', 'delta_l1_w_out': 'delta_w', 'delta_l1_norm2_g': 'delta_w', 'delta_l1_w_gate': 'delta_w', 'delta_l1_w_up': 'delta_w', 'delta_l1_w_down': 'delta_w', 'delta_final_norm_g': 'delta_w', 'new_m_c_ctx': 'new_m', 'new_m_l0_norm1_g': 'new_m', 'new_m_l0_w_mod': 'new_m', 'new_m_l0_b_mod': 'new_m', 'new_m_l0_w_in': 'new_m', 'new_m_l0_q_norm_g': 'new_m', 'new_m_l0_k_norm_g': 'new_m', 'new_m_l0_conv_w': 'new_m', 'new_m_l0_w_out': 'new_m', 'new_m_l0_norm2_g': 'new_m', 'new_m_l0_w_gate': 'new_m', 'new_m_l0_w_up': 'new_m', 'new_m_l0_w_down': 'new_m', 'new_m_l1_norm1_g': 'new_m', 'new_m_l1_w_mod': 'new_m', 'new_m_l1_b_mod': 'new_m', 'new_m_l1_w_in': 'new_m', 'new_m_l1_q_norm_g': 'new_m', 'new_m_l1_k_norm_g': 'new_m', 'new_m_l1_sink': 'new_m', 'new_m_l1_pool_w': 'new_m', 'new_m_l1_pool_scale': 'new_m', 'new_m_l1_w_out': 'new_m', 'new_m_l1_norm2_g': 'new_m', 'new_m_l1_w_gate': 'new_m', 'new_m_l1_w_up': 'new_m', 'new_m_l1_w_down': 'new_m', 'new_m_final_norm_g': 'new_m', 'new_v_c_ctx': 'new_v', 'new_v_l0_norm1_g': 'new_v', 'new_v_l0_w_mod': 'new_v', 'new_v_l0_b_mod': 'new_v', 'new_v_l0_w_in': 'new_v', 'new_v_l0_q_norm_g': 'new_v', 'new_v_l0_k_norm_g': 'new_v', 'new_v_l0_conv_w': 'new_v', 'new_v_l0_w_out': 'new_v', 'new_v_l0_norm2_g': 'new_v', 'new_v_l0_w_gate': 'new_v', 'new_v_l0_w_up': 'new_v', 'new_v_l0_w_down': 'new_v', 'new_v_l1_norm1_g': 'new_v', 'new_v_l1_w_mod': 'new_v', 'new_v_l1_b_mod': 'new_v', 'new_v_l1_w_in': 'new_v', 'new_v_l1_q_norm_g': 'new_v', 'new_v_l1_k_norm_g': 'new_v', 'new_v_l1_sink': 'new_v', 'new_v_l1_pool_w': 'new_v', 'new_v_l1_pool_scale': 'new_v', 'new_v_l1_w_out': 'new_v', 'new_v_l1_norm2_g': 'new_v', 'new_v_l1_w_gate': 'new_v', 'new_v_l1_w_up': 'new_v', 'new_v_l1_w_down': 'new_v', 'new_v_final_norm_g': 'new_v'}


def _forward(args):
    return _fwd_reference(*[args[k] for k in FWD_PARAMS])


def _output_shape():
    def fwd():
        inp = _fwd_setup_inputs(0)
        return _fwd_reference(*[inp[k] for k in FWD_PARAMS])
    out = _jax.eval_shape(fwd)
    return out.shape, out.dtype

N_MICROBATCH = 1
ADAM_LR = 0.001
ADAM_B1 = 0.9
ADAM_B2 = 0.999
ADAM_EPS = 1e-08
ADAM_WD = 0.01
ADAM_STEP = 10
PER_EXAMPLE_BATCH_AXIS = {'x': 0, 'c': 0, 'ctx': 0, 'loss_target': 0}
SHARED_INPUTS = []
_WEIGHT_DTYPES = {'c_ctx': _jnp.float32, 'l0_norm1_g': _jnp.float32, 'l0_w_mod': _jnp.float32, 'l0_b_mod': _jnp.float32, 'l0_w_in': _jnp.float32, 'l0_q_norm_g': _jnp.float32, 'l0_k_norm_g': _jnp.float32, 'l0_conv_w': _jnp.float32, 'l0_w_out': _jnp.float32, 'l0_norm2_g': _jnp.float32, 'l0_w_gate': _jnp.float32, 'l0_w_up': _jnp.float32, 'l0_w_down': _jnp.float32, 'l1_norm1_g': _jnp.float32, 'l1_w_mod': _jnp.float32, 'l1_b_mod': _jnp.float32, 'l1_w_in': _jnp.float32, 'l1_q_norm_g': _jnp.float32, 'l1_k_norm_g': _jnp.float32, 'l1_sink': _jnp.float32, 'l1_pool_w': _jnp.float32, 'l1_pool_scale': _jnp.float32, 'l1_w_out': _jnp.float32, 'l1_norm2_g': _jnp.float32, 'l1_w_gate': _jnp.float32, 'l1_w_up': _jnp.float32, 'l1_w_down': _jnp.float32, 'final_norm_g': _jnp.float32}
MOMENT_SCALE = {'c_ctx': 5.348432e-03, 'l0_norm1_g': 3.129712e-02, 'l0_w_mod': 2.627438e-02, 'l0_b_mod': 4.570591e-02, 'l0_w_in': 2.170746e-02, 'l0_q_norm_g': 7.891910e-03, 'l0_k_norm_g': 7.860764e-03, 'l0_conv_w': 3.538245e-02, 'l0_w_out': 1.853201e-02, 'l0_norm2_g': 2.517447e-02, 'l0_w_gate': 1.139747e-02, 'l0_w_up': 1.104385e-02, 'l0_w_down': 1.828839e-02, 'l1_norm1_g': 1.213812e-02, 'l1_w_mod': 2.094020e-02, 'l1_b_mod': 3.670639e-02, 'l1_w_in': 1.082712e-02, 'l1_q_norm_g': 7.343921e-03, 'l1_k_norm_g': 7.496136e-03, 'l1_sink': 2.325506e-04, 'l1_pool_w': 2.384257e-02, 'l1_pool_scale': 2.575515e-02, 'l1_w_out': 1.308486e-02, 'l1_norm2_g': 2.484072e-02, 'l1_w_gate': 1.110453e-02, 'l1_w_up': 1.073488e-02, 'l1_w_down': 1.780627e-02, 'final_norm_g': 1.599694e+01}


def _to_microbatches(a, axis):
    t = _jnp.moveaxis(a, axis, 0)
    t = t.reshape((N_MICROBATCH, t.shape[0] // N_MICROBATCH) + t.shape[1:])
    return _jnp.moveaxis(t, 1, axis + 1)


def setup_inputs(seed: int = 0) -> dict:
    inp = _fwd_setup_inputs(seed)
    key = _jax.random.fold_in(_jax.random.key(seed), 7919)
    shape, _ = _output_shape()
    out = dict(inp)
    out["loss_target"] = _jax.random.normal(_jax.random.fold_in(key, 0), shape, _jnp.float32)
    for i, name in enumerate(TWIN_WEIGHTS):
        w = inp[name].astype(_jnp.float32)
        if MOMENT_SCALE is None:
            s = _jnp.sqrt(_jnp.mean(_jnp.square(w)) + 1e-30)
        else:
            s = MOMENT_SCALE[name]
        km, kv = _jax.random.split(_jax.random.fold_in(key, i + 1))
        out[name] = w
        out["m_" + name] = s * _jax.random.normal(km, w.shape, _jnp.float32)
        out["v_" + name] = (s * s) * _jax.random.uniform(kv, w.shape, _jnp.float32, 0.5, 1.5)
    if N_MICROBATCH > 1:
        for name, axis in PER_EXAMPLE_BATCH_AXIS.items():
            out[name] = _to_microbatches(out[name], axis)
    return {'x': out['x'], 'c': out['c'], 'ctx': out['ctx'], 'c_ctx': out['c_ctx'], 'l0_norm1_g': out['l0_norm1_g'], 'l0_w_mod': out['l0_w_mod'], 'l0_b_mod': out['l0_b_mod'], 'l0_w_in': out['l0_w_in'], 'l0_q_norm_g': out['l0_q_norm_g'], 'l0_k_norm_g': out['l0_k_norm_g'], 'l0_conv_w': out['l0_conv_w'], 'l0_w_out': out['l0_w_out'], 'l0_norm2_g': out['l0_norm2_g'], 'l0_w_gate': out['l0_w_gate'], 'l0_w_up': out['l0_w_up'], 'l0_w_down': out['l0_w_down'], 'l1_norm1_g': out['l1_norm1_g'], 'l1_w_mod': out['l1_w_mod'], 'l1_b_mod': out['l1_b_mod'], 'l1_w_in': out['l1_w_in'], 'l1_q_norm_g': out['l1_q_norm_g'], 'l1_k_norm_g': out['l1_k_norm_g'], 'l1_sink': out['l1_sink'], 'l1_pool_w': out['l1_pool_w'], 'l1_pool_scale': out['l1_pool_scale'], 'l1_w_out': out['l1_w_out'], 'l1_norm2_g': out['l1_norm2_g'], 'l1_w_gate': out['l1_w_gate'], 'l1_w_up': out['l1_w_up'], 'l1_w_down': out['l1_w_down'], 'final_norm_g': out['final_norm_g'], 'loss_target': out['loss_target'], 'm_c_ctx': out['m_c_ctx'], 'm_l0_norm1_g': out['m_l0_norm1_g'], 'm_l0_w_mod': out['m_l0_w_mod'], 'm_l0_b_mod': out['m_l0_b_mod'], 'm_l0_w_in': out['m_l0_w_in'], 'm_l0_q_norm_g': out['m_l0_q_norm_g'], 'm_l0_k_norm_g': out['m_l0_k_norm_g'], 'm_l0_conv_w': out['m_l0_conv_w'], 'm_l0_w_out': out['m_l0_w_out'], 'm_l0_norm2_g': out['m_l0_norm2_g'], 'm_l0_w_gate': out['m_l0_w_gate'], 'm_l0_w_up': out['m_l0_w_up'], 'm_l0_w_down': out['m_l0_w_down'], 'm_l1_norm1_g': out['m_l1_norm1_g'], 'm_l1_w_mod': out['m_l1_w_mod'], 'm_l1_b_mod': out['m_l1_b_mod'], 'm_l1_w_in': out['m_l1_w_in'], 'm_l1_q_norm_g': out['m_l1_q_norm_g'], 'm_l1_k_norm_g': out['m_l1_k_norm_g'], 'm_l1_sink': out['m_l1_sink'], 'm_l1_pool_w': out['m_l1_pool_w'], 'm_l1_pool_scale': out['m_l1_pool_scale'], 'm_l1_w_out': out['m_l1_w_out'], 'm_l1_norm2_g': out['m_l1_norm2_g'], 'm_l1_w_gate': out['m_l1_w_gate'], 'm_l1_w_up': out['m_l1_w_up'], 'm_l1_w_down': out['m_l1_w_down'], 'm_final_norm_g': out['m_final_norm_g'], 'v_c_ctx': out['v_c_ctx'], 'v_l0_norm1_g': out['v_l0_norm1_g'], 'v_l0_w_mod': out['v_l0_w_mod'], 'v_l0_b_mod': out['v_l0_b_mod'], 'v_l0_w_in': out['v_l0_w_in'], 'v_l0_q_norm_g': out['v_l0_q_norm_g'], 'v_l0_k_norm_g': out['v_l0_k_norm_g'], 'v_l0_conv_w': out['v_l0_conv_w'], 'v_l0_w_out': out['v_l0_w_out'], 'v_l0_norm2_g': out['v_l0_norm2_g'], 'v_l0_w_gate': out['v_l0_w_gate'], 'v_l0_w_up': out['v_l0_w_up'], 'v_l0_w_down': out['v_l0_w_down'], 'v_l1_norm1_g': out['v_l1_norm1_g'], 'v_l1_w_mod': out['v_l1_w_mod'], 'v_l1_b_mod': out['v_l1_b_mod'], 'v_l1_w_in': out['v_l1_w_in'], 'v_l1_q_norm_g': out['v_l1_q_norm_g'], 'v_l1_k_norm_g': out['v_l1_k_norm_g'], 'v_l1_sink': out['v_l1_sink'], 'v_l1_pool_w': out['v_l1_pool_w'], 'v_l1_pool_scale': out['v_l1_pool_scale'], 'v_l1_w_out': out['v_l1_w_out'], 'v_l1_norm2_g': out['v_l1_norm2_g'], 'v_l1_w_gate': out['v_l1_w_gate'], 'v_l1_w_up': out['v_l1_w_up'], 'v_l1_w_down': out['v_l1_w_down'], 'v_final_norm_g': out['v_final_norm_g']}


def _loss(weights, diff, rest, loss_target):
    with _jax.named_scope("forward"):
        args = {**rest, TWIN_DIFF_INPUT: diff, **{k: w.astype(_WEIGHT_DTYPES[k]) for k, w in weights.items()}}
        y = _forward(args)
    with _jax.named_scope("loss_head"):
        err = _jnp.square(y.astype(_jnp.float32) - loss_target)
        return 0.5 * _jnp.sum(_jnp.mean(err, axis=-1)) if err.ndim else 0.5 * err


def _adamw(w, g, m, v):
    m = ADAM_B1 * m + (1.0 - ADAM_B1) * g
    v = ADAM_B2 * v + (1.0 - ADAM_B2) * _jnp.square(g)
    m_hat = m / (1.0 - ADAM_B1 ** ADAM_STEP)
    v_hat = v / (1.0 - ADAM_B2 ** ADAM_STEP)
    delta = -ADAM_LR * (m_hat / (_jnp.sqrt(v_hat) + ADAM_EPS) + ADAM_WD * w)
    return delta, m, v


def reference(x, c, ctx, c_ctx, l0_norm1_g, l0_w_mod, l0_b_mod, l0_w_in, l0_q_norm_g, l0_k_norm_g, l0_conv_w, l0_w_out, l0_norm2_g, l0_w_gate, l0_w_up, l0_w_down, l1_norm1_g, l1_w_mod, l1_b_mod, l1_w_in, l1_q_norm_g, l1_k_norm_g, l1_sink, l1_pool_w, l1_pool_scale, l1_w_out, l1_norm2_g, l1_w_gate, l1_w_up, l1_w_down, final_norm_g, loss_target, m_c_ctx, m_l0_norm1_g, m_l0_w_mod, m_l0_b_mod, m_l0_w_in, m_l0_q_norm_g, m_l0_k_norm_g, m_l0_conv_w, m_l0_w_out, m_l0_norm2_g, m_l0_w_gate, m_l0_w_up, m_l0_w_down, m_l1_norm1_g, m_l1_w_mod, m_l1_b_mod, m_l1_w_in, m_l1_q_norm_g, m_l1_k_norm_g, m_l1_sink, m_l1_pool_w, m_l1_pool_scale, m_l1_w_out, m_l1_norm2_g, m_l1_w_gate, m_l1_w_up, m_l1_w_down, m_final_norm_g, v_c_ctx, v_l0_norm1_g, v_l0_w_mod, v_l0_b_mod, v_l0_w_in, v_l0_q_norm_g, v_l0_k_norm_g, v_l0_conv_w, v_l0_w_out, v_l0_norm2_g, v_l0_w_gate, v_l0_w_up, v_l0_w_down, v_l1_norm1_g, v_l1_w_mod, v_l1_b_mod, v_l1_w_in, v_l1_q_norm_g, v_l1_k_norm_g, v_l1_sink, v_l1_pool_w, v_l1_pool_scale, v_l1_w_out, v_l1_norm2_g, v_l1_w_gate, v_l1_w_up, v_l1_w_down, v_final_norm_g):
    given = dict(x=x, c=c, ctx=ctx, c_ctx=c_ctx, l0_norm1_g=l0_norm1_g, l0_w_mod=l0_w_mod, l0_b_mod=l0_b_mod, l0_w_in=l0_w_in, l0_q_norm_g=l0_q_norm_g, l0_k_norm_g=l0_k_norm_g, l0_conv_w=l0_conv_w, l0_w_out=l0_w_out, l0_norm2_g=l0_norm2_g, l0_w_gate=l0_w_gate, l0_w_up=l0_w_up, l0_w_down=l0_w_down, l1_norm1_g=l1_norm1_g, l1_w_mod=l1_w_mod, l1_b_mod=l1_b_mod, l1_w_in=l1_w_in, l1_q_norm_g=l1_q_norm_g, l1_k_norm_g=l1_k_norm_g, l1_sink=l1_sink, l1_pool_w=l1_pool_w, l1_pool_scale=l1_pool_scale, l1_w_out=l1_w_out, l1_norm2_g=l1_norm2_g, l1_w_gate=l1_w_gate, l1_w_up=l1_w_up, l1_w_down=l1_w_down, final_norm_g=final_norm_g, loss_target=loss_target, m_c_ctx=m_c_ctx, m_l0_norm1_g=m_l0_norm1_g, m_l0_w_mod=m_l0_w_mod, m_l0_b_mod=m_l0_b_mod, m_l0_w_in=m_l0_w_in, m_l0_q_norm_g=m_l0_q_norm_g, m_l0_k_norm_g=m_l0_k_norm_g, m_l0_conv_w=m_l0_conv_w, m_l0_w_out=m_l0_w_out, m_l0_norm2_g=m_l0_norm2_g, m_l0_w_gate=m_l0_w_gate, m_l0_w_up=m_l0_w_up, m_l0_w_down=m_l0_w_down, m_l1_norm1_g=m_l1_norm1_g, m_l1_w_mod=m_l1_w_mod, m_l1_b_mod=m_l1_b_mod, m_l1_w_in=m_l1_w_in, m_l1_q_norm_g=m_l1_q_norm_g, m_l1_k_norm_g=m_l1_k_norm_g, m_l1_sink=m_l1_sink, m_l1_pool_w=m_l1_pool_w, m_l1_pool_scale=m_l1_pool_scale, m_l1_w_out=m_l1_w_out, m_l1_norm2_g=m_l1_norm2_g, m_l1_w_gate=m_l1_w_gate, m_l1_w_up=m_l1_w_up, m_l1_w_down=m_l1_w_down, m_final_norm_g=m_final_norm_g, v_c_ctx=v_c_ctx, v_l0_norm1_g=v_l0_norm1_g, v_l0_w_mod=v_l0_w_mod, v_l0_b_mod=v_l0_b_mod, v_l0_w_in=v_l0_w_in, v_l0_q_norm_g=v_l0_q_norm_g, v_l0_k_norm_g=v_l0_k_norm_g, v_l0_conv_w=v_l0_conv_w, v_l0_w_out=v_l0_w_out, v_l0_norm2_g=v_l0_norm2_g, v_l0_w_gate=v_l0_w_gate, v_l0_w_up=v_l0_w_up, v_l0_w_down=v_l0_w_down, v_l1_norm1_g=v_l1_norm1_g, v_l1_w_mod=v_l1_w_mod, v_l1_b_mod=v_l1_b_mod, v_l1_w_in=v_l1_w_in, v_l1_q_norm_g=v_l1_q_norm_g, v_l1_k_norm_g=v_l1_k_norm_g, v_l1_sink=v_l1_sink, v_l1_pool_w=v_l1_pool_w, v_l1_pool_scale=v_l1_pool_scale, v_l1_w_out=v_l1_w_out, v_l1_norm2_g=v_l1_norm2_g, v_l1_w_gate=v_l1_w_gate, v_l1_w_up=v_l1_w_up, v_l1_w_down=v_l1_w_down, v_final_norm_g=v_final_norm_g)
    weights = {n: given[n] for n in TWIN_WEIGHTS}
    shared = {n: given[n] for n in SHARED_INPUTS}
    per_example = {n: given[n] for n in ['x', 'c', 'ctx']}
    grad_fn = _jax.value_and_grad(_loss, argnums=(0, 1))

    def one_microbatch(ex, loss_target):
        ex = dict(ex)
        diff = ex.pop(TWIN_DIFF_INPUT)
        return grad_fn(weights, diff, {**shared, **ex}, loss_target)

    if N_MICROBATCH == 1:
        loss, (grad_w, grad_x) = one_microbatch(per_example, given["loss_target"])
    else:
        def body(carry, xs):
            loss_sum, grad_sum = carry
            l_k, (gw_k, gx_k) = one_microbatch(xs[0], xs[1])
            with _jax.named_scope("update"):
                return (loss_sum + l_k, _jax.tree.map(_jnp.add, grad_sum, gw_k)), gx_k

        init = (_jnp.zeros((), _jnp.float32), _jax.tree.map(_jnp.zeros_like, weights))
        (loss, grad_w), grad_x = _jax.lax.scan(body, init, (per_example, given["loss_target"]))
    with _jax.named_scope("update"):
        delta_w, new_m, new_v = {}, {}, {}
        for n in TWIN_WEIGHTS:
            delta_w[n], new_m[n], new_v[n] = _adamw(weights[n], grad_w[n], given["m_" + n], given["v_" + n])
    return (loss, grad_x, *[grad_w[n] for n in TWIN_WEIGHTS], *[delta_w[n] for n in TWIN_WEIGHTS],
            *[new_m[n] for n in TWIN_WEIGHTS], *[new_v[n] for n in TWIN_WEIGHTS])
```

```python
import numpy as np
import jax
import jax.numpy as jnp
from jax import lax
from jax.experimental import pallas as pl
from jax.experimental.pallas import tpu as pltpu

F32 = jnp.float32
BF16 = jnp.bfloat16
MESH = pl.DeviceIdType.MESH

HEAD_DIM = 128
Q_HEADS = 12
KV_HEADS = 4
GROUPS = Q_HEADS // KV_HEADS
ATT_WIDTH = Q_HEADS * HEAD_DIM
KV_WIDTH = KV_HEADS * HEAD_DIM
QKV_WIDTH = ATT_WIDTH + 2 * KV_WIDTH
AUX_WIDTH = 512
AUX_GROUPS = 4
POOL_WINDOWS = (2, 4, 8, 16)
WINDOW = 128
GRID_W = 64
ROPE_THETA = 10000.0
N_MOD = 6
EPS = 1e-6
NEG_INF = -1e30
ATT_SCALE = HEAD_DIM ** -0.5
N_SHARD = 4
N_DEV = 8

ADAM_LR = 0.001
ADAM_B1 = 0.9
ADAM_B2 = 0.999
ADAM_EPS = 1e-08
ADAM_WD = 0.01
ADAM_STEP = 10

TM = 256
BAND = TM + 2 * WINDOW
VMEM_LIMIT = 56 * 1024 * 1024

NT_DIMS = (((1,), (1,)), ((), ()))
TN_DIMS = (((0,), (0,)), ((), ()))


def _pcall(body, **kw):
    return pl.pallas_call(body, **kw)


def _params(*sem):
    return pltpu.CompilerParams(dimension_semantics=sem, vmem_limit_bytes=VMEM_LIMIT)


def _sel(i):
    return jnp.minimum(i, 1)


def _sel_spec(d):
    return pl.BlockSpec((None, 1, d), lambda i: (_sel(i), 0, 0))


def _dot(a, b):
    return jnp.dot(a, b, preferred_element_type=F32)


def _dot_nt(a, b):
    return lax.dot_general(a, b, NT_DIMS, preferred_element_type=F32)


def _dot_tn(a, b):
    return lax.dot_general(a, b, TN_DIMS, preferred_element_type=F32)


def _sigmoid(x):
    return 1.0 / (1.0 + jnp.exp(-x))


def norm_mod_fwd(h, g, shift, scale, name):
    t, d = h.shape

    def body(h_ref, g_ref, sh_ref, sc_ref, o_ref):
        x = h_ref[...]
        r = lax.rsqrt(jnp.mean(x * x, axis=-1, keepdims=True) + EPS)
        y = x * r * g_ref[...]
        o_ref[...] = (y * (1.0 + sc_ref[...]) + sh_ref[...]).astype(BF16)

    return _pcall(
        body, name=name, grid=(t // TM,),
        in_specs=[pl.BlockSpec((TM, d), lambda i: (i, 0)), pl.BlockSpec((1, d), lambda i: (0, 0)),
                  _sel_spec(d), _sel_spec(d)],
        out_specs=pl.BlockSpec((TM, d), lambda i: (i, 0)),
        out_shape=jax.ShapeDtypeStruct((t, d), BF16),
        compiler_params=_params("parallel"),
    )(h, g, shift, scale)


def norm_mod_bwd(dy, h, g, scale, dres, name):
    t, d = h.shape

    def body(dy_ref, h_ref, g_ref, sc_ref, dres_ref, dh_ref, dsh_ref, dsc_ref, dg_ref):
        i = pl.program_id(0)

        @pl.when(i == 0)
        def _():
            dsh_ref[...] = jnp.zeros_like(dsh_ref)
            dsc_ref[...] = jnp.zeros_like(dsc_ref)
            dg_ref[...] = jnp.zeros_like(dg_ref)

        x = h_ref[...]
        gv = g_ref[...]
        r = lax.rsqrt(jnp.mean(x * x, axis=-1, keepdims=True) + EPS)
        xhat = x * r
        dyv = dy_ref[...]
        s = _sel(i)
        dsh_ref[pl.ds(s, 1)] += jnp.sum(dyv, axis=0, keepdims=True)[None]
        dsc_ref[pl.ds(s, 1)] += jnp.sum(dyv * xhat * gv, axis=0, keepdims=True)[None]
        tt = dyv * (1.0 + sc_ref[...])
        dg_ref[...] += jnp.sum(tt * xhat, axis=0, keepdims=True)
        dxhat = tt * gv
        dx = r * (dxhat - xhat * jnp.mean(dxhat * xhat, axis=-1, keepdims=True))
        dh_ref[...] = dx + dres_ref[...]

    row = pl.BlockSpec((TM, d), lambda i: (i, 0))
    acc2 = pl.BlockSpec((2, 1, d), lambda i: (0, 0, 0))
    return _pcall(
        body, name=name, grid=(t // TM,),
        in_specs=[row, row, pl.BlockSpec((1, d), lambda i: (0, 0)), _sel_spec(d), row],
        out_specs=[row, acc2, acc2, pl.BlockSpec((1, d), lambda i: (0, 0))],
        out_shape=[jax.ShapeDtypeStruct((t, d), F32), jax.ShapeDtypeStruct((2, 1, d), F32),
                   jax.ShapeDtypeStruct((2, 1, d), F32), jax.ShapeDtypeStruct((1, d), F32)],
        compiler_params=_params("arbitrary"),
    )(dy, h, g, scale, dres)


def gate_bwd(dh, f, gate, name):
    t, d = dh.shape

    def body(dh_ref, f_ref, gt_ref, df_ref, dgt_ref):
        i = pl.program_id(0)

        @pl.when(i == 0)
        def _():
            dgt_ref[...] = jnp.zeros_like(dgt_ref)

        dhv = dh_ref[...]
        df_ref[...] = (dhv * gt_ref[...]).astype(BF16)
        dgt_ref[pl.ds(_sel(i), 1)] += jnp.sum(dhv * f_ref[...], axis=0, keepdims=True)[None]

    row = pl.BlockSpec((TM, d), lambda i: (i, 0))
    return _pcall(
        body, name=name, grid=(t // TM,),
        in_specs=[row, row, _sel_spec(d)],
        out_specs=[row, pl.BlockSpec((2, 1, d), lambda i: (0, 0, 0))],
        out_shape=[jax.ShapeDtypeStruct((t, d), BF16), jax.ShapeDtypeStruct((2, 1, d), F32)],
        compiler_params=_params("arbitrary"),
    )(dh, f, gate)


def final_loss(h, fg, target):
    t, d = h.shape

    def body(h_ref, g_ref, tg_ref, loss_ref, dh_ref, dg_ref):
        i = pl.program_id(0)

        @pl.when(i == 0)
        def _():
            loss_ref[...] = jnp.zeros_like(loss_ref)
            dg_ref[...] = jnp.zeros_like(dg_ref)
            dh_ref[...] = jnp.zeros_like(dh_ref)

        @pl.when(i > 0)
        def _():
            x = h_ref[...]
            gv = g_ref[...]
            r = lax.rsqrt(jnp.mean(x * x, axis=-1, keepdims=True) + EPS)
            xhat = x * r
            diff = xhat * gv - tg_ref[...]
            loss_ref[...] += 0.5 * jnp.sum(jnp.mean(diff * diff, axis=-1, keepdims=True), axis=0, keepdims=True)
            dout = diff * (1.0 / d)
            dg_ref[...] += jnp.sum(dout * xhat, axis=0, keepdims=True)
            dxhat = dout * gv
            dh_ref[...] = r * (dxhat - xhat * jnp.mean(dxhat * xhat, axis=-1, keepdims=True))

    row = pl.BlockSpec((TM, d), lambda i: (i, 0))
    return _pcall(
        body, name="final_loss", grid=(t // TM,),
        in_specs=[row, pl.BlockSpec((1, d), lambda i: (0, 0)),
                  pl.BlockSpec((TM, d), lambda i: (jnp.maximum(i - 1, 0), 0))],
        out_specs=[pl.BlockSpec((8, 128), lambda i: (0, 0)), row, pl.BlockSpec((1, d), lambda i: (0, 0))],
        out_shape=[jax.ShapeDtypeStruct((8, 128), F32), jax.ShapeDtypeStruct((t, d), F32),
                   jax.ShapeDtypeStruct((1, d), F32)],
        compiler_params=_params("arbitrary"),
    )(h, fg, target)


def _w_nn_spec(w, k):
    if w.shape[1] == k:
        ns = w.shape[2]
        return w, pl.BlockSpec((None, k, ns), lambda j, i: (j, 0, 0)), w.shape[0], ns
    w2 = w.reshape(w.shape[0] * w.shape[1], w.shape[2])
    tn = 1024 if w2.shape[1] % 1024 == 0 else w2.shape[1]
    return w2, pl.BlockSpec((k, tn), lambda j, i: (0, j)), w2.shape[1] // tn, tn


def mm_nn(a, w, name, out_dtype=F32):
    t, k = a.shape
    w, wspec, nb, tn = _w_nn_spec(w, k)

    def body(a_ref, w_ref, o_ref):
        o_ref[...] = _dot(a_ref[...], w_ref[...]).astype(o_ref.dtype)

    return _pcall(
        body, name=name, grid=(nb, t // TM),
        in_specs=[pl.BlockSpec((TM, k), lambda j, i: (i, 0)), wspec],
        out_specs=pl.BlockSpec((TM, tn), lambda j, i: (i, j)),
        out_shape=jax.ShapeDtypeStruct((t, nb * tn), out_dtype),
        compiler_params=_params("parallel", "parallel"),
    )(a, w)


def mm_nn_residual(a, w, h, gate, name):
    t, k = a.shape
    w, wspec, nb, tn = _w_nn_spec(w, k)

    def body(a_ref, w_ref, h_ref, gt_ref, y_ref, o_ref):
        y = _dot(a_ref[...], w_ref[...])
        y_ref[...] = y
        o_ref[...] = h_ref[...] + gt_ref[...] * y

    out = pl.BlockSpec((TM, tn), lambda j, i: (i, j))
    return _pcall(
        body, name=name, grid=(nb, t // TM),
        in_specs=[pl.BlockSpec((TM, k), lambda j, i: (i, 0)), wspec, out,
                  pl.BlockSpec((None, 1, tn), lambda j, i: (_sel(i), 0, j))],
        out_specs=[out, out],
        out_shape=[jax.ShapeDtypeStruct((t, nb * tn), F32)] * 2,
        compiler_params=_params("parallel", "parallel"),
    )(a, w, h, gate)


def mm_swiglu(a, wg, wu, name):
    t, k = a.shape
    s, _, ns = wg.shape

    def body(a_ref, wg_ref, wu_ref, g_ref, u_ref, act_ref):
        av = a_ref[...]
        g = _dot(av, wg_ref[...])
        u = _dot(av, wu_ref[...])
        g_ref[...] = g
        u_ref[...] = u
        act_ref[...] = (g * _sigmoid(g) * u).astype(BF16)

    wspec = pl.BlockSpec((None, k, ns), lambda j, i: (j, 0, 0))
    out = pl.BlockSpec((TM, ns), lambda j, i: (i, j))
    return _pcall(
        body, name=name, grid=(s, t // TM),
        in_specs=[pl.BlockSpec((TM, k), lambda j, i: (i, 0)), wspec, wspec],
        out_specs=[out, out, out],
        out_shape=[jax.ShapeDtypeStruct((t, s * ns), F32), jax.ShapeDtypeStruct((t, s * ns), F32),
                   jax.ShapeDtypeStruct((t, s * ns), BF16)],
        compiler_params=_params("parallel", "parallel"),
    )(a, wg, wu)


def mm_nt_cols(dy, w, name, add=None):
    t, n = dy.shape
    s, k, ns = w.shape
    tk = 512

    def body(*refs):
        if add is None:
            dy_ref, w_ref, o_ref = refs
        else:
            dy_ref, w_ref, add_ref, o_ref = refs
        acc = _dot_nt(dy_ref[:, 0:ns], w_ref[0])
        for sh in range(1, s):
            acc += _dot_nt(dy_ref[:, sh * ns:(sh + 1) * ns], w_ref[sh])
        if add is not None:
            acc += add_ref[...]
        o_ref[...] = acc

    out = pl.BlockSpec((TM, tk), lambda j, i: (i, j))
    in_specs = [pl.BlockSpec((TM, n), lambda j, i: (i, 0)), pl.BlockSpec((s, tk, ns), lambda j, i: (0, j, 0))]
    args = [dy, w]
    if add is not None:
        in_specs.append(out)
        args.append(add)
    return _pcall(
        body, name=name, grid=(k // tk, t // TM),
        in_specs=in_specs, out_specs=out,
        out_shape=jax.ShapeDtypeStruct((t, k), F32),
        compiler_params=_params("parallel", "parallel"),
    )(*args)


def mm_nt_rows(dy, w, name):
    t, n = dy.shape
    s, ks, _ = w.shape

    def body(dy_ref, w_ref, o_ref):
        o_ref[...] = _dot_nt(dy_ref[...], w_ref[...])

    return _pcall(
        body, name=name, grid=(s, t // TM),
        in_specs=[pl.BlockSpec((TM, n), lambda j, i: (i, 0)), pl.BlockSpec((None, ks, n), lambda j, i: (j, 0, 0))],
        out_specs=pl.BlockSpec((TM, ks), lambda j, i: (i, j)),
        out_shape=jax.ShapeDtypeStruct((t, s * ks), F32),
        compiler_params=_params("parallel", "parallel"),
    )(dy, w)


def mm_nt_swiglu_bwd(df, wd, g, u, name):
    t, n = df.shape
    s, ks, _ = wd.shape

    def body(df_ref, w_ref, g_ref, u_ref, dg_ref, du_ref):
        da = _dot_nt(df_ref[...], w_ref[...])
        gv = g_ref[...]
        sg = _sigmoid(gv)
        silu = gv * sg
        dg_ref[...] = (da * u_ref[...] * (sg * (1.0 + gv * (1.0 - sg)))).astype(BF16)
        du_ref[...] = (da * silu).astype(BF16)

    blk = pl.BlockSpec((TM, ks), lambda j, i: (i, j))
    return _pcall(
        body, name=name, grid=(s, t // TM),
        in_specs=[pl.BlockSpec((TM, n), lambda j, i: (i, 0)), pl.BlockSpec((None, ks, n), lambda j, i: (j, 0, 0)),
                  blk, blk],
        out_specs=[blk, blk],
        out_shape=[jax.ShapeDtypeStruct((t, s * ks), BF16)] * 2,
        compiler_params=_params("parallel", "parallel"),
    )(df, wd, g, u)


def mm_tn(x, dy, name, col_sharded):
    t, k = x.shape
    n = dy.shape[1]
    tk = 512
    mc = t // 2 if (t // 2) % 128 == 0 else t
    if col_sharded:
        tn = n // N_SHARD
        out_shape = jax.ShapeDtypeStruct((N_SHARD, k, tn), F32)
        out_spec = pl.BlockSpec((None, tk, tn), lambda j, kb, m: (j, kb, 0))
    else:
        tn = 1024
        out_shape = jax.ShapeDtypeStruct((k, n), F32)
        out_spec = pl.BlockSpec((tk, tn), lambda j, kb, m: (kb, j))

    def body(x_ref, dy_ref, o_ref):
        m = pl.program_id(2)
        acc = _dot_tn(x_ref[...], dy_ref[...])

        @pl.when(m == 0)
        def _():
            o_ref[...] = acc

        @pl.when(m > 0)
        def _():
            o_ref[...] += acc

    out = _pcall(
        body, name=name, grid=(n // tn, k // tk, t // mc),
        in_specs=[pl.BlockSpec((mc, tk), lambda j, kb, m: (m, kb)), pl.BlockSpec((mc, tn), lambda j, kb, m: (m, j))],
        out_specs=out_spec, out_shape=out_shape,
        compiler_params=_params("parallel", "parallel", "arbitrary"),
    )(x, dy)
    return out if col_sharded else out.reshape(N_SHARD, k // N_SHARD, n)


def _swap32(y):
    right = pltpu.roll(y, 32, 1)
    left = pltpu.roll(y, 96, 1)
    lane = lax.broadcasted_iota(jnp.int32, y.shape, 1)
    return jnp.where((lane // 32) % 2 == 0, left, right)


def _rope_tables(t, ctx_len):
    s = t - ctx_len
    pos = np.arange(s)
    row = (pos // GRID_W).astype(np.float32)
    col = (pos % GRID_W).astype(np.float32)
    half = HEAD_DIM // 2
    inv = np.power(np.float32(ROPE_THETA), -np.arange(0, half, 2, dtype=np.float32) / np.float32(half))
    ar = row[:, None] * inv
    ac = col[:, None] * inv
    cos = np.concatenate([np.cos(ar), np.cos(ar), np.cos(ac), np.cos(ac)], axis=1)
    sin = np.concatenate([-np.sin(ar), np.sin(ar), -np.sin(ac), np.sin(ac)], axis=1)
    cos = np.concatenate([np.ones((ctx_len, HEAD_DIM), np.float32), cos.astype(np.float32)], axis=0)
    sin = np.concatenate([np.zeros((ctx_len, HEAD_DIM), np.float32), sin.astype(np.float32)], axis=0)
    return jnp.asarray(cos, F32), jnp.asarray(sin, F32)


def qk_prep_fwd(proj, qg, kg, cos, sin, name):
    t = proj.shape[0]

    def body(p_ref, qg_ref, kg_ref, c_ref, s_ref, q_ref, k_ref, v_ref):
        cv = c_ref[...]
        sv = s_ref[...]
        for hd in range(Q_HEADS + KV_HEADS):
            x = p_ref[:, hd * HEAD_DIM:(hd + 1) * HEAD_DIM]
            gv = qg_ref[...] if hd < Q_HEADS else kg_ref[...]
            y = x * lax.rsqrt(jnp.mean(x * x, axis=-1, keepdims=True) + EPS) * gv
            out = (y * cv + _swap32(y) * sv).astype(BF16)
            if hd < Q_HEADS:
                q_ref[:, hd * HEAD_DIM:(hd + 1) * HEAD_DIM] = out
            else:
                k_ref[:, (hd - Q_HEADS) * HEAD_DIM:(hd - Q_HEADS + 1) * HEAD_DIM] = out
        v_ref[...] = p_ref[:, ATT_WIDTH + KV_WIDTH:QKV_WIDTH].astype(BF16)

    vec = pl.BlockSpec((1, HEAD_DIM), lambda i: (0, 0))
    tab = pl.BlockSpec((TM, HEAD_DIM), lambda i: (i, 0))
    return _pcall(
        body, name=name, grid=(t // TM,),
        in_specs=[pl.BlockSpec((TM, QKV_WIDTH), lambda i: (i, 0)), vec, vec, tab, tab],
        out_specs=[pl.BlockSpec((TM, ATT_WIDTH), lambda i: (i, 0)), pl.BlockSpec((TM, KV_WIDTH), lambda i: (i, 0)),
                   pl.BlockSpec((TM, KV_WIDTH), lambda i: (i, 0))],
        out_shape=[jax.ShapeDtypeStruct((t, ATT_WIDTH), BF16), jax.ShapeDtypeStruct((t, KV_WIDTH), BF16),
                   jax.ShapeDtypeStruct((t, KV_WIDTH), BF16)],
        compiler_params=_params("parallel"),
    )(proj, qg, kg, cos, sin)


def qk_prep_bwd(dq, dk, dv, daux, proj, qg, kg, cos, sin, name):
    t, n_in = proj.shape
    n_aux = daux.shape[0]

    def body(dq_ref, dk_ref, dv_ref, da_ref, p_ref, qg_ref, kg_ref, c_ref, s_ref, o_ref, dqg_ref, dkg_ref):
        @pl.when(pl.program_id(0) == 0)
        def _():
            dqg_ref[...] = jnp.zeros_like(dqg_ref)
            dkg_ref[...] = jnp.zeros_like(dkg_ref)

        cv = c_ref[...]
        sv = s_ref[...]
        for hd in range(Q_HEADS + KV_HEADS):
            cols = slice(hd * HEAD_DIM, (hd + 1) * HEAD_DIM)
            x = p_ref[:, cols]
            if hd < Q_HEADS:
                gv, dyr, dg_ref = qg_ref[...], dq_ref[:, cols], dqg_ref
            else:
                kc = slice((hd - Q_HEADS) * HEAD_DIM, (hd - Q_HEADS + 1) * HEAD_DIM)
                gv, dyr, dg_ref = kg_ref[...], dk_ref[:, kc], dkg_ref
            r = lax.rsqrt(jnp.mean(x * x, axis=-1, keepdims=True) + EPS)
            xhat = x * r
            dy = dyr * cv + _swap32(dyr * sv)
            dg_ref[...] += jnp.sum(dy * xhat, axis=0, keepdims=True)
            dxhat = dy * gv
            o_ref[:, cols] = (r * (dxhat - xhat * jnp.mean(dxhat * xhat, axis=-1, keepdims=True))).astype(BF16)
        o_ref[:, ATT_WIDTH + KV_WIDTH:QKV_WIDTH] = dv_ref[...].astype(BF16)
        for a in range(n_aux):
            o_ref[:, QKV_WIDTH + a * AUX_WIDTH:QKV_WIDTH + (a + 1) * AUX_WIDTH] = da_ref[a]

    vec = pl.BlockSpec((1, HEAD_DIM), lambda i: (0, 0))
    tab = pl.BlockSpec((TM, HEAD_DIM), lambda i: (i, 0))
    return _pcall(
        body, name=name, grid=(t // TM,),
        in_specs=[pl.BlockSpec((TM, ATT_WIDTH), lambda i: (i, 0)), pl.BlockSpec((TM, KV_WIDTH), lambda i: (i, 0)),
                  pl.BlockSpec((TM, KV_WIDTH), lambda i: (i, 0)),
                  pl.BlockSpec((n_aux, TM, AUX_WIDTH), lambda i: (0, i, 0)),
                  pl.BlockSpec((TM, QKV_WIDTH), lambda i: (i, 0)), vec, vec, tab, tab],
        out_specs=[pl.BlockSpec((TM, n_in), lambda i: (i, 0)), vec, vec],
        out_shape=[jax.ShapeDtypeStruct((t, n_in), BF16), jax.ShapeDtypeStruct((1, HEAD_DIM), F32),
                   jax.ShapeDtypeStruct((1, HEAD_DIM), F32)],
        compiler_params=_params("arbitrary"),
    )(dq, dk, dv, daux, proj, qg, kg, cos, sin)


def _band_start(i, t):
    return pl.multiple_of(jnp.clip(i * TM - WINDOW, 0, t - BAND), WINDOW)


def _band_mask(i, start, ctx_len):
    shape = (TM, ctx_len + BAND)
    col = lax.broadcasted_iota(jnp.int32, shape, 1)
    qrow = i * TM + lax.broadcasted_iota(jnp.int32, shape, 0)
    krow = start + col - ctx_len
    band_ok = (krow >= ctx_len) & (jnp.abs(krow - qrow) <= WINDOW)
    return (col < ctx_len) | band_ok


def attention_fwd(q, k, v, sink, ctx_len, windowed, name):
    t = q.shape[0]
    d_model = ATT_WIDTH + AUX_WIDTH
    gw = GROUPS * HEAD_DIM

    def one_head(qh, kk, vv, mask, sink_val):
        s = _dot_nt(qh, kk) * ATT_SCALE
        if mask is not None:
            s = jnp.where(mask, s, NEG_INF)
        m = jnp.max(s, axis=-1, keepdims=True)
        if sink_val is not None:
            m = jnp.maximum(m, sink_val)
        p = jnp.exp(s - m)
        l = jnp.sum(p, axis=-1, keepdims=True)
        if sink_val is not None:
            l = l + jnp.exp(sink_val - m)
        o = _dot(p.astype(BF16), vv) / l
        return o, m + jnp.log(l)

    def body(sink_ref, q_ref, k_ref, v_ref, o_ref, lse_ref):
        kv = pl.program_id(0)
        i = pl.program_id(1)

        def run(kk, vv, mask):
            for g in range(GROUPS):
                sink_val = sink_ref[kv * GROUPS + g] if windowed else None
                o, lse = one_head(q_ref[:, g * HEAD_DIM:(g + 1) * HEAD_DIM], kk, vv, mask, sink_val)
                o_ref[:, g * HEAD_DIM:(g + 1) * HEAD_DIM] = o.astype(BF16)
                lse_ref[g] = lse

        @pl.when(i == 0)
        def _():
            if windowed:
                o_ref[...] = jnp.zeros_like(o_ref)
                lse_ref[...] = jnp.zeros_like(lse_ref)
            else:
                run(k_ref[0:ctx_len], v_ref[0:ctx_len], None)

        @pl.when(i > 0)
        def _():
            if windowed:
                start = _band_start(i, t)
                kk = jnp.concatenate([k_ref[0:ctx_len], k_ref[pl.ds(start, BAND)]], axis=0)
                vv = jnp.concatenate([v_ref[0:ctx_len], v_ref[pl.ds(start, BAND)]], axis=0)
                run(kk, vv, _band_mask(i, start, ctx_len))
            else:
                run(k_ref[...], v_ref[...], None)

    kvspec = pl.BlockSpec((t, HEAD_DIM), lambda kv, i: (0, kv))
    return _pcall(
        body, name=name, grid=(KV_HEADS, t // TM),
        in_specs=[pl.BlockSpec(memory_space=pltpu.SMEM), pl.BlockSpec((TM, gw), lambda kv, i: (i, kv)), kvspec, kvspec],
        out_specs=[pl.BlockSpec((TM, gw), lambda kv, i: (i, kv)),
                   pl.BlockSpec((GROUPS, TM, 1), lambda kv, i: (kv, i, 0))],
        out_shape=[jax.ShapeDtypeStruct((t, d_model), BF16), jax.ShapeDtypeStruct((Q_HEADS, t, 1), F32)],
        compiler_params=_params("parallel", "parallel"),
    )(sink, q, k, v)


def attention_bwd(q, k, v, cat, lse, dcat, sink, ctx_len, windowed, name):
    t = q.shape[0]
    gw = GROUPS * HEAD_DIM

    def body(sink_ref, q_ref, k_ref, v_ref, o_ref, lse_ref, do_ref, dq_ref, dk_ref, dv_ref, dsink_ref):
        kv = pl.program_id(0)
        i = pl.program_id(1)

        @pl.when(i == 0)
        def _():
            dk_ref[...] = jnp.zeros_like(dk_ref)
            dv_ref[...] = jnp.zeros_like(dv_ref)
            dsink_ref[...] = jnp.zeros_like(dsink_ref)

        def run(kk, vv, mask, accumulate):
            for g in range(GROUPS):
                cols = slice(g * HEAD_DIM, (g + 1) * HEAD_DIM)
                qh = q_ref[:, cols]
                doh = do_ref[:, cols]
                delta = jnp.sum(doh * o_ref[:, cols].astype(F32), axis=-1, keepdims=True)
                lse_g = lse_ref[g]
                s = _dot_nt(qh, kk) * ATT_SCALE
                if mask is not None:
                    s = jnp.where(mask, s, NEG_INF)
                p = jnp.exp(s - lse_g)
                dob = doh.astype(BF16)
                dp = _dot_nt(dob, vv)
                ds = (p * (dp - delta) * ATT_SCALE).astype(BF16)
                dq_ref[:, cols] = _dot(ds, kk)
                accumulate(_dot_tn(ds, qh), _dot_tn(p.astype(BF16), dob))
                if windowed:
                    p_sink = jnp.exp(sink_ref[kv * GROUPS + g] - lse_g)
                    dsink_ref[g:g + 1, :] += jnp.sum(-p_sink * delta, axis=0, keepdims=True)

        @pl.when(i == 0)
        def _():
            if windowed:
                dq_ref[...] = jnp.zeros_like(dq_ref)
            else:
                def acc_ctx(dkp, dvp):
                    dk_ref[0:ctx_len] += dkp
                    dv_ref[0:ctx_len] += dvp

                run(k_ref[0:ctx_len], v_ref[0:ctx_len], None, acc_ctx)

        @pl.when(i > 0)
        def _():
            if windowed:
                start = _band_start(i, t)
                kk = jnp.concatenate([k_ref[0:ctx_len], k_ref[pl.ds(start, BAND)]], axis=0)
                vv = jnp.concatenate([v_ref[0:ctx_len], v_ref[pl.ds(start, BAND)]], axis=0)

                def acc_band(dkp, dvp):
                    dk_ref[0:ctx_len] += dkp[0:ctx_len]
                    dv_ref[0:ctx_len] += dvp[0:ctx_len]
                    dk_ref[pl.ds(start, BAND)] += dkp[ctx_len:ctx_len + BAND]
                    dv_ref[pl.ds(start, BAND)] += dvp[ctx_len:ctx_len + BAND]

                run(kk, vv, _band_mask(i, start, ctx_len), acc_band)
            else:
                def acc_all(dkp, dvp):
                    dk_ref[...] += dkp
                    dv_ref[...] += dvp

                run(k_ref[...], v_ref[...], None, acc_all)

    kvspec = pl.BlockSpec((t, HEAD_DIM), lambda kv, i: (0, kv))
    grp = pl.BlockSpec((TM, gw), lambda kv, i: (i, kv))
    return _pcall(
        body, name=name, grid=(KV_HEADS, t // TM),
        in_specs=[pl.BlockSpec(memory_space=pltpu.SMEM), grp, kvspec, kvspec, grp,
                  pl.BlockSpec((GROUPS, TM, 1), lambda kv, i: (kv, i, 0)), grp],
        out_specs=[grp, kvspec, kvspec, pl.BlockSpec((None, 8, 128), lambda kv, i: (kv, 0, 0))],
        out_shape=[jax.ShapeDtypeStruct((t, ATT_WIDTH), F32), jax.ShapeDtypeStruct((t, KV_WIDTH), F32),
                   jax.ShapeDtypeStruct((t, KV_WIDTH), F32), jax.ShapeDtypeStruct((KV_HEADS, 8, 128), F32)],
        compiler_params=_params("parallel", "arbitrary"),
    )(sink, q, k, v, cat, lse, dcat)


def _segment_pos(t, ctx_len, width):
    row = lax.broadcasted_iota(jnp.int32, (t, width), 0)
    in_ctx = row < ctx_len
    return jnp.where(in_ctx, row, row - ctx_len), jnp.where(in_ctx, ctx_len, t - ctx_len)


def _shifted(x, offset, pos, seg_len):
    t = x.shape[0]
    moved = x if offset == 0 else pltpu.roll(x, (-offset) % t, 0)
    ok = (pos + offset >= 0) & (pos + offset < seg_len)
    return jnp.where(ok, moved, 0.0)


def conv_fwd(proj, conv_w, cat, ctx_len, name):
    t = proj.shape[0]
    base = QKV_WIDTH // 128
    per = AUX_WIDTH // 128

    def body(gb_ref, gc_ref, u_ref, w_ref, cat_in, o_ref):
        del cat_in
        pos, seg = _segment_pos(t, ctx_len, 128)
        z = gc_ref[...] * u_ref[...]
        conv = (w_ref[0:1, :] * _shifted(z, -1, pos, seg) + w_ref[1:2, :] * z
                + w_ref[2:3, :] * _shifted(z, 1, pos, seg))
        o_ref[...] = (gb_ref[...] * conv).astype(BF16)

    def col(off):
        return pl.BlockSpec((t, 128), lambda j: (0, base + off * per + j))

    return _pcall(
        body, name=name, grid=(per,),
        in_specs=[col(0), col(1), col(2), pl.BlockSpec((3, 128), lambda j: (0, j)),
                  pl.BlockSpec(memory_space=pl.ANY)],
        out_specs=pl.BlockSpec((t, 128), lambda j: (0, ATT_WIDTH // 128 + j)),
        out_shape=jax.ShapeDtypeStruct(cat.shape, BF16),
        input_output_aliases={4: 0},
        compiler_params=_params("parallel"),
    )(proj, proj, proj, conv_w, cat)


def conv_bwd(dcat, proj, conv_w, ctx_len, name):
    t = proj.shape[0]
    base = QKV_WIDTH // 128
    per = AUX_WIDTH // 128

    def body(do_ref, gb_ref, gc_ref, u_ref, w_ref, da_ref, dw_ref):
        pos, seg = _segment_pos(t, ctx_len, 128)
        gc = gc_ref[...]
        u = u_ref[...]
        z = gc * u
        zm = _shifted(z, -1, pos, seg)
        zp = _shifted(z, 1, pos, seg)
        conv = w_ref[0:1, :] * zm + w_ref[1:2, :] * z + w_ref[2:3, :] * zp
        dout = do_ref[...]
        da_ref[0] = (dout * conv).astype(BF16)
        dconv = dout * gb_ref[...]
        dw_ref[0:1, :] = jnp.sum(dconv * zm, axis=0, keepdims=True)
        dw_ref[1:2, :] = jnp.sum(dconv * z, axis=0, keepdims=True)
        dw_ref[2:3, :] = jnp.sum(dconv * zp, axis=0, keepdims=True)
        dz = (w_ref[1:2, :] * dconv + w_ref[0:1, :] * _shifted(dconv, 1, pos, seg)
              + w_ref[2:3, :] * _shifted(dconv, -1, pos, seg))
        da_ref[1] = (dz * u).astype(BF16)
        da_ref[2] = (dz * gc).astype(BF16)

    def col(off):
        return pl.BlockSpec((t, 128), lambda j: (0, base + off * per + j))

    return _pcall(
        body, name=name, grid=(per,),
        in_specs=[pl.BlockSpec((t, 128), lambda j: (0, ATT_WIDTH // 128 + j)), col(0), col(1), col(2),
                  pl.BlockSpec((3, 128), lambda j: (0, j))],
        out_specs=[pl.BlockSpec((3, t, 128), lambda j: (0, 0, j)), pl.BlockSpec((3, 128), lambda j: (0, j))],
        out_shape=[jax.ShapeDtypeStruct((3, t, AUX_WIDTH), BF16), jax.ShapeDtypeStruct((3, AUX_WIDTH), F32)],
        compiler_params=_params("parallel"),
    )(dcat, proj, proj, proj, conv_w)


def _pooled(u, w, pos, seg):
    lo = jnp.clip(pos - w // 2, 0, seg)
    hi = jnp.clip(pos - w // 2 + w, 0, seg)
    inv = 1.0 / (hi - lo).astype(F32)
    total = _shifted(u, -(w // 2), pos, seg)
    for o in range(-(w // 2) + 1, w // 2):
        total = total + _shifted(u, o, pos, seg)
    return total * inv - u, inv


def pool_fwd(proj, pool_w, pool_scale, cat, ctx_len, name):
    t = proj.shape[0]

    def body(u_ref, w_ref, sc_ref, cat_in, o_ref):
        del cat_in
        pos, seg = _segment_pos(t, ctx_len, 128)
        for g, w in enumerate(POOL_WINDOWS):
            cols = slice(g * 128, (g + 1) * 128)
            pooled, _ = _pooled(u_ref[:, cols], w, pos, seg)
            mixed = _dot(pooled.astype(BF16), w_ref[g].astype(BF16))
            o_ref[:, cols] = (mixed * sc_ref[:, cols]).astype(BF16)

    return _pcall(
        body, name=name, grid=(1,),
        in_specs=[pl.BlockSpec((t, AUX_WIDTH), lambda j: (0, QKV_WIDTH // AUX_WIDTH)),
                  pl.BlockSpec((AUX_GROUPS, 128, 128), lambda j: (0, 0, 0)),
                  pl.BlockSpec((1, AUX_WIDTH), lambda j: (0, 0)), pl.BlockSpec(memory_space=pl.ANY)],
        out_specs=pl.BlockSpec((t, AUX_WIDTH), lambda j: (0, ATT_WIDTH // AUX_WIDTH)),
        out_shape=jax.ShapeDtypeStruct(cat.shape, BF16),
        input_output_aliases={3: 0},
        compiler_params=_params("arbitrary"),
    )(proj, pool_w, pool_scale, cat)


def pool_bwd(dcat, proj, pool_w, pool_scale, ctx_len, name):
    t = proj.shape[0]

    def body(do_ref, u_ref, w_ref, sc_ref, da_ref, dw_ref, dsc_ref):
        pos, seg = _segment_pos(t, ctx_len, 128)
        for g, w in enumerate(POOL_WINDOWS):
            cols = slice(g * 128, (g + 1) * 128)
            pooled, inv = _pooled(u_ref[:, cols], w, pos, seg)
            pb = pooled.astype(BF16)
            wb = w_ref[g].astype(BF16)
            mixed = _dot(pb, wb)
            dout = do_ref[:, cols]
            dsc_ref[:, cols] = jnp.sum(dout * mixed, axis=0, keepdims=True)
            dmixed = (dout * sc_ref[:, cols]).astype(BF16)
            dw_ref[g] = _dot_tn(pb, dmixed)
            dpooled = _dot_nt(dmixed, wb)
            spread = dpooled * inv
            du = _shifted(spread, w // 2, pos, seg) - dpooled
            for o in range(-(w // 2) + 1, w // 2):
                du = du + _shifted(spread, -o, pos, seg)
            da_ref[0, :, cols] = du.astype(BF16)

    return _pcall(
        body, name=name, grid=(1,),
        in_specs=[pl.BlockSpec((t, AUX_WIDTH), lambda j: (0, ATT_WIDTH // AUX_WIDTH)),
                  pl.BlockSpec((t, AUX_WIDTH), lambda j: (0, QKV_WIDTH // AUX_WIDTH)),
                  pl.BlockSpec((AUX_GROUPS, 128, 128), lambda j: (0, 0, 0)),
                  pl.BlockSpec((1, AUX_WIDTH), lambda j: (0, 0))],
        out_specs=[pl.BlockSpec((1, t, AUX_WIDTH), lambda j: (0, 0, 0)),
                   pl.BlockSpec((AUX_GROUPS, 128, 128), lambda j: (0, 0, 0)),
                   pl.BlockSpec((1, AUX_WIDTH), lambda j: (0, 0))],
        out_shape=[jax.ShapeDtypeStruct((1, t, AUX_WIDTH), BF16), jax.ShapeDtypeStruct((AUX_GROUPS, 128, 128), F32),
                   jax.ShapeDtypeStruct((1, AUX_WIDTH), F32)],
        compiler_params=_params("arbitrary"),
    )(dcat, proj, pool_w, pool_scale)


MOD_NAMES = ("shift1", "scale1", "gate1", "shift2", "scale2", "gate2")


def _layer_fwd(h, lw, mods, tabs, ctx_len, windowed, tag):
    cos, sin = tabs
    xn = norm_mod_fwd(h, lw["norm1_g"], mods["shift1"], mods["scale1"], tag + "norm1_fwd")
    proj = mm_nn(xn, lw["w_in"], tag + "w_in_fwd")
    q, k, v = qk_prep_fwd(proj, lw["q_g"], lw["k_g"], cos, sin, tag + "qk_fwd")
    cat, lse = attention_fwd(q, k, v, lw["sink"], ctx_len, windowed, tag + "attn_fwd")
    if windowed:
        cat = pool_fwd(proj, lw["pool_w"], lw["pool_scale"], cat, ctx_len, tag + "pool_fwd")
    else:
        cat = conv_fwd(proj, lw["conv_w"], cat, ctx_len, tag + "conv_fwd")
    y, h1 = mm_nn_residual(cat, lw["w_out"], h, mods["gate1"], tag + "w_out_fwd")
    hn = norm_mod_fwd(h1, lw["norm2_g"], mods["shift2"], mods["scale2"], tag + "norm2_fwd")
    g, u, act = mm_swiglu(hn, lw["w_gate"], lw["w_up"], tag + "ffn_in_fwd")
    f, h2 = mm_nn_residual(act, lw["w_down"], h1, mods["gate2"], tag + "w_down_fwd")
    saved = dict(h=h, xn=xn, proj=proj, q=q, k=k, v=v, cat=cat, lse=lse, y=y, h1=h1, hn=hn, g=g, u=u, act=act, f=f)
    return h2, saved


def _layer_bwd(dh, sv, lw, mods, tabs, ctx_len, windowed, tag):
    cos, sin = tabs
    big, small = {}, {}
    df, dgate2 = gate_bwd(dh, sv["f"], mods["gate2"], tag + "gate2_bwd")
    dgp, du = mm_nt_swiglu_bwd(df, lw["w_down"], sv["g"], sv["u"], tag + "w_down_dgrad")
    big["w_down"] = mm_tn(sv["act"], df, tag + "w_down_wgrad", col_sharded=False)
    dhn = mm_nt_cols(dgp, lw["w_gate"], tag + "w_gate_dgrad")
    dhn = mm_nt_cols(du, lw["w_up"], tag + "w_up_dgrad", add=dhn)
    big["w_gate"] = mm_tn(sv["hn"], dgp, tag + "w_gate_wgrad", col_sharded=True)
    big["w_up"] = mm_tn(sv["hn"], du, tag + "w_up_wgrad", col_sharded=True)
    dh1, dshift2, dscale2, small["norm2_g"] = norm_mod_bwd(dhn, sv["h1"], lw["norm2_g"], mods["scale2"], dh,
                                                           tag + "norm2_bwd")
    dy, dgate1 = gate_bwd(dh1, sv["y"], mods["gate1"], tag + "gate1_bwd")
    dcat = mm_nt_rows(dy, lw["w_out"], tag + "w_out_dgrad")
    big["w_out"] = mm_tn(sv["cat"], dy, tag + "w_out_wgrad", col_sharded=False)
    if windowed:
        daux, small["pool_w"], small["pool_scale"] = pool_bwd(dcat, sv["proj"], lw["pool_w"], lw["pool_scale"],
                                                              ctx_len, tag + "pool_bwd")
    else:
        daux, small["conv_w"] = conv_bwd(dcat, sv["proj"], lw["conv_w"], ctx_len, tag + "conv_bwd")
    dq, dk, dv, dsink = attention_bwd(sv["q"], sv["k"], sv["v"], sv["cat"], sv["lse"], dcat, lw["sink"], ctx_len,
                                      windowed, tag + "attn_bwd")
    if windowed:
        small["sink"] = dsink[:, :GROUPS, 0].reshape(Q_HEADS)
    dproj, small["q_g"], small["k_g"] = qk_prep_bwd(dq, dk, dv, daux, sv["proj"], lw["q_g"], lw["k_g"], cos, sin,
                                                    tag + "qk_bwd")
    dxn = mm_nt_cols(dproj, lw["w_in"], tag + "w_in_dgrad")
    big["w_in"] = mm_tn(sv["xn"], dproj, tag + "w_in_wgrad", col_sharded=True)
    dh0, dshift1, dscale1, small["norm1_g"] = norm_mod_bwd(dxn, sv["h"], lw["norm1_g"], mods["scale1"], dh1,
                                                           tag + "norm1_bwd")
    dmod = jnp.concatenate([a[:, 0, :] for a in (dshift1, dscale1, dgate1, dshift2, dscale2, dgate2)], axis=1)
    return dh0, big, small, dmod


def _local_step(x, ctx, target, layers, mods, final_g):
    ctx_len = ctx.shape[0]
    h = jnp.concatenate([ctx, x], axis=0)
    tabs = _rope_tables(h.shape[0], ctx_len)
    saved = []
    for l, lw in enumerate(layers):
        h, sv = _layer_fwd(h, lw, mods[l], tabs, ctx_len, l % 2 == 1, f"l{l}_")
        saved.append(sv)
    loss, dh, dfg = final_loss(h, final_g, target)
    bigs, smalls, dmods = [None] * len(layers), [None] * len(layers), [None] * len(layers)
    for l in reversed(range(len(layers))):
        dh, bigs[l], smalls[l], dmods[l] = _layer_bwd(dh, saved[l], layers[l], mods[l], tabs, ctx_len, l % 2 == 1,
                                                       f"l{l}_")
    return loss, dh[ctx_len:], bigs, smalls, dmods, dfg


def _place():
    x, y, c = lax.axis_index("x"), lax.axis_index("y"), lax.axis_index("c")
    chips = [(1 - x, y), (x, 1 - y), (1 - x, 1 - y)]
    return x, y, c, chips


def _remote(src, dst, send_sem, recv_sem, to):
    return pltpu.make_async_remote_copy(src_ref=src, dst_ref=dst, send_sem=send_sem, recv_sem=recv_sem,
                                        device_id=to, device_id_type=MESH)


ANY = pl.BlockSpec(memory_space=pl.ANY)


def all_gather8(x_shard, name):
    m_per, n = x_shard.shape

    def body(x_ref, out_ref, send_sems, recv_sems, local_sem):
        x, y, c, chips = _place()
        me, sibling = (x, y, c), (x, y, 1 - c)

        def rows(px, py, pc):
            return out_ref.at[pl.ds((4 * px + 2 * py + pc) * m_per, m_per), :]

        def copy(k, block, to, src=None):
            return _remote(rows(*block) if src is None else src, rows(*block), send_sems.at[k], recv_sems.at[k], to)

        mine = pltpu.make_async_copy(x_ref, rows(*me), local_sem)
        mine.start()
        first = [copy(0, me, sibling, src=x_ref)]
        first += [copy(1 + j, me, (*chip, c), src=x_ref) for j, chip in enumerate(chips)]
        for cp in first:
            cp.start()
        passed = [copy(4 + j, (*chip, c), sibling) for j, chip in enumerate(chips)]
        for j, chip in enumerate(chips):
            copy(1 + j, (*chip, c), me).wait_recv()
            passed[j].start()
        copy(0, sibling, me).wait_recv()
        for j, chip in enumerate(chips):
            copy(4 + j, (*chip, 1 - c), me).wait_recv()
        for cp in first + passed:
            cp.wait_send()
        mine.wait()

    return _pcall(
        body, name=name,
        out_shape=jax.ShapeDtypeStruct((N_DEV * m_per, n), x_shard.dtype),
        in_specs=[pl.BlockSpec(memory_space=pltpu.VMEM)],
        out_specs=pl.BlockSpec(memory_space=pltpu.VMEM),
        scratch_shapes=[pltpu.SemaphoreType.DMA((7,)), pltpu.SemaphoreType.DMA((7,)), pltpu.SemaphoreType.DMA],
        compiler_params=pltpu.CompilerParams(vmem_limit_bytes=VMEM_LIMIT),
    )(x_shard)


def gather_weight_shards(shards):
    n = len(shards)

    def body(*refs):
        ins, outs = refs[:n], refs[n:2 * n]
        send_sems, recv_sems, local_sems = refs[2 * n:]
        x, y, c, chips = _place()
        p = 2 * x + y
        sibling = (x, y, 1 - c)
        started, own = [], []
        for i in range(n):
            hr = ins[i].shape[0] // 2
            mine = pltpu.make_async_copy(ins[i], outs[i].at[p], local_sems.at[i])
            mine.start()
            own.append(mine)
            for j, chip in enumerate(chips):
                cp = _remote(ins[i].at[pl.ds(c * hr, hr)], outs[i].at[p, pl.ds(c * hr, hr)],
                             send_sems.at[i, j], recv_sems.at[i, j], (*chip, c))
                cp.start()
                started.append(cp)
        for i in range(n):
            hr = ins[i].shape[0] // 2
            for j, (px, py) in enumerate(chips):
                half = outs[i].at[2 * px + py, pl.ds(c * hr, hr)]
                _remote(half, half, send_sems.at[i, j], recv_sems.at[i, j], (px, py, c)).wait_recv()
                fwd = _remote(half, half, send_sems.at[i, 3 + j], recv_sems.at[i, 3 + j], sibling)
                fwd.start()
                started.append(fwd)
        for i in range(n):
            hr = ins[i].shape[0] // 2
            for j, (px, py) in enumerate(chips):
                other = outs[i].at[2 * px + py, pl.ds((1 - c) * hr, hr)]
                _remote(other, other, send_sems.at[i, 3 + j], recv_sems.at[i, 3 + j], sibling).wait_recv()
        for cp in started:
            cp.wait_send()
        for cp in own:
            cp.wait()

    return _pcall(
        body, name="gather_weight_shards",
        out_shape=[jax.ShapeDtypeStruct((N_SHARD,) + s.shape, s.dtype) for s in shards],
        in_specs=[ANY] * n, out_specs=[ANY] * n,
        scratch_shapes=[pltpu.SemaphoreType.DMA((n, 6)), pltpu.SemaphoreType.DMA((n, 6)),
                        pltpu.SemaphoreType.DMA((n,))],
    )(*shards)


def exchange_sibling_halves(grads):
    n = len(grads)

    def body(*refs):
        ins, outs = refs[:n], refs[n:2 * n]
        send_sems, recv_sems = refs[2 * n:]
        x, y, c, _ = _place()
        copies = []
        for i in range(n):
            hr = ins[i].shape[1] // 2
            cp = _remote(ins[i].at[:, pl.ds((1 - c) * hr, hr), :], outs[i], send_sems.at[i], recv_sems.at[i],
                         (x, y, 1 - c))
            cp.start()
            copies.append(cp)
        for cp in copies:
            cp.wait()

    return _pcall(
        body, name="exchange_sibling_halves",
        out_shape=[jax.ShapeDtypeStruct((g.shape[0], g.shape[1] // 2, g.shape[2]), g.dtype) for g in grads],
        in_specs=[ANY] * n, out_specs=[ANY] * n,
        scratch_shapes=[pltpu.SemaphoreType.DMA((n,)), pltpu.SemaphoreType.DMA((n,))],
    )(*grads)


def exchange_chip_quarters(parts):
    n = len(parts)

    def body(*refs):
        ins, outs = refs[:n], refs[n:2 * n]
        send_sems, recv_sems, local_sems = refs[2 * n:]
        x, y, c, chips = _place()
        p = 2 * x + y
        sends, own = [], []
        for i in range(n):
            mine = pltpu.make_async_copy(ins[i].at[p], outs[i].at[p], local_sems.at[i])
            mine.start()
            own.append(mine)
            for j, (px, py) in enumerate(chips):
                cp = _remote(ins[i].at[2 * px + py], outs[i].at[p], send_sems.at[i, j], recv_sems.at[i, j], (px, py, c))
                cp.start()
                sends.append(cp)
        for i in range(n):
            for j, (px, py) in enumerate(chips):
                slot = outs[i].at[2 * px + py]
                _remote(slot, slot, send_sems.at[i, j], recv_sems.at[i, j], (px, py, c)).wait_recv()
        for cp in sends:
            cp.wait_send()
        for cp in own:
            cp.wait()

    return _pcall(
        body, name="exchange_chip_quarters",
        out_shape=[jax.ShapeDtypeStruct(g.shape, g.dtype) for g in parts],
        in_specs=[ANY] * n, out_specs=[ANY] * n,
        scratch_shapes=[pltpu.SemaphoreType.DMA((n, 3)), pltpu.SemaphoreType.DMA((n, 3)),
                        pltpu.SemaphoreType.DMA((n,))],
    )(*parts)


def exchange_final_halves(halves):
    n = len(halves)

    def body(*refs):
        ins, outs = refs[:n], refs[n:2 * n]
        send_sems, recv_sems, local_sems = refs[2 * n:]
        x, y, c, _ = _place()
        sends, own = [], []
        for i in range(n):
            hr = ins[i].shape[0]
            mine = pltpu.make_async_copy(ins[i], outs[i].at[pl.ds(c * hr, hr)], local_sems.at[i])
            mine.start()
            own.append(mine)
            cp = _remote(ins[i], outs[i].at[pl.ds(c * hr, hr)], send_sems.at[i], recv_sems.at[i], (x, y, 1 - c))
            cp.start()
            sends.append(cp)
        for i in range(n):
            hr = ins[i].shape[0]
            other = outs[i].at[pl.ds((1 - c) * hr, hr)]
            _remote(other, other, send_sems.at[i], recv_sems.at[i], (x, y, 1 - c)).wait_recv()
        for cp in sends:
            cp.wait_send()
        for cp in own:
            cp.wait()

    return _pcall(
        body, name="exchange_final_halves",
        out_shape=[jax.ShapeDtypeStruct((2 * f.shape[0], f.shape[1]), f.dtype) for f in halves],
        in_specs=[ANY] * n, out_specs=[ANY] * n,
        scratch_shapes=[pltpu.SemaphoreType.DMA((n,)), pltpu.SemaphoreType.DMA((n,)), pltpu.SemaphoreType.DMA((n,))],
    )(*halves)


def _row_tile(rows):
    for tr in (256, 128, 64, 32, 16, 8):
        if rows % tr == 0:
            return tr
    return rows


def add_own_half(g, recv, c, name):
    s, r, cols = g.shape
    hr = r // 2
    tr = _row_tile(hr)
    nb = hr // tr

    def body(c_ref, g_ref, r_ref, o_ref):
        del c_ref
        o_ref[...] = g_ref[...] + r_ref[...]

    return _pcall(
        body, name=name,
        grid_spec=pltpu.PrefetchScalarGridSpec(
            num_scalar_prefetch=1, grid=(s, nb),
            in_specs=[pl.BlockSpec((None, tr, cols), lambda j, i, c_ref: (j, c_ref[0] * nb + i, 0)),
                      pl.BlockSpec((None, tr, cols), lambda j, i, c_ref: (j, i, 0))],
            out_specs=pl.BlockSpec((None, tr, cols), lambda j, i, c_ref: (j, i, 0))),
        out_shape=jax.ShapeDtypeStruct((s, hr, cols), g.dtype),
        compiler_params=_params("parallel", "parallel"),
    )(c.reshape(1).astype(jnp.int32), g, recv)


def add_quarters(b, name):
    s, hr, cols = b.shape
    tr = _row_tile(hr)

    def body(b_ref, o_ref):
        acc = b_ref[0] + b_ref[1]
        for q in range(2, s):
            acc = acc + b_ref[q]
        o_ref[...] = acc

    return _pcall(
        body, name=name, grid=(hr // tr,),
        in_specs=[pl.BlockSpec((s, tr, cols), lambda i: (0, i, 0))],
        out_specs=pl.BlockSpec((tr, cols), lambda i: (i, 0)),
        out_shape=jax.ShapeDtypeStruct((hr, cols), b.dtype),
        compiler_params=_params("parallel"),
    )(b)


def _adamw_math(w, g, m, v):
    m2 = ADAM_B1 * m + (1.0 - ADAM_B1) * g
    v2 = ADAM_B2 * v + (1.0 - ADAM_B2) * (g * g)
    m_hat = m2 / (1.0 - ADAM_B1 ** ADAM_STEP)
    v_hat = v2 / (1.0 - ADAM_B2 ** ADAM_STEP)
    delta = -ADAM_LR * (m_hat / (jnp.sqrt(v_hat) + ADAM_EPS) + ADAM_WD * w)
    return delta, m2, v2


def adamw(w, g, m, v, name):
    r, cols = w.shape
    tr = 128 if r % 128 == 0 else r

    def body(w_ref, g_ref, m_ref, v_ref, d_ref, m2_ref, v2_ref):
        d_ref[...], m2_ref[...], v2_ref[...] = _adamw_math(w_ref[...], g_ref[...], m_ref[...], v_ref[...])

    blk = pl.BlockSpec((tr, cols), lambda i: (i, 0))
    return _pcall(
        body, name=name, grid=(r // tr,),
        in_specs=[blk] * 4, out_specs=[blk] * 3,
        out_shape=[jax.ShapeDtypeStruct((r, cols), F32)] * 3,
        compiler_params=_params("parallel"),
    )(w, g, m, v)


def small_update(gathered, w, m, v):
    nd, r, lanes = gathered.shape

    def body(ga_ref, w_ref, m_ref, v_ref, g_ref, d_ref, m2_ref, v2_ref):
        g = ga_ref[0] + ga_ref[1]
        for dev in range(2, nd):
            g = g + ga_ref[dev]
        g_ref[...] = g
        d_ref[...], m2_ref[...], v2_ref[...] = _adamw_math(w_ref[...], g, m_ref[...], v_ref[...])

    return _pcall(
        body, name="small_update",
        out_shape=[jax.ShapeDtypeStruct((r, lanes), F32)] * 4,
        compiler_params=pltpu.CompilerParams(vmem_limit_bytes=VMEM_LIMIT),
    )(gathered, w, m, v)


def mod_fwd(c16, w_mod, b_mod, name):
    d, ns = w_mod.shape
    tn = 512

    def body(c_ref, w_ref, b_ref, o_ref):
        cv = c_ref[...]
        sc = (cv * _sigmoid(cv)).astype(BF16)
        o_ref[...] = _dot(sc, w_ref[...].astype(BF16)) + b_ref[...]

    return _pcall(
        body, name=name, grid=(ns // tn,),
        in_specs=[pl.BlockSpec((16, d), lambda j: (0, 0)), pl.BlockSpec((d, tn), lambda j: (0, j)),
                  pl.BlockSpec((1, tn), lambda j: (0, j))],
        out_specs=pl.BlockSpec((16, tn), lambda j: (0, j)),
        out_shape=jax.ShapeDtypeStruct((16, ns), F32),
        compiler_params=_params("parallel"),
    )(c16, w_mod, b_mod)


def wmod_update(c16, dmod16, w, m, v, name):
    d, ns = w.shape
    tn = 256

    def body(c_ref, dm_ref, w_ref, m_ref, v_ref, g_ref, d_ref, m2_ref, v2_ref):
        cv = c_ref[...]
        sc = (cv * _sigmoid(cv)).astype(BF16)
        g = _dot_tn(sc, dm_ref[...].astype(BF16))
        g_ref[...] = g
        d_ref[...], m2_ref[...], v2_ref[...] = _adamw_math(w_ref[...], g, m_ref[...], v_ref[...])

    blk = pl.BlockSpec((d, tn), lambda j: (0, j))
    return _pcall(
        body, name=name, grid=(ns // tn,),
        in_specs=[pl.BlockSpec((16, d), lambda j: (0, 0)), pl.BlockSpec((16, tn), lambda j: (0, j)), blk, blk, blk],
        out_specs=[blk] * 4,
        out_shape=[jax.ShapeDtypeStruct((d, ns), F32)] * 4,
        compiler_params=_params("parallel"),
    )(c16, dmod16, w, m, v)


def cctx_partial(dm0, w0, dm1, w1):
    d, ns = w0.shape
    tn = 512

    def body(dm0_ref, w0_ref, dm1_ref, w1_ref, o_ref):
        @pl.when(pl.program_id(0) == 0)
        def _():
            o_ref[...] = jnp.zeros_like(o_ref)

        for dm_ref, w_ref in ((dm0_ref, w0_ref), (dm1_ref, w1_ref)):
            tot = jnp.sum(dm_ref[...], axis=0, keepdims=True)
            lhs = jnp.broadcast_to(tot, (8, tn)).astype(BF16)
            o_ref[...] += _dot_nt(lhs, w_ref[...].astype(BF16))

    dspec = pl.BlockSpec((8, tn), lambda j: (0, j))
    wspec = pl.BlockSpec((d, tn), lambda j: (0, j))
    return _pcall(
        body, name="cctx_partial", grid=(ns // tn,),
        in_specs=[dspec, wspec, dspec, wspec],
        out_specs=pl.BlockSpec((8, d), lambda j: (0, 0)),
        out_shape=jax.ShapeDtypeStruct((8, d), F32),
        compiler_params=_params("arbitrary"),
    )(dm0, w0, dm1, w1)


def cctx_update(parts, c_ctx, m, v):
    d = c_ctx.shape[1]

    def body(p_ref, c_ref, m_ref, v_ref, g_ref, d_ref, m2_ref, v2_ref):
        tot = p_ref[0:1, :] + p_ref[1:2, :]
        tot = tot + p_ref[2:3, :]
        tot = tot + p_ref[3:4, :]
        cv = c_ref[...]
        sg = _sigmoid(cv)
        g = tot * (sg * (1.0 + cv * (1.0 - sg)))
        g_ref[...] = g
        d_ref[...], m2_ref[...], v2_ref[...] = _adamw_math(cv, g, m_ref[...], v_ref[...])

    return _pcall(
        body, name="cctx_update",
        out_shape=[jax.ShapeDtypeStruct((1, d), F32)] * 4,
    )(parts, c_ctx, m, v)


WEIGHT_NAMES = (
    "c_ctx", "l0_norm1_g", "l0_w_mod", "l0_b_mod", "l0_w_in", "l0_q_norm_g", "l0_k_norm_g", "l0_conv_w", "l0_w_out",
    "l0_norm2_g", "l0_w_gate", "l0_w_up", "l0_w_down", "l1_norm1_g", "l1_w_mod", "l1_b_mod", "l1_w_in",
    "l1_q_norm_g", "l1_k_norm_g", "l1_sink", "l1_pool_w", "l1_pool_scale", "l1_w_out", "l1_norm2_g", "l1_w_gate",
    "l1_w_up", "l1_w_down", "final_norm_g")
BIG_NAMES = ("w_in", "w_out", "w_gate", "w_up", "w_down")
SMALL_SLOTS = tuple(
    [(f"l{l}_{nm}", par and f"l{l}_{nm}") for l in (0, 1)
     for nm, par in (("dmod_lat", False), ("dmod_ctx", False), ("b_mod", True), ("norm1_g", True), ("norm2_g", True),
                     ("q_norm_g", True), ("k_norm_g", True))]
    + [("l0_conv_w", None), ("l1_sink", "l1_sink"), ("l1_pool_w", "l1_pool_w"), ("l1_pool_scale", "l1_pool_scale"),
       ("final_norm_g", "final_norm_g")])


def _pack(values, sizes):
    parts = []
    for (name, _), size in zip(SMALL_SLOTS, sizes):
        v = values.get(name)
        padded = -(-size // 128) * 128
        v = jnp.zeros((padded,), F32) if v is None else jnp.pad(v.reshape(-1).astype(F32), (0, padded - size))
        parts.append(v)
    total = sum(p.shape[0] for p in parts)
    parts.append(jnp.zeros((-(-total // 1024) * 1024 - total,), F32))
    return jnp.concatenate(parts).reshape(-1, 128)


def _offsets(sizes):
    offs, o = {}, 0
    for (name, _), size in zip(SMALL_SLOTS, sizes):
        offs[name] = (o, size)
        o += -(-size // 128) * 128
    return offs


def kernel(x, c, ctx, c_ctx, l0_norm1_g, l0_w_mod, l0_b_mod, l0_w_in, l0_q_norm_g, l0_k_norm_g, l0_conv_w, l0_w_out, l0_norm2_g, l0_w_gate, l0_w_up, l0_w_down, l1_norm1_g, l1_w_mod, l1_b_mod, l1_w_in, l1_q_norm_g, l1_k_norm_g, l1_sink, l1_pool_w, l1_pool_scale, l1_w_out, l1_norm2_g, l1_w_gate, l1_w_up, l1_w_down, final_norm_g, loss_target, m_c_ctx, m_l0_norm1_g, m_l0_w_mod, m_l0_b_mod, m_l0_w_in, m_l0_q_norm_g, m_l0_k_norm_g, m_l0_conv_w, m_l0_w_out, m_l0_norm2_g, m_l0_w_gate, m_l0_w_up, m_l0_w_down, m_l1_norm1_g, m_l1_w_mod, m_l1_b_mod, m_l1_w_in, m_l1_q_norm_g, m_l1_k_norm_g, m_l1_sink, m_l1_pool_w, m_l1_pool_scale, m_l1_w_out, m_l1_norm2_g, m_l1_w_gate, m_l1_w_up, m_l1_w_down, m_final_norm_g, v_c_ctx, v_l0_norm1_g, v_l0_w_mod, v_l0_b_mod, v_l0_w_in, v_l0_q_norm_g, v_l0_k_norm_g, v_l0_conv_w, v_l0_w_out, v_l0_norm2_g, v_l0_w_gate, v_l0_w_up, v_l0_w_down, v_l1_norm1_g, v_l1_w_mod, v_l1_b_mod, v_l1_w_in, v_l1_q_norm_g, v_l1_k_norm_g, v_l1_sink, v_l1_pool_w, v_l1_pool_scale, v_l1_w_out, v_l1_norm2_g, v_l1_w_gate, v_l1_w_up, v_l1_w_down, v_final_norm_g):
    a = dict(locals())
    xi, yi, ci = lax.axis_index("x"), lax.axis_index("y"), lax.axis_index("c")
    p = 2 * xi + yi
    me = 4 * xi + 2 * yi + ci
    d = x.shape[-1]
    conv_cols = l0_conv_w.shape[1]

    row0 = jnp.concatenate([c, jnp.pad(l0_conv_w, ((0, 0), (0, d - conv_cols))), jnp.zeros((4, d), F32)], axis=0)
    gathered = all_gather8(row0, "gather_cond").reshape(N_DEV, 8, d)
    c_all = gathered[:, 0]
    conv_w = gathered[0::2, 1:4, :conv_cols].transpose(1, 0, 2).reshape(3, N_SHARD * conv_cols)
    c16_fwd = jnp.concatenate([c_all, c_ctx[None], jnp.zeros((7, d), F32)], axis=0)
    c16_bwd = jnp.concatenate([c_all, jnp.broadcast_to(c_ctx[None], (8, d))], axis=0)

    ns_mod = l0_w_mod.shape[1]
    mod_parts = [mod_fwd(c16_fwd, a[f"l{l}_w_mod"], lax.dynamic_slice(a[f"l{l}_b_mod"], (p * ns_mod,), (ns_mod,))[None],
                         f"l{l}_mod_fwd") for l in (0, 1)]
    modg = all_gather8(jnp.concatenate(mod_parts, axis=0), "gather_mod").reshape(N_DEV, 2, 16, ns_mod)[0::2]
    mod_full = modg.transpose(1, 2, 0, 3).reshape(2, 16, N_SHARD * ns_mod)
    mods = []
    for l in (0, 1):
        lat = lax.dynamic_index_in_dim(mod_full[l], me, axis=0, keepdims=False)
        cx = mod_full[l, 8]
        mods.append({nm: jnp.stack([cx[j * d:(j + 1) * d], lat[j * d:(j + 1) * d]])[:, None, :]
                     for j, nm in enumerate(MOD_NAMES)})

    stacks = gather_weight_shards([a[f"l{l}_{nm}"].astype(BF16) for l in (0, 1) for nm in BIG_NAMES])
    layers = []
    for l in (0, 1):
        lw = {nm: stacks[l * len(BIG_NAMES) + k] for k, nm in enumerate(BIG_NAMES)}
        lw.update(norm1_g=a[f"l{l}_norm1_g"][None], norm2_g=a[f"l{l}_norm2_g"][None],
                  q_g=a[f"l{l}_q_norm_g"][None], k_g=a[f"l{l}_k_norm_g"][None])
        layers.append(lw)
    layers[0].update(conv_w=conv_w, sink=jnp.zeros((Q_HEADS,), F32))
    layers[1].update(sink=l1_sink, pool_w=l1_pool_w, pool_scale=l1_pool_scale[None])

    loss_tile, dx, bigs, smalls, dmods, dfg = _local_step(x[0], ctx[0], loss_target[0], layers, mods,
                                                          final_norm_g[None])

    partial = {"l0_conv_w": smalls[0]["conv_w"], "l1_sink": smalls[1]["sink"], "l1_pool_w": smalls[1]["pool_w"],
               "l1_pool_scale": smalls[1]["pool_scale"], "final_norm_g": dfg}
    for l in (0, 1):
        partial.update({f"l{l}_dmod_lat": dmods[l][1], f"l{l}_dmod_ctx": dmods[l][0],
                        f"l{l}_b_mod": dmods[l][0] + dmods[l][1], f"l{l}_norm1_g": smalls[l]["norm1_g"],
                        f"l{l}_norm2_g": smalls[l]["norm2_g"], f"l{l}_q_norm_g": smalls[l]["q_g"],
                        f"l{l}_k_norm_g": smalls[l]["k_g"]})
    sizes = [int(np.prod(partial[name].shape)) for name, _ in SMALL_SLOTS]
    offs = _offsets(sizes)
    gpack = _pack(partial, sizes)
    rows = gpack.shape[0]
    small_all = all_gather8(gpack, "gather_small").reshape(N_DEV, rows, 128)
    packs = [_pack({name: a[pre + par] for name, par in SMALL_SLOTS if par}, sizes) for pre in ("", "m_", "v_")]
    small_out = [t.reshape(-1) for t in small_update(small_all, *packs)]
    flat_all = small_all.reshape(N_DEV, rows * 128)

    def slot(flat, name, shape):
        o, size = offs[name]
        return flat[o:o + size].reshape(shape)

    results = {}
    for name, par in SMALL_SLOTS:
        if par:
            results[par] = tuple(slot(t, name, a[par].shape) for t in small_out)

    ctx_rows = []
    for l in (0, 1):
        o_lat, o_ctx = offs[f"l{l}_dmod_lat"][0], offs[f"l{l}_dmod_ctx"][0]
        lat = lax.dynamic_slice(flat_all, (0, o_lat + p * ns_mod), (N_DEV, ns_mod))
        cxr = lax.dynamic_slice(flat_all, (0, o_ctx + p * ns_mod), (N_DEV, ns_mod))
        ctx_rows.append(cxr)
        results[f"l{l}_w_mod"] = tuple(wmod_update(c16_bwd, jnp.concatenate([lat, cxr], axis=0), a[f"l{l}_w_mod"],
                                                   a[f"m_l{l}_w_mod"], a[f"v_l{l}_w_mod"], f"l{l}_w_mod_update"))
    part = cctx_partial(ctx_rows[0], l0_w_mod, ctx_rows[1], l1_w_mod)
    parts4 = all_gather8(part, "gather_cctx").reshape(N_DEV, 8, d)[0::2, 0]
    results["c_ctx"] = tuple(t[0] for t in cctx_update(parts4, c_ctx[None], m_c_ctx[None], v_c_ctx[None]))

    gconv = lax.dynamic_slice(slot(small_out[0], "l0_conv_w", (3, N_SHARD * conv_cols)), (0, p * conv_cols),
                              (3, conv_cols))
    results["l0_conv_w"] = (gconv,) + tuple(adamw(l0_conv_w, gconv, m_l0_conv_w, v_l0_conv_w, "l0_conv_w_adamw"))

    keys = [f"l{l}_{nm}" for l in (0, 1) for nm in BIG_NAMES]
    grads = [bigs[l][nm] for l in (0, 1) for nm in BIG_NAMES]
    recv = exchange_sibling_halves(grads)
    chip_sums = [add_own_half(g, r, ci, k + "_add_halves") for k, g, r in zip(keys, grads, recv)]
    quarters = exchange_chip_quarters(chip_sums)
    halves = [add_quarters(q, k + "_add_quarters") for k, q in zip(keys, quarters)]
    fulls = exchange_final_halves(halves)
    for k, g in zip(keys, fulls):
        results[k] = (g,) + tuple(adamw(a[k], g, a["m_" + k], a["v_" + k], k + "_adamw"))

    loss = lax.psum(loss_tile[0, 0], ("x", "y", "c"))
    out = [loss, dx[None]]
    for j in range(4):
        out += [results[k][j] for k in WEIGHT_NAMES]
    return tuple(out)
```

```python
import numpy as np
import jax
import jax.numpy as jnp
from jax import lax
from jax.experimental import pallas as pl
from jax.experimental.pallas import tpu as pltpu

F32 = jnp.float32
BF16 = jnp.bfloat16
MESH = pl.DeviceIdType.MESH

HEAD_DIM = 128
Q_HEADS = 12
KV_HEADS = 4
GROUPS = Q_HEADS // KV_HEADS
ATT_WIDTH = Q_HEADS * HEAD_DIM
KV_WIDTH = KV_HEADS * HEAD_DIM
QKV_WIDTH = ATT_WIDTH + 2 * KV_WIDTH
AUX_WIDTH = 512
AUX_GROUPS = 4
POOL_WINDOWS = (2, 4, 8, 16)
WINDOW = 128
GRID_W = 64
ROPE_THETA = 10000.0
N_MOD = 6
EPS = 1e-6
NEG_INF = -1e30
ATT_SCALE = HEAD_DIM ** -0.5
N_SHARD = 4
N_DEV = 8

ADAM_LR = 0.001
ADAM_B1 = 0.9
ADAM_B2 = 0.999
ADAM_EPS = 1e-08
ADAM_WD = 0.01
ADAM_STEP = 10

TM = 256
BAND = TM + 2 * WINDOW
VMEM_LIMIT = 56 * 1024 * 1024

NT_DIMS = (((1,), (1,)), ((), ()))
TN_DIMS = (((0,), (0,)), ((), ()))


def _pcall(body, **kw):
    return pl.pallas_call(body, **kw)


def _params(*sem):
    return pltpu.CompilerParams(dimension_semantics=sem, vmem_limit_bytes=VMEM_LIMIT)


def _sel(i):
    return jnp.minimum(i, 1)


def _sel_spec(d):
    return pl.BlockSpec((None, 1, d), lambda i: (_sel(i), 0, 0))


def _dot(a, b):
    return jnp.dot(a, b, preferred_element_type=F32)


def _dot_nt(a, b):
    return lax.dot_general(a, b, NT_DIMS, preferred_element_type=F32)


def _dot_tn(a, b):
    return lax.dot_general(a, b, TN_DIMS, preferred_element_type=F32)


def _sigmoid(x):
    return 1.0 / (1.0 + jnp.exp(-x))


def norm_mod_fwd(h, g, shift, scale, name):
    t, d = h.shape

    def body(h_ref, g_ref, sh_ref, sc_ref, o_ref):
        x = h_ref[...]
        r = lax.rsqrt(jnp.mean(x * x, axis=-1, keepdims=True) + EPS)
        y = x * r * g_ref[...]
        o_ref[...] = (y * (1.0 + sc_ref[...]) + sh_ref[...]).astype(BF16)

    return _pcall(
        body, name=name, grid=(t // TM,),
        in_specs=[pl.BlockSpec((TM, d), lambda i: (i, 0)), pl.BlockSpec((1, d), lambda i: (0, 0)),
                  _sel_spec(d), _sel_spec(d)],
        out_specs=pl.BlockSpec((TM, d), lambda i: (i, 0)),
        out_shape=jax.ShapeDtypeStruct((t, d), BF16),
        compiler_params=_params("parallel"),
    )(h, g, shift, scale)


def norm_mod_bwd(dy, h, g, scale, dres, name):
    t, d = h.shape

    def body(dy_ref, h_ref, g_ref, sc_ref, dres_ref, dh_ref, dsh_ref, dsc_ref, dg_ref):
        i = pl.program_id(0)

        @pl.when(i == 0)
        def _():
            dsh_ref[...] = jnp.zeros_like(dsh_ref)
            dsc_ref[...] = jnp.zeros_like(dsc_ref)
            dg_ref[...] = jnp.zeros_like(dg_ref)

        x = h_ref[...]
        gv = g_ref[...]
        r = lax.rsqrt(jnp.mean(x * x, axis=-1, keepdims=True) + EPS)
        xhat = x * r
        dyv = dy_ref[...]
        s = _sel(i)
        dsh_ref[pl.ds(s, 1)] += jnp.sum(dyv, axis=0, keepdims=True)[None]
        dsc_ref[pl.ds(s, 1)] += jnp.sum(dyv * xhat * gv, axis=0, keepdims=True)[None]
        tt = dyv * (1.0 + sc_ref[...])
        dg_ref[...] += jnp.sum(tt * xhat, axis=0, keepdims=True)
        dxhat = tt * gv
        dx = r * (dxhat - xhat * jnp.mean(dxhat * xhat, axis=-1, keepdims=True))
        dh_ref[...] = dx + dres_ref[...]

    row = pl.BlockSpec((TM, d), lambda i: (i, 0))
    acc2 = pl.BlockSpec((2, 1, d), lambda i: (0, 0, 0))
    return _pcall(
        body, name=name, grid=(t // TM,),
        in_specs=[row, row, pl.BlockSpec((1, d), lambda i: (0, 0)), _sel_spec(d), row],
        out_specs=[row, acc2, acc2, pl.BlockSpec((1, d), lambda i: (0, 0))],
        out_shape=[jax.ShapeDtypeStruct((t, d), F32), jax.ShapeDtypeStruct((2, 1, d), F32),
                   jax.ShapeDtypeStruct((2, 1, d), F32), jax.ShapeDtypeStruct((1, d), F32)],
        compiler_params=_params("arbitrary"),
    )(dy, h, g, scale, dres)


def gate_bwd(dh, f, gate, name):
    t, d = dh.shape

    def body(dh_ref, f_ref, gt_ref, df_ref, dgt_ref):
        i = pl.program_id(0)

        @pl.when(i == 0)
        def _():
            dgt_ref[...] = jnp.zeros_like(dgt_ref)

        dhv = dh_ref[...]
        df_ref[...] = (dhv * gt_ref[...]).astype(BF16)
        dgt_ref[pl.ds(_sel(i), 1)] += jnp.sum(dhv * f_ref[...], axis=0, keepdims=True)[None]

    row = pl.BlockSpec((TM, d), lambda i: (i, 0))
    return _pcall(
        body, name=name, grid=(t // TM,),
        in_specs=[row, row, _sel_spec(d)],
        out_specs=[row, pl.BlockSpec((2, 1, d), lambda i: (0, 0, 0))],
        out_shape=[jax.ShapeDtypeStruct((t, d), BF16), jax.ShapeDtypeStruct((2, 1, d), F32)],
        compiler_params=_params("arbitrary"),
    )(dh, f, gate)


def final_loss(h, fg, target):
    t, d = h.shape

    def body(h_ref, g_ref, tg_ref, loss_ref, dh_ref, dg_ref):
        i = pl.program_id(0)

        @pl.when(i == 0)
        def _():
            loss_ref[...] = jnp.zeros_like(loss_ref)
            dg_ref[...] = jnp.zeros_like(dg_ref)
            dh_ref[...] = jnp.zeros_like(dh_ref)

        @pl.when(i > 0)
        def _():
            x = h_ref[...]
            gv = g_ref[...]
            r = lax.rsqrt(jnp.mean(x * x, axis=-1, keepdims=True) + EPS)
            xhat = x * r
            diff = xhat * gv - tg_ref[...]
            loss_ref[...] += 0.5 * jnp.sum(jnp.mean(diff * diff, axis=-1, keepdims=True), axis=0, keepdims=True)
            dout = diff * (1.0 / d)
            dg_ref[...] += jnp.sum(dout * xhat, axis=0, keepdims=True)
            dxhat = dout * gv
            dh_ref[...] = r * (dxhat - xhat * jnp.mean(dxhat * xhat, axis=-1, keepdims=True))

    row = pl.BlockSpec((TM, d), lambda i: (i, 0))
    return _pcall(
        body, name="final_loss", grid=(t // TM,),
        in_specs=[row, pl.BlockSpec((1, d), lambda i: (0, 0)),
                  pl.BlockSpec((TM, d), lambda i: (jnp.maximum(i - 1, 0), 0))],
        out_specs=[pl.BlockSpec((8, 128), lambda i: (0, 0)), row, pl.BlockSpec((1, d), lambda i: (0, 0))],
        out_shape=[jax.ShapeDtypeStruct((8, 128), F32), jax.ShapeDtypeStruct((t, d), F32),
                   jax.ShapeDtypeStruct((1, d), F32)],
        compiler_params=_params("arbitrary"),
    )(h, fg, target)


def _w_nn_spec(w, k):
    if w.shape[1] == k:
        ns = w.shape[2]
        return w, pl.BlockSpec((None, k, ns), lambda j, i: (j, 0, 0)), w.shape[0], ns
    w2 = w.reshape(w.shape[0] * w.shape[1], w.shape[2])
    tn = 1024 if w2.shape[1] % 1024 == 0 else w2.shape[1]
    return w2, pl.BlockSpec((k, tn), lambda j, i: (0, j)), w2.shape[1] // tn, tn


def mm_nn(a, w, name, out_dtype=F32):
    t, k = a.shape
    w, wspec, nb, tn = _w_nn_spec(w, k)

    def body(a_ref, w_ref, o_ref):
        o_ref[...] = _dot(a_ref[...], w_ref[...]).astype(o_ref.dtype)

    return _pcall(
        body, name=name, grid=(nb, t // TM),
        in_specs=[pl.BlockSpec((TM, k), lambda j, i: (i, 0)), wspec],
        out_specs=pl.BlockSpec((TM, tn), lambda j, i: (i, j)),
        out_shape=jax.ShapeDtypeStruct((t, nb * tn), out_dtype),
        compiler_params=_params("parallel", "parallel"),
    )(a, w)


def mm_nn_residual(a, w, h, gate, name):
    t, k = a.shape
    w, wspec, nb, tn = _w_nn_spec(w, k)

    def body(a_ref, w_ref, h_ref, gt_ref, y_ref, o_ref):
        y = _dot(a_ref[...], w_ref[...])
        y_ref[...] = y
        o_ref[...] = h_ref[...] + gt_ref[...] * y

    out = pl.BlockSpec((TM, tn), lambda j, i: (i, j))
    return _pcall(
        body, name=name, grid=(nb, t // TM),
        in_specs=[pl.BlockSpec((TM, k), lambda j, i: (i, 0)), wspec, out,
                  pl.BlockSpec((None, 1, tn), lambda j, i: (_sel(i), 0, j))],
        out_specs=[out, out],
        out_shape=[jax.ShapeDtypeStruct((t, nb * tn), F32)] * 2,
        compiler_params=_params("parallel", "parallel"),
    )(a, w, h, gate)


def mm_swiglu(a, wg, wu, name):
    t, k = a.shape
    s, _, ns = wg.shape

    def body(a_ref, wg_ref, wu_ref, g_ref, u_ref, act_ref):
        av = a_ref[...]
        g = _dot(av, wg_ref[...])
        u = _dot(av, wu_ref[...])
        g_ref[...] = g
        u_ref[...] = u
        act_ref[...] = (g * _sigmoid(g) * u).astype(BF16)

    wspec = pl.BlockSpec((None, k, ns), lambda j, i: (j, 0, 0))
    out = pl.BlockSpec((TM, ns), lambda j, i: (i, j))
    return _pcall(
        body, name=name, grid=(s, t // TM),
        in_specs=[pl.BlockSpec((TM, k), lambda j, i: (i, 0)), wspec, wspec],
        out_specs=[out, out, out],
        out_shape=[jax.ShapeDtypeStruct((t, s * ns), F32), jax.ShapeDtypeStruct((t, s * ns), F32),
                   jax.ShapeDtypeStruct((t, s * ns), BF16)],
        compiler_params=_params("parallel", "parallel"),
    )(a, wg, wu)


def mm_nt_cols(dy, w, name, add=None):
    t, n = dy.shape
    s, k, ns = w.shape
    tk = 512

    def body(*refs):
        if add is None:
            dy_ref, w_ref, o_ref = refs
        else:
            dy_ref, w_ref, add_ref, o_ref = refs
        acc = _dot_nt(dy_ref[:, 0:ns], w_ref[0])
        for sh in range(1, s):
            acc += _dot_nt(dy_ref[:, sh * ns:(sh + 1) * ns], w_ref[sh])
        if add is not None:
            acc += add_ref[...]
        o_ref[...] = acc

    out = pl.BlockSpec((TM, tk), lambda j, i: (i, j))
    in_specs = [pl.BlockSpec((TM, n), lambda j, i: (i, 0)), pl.BlockSpec((s, tk, ns), lambda j, i: (0, j, 0))]
    args = [dy, w]
    if add is not None:
        in_specs.append(out)
        args.append(add)
    return _pcall(
        body, name=name, grid=(k // tk, t // TM),
        in_specs=in_specs, out_specs=out,
        out_shape=jax.ShapeDtypeStruct((t, k), F32),
        compiler_params=_params("parallel", "parallel"),
    )(*args)


def mm_nt_rows(dy, w, name):
    t, n = dy.shape
    s, ks, _ = w.shape

    def body(dy_ref, w_ref, o_ref):
        o_ref[...] = _dot_nt(dy_ref[...], w_ref[...])

    return _pcall(
        body, name=name, grid=(s, t // TM),
        in_specs=[pl.BlockSpec((TM, n), lambda j, i: (i, 0)), pl.BlockSpec((None, ks, n), lambda j, i: (j, 0, 0))],
        out_specs=pl.BlockSpec((TM, ks), lambda j, i: (i, j)),
        out_shape=jax.ShapeDtypeStruct((t, s * ks), F32),
        compiler_params=_params("parallel", "parallel"),
    )(dy, w)


def mm_nt_swiglu_bwd(df, wd, g, u, name):
    t, n = df.shape
    s, ks, _ = wd.shape

    def body(df_ref, w_ref, g_ref, u_ref, dg_ref, du_ref):
        da = _dot_nt(df_ref[...], w_ref[...])
        gv = g_ref[...]
        sg = _sigmoid(gv)
        silu = gv * sg
        dg_ref[...] = (da * u_ref[...] * (sg * (1.0 + gv * (1.0 - sg)))).astype(BF16)
        du_ref[...] = (da * silu).astype(BF16)

    blk = pl.BlockSpec((TM, ks), lambda j, i: (i, j))
    return _pcall(
        body, name=name, grid=(s, t // TM),
        in_specs=[pl.BlockSpec((TM, n), lambda j, i: (i, 0)), pl.BlockSpec((None, ks, n), lambda j, i: (j, 0, 0)),
                  blk, blk],
        out_specs=[blk, blk],
        out_shape=[jax.ShapeDtypeStruct((t, s * ks), BF16)] * 2,
        compiler_params=_params("parallel", "parallel"),
    )(df, wd, g, u)


def mm_tn(x, dy, name, col_sharded):
    t, k = x.shape
    n = dy.shape[1]
    tk = 512
    mc = t // 2 if (t // 2) % 128 == 0 else t
    if col_sharded:
        tn = n // N_SHARD
        out_shape = jax.ShapeDtypeStruct((N_SHARD, k, tn), F32)
        out_spec = pl.BlockSpec((None, tk, tn), lambda j, kb, m: (j, kb, 0))
    else:
        tn = 1024
        out_shape = jax.ShapeDtypeStruct((k, n), F32)
        out_spec = pl.BlockSpec((tk, tn), lambda j, kb, m: (kb, j))

    def body(x_ref, dy_ref, o_ref):
        m = pl.program_id(2)
        acc = _dot_tn(x_ref[...], dy_ref[...])

        @pl.when(m == 0)
        def _():
            o_ref[...] = acc

        @pl.when(m > 0)
        def _():
            o_ref[...] += acc

    out = _pcall(
        body, name=name, grid=(n // tn, k // tk, t // mc),
        in_specs=[pl.BlockSpec((mc, tk), lambda j, kb, m: (m, kb)), pl.BlockSpec((mc, tn), lambda j, kb, m: (m, j))],
        out_specs=out_spec, out_shape=out_shape,
        compiler_params=_params("parallel", "parallel", "arbitrary"),
    )(x, dy)
    return out if col_sharded else out.reshape(N_SHARD, k // N_SHARD, n)


def _swap32(y):
    right = pltpu.roll(y, 32, 1)
    left = pltpu.roll(y, 96, 1)
    lane = lax.broadcasted_iota(jnp.int32, y.shape, 1)
    return jnp.where((lane // 32) % 2 == 0, left, right)


def _rope_tables(t, ctx_len):
    s = t - ctx_len
    pos = np.arange(s)
    row = (pos // GRID_W).astype(np.float32)
    col = (pos % GRID_W).astype(np.float32)
    half = HEAD_DIM // 2
    inv = np.power(np.float32(ROPE_THETA), -np.arange(0, half, 2, dtype=np.float32) / np.float32(half))
    ar = row[:, None] * inv
    ac = col[:, None] * inv
    cos = np.concatenate([np.cos(ar), np.cos(ar), np.cos(ac), np.cos(ac)], axis=1)
    sin = np.concatenate([-np.sin(ar), np.sin(ar), -np.sin(ac), np.sin(ac)], axis=1)
    cos = np.concatenate([np.ones((ctx_len, HEAD_DIM), np.float32), cos.astype(np.float32)], axis=0)
    sin = np.concatenate([np.zeros((ctx_len, HEAD_DIM), np.float32), sin.astype(np.float32)], axis=0)
    return jnp.asarray(cos, F32), jnp.asarray(sin, F32)


def qk_prep_fwd(proj, qg, kg, cos, sin, name):
    t = proj.shape[0]

    def body(p_ref, qg_ref, kg_ref, c_ref, s_ref, q_ref, k_ref, v_ref):
        cv = c_ref[...]
        sv = s_ref[...]
        for hd in range(Q_HEADS + KV_HEADS):
            x = p_ref[:, hd * HEAD_DIM:(hd + 1) * HEAD_DIM]
            gv = qg_ref[...] if hd < Q_HEADS else kg_ref[...]
            y = x * lax.rsqrt(jnp.mean(x * x, axis=-1, keepdims=True) + EPS) * gv
            out = (y * cv + _swap32(y) * sv).astype(BF16)
            if hd < Q_HEADS:
                q_ref[:, hd * HEAD_DIM:(hd + 1) * HEAD_DIM] = out
            else:
                k_ref[:, (hd - Q_HEADS) * HEAD_DIM:(hd - Q_HEADS + 1) * HEAD_DIM] = out
        v_ref[...] = p_ref[:, ATT_WIDTH + KV_WIDTH:QKV_WIDTH].astype(BF16)

    vec = pl.BlockSpec((1, HEAD_DIM), lambda i: (0, 0))
    tab = pl.BlockSpec((TM, HEAD_DIM), lambda i: (i, 0))
    return _pcall(
        body, name=name, grid=(t // TM,),
        in_specs=[pl.BlockSpec((TM, QKV_WIDTH), lambda i: (i, 0)), vec, vec, tab, tab],
        out_specs=[pl.BlockSpec((TM, ATT_WIDTH), lambda i: (i, 0)), pl.BlockSpec((TM, KV_WIDTH), lambda i: (i, 0)),
                   pl.BlockSpec((TM, KV_WIDTH), lambda i: (i, 0))],
        out_shape=[jax.ShapeDtypeStruct((t, ATT_WIDTH), BF16), jax.ShapeDtypeStruct((t, KV_WIDTH), BF16),
                   jax.ShapeDtypeStruct((t, KV_WIDTH), BF16)],
        compiler_params=_params("parallel"),
    )(proj, qg, kg, cos, sin)


def qk_prep_bwd(dq, dk, dv, daux, proj, qg, kg, cos, sin, name):
    t, n_in = proj.shape
    n_aux = daux.shape[0]

    def body(dq_ref, dk_ref, dv_ref, da_ref, p_ref, qg_ref, kg_ref, c_ref, s_ref, o_ref, dqg_ref, dkg_ref):
        @pl.when(pl.program_id(0) == 0)
        def _():
            dqg_ref[...] = jnp.zeros_like(dqg_ref)
            dkg_ref[...] = jnp.zeros_like(dkg_ref)

        cv = c_ref[...]
        sv = s_ref[...]
        for hd in range(Q_HEADS + KV_HEADS):
            cols = slice(hd * HEAD_DIM, (hd + 1) * HEAD_DIM)
            x = p_ref[:, cols]
            if hd < Q_HEADS:
                gv, dyr, dg_ref = qg_ref[...], dq_ref[:, cols], dqg_ref
            else:
                kc = slice((hd - Q_HEADS) * HEAD_DIM, (hd - Q_HEADS + 1) * HEAD_DIM)
                gv, dyr, dg_ref = kg_ref[...], dk_ref[:, kc], dkg_ref
            r = lax.rsqrt(jnp.mean(x * x, axis=-1, keepdims=True) + EPS)
            xhat = x * r
            dy = dyr * cv + _swap32(dyr * sv)
            dg_ref[...] += jnp.sum(dy * xhat, axis=0, keepdims=True)
            dxhat = dy * gv
            o_ref[:, cols] = (r * (dxhat - xhat * jnp.mean(dxhat * xhat, axis=-1, keepdims=True))).astype(BF16)
        o_ref[:, ATT_WIDTH + KV_WIDTH:QKV_WIDTH] = dv_ref[...].astype(BF16)
        for a in range(n_aux):
            o_ref[:, QKV_WIDTH + a * AUX_WIDTH:QKV_WIDTH + (a + 1) * AUX_WIDTH] = da_ref[a]

    vec = pl.BlockSpec((1, HEAD_DIM), lambda i: (0, 0))
    tab = pl.BlockSpec((TM, HEAD_DIM), lambda i: (i, 0))
    return _pcall(
        body, name=name, grid=(t // TM,),
        in_specs=[pl.BlockSpec((TM, ATT_WIDTH), lambda i: (i, 0)), pl.BlockSpec((TM, KV_WIDTH), lambda i: (i, 0)),
                  pl.BlockSpec((TM, KV_WIDTH), lambda i: (i, 0)),
                  pl.BlockSpec((n_aux, TM, AUX_WIDTH), lambda i: (0, i, 0)),
                  pl.BlockSpec((TM, QKV_WIDTH), lambda i: (i, 0)), vec, vec, tab, tab],
        out_specs=[pl.BlockSpec((TM, n_in), lambda i: (i, 0)), vec, vec],
        out_shape=[jax.ShapeDtypeStruct((t, n_in), BF16), jax.ShapeDtypeStruct((1, HEAD_DIM), F32),
                   jax.ShapeDtypeStruct((1, HEAD_DIM), F32)],
        compiler_params=_params("arbitrary"),
    )(dq, dk, dv, daux, proj, qg, kg, cos, sin)


def _band_start(i, t):
    return pl.multiple_of(jnp.clip(i * TM - WINDOW, 0, t - BAND), WINDOW)


def _band_mask(i, start, ctx_len):
    shape = (TM, ctx_len + BAND)
    col = lax.broadcasted_iota(jnp.int32, shape, 1)
    qrow = i * TM + lax.broadcasted_iota(jnp.int32, shape, 0)
    krow = start + col - ctx_len
    band_ok = (krow >= ctx_len) & (jnp.abs(krow - qrow) <= WINDOW)
    return (col < ctx_len) | band_ok


def attention_fwd(q, k, v, sink, ctx_len, windowed, name):
    t = q.shape[0]
    d_model = ATT_WIDTH + AUX_WIDTH
    gw = GROUPS * HEAD_DIM

    def one_head(qh, kk, vv, mask, sink_val):
        s = _dot_nt(qh, kk) * ATT_SCALE
        if mask is not None:
            s = jnp.where(mask, s, NEG_INF)
        m = jnp.max(s, axis=-1, keepdims=True)
        if sink_val is not None:
            m = jnp.maximum(m, sink_val)
        p = jnp.exp(s - m)
        l = jnp.sum(p, axis=-1, keepdims=True)
        if sink_val is not None:
            l = l + jnp.exp(sink_val - m)
        o = _dot(p.astype(BF16), vv) / l
        return o, m + jnp.log(l)

    def body(sink_ref, q_ref, k_ref, v_ref, o_ref, lse_ref):
        kv = pl.program_id(0)
        i = pl.program_id(1)

        def run(kk, vv, mask):
            for g in range(GROUPS):
                sink_val = sink_ref[kv * GROUPS + g] if windowed else None
                o, lse = one_head(q_ref[:, g * HEAD_DIM:(g + 1) * HEAD_DIM], kk, vv, mask, sink_val)
                o_ref[:, g * HEAD_DIM:(g + 1) * HEAD_DIM] = o.astype(BF16)
                lse_ref[g] = lse

        @pl.when(i == 0)
        def _():
            if windowed:
                o_ref[...] = jnp.zeros_like(o_ref)
                lse_ref[...] = jnp.zeros_like(lse_ref)
            else:
                run(k_ref[0:ctx_len], v_ref[0:ctx_len], None)

        @pl.when(i > 0)
        def _():
            if windowed:
                start = _band_start(i, t)
                kk = jnp.concatenate([k_ref[0:ctx_len], k_ref[pl.ds(start, BAND)]], axis=0)
                vv = jnp.concatenate([v_ref[0:ctx_len], v_ref[pl.ds(start, BAND)]], axis=0)
                run(kk, vv, _band_mask(i, start, ctx_len))
            else:
                run(k_ref[...], v_ref[...], None)

    kvspec = pl.BlockSpec((t, HEAD_DIM), lambda kv, i: (0, kv))
    return _pcall(
        body, name=name, grid=(KV_HEADS, t // TM),
        in_specs=[pl.BlockSpec(memory_space=pltpu.SMEM), pl.BlockSpec((TM, gw), lambda kv, i: (i, kv)), kvspec, kvspec],
        out_specs=[pl.BlockSpec((TM, gw), lambda kv, i: (i, kv)),
                   pl.BlockSpec((GROUPS, TM, 1), lambda kv, i: (kv, i, 0))],
        out_shape=[jax.ShapeDtypeStruct((t, d_model), BF16), jax.ShapeDtypeStruct((Q_HEADS, t, 1), F32)],
        compiler_params=_params("parallel", "parallel"),
    )(sink, q, k, v)


def attention_bwd(q, k, v, cat, lse, dcat, sink, ctx_len, windowed, name):
    t = q.shape[0]
    gw = GROUPS * HEAD_DIM

    def body(sink_ref, q_ref, k_ref, v_ref, o_ref, lse_ref, do_ref, dq_ref, dk_ref, dv_ref, dsink_ref):
        kv = pl.program_id(0)
        i = pl.program_id(1)

        @pl.when(i == 0)
        def _():
            dk_ref[...] = jnp.zeros_like(dk_ref)
            dv_ref[...] = jnp.zeros_like(dv_ref)
            dsink_ref[...] = jnp.zeros_like(dsink_ref)

        def run(kk, vv, mask, accumulate):
            for g in range(GROUPS):
                cols = slice(g * HEAD_DIM, (g + 1) * HEAD_DIM)
                qh = q_ref[:, cols]
                doh = do_ref[:, cols]
                delta = jnp.sum(doh * o_ref[:, cols].astype(F32), axis=-1, keepdims=True)
                lse_g = lse_ref[g]
                s = _dot_nt(qh, kk) * ATT_SCALE
                if mask is not None:
                    s = jnp.where(mask, s, NEG_INF)
                p = jnp.exp(s - lse_g)
                dob = doh.astype(BF16)
                dp = _dot_nt(dob, vv)
                ds = (p * (dp - delta) * ATT_SCALE).astype(BF16)
                dq_ref[:, cols] = _dot(ds, kk)
                accumulate(_dot_tn(ds, qh), _dot_tn(p.astype(BF16), dob))
                if windowed:
                    p_sink = jnp.exp(sink_ref[kv * GROUPS + g] - lse_g)
                    dsink_ref[g:g + 1, :] += jnp.sum(-p_sink * delta, axis=0, keepdims=True)

        @pl.when(i == 0)
        def _():
            if windowed:
                dq_ref[...] = jnp.zeros_like(dq_ref)
            else:
                def acc_ctx(dkp, dvp):
                    dk_ref[0:ctx_len] += dkp
                    dv_ref[0:ctx_len] += dvp

                run(k_ref[0:ctx_len], v_ref[0:ctx_len], None, acc_ctx)

        @pl.when(i > 0)
        def _():
            if windowed:
                start = _band_start(i, t)
                kk = jnp.concatenate([k_ref[0:ctx_len], k_ref[pl.ds(start, BAND)]], axis=0)
                vv = jnp.concatenate([v_ref[0:ctx_len], v_ref[pl.ds(start, BAND)]], axis=0)

                def acc_band(dkp, dvp):
                    dk_ref[0:ctx_len] += dkp[0:ctx_len]
                    dv_ref[0:ctx_len] += dvp[0:ctx_len]
                    dk_ref[pl.ds(start, BAND)] += dkp[ctx_len:ctx_len + BAND]
                    dv_ref[pl.ds(start, BAND)] += dvp[ctx_len:ctx_len + BAND]

                run(kk, vv, _band_mask(i, start, ctx_len), acc_band)
            else:
                def acc_all(dkp, dvp):
                    dk_ref[...] += dkp
                    dv_ref[...] += dvp

                run(k_ref[...], v_ref[...], None, acc_all)

    kvspec = pl.BlockSpec((t, HEAD_DIM), lambda kv, i: (0, kv))
    grp = pl.BlockSpec((TM, gw), lambda kv, i: (i, kv))
    return _pcall(
        body, name=name, grid=(KV_HEADS, t // TM),
        in_specs=[pl.BlockSpec(memory_space=pltpu.SMEM), grp, kvspec, kvspec, grp,
                  pl.BlockSpec((GROUPS, TM, 1), lambda kv, i: (kv, i, 0)), grp],
        out_specs=[grp, kvspec, kvspec, pl.BlockSpec((None, 8, 128), lambda kv, i: (kv, 0, 0))],
        out_shape=[jax.ShapeDtypeStruct((t, ATT_WIDTH), F32), jax.ShapeDtypeStruct((t, KV_WIDTH), F32),
                   jax.ShapeDtypeStruct((t, KV_WIDTH), F32), jax.ShapeDtypeStruct((KV_HEADS, 8, 128), F32)],
        compiler_params=_params("parallel", "arbitrary"),
    )(sink, q, k, v, cat, lse, dcat)


def _segment_pos(t, ctx_len, width):
    row = lax.broadcasted_iota(jnp.int32, (t, width), 0)
    in_ctx = row < ctx_len
    return jnp.where(in_ctx, row, row - ctx_len), jnp.where(in_ctx, ctx_len, t - ctx_len)


def _shifted(x, offset, pos, seg_len):
    t = x.shape[0]
    moved = x if offset == 0 else pltpu.roll(x, (-offset) % t, 0)
    ok = (pos + offset >= 0) & (pos + offset < seg_len)
    return jnp.where(ok, moved, 0.0)


def conv_fwd(proj, conv_w, cat, ctx_len, name):
    t = proj.shape[0]
    base = QKV_WIDTH // 128
    per = AUX_WIDTH // 128

    def body(gb_ref, gc_ref, u_ref, w_ref, cat_in, o_ref):
        del cat_in
        pos, seg = _segment_pos(t, ctx_len, 128)
        z = gc_ref[...] * u_ref[...]
        conv = (w_ref[0:1, :] * _shifted(z, -1, pos, seg) + w_ref[1:2, :] * z
                + w_ref[2:3, :] * _shifted(z, 1, pos, seg))
        o_ref[...] = (gb_ref[...] * conv).astype(BF16)

    def col(off):
        return pl.BlockSpec((t, 128), lambda j: (0, base + off * per + j))

    return _pcall(
        body, name=name, grid=(per,),
        in_specs=[col(0), col(1), col(2), pl.BlockSpec((3, 128), lambda j: (0, j)),
                  pl.BlockSpec(memory_space=pl.ANY)],
        out_specs=pl.BlockSpec((t, 128), lambda j: (0, ATT_WIDTH // 128 + j)),
        out_shape=jax.ShapeDtypeStruct(cat.shape, BF16),
        input_output_aliases={4: 0},
        compiler_params=_params("parallel"),
    )(proj, proj, proj, conv_w, cat)


def conv_bwd(dcat, proj, conv_w, ctx_len, name):
    t = proj.shape[0]
    base = QKV_WIDTH // 128
    per = AUX_WIDTH // 128

    def body(do_ref, gb_ref, gc_ref, u_ref, w_ref, da_ref, dw_ref):
        pos, seg = _segment_pos(t, ctx_len, 128)
        gc = gc_ref[...]
        u = u_ref[...]
        z = gc * u
        zm = _shifted(z, -1, pos, seg)
        zp = _shifted(z, 1, pos, seg)
        conv = w_ref[0:1, :] * zm + w_ref[1:2, :] * z + w_ref[2:3, :] * zp
        dout = do_ref[...]
        da_ref[0] = (dout * conv).astype(BF16)
        dconv = dout * gb_ref[...]
        dw_ref[0:1, :] = jnp.sum(dconv * zm, axis=0, keepdims=True)
        dw_ref[1:2, :] = jnp.sum(dconv * z, axis=0, keepdims=True)
        dw_ref[2:3, :] = jnp.sum(dconv * zp, axis=0, keepdims=True)
        dz = (w_ref[1:2, :] * dconv + w_ref[0:1, :] * _shifted(dconv, 1, pos, seg)
              + w_ref[2:3, :] * _shifted(dconv, -1, pos, seg))
        da_ref[1] = (dz * u).astype(BF16)
        da_ref[2] = (dz * gc).astype(BF16)

    def col(off):
        return pl.BlockSpec((t, 128), lambda j: (0, base + off * per + j))

    return _pcall(
        body, name=name, grid=(per,),
        in_specs=[pl.BlockSpec((t, 128), lambda j: (0, ATT_WIDTH // 128 + j)), col(0), col(1), col(2),
                  pl.BlockSpec((3, 128), lambda j: (0, j))],
        out_specs=[pl.BlockSpec((3, t, 128), lambda j: (0, 0, j)), pl.BlockSpec((3, 128), lambda j: (0, j))],
        out_shape=[jax.ShapeDtypeStruct((3, t, AUX_WIDTH), BF16), jax.ShapeDtypeStruct((3, AUX_WIDTH), F32)],
        compiler_params=_params("parallel"),
    )(dcat, proj, proj, proj, conv_w)


def _pooled(u, w, pos, seg):
    lo = jnp.clip(pos - w // 2, 0, seg)
    hi = jnp.clip(pos - w // 2 + w, 0, seg)
    inv = 1.0 / (hi - lo).astype(F32)
    total = _shifted(u, -(w // 2), pos, seg)
    for o in range(-(w // 2) + 1, w // 2):
        total = total + _shifted(u, o, pos, seg)
    return total * inv - u, inv


def pool_fwd(proj, pool_w, pool_scale, cat, ctx_len, name):
    t = proj.shape[0]

    def body(u_ref, w_ref, sc_ref, cat_in, o_ref):
        del cat_in
        pos, seg = _segment_pos(t, ctx_len, 128)
        for g, w in enumerate(POOL_WINDOWS):
            cols = slice(g * 128, (g + 1) * 128)
            pooled, _ = _pooled(u_ref[:, cols], w, pos, seg)
            mixed = _dot(pooled.astype(BF16), w_ref[g].astype(BF16))
            o_ref[:, cols] = (mixed * sc_ref[:, cols]).astype(BF16)

    return _pcall(
        body, name=name, grid=(1,),
        in_specs=[pl.BlockSpec((t, AUX_WIDTH), lambda j: (0, QKV_WIDTH // AUX_WIDTH)),
                  pl.BlockSpec((AUX_GROUPS, 128, 128), lambda j: (0, 0, 0)),
                  pl.BlockSpec((1, AUX_WIDTH), lambda j: (0, 0)), pl.BlockSpec(memory_space=pl.ANY)],
        out_specs=pl.BlockSpec((t, AUX_WIDTH), lambda j: (0, ATT_WIDTH // AUX_WIDTH)),
        out_shape=jax.ShapeDtypeStruct(cat.shape, BF16),
        input_output_aliases={3: 0},
        compiler_params=_params("arbitrary"),
    )(proj, pool_w, pool_scale, cat)


def pool_bwd(dcat, proj, pool_w, pool_scale, ctx_len, name):
    t = proj.shape[0]

    def body(do_ref, u_ref, w_ref, sc_ref, da_ref, dw_ref, dsc_ref):
        pos, seg = _segment_pos(t, ctx_len, 128)
        for g, w in enumerate(POOL_WINDOWS):
            cols = slice(g * 128, (g + 1) * 128)
            pooled, inv = _pooled(u_ref[:, cols], w, pos, seg)
            pb = pooled.astype(BF16)
            wb = w_ref[g].astype(BF16)
            mixed = _dot(pb, wb)
            dout = do_ref[:, cols]
            dsc_ref[:, cols] = jnp.sum(dout * mixed, axis=0, keepdims=True)
            dmixed = (dout * sc_ref[:, cols]).astype(BF16)
            dw_ref[g] = _dot_tn(pb, dmixed)
            dpooled = _dot_nt(dmixed, wb)
            spread = dpooled * inv
            du = _shifted(spread, w // 2, pos, seg) - dpooled
            for o in range(-(w // 2) + 1, w // 2):
                du = du + _shifted(spread, -o, pos, seg)
            da_ref[0, :, cols] = du.astype(BF16)

    return _pcall(
        body, name=name, grid=(1,),
        in_specs=[pl.BlockSpec((t, AUX_WIDTH), lambda j: (0, ATT_WIDTH // AUX_WIDTH)),
                  pl.BlockSpec((t, AUX_WIDTH), lambda j: (0, QKV_WIDTH // AUX_WIDTH)),
                  pl.BlockSpec((AUX_GROUPS, 128, 128), lambda j: (0, 0, 0)),
                  pl.BlockSpec((1, AUX_WIDTH), lambda j: (0, 0))],
        out_specs=[pl.BlockSpec((1, t, AUX_WIDTH), lambda j: (0, 0, 0)),
                   pl.BlockSpec((AUX_GROUPS, 128, 128), lambda j: (0, 0, 0)),
                   pl.BlockSpec((1, AUX_WIDTH), lambda j: (0, 0))],
        out_shape=[jax.ShapeDtypeStruct((1, t, AUX_WIDTH), BF16), jax.ShapeDtypeStruct((AUX_GROUPS, 128, 128), F32),
                   jax.ShapeDtypeStruct((1, AUX_WIDTH), F32)],
        compiler_params=_params("arbitrary"),
    )(dcat, proj, pool_w, pool_scale)


MOD_NAMES = ("shift1", "scale1", "gate1", "shift2", "scale2", "gate2")


def _layer_fwd(h, lw, mods, tabs, ctx_len, windowed, tag):
    cos, sin = tabs
    xn = norm_mod_fwd(h, lw["norm1_g"], mods["shift1"], mods["scale1"], tag + "norm1_fwd")
    proj = mm_nn(xn, lw["w_in"], tag + "w_in_fwd")
    q, k, v = qk_prep_fwd(proj, lw["q_g"], lw["k_g"], cos, sin, tag + "qk_fwd")
    cat, lse = attention_fwd(q, k, v, lw["sink"], ctx_len, windowed, tag + "attn_fwd")
    if windowed:
        cat = pool_fwd(proj, lw["pool_w"], lw["pool_scale"], cat, ctx_len, tag + "pool_fwd")
    else:
        cat = conv_fwd(proj, lw["conv_w"], cat, ctx_len, tag + "conv_fwd")
    y, h1 = mm_nn_residual(cat, lw["w_out"], h, mods["gate1"], tag + "w_out_fwd")
    hn = norm_mod_fwd(h1, lw["norm2_g"], mods["shift2"], mods["scale2"], tag + "norm2_fwd")
    g, u, act = mm_swiglu(hn, lw["w_gate"], lw["w_up"], tag + "ffn_in_fwd")
    f, h2 = mm_nn_residual(act, lw["w_down"], h1, mods["gate2"], tag + "w_down_fwd")
    saved = dict(h=h, xn=xn, proj=proj, q=q, k=k, v=v, cat=cat, lse=lse, y=y, h1=h1, hn=hn, g=g, u=u, act=act, f=f)
    return h2, saved


def _layer_bwd(dh, sv, lw, mods, tabs, ctx_len, windowed, tag):
    cos, sin = tabs
    big, small = {}, {}
    df, dgate2 = gate_bwd(dh, sv["f"], mods["gate2"], tag + "gate2_bwd")
    dgp, du = mm_nt_swiglu_bwd(df, lw["w_down"], sv["g"], sv["u"], tag + "w_down_dgrad")
    big["w_down"] = mm_tn(sv["act"], df, tag + "w_down_wgrad", col_sharded=False)
    dhn = mm_nt_cols(dgp, lw["w_gate"], tag + "w_gate_dgrad")
    dhn = mm_nt_cols(du, lw["w_up"], tag + "w_up_dgrad", add=dhn)
    big["w_gate"] = mm_tn(sv["hn"], dgp, tag + "w_gate_wgrad", col_sharded=True)
    big["w_up"] = mm_tn(sv["hn"], du, tag + "w_up_wgrad", col_sharded=True)
    dh1, dshift2, dscale2, small["norm2_g"] = norm_mod_bwd(dhn, sv["h1"], lw["norm2_g"], mods["scale2"], dh,
                                                           tag + "norm2_bwd")
    dy, dgate1 = gate_bwd(dh1, sv["y"], mods["gate1"], tag + "gate1_bwd")
    dcat = mm_nt_rows(dy, lw["w_out"], tag + "w_out_dgrad")
    big["w_out"] = mm_tn(sv["cat"], dy, tag + "w_out_wgrad", col_sharded=False)
    if windowed:
        daux, small["pool_w"], small["pool_scale"] = pool_bwd(dcat, sv["proj"], lw["pool_w"], lw["pool_scale"],
                                                              ctx_len, tag + "pool_bwd")
    else:
        daux, small["conv_w"] = conv_bwd(dcat, sv["proj"], lw["conv_w"], ctx_len, tag + "conv_bwd")
    dq, dk, dv, dsink = attention_bwd(sv["q"], sv["k"], sv["v"], sv["cat"], sv["lse"], dcat, lw["sink"], ctx_len,
                                      windowed, tag + "attn_bwd")
    if windowed:
        small["sink"] = dsink[:, :GROUPS, 0].reshape(Q_HEADS)
    dproj, small["q_g"], small["k_g"] = qk_prep_bwd(dq, dk, dv, daux, sv["proj"], lw["q_g"], lw["k_g"], cos, sin,
                                                    tag + "qk_bwd")
    dxn = mm_nt_cols(dproj, lw["w_in"], tag + "w_in_dgrad")
    big["w_in"] = mm_tn(sv["xn"], dproj, tag + "w_in_wgrad", col_sharded=True)
    dh0, dshift1, dscale1, small["norm1_g"] = norm_mod_bwd(dxn, sv["h"], lw["norm1_g"], mods["scale1"], dh1,
                                                           tag + "norm1_bwd")
    dmod = jnp.concatenate([a[:, 0, :] for a in (dshift1, dscale1, dgate1, dshift2, dscale2, dgate2)], axis=1)
    return dh0, big, small, dmod


def _local_step(x, ctx, target, layers, mods, final_g):
    ctx_len = ctx.shape[0]
    h = jnp.concatenate([ctx, x], axis=0)
    tabs = _rope_tables(h.shape[0], ctx_len)
    saved = []
    for l, lw in enumerate(layers):
        h, sv = _layer_fwd(h, lw, mods[l], tabs, ctx_len, l % 2 == 1, f"l{l}_")
        saved.append(sv)
    loss, dh, dfg = final_loss(h, final_g, target)
    bigs, smalls, dmods = [None] * len(layers), [None] * len(layers), [None] * len(layers)
    for l in reversed(range(len(layers))):
        dh, bigs[l], smalls[l], dmods[l] = _layer_bwd(dh, saved[l], layers[l], mods[l], tabs, ctx_len, l % 2 == 1,
                                                       f"l{l}_")
    return loss, dh[ctx_len:], bigs, smalls, dmods, dfg


def _place():
    x, y, c = lax.axis_index("x"), lax.axis_index("y"), lax.axis_index("c")
    chips = [(1 - x, y), (x, 1 - y), (1 - x, 1 - y)]
    return x, y, c, chips


def _remote(src, dst, send_sem, recv_sem, to):
    return pltpu.make_async_remote_copy(src_ref=src, dst_ref=dst, send_sem=send_sem, recv_sem=recv_sem,
                                        device_id=to, device_id_type=MESH)


ANY = pl.BlockSpec(memory_space=pl.ANY)


def all_gather8(x_shard, name):
    m_per, n = x_shard.shape

    def body(x_ref, out_ref, send_sems, recv_sems, local_sem):
        x, y, c, chips = _place()
        me, sibling = (x, y, c), (x, y, 1 - c)

        def rows(px, py, pc):
            return out_ref.at[pl.ds((4 * px + 2 * py + pc) * m_per, m_per), :]

        def copy(k, block, to, src=None):
            return _remote(rows(*block) if src is None else src, rows(*block), send_sems.at[k], recv_sems.at[k], to)

        mine = pltpu.make_async_copy(x_ref, rows(*me), local_sem)
        mine.start()
        first = [copy(0, me, sibling, src=x_ref)]
        first += [copy(1 + j, me, (*chip, c), src=x_ref) for j, chip in enumerate(chips)]
        for cp in first:
            cp.start()
        passed = [copy(4 + j, (*chip, c), sibling) for j, chip in enumerate(chips)]
        for j, chip in enumerate(chips):
            copy(1 + j, (*chip, c), me).wait_recv()
            passed[j].start()
        copy(0, sibling, me).wait_recv()
        for j, chip in enumerate(chips):
            copy(4 + j, (*chip, 1 - c), me).wait_recv()
        for cp in first + passed:
            cp.wait_send()
        mine.wait()

    return _pcall(
        body, name=name,
        out_shape=jax.ShapeDtypeStruct((N_DEV * m_per, n), x_shard.dtype),
        in_specs=[pl.BlockSpec(memory_space=pltpu.VMEM)],
        out_specs=pl.BlockSpec(memory_space=pltpu.VMEM),
        scratch_shapes=[pltpu.SemaphoreType.DMA((7,)), pltpu.SemaphoreType.DMA((7,)), pltpu.SemaphoreType.DMA],
        compiler_params=pltpu.CompilerParams(vmem_limit_bytes=VMEM_LIMIT),
    )(x_shard)


def _row_tile(rows):
    for tr in (256, 128, 64, 32, 16):
        if rows % tr == 0:
            return tr
    return rows


def place_shard(w, p, name):
    r, cols = w.shape
    tr = _row_tile(r)

    def body(p_ref, w_ref, o_ref):
        del p_ref
        o_ref[...] = w_ref[...].astype(BF16)

    return _pcall(
        body, name=name,
        grid_spec=pltpu.PrefetchScalarGridSpec(
            num_scalar_prefetch=1, grid=(r // tr,),
            in_specs=[pl.BlockSpec((tr, cols), lambda i, p_ref: (i, 0))],
            out_specs=pl.BlockSpec((None, tr, cols), lambda i, p_ref: (p_ref[0], i, 0))),
        out_shape=jax.ShapeDtypeStruct((N_SHARD, r, cols), BF16),
        compiler_params=_params("parallel"),
    )(p.reshape(1).astype(jnp.int32), w)


def gather_weight_shards(stacks):
    n = len(stacks)

    def body(*refs):
        outs = refs[n:2 * n]
        send_sems, recv_sems = refs[2 * n:]
        x, y, c, chips = _place()
        p = 2 * x + y
        sibling = (x, y, 1 - c)
        started = []
        for i in range(n):
            hr = outs[i].shape[1] // 2
            mine = outs[i].at[p, pl.ds(c * hr, hr)]
            for j, chip in enumerate(chips):
                cp = _remote(mine, mine, send_sems.at[i, j], recv_sems.at[i, j], (*chip, c))
                cp.start()
                started.append(cp)
        for i in range(n):
            hr = outs[i].shape[1] // 2
            for j, (px, py) in enumerate(chips):
                half = outs[i].at[2 * px + py, pl.ds(c * hr, hr)]
                _remote(half, half, send_sems.at[i, j], recv_sems.at[i, j], (px, py, c)).wait_recv()
                fwd = _remote(half, half, send_sems.at[i, 3 + j], recv_sems.at[i, 3 + j], sibling)
                fwd.start()
                started.append(fwd)
        for i in range(n):
            hr = outs[i].shape[1] // 2
            for j, (px, py) in enumerate(chips):
                other = outs[i].at[2 * px + py, pl.ds((1 - c) * hr, hr)]
                _remote(other, other, send_sems.at[i, 3 + j], recv_sems.at[i, 3 + j], sibling).wait_recv()
        for cp in started:
            cp.wait_send()

    return _pcall(
        body, name="gather_weight_shards",
        out_shape=[jax.ShapeDtypeStruct(s.shape, s.dtype) for s in stacks],
        in_specs=[ANY] * n, out_specs=[ANY] * n,
        input_output_aliases={i: i for i in range(n)},
        scratch_shapes=[pltpu.SemaphoreType.DMA((n, 6)), pltpu.SemaphoreType.DMA((n, 6))],
    )(*stacks)


def exchange_sibling_halves(grads):
    n = len(grads)

    def body(*refs):
        ins, outs = refs[:n], refs[n:2 * n]
        send_sems, recv_sems = refs[2 * n:]
        x, y, c, _ = _place()
        copies = []
        for i in range(n):
            hr = ins[i].shape[1] // 2
            cp = _remote(ins[i].at[:, pl.ds((1 - c) * hr, hr), :], outs[i], send_sems.at[i], recv_sems.at[i],
                         (x, y, 1 - c))
            cp.start()
            copies.append(cp)
        for cp in copies:
            cp.wait()

    return _pcall(
        body, name="exchange_sibling_halves",
        out_shape=[jax.ShapeDtypeStruct((g.shape[0], g.shape[1] // 2, g.shape[2]), g.dtype) for g in grads],
        in_specs=[ANY] * n, out_specs=[ANY] * n,
        scratch_shapes=[pltpu.SemaphoreType.DMA((n,)), pltpu.SemaphoreType.DMA((n,))],
    )(*grads)


def exchange_chip_quarters(parts):
    n = len(parts)

    def body(*refs):
        ins, outs = refs[:n], refs[n:2 * n]
        send_sems, recv_sems = refs[2 * n:]
        _, _, c, chips = _place()
        sends = []
        for i in range(n):
            for j, (px, py) in enumerate(chips):
                cp = _remote(ins[i].at[2 * px + py], outs[i].at[j], send_sems.at[i, j], recv_sems.at[i, j], (px, py, c))
                cp.start()
                sends.append(cp)
        for i in range(n):
            for j, (px, py) in enumerate(chips):
                slot = outs[i].at[j]
                _remote(slot, slot, send_sems.at[i, j], recv_sems.at[i, j], (px, py, c)).wait_recv()
        for cp in sends:
            cp.wait_send()

    return _pcall(
        body, name="exchange_chip_quarters",
        out_shape=[jax.ShapeDtypeStruct((3,) + g.shape[1:], g.dtype) for g in parts],
        in_specs=[ANY] * n, out_specs=[ANY] * n,
        scratch_shapes=[pltpu.SemaphoreType.DMA((n, 3)), pltpu.SemaphoreType.DMA((n, 3))],
    )(*parts)


def exchange_final_halves(grads):
    n = len(grads)

    def body(*refs):
        outs = refs[n:2 * n]
        send_sems, recv_sems = refs[2 * n:]
        x, y, c, _ = _place()
        sends = []
        for i in range(n):
            hr = outs[i].shape[0] // 2
            mine = outs[i].at[pl.ds(c * hr, hr)]
            cp = _remote(mine, mine, send_sems.at[i], recv_sems.at[i], (x, y, 1 - c))
            cp.start()
            sends.append(cp)
        for i in range(n):
            hr = outs[i].shape[0] // 2
            other = outs[i].at[pl.ds((1 - c) * hr, hr)]
            _remote(other, other, send_sems.at[i], recv_sems.at[i], (x, y, 1 - c)).wait_recv()
        for cp in sends:
            cp.wait_send()

    return _pcall(
        body, name="exchange_final_halves",
        out_shape=[jax.ShapeDtypeStruct(g.shape, g.dtype) for g in grads],
        in_specs=[ANY] * n, out_specs=[ANY] * n,
        input_output_aliases={i: i for i in range(n)},
        scratch_shapes=[pltpu.SemaphoreType.DMA((n,)), pltpu.SemaphoreType.DMA((n,))],
    )(*grads)


def add_own_half(g, recv, c, name):
    s, r, cols = g.shape
    hr = r // 2
    tr = _row_tile(hr)
    nb = hr // tr

    def body(c_ref, g_ref, r_ref, o_ref):
        del c_ref
        o_ref[...] = (g_ref[...] + r_ref[...]).astype(BF16)

    return _pcall(
        body, name=name,
        grid_spec=pltpu.PrefetchScalarGridSpec(
            num_scalar_prefetch=1, grid=(s, nb),
            in_specs=[pl.BlockSpec((None, tr, cols), lambda j, i, c_ref: (j, c_ref[0] * nb + i, 0)),
                      pl.BlockSpec((None, tr, cols), lambda j, i, c_ref: (j, i, 0))],
            out_specs=pl.BlockSpec((None, tr, cols), lambda j, i, c_ref: (j, i, 0))),
        out_shape=jax.ShapeDtypeStruct((s, hr, cols), BF16),
        compiler_params=_params("parallel", "parallel"),
    )(c.reshape(1).astype(jnp.int32), g, recv)


def add_quarters(own, others, p, c, name):
    _, hr, cols = own.shape
    tr = _row_tile(hr)
    nb = hr // tr

    def body(pc_ref, own_ref, oth_ref, o_ref):
        del pc_ref
        acc = own_ref[...].astype(F32) + oth_ref[0].astype(F32)
        acc = acc + oth_ref[1].astype(F32)
        o_ref[...] = acc + oth_ref[2].astype(F32)

    return _pcall(
        body, name=name,
        grid_spec=pltpu.PrefetchScalarGridSpec(
            num_scalar_prefetch=1, grid=(nb,),
            in_specs=[pl.BlockSpec((None, tr, cols), lambda i, pc: (pc[0], i, 0)),
                      pl.BlockSpec((3, tr, cols), lambda i, pc: (0, i, 0))],
            out_specs=pl.BlockSpec((tr, cols), lambda i, pc: (pc[1] * nb + i, 0))),
        out_shape=jax.ShapeDtypeStruct((2 * hr, cols), F32),
        compiler_params=_params("parallel"),
    )(jnp.stack([p, c]).astype(jnp.int32), own, others)


def _adamw_math(w, g, m, v):
    m2 = ADAM_B1 * m + (1.0 - ADAM_B1) * g
    v2 = ADAM_B2 * v + (1.0 - ADAM_B2) * (g * g)
    m_hat = m2 / (1.0 - ADAM_B1 ** ADAM_STEP)
    v_hat = v2 / (1.0 - ADAM_B2 ** ADAM_STEP)
    delta = -ADAM_LR * (m_hat / (jnp.sqrt(v_hat) + ADAM_EPS) + ADAM_WD * w)
    return delta, m2, v2


def adamw(w, g, m, v, name):
    r, cols = w.shape
    tr = 128 if r % 128 == 0 else r

    def body(w_ref, g_ref, m_ref, v_ref, d_ref, m2_ref, v2_ref):
        d_ref[...], m2_ref[...], v2_ref[...] = _adamw_math(w_ref[...], g_ref[...], m_ref[...], v_ref[...])

    blk = pl.BlockSpec((tr, cols), lambda i: (i, 0))
    return _pcall(
        body, name=name, grid=(r // tr,),
        in_specs=[blk] * 4, out_specs=[blk] * 3,
        out_shape=[jax.ShapeDtypeStruct((r, cols), F32)] * 3,
        compiler_params=_params("parallel"),
    )(w, g, m, v)


def small_update(gathered, w, m, v):
    nd, r, lanes = gathered.shape

    def body(ga_ref, w_ref, m_ref, v_ref, g_ref, d_ref, m2_ref, v2_ref):
        g = ga_ref[0] + ga_ref[1]
        for dev in range(2, nd):
            g = g + ga_ref[dev]
        g_ref[...] = g
        d_ref[...], m2_ref[...], v2_ref[...] = _adamw_math(w_ref[...], g, m_ref[...], v_ref[...])

    return _pcall(
        body, name="small_update",
        out_shape=[jax.ShapeDtypeStruct((r, lanes), F32)] * 4,
        compiler_params=pltpu.CompilerParams(vmem_limit_bytes=VMEM_LIMIT),
    )(gathered, w, m, v)


def mod_fwd(c16, w_mod, b_mod, name):
    d, ns = w_mod.shape
    tn = 512

    def body(c_ref, w_ref, b_ref, o_ref):
        cv = c_ref[...]
        sc = (cv * _sigmoid(cv)).astype(BF16)
        o_ref[...] = _dot(sc, w_ref[...].astype(BF16)) + b_ref[...]

    return _pcall(
        body, name=name, grid=(ns // tn,),
        in_specs=[pl.BlockSpec((16, d), lambda j: (0, 0)), pl.BlockSpec((d, tn), lambda j: (0, j)),
                  pl.BlockSpec((1, tn), lambda j: (0, j))],
        out_specs=pl.BlockSpec((16, tn), lambda j: (0, j)),
        out_shape=jax.ShapeDtypeStruct((16, ns), F32),
        compiler_params=_params("parallel"),
    )(c16, w_mod, b_mod)


def wmod_update(c16, dmod16, w, m, v, name):
    d, ns = w.shape
    tn = 256

    def body(c_ref, dm_ref, w_ref, m_ref, v_ref, g_ref, d_ref, m2_ref, v2_ref):
        cv = c_ref[...]
        sc = (cv * _sigmoid(cv)).astype(BF16)
        g = _dot_tn(sc, dm_ref[...].astype(BF16))
        g_ref[...] = g
        d_ref[...], m2_ref[...], v2_ref[...] = _adamw_math(w_ref[...], g, m_ref[...], v_ref[...])

    blk = pl.BlockSpec((d, tn), lambda j: (0, j))
    return _pcall(
        body, name=name, grid=(ns // tn,),
        in_specs=[pl.BlockSpec((16, d), lambda j: (0, 0)), pl.BlockSpec((16, tn), lambda j: (0, j)), blk, blk, blk],
        out_specs=[blk] * 4,
        out_shape=[jax.ShapeDtypeStruct((d, ns), F32)] * 4,
        compiler_params=_params("parallel"),
    )(c16, dmod16, w, m, v)


def cctx_partial(dm0, w0, dm1, w1):
    d, ns = w0.shape
    tn = 512

    def body(dm0_ref, w0_ref, dm1_ref, w1_ref, o_ref):
        @pl.when(pl.program_id(0) == 0)
        def _():
            o_ref[...] = jnp.zeros_like(o_ref)

        for dm_ref, w_ref in ((dm0_ref, w0_ref), (dm1_ref, w1_ref)):
            tot = jnp.sum(dm_ref[...], axis=0, keepdims=True)
            lhs = jnp.broadcast_to(tot, (8, tn)).astype(BF16)
            o_ref[...] += _dot_nt(lhs, w_ref[...].astype(BF16))

    dspec = pl.BlockSpec((8, tn), lambda j: (0, j))
    wspec = pl.BlockSpec((d, tn), lambda j: (0, j))
    return _pcall(
        body, name="cctx_partial", grid=(ns // tn,),
        in_specs=[dspec, wspec, dspec, wspec],
        out_specs=pl.BlockSpec((8, d), lambda j: (0, 0)),
        out_shape=jax.ShapeDtypeStruct((8, d), F32),
        compiler_params=_params("arbitrary"),
    )(dm0, w0, dm1, w1)


def cctx_update(parts, c_ctx, m, v):
    d = c_ctx.shape[1]

    def body(p_ref, c_ref, m_ref, v_ref, g_ref, d_ref, m2_ref, v2_ref):
        tot = p_ref[0:1, :] + p_ref[1:2, :]
        tot = tot + p_ref[2:3, :]
        tot = tot + p_ref[3:4, :]
        cv = c_ref[...]
        sg = _sigmoid(cv)
        g = tot * (sg * (1.0 + cv * (1.0 - sg)))
        g_ref[...] = g
        d_ref[...], m2_ref[...], v2_ref[...] = _adamw_math(cv, g, m_ref[...], v_ref[...])

    return _pcall(
        body, name="cctx_update",
        out_shape=[jax.ShapeDtypeStruct((1, d), F32)] * 4,
    )(parts, c_ctx, m, v)


WEIGHT_NAMES = (
    "c_ctx", "l0_norm1_g", "l0_w_mod", "l0_b_mod", "l0_w_in", "l0_q_norm_g", "l0_k_norm_g", "l0_conv_w", "l0_w_out",
    "l0_norm2_g", "l0_w_gate", "l0_w_up", "l0_w_down", "l1_norm1_g", "l1_w_mod", "l1_b_mod", "l1_w_in",
    "l1_q_norm_g", "l1_k_norm_g", "l1_sink", "l1_pool_w", "l1_pool_scale", "l1_w_out", "l1_norm2_g", "l1_w_gate",
    "l1_w_up", "l1_w_down", "final_norm_g")
BIG_NAMES = ("w_in", "w_out", "w_gate", "w_up", "w_down")
SMALL_SLOTS = tuple(
    [(f"l{l}_{nm}", par and f"l{l}_{nm}") for l in (0, 1)
     for nm, par in (("dmod_lat", False), ("dmod_ctx", False), ("b_mod", True), ("norm1_g", True), ("norm2_g", True),
                     ("q_norm_g", True), ("k_norm_g", True))]
    + [("l0_conv_w", None), ("l1_sink", "l1_sink"), ("l1_pool_w", "l1_pool_w"), ("l1_pool_scale", "l1_pool_scale"),
       ("final_norm_g", "final_norm_g")])


def _pack(values, sizes):
    parts = []
    for (name, _), size in zip(SMALL_SLOTS, sizes):
        v = values.get(name)
        padded = -(-size // 128) * 128
        v = jnp.zeros((padded,), F32) if v is None else jnp.pad(v.reshape(-1).astype(F32), (0, padded - size))
        parts.append(v)
    total = sum(p.shape[0] for p in parts)
    parts.append(jnp.zeros((-(-total // 1024) * 1024 - total,), F32))
    return jnp.concatenate(parts).reshape(-1, 128)


def _offsets(sizes):
    offs, o = {}, 0
    for (name, _), size in zip(SMALL_SLOTS, sizes):
        offs[name] = (o, size)
        o += -(-size // 128) * 128
    return offs


def kernel(x, c, ctx, c_ctx, l0_norm1_g, l0_w_mod, l0_b_mod, l0_w_in, l0_q_norm_g, l0_k_norm_g, l0_conv_w, l0_w_out, l0_norm2_g, l0_w_gate, l0_w_up, l0_w_down, l1_norm1_g, l1_w_mod, l1_b_mod, l1_w_in, l1_q_norm_g, l1_k_norm_g, l1_sink, l1_pool_w, l1_pool_scale, l1_w_out, l1_norm2_g, l1_w_gate, l1_w_up, l1_w_down, final_norm_g, loss_target, m_c_ctx, m_l0_norm1_g, m_l0_w_mod, m_l0_b_mod, m_l0_w_in, m_l0_q_norm_g, m_l0_k_norm_g, m_l0_conv_w, m_l0_w_out, m_l0_norm2_g, m_l0_w_gate, m_l0_w_up, m_l0_w_down, m_l1_norm1_g, m_l1_w_mod, m_l1_b_mod, m_l1_w_in, m_l1_q_norm_g, m_l1_k_norm_g, m_l1_sink, m_l1_pool_w, m_l1_pool_scale, m_l1_w_out, m_l1_norm2_g, m_l1_w_gate, m_l1_w_up, m_l1_w_down, m_final_norm_g, v_c_ctx, v_l0_norm1_g, v_l0_w_mod, v_l0_b_mod, v_l0_w_in, v_l0_q_norm_g, v_l0_k_norm_g, v_l0_conv_w, v_l0_w_out, v_l0_norm2_g, v_l0_w_gate, v_l0_w_up, v_l0_w_down, v_l1_norm1_g, v_l1_w_mod, v_l1_b_mod, v_l1_w_in, v_l1_q_norm_g, v_l1_k_norm_g, v_l1_sink, v_l1_pool_w, v_l1_pool_scale, v_l1_w_out, v_l1_norm2_g, v_l1_w_gate, v_l1_w_up, v_l1_w_down, v_final_norm_g):
    a = dict(locals())
    xi, yi, ci = lax.axis_index("x"), lax.axis_index("y"), lax.axis_index("c")
    p = 2 * xi + yi
    me = 4 * xi + 2 * yi + ci
    d = x.shape[-1]
    conv_cols = l0_conv_w.shape[1]

    row0 = jnp.concatenate([c, jnp.pad(l0_conv_w, ((0, 0), (0, d - conv_cols))), jnp.zeros((4, d), F32)], axis=0)
    gathered = all_gather8(row0, "gather_cond").reshape(N_DEV, 8, d)
    c_all = gathered[:, 0]
    conv_w = gathered[0::2, 1:4, :conv_cols].transpose(1, 0, 2).reshape(3, N_SHARD * conv_cols)
    c16_fwd = jnp.concatenate([c_all, c_ctx[None], jnp.zeros((7, d), F32)], axis=0)
    c16_bwd = jnp.concatenate([c_all, jnp.broadcast_to(c_ctx[None], (8, d))], axis=0)

    ns_mod = l0_w_mod.shape[1]
    mod_parts = [mod_fwd(c16_fwd, a[f"l{l}_w_mod"], lax.dynamic_slice(a[f"l{l}_b_mod"], (p * ns_mod,), (ns_mod,))[None],
                         f"l{l}_mod_fwd") for l in (0, 1)]
    modg = all_gather8(jnp.concatenate(mod_parts, axis=0), "gather_mod").reshape(N_DEV, 2, 16, ns_mod)[0::2]
    mod_full = modg.transpose(1, 2, 0, 3).reshape(2, 16, N_SHARD * ns_mod)
    mods = []
    for l in (0, 1):
        lat = lax.dynamic_index_in_dim(mod_full[l], me, axis=0, keepdims=False)
        cx = mod_full[l, 8]
        mods.append({nm: jnp.stack([cx[j * d:(j + 1) * d], lat[j * d:(j + 1) * d]])[:, None, :]
                     for j, nm in enumerate(MOD_NAMES)})

    stacks = gather_weight_shards([place_shard(a[f"l{l}_{nm}"], p, f"l{l}_{nm}_place")
                                   for l in (0, 1) for nm in BIG_NAMES])
    layers = []
    for l in (0, 1):
        lw = {nm: stacks[l * len(BIG_NAMES) + k] for k, nm in enumerate(BIG_NAMES)}
        lw.update(norm1_g=a[f"l{l}_norm1_g"][None], norm2_g=a[f"l{l}_norm2_g"][None],
                  q_g=a[f"l{l}_q_norm_g"][None], k_g=a[f"l{l}_k_norm_g"][None])
        layers.append(lw)
    layers[0].update(conv_w=conv_w, sink=jnp.zeros((Q_HEADS,), F32))
    layers[1].update(sink=l1_sink, pool_w=l1_pool_w, pool_scale=l1_pool_scale[None])

    loss_tile, dx, bigs, smalls, dmods, dfg = _local_step(x[0], ctx[0], loss_target[0], layers, mods,
                                                          final_norm_g[None])

    partial = {"l0_conv_w": smalls[0]["conv_w"], "l1_sink": smalls[1]["sink"], "l1_pool_w": smalls[1]["pool_w"],
               "l1_pool_scale": smalls[1]["pool_scale"], "final_norm_g": dfg}
    for l in (0, 1):
        partial.update({f"l{l}_dmod_lat": dmods[l][1], f"l{l}_dmod_ctx": dmods[l][0],
                        f"l{l}_b_mod": dmods[l][0] + dmods[l][1], f"l{l}_norm1_g": smalls[l]["norm1_g"],
                        f"l{l}_norm2_g": smalls[l]["norm2_g"], f"l{l}_q_norm_g": smalls[l]["q_g"],
                        f"l{l}_k_norm_g": smalls[l]["k_g"]})
    sizes = [int(np.prod(partial[name].shape)) for name, _ in SMALL_SLOTS]
    offs = _offsets(sizes)
    gpack = _pack(partial, sizes)
    rows = gpack.shape[0]
    small_all = all_gather8(gpack, "gather_small").reshape(N_DEV, rows, 128)
    packs = [_pack({name: a[pre + par] for name, par in SMALL_SLOTS if par}, sizes) for pre in ("", "m_", "v_")]
    small_out = [t.reshape(-1) for t in small_update(small_all, *packs)]
    flat_all = small_all.reshape(N_DEV, rows * 128)

    def slot(flat, name, shape):
        o, size = offs[name]
        return flat[o:o + size].reshape(shape)

    results = {}
    for name, par in SMALL_SLOTS:
        if par:
            results[par] = tuple(slot(t, name, a[par].shape) for t in small_out)

    ctx_rows = []
    for l in (0, 1):
        o_lat, o_ctx = offs[f"l{l}_dmod_lat"][0], offs[f"l{l}_dmod_ctx"][0]
        lat = lax.dynamic_slice(flat_all, (0, o_lat + p * ns_mod), (N_DEV, ns_mod))
        cxr = lax.dynamic_slice(flat_all, (0, o_ctx + p * ns_mod), (N_DEV, ns_mod))
        ctx_rows.append(cxr)
        results[f"l{l}_w_mod"] = tuple(wmod_update(c16_bwd, jnp.concatenate([lat, cxr], axis=0), a[f"l{l}_w_mod"],
                                                   a[f"m_l{l}_w_mod"], a[f"v_l{l}_w_mod"], f"l{l}_w_mod_update"))
    part = cctx_partial(ctx_rows[0], l0_w_mod, ctx_rows[1], l1_w_mod)
    parts4 = all_gather8(part, "gather_cctx").reshape(N_DEV, 8, d)[0::2, 0]
    results["c_ctx"] = tuple(t[0] for t in cctx_update(parts4, c_ctx[None], m_c_ctx[None], v_c_ctx[None]))

    gconv = lax.dynamic_slice(slot(small_out[0], "l0_conv_w", (3, N_SHARD * conv_cols)), (0, p * conv_cols),
                              (3, conv_cols))
    results["l0_conv_w"] = (gconv,) + tuple(adamw(l0_conv_w, gconv, m_l0_conv_w, v_l0_conv_w, "l0_conv_w_adamw"))

    keys = [f"l{l}_{nm}" for l in (0, 1) for nm in BIG_NAMES]
    grads = [bigs[l][nm] for l in (0, 1) for nm in BIG_NAMES]
    recv = exchange_sibling_halves(grads)
    chip_sums = [add_own_half(g, r, ci, k + "_add_halves") for k, g, r in zip(keys, grads, recv)]
    quarters = exchange_chip_quarters(chip_sums)
    halves = [add_quarters(own, q, p, ci, k + "_add_quarters") for k, own, q in zip(keys, chip_sums, quarters)]
    fulls = exchange_final_halves(halves)
    for k, g in zip(keys, fulls):
        results[k] = (g,) + tuple(adamw(a[k], g, a["m_" + k], a["v_" + k], k + "_adamw"))

    loss = lax.psum(loss_tile[0, 0], ("x", "y", "c"))
    out = [loss, dx[None]]
    for j in range(4):
        out += [results[k][j] for k in WEIGHT_NAMES]
    return tuple(out)
```

```python
import numpy as np
import jax
import jax.numpy as jnp
from jax import lax
from jax.experimental import pallas as pl
from jax.experimental.pallas import tpu as pltpu

F32 = jnp.float32
BF16 = jnp.bfloat16
MESH = pl.DeviceIdType.MESH

HEAD_DIM = 128
Q_HEADS = 12
KV_HEADS = 4
GROUPS = Q_HEADS // KV_HEADS
ATT_WIDTH = Q_HEADS * HEAD_DIM
KV_WIDTH = KV_HEADS * HEAD_DIM
QKV_WIDTH = ATT_WIDTH + 2 * KV_WIDTH
AUX_WIDTH = 512
AUX_GROUPS = 4
POOL_WINDOWS = (2, 4, 8, 16)
WINDOW = 128
GRID_W = 64
ROPE_THETA = 10000.0
N_MOD = 6
EPS = 1e-6
NEG_INF = -1e30
ATT_SCALE = HEAD_DIM ** -0.5
N_SHARD = 4
N_DEV = 8

ADAM_LR = 0.001
ADAM_B1 = 0.9
ADAM_B2 = 0.999
ADAM_EPS = 1e-08
ADAM_WD = 0.01
ADAM_STEP = 10

TM = 256
BAND = TM + 2 * WINDOW
VMEM_LIMIT = 56 * 1024 * 1024

NT_DIMS = (((1,), (1,)), ((), ()))
TN_DIMS = (((0,), (0,)), ((), ()))


def _pcall(body, **kw):
    return pl.pallas_call(body, **kw)


def _params(*sem):
    return pltpu.CompilerParams(dimension_semantics=sem, vmem_limit_bytes=VMEM_LIMIT)


def _sel(i):
    return jnp.minimum(i, 1)


def _sel_spec(d):
    return pl.BlockSpec((None, 1, d), lambda i: (_sel(i), 0, 0))


def _dot(a, b):
    return jnp.dot(a, b, preferred_element_type=F32)


def _dot_nt(a, b):
    return lax.dot_general(a, b, NT_DIMS, preferred_element_type=F32)


def _dot_tn(a, b):
    return lax.dot_general(a, b, TN_DIMS, preferred_element_type=F32)


def _sigmoid(x):
    return 1.0 / (1.0 + jnp.exp(-x))


def norm_mod_fwd(h, g, shift, scale, name, deps=()):
    t, d = h.shape

    def body(h_ref, g_ref, sh_ref, sc_ref, *rest):
        o_ref = rest[-1]
        x = h_ref[...]
        r = lax.rsqrt(jnp.mean(x * x, axis=-1, keepdims=True) + EPS)
        y = x * r * g_ref[...]
        o_ref[...] = (y * (1.0 + sc_ref[...]) + sh_ref[...]).astype(BF16)

    return _pcall(
        body, name=name, grid=(t // TM,),
        in_specs=[pl.BlockSpec((TM, d), lambda i: (i, 0)), pl.BlockSpec((1, d), lambda i: (0, 0)),
                  _sel_spec(d), _sel_spec(d)] + [pl.BlockSpec(memory_space=pl.ANY)] * len(deps),
        out_specs=pl.BlockSpec((TM, d), lambda i: (i, 0)),
        out_shape=jax.ShapeDtypeStruct((t, d), BF16),
        compiler_params=_params("parallel"),
    )(h, g, shift, scale, *deps)


def norm_mod_bwd(dy, h, g, scale, dres, name):
    t, d = h.shape

    def body(dy_ref, h_ref, g_ref, sc_ref, dres_ref, dh_ref, dsh_ref, dsc_ref, dg_ref):
        i = pl.program_id(0)

        @pl.when(i == 0)
        def _():
            dsh_ref[...] = jnp.zeros_like(dsh_ref)
            dsc_ref[...] = jnp.zeros_like(dsc_ref)
            dg_ref[...] = jnp.zeros_like(dg_ref)

        x = h_ref[...]
        gv = g_ref[...]
        r = lax.rsqrt(jnp.mean(x * x, axis=-1, keepdims=True) + EPS)
        xhat = x * r
        dyv = dy_ref[...]
        s = _sel(i)
        dsh_ref[pl.ds(s, 1)] += jnp.sum(dyv, axis=0, keepdims=True)[None]
        dsc_ref[pl.ds(s, 1)] += jnp.sum(dyv * xhat * gv, axis=0, keepdims=True)[None]
        tt = dyv * (1.0 + sc_ref[...])
        dg_ref[...] += jnp.sum(tt * xhat, axis=0, keepdims=True)
        dxhat = tt * gv
        dx = r * (dxhat - xhat * jnp.mean(dxhat * xhat, axis=-1, keepdims=True))
        dh_ref[...] = dx + dres_ref[...]

    row = pl.BlockSpec((TM, d), lambda i: (i, 0))
    acc2 = pl.BlockSpec((2, 1, d), lambda i: (0, 0, 0))
    return _pcall(
        body, name=name, grid=(t // TM,),
        in_specs=[row, row, pl.BlockSpec((1, d), lambda i: (0, 0)), _sel_spec(d), row],
        out_specs=[row, acc2, acc2, pl.BlockSpec((1, d), lambda i: (0, 0))],
        out_shape=[jax.ShapeDtypeStruct((t, d), F32), jax.ShapeDtypeStruct((2, 1, d), F32),
                   jax.ShapeDtypeStruct((2, 1, d), F32), jax.ShapeDtypeStruct((1, d), F32)],
        compiler_params=_params("arbitrary"),
    )(dy, h, g, scale, dres)


def gate_bwd(dh, f, gate, name):
    t, d = dh.shape

    def body(dh_ref, f_ref, gt_ref, df_ref, dgt_ref):
        i = pl.program_id(0)

        @pl.when(i == 0)
        def _():
            dgt_ref[...] = jnp.zeros_like(dgt_ref)

        dhv = dh_ref[...]
        df_ref[...] = (dhv * gt_ref[...]).astype(BF16)
        dgt_ref[pl.ds(_sel(i), 1)] += jnp.sum(dhv * f_ref[...], axis=0, keepdims=True)[None]

    row = pl.BlockSpec((TM, d), lambda i: (i, 0))
    return _pcall(
        body, name=name, grid=(t // TM,),
        in_specs=[row, row, _sel_spec(d)],
        out_specs=[row, pl.BlockSpec((2, 1, d), lambda i: (0, 0, 0))],
        out_shape=[jax.ShapeDtypeStruct((t, d), BF16), jax.ShapeDtypeStruct((2, 1, d), F32)],
        compiler_params=_params("arbitrary"),
    )(dh, f, gate)


def final_loss(h, fg, target):
    t, d = h.shape

    def body(h_ref, g_ref, tg_ref, loss_ref, dh_ref, dg_ref):
        i = pl.program_id(0)

        @pl.when(i == 0)
        def _():
            loss_ref[...] = jnp.zeros_like(loss_ref)
            dg_ref[...] = jnp.zeros_like(dg_ref)
            dh_ref[...] = jnp.zeros_like(dh_ref)

        @pl.when(i > 0)
        def _():
            x = h_ref[...]
            gv = g_ref[...]
            r = lax.rsqrt(jnp.mean(x * x, axis=-1, keepdims=True) + EPS)
            xhat = x * r
            diff = xhat * gv - tg_ref[...]
            loss_ref[...] += 0.5 * jnp.sum(jnp.mean(diff * diff, axis=-1, keepdims=True), axis=0, keepdims=True)
            dout = diff * (1.0 / d)
            dg_ref[...] += jnp.sum(dout * xhat, axis=0, keepdims=True)
            dxhat = dout * gv
            dh_ref[...] = r * (dxhat - xhat * jnp.mean(dxhat * xhat, axis=-1, keepdims=True))

    row = pl.BlockSpec((TM, d), lambda i: (i, 0))
    return _pcall(
        body, name="final_loss", grid=(t // TM,),
        in_specs=[row, pl.BlockSpec((1, d), lambda i: (0, 0)),
                  pl.BlockSpec((TM, d), lambda i: (jnp.maximum(i - 1, 0), 0))],
        out_specs=[pl.BlockSpec((8, 128), lambda i: (0, 0)), row, pl.BlockSpec((1, d), lambda i: (0, 0))],
        out_shape=[jax.ShapeDtypeStruct((8, 128), F32), jax.ShapeDtypeStruct((t, d), F32),
                   jax.ShapeDtypeStruct((1, d), F32)],
        compiler_params=_params("arbitrary"),
    )(h, fg, target)


def _w_nn_spec(w, k):
    if w.shape[1] == k:
        ns = w.shape[2]
        return w, pl.BlockSpec((None, k, ns), lambda j, i: (j, 0, 0)), w.shape[0], ns
    w2 = w.reshape(w.shape[0] * w.shape[1], w.shape[2])
    tn = 1024 if w2.shape[1] % 1024 == 0 else w2.shape[1]
    return w2, pl.BlockSpec((k, tn), lambda j, i: (0, j)), w2.shape[1] // tn, tn


def mm_nn(a, w, name, out_dtype=F32):
    t, k = a.shape
    w, wspec, nb, tn = _w_nn_spec(w, k)

    def body(a_ref, w_ref, o_ref):
        o_ref[...] = _dot(a_ref[...], w_ref[...]).astype(o_ref.dtype)

    return _pcall(
        body, name=name, grid=(nb, t // TM),
        in_specs=[pl.BlockSpec((TM, k), lambda j, i: (i, 0)), wspec],
        out_specs=pl.BlockSpec((TM, tn), lambda j, i: (i, j)),
        out_shape=jax.ShapeDtypeStruct((t, nb * tn), out_dtype),
        compiler_params=_params("parallel", "parallel"),
    )(a, w)


def mm_nn_residual(a, w, h, gate, name):
    t, k = a.shape
    w, wspec, nb, tn = _w_nn_spec(w, k)

    def body(a_ref, w_ref, h_ref, gt_ref, y_ref, o_ref):
        y = _dot(a_ref[...], w_ref[...])
        y_ref[...] = y
        o_ref[...] = h_ref[...] + gt_ref[...] * y

    out = pl.BlockSpec((TM, tn), lambda j, i: (i, j))
    return _pcall(
        body, name=name, grid=(nb, t // TM),
        in_specs=[pl.BlockSpec((TM, k), lambda j, i: (i, 0)), wspec, out,
                  pl.BlockSpec((None, 1, tn), lambda j, i: (_sel(i), 0, j))],
        out_specs=[out, out],
        out_shape=[jax.ShapeDtypeStruct((t, nb * tn), F32)] * 2,
        compiler_params=_params("parallel", "parallel"),
    )(a, w, h, gate)


def mm_swiglu(a, wg, wu, name):
    t, k = a.shape
    s, _, ns = wg.shape

    def body(a_ref, wg_ref, wu_ref, g_ref, u_ref, act_ref):
        av = a_ref[...]
        g = _dot(av, wg_ref[...])
        u = _dot(av, wu_ref[...])
        g_ref[...] = g
        u_ref[...] = u
        act_ref[...] = (g * _sigmoid(g) * u).astype(BF16)

    wspec = pl.BlockSpec((None, k, ns), lambda j, i: (j, 0, 0))
    out = pl.BlockSpec((TM, ns), lambda j, i: (i, j))
    return _pcall(
        body, name=name, grid=(s, t // TM),
        in_specs=[pl.BlockSpec((TM, k), lambda j, i: (i, 0)), wspec, wspec],
        out_specs=[out, out, out],
        out_shape=[jax.ShapeDtypeStruct((t, s * ns), F32), jax.ShapeDtypeStruct((t, s * ns), F32),
                   jax.ShapeDtypeStruct((t, s * ns), BF16)],
        compiler_params=_params("parallel", "parallel"),
    )(a, wg, wu)


def mm_nt_cols(dy, w, name, add=None):
    t, n = dy.shape
    s, k, ns = w.shape
    tk = 512

    def body(*refs):
        if add is None:
            dy_ref, w_ref, o_ref = refs
        else:
            dy_ref, w_ref, add_ref, o_ref = refs
        acc = _dot_nt(dy_ref[:, 0:ns], w_ref[0])
        for sh in range(1, s):
            acc += _dot_nt(dy_ref[:, sh * ns:(sh + 1) * ns], w_ref[sh])
        if add is not None:
            acc += add_ref[...]
        o_ref[...] = acc

    out = pl.BlockSpec((TM, tk), lambda j, i: (i, j))
    in_specs = [pl.BlockSpec((TM, n), lambda j, i: (i, 0)), pl.BlockSpec((s, tk, ns), lambda j, i: (0, j, 0))]
    args = [dy, w]
    if add is not None:
        in_specs.append(out)
        args.append(add)
    return _pcall(
        body, name=name, grid=(k // tk, t // TM),
        in_specs=in_specs, out_specs=out,
        out_shape=jax.ShapeDtypeStruct((t, k), F32),
        compiler_params=_params("parallel", "parallel"),
    )(*args)


def mm_nt_rows(dy, w, name):
    t, n = dy.shape
    s, ks, _ = w.shape

    def body(dy_ref, w_ref, o_ref):
        o_ref[...] = _dot_nt(dy_ref[...], w_ref[...])

    return _pcall(
        body, name=name, grid=(s, t // TM),
        in_specs=[pl.BlockSpec((TM, n), lambda j, i: (i, 0)), pl.BlockSpec((None, ks, n), lambda j, i: (j, 0, 0))],
        out_specs=pl.BlockSpec((TM, ks), lambda j, i: (i, j)),
        out_shape=jax.ShapeDtypeStruct((t, s * ks), F32),
        compiler_params=_params("parallel", "parallel"),
    )(dy, w)


def mm_nt_swiglu_bwd(df, wd, g, u, name):
    t, n = df.shape
    s, ks, _ = wd.shape

    def body(df_ref, w_ref, g_ref, u_ref, dg_ref, du_ref):
        da = _dot_nt(df_ref[...], w_ref[...])
        gv = g_ref[...]
        sg = _sigmoid(gv)
        silu = gv * sg
        dg_ref[...] = (da * u_ref[...] * (sg * (1.0 + gv * (1.0 - sg)))).astype(BF16)
        du_ref[...] = (da * silu).astype(BF16)

    blk = pl.BlockSpec((TM, ks), lambda j, i: (i, j))
    return _pcall(
        body, name=name, grid=(s, t // TM),
        in_specs=[pl.BlockSpec((TM, n), lambda j, i: (i, 0)), pl.BlockSpec((None, ks, n), lambda j, i: (j, 0, 0)),
                  blk, blk],
        out_specs=[blk, blk],
        out_shape=[jax.ShapeDtypeStruct((t, s * ks), BF16)] * 2,
        compiler_params=_params("parallel", "parallel"),
    )(df, wd, g, u)


def mm_tn(x, dy, name, col_sharded):
    t, k = x.shape
    n = dy.shape[1]
    tk = 512
    mc = t // 2 if (t // 2) % 128 == 0 else t
    if col_sharded:
        tn = n // N_SHARD
        out_shape = jax.ShapeDtypeStruct((N_SHARD, k, tn), F32)
        out_spec = pl.BlockSpec((None, tk, tn), lambda j, kb, m: (j, kb, 0))
    else:
        tn = 1024
        out_shape = jax.ShapeDtypeStruct((k, n), F32)
        out_spec = pl.BlockSpec((tk, tn), lambda j, kb, m: (kb, j))

    def body(x_ref, dy_ref, o_ref):
        m = pl.program_id(2)
        acc = _dot_tn(x_ref[...], dy_ref[...])

        @pl.when(m == 0)
        def _():
            o_ref[...] = acc

        @pl.when(m > 0)
        def _():
            o_ref[...] += acc

    out = _pcall(
        body, name=name, grid=(n // tn, k // tk, t // mc),
        in_specs=[pl.BlockSpec((mc, tk), lambda j, kb, m: (m, kb)), pl.BlockSpec((mc, tn), lambda j, kb, m: (m, j))],
        out_specs=out_spec, out_shape=out_shape,
        compiler_params=_params("parallel", "parallel", "arbitrary"),
    )(x, dy)
    return out if col_sharded else out.reshape(N_SHARD, k // N_SHARD, n)


def _swap32(y):
    right = pltpu.roll(y, 32, 1)
    left = pltpu.roll(y, 96, 1)
    lane = lax.broadcasted_iota(jnp.int32, y.shape, 1)
    return jnp.where((lane // 32) % 2 == 0, left, right)


def _rope_tables(t, ctx_len):
    s = t - ctx_len
    pos = np.arange(s)
    row = (pos // GRID_W).astype(np.float32)
    col = (pos % GRID_W).astype(np.float32)
    half = HEAD_DIM // 2
    inv = np.power(np.float32(ROPE_THETA), -np.arange(0, half, 2, dtype=np.float32) / np.float32(half))
    ar = row[:, None] * inv
    ac = col[:, None] * inv
    cos = np.concatenate([np.cos(ar), np.cos(ar), np.cos(ac), np.cos(ac)], axis=1)
    sin = np.concatenate([-np.sin(ar), np.sin(ar), -np.sin(ac), np.sin(ac)], axis=1)
    cos = np.concatenate([np.ones((ctx_len, HEAD_DIM), np.float32), cos.astype(np.float32)], axis=0)
    sin = np.concatenate([np.zeros((ctx_len, HEAD_DIM), np.float32), sin.astype(np.float32)], axis=0)
    return jnp.asarray(cos, F32), jnp.asarray(sin, F32)


def qk_prep_fwd(proj, qg, kg, cos, sin, name):
    t = proj.shape[0]

    def body(p_ref, qg_ref, kg_ref, c_ref, s_ref, q_ref, k_ref, v_ref):
        cv = c_ref[...]
        sv = s_ref[...]
        for hd in range(Q_HEADS + KV_HEADS):
            x = p_ref[:, hd * HEAD_DIM:(hd + 1) * HEAD_DIM]
            gv = qg_ref[...] if hd < Q_HEADS else kg_ref[...]
            y = x * lax.rsqrt(jnp.mean(x * x, axis=-1, keepdims=True) + EPS) * gv
            out = (y * cv + _swap32(y) * sv).astype(BF16)
            if hd < Q_HEADS:
                q_ref[:, hd * HEAD_DIM:(hd + 1) * HEAD_DIM] = out
            else:
                k_ref[:, (hd - Q_HEADS) * HEAD_DIM:(hd - Q_HEADS + 1) * HEAD_DIM] = out
        v_ref[...] = p_ref[:, ATT_WIDTH + KV_WIDTH:QKV_WIDTH].astype(BF16)

    vec = pl.BlockSpec((1, HEAD_DIM), lambda i: (0, 0))
    tab = pl.BlockSpec((TM, HEAD_DIM), lambda i: (i, 0))
    return _pcall(
        body, name=name, grid=(t // TM,),
        in_specs=[pl.BlockSpec((TM, QKV_WIDTH), lambda i: (i, 0)), vec, vec, tab, tab],
        out_specs=[pl.BlockSpec((TM, ATT_WIDTH), lambda i: (i, 0)), pl.BlockSpec((TM, KV_WIDTH), lambda i: (i, 0)),
                   pl.BlockSpec((TM, KV_WIDTH), lambda i: (i, 0))],
        out_shape=[jax.ShapeDtypeStruct((t, ATT_WIDTH), BF16), jax.ShapeDtypeStruct((t, KV_WIDTH), BF16),
                   jax.ShapeDtypeStruct((t, KV_WIDTH), BF16)],
        compiler_params=_params("parallel"),
    )(proj, qg, kg, cos, sin)


def qk_prep_bwd(dq, dk, dv, daux, proj, qg, kg, cos, sin, name):
    t, n_in = proj.shape
    n_aux = daux.shape[0]

    def body(dq_ref, dk_ref, dv_ref, da_ref, p_ref, qg_ref, kg_ref, c_ref, s_ref, o_ref, dqg_ref, dkg_ref):
        @pl.when(pl.program_id(0) == 0)
        def _():
            dqg_ref[...] = jnp.zeros_like(dqg_ref)
            dkg_ref[...] = jnp.zeros_like(dkg_ref)

        cv = c_ref[...]
        sv = s_ref[...]
        for hd in range(Q_HEADS + KV_HEADS):
            cols = slice(hd * HEAD_DIM, (hd + 1) * HEAD_DIM)
            x = p_ref[:, cols]
            if hd < Q_HEADS:
                gv, dyr, dg_ref = qg_ref[...], dq_ref[:, cols], dqg_ref
            else:
                kc = slice((hd - Q_HEADS) * HEAD_DIM, (hd - Q_HEADS + 1) * HEAD_DIM)
                gv, dyr, dg_ref = kg_ref[...], dk_ref[:, kc], dkg_ref
            r = lax.rsqrt(jnp.mean(x * x, axis=-1, keepdims=True) + EPS)
            xhat = x * r
            dy = dyr * cv + _swap32(dyr * sv)
            dg_ref[...] += jnp.sum(dy * xhat, axis=0, keepdims=True)
            dxhat = dy * gv
            o_ref[:, cols] = (r * (dxhat - xhat * jnp.mean(dxhat * xhat, axis=-1, keepdims=True))).astype(BF16)
        o_ref[:, ATT_WIDTH + KV_WIDTH:QKV_WIDTH] = dv_ref[...].astype(BF16)
        for a in range(n_aux):
            o_ref[:, QKV_WIDTH + a * AUX_WIDTH:QKV_WIDTH + (a + 1) * AUX_WIDTH] = da_ref[a]

    vec = pl.BlockSpec((1, HEAD_DIM), lambda i: (0, 0))
    tab = pl.BlockSpec((TM, HEAD_DIM), lambda i: (i, 0))
    return _pcall(
        body, name=name, grid=(t // TM,),
        in_specs=[pl.BlockSpec((TM, ATT_WIDTH), lambda i: (i, 0)), pl.BlockSpec((TM, KV_WIDTH), lambda i: (i, 0)),
                  pl.BlockSpec((TM, KV_WIDTH), lambda i: (i, 0)),
                  pl.BlockSpec((n_aux, TM, AUX_WIDTH), lambda i: (0, i, 0)),
                  pl.BlockSpec((TM, QKV_WIDTH), lambda i: (i, 0)), vec, vec, tab, tab],
        out_specs=[pl.BlockSpec((TM, n_in), lambda i: (i, 0)), vec, vec],
        out_shape=[jax.ShapeDtypeStruct((t, n_in), BF16), jax.ShapeDtypeStruct((1, HEAD_DIM), F32),
                   jax.ShapeDtypeStruct((1, HEAD_DIM), F32)],
        compiler_params=_params("arbitrary"),
    )(dq, dk, dv, daux, proj, qg, kg, cos, sin)


def _band_start(i, t):
    return pl.multiple_of(jnp.clip(i * TM - WINDOW, 0, t - BAND), WINDOW)


def _band_mask(i, start, ctx_len):
    shape = (TM, ctx_len + BAND)
    col = lax.broadcasted_iota(jnp.int32, shape, 1)
    qrow = i * TM + lax.broadcasted_iota(jnp.int32, shape, 0)
    krow = start + col - ctx_len
    band_ok = (krow >= ctx_len) & (jnp.abs(krow - qrow) <= WINDOW)
    return (col < ctx_len) | band_ok


def attention_fwd(q, k, v, sink, ctx_len, windowed, name):
    t = q.shape[0]
    d_model = ATT_WIDTH + AUX_WIDTH
    gw = GROUPS * HEAD_DIM

    def one_head(qh, kk, vv, mask, sink_val):
        s = _dot_nt(qh, kk) * ATT_SCALE
        if mask is not None:
            s = jnp.where(mask, s, NEG_INF)
        m = jnp.max(s, axis=-1, keepdims=True)
        if sink_val is not None:
            m = jnp.maximum(m, sink_val)
        p = jnp.exp(s - m)
        l = jnp.sum(p, axis=-1, keepdims=True)
        if sink_val is not None:
            l = l + jnp.exp(sink_val - m)
        o = _dot(p.astype(BF16), vv) / l
        return o, m + jnp.log(l)

    def body(sink_ref, q_ref, k_ref, v_ref, o_ref, lse_ref):
        kv = pl.program_id(0)
        i = pl.program_id(1)

        def run(kk, vv, mask):
            for g in range(GROUPS):
                sink_val = sink_ref[kv * GROUPS + g] if windowed else None
                o, lse = one_head(q_ref[:, g * HEAD_DIM:(g + 1) * HEAD_DIM], kk, vv, mask, sink_val)
                o_ref[:, g * HEAD_DIM:(g + 1) * HEAD_DIM] = o.astype(BF16)
                lse_ref[g] = lse

        @pl.when(i == 0)
        def _():
            if windowed:
                o_ref[...] = jnp.zeros_like(o_ref)
                lse_ref[...] = jnp.zeros_like(lse_ref)
            else:
                run(k_ref[0:ctx_len], v_ref[0:ctx_len], None)

        @pl.when(i > 0)
        def _():
            if windowed:
                start = _band_start(i, t)
                kk = jnp.concatenate([k_ref[0:ctx_len], k_ref[pl.ds(start, BAND)]], axis=0)
                vv = jnp.concatenate([v_ref[0:ctx_len], v_ref[pl.ds(start, BAND)]], axis=0)
                run(kk, vv, _band_mask(i, start, ctx_len))
            else:
                run(k_ref[...], v_ref[...], None)

    kvspec = pl.BlockSpec((t, HEAD_DIM), lambda kv, i: (0, kv))
    return _pcall(
        body, name=name, grid=(KV_HEADS, t // TM),
        in_specs=[pl.BlockSpec(memory_space=pltpu.SMEM), pl.BlockSpec((TM, gw), lambda kv, i: (i, kv)), kvspec, kvspec],
        out_specs=[pl.BlockSpec((TM, gw), lambda kv, i: (i, kv)),
                   pl.BlockSpec((GROUPS, TM, 1), lambda kv, i: (kv, i, 0))],
        out_shape=[jax.ShapeDtypeStruct((t, d_model), BF16), jax.ShapeDtypeStruct((Q_HEADS, t, 1), F32)],
        compiler_params=_params("parallel", "parallel"),
    )(sink, q, k, v)


def attention_bwd(q, k, v, cat, lse, dcat, sink, ctx_len, windowed, name):
    t = q.shape[0]
    gw = GROUPS * HEAD_DIM

    def body(sink_ref, q_ref, k_ref, v_ref, o_ref, lse_ref, do_ref, dq_ref, dk_ref, dv_ref, dsink_ref):
        kv = pl.program_id(0)
        i = pl.program_id(1)

        @pl.when(i == 0)
        def _():
            dk_ref[...] = jnp.zeros_like(dk_ref)
            dv_ref[...] = jnp.zeros_like(dv_ref)
            dsink_ref[...] = jnp.zeros_like(dsink_ref)

        def run(kk, vv, mask, accumulate):
            for g in range(GROUPS):
                cols = slice(g * HEAD_DIM, (g + 1) * HEAD_DIM)
                qh = q_ref[:, cols]
                doh = do_ref[:, cols]
                delta = jnp.sum(doh * o_ref[:, cols].astype(F32), axis=-1, keepdims=True)
                lse_g = lse_ref[g]
                s = _dot_nt(qh, kk) * ATT_SCALE
                if mask is not None:
                    s = jnp.where(mask, s, NEG_INF)
                p = jnp.exp(s - lse_g)
                dob = doh.astype(BF16)
                dp = _dot_nt(dob, vv)
                ds = (p * (dp - delta) * ATT_SCALE).astype(BF16)
                dq_ref[:, cols] = _dot(ds, kk)
                accumulate(_dot_tn(ds, qh), _dot_tn(p.astype(BF16), dob))
                if windowed:
                    p_sink = jnp.exp(sink_ref[kv * GROUPS + g] - lse_g)
                    dsink_ref[g:g + 1, :] += jnp.sum(-p_sink * delta, axis=0, keepdims=True)

        @pl.when(i == 0)
        def _():
            if windowed:
                dq_ref[...] = jnp.zeros_like(dq_ref)
            else:
                def acc_ctx(dkp, dvp):
                    dk_ref[0:ctx_len] += dkp
                    dv_ref[0:ctx_len] += dvp

                run(k_ref[0:ctx_len], v_ref[0:ctx_len], None, acc_ctx)

        @pl.when(i > 0)
        def _():
            if windowed:
                start = _band_start(i, t)
                kk = jnp.concatenate([k_ref[0:ctx_len], k_ref[pl.ds(start, BAND)]], axis=0)
                vv = jnp.concatenate([v_ref[0:ctx_len], v_ref[pl.ds(start, BAND)]], axis=0)

                def acc_band(dkp, dvp):
                    dk_ref[0:ctx_len] += dkp[0:ctx_len]
                    dv_ref[0:ctx_len] += dvp[0:ctx_len]
                    dk_ref[pl.ds(start, BAND)] += dkp[ctx_len:ctx_len + BAND]
                    dv_ref[pl.ds(start, BAND)] += dvp[ctx_len:ctx_len + BAND]

                run(kk, vv, _band_mask(i, start, ctx_len), acc_band)
            else:
                def acc_all(dkp, dvp):
                    dk_ref[...] += dkp
                    dv_ref[...] += dvp

                run(k_ref[...], v_ref[...], None, acc_all)

    kvspec = pl.BlockSpec((t, HEAD_DIM), lambda kv, i: (0, kv))
    grp = pl.BlockSpec((TM, gw), lambda kv, i: (i, kv))
    return _pcall(
        body, name=name, grid=(KV_HEADS, t // TM),
        in_specs=[pl.BlockSpec(memory_space=pltpu.SMEM), grp, kvspec, kvspec, grp,
                  pl.BlockSpec((GROUPS, TM, 1), lambda kv, i: (kv, i, 0)), grp],
        out_specs=[grp, kvspec, kvspec, pl.BlockSpec((None, 8, 128), lambda kv, i: (kv, 0, 0))],
        out_shape=[jax.ShapeDtypeStruct((t, ATT_WIDTH), F32), jax.ShapeDtypeStruct((t, KV_WIDTH), F32),
                   jax.ShapeDtypeStruct((t, KV_WIDTH), F32), jax.ShapeDtypeStruct((KV_HEADS, 8, 128), F32)],
        compiler_params=_params("parallel", "arbitrary"),
    )(sink, q, k, v, cat, lse, dcat)


def _segment_pos(t, ctx_len, width):
    row = lax.broadcasted_iota(jnp.int32, (t, width), 0)
    in_ctx = row < ctx_len
    return jnp.where(in_ctx, row, row - ctx_len), jnp.where(in_ctx, ctx_len, t - ctx_len)


def _shifted(x, offset, pos, seg_len):
    t = x.shape[0]
    moved = x if offset == 0 else pltpu.roll(x, (-offset) % t, 0)
    ok = (pos + offset >= 0) & (pos + offset < seg_len)
    return jnp.where(ok, moved, 0.0)


def conv_fwd(proj, conv_w, cat, ctx_len, name):
    t = proj.shape[0]
    base = QKV_WIDTH // 128
    per = AUX_WIDTH // 128

    def body(gb_ref, gc_ref, u_ref, w_ref, cat_in, o_ref):
        del cat_in
        pos, seg = _segment_pos(t, ctx_len, 128)
        z = gc_ref[...] * u_ref[...]
        conv = (w_ref[0:1, :] * _shifted(z, -1, pos, seg) + w_ref[1:2, :] * z
                + w_ref[2:3, :] * _shifted(z, 1, pos, seg))
        o_ref[...] = (gb_ref[...] * conv).astype(BF16)

    def col(off):
        return pl.BlockSpec((t, 128), lambda j: (0, base + off * per + j))

    return _pcall(
        body, name=name, grid=(per,),
        in_specs=[col(0), col(1), col(2), pl.BlockSpec((3, 128), lambda j: (0, j)),
                  pl.BlockSpec(memory_space=pl.ANY)],
        out_specs=pl.BlockSpec((t, 128), lambda j: (0, ATT_WIDTH // 128 + j)),
        out_shape=jax.ShapeDtypeStruct(cat.shape, BF16),
        input_output_aliases={4: 0},
        compiler_params=_params("parallel"),
    )(proj, proj, proj, conv_w, cat)


def conv_bwd(dcat, proj, conv_w, ctx_len, name):
    t = proj.shape[0]
    base = QKV_WIDTH // 128
    per = AUX_WIDTH // 128

    def body(do_ref, gb_ref, gc_ref, u_ref, w_ref, da_ref, dw_ref):
        pos, seg = _segment_pos(t, ctx_len, 128)
        gc = gc_ref[...]
        u = u_ref[...]
        z = gc * u
        zm = _shifted(z, -1, pos, seg)
        zp = _shifted(z, 1, pos, seg)
        conv = w_ref[0:1, :] * zm + w_ref[1:2, :] * z + w_ref[2:3, :] * zp
        dout = do_ref[...]
        da_ref[0] = (dout * conv).astype(BF16)
        dconv = dout * gb_ref[...]
        dw_ref[0:1, :] = jnp.sum(dconv * zm, axis=0, keepdims=True)
        dw_ref[1:2, :] = jnp.sum(dconv * z, axis=0, keepdims=True)
        dw_ref[2:3, :] = jnp.sum(dconv * zp, axis=0, keepdims=True)
        dz = (w_ref[1:2, :] * dconv + w_ref[0:1, :] * _shifted(dconv, 1, pos, seg)
              + w_ref[2:3, :] * _shifted(dconv, -1, pos, seg))
        da_ref[1] = (dz * u).astype(BF16)
        da_ref[2] = (dz * gc).astype(BF16)

    def col(off):
        return pl.BlockSpec((t, 128), lambda j: (0, base + off * per + j))

    return _pcall(
        body, name=name, grid=(per,),
        in_specs=[pl.BlockSpec((t, 128), lambda j: (0, ATT_WIDTH // 128 + j)), col(0), col(1), col(2),
                  pl.BlockSpec((3, 128), lambda j: (0, j))],
        out_specs=[pl.BlockSpec((3, t, 128), lambda j: (0, 0, j)), pl.BlockSpec((3, 128), lambda j: (0, j))],
        out_shape=[jax.ShapeDtypeStruct((3, t, AUX_WIDTH), BF16), jax.ShapeDtypeStruct((3, AUX_WIDTH), F32)],
        compiler_params=_params("parallel"),
    )(dcat, proj, proj, proj, conv_w)


def _pooled(u, w, pos, seg):
    lo = jnp.clip(pos - w // 2, 0, seg)
    hi = jnp.clip(pos - w // 2 + w, 0, seg)
    inv = 1.0 / (hi - lo).astype(F32)
    total = _shifted(u, -(w // 2), pos, seg)
    for o in range(-(w // 2) + 1, w // 2):
        total = total + _shifted(u, o, pos, seg)
    return total * inv - u, inv


def pool_fwd(proj, pool_w, pool_scale, cat, ctx_len, name):
    t = proj.shape[0]

    def body(u_ref, w_ref, sc_ref, cat_in, o_ref):
        del cat_in
        pos, seg = _segment_pos(t, ctx_len, 128)
        for g, w in enumerate(POOL_WINDOWS):
            cols = slice(g * 128, (g + 1) * 128)
            pooled, _ = _pooled(u_ref[:, cols], w, pos, seg)
            mixed = _dot(pooled.astype(BF16), w_ref[g].astype(BF16))
            o_ref[:, cols] = (mixed * sc_ref[:, cols]).astype(BF16)

    return _pcall(
        body, name=name, grid=(1,),
        in_specs=[pl.BlockSpec((t, AUX_WIDTH), lambda j: (0, QKV_WIDTH // AUX_WIDTH)),
                  pl.BlockSpec((AUX_GROUPS, 128, 128), lambda j: (0, 0, 0)),
                  pl.BlockSpec((1, AUX_WIDTH), lambda j: (0, 0)), pl.BlockSpec(memory_space=pl.ANY)],
        out_specs=pl.BlockSpec((t, AUX_WIDTH), lambda j: (0, ATT_WIDTH // AUX_WIDTH)),
        out_shape=jax.ShapeDtypeStruct(cat.shape, BF16),
        input_output_aliases={3: 0},
        compiler_params=_params("arbitrary"),
    )(proj, pool_w, pool_scale, cat)


def pool_bwd(dcat, proj, pool_w, pool_scale, ctx_len, name):
    t = proj.shape[0]

    def body(do_ref, u_ref, w_ref, sc_ref, da_ref, dw_ref, dsc_ref):
        pos, seg = _segment_pos(t, ctx_len, 128)
        for g, w in enumerate(POOL_WINDOWS):
            cols = slice(g * 128, (g + 1) * 128)
            pooled, inv = _pooled(u_ref[:, cols], w, pos, seg)
            pb = pooled.astype(BF16)
            wb = w_ref[g].astype(BF16)
            mixed = _dot(pb, wb)
            dout = do_ref[:, cols]
            dsc_ref[:, cols] = jnp.sum(dout * mixed, axis=0, keepdims=True)
            dmixed = (dout * sc_ref[:, cols]).astype(BF16)
            dw_ref[g] = _dot_tn(pb, dmixed)
            dpooled = _dot_nt(dmixed, wb)
            spread = dpooled * inv
            du = _shifted(spread, w // 2, pos, seg) - dpooled
            for o in range(-(w // 2) + 1, w // 2):
                du = du + _shifted(spread, -o, pos, seg)
            da_ref[0, :, cols] = du.astype(BF16)

    return _pcall(
        body, name=name, grid=(1,),
        in_specs=[pl.BlockSpec((t, AUX_WIDTH), lambda j: (0, ATT_WIDTH // AUX_WIDTH)),
                  pl.BlockSpec((t, AUX_WIDTH), lambda j: (0, QKV_WIDTH // AUX_WIDTH)),
                  pl.BlockSpec((AUX_GROUPS, 128, 128), lambda j: (0, 0, 0)),
                  pl.BlockSpec((1, AUX_WIDTH), lambda j: (0, 0))],
        out_specs=[pl.BlockSpec((1, t, AUX_WIDTH), lambda j: (0, 0, 0)),
                   pl.BlockSpec((AUX_GROUPS, 128, 128), lambda j: (0, 0, 0)),
                   pl.BlockSpec((1, AUX_WIDTH), lambda j: (0, 0))],
        out_shape=[jax.ShapeDtypeStruct((1, t, AUX_WIDTH), BF16), jax.ShapeDtypeStruct((AUX_GROUPS, 128, 128), F32),
                   jax.ShapeDtypeStruct((1, AUX_WIDTH), F32)],
        compiler_params=_params("arbitrary"),
    )(dcat, proj, pool_w, pool_scale)


MOD_NAMES = ("shift1", "scale1", "gate1", "shift2", "scale2", "gate2")


def _layer_fwd(h, lw, mods, tabs, ctx_len, windowed, tag, deps=()):
    cos, sin = tabs
    xn = norm_mod_fwd(h, lw["norm1_g"], mods["shift1"], mods["scale1"], tag + "norm1_fwd", deps)
    proj = mm_nn(xn, lw["w_in"], tag + "w_in_fwd")
    q, k, v = qk_prep_fwd(proj, lw["q_g"], lw["k_g"], cos, sin, tag + "qk_fwd")
    cat, lse = attention_fwd(q, k, v, lw["sink"], ctx_len, windowed, tag + "attn_fwd")
    if windowed:
        cat = pool_fwd(proj, lw["pool_w"], lw["pool_scale"], cat, ctx_len, tag + "pool_fwd")
    else:
        cat = conv_fwd(proj, lw["conv_w"], cat, ctx_len, tag + "conv_fwd")
    y, h1 = mm_nn_residual(cat, lw["w_out"], h, mods["gate1"], tag + "w_out_fwd")
    hn = norm_mod_fwd(h1, lw["norm2_g"], mods["shift2"], mods["scale2"], tag + "norm2_fwd")
    g, u, act = mm_swiglu(hn, lw["w_gate"], lw["w_up"], tag + "ffn_in_fwd")
    f, h2 = mm_nn_residual(act, lw["w_down"], h1, mods["gate2"], tag + "w_down_fwd")
    saved = dict(h=h, xn=xn, proj=proj, q=q, k=k, v=v, cat=cat, lse=lse, y=y, h1=h1, hn=hn, g=g, u=u, act=act, f=f)
    return h2, saved


def _layer_bwd(dh, sv, lw, mods, tabs, ctx_len, windowed, tag):
    cos, sin = tabs
    big, small = {}, {}
    df, dgate2 = gate_bwd(dh, sv["f"], mods["gate2"], tag + "gate2_bwd")
    dgp, du = mm_nt_swiglu_bwd(df, lw["w_down"], sv["g"], sv["u"], tag + "w_down_dgrad")
    big["w_down"] = mm_tn(sv["act"], df, tag + "w_down_wgrad", col_sharded=False)
    dhn = mm_nt_cols(dgp, lw["w_gate"], tag + "w_gate_dgrad")
    dhn = mm_nt_cols(du, lw["w_up"], tag + "w_up_dgrad", add=dhn)
    big["w_gate"] = mm_tn(sv["hn"], dgp, tag + "w_gate_wgrad", col_sharded=True)
    big["w_up"] = mm_tn(sv["hn"], du, tag + "w_up_wgrad", col_sharded=True)
    dh1, dshift2, dscale2, small["norm2_g"] = norm_mod_bwd(dhn, sv["h1"], lw["norm2_g"], mods["scale2"], dh,
                                                           tag + "norm2_bwd")
    dy, dgate1 = gate_bwd(dh1, sv["y"], mods["gate1"], tag + "gate1_bwd")
    dcat = mm_nt_rows(dy, lw["w_out"], tag + "w_out_dgrad")
    big["w_out"] = mm_tn(sv["cat"], dy, tag + "w_out_wgrad", col_sharded=False)
    if windowed:
        daux, small["pool_w"], small["pool_scale"] = pool_bwd(dcat, sv["proj"], lw["pool_w"], lw["pool_scale"],
                                                              ctx_len, tag + "pool_bwd")
    else:
        daux, small["conv_w"] = conv_bwd(dcat, sv["proj"], lw["conv_w"], ctx_len, tag + "conv_bwd")
    dq, dk, dv, dsink = attention_bwd(sv["q"], sv["k"], sv["v"], sv["cat"], sv["lse"], dcat, lw["sink"], ctx_len,
                                      windowed, tag + "attn_bwd")
    if windowed:
        small["sink"] = dsink[:, :GROUPS, 0].reshape(Q_HEADS)
    dproj, small["q_g"], small["k_g"] = qk_prep_bwd(dq, dk, dv, daux, sv["proj"], lw["q_g"], lw["k_g"], cos, sin,
                                                    tag + "qk_bwd")
    dxn = mm_nt_cols(dproj, lw["w_in"], tag + "w_in_dgrad")
    big["w_in"] = mm_tn(sv["xn"], dproj, tag + "w_in_wgrad", col_sharded=True)
    dh0, dshift1, dscale1, small["norm1_g"] = norm_mod_bwd(dxn, sv["h"], lw["norm1_g"], mods["scale1"], dh1,
                                                           tag + "norm1_bwd")
    dmod = jnp.concatenate([a[:, 0, :] for a in (dshift1, dscale1, dgate1, dshift2, dscale2, dgate2)], axis=1)
    return dh0, big, small, dmod


def _local_step(x, ctx, target, layer_fns, mods, final_g, first_deps=()):
    ctx_len = ctx.shape[0]
    h = jnp.concatenate([ctx, x], axis=0)
    tabs = _rope_tables(h.shape[0], ctx_len)
    saved, layers = [], []
    for l, fn in enumerate(layer_fns):
        layers.append(fn(h))
        h, sv = _layer_fwd(h, layers[l], mods[l], tabs, ctx_len, l % 2 == 1, f"l{l}_", first_deps if l == 0 else ())
        saved.append(sv)
    loss, dh, dfg = final_loss(h, final_g, target)
    bigs, smalls, dmods = [None] * len(layers), [None] * len(layers), [None] * len(layers)
    for l in reversed(range(len(layers))):
        dh, bigs[l], smalls[l], dmods[l] = _layer_bwd(dh, saved[l], layers[l], mods[l], tabs, ctx_len, l % 2 == 1,
                                                       f"l{l}_")
    return loss, dh[ctx_len:], bigs, smalls, dmods, dfg


def _place():
    x, y, c = lax.axis_index("x"), lax.axis_index("y"), lax.axis_index("c")
    chips = [(1 - x, y), (x, 1 - y), (1 - x, 1 - y)]
    return x, y, c, chips


def _remote(src, dst, send_sem, recv_sem, to):
    return pltpu.make_async_remote_copy(src_ref=src, dst_ref=dst, send_sem=send_sem, recv_sem=recv_sem,
                                        device_id=to, device_id_type=MESH)


ANY = pl.BlockSpec(memory_space=pl.ANY)


def all_gather8(x_shard, name):
    m_per, n = x_shard.shape

    def body(x_ref, out_ref, send_sems, recv_sems, local_sem):
        x, y, c, chips = _place()
        me, sibling = (x, y, c), (x, y, 1 - c)

        def rows(px, py, pc):
            return out_ref.at[pl.ds((4 * px + 2 * py + pc) * m_per, m_per), :]

        def copy(k, block, to, src=None):
            return _remote(rows(*block) if src is None else src, rows(*block), send_sems.at[k], recv_sems.at[k], to)

        mine = pltpu.make_async_copy(x_ref, rows(*me), local_sem)
        mine.start()
        first = [copy(0, me, sibling, src=x_ref)]
        first += [copy(1 + j, me, (*chip, c), src=x_ref) for j, chip in enumerate(chips)]
        for cp in first:
            cp.start()
        passed = [copy(4 + j, (*chip, c), sibling) for j, chip in enumerate(chips)]
        for j, chip in enumerate(chips):
            copy(1 + j, (*chip, c), me).wait_recv()
            passed[j].start()
        copy(0, sibling, me).wait_recv()
        for j, chip in enumerate(chips):
            copy(4 + j, (*chip, 1 - c), me).wait_recv()
        for cp in first + passed:
            cp.wait_send()
        mine.wait()

    return _pcall(
        body, name=name,
        out_shape=jax.ShapeDtypeStruct((N_DEV * m_per, n), x_shard.dtype),
        in_specs=[pl.BlockSpec(memory_space=pltpu.VMEM)],
        out_specs=pl.BlockSpec(memory_space=pltpu.VMEM),
        scratch_shapes=[pltpu.SemaphoreType.DMA((7,)), pltpu.SemaphoreType.DMA((7,)), pltpu.SemaphoreType.DMA],
        compiler_params=pltpu.CompilerParams(vmem_limit_bytes=VMEM_LIMIT),
    )(x_shard)


def _row_tile(rows):
    for tr in (256, 128, 64, 32, 16):
        if rows % tr == 0:
            return tr
    return rows


def place_shard(w, p, name):
    r, cols = w.shape
    tr = _row_tile(r)

    def body(p_ref, w_ref, o_ref):
        del p_ref
        o_ref[...] = w_ref[...].astype(BF16)

    return _pcall(
        body, name=name,
        grid_spec=pltpu.PrefetchScalarGridSpec(
            num_scalar_prefetch=1, grid=(r // tr,),
            in_specs=[pl.BlockSpec((tr, cols), lambda i, p_ref: (i, 0))],
            out_specs=pl.BlockSpec((None, tr, cols), lambda i, p_ref: (p_ref[0], i, 0))),
        out_shape=jax.ShapeDtypeStruct((N_SHARD, r, cols), BF16),
        compiler_params=_params("parallel"),
    )(p.reshape(1).astype(jnp.int32), w)


def gather_weight_shards(stacks):
    n = len(stacks)

    def body(*refs):
        outs = refs[n:2 * n]
        send_sems, recv_sems = refs[2 * n:]
        x, y, c, chips = _place()
        p = 2 * x + y
        sibling = (x, y, 1 - c)
        started = []
        for i in range(n):
            hr = outs[i].shape[1] // 2
            mine = outs[i].at[p, pl.ds(c * hr, hr)]
            for j, chip in enumerate(chips):
                cp = _remote(mine, mine, send_sems.at[i, j], recv_sems.at[i, j], (*chip, c))
                cp.start()
                started.append(cp)
        for i in range(n):
            hr = outs[i].shape[1] // 2
            for j, (px, py) in enumerate(chips):
                half = outs[i].at[2 * px + py, pl.ds(c * hr, hr)]
                _remote(half, half, send_sems.at[i, j], recv_sems.at[i, j], (px, py, c)).wait_recv()
                fwd = _remote(half, half, send_sems.at[i, 3 + j], recv_sems.at[i, 3 + j], sibling)
                fwd.start()
                started.append(fwd)
        for i in range(n):
            hr = outs[i].shape[1] // 2
            for j, (px, py) in enumerate(chips):
                other = outs[i].at[2 * px + py, pl.ds((1 - c) * hr, hr)]
                _remote(other, other, send_sems.at[i, 3 + j], recv_sems.at[i, 3 + j], sibling).wait_recv()
        for cp in started:
            cp.wait_send()

    return _pcall(
        body, name="gather_weight_shards",
        out_shape=[jax.ShapeDtypeStruct(s.shape, s.dtype) for s in stacks],
        in_specs=[ANY] * n, out_specs=[ANY] * n,
        input_output_aliases={i: i for i in range(n)},
        scratch_shapes=[pltpu.SemaphoreType.DMA((n, 6)), pltpu.SemaphoreType.DMA((n, 6))],
    )(*stacks)


HBM_SPEC = pl.BlockSpec(memory_space=pltpu.HBM)
SEM_SPEC = pl.BlockSpec(memory_space=pltpu.SEMAPHORE)
EFFECT = pltpu.SideEffectType.DATAFLOW_SIDE_EFFECTING


def gather_direct_start(stacks):
    n = len(stacks)

    def body(*refs):
        ins = refs[:n]
        send_sems, recv_sems = refs[n], refs[n + 1]
        token = refs[2 * n + 2]
        x, y, c, chips = _place()
        p = 2 * x + y
        for i in range(n):
            mine = ins[i].at[p]
            for j, chip in enumerate(chips):
                _remote(mine, mine, send_sems.at[3 * i + j], recv_sems.at[3 * i + j], (*chip, c)).start()
        token[...] = jnp.zeros_like(token)

    res = _pcall(
        body, name="gather_direct_start",
        out_shape=(pltpu.SemaphoreType.DMA((3 * n,)), pltpu.SemaphoreType.DMA((3 * n,)),
                   *[pltpu.HBM(s.shape, s.dtype) for s in stacks], jax.ShapeDtypeStruct((8, 128), F32)),
        in_specs=[HBM_SPEC] * n,
        out_specs=(SEM_SPEC, SEM_SPEC, *[HBM_SPEC] * n, pl.BlockSpec(memory_space=pltpu.VMEM)),
        input_output_aliases={i: 2 + i for i in range(n)},
        compiler_params=pltpu.CompilerParams(has_side_effects=EFFECT),
    )(*[pltpu.with_memory_space_constraint(s, pltpu.HBM) for s in stacks])
    return res[0], res[1], list(res[2:2 + n]), res[2 + n]


def gather_direct_wait(send_sems, recv_sems, stacks, after):
    n = len(stacks)

    def body(*refs):
        ins = refs[:n]
        send_sems, recv_sems = refs[n], refs[n + 1]
        x, y, c, chips = _place()
        p = 2 * x + y
        for i in range(n):
            for j, (px, py) in enumerate(chips):
                cp = _remote(ins[i].at[p], ins[i].at[2 * px + py], send_sems.at[3 * i + j], recv_sems.at[3 * i + j],
                             (px, py, c))
                cp.wait_send()
                cp.wait_recv()

    return _pcall(
        body, name="gather_direct_wait",
        out_shape=[pltpu.HBM(s.shape, s.dtype) for s in stacks],
        in_specs=[HBM_SPEC] * n + [SEM_SPEC, SEM_SPEC, ANY],
        out_specs=[HBM_SPEC] * n,
        input_output_aliases={i: i for i in range(n)},
        compiler_params=pltpu.CompilerParams(has_side_effects=EFFECT),
    )(*stacks, send_sems, recv_sems, after)


def exchange_sibling_halves(grads):
    n = len(grads)

    def body(*refs):
        ins, outs = refs[:n], refs[n:2 * n]
        send_sems, recv_sems = refs[2 * n:]
        x, y, c, _ = _place()
        copies = []
        for i in range(n):
            hr = ins[i].shape[1] // 2
            cp = _remote(ins[i].at[:, pl.ds((1 - c) * hr, hr), :], outs[i], send_sems.at[i], recv_sems.at[i],
                         (x, y, 1 - c))
            cp.start()
            copies.append(cp)
        for cp in copies:
            cp.wait()

    return _pcall(
        body, name="exchange_sibling_halves",
        out_shape=[jax.ShapeDtypeStruct((g.shape[0], g.shape[1] // 2, g.shape[2]), g.dtype) for g in grads],
        in_specs=[ANY] * n, out_specs=[ANY] * n,
        scratch_shapes=[pltpu.SemaphoreType.DMA((n,)), pltpu.SemaphoreType.DMA((n,))],
    )(*grads)


def exchange_chip_quarters(parts):
    n = len(parts)

    def body(*refs):
        ins, outs = refs[:n], refs[n:2 * n]
        send_sems, recv_sems = refs[2 * n:]
        _, _, c, chips = _place()
        sends = []
        for i in range(n):
            for j, (px, py) in enumerate(chips):
                cp = _remote(ins[i].at[2 * px + py], outs[i].at[j], send_sems.at[i, j], recv_sems.at[i, j], (px, py, c))
                cp.start()
                sends.append(cp)
        for i in range(n):
            for j, (px, py) in enumerate(chips):
                slot = outs[i].at[j]
                _remote(slot, slot, send_sems.at[i, j], recv_sems.at[i, j], (px, py, c)).wait_recv()
        for cp in sends:
            cp.wait_send()

    return _pcall(
        body, name="exchange_chip_quarters",
        out_shape=[jax.ShapeDtypeStruct((3,) + g.shape[1:], g.dtype) for g in parts],
        in_specs=[ANY] * n, out_specs=[ANY] * n,
        scratch_shapes=[pltpu.SemaphoreType.DMA((n, 3)), pltpu.SemaphoreType.DMA((n, 3))],
    )(*parts)


def exchange_final_halves(grads):
    n = len(grads)

    def body(*refs):
        outs = refs[n:2 * n]
        send_sems, recv_sems = refs[2 * n:]
        x, y, c, _ = _place()
        sends = []
        for i in range(n):
            hr = outs[i].shape[0] // 2
            mine = outs[i].at[pl.ds(c * hr, hr)]
            cp = _remote(mine, mine, send_sems.at[i], recv_sems.at[i], (x, y, 1 - c))
            cp.start()
            sends.append(cp)
        for i in range(n):
            hr = outs[i].shape[0] // 2
            other = outs[i].at[pl.ds((1 - c) * hr, hr)]
            _remote(other, other, send_sems.at[i], recv_sems.at[i], (x, y, 1 - c)).wait_recv()
        for cp in sends:
            cp.wait_send()

    return _pcall(
        body, name="exchange_final_halves",
        out_shape=[jax.ShapeDtypeStruct(g.shape, g.dtype) for g in grads],
        in_specs=[ANY] * n, out_specs=[ANY] * n,
        input_output_aliases={i: i for i in range(n)},
        scratch_shapes=[pltpu.SemaphoreType.DMA((n,)), pltpu.SemaphoreType.DMA((n,))],
    )(*grads)


def add_own_half(g, recv, c, name):
    s, r, cols = g.shape
    hr = r // 2
    tr = _row_tile(hr)
    nb = hr // tr

    def body(c_ref, g_ref, r_ref, o_ref):
        del c_ref
        o_ref[...] = (g_ref[...] + r_ref[...]).astype(BF16)

    return _pcall(
        body, name=name,
        grid_spec=pltpu.PrefetchScalarGridSpec(
            num_scalar_prefetch=1, grid=(s, nb),
            in_specs=[pl.BlockSpec((None, tr, cols), lambda j, i, c_ref: (j, c_ref[0] * nb + i, 0)),
                      pl.BlockSpec((None, tr, cols), lambda j, i, c_ref: (j, i, 0))],
            out_specs=pl.BlockSpec((None, tr, cols), lambda j, i, c_ref: (j, i, 0))),
        out_shape=jax.ShapeDtypeStruct((s, hr, cols), BF16),
        compiler_params=_params("parallel", "parallel"),
    )(c.reshape(1).astype(jnp.int32), g, recv)


def add_quarters(own, others, p, c, name):
    _, hr, cols = own.shape
    tr = _row_tile(hr)
    nb = hr // tr

    def body(pc_ref, own_ref, oth_ref, o_ref):
        del pc_ref
        acc = own_ref[...].astype(F32) + oth_ref[0].astype(F32)
        acc = acc + oth_ref[1].astype(F32)
        o_ref[...] = acc + oth_ref[2].astype(F32)

    return _pcall(
        body, name=name,
        grid_spec=pltpu.PrefetchScalarGridSpec(
            num_scalar_prefetch=1, grid=(nb,),
            in_specs=[pl.BlockSpec((None, tr, cols), lambda i, pc: (pc[0], i, 0)),
                      pl.BlockSpec((3, tr, cols), lambda i, pc: (0, i, 0))],
            out_specs=pl.BlockSpec((tr, cols), lambda i, pc: (pc[1] * nb + i, 0))),
        out_shape=jax.ShapeDtypeStruct((2 * hr, cols), F32),
        compiler_params=_params("parallel"),
    )(jnp.stack([p, c]).astype(jnp.int32), own, others)


def _adamw_math(w, g, m, v):
    m2 = ADAM_B1 * m + (1.0 - ADAM_B1) * g
    v2 = ADAM_B2 * v + (1.0 - ADAM_B2) * (g * g)
    m_hat = m2 / (1.0 - ADAM_B1 ** ADAM_STEP)
    v_hat = v2 / (1.0 - ADAM_B2 ** ADAM_STEP)
    delta = -ADAM_LR * (m_hat / (jnp.sqrt(v_hat) + ADAM_EPS) + ADAM_WD * w)
    return delta, m2, v2


def adamw(w, g, m, v, name):
    r, cols = w.shape
    tr = 128 if r % 128 == 0 else r

    def body(w_ref, g_ref, m_ref, v_ref, d_ref, m2_ref, v2_ref):
        d_ref[...], m2_ref[...], v2_ref[...] = _adamw_math(w_ref[...], g_ref[...], m_ref[...], v_ref[...])

    blk = pl.BlockSpec((tr, cols), lambda i: (i, 0))
    return _pcall(
        body, name=name, grid=(r // tr,),
        in_specs=[blk] * 4, out_specs=[blk] * 3,
        out_shape=[jax.ShapeDtypeStruct((r, cols), F32)] * 3,
        compiler_params=_params("parallel"),
    )(w, g, m, v)


def small_update(gathered, w, m, v):
    nd, r, lanes = gathered.shape

    def body(ga_ref, w_ref, m_ref, v_ref, g_ref, d_ref, m2_ref, v2_ref):
        g = ga_ref[0] + ga_ref[1]
        for dev in range(2, nd):
            g = g + ga_ref[dev]
        g_ref[...] = g
        d_ref[...], m2_ref[...], v2_ref[...] = _adamw_math(w_ref[...], g, m_ref[...], v_ref[...])

    return _pcall(
        body, name="small_update",
        out_shape=[jax.ShapeDtypeStruct((r, lanes), F32)] * 4,
        compiler_params=pltpu.CompilerParams(vmem_limit_bytes=VMEM_LIMIT),
    )(gathered, w, m, v)


def mod_fwd(c16, w_mod, b_mod, name):
    d, ns = w_mod.shape
    tn = 512

    def body(c_ref, w_ref, b_ref, o_ref):
        cv = c_ref[...]
        sc = (cv * _sigmoid(cv)).astype(BF16)
        o_ref[...] = _dot(sc, w_ref[...].astype(BF16)) + b_ref[...]

    return _pcall(
        body, name=name, grid=(ns // tn,),
        in_specs=[pl.BlockSpec((16, d), lambda j: (0, 0)), pl.BlockSpec((d, tn), lambda j: (0, j)),
                  pl.BlockSpec((1, tn), lambda j: (0, j))],
        out_specs=pl.BlockSpec((16, tn), lambda j: (0, j)),
        out_shape=jax.ShapeDtypeStruct((16, ns), F32),
        compiler_params=_params("parallel"),
    )(c16, w_mod, b_mod)


def wmod_update(c16, dmod16, w, m, v, name):
    d, ns = w.shape
    tn = 256

    def body(c_ref, dm_ref, w_ref, m_ref, v_ref, g_ref, d_ref, m2_ref, v2_ref):
        cv = c_ref[...]
        sc = (cv * _sigmoid(cv)).astype(BF16)
        g = _dot_tn(sc, dm_ref[...].astype(BF16))
        g_ref[...] = g
        d_ref[...], m2_ref[...], v2_ref[...] = _adamw_math(w_ref[...], g, m_ref[...], v_ref[...])

    blk = pl.BlockSpec((d, tn), lambda j: (0, j))
    return _pcall(
        body, name=name, grid=(ns // tn,),
        in_specs=[pl.BlockSpec((16, d), lambda j: (0, 0)), pl.BlockSpec((16, tn), lambda j: (0, j)), blk, blk, blk],
        out_specs=[blk] * 4,
        out_shape=[jax.ShapeDtypeStruct((d, ns), F32)] * 4,
        compiler_params=_params("parallel"),
    )(c16, dmod16, w, m, v)


def cctx_partial(dm0, w0, dm1, w1):
    d, ns = w0.shape
    tn = 512

    def body(dm0_ref, w0_ref, dm1_ref, w1_ref, o_ref):
        @pl.when(pl.program_id(0) == 0)
        def _():
            o_ref[...] = jnp.zeros_like(o_ref)

        for dm_ref, w_ref in ((dm0_ref, w0_ref), (dm1_ref, w1_ref)):
            tot = jnp.sum(dm_ref[...], axis=0, keepdims=True)
            lhs = jnp.broadcast_to(tot, (8, tn)).astype(BF16)
            o_ref[...] += _dot_nt(lhs, w_ref[...].astype(BF16))

    dspec = pl.BlockSpec((8, tn), lambda j: (0, j))
    wspec = pl.BlockSpec((d, tn), lambda j: (0, j))
    return _pcall(
        body, name="cctx_partial", grid=(ns // tn,),
        in_specs=[dspec, wspec, dspec, wspec],
        out_specs=pl.BlockSpec((8, d), lambda j: (0, 0)),
        out_shape=jax.ShapeDtypeStruct((8, d), F32),
        compiler_params=_params("arbitrary"),
    )(dm0, w0, dm1, w1)


def cctx_update(parts, c_ctx, m, v):
    d = c_ctx.shape[1]

    def body(p_ref, c_ref, m_ref, v_ref, g_ref, d_ref, m2_ref, v2_ref):
        tot = p_ref[0:1, :] + p_ref[1:2, :]
        tot = tot + p_ref[2:3, :]
        tot = tot + p_ref[3:4, :]
        cv = c_ref[...]
        sg = _sigmoid(cv)
        g = tot * (sg * (1.0 + cv * (1.0 - sg)))
        g_ref[...] = g
        d_ref[...], m2_ref[...], v2_ref[...] = _adamw_math(cv, g, m_ref[...], v_ref[...])

    return _pcall(
        body, name="cctx_update",
        out_shape=[jax.ShapeDtypeStruct((1, d), F32)] * 4,
    )(parts, c_ctx, m, v)


WEIGHT_NAMES = (
    "c_ctx", "l0_norm1_g", "l0_w_mod", "l0_b_mod", "l0_w_in", "l0_q_norm_g", "l0_k_norm_g", "l0_conv_w", "l0_w_out",
    "l0_norm2_g", "l0_w_gate", "l0_w_up", "l0_w_down", "l1_norm1_g", "l1_w_mod", "l1_b_mod", "l1_w_in",
    "l1_q_norm_g", "l1_k_norm_g", "l1_sink", "l1_pool_w", "l1_pool_scale", "l1_w_out", "l1_norm2_g", "l1_w_gate",
    "l1_w_up", "l1_w_down", "final_norm_g")
BIG_NAMES = ("w_in", "w_out", "w_gate", "w_up", "w_down")
SMALL_SLOTS = tuple(
    [(f"l{l}_{nm}", par and f"l{l}_{nm}") for l in (0, 1)
     for nm, par in (("dmod_lat", False), ("dmod_ctx", False), ("b_mod", True), ("norm1_g", True), ("norm2_g", True),
                     ("q_norm_g", True), ("k_norm_g", True))]
    + [("l0_conv_w", None), ("l1_sink", "l1_sink"), ("l1_pool_w", "l1_pool_w"), ("l1_pool_scale", "l1_pool_scale"),
       ("final_norm_g", "final_norm_g")])


def _pack(values, sizes):
    parts = []
    for (name, _), size in zip(SMALL_SLOTS, sizes):
        v = values.get(name)
        padded = -(-size // 128) * 128
        v = jnp.zeros((padded,), F32) if v is None else jnp.pad(v.reshape(-1).astype(F32), (0, padded - size))
        parts.append(v)
    total = sum(p.shape[0] for p in parts)
    parts.append(jnp.zeros((-(-total // 1024) * 1024 - total,), F32))
    return jnp.concatenate(parts).reshape(-1, 128)


def _offsets(sizes):
    offs, o = {}, 0
    for (name, _), size in zip(SMALL_SLOTS, sizes):
        offs[name] = (o, size)
        o += -(-size // 128) * 128
    return offs


def kernel(x, c, ctx, c_ctx, l0_norm1_g, l0_w_mod, l0_b_mod, l0_w_in, l0_q_norm_g, l0_k_norm_g, l0_conv_w, l0_w_out, l0_norm2_g, l0_w_gate, l0_w_up, l0_w_down, l1_norm1_g, l1_w_mod, l1_b_mod, l1_w_in, l1_q_norm_g, l1_k_norm_g, l1_sink, l1_pool_w, l1_pool_scale, l1_w_out, l1_norm2_g, l1_w_gate, l1_w_up, l1_w_down, final_norm_g, loss_target, m_c_ctx, m_l0_norm1_g, m_l0_w_mod, m_l0_b_mod, m_l0_w_in, m_l0_q_norm_g, m_l0_k_norm_g, m_l0_conv_w, m_l0_w_out, m_l0_norm2_g, m_l0_w_gate, m_l0_w_up, m_l0_w_down, m_l1_norm1_g, m_l1_w_mod, m_l1_b_mod, m_l1_w_in, m_l1_q_norm_g, m_l1_k_norm_g, m_l1_sink, m_l1_pool_w, m_l1_pool_scale, m_l1_w_out, m_l1_norm2_g, m_l1_w_gate, m_l1_w_up, m_l1_w_down, m_final_norm_g, v_c_ctx, v_l0_norm1_g, v_l0_w_mod, v_l0_b_mod, v_l0_w_in, v_l0_q_norm_g, v_l0_k_norm_g, v_l0_conv_w, v_l0_w_out, v_l0_norm2_g, v_l0_w_gate, v_l0_w_up, v_l0_w_down, v_l1_norm1_g, v_l1_w_mod, v_l1_b_mod, v_l1_w_in, v_l1_q_norm_g, v_l1_k_norm_g, v_l1_sink, v_l1_pool_w, v_l1_pool_scale, v_l1_w_out, v_l1_norm2_g, v_l1_w_gate, v_l1_w_up, v_l1_w_down, v_final_norm_g):
    a = dict(locals())
    xi, yi, ci = lax.axis_index("x"), lax.axis_index("y"), lax.axis_index("c")
    p = 2 * xi + yi
    me = 4 * xi + 2 * yi + ci
    d = x.shape[-1]
    conv_cols = l0_conv_w.shape[1]

    row0 = jnp.concatenate([c, jnp.pad(l0_conv_w, ((0, 0), (0, d - conv_cols))), jnp.zeros((4, d), F32)], axis=0)
    gathered = all_gather8(row0, "gather_cond").reshape(N_DEV, 8, d)
    c_all = gathered[:, 0]
    conv_w = gathered[0::2, 1:4, :conv_cols].transpose(1, 0, 2).reshape(3, N_SHARD * conv_cols)
    c16_fwd = jnp.concatenate([c_all, c_ctx[None], jnp.zeros((7, d), F32)], axis=0)
    c16_bwd = jnp.concatenate([c_all, jnp.broadcast_to(c_ctx[None], (8, d))], axis=0)

    ns_mod = l0_w_mod.shape[1]
    mod_parts = [mod_fwd(c16_fwd, a[f"l{l}_w_mod"], lax.dynamic_slice(a[f"l{l}_b_mod"], (p * ns_mod,), (ns_mod,))[None],
                         f"l{l}_mod_fwd") for l in (0, 1)]
    modg = all_gather8(jnp.concatenate(mod_parts, axis=0), "gather_mod").reshape(N_DEV, 2, 16, ns_mod)[0::2]
    mod_full = modg.transpose(1, 2, 0, 3).reshape(2, 16, N_SHARD * ns_mod)
    mods = []
    for l in (0, 1):
        lat = lax.dynamic_index_in_dim(mod_full[l], me, axis=0, keepdims=False)
        cx = mod_full[l, 8]
        mods.append({nm: jnp.stack([cx[j * d:(j + 1) * d], lat[j * d:(j + 1) * d]])[:, None, :]
                     for j, nm in enumerate(MOD_NAMES)})

    placed = [[place_shard(a[f"l{l}_{nm}"], p, f"l{l}_{nm}_place") for nm in BIG_NAMES] for l in (0, 1)]
    stacks0 = gather_weight_shards(placed[0])
    send_sems, recv_sems, in_flight, token = gather_direct_start(placed[1])

    def layer_weights(l, stacks):
        lw = dict(zip(BIG_NAMES, stacks))
        lw.update(norm1_g=a[f"l{l}_norm1_g"][None], norm2_g=a[f"l{l}_norm2_g"][None],
                  q_g=a[f"l{l}_q_norm_g"][None], k_g=a[f"l{l}_k_norm_g"][None])
        return lw

    def layer0(h):
        return dict(layer_weights(0, stacks0), conv_w=conv_w, sink=jnp.zeros((Q_HEADS,), F32))

    def layer1(h):
        stacks1 = gather_direct_wait(send_sems, recv_sems, in_flight, h)
        return dict(layer_weights(1, stacks1), sink=l1_sink, pool_w=l1_pool_w, pool_scale=l1_pool_scale[None])

    loss_tile, dx, bigs, smalls, dmods, dfg = _local_step(x[0], ctx[0], loss_target[0], (layer0, layer1), mods,
                                                          final_norm_g[None], first_deps=(token,))

    partial = {"l0_conv_w": smalls[0]["conv_w"], "l1_sink": smalls[1]["sink"], "l1_pool_w": smalls[1]["pool_w"],
               "l1_pool_scale": smalls[1]["pool_scale"], "final_norm_g": dfg}
    for l in (0, 1):
        partial.update({f"l{l}_dmod_lat": dmods[l][1], f"l{l}_dmod_ctx": dmods[l][0],
                        f"l{l}_b_mod": dmods[l][0] + dmods[l][1], f"l{l}_norm1_g": smalls[l]["norm1_g"],
                        f"l{l}_norm2_g": smalls[l]["norm2_g"], f"l{l}_q_norm_g": smalls[l]["q_g"],
                        f"l{l}_k_norm_g": smalls[l]["k_g"]})
    sizes = [int(np.prod(partial[name].shape)) for name, _ in SMALL_SLOTS]
    offs = _offsets(sizes)
    gpack = _pack(partial, sizes)
    rows = gpack.shape[0]
    small_all = all_gather8(gpack, "gather_small").reshape(N_DEV, rows, 128)
    packs = [_pack({name: a[pre + par] for name, par in SMALL_SLOTS if par}, sizes) for pre in ("", "m_", "v_")]
    small_out = [t.reshape(-1) for t in small_update(small_all, *packs)]
    flat_all = small_all.reshape(N_DEV, rows * 128)

    def slot(flat, name, shape):
        o, size = offs[name]
        return flat[o:o + size].reshape(shape)

    results = {}
    for name, par in SMALL_SLOTS:
        if par:
            results[par] = tuple(slot(t, name, a[par].shape) for t in small_out)

    ctx_rows = []
    for l in (0, 1):
        o_lat, o_ctx = offs[f"l{l}_dmod_lat"][0], offs[f"l{l}_dmod_ctx"][0]
        lat = lax.dynamic_slice(flat_all, (0, o_lat + p * ns_mod), (N_DEV, ns_mod))
        cxr = lax.dynamic_slice(flat_all, (0, o_ctx + p * ns_mod), (N_DEV, ns_mod))
        ctx_rows.append(cxr)
        results[f"l{l}_w_mod"] = tuple(wmod_update(c16_bwd, jnp.concatenate([lat, cxr], axis=0), a[f"l{l}_w_mod"],
                                                   a[f"m_l{l}_w_mod"], a[f"v_l{l}_w_mod"], f"l{l}_w_mod_update"))
    part = cctx_partial(ctx_rows[0], l0_w_mod, ctx_rows[1], l1_w_mod)
    parts4 = all_gather8(part, "gather_cctx").reshape(N_DEV, 8, d)[0::2, 0]
    results["c_ctx"] = tuple(t[0] for t in cctx_update(parts4, c_ctx[None], m_c_ctx[None], v_c_ctx[None]))

    gconv = lax.dynamic_slice(slot(small_out[0], "l0_conv_w", (3, N_SHARD * conv_cols)), (0, p * conv_cols),
                              (3, conv_cols))
    results["l0_conv_w"] = (gconv,) + tuple(adamw(l0_conv_w, gconv, m_l0_conv_w, v_l0_conv_w, "l0_conv_w_adamw"))

    keys = [f"l{l}_{nm}" for l in (0, 1) for nm in BIG_NAMES]
    grads = [bigs[l][nm] for l in (0, 1) for nm in BIG_NAMES]
    recv = exchange_sibling_halves(grads)
    chip_sums = [add_own_half(g, r, ci, k + "_add_halves") for k, g, r in zip(keys, grads, recv)]
    quarters = exchange_chip_quarters(chip_sums)
    halves = [add_quarters(own, q, p, ci, k + "_add_quarters") for k, own, q in zip(keys, chip_sums, quarters)]
    fulls = exchange_final_halves(halves)
    for k, g in zip(keys, fulls):
        results[k] = (g,) + tuple(adamw(a[k], g, a["m_" + k], a["v_" + k], k + "_adamw"))

    loss = lax.psum(loss_tile[0, 0], ("x", "y", "c"))
    out = [loss, dx[None]]
    for j in range(4):
        out += [results[k][j] for k in WEIGHT_NAMES]
    return tuple(out)
```

```python
import numpy as np
import jax
import jax.numpy as jnp
from jax import lax
from jax.experimental import pallas as pl
from jax.experimental.pallas import tpu as pltpu

F32 = jnp.float32
BF16 = jnp.bfloat16
MESH = pl.DeviceIdType.MESH

HEAD_DIM = 128
Q_HEADS = 12
KV_HEADS = 4
GROUPS = Q_HEADS // KV_HEADS
ATT_WIDTH = Q_HEADS * HEAD_DIM
KV_WIDTH = KV_HEADS * HEAD_DIM
QKV_WIDTH = ATT_WIDTH + 2 * KV_WIDTH
AUX_WIDTH = 512
AUX_GROUPS = 4
POOL_WINDOWS = (2, 4, 8, 16)
WINDOW = 128
GRID_W = 64
ROPE_THETA = 10000.0
N_MOD = 6
EPS = 1e-6
NEG_INF = -1e30
ATT_SCALE = HEAD_DIM ** -0.5
N_SHARD = 4
N_DEV = 8

ADAM_LR = 0.001
ADAM_B1 = 0.9
ADAM_B2 = 0.999
ADAM_EPS = 1e-08
ADAM_WD = 0.01
ADAM_STEP = 10

TM = 256
BAND = TM + 2 * WINDOW
VMEM_LIMIT = 56 * 1024 * 1024

NT_DIMS = (((1,), (1,)), ((), ()))
TN_DIMS = (((0,), (0,)), ((), ()))


def _pcall(body, **kw):
    return pl.pallas_call(body, **kw)


def _params(*sem):
    return pltpu.CompilerParams(dimension_semantics=sem, vmem_limit_bytes=VMEM_LIMIT)


def _sel(i):
    return jnp.minimum(i, 1)


def _sel_spec(d):
    return pl.BlockSpec((None, 1, d), lambda i: (_sel(i), 0, 0))


def _dot(a, b):
    return jnp.dot(a, b, preferred_element_type=F32)


def _dot_nt(a, b):
    return lax.dot_general(a, b, NT_DIMS, preferred_element_type=F32)


def _dot_tn(a, b):
    return lax.dot_general(a, b, TN_DIMS, preferred_element_type=F32)


def _sigmoid(x):
    return 1.0 / (1.0 + jnp.exp(-x))


def norm_mod_fwd(h, g, shift, scale, name):
    t, d = h.shape

    def body(h_ref, g_ref, sh_ref, sc_ref, o_ref):
        x = h_ref[...]
        r = lax.rsqrt(jnp.mean(x * x, axis=-1, keepdims=True) + EPS)
        y = x * r * g_ref[...]
        o_ref[...] = (y * (1.0 + sc_ref[...]) + sh_ref[...]).astype(BF16)

    return _pcall(
        body, name=name, grid=(t // TM,),
        in_specs=[pl.BlockSpec((TM, d), lambda i: (i, 0)), pl.BlockSpec((1, d), lambda i: (0, 0)),
                  _sel_spec(d), _sel_spec(d)],
        out_specs=pl.BlockSpec((TM, d), lambda i: (i, 0)),
        out_shape=jax.ShapeDtypeStruct((t, d), BF16),
        compiler_params=_params("parallel"),
    )(h, g, shift, scale)


def norm_mod_bwd(dy, h, g, scale, dres, name):
    t, d = h.shape

    def body(dy_ref, h_ref, g_ref, sc_ref, dres_ref, dh_ref, dsh_ref, dsc_ref, dg_ref):
        i = pl.program_id(0)

        @pl.when(i == 0)
        def _():
            dsh_ref[...] = jnp.zeros_like(dsh_ref)
            dsc_ref[...] = jnp.zeros_like(dsc_ref)
            dg_ref[...] = jnp.zeros_like(dg_ref)

        x = h_ref[...]
        gv = g_ref[...]
        r = lax.rsqrt(jnp.mean(x * x, axis=-1, keepdims=True) + EPS)
        xhat = x * r
        dyv = dy_ref[...]
        s = _sel(i)
        dsh_ref[pl.ds(s, 1)] += jnp.sum(dyv, axis=0, keepdims=True)[None]
        dsc_ref[pl.ds(s, 1)] += jnp.sum(dyv * xhat * gv, axis=0, keepdims=True)[None]
        tt = dyv * (1.0 + sc_ref[...])
        dg_ref[...] += jnp.sum(tt * xhat, axis=0, keepdims=True)
        dxhat = tt * gv
        dx = r * (dxhat - xhat * jnp.mean(dxhat * xhat, axis=-1, keepdims=True))
        dh_ref[...] = dx + dres_ref[...]

    row = pl.BlockSpec((TM, d), lambda i: (i, 0))
    acc2 = pl.BlockSpec((2, 1, d), lambda i: (0, 0, 0))
    return _pcall(
        body, name=name, grid=(t // TM,),
        in_specs=[row, row, pl.BlockSpec((1, d), lambda i: (0, 0)), _sel_spec(d), row],
        out_specs=[row, acc2, acc2, pl.BlockSpec((1, d), lambda i: (0, 0))],
        out_shape=[jax.ShapeDtypeStruct((t, d), F32), jax.ShapeDtypeStruct((2, 1, d), F32),
                   jax.ShapeDtypeStruct((2, 1, d), F32), jax.ShapeDtypeStruct((1, d), F32)],
        compiler_params=_params("arbitrary"),
    )(dy, h, g, scale, dres)


def gate_bwd(dh, f, gate, name):
    t, d = dh.shape

    def body(dh_ref, f_ref, gt_ref, df_ref, dgt_ref):
        i = pl.program_id(0)

        @pl.when(i == 0)
        def _():
            dgt_ref[...] = jnp.zeros_like(dgt_ref)

        dhv = dh_ref[...]
        df_ref[...] = (dhv * gt_ref[...]).astype(BF16)
        dgt_ref[pl.ds(_sel(i), 1)] += jnp.sum(dhv * f_ref[...], axis=0, keepdims=True)[None]

    row = pl.BlockSpec((TM, d), lambda i: (i, 0))
    return _pcall(
        body, name=name, grid=(t // TM,),
        in_specs=[row, row, _sel_spec(d)],
        out_specs=[row, pl.BlockSpec((2, 1, d), lambda i: (0, 0, 0))],
        out_shape=[jax.ShapeDtypeStruct((t, d), BF16), jax.ShapeDtypeStruct((2, 1, d), F32)],
        compiler_params=_params("arbitrary"),
    )(dh, f, gate)


def final_loss(h, fg, target):
    t, d = h.shape

    def body(h_ref, g_ref, tg_ref, loss_ref, dh_ref, dg_ref):
        i = pl.program_id(0)

        @pl.when(i == 0)
        def _():
            loss_ref[...] = jnp.zeros_like(loss_ref)
            dg_ref[...] = jnp.zeros_like(dg_ref)
            dh_ref[...] = jnp.zeros_like(dh_ref)

        @pl.when(i > 0)
        def _():
            x = h_ref[...]
            gv = g_ref[...]
            r = lax.rsqrt(jnp.mean(x * x, axis=-1, keepdims=True) + EPS)
            xhat = x * r
            diff = xhat * gv - tg_ref[...]
            loss_ref[...] += 0.5 * jnp.sum(jnp.mean(diff * diff, axis=-1, keepdims=True), axis=0, keepdims=True)
            dout = diff * (1.0 / d)
            dg_ref[...] += jnp.sum(dout * xhat, axis=0, keepdims=True)
            dxhat = dout * gv
            dh_ref[...] = r * (dxhat - xhat * jnp.mean(dxhat * xhat, axis=-1, keepdims=True))

    row = pl.BlockSpec((TM, d), lambda i: (i, 0))
    return _pcall(
        body, name="final_loss", grid=(t // TM,),
        in_specs=[row, pl.BlockSpec((1, d), lambda i: (0, 0)),
                  pl.BlockSpec((TM, d), lambda i: (jnp.maximum(i - 1, 0), 0))],
        out_specs=[pl.BlockSpec((8, 128), lambda i: (0, 0)), row, pl.BlockSpec((1, d), lambda i: (0, 0))],
        out_shape=[jax.ShapeDtypeStruct((8, 128), F32), jax.ShapeDtypeStruct((t, d), F32),
                   jax.ShapeDtypeStruct((1, d), F32)],
        compiler_params=_params("arbitrary"),
    )(h, fg, target)


def _w_nn_spec(w, k):
    if w.shape[1] == k:
        ns = w.shape[2]
        return w, pl.BlockSpec((None, k, ns), lambda j, i: (j, 0, 0)), w.shape[0], ns
    w2 = w.reshape(w.shape[0] * w.shape[1], w.shape[2])
    tn = 1024 if w2.shape[1] % 1024 == 0 else w2.shape[1]
    return w2, pl.BlockSpec((k, tn), lambda j, i: (0, j)), w2.shape[1] // tn, tn


def mm_nn(a, w, name, out_dtype=F32):
    t, k = a.shape
    w, wspec, nb, tn = _w_nn_spec(w, k)

    def body(a_ref, w_ref, o_ref):
        o_ref[...] = _dot(a_ref[...], w_ref[...]).astype(o_ref.dtype)

    return _pcall(
        body, name=name, grid=(nb, t // TM),
        in_specs=[pl.BlockSpec((TM, k), lambda j, i: (i, 0)), wspec],
        out_specs=pl.BlockSpec((TM, tn), lambda j, i: (i, j)),
        out_shape=jax.ShapeDtypeStruct((t, nb * tn), out_dtype),
        compiler_params=_params("parallel", "parallel"),
    )(a, w)


def mm_nn_residual(a, w, h, gate, name):
    t, k = a.shape
    w, wspec, nb, tn = _w_nn_spec(w, k)

    def body(a_ref, w_ref, h_ref, gt_ref, y_ref, o_ref):
        y = _dot(a_ref[...], w_ref[...])
        y_ref[...] = y
        o_ref[...] = h_ref[...] + gt_ref[...] * y

    out = pl.BlockSpec((TM, tn), lambda j, i: (i, j))
    return _pcall(
        body, name=name, grid=(nb, t // TM),
        in_specs=[pl.BlockSpec((TM, k), lambda j, i: (i, 0)), wspec, out,
                  pl.BlockSpec((None, 1, tn), lambda j, i: (_sel(i), 0, j))],
        out_specs=[out, out],
        out_shape=[jax.ShapeDtypeStruct((t, nb * tn), F32)] * 2,
        compiler_params=_params("parallel", "parallel"),
    )(a, w, h, gate)


def mm_swiglu(a, wg, wu, name):
    t, k = a.shape
    s, _, ns = wg.shape

    def body(a_ref, wg_ref, wu_ref, g_ref, u_ref, act_ref):
        av = a_ref[...]
        g = _dot(av, wg_ref[...])
        u = _dot(av, wu_ref[...])
        g_ref[...] = g
        u_ref[...] = u
        act_ref[...] = (g * _sigmoid(g) * u).astype(BF16)

    wspec = pl.BlockSpec((None, k, ns), lambda j, i: (j, 0, 0))
    out = pl.BlockSpec((TM, ns), lambda j, i: (i, j))
    return _pcall(
        body, name=name, grid=(s, t // TM),
        in_specs=[pl.BlockSpec((TM, k), lambda j, i: (i, 0)), wspec, wspec],
        out_specs=[out, out, out],
        out_shape=[jax.ShapeDtypeStruct((t, s * ns), F32), jax.ShapeDtypeStruct((t, s * ns), F32),
                   jax.ShapeDtypeStruct((t, s * ns), BF16)],
        compiler_params=_params("parallel", "parallel"),
    )(a, wg, wu)


def mm_nt_cols(dy, w, name, add=None):
    t, n = dy.shape
    s, k, ns = w.shape
    tk = 512

    def body(*refs):
        if add is None:
            dy_ref, w_ref, o_ref = refs
        else:
            dy_ref, w_ref, add_ref, o_ref = refs
        acc = _dot_nt(dy_ref[:, 0:ns], w_ref[0])
        for sh in range(1, s):
            acc += _dot_nt(dy_ref[:, sh * ns:(sh + 1) * ns], w_ref[sh])
        if add is not None:
            acc += add_ref[...]
        o_ref[...] = acc

    out = pl.BlockSpec((TM, tk), lambda j, i: (i, j))
    in_specs = [pl.BlockSpec((TM, n), lambda j, i: (i, 0)), pl.BlockSpec((s, tk, ns), lambda j, i: (0, j, 0))]
    args = [dy, w]
    if add is not None:
        in_specs.append(out)
        args.append(add)
    return _pcall(
        body, name=name, grid=(k // tk, t // TM),
        in_specs=in_specs, out_specs=out,
        out_shape=jax.ShapeDtypeStruct((t, k), F32),
        compiler_params=_params("parallel", "parallel"),
    )(*args)


def mm_nt_rows(dy, w, name):
    t, n = dy.shape
    s, ks, _ = w.shape

    def body(dy_ref, w_ref, o_ref):
        o_ref[...] = _dot_nt(dy_ref[...], w_ref[...])

    return _pcall(
        body, name=name, grid=(s, t // TM),
        in_specs=[pl.BlockSpec((TM, n), lambda j, i: (i, 0)), pl.BlockSpec((None, ks, n), lambda j, i: (j, 0, 0))],
        out_specs=pl.BlockSpec((TM, ks), lambda j, i: (i, j)),
        out_shape=jax.ShapeDtypeStruct((t, s * ks), F32),
        compiler_params=_params("parallel", "parallel"),
    )(dy, w)


def mm_nt_swiglu_bwd(df, wd, g, u, name):
    t, n = df.shape
    s, ks, _ = wd.shape

    def body(df_ref, w_ref, g_ref, u_ref, dg_ref, du_ref):
        da = _dot_nt(df_ref[...], w_ref[...])
        gv = g_ref[...]
        sg = _sigmoid(gv)
        silu = gv * sg
        dg_ref[...] = (da * u_ref[...] * (sg * (1.0 + gv * (1.0 - sg)))).astype(BF16)
        du_ref[...] = (da * silu).astype(BF16)

    blk = pl.BlockSpec((TM, ks), lambda j, i: (i, j))
    return _pcall(
        body, name=name, grid=(s, t // TM),
        in_specs=[pl.BlockSpec((TM, n), lambda j, i: (i, 0)), pl.BlockSpec((None, ks, n), lambda j, i: (j, 0, 0)),
                  blk, blk],
        out_specs=[blk, blk],
        out_shape=[jax.ShapeDtypeStruct((t, s * ks), BF16)] * 2,
        compiler_params=_params("parallel", "parallel"),
    )(df, wd, g, u)


def mm_tn(x, dy, name, col_sharded):
    t, k = x.shape
    n = dy.shape[1]
    tk = 512
    mc = t // 2 if (t // 2) % 128 == 0 else t
    if col_sharded:
        tn = n // N_SHARD
        out_shape = jax.ShapeDtypeStruct((N_SHARD, k, tn), F32)
        out_spec = pl.BlockSpec((None, tk, tn), lambda j, kb, m: (j, kb, 0))
    else:
        tn = 1024
        out_shape = jax.ShapeDtypeStruct((k, n), F32)
        out_spec = pl.BlockSpec((tk, tn), lambda j, kb, m: (kb, j))

    def body(x_ref, dy_ref, o_ref):
        m = pl.program_id(2)
        acc = _dot_tn(x_ref[...], dy_ref[...])

        @pl.when(m == 0)
        def _():
            o_ref[...] = acc

        @pl.when(m > 0)
        def _():
            o_ref[...] += acc

    out = _pcall(
        body, name=name, grid=(n // tn, k // tk, t // mc),
        in_specs=[pl.BlockSpec((mc, tk), lambda j, kb, m: (m, kb)), pl.BlockSpec((mc, tn), lambda j, kb, m: (m, j))],
        out_specs=out_spec, out_shape=out_shape,
        compiler_params=_params("parallel", "parallel", "arbitrary"),
    )(x, dy)
    return out if col_sharded else out.reshape(N_SHARD, k // N_SHARD, n)


def _swap32(y):
    right = pltpu.roll(y, 32, 1)
    left = pltpu.roll(y, 96, 1)
    lane = lax.broadcasted_iota(jnp.int32, y.shape, 1)
    return jnp.where((lane // 32) % 2 == 0, left, right)


def _rope_tables(t, ctx_len):
    s = t - ctx_len
    pos = np.arange(s)
    row = (pos // GRID_W).astype(np.float32)
    col = (pos % GRID_W).astype(np.float32)
    half = HEAD_DIM // 2
    inv = np.power(np.float32(ROPE_THETA), -np.arange(0, half, 2, dtype=np.float32) / np.float32(half))
    ar = row[:, None] * inv
    ac = col[:, None] * inv
    cos = np.concatenate([np.cos(ar), np.cos(ar), np.cos(ac), np.cos(ac)], axis=1)
    sin = np.concatenate([-np.sin(ar), np.sin(ar), -np.sin(ac), np.sin(ac)], axis=1)
    cos = np.concatenate([np.ones((ctx_len, HEAD_DIM), np.float32), cos.astype(np.float32)], axis=0)
    sin = np.concatenate([np.zeros((ctx_len, HEAD_DIM), np.float32), sin.astype(np.float32)], axis=0)
    return jnp.asarray(cos, F32), jnp.asarray(sin, F32)


def qk_prep_fwd(proj, qg, kg, cos, sin, name):
    t = proj.shape[0]

    def body(p_ref, qg_ref, kg_ref, c_ref, s_ref, q_ref, k_ref, v_ref):
        cv = c_ref[...]
        sv = s_ref[...]
        for hd in range(Q_HEADS + KV_HEADS):
            x = p_ref[:, hd * HEAD_DIM:(hd + 1) * HEAD_DIM]
            gv = qg_ref[...] if hd < Q_HEADS else kg_ref[...]
            y = x * lax.rsqrt(jnp.mean(x * x, axis=-1, keepdims=True) + EPS) * gv
            out = (y * cv + _swap32(y) * sv).astype(BF16)
            if hd < Q_HEADS:
                q_ref[:, hd * HEAD_DIM:(hd + 1) * HEAD_DIM] = out
            else:
                k_ref[:, (hd - Q_HEADS) * HEAD_DIM:(hd - Q_HEADS + 1) * HEAD_DIM] = out
        v_ref[...] = p_ref[:, ATT_WIDTH + KV_WIDTH:QKV_WIDTH].astype(BF16)

    vec = pl.BlockSpec((1, HEAD_DIM), lambda i: (0, 0))
    tab = pl.BlockSpec((TM, HEAD_DIM), lambda i: (i, 0))
    return _pcall(
        body, name=name, grid=(t // TM,),
        in_specs=[pl.BlockSpec((TM, QKV_WIDTH), lambda i: (i, 0)), vec, vec, tab, tab],
        out_specs=[pl.BlockSpec((TM, ATT_WIDTH), lambda i: (i, 0)), pl.BlockSpec((TM, KV_WIDTH), lambda i: (i, 0)),
                   pl.BlockSpec((TM, KV_WIDTH), lambda i: (i, 0))],
        out_shape=[jax.ShapeDtypeStruct((t, ATT_WIDTH), BF16), jax.ShapeDtypeStruct((t, KV_WIDTH), BF16),
                   jax.ShapeDtypeStruct((t, KV_WIDTH), BF16)],
        compiler_params=_params("parallel"),
    )(proj, qg, kg, cos, sin)


def qk_prep_bwd(dq, dk, dv, daux, proj, qg, kg, cos, sin, name):
    t, n_in = proj.shape
    n_aux = daux.shape[0]

    def body(dq_ref, dk_ref, dv_ref, da_ref, p_ref, qg_ref, kg_ref, c_ref, s_ref, o_ref, dqg_ref, dkg_ref):
        @pl.when(pl.program_id(0) == 0)
        def _():
            dqg_ref[...] = jnp.zeros_like(dqg_ref)
            dkg_ref[...] = jnp.zeros_like(dkg_ref)

        cv = c_ref[...]
        sv = s_ref[...]
        for hd in range(Q_HEADS + KV_HEADS):
            cols = slice(hd * HEAD_DIM, (hd + 1) * HEAD_DIM)
            x = p_ref[:, cols]
            if hd < Q_HEADS:
                gv, dyr, dg_ref = qg_ref[...], dq_ref[:, cols], dqg_ref
            else:
                kc = slice((hd - Q_HEADS) * HEAD_DIM, (hd - Q_HEADS + 1) * HEAD_DIM)
                gv, dyr, dg_ref = kg_ref[...], dk_ref[:, kc], dkg_ref
            r = lax.rsqrt(jnp.mean(x * x, axis=-1, keepdims=True) + EPS)
            xhat = x * r
            dy = dyr * cv + _swap32(dyr * sv)
            dg_ref[...] += jnp.sum(dy * xhat, axis=0, keepdims=True)
            dxhat = dy * gv
            o_ref[:, cols] = (r * (dxhat - xhat * jnp.mean(dxhat * xhat, axis=-1, keepdims=True))).astype(BF16)
        o_ref[:, ATT_WIDTH + KV_WIDTH:QKV_WIDTH] = dv_ref[...].astype(BF16)
        for a in range(n_aux):
            o_ref[:, QKV_WIDTH + a * AUX_WIDTH:QKV_WIDTH + (a + 1) * AUX_WIDTH] = da_ref[a]

    vec = pl.BlockSpec((1, HEAD_DIM), lambda i: (0, 0))
    tab = pl.BlockSpec((TM, HEAD_DIM), lambda i: (i, 0))
    return _pcall(
        body, name=name, grid=(t // TM,),
        in_specs=[pl.BlockSpec((TM, ATT_WIDTH), lambda i: (i, 0)), pl.BlockSpec((TM, KV_WIDTH), lambda i: (i, 0)),
                  pl.BlockSpec((TM, KV_WIDTH), lambda i: (i, 0)),
                  pl.BlockSpec((n_aux, TM, AUX_WIDTH), lambda i: (0, i, 0)),
                  pl.BlockSpec((TM, QKV_WIDTH), lambda i: (i, 0)), vec, vec, tab, tab],
        out_specs=[pl.BlockSpec((TM, n_in), lambda i: (i, 0)), vec, vec],
        out_shape=[jax.ShapeDtypeStruct((t, n_in), BF16), jax.ShapeDtypeStruct((1, HEAD_DIM), F32),
                   jax.ShapeDtypeStruct((1, HEAD_DIM), F32)],
        compiler_params=_params("arbitrary"),
    )(dq, dk, dv, daux, proj, qg, kg, cos, sin)


def _band_start(i, t):
    return pl.multiple_of(jnp.clip(i * TM - WINDOW, 0, t - BAND), WINDOW)


def _band_mask(i, start, ctx_len):
    shape = (TM, ctx_len + BAND)
    col = lax.broadcasted_iota(jnp.int32, shape, 1)
    qrow = i * TM + lax.broadcasted_iota(jnp.int32, shape, 0)
    krow = start + col - ctx_len
    band_ok = (krow >= ctx_len) & (jnp.abs(krow - qrow) <= WINDOW)
    return (col < ctx_len) | band_ok


def attention_fwd(q, k, v, sink, ctx_len, windowed, name):
    t = q.shape[0]
    d_model = ATT_WIDTH + AUX_WIDTH
    gw = GROUPS * HEAD_DIM

    def one_head(qh, kk, vv, mask, sink_val):
        s = _dot_nt(qh, kk) * ATT_SCALE
        if mask is not None:
            s = jnp.where(mask, s, NEG_INF)
        m = jnp.max(s, axis=-1, keepdims=True)
        if sink_val is not None:
            m = jnp.maximum(m, sink_val)
        p = jnp.exp(s - m)
        l = jnp.sum(p, axis=-1, keepdims=True)
        if sink_val is not None:
            l = l + jnp.exp(sink_val - m)
        o = _dot(p.astype(BF16), vv) / l
        return o, m + jnp.log(l)

    def body(sink_ref, q_ref, k_ref, v_ref, o_ref, lse_ref):
        kv = pl.program_id(0)
        i = pl.program_id(1)

        def run(kk, vv, mask):
            for g in range(GROUPS):
                sink_val = sink_ref[kv * GROUPS + g] if windowed else None
                o, lse = one_head(q_ref[:, g * HEAD_DIM:(g + 1) * HEAD_DIM], kk, vv, mask, sink_val)
                o_ref[:, g * HEAD_DIM:(g + 1) * HEAD_DIM] = o.astype(BF16)
                lse_ref[g] = lse

        @pl.when(i == 0)
        def _():
            if windowed:
                o_ref[...] = jnp.zeros_like(o_ref)
                lse_ref[...] = jnp.zeros_like(lse_ref)
            else:
                run(k_ref[0:ctx_len], v_ref[0:ctx_len], None)

        @pl.when(i > 0)
        def _():
            if windowed:
                start = _band_start(i, t)
                kk = jnp.concatenate([k_ref[0:ctx_len], k_ref[pl.ds(start, BAND)]], axis=0)
                vv = jnp.concatenate([v_ref[0:ctx_len], v_ref[pl.ds(start, BAND)]], axis=0)
                run(kk, vv, _band_mask(i, start, ctx_len))
            else:
                run(k_ref[...], v_ref[...], None)

    kvspec = pl.BlockSpec((t, HEAD_DIM), lambda kv, i: (0, kv))
    return _pcall(
        body, name=name, grid=(KV_HEADS, t // TM),
        in_specs=[pl.BlockSpec(memory_space=pltpu.SMEM), pl.BlockSpec((TM, gw), lambda kv, i: (i, kv)), kvspec, kvspec],
        out_specs=[pl.BlockSpec((TM, gw), lambda kv, i: (i, kv)),
                   pl.BlockSpec((GROUPS, TM, 1), lambda kv, i: (kv, i, 0))],
        out_shape=[jax.ShapeDtypeStruct((t, d_model), BF16), jax.ShapeDtypeStruct((Q_HEADS, t, 1), F32)],
        compiler_params=_params("parallel", "parallel"),
    )(sink, q, k, v)


def attention_bwd(q, k, v, cat, lse, dcat, sink, ctx_len, windowed, name):
    t = q.shape[0]
    gw = GROUPS * HEAD_DIM

    def body(sink_ref, q_ref, k_ref, v_ref, o_ref, lse_ref, do_ref, dq_ref, dk_ref, dv_ref, dsink_ref):
        kv = pl.program_id(0)
        i = pl.program_id(1)

        @pl.when(i == 0)
        def _():
            dk_ref[...] = jnp.zeros_like(dk_ref)
            dv_ref[...] = jnp.zeros_like(dv_ref)
            dsink_ref[...] = jnp.zeros_like(dsink_ref)

        def run(kk, vv, mask, accumulate):
            for g in range(GROUPS):
                cols = slice(g * HEAD_DIM, (g + 1) * HEAD_DIM)
                qh = q_ref[:, cols]
                doh = do_ref[:, cols]
                delta = jnp.sum(doh * o_ref[:, cols].astype(F32), axis=-1, keepdims=True)
                lse_g = lse_ref[g]
                s = _dot_nt(qh, kk) * ATT_SCALE
                if mask is not None:
                    s = jnp.where(mask, s, NEG_INF)
                p = jnp.exp(s - lse_g)
                dob = doh.astype(BF16)
                dp = _dot_nt(dob, vv)
                ds = (p * (dp - delta) * ATT_SCALE).astype(BF16)
                dq_ref[:, cols] = _dot(ds, kk)
                accumulate(_dot_tn(ds, qh), _dot_tn(p.astype(BF16), dob))
                if windowed:
                    p_sink = jnp.exp(sink_ref[kv * GROUPS + g] - lse_g)
                    dsink_ref[g:g + 1, :] += jnp.sum(-p_sink * delta, axis=0, keepdims=True)

        @pl.when(i == 0)
        def _():
            if windowed:
                dq_ref[...] = jnp.zeros_like(dq_ref)
            else:
                def acc_ctx(dkp, dvp):
                    dk_ref[0:ctx_len] += dkp
                    dv_ref[0:ctx_len] += dvp

                run(k_ref[0:ctx_len], v_ref[0:ctx_len], None, acc_ctx)

        @pl.when(i > 0)
        def _():
            if windowed:
                start = _band_start(i, t)
                kk = jnp.concatenate([k_ref[0:ctx_len], k_ref[pl.ds(start, BAND)]], axis=0)
                vv = jnp.concatenate([v_ref[0:ctx_len], v_ref[pl.ds(start, BAND)]], axis=0)

                def acc_band(dkp, dvp):
                    dk_ref[0:ctx_len] += dkp[0:ctx_len]
                    dv_ref[0:ctx_len] += dvp[0:ctx_len]
                    dk_ref[pl.ds(start, BAND)] += dkp[ctx_len:ctx_len + BAND]
                    dv_ref[pl.ds(start, BAND)] += dvp[ctx_len:ctx_len + BAND]

                run(kk, vv, _band_mask(i, start, ctx_len), acc_band)
            else:
                def acc_all(dkp, dvp):
                    dk_ref[...] += dkp
                    dv_ref[...] += dvp

                run(k_ref[...], v_ref[...], None, acc_all)

    kvspec = pl.BlockSpec((t, HEAD_DIM), lambda kv, i: (0, kv))
    grp = pl.BlockSpec((TM, gw), lambda kv, i: (i, kv))
    return _pcall(
        body, name=name, grid=(KV_HEADS, t // TM),
        in_specs=[pl.BlockSpec(memory_space=pltpu.SMEM), grp, kvspec, kvspec, grp,
                  pl.BlockSpec((GROUPS, TM, 1), lambda kv, i: (kv, i, 0)), grp],
        out_specs=[grp, kvspec, kvspec, pl.BlockSpec((None, 8, 128), lambda kv, i: (kv, 0, 0))],
        out_shape=[jax.ShapeDtypeStruct((t, ATT_WIDTH), F32), jax.ShapeDtypeStruct((t, KV_WIDTH), F32),
                   jax.ShapeDtypeStruct((t, KV_WIDTH), F32), jax.ShapeDtypeStruct((KV_HEADS, 8, 128), F32)],
        compiler_params=_params("parallel", "arbitrary"),
    )(sink, q, k, v, cat, lse, dcat)


def _segment_pos(t, ctx_len, width):
    row = lax.broadcasted_iota(jnp.int32, (t, width), 0)
    in_ctx = row < ctx_len
    return jnp.where(in_ctx, row, row - ctx_len), jnp.where(in_ctx, ctx_len, t - ctx_len)


def _shifted(x, offset, pos, seg_len):
    t = x.shape[0]
    moved = x if offset == 0 else pltpu.roll(x, (-offset) % t, 0)
    ok = (pos + offset >= 0) & (pos + offset < seg_len)
    return jnp.where(ok, moved, 0.0)


def conv_fwd(proj, conv_w, cat, ctx_len, name):
    t = proj.shape[0]
    base = QKV_WIDTH // 128
    per = AUX_WIDTH // 128

    def body(gb_ref, gc_ref, u_ref, w_ref, cat_in, o_ref):
        del cat_in
        pos, seg = _segment_pos(t, ctx_len, 128)
        z = gc_ref[...] * u_ref[...]
        conv = (w_ref[0:1, :] * _shifted(z, -1, pos, seg) + w_ref[1:2, :] * z
                + w_ref[2:3, :] * _shifted(z, 1, pos, seg))
        o_ref[...] = (gb_ref[...] * conv).astype(BF16)

    def col(off):
        return pl.BlockSpec((t, 128), lambda j: (0, base + off * per + j))

    return _pcall(
        body, name=name, grid=(per,),
        in_specs=[col(0), col(1), col(2), pl.BlockSpec((3, 128), lambda j: (0, j)),
                  pl.BlockSpec(memory_space=pl.ANY)],
        out_specs=pl.BlockSpec((t, 128), lambda j: (0, ATT_WIDTH // 128 + j)),
        out_shape=jax.ShapeDtypeStruct(cat.shape, BF16),
        input_output_aliases={4: 0},
        compiler_params=_params("parallel"),
    )(proj, proj, proj, conv_w, cat)


def conv_bwd(dcat, proj, conv_w, ctx_len, name):
    t = proj.shape[0]
    base = QKV_WIDTH // 128
    per = AUX_WIDTH // 128

    def body(do_ref, gb_ref, gc_ref, u_ref, w_ref, da_ref, dw_ref):
        pos, seg = _segment_pos(t, ctx_len, 128)
        gc = gc_ref[...]
        u = u_ref[...]
        z = gc * u
        zm = _shifted(z, -1, pos, seg)
        zp = _shifted(z, 1, pos, seg)
        conv = w_ref[0:1, :] * zm + w_ref[1:2, :] * z + w_ref[2:3, :] * zp
        dout = do_ref[...]
        da_ref[0] = (dout * conv).astype(BF16)
        dconv = dout * gb_ref[...]
        dw_ref[0:1, :] = jnp.sum(dconv * zm, axis=0, keepdims=True)
        dw_ref[1:2, :] = jnp.sum(dconv * z, axis=0, keepdims=True)
        dw_ref[2:3, :] = jnp.sum(dconv * zp, axis=0, keepdims=True)
        dz = (w_ref[1:2, :] * dconv + w_ref[0:1, :] * _shifted(dconv, 1, pos, seg)
              + w_ref[2:3, :] * _shifted(dconv, -1, pos, seg))
        da_ref[1] = (dz * u).astype(BF16)
        da_ref[2] = (dz * gc).astype(BF16)

    def col(off):
        return pl.BlockSpec((t, 128), lambda j: (0, base + off * per + j))

    return _pcall(
        body, name=name, grid=(per,),
        in_specs=[pl.BlockSpec((t, 128), lambda j: (0, ATT_WIDTH // 128 + j)), col(0), col(1), col(2),
                  pl.BlockSpec((3, 128), lambda j: (0, j))],
        out_specs=[pl.BlockSpec((3, t, 128), lambda j: (0, 0, j)), pl.BlockSpec((3, 128), lambda j: (0, j))],
        out_shape=[jax.ShapeDtypeStruct((3, t, AUX_WIDTH), BF16), jax.ShapeDtypeStruct((3, AUX_WIDTH), F32)],
        compiler_params=_params("parallel"),
    )(dcat, proj, proj, proj, conv_w)


def _pooled(u, w, pos, seg):
    lo = jnp.clip(pos - w // 2, 0, seg)
    hi = jnp.clip(pos - w // 2 + w, 0, seg)
    inv = 1.0 / (hi - lo).astype(F32)
    total = _shifted(u, -(w // 2), pos, seg)
    for o in range(-(w // 2) + 1, w // 2):
        total = total + _shifted(u, o, pos, seg)
    return total * inv - u, inv


def pool_fwd(proj, pool_w, pool_scale, cat, ctx_len, name):
    t = proj.shape[0]

    def body(u_ref, w_ref, sc_ref, cat_in, o_ref):
        del cat_in
        pos, seg = _segment_pos(t, ctx_len, 128)
        for g, w in enumerate(POOL_WINDOWS):
            cols = slice(g * 128, (g + 1) * 128)
            pooled, _ = _pooled(u_ref[:, cols], w, pos, seg)
            mixed = _dot(pooled.astype(BF16), w_ref[g].astype(BF16))
            o_ref[:, cols] = (mixed * sc_ref[:, cols]).astype(BF16)

    return _pcall(
        body, name=name, grid=(1,),
        in_specs=[pl.BlockSpec((t, AUX_WIDTH), lambda j: (0, QKV_WIDTH // AUX_WIDTH)),
                  pl.BlockSpec((AUX_GROUPS, 128, 128), lambda j: (0, 0, 0)),
                  pl.BlockSpec((1, AUX_WIDTH), lambda j: (0, 0)), pl.BlockSpec(memory_space=pl.ANY)],
        out_specs=pl.BlockSpec((t, AUX_WIDTH), lambda j: (0, ATT_WIDTH // AUX_WIDTH)),
        out_shape=jax.ShapeDtypeStruct(cat.shape, BF16),
        input_output_aliases={3: 0},
        compiler_params=_params("arbitrary"),
    )(proj, pool_w, pool_scale, cat)


def pool_bwd(dcat, proj, pool_w, pool_scale, ctx_len, name):
    t = proj.shape[0]

    def body(do_ref, u_ref, w_ref, sc_ref, da_ref, dw_ref, dsc_ref):
        pos, seg = _segment_pos(t, ctx_len, 128)
        for g, w in enumerate(POOL_WINDOWS):
            cols = slice(g * 128, (g + 1) * 128)
            pooled, inv = _pooled(u_ref[:, cols], w, pos, seg)
            pb = pooled.astype(BF16)
            wb = w_ref[g].astype(BF16)
            mixed = _dot(pb, wb)
            dout = do_ref[:, cols]
            dsc_ref[:, cols] = jnp.sum(dout * mixed, axis=0, keepdims=True)
            dmixed = (dout * sc_ref[:, cols]).astype(BF16)
            dw_ref[g] = _dot_tn(pb, dmixed)
            dpooled = _dot_nt(dmixed, wb)
            spread = dpooled * inv
            du = _shifted(spread, w // 2, pos, seg) - dpooled
            for o in range(-(w // 2) + 1, w // 2):
                du = du + _shifted(spread, -o, pos, seg)
            da_ref[0, :, cols] = du.astype(BF16)

    return _pcall(
        body, name=name, grid=(1,),
        in_specs=[pl.BlockSpec((t, AUX_WIDTH), lambda j: (0, ATT_WIDTH // AUX_WIDTH)),
                  pl.BlockSpec((t, AUX_WIDTH), lambda j: (0, QKV_WIDTH // AUX_WIDTH)),
                  pl.BlockSpec((AUX_GROUPS, 128, 128), lambda j: (0, 0, 0)),
                  pl.BlockSpec((1, AUX_WIDTH), lambda j: (0, 0))],
        out_specs=[pl.BlockSpec((1, t, AUX_WIDTH), lambda j: (0, 0, 0)),
                   pl.BlockSpec((AUX_GROUPS, 128, 128), lambda j: (0, 0, 0)),
                   pl.BlockSpec((1, AUX_WIDTH), lambda j: (0, 0))],
        out_shape=[jax.ShapeDtypeStruct((1, t, AUX_WIDTH), BF16), jax.ShapeDtypeStruct((AUX_GROUPS, 128, 128), F32),
                   jax.ShapeDtypeStruct((1, AUX_WIDTH), F32)],
        compiler_params=_params("arbitrary"),
    )(dcat, proj, pool_w, pool_scale)


MOD_NAMES = ("shift1", "scale1", "gate1", "shift2", "scale2", "gate2")


def _layer_fwd(h, lw, weight, mods, tabs, ctx_len, windowed, tag):
    cos, sin = tabs
    xn = norm_mod_fwd(h, lw["norm1_g"], mods["shift1"], mods["scale1"], tag + "norm1_fwd")
    lw["w_in"] = weight("w_in", xn)
    proj = mm_nn(xn, lw["w_in"], tag + "w_in_fwd")
    q, k, v = qk_prep_fwd(proj, lw["q_g"], lw["k_g"], cos, sin, tag + "qk_fwd")
    cat, lse = attention_fwd(q, k, v, lw["sink"], ctx_len, windowed, tag + "attn_fwd")
    if windowed:
        cat = pool_fwd(proj, lw["pool_w"], lw["pool_scale"], cat, ctx_len, tag + "pool_fwd")
    else:
        cat = conv_fwd(proj, lw["conv_w"], cat, ctx_len, tag + "conv_fwd")
    lw["w_out"] = weight("w_out", cat)
    y, h1 = mm_nn_residual(cat, lw["w_out"], h, mods["gate1"], tag + "w_out_fwd")
    hn = norm_mod_fwd(h1, lw["norm2_g"], mods["shift2"], mods["scale2"], tag + "norm2_fwd")
    lw["w_gate"], lw["w_up"] = weight("w_gate", hn), weight("w_up", hn)
    g, u, act = mm_swiglu(hn, lw["w_gate"], lw["w_up"], tag + "ffn_in_fwd")
    lw["w_down"] = weight("w_down", act)
    f, h2 = mm_nn_residual(act, lw["w_down"], h1, mods["gate2"], tag + "w_down_fwd")
    saved = dict(h=h, xn=xn, proj=proj, q=q, k=k, v=v, cat=cat, lse=lse, y=y, h1=h1, hn=hn, g=g, u=u, act=act, f=f)
    return h2, saved


def _layer_bwd_ffn(dh, sv, lw, mods, tag):
    big, small = {}, {}
    df, dgate2 = gate_bwd(dh, sv["f"], mods["gate2"], tag + "gate2_bwd")
    dgp, du = mm_nt_swiglu_bwd(df, lw["w_down"], sv["g"], sv["u"], tag + "w_down_dgrad")
    big["w_down"] = mm_tn(sv["act"], df, tag + "w_down_wgrad", col_sharded=False)
    dhn = mm_nt_cols(dgp, lw["w_gate"], tag + "w_gate_dgrad")
    dhn = mm_nt_cols(du, lw["w_up"], tag + "w_up_dgrad", add=dhn)
    big["w_gate"] = mm_tn(sv["hn"], dgp, tag + "w_gate_wgrad", col_sharded=True)
    big["w_up"] = mm_tn(sv["hn"], du, tag + "w_up_wgrad", col_sharded=True)
    dh1, dshift2, dscale2, small["norm2_g"] = norm_mod_bwd(dhn, sv["h1"], lw["norm2_g"], mods["scale2"], dh,
                                                           tag + "norm2_bwd")
    return dh1, big, small, (dshift2, dscale2, dgate2)


def _layer_bwd_mixer(dh1, sv, lw, mods, tabs, ctx_len, windowed, tag):
    cos, sin = tabs
    big, small = {}, {}
    dy, dgate1 = gate_bwd(dh1, sv["y"], mods["gate1"], tag + "gate1_bwd")
    dcat = mm_nt_rows(dy, lw["w_out"], tag + "w_out_dgrad")
    big["w_out"] = mm_tn(sv["cat"], dy, tag + "w_out_wgrad", col_sharded=False)
    if windowed:
        daux, small["pool_w"], small["pool_scale"] = pool_bwd(dcat, sv["proj"], lw["pool_w"], lw["pool_scale"],
                                                              ctx_len, tag + "pool_bwd")
    else:
        daux, small["conv_w"] = conv_bwd(dcat, sv["proj"], lw["conv_w"], ctx_len, tag + "conv_bwd")
    dq, dk, dv, dsink = attention_bwd(sv["q"], sv["k"], sv["v"], sv["cat"], sv["lse"], dcat, lw["sink"], ctx_len,
                                      windowed, tag + "attn_bwd")
    if windowed:
        small["sink"] = dsink[:, :GROUPS, 0].reshape(Q_HEADS)
    dproj, small["q_g"], small["k_g"] = qk_prep_bwd(dq, dk, dv, daux, sv["proj"], lw["q_g"], lw["k_g"], cos, sin,
                                                    tag + "qk_bwd")
    dxn = mm_nt_cols(dproj, lw["w_in"], tag + "w_in_dgrad")
    big["w_in"] = mm_tn(sv["xn"], dproj, tag + "w_in_wgrad", col_sharded=True)
    dh0, dshift1, dscale1, small["norm1_g"] = norm_mod_bwd(dxn, sv["h"], lw["norm1_g"], mods["scale1"], dh1,
                                                           tag + "norm1_bwd")
    return dh0, big, small, (dshift1, dscale1, dgate1)


def _dmod_rows(mixer_part, ffn_part):
    return jnp.concatenate([m[:, 0, :] for m in (*mixer_part, *ffn_part)], axis=1)


def _tie(mods, name, token):
    return {**mods, name: mods[name] + token[0, 0]}


def _place():
    x, y, c = lax.axis_index("x"), lax.axis_index("y"), lax.axis_index("c")
    chips = [(1 - x, y), (x, 1 - y), (1 - x, 1 - y)]
    return x, y, c, chips


def _remote(src, dst, send_sem, recv_sem, to):
    return pltpu.make_async_remote_copy(src_ref=src, dst_ref=dst, send_sem=send_sem, recv_sem=recv_sem,
                                        device_id=to, device_id_type=MESH)


ANY = pl.BlockSpec(memory_space=pl.ANY)


def all_gather8(x_shard, name):
    m_per, n = x_shard.shape

    def body(x_ref, out_ref, send_sems, recv_sems, local_sem):
        x, y, c, chips = _place()
        me, sibling = (x, y, c), (x, y, 1 - c)

        def rows(px, py, pc):
            return out_ref.at[pl.ds((4 * px + 2 * py + pc) * m_per, m_per), :]

        def copy(k, block, to, src=None):
            return _remote(rows(*block) if src is None else src, rows(*block), send_sems.at[k], recv_sems.at[k], to)

        mine = pltpu.make_async_copy(x_ref, rows(*me), local_sem)
        mine.start()
        first = [copy(0, me, sibling, src=x_ref)]
        first += [copy(1 + j, me, (*chip, c), src=x_ref) for j, chip in enumerate(chips)]
        for cp in first:
            cp.start()
        passed = [copy(4 + j, (*chip, c), sibling) for j, chip in enumerate(chips)]
        for j, chip in enumerate(chips):
            copy(1 + j, (*chip, c), me).wait_recv()
            passed[j].start()
        copy(0, sibling, me).wait_recv()
        for j, chip in enumerate(chips):
            copy(4 + j, (*chip, 1 - c), me).wait_recv()
        for cp in first + passed:
            cp.wait_send()
        mine.wait()

    return _pcall(
        body, name=name,
        out_shape=jax.ShapeDtypeStruct((N_DEV * m_per, n), x_shard.dtype),
        in_specs=[pl.BlockSpec(memory_space=pltpu.VMEM)],
        out_specs=pl.BlockSpec(memory_space=pltpu.VMEM),
        scratch_shapes=[pltpu.SemaphoreType.DMA((7,)), pltpu.SemaphoreType.DMA((7,)), pltpu.SemaphoreType.DMA],
        compiler_params=pltpu.CompilerParams(vmem_limit_bytes=VMEM_LIMIT),
    )(x_shard)


def _row_tile(rows):
    for tr in (256, 128, 64, 32, 16):
        if rows % tr == 0:
            return tr
    return rows


def place_shard(w, p, name):
    r, cols = w.shape
    tr = _row_tile(r)

    def body(p_ref, w_ref, o_ref):
        del p_ref
        o_ref[...] = w_ref[...].astype(BF16)

    return _pcall(
        body, name=name,
        grid_spec=pltpu.PrefetchScalarGridSpec(
            num_scalar_prefetch=1, grid=(r // tr,),
            in_specs=[pl.BlockSpec((tr, cols), lambda i, p_ref: (i, 0))],
            out_specs=pl.BlockSpec((None, tr, cols), lambda i, p_ref: (p_ref[0], i, 0))),
        out_shape=jax.ShapeDtypeStruct((N_SHARD, r, cols), BF16),
        compiler_params=_params("parallel"),
    )(p.reshape(1).astype(jnp.int32), w)


def gather_weight_shards(stacks):
    n = len(stacks)

    def body(*refs):
        outs = refs[n:2 * n]
        send_sems, recv_sems = refs[2 * n:]
        x, y, c, chips = _place()
        p = 2 * x + y
        sibling = (x, y, 1 - c)
        started = []
        for i in range(n):
            hr = outs[i].shape[1] // 2
            mine = outs[i].at[p, pl.ds(c * hr, hr)]
            for j, chip in enumerate(chips):
                cp = _remote(mine, mine, send_sems.at[i, j], recv_sems.at[i, j], (*chip, c))
                cp.start()
                started.append(cp)
        for i in range(n):
            hr = outs[i].shape[1] // 2
            for j, (px, py) in enumerate(chips):
                half = outs[i].at[2 * px + py, pl.ds(c * hr, hr)]
                _remote(half, half, send_sems.at[i, j], recv_sems.at[i, j], (px, py, c)).wait_recv()
                fwd = _remote(half, half, send_sems.at[i, 3 + j], recv_sems.at[i, 3 + j], sibling)
                fwd.start()
                started.append(fwd)
        for i in range(n):
            hr = outs[i].shape[1] // 2
            for j, (px, py) in enumerate(chips):
                other = outs[i].at[2 * px + py, pl.ds((1 - c) * hr, hr)]
                _remote(other, other, send_sems.at[i, 3 + j], recv_sems.at[i, 3 + j], sibling).wait_recv()
        for cp in started:
            cp.wait_send()

    return _pcall(
        body, name="gather_weight_shards",
        out_shape=[jax.ShapeDtypeStruct(s.shape, s.dtype) for s in stacks],
        in_specs=[ANY] * n, out_specs=[ANY] * n,
        input_output_aliases={i: i for i in range(n)},
        scratch_shapes=[pltpu.SemaphoreType.DMA((n, 6)), pltpu.SemaphoreType.DMA((n, 6))],
    )(*stacks)


HBM_SPEC = pl.BlockSpec(memory_space=pltpu.HBM)
SEM_SPEC = pl.BlockSpec(memory_space=pltpu.SEMAPHORE)
EFFECT = pltpu.SideEffectType.DATAFLOW_SIDE_EFFECTING


def split_start(name, copies, n_sems, arrays, after):
    m = len(arrays)

    def body(*refs):
        for cp in copies(refs[:m], refs[m + 1], refs[m + 2]):
            cp.start()
        refs[-1][...] = jnp.zeros_like(refs[-1])

    res = _pcall(
        body, name=name,
        out_shape=(pltpu.SemaphoreType.DMA((n_sems,)), pltpu.SemaphoreType.DMA((n_sems,)),
                   *[pltpu.HBM(s.shape, s.dtype) for s in arrays], jax.ShapeDtypeStruct((8, 128), F32)),
        in_specs=[HBM_SPEC] * m + [ANY],
        out_specs=(SEM_SPEC, SEM_SPEC, *[HBM_SPEC] * m, pl.BlockSpec(memory_space=pltpu.VMEM)),
        input_output_aliases={i: 2 + i for i in range(m)},
        compiler_params=pltpu.CompilerParams(has_side_effects=EFFECT),
    )(*[pltpu.with_memory_space_constraint(s, pltpu.HBM) for s in arrays], after)
    return res[0], res[1], list(res[2:2 + m]), res[2 + m]


def split_wait(name, copies, send_sems, recv_sems, arrays, after):
    m = len(arrays)

    def body(*refs):
        for cp in copies(refs[:m], refs[m], refs[m + 1]):
            cp.wait_send()
            cp.wait_recv()

    return _pcall(
        body, name=name,
        out_shape=[pltpu.HBM(s.shape, s.dtype) for s in arrays],
        in_specs=[HBM_SPEC] * m + [SEM_SPEC, SEM_SPEC, ANY],
        out_specs=[HBM_SPEC] * m,
        input_output_aliases={i: i for i in range(m)},
        compiler_params=pltpu.CompilerParams(has_side_effects=EFFECT),
    )(*arrays, send_sems, recv_sems, after)


def gather_copies(n):
    def copies(stacks, send_sems, recv_sems):
        x, y, c, chips = _place()
        p = 2 * x + y
        return [_remote(stacks[i].at[p], stacks[i].at[p], send_sems.at[3 * i + j], recv_sems.at[3 * i + j], (*chip, c))
                for i in range(n) for j, chip in enumerate(chips)]
    return copies


def sibling_half_copies(n):
    def copies(refs, send_sems, recv_sems):
        x, y, c, _ = _place()
        out = []
        for i in range(n):
            hr = refs[n + i].shape[1]
            out.append(_remote(refs[i].at[:, pl.ds((1 - c) * hr, hr), :], refs[n + i], send_sems.at[i], recv_sems.at[i],
                               (x, y, 1 - c)))
        return out
    return copies


def chip_quarter_copies(n):
    def copies(refs, send_sems, recv_sems):
        _, _, c, chips = _place()
        return [_remote(refs[i].at[2 * px + py], refs[n + i].at[j], send_sems.at[3 * i + j], recv_sems.at[3 * i + j],
                        (px, py, c)) for i in range(n) for j, (px, py) in enumerate(chips)]
    return copies


def exchange_sibling_halves(grads):
    n = len(grads)

    def body(*refs):
        ins, outs = refs[:n], refs[n:2 * n]
        send_sems, recv_sems = refs[2 * n:]
        x, y, c, _ = _place()
        copies = []
        for i in range(n):
            hr = ins[i].shape[1] // 2
            cp = _remote(ins[i].at[:, pl.ds((1 - c) * hr, hr), :], outs[i], send_sems.at[i], recv_sems.at[i],
                         (x, y, 1 - c))
            cp.start()
            copies.append(cp)
        for cp in copies:
            cp.wait()

    return _pcall(
        body, name="exchange_sibling_halves",
        out_shape=[jax.ShapeDtypeStruct((g.shape[0], g.shape[1] // 2, g.shape[2]), g.dtype) for g in grads],
        in_specs=[ANY] * n, out_specs=[ANY] * n,
        scratch_shapes=[pltpu.SemaphoreType.DMA((n,)), pltpu.SemaphoreType.DMA((n,))],
    )(*grads)


def exchange_chip_quarters(parts):
    n = len(parts)

    def body(*refs):
        ins, outs = refs[:n], refs[n:2 * n]
        send_sems, recv_sems = refs[2 * n:]
        _, _, c, chips = _place()
        sends = []
        for i in range(n):
            for j, (px, py) in enumerate(chips):
                cp = _remote(ins[i].at[2 * px + py], outs[i].at[j], send_sems.at[i, j], recv_sems.at[i, j], (px, py, c))
                cp.start()
                sends.append(cp)
        for i in range(n):
            for j, (px, py) in enumerate(chips):
                slot = outs[i].at[j]
                _remote(slot, slot, send_sems.at[i, j], recv_sems.at[i, j], (px, py, c)).wait_recv()
        for cp in sends:
            cp.wait_send()

    return _pcall(
        body, name="exchange_chip_quarters",
        out_shape=[jax.ShapeDtypeStruct((3,) + g.shape[1:], g.dtype) for g in parts],
        in_specs=[ANY] * n, out_specs=[ANY] * n,
        scratch_shapes=[pltpu.SemaphoreType.DMA((n, 3)), pltpu.SemaphoreType.DMA((n, 3))],
    )(*parts)


def exchange_final_halves(grads):
    n = len(grads)

    def body(*refs):
        outs = refs[n:2 * n]
        send_sems, recv_sems = refs[2 * n:]
        x, y, c, _ = _place()
        sends = []
        for i in range(n):
            hr = outs[i].shape[0] // 2
            mine = outs[i].at[pl.ds(c * hr, hr)]
            cp = _remote(mine, mine, send_sems.at[i], recv_sems.at[i], (x, y, 1 - c))
            cp.start()
            sends.append(cp)
        for i in range(n):
            hr = outs[i].shape[0] // 2
            other = outs[i].at[pl.ds((1 - c) * hr, hr)]
            _remote(other, other, send_sems.at[i], recv_sems.at[i], (x, y, 1 - c)).wait_recv()
        for cp in sends:
            cp.wait_send()

    return _pcall(
        body, name="exchange_final_halves",
        out_shape=[jax.ShapeDtypeStruct(g.shape, g.dtype) for g in grads],
        in_specs=[ANY] * n, out_specs=[ANY] * n,
        input_output_aliases={i: i for i in range(n)},
        scratch_shapes=[pltpu.SemaphoreType.DMA((n,)), pltpu.SemaphoreType.DMA((n,))],
    )(*grads)


def add_own_half(g, recv, c, name):
    s, r, cols = g.shape
    hr = r // 2
    tr = _row_tile(hr)
    nb = hr // tr

    def body(c_ref, g_ref, r_ref, o_ref):
        del c_ref
        o_ref[...] = (g_ref[...] + r_ref[...]).astype(BF16)

    return _pcall(
        body, name=name,
        grid_spec=pltpu.PrefetchScalarGridSpec(
            num_scalar_prefetch=1, grid=(s, nb),
            in_specs=[pl.BlockSpec((None, tr, cols), lambda j, i, c_ref: (j, c_ref[0] * nb + i, 0)),
                      pl.BlockSpec((None, tr, cols), lambda j, i, c_ref: (j, i, 0))],
            out_specs=pl.BlockSpec((None, tr, cols), lambda j, i, c_ref: (j, i, 0))),
        out_shape=jax.ShapeDtypeStruct((s, hr, cols), BF16),
        compiler_params=_params("parallel", "parallel"),
    )(c.reshape(1).astype(jnp.int32), g, recv)


def add_quarters(own, others, p, c, name):
    _, hr, cols = own.shape
    tr = _row_tile(hr)
    nb = hr // tr

    def body(pc_ref, own_ref, oth_ref, o_ref):
        del pc_ref
        acc = own_ref[...].astype(F32) + oth_ref[0].astype(F32)
        acc = acc + oth_ref[1].astype(F32)
        o_ref[...] = acc + oth_ref[2].astype(F32)

    return _pcall(
        body, name=name,
        grid_spec=pltpu.PrefetchScalarGridSpec(
            num_scalar_prefetch=1, grid=(nb,),
            in_specs=[pl.BlockSpec((None, tr, cols), lambda i, pc: (pc[0], i, 0)),
                      pl.BlockSpec((3, tr, cols), lambda i, pc: (0, i, 0))],
            out_specs=pl.BlockSpec((tr, cols), lambda i, pc: (pc[1] * nb + i, 0))),
        out_shape=jax.ShapeDtypeStruct((2 * hr, cols), F32),
        compiler_params=_params("parallel"),
    )(jnp.stack([p, c]).astype(jnp.int32), own, others)


def _adamw_math(w, g, m, v):
    m2 = ADAM_B1 * m + (1.0 - ADAM_B1) * g
    v2 = ADAM_B2 * v + (1.0 - ADAM_B2) * (g * g)
    m_hat = m2 / (1.0 - ADAM_B1 ** ADAM_STEP)
    v_hat = v2 / (1.0 - ADAM_B2 ** ADAM_STEP)
    delta = -ADAM_LR * (m_hat / (jnp.sqrt(v_hat) + ADAM_EPS) + ADAM_WD * w)
    return delta, m2, v2


def adamw(w, g, m, v, name):
    r, cols = w.shape
    tr = 128 if r % 128 == 0 else r

    def body(w_ref, g_ref, m_ref, v_ref, d_ref, m2_ref, v2_ref):
        d_ref[...], m2_ref[...], v2_ref[...] = _adamw_math(w_ref[...], g_ref[...], m_ref[...], v_ref[...])

    blk = pl.BlockSpec((tr, cols), lambda i: (i, 0))
    return _pcall(
        body, name=name, grid=(r // tr,),
        in_specs=[blk] * 4, out_specs=[blk] * 3,
        out_shape=[jax.ShapeDtypeStruct((r, cols), F32)] * 3,
        compiler_params=_params("parallel"),
    )(w, g, m, v)


def small_update(gathered, w, m, v):
    nd, r, lanes = gathered.shape

    def body(ga_ref, w_ref, m_ref, v_ref, g_ref, d_ref, m2_ref, v2_ref):
        g = ga_ref[0] + ga_ref[1]
        for dev in range(2, nd):
            g = g + ga_ref[dev]
        g_ref[...] = g
        d_ref[...], m2_ref[...], v2_ref[...] = _adamw_math(w_ref[...], g, m_ref[...], v_ref[...])

    return _pcall(
        body, name="small_update",
        out_shape=[jax.ShapeDtypeStruct((r, lanes), F32)] * 4,
        compiler_params=pltpu.CompilerParams(vmem_limit_bytes=VMEM_LIMIT),
    )(gathered, w, m, v)


def mod_fwd(c16, w_mod, b_mod, name):
    d, ns = w_mod.shape
    tn = 512

    def body(c_ref, w_ref, b_ref, o_ref):
        cv = c_ref[...]
        sc = (cv * _sigmoid(cv)).astype(BF16)
        o_ref[...] = _dot(sc, w_ref[...].astype(BF16)) + b_ref[...]

    return _pcall(
        body, name=name, grid=(ns // tn,),
        in_specs=[pl.BlockSpec((16, d), lambda j: (0, 0)), pl.BlockSpec((d, tn), lambda j: (0, j)),
                  pl.BlockSpec((1, tn), lambda j: (0, j))],
        out_specs=pl.BlockSpec((16, tn), lambda j: (0, j)),
        out_shape=jax.ShapeDtypeStruct((16, ns), F32),
        compiler_params=_params("parallel"),
    )(c16, w_mod, b_mod)


def wmod_update(c16, dmod16, w, m, v, name):
    d, ns = w.shape
    tn = 256

    def body(c_ref, dm_ref, w_ref, m_ref, v_ref, g_ref, d_ref, m2_ref, v2_ref):
        cv = c_ref[...]
        sc = (cv * _sigmoid(cv)).astype(BF16)
        g = _dot_tn(sc, dm_ref[...].astype(BF16))
        g_ref[...] = g
        d_ref[...], m2_ref[...], v2_ref[...] = _adamw_math(w_ref[...], g, m_ref[...], v_ref[...])

    blk = pl.BlockSpec((d, tn), lambda j: (0, j))
    return _pcall(
        body, name=name, grid=(ns // tn,),
        in_specs=[pl.BlockSpec((16, d), lambda j: (0, 0)), pl.BlockSpec((16, tn), lambda j: (0, j)), blk, blk, blk],
        out_specs=[blk] * 4,
        out_shape=[jax.ShapeDtypeStruct((d, ns), F32)] * 4,
        compiler_params=_params("parallel"),
    )(c16, dmod16, w, m, v)


def cctx_partial(dm0, w0, dm1, w1):
    d, ns = w0.shape
    tn = 512

    def body(dm0_ref, w0_ref, dm1_ref, w1_ref, o_ref):
        @pl.when(pl.program_id(0) == 0)
        def _():
            o_ref[...] = jnp.zeros_like(o_ref)

        for dm_ref, w_ref in ((dm0_ref, w0_ref), (dm1_ref, w1_ref)):
            tot = jnp.sum(dm_ref[...], axis=0, keepdims=True)
            lhs = jnp.broadcast_to(tot, (8, tn)).astype(BF16)
            o_ref[...] += _dot_nt(lhs, w_ref[...].astype(BF16))

    dspec = pl.BlockSpec((8, tn), lambda j: (0, j))
    wspec = pl.BlockSpec((d, tn), lambda j: (0, j))
    return _pcall(
        body, name="cctx_partial", grid=(ns // tn,),
        in_specs=[dspec, wspec, dspec, wspec],
        out_specs=pl.BlockSpec((8, d), lambda j: (0, 0)),
        out_shape=jax.ShapeDtypeStruct((8, d), F32),
        compiler_params=_params("arbitrary"),
    )(dm0, w0, dm1, w1)


def cctx_update(parts, c_ctx, m, v):
    d = c_ctx.shape[1]

    def body(p_ref, c_ref, m_ref, v_ref, g_ref, d_ref, m2_ref, v2_ref):
        tot = p_ref[0:1, :] + p_ref[1:2, :]
        tot = tot + p_ref[2:3, :]
        tot = tot + p_ref[3:4, :]
        cv = c_ref[...]
        sg = _sigmoid(cv)
        g = tot * (sg * (1.0 + cv * (1.0 - sg)))
        g_ref[...] = g
        d_ref[...], m2_ref[...], v2_ref[...] = _adamw_math(cv, g, m_ref[...], v_ref[...])

    return _pcall(
        body, name="cctx_update",
        out_shape=[jax.ShapeDtypeStruct((1, d), F32)] * 4,
    )(parts, c_ctx, m, v)


WEIGHT_NAMES = (
    "c_ctx", "l0_norm1_g", "l0_w_mod", "l0_b_mod", "l0_w_in", "l0_q_norm_g", "l0_k_norm_g", "l0_conv_w", "l0_w_out",
    "l0_norm2_g", "l0_w_gate", "l0_w_up", "l0_w_down", "l1_norm1_g", "l1_w_mod", "l1_b_mod", "l1_w_in",
    "l1_q_norm_g", "l1_k_norm_g", "l1_sink", "l1_pool_w", "l1_pool_scale", "l1_w_out", "l1_norm2_g", "l1_w_gate",
    "l1_w_up", "l1_w_down", "final_norm_g")
BIG_NAMES = ("w_in", "w_out", "w_gate", "w_up", "w_down")
SMALL_SLOTS = tuple(
    [(f"l{l}_{nm}", par and f"l{l}_{nm}") for l in (0, 1)
     for nm, par in (("dmod_lat", False), ("dmod_ctx", False), ("b_mod", True), ("norm1_g", True), ("norm2_g", True),
                     ("q_norm_g", True), ("k_norm_g", True))]
    + [("l0_conv_w", None), ("l1_sink", "l1_sink"), ("l1_pool_w", "l1_pool_w"), ("l1_pool_scale", "l1_pool_scale"),
       ("final_norm_g", "final_norm_g")])


def _pack(values, sizes):
    parts = []
    for (name, _), size in zip(SMALL_SLOTS, sizes):
        v = values.get(name)
        padded = -(-size // 128) * 128
        v = jnp.zeros((padded,), F32) if v is None else jnp.pad(v.reshape(-1).astype(F32), (0, padded - size))
        parts.append(v)
    total = sum(p.shape[0] for p in parts)
    parts.append(jnp.zeros((-(-total // 1024) * 1024 - total,), F32))
    return jnp.concatenate(parts).reshape(-1, 128)


def _offsets(sizes):
    offs, o = {}, 0
    for (name, _), size in zip(SMALL_SLOTS, sizes):
        offs[name] = (o, size)
        o += -(-size // 128) * 128
    return offs


def kernel(x, c, ctx, c_ctx, l0_norm1_g, l0_w_mod, l0_b_mod, l0_w_in, l0_q_norm_g, l0_k_norm_g, l0_conv_w, l0_w_out, l0_norm2_g, l0_w_gate, l0_w_up, l0_w_down, l1_norm1_g, l1_w_mod, l1_b_mod, l1_w_in, l1_q_norm_g, l1_k_norm_g, l1_sink, l1_pool_w, l1_pool_scale, l1_w_out, l1_norm2_g, l1_w_gate, l1_w_up, l1_w_down, final_norm_g, loss_target, m_c_ctx, m_l0_norm1_g, m_l0_w_mod, m_l0_b_mod, m_l0_w_in, m_l0_q_norm_g, m_l0_k_norm_g, m_l0_conv_w, m_l0_w_out, m_l0_norm2_g, m_l0_w_gate, m_l0_w_up, m_l0_w_down, m_l1_norm1_g, m_l1_w_mod, m_l1_b_mod, m_l1_w_in, m_l1_q_norm_g, m_l1_k_norm_g, m_l1_sink, m_l1_pool_w, m_l1_pool_scale, m_l1_w_out, m_l1_norm2_g, m_l1_w_gate, m_l1_w_up, m_l1_w_down, m_final_norm_g, v_c_ctx, v_l0_norm1_g, v_l0_w_mod, v_l0_b_mod, v_l0_w_in, v_l0_q_norm_g, v_l0_k_norm_g, v_l0_conv_w, v_l0_w_out, v_l0_norm2_g, v_l0_w_gate, v_l0_w_up, v_l0_w_down, v_l1_norm1_g, v_l1_w_mod, v_l1_b_mod, v_l1_w_in, v_l1_q_norm_g, v_l1_k_norm_g, v_l1_sink, v_l1_pool_w, v_l1_pool_scale, v_l1_w_out, v_l1_norm2_g, v_l1_w_gate, v_l1_w_up, v_l1_w_down, v_final_norm_g):
    a = dict(locals())
    xi, yi, ci = lax.axis_index("x"), lax.axis_index("y"), lax.axis_index("c")
    p = 2 * xi + yi
    me = 4 * xi + 2 * yi + ci
    d = x.shape[-1]
    conv_cols = l0_conv_w.shape[1]

    row0 = jnp.concatenate([c, jnp.pad(l0_conv_w, ((0, 0), (0, d - conv_cols))), jnp.zeros((4, d), F32)], axis=0)
    gathered = all_gather8(row0, "gather_cond").reshape(N_DEV, 8, d)
    c_all = gathered[:, 0]
    conv_w = gathered[0::2, 1:4, :conv_cols].transpose(1, 0, 2).reshape(3, N_SHARD * conv_cols)
    c16_fwd = jnp.concatenate([c_all, c_ctx[None], jnp.zeros((7, d), F32)], axis=0)
    c16_bwd = jnp.concatenate([c_all, jnp.broadcast_to(c_ctx[None], (8, d))], axis=0)

    ns_mod = l0_w_mod.shape[1]
    mod_parts = [mod_fwd(c16_fwd, a[f"l{l}_w_mod"], lax.dynamic_slice(a[f"l{l}_b_mod"], (p * ns_mod,), (ns_mod,))[None],
                         f"l{l}_mod_fwd") for l in (0, 1)]
    modg = all_gather8(jnp.concatenate(mod_parts, axis=0), "gather_mod").reshape(N_DEV, 2, 16, ns_mod)[0::2]
    mod_full = modg.transpose(1, 2, 0, 3).reshape(2, 16, N_SHARD * ns_mod)
    mods = []
    for l in (0, 1):
        lat = lax.dynamic_index_in_dim(mod_full[l], me, axis=0, keepdims=False)
        cx = mod_full[l, 8]
        mods.append({nm: jnp.stack([cx[j * d:(j + 1) * d], lat[j * d:(j + 1) * d]])[:, None, :]
                     for j, nm in enumerate(MOD_NAMES)})

    placed = {(l, nm): place_shard(a[f"l{l}_{nm}"], p, f"l{l}_{nm}_place") for l in (0, 1) for nm in BIG_NAMES}
    ready = {(0, "w_in"): gather_weight_shards([placed[0, "w_in"]])[0]}
    groups = (((0, "w_out"), (0, "w_gate"), (0, "w_up")), ((0, "w_down"),), ((1, "w_in"), (1, "w_out")),
              ((1, "w_gate"), (1, "w_up"), (1, "w_down")))
    flights, after = [], ready[0, "w_in"]
    for gi, grp in enumerate(groups):
        flights.append(split_start(f"gather_start_{gi}", gather_copies(len(grp)), 3 * len(grp),
                                   [placed[k] for k in grp], after))
        after = flights[-1][3]
    started_token = after

    def weight_of(l):
        def weight(nm, after):
            if (l, nm) not in ready:
                gi = next(i for i, grp in enumerate(groups) if (l, nm) in grp)
                send_sems, recv_sems, in_flight, _ = flights[gi]
                landed = split_wait(f"gather_wait_{gi}", gather_copies(len(groups[gi])), send_sems, recv_sems,
                                    in_flight, after)
                ready.update(zip(groups[gi], landed))
            return ready[l, nm]
        return weight

    layers = [dict(norm1_g=a[f"l{l}_norm1_g"][None], norm2_g=a[f"l{l}_norm2_g"][None],
                   q_g=a[f"l{l}_q_norm_g"][None], k_g=a[f"l{l}_k_norm_g"][None]) for l in (0, 1)]
    layers[0].update(conv_w=conv_w, sink=jnp.zeros((Q_HEADS,), F32))
    layers[1].update(sink=l1_sink, pool_w=l1_pool_w, pool_scale=l1_pool_scale[None])

    ctx_len = ctx.shape[1]
    h = jnp.concatenate([ctx[0], x[0]], axis=0)
    tabs = _rope_tables(h.shape[0], ctx_len)
    h, saved0 = _layer_fwd(h, layers[0], weight_of(0), _tie(mods[0], "shift1", started_token), tabs, ctx_len, False,
                           "l0_")
    h, saved1 = _layer_fwd(h, layers[1], weight_of(1), mods[1], tabs, ctx_len, True, "l1_")
    loss_tile, dh, dfg = final_loss(h, final_norm_g[None], loss_target[0])

    keys = [[f"l{l}_{nm}" for nm in BIG_NAMES] for l in (0, 1)]
    n_big = len(BIG_NAMES)
    dh, ffn_big, ffn_small, ffn_mod = _layer_bwd_ffn(dh, saved1, layers[1], mods[1], "l1_")
    dh, mix_big, mix_small, mix_mod = _layer_bwd_mixer(dh, saved1, layers[1], mods[1], tabs, ctx_len, True, "l1_")
    bigs, smalls, dmods = [None, {**ffn_big, **mix_big}], [None, {**ffn_small, **mix_small}], [None, None]
    dmods[1] = _dmod_rows(mix_mod, ffn_mod)
    grads1 = [bigs[1][nm] for nm in BIG_NAMES]
    lands = [lax.empty((g.shape[0], g.shape[1] // 2, g.shape[2]), g.dtype) for g in grads1]
    send_a, recv_a, fly_a, token_a = split_start("l1_sibling_halves_start", sibling_half_copies(n_big), n_big,
                                                 grads1 + lands, dh)
    dh, ffn_big, ffn_small, ffn_mod = _layer_bwd_ffn(dh, saved0, layers[0], _tie(mods[0], "gate2", token_a), "l0_")
    fly_a = split_wait("l1_sibling_halves_wait", sibling_half_copies(n_big), send_a, recv_a, fly_a, dh)
    chip_sums1 = [add_own_half(g, r, ci, k + "_add_halves")
                  for k, g, r in zip(keys[1], fly_a[:n_big], fly_a[n_big:])]
    lands = [lax.empty((3,) + s.shape[1:], s.dtype) for s in chip_sums1]
    send_b, recv_b, fly_b, token_b = split_start("l1_chip_quarters_start", chip_quarter_copies(n_big), 3 * n_big,
                                                 chip_sums1 + lands, dh)
    dh, mix_big, mix_small, mix_mod = _layer_bwd_mixer(dh, saved0, layers[0], _tie(mods[0], "gate1", token_b), tabs,
                                                       ctx_len, False, "l0_")
    fly_b = split_wait("l1_chip_quarters_wait", chip_quarter_copies(n_big), send_b, recv_b, fly_b, dh)
    bigs[0], smalls[0], dmods[0] = {**ffn_big, **mix_big}, {**ffn_small, **mix_small}, _dmod_rows(mix_mod, ffn_mod)
    dx = dh[ctx_len:]

    partial = {"l0_conv_w": smalls[0]["conv_w"], "l1_sink": smalls[1]["sink"], "l1_pool_w": smalls[1]["pool_w"],
               "l1_pool_scale": smalls[1]["pool_scale"], "final_norm_g": dfg}
    for l in (0, 1):
        partial.update({f"l{l}_dmod_lat": dmods[l][1], f"l{l}_dmod_ctx": dmods[l][0],
                        f"l{l}_b_mod": dmods[l][0] + dmods[l][1], f"l{l}_norm1_g": smalls[l]["norm1_g"],
                        f"l{l}_norm2_g": smalls[l]["norm2_g"], f"l{l}_q_norm_g": smalls[l]["q_g"],
                        f"l{l}_k_norm_g": smalls[l]["k_g"]})
    sizes = [int(np.prod(partial[name].shape)) for name, _ in SMALL_SLOTS]
    offs = _offsets(sizes)
    gpack = _pack(partial, sizes)
    rows = gpack.shape[0]
    small_all = all_gather8(gpack, "gather_small").reshape(N_DEV, rows, 128)
    packs = [_pack({name: a[pre + par] for name, par in SMALL_SLOTS if par}, sizes) for pre in ("", "m_", "v_")]
    small_out = [t.reshape(-1) for t in small_update(small_all, *packs)]
    flat_all = small_all.reshape(N_DEV, rows * 128)

    def slot(flat, name, shape):
        o, size = offs[name]
        return flat[o:o + size].reshape(shape)

    results = {}
    for name, par in SMALL_SLOTS:
        if par:
            results[par] = tuple(slot(t, name, a[par].shape) for t in small_out)

    ctx_rows = []
    for l in (0, 1):
        o_lat, o_ctx = offs[f"l{l}_dmod_lat"][0], offs[f"l{l}_dmod_ctx"][0]
        lat = lax.dynamic_slice(flat_all, (0, o_lat + p * ns_mod), (N_DEV, ns_mod))
        cxr = lax.dynamic_slice(flat_all, (0, o_ctx + p * ns_mod), (N_DEV, ns_mod))
        ctx_rows.append(cxr)
        results[f"l{l}_w_mod"] = tuple(wmod_update(c16_bwd, jnp.concatenate([lat, cxr], axis=0), a[f"l{l}_w_mod"],
                                                   a[f"m_l{l}_w_mod"], a[f"v_l{l}_w_mod"], f"l{l}_w_mod_update"))
    part = cctx_partial(ctx_rows[0], l0_w_mod, ctx_rows[1], l1_w_mod)
    parts4 = all_gather8(part, "gather_cctx").reshape(N_DEV, 8, d)[0::2, 0]
    results["c_ctx"] = tuple(t[0] for t in cctx_update(parts4, c_ctx[None], m_c_ctx[None], v_c_ctx[None]))

    gconv = lax.dynamic_slice(slot(small_out[0], "l0_conv_w", (3, N_SHARD * conv_cols)), (0, p * conv_cols),
                              (3, conv_cols))
    results["l0_conv_w"] = (gconv,) + tuple(adamw(l0_conv_w, gconv, m_l0_conv_w, v_l0_conv_w, "l0_conv_w_adamw"))

    grads0 = [bigs[0][nm] for nm in BIG_NAMES]
    recv0 = exchange_sibling_halves(grads0)
    chip_sums0 = [add_own_half(g, r, ci, k + "_add_halves") for k, g, r in zip(keys[0], grads0, recv0)]
    quarters0 = exchange_chip_quarters(chip_sums0)
    halves = [add_quarters(own, q, p, ci, k + "_add_quarters")
              for k, own, q in zip(keys[0] + keys[1], chip_sums0 + list(fly_b[:n_big]),
                                   list(quarters0) + list(fly_b[n_big:]))]
    fulls = exchange_final_halves(halves)
    for k, g in zip(keys[0] + keys[1], fulls):
        results[k] = (g,) + tuple(adamw(a[k], g, a["m_" + k], a["v_" + k], k + "_adamw"))

    loss = lax.psum(loss_tile[0, 0], ("x", "y", "c"))
    out = [loss, dx[None]]
    for j in range(4):
        out += [results[k][j] for k in WEIGHT_NAMES]
    return tuple(out)
```

```python
import numpy as np
import jax
import jax.numpy as jnp
from jax import lax
from jax.experimental import pallas as pl
from jax.experimental.pallas import tpu as pltpu

F32 = jnp.float32
BF16 = jnp.bfloat16
MESH = pl.DeviceIdType.MESH

HEAD_DIM = 128
Q_HEADS = 12
KV_HEADS = 4
GROUPS = Q_HEADS // KV_HEADS
ATT_WIDTH = Q_HEADS * HEAD_DIM
KV_WIDTH = KV_HEADS * HEAD_DIM
QKV_WIDTH = ATT_WIDTH + 2 * KV_WIDTH
AUX_WIDTH = 512
AUX_GROUPS = 4
POOL_WINDOWS = (2, 4, 8, 16)
WINDOW = 128
GRID_W = 64
ROPE_THETA = 10000.0
N_MOD = 6
EPS = 1e-6
NEG_INF = -1e30
ATT_SCALE = HEAD_DIM ** -0.5
N_SHARD = 4
N_DEV = 8

ADAM_LR = 0.001
ADAM_B1 = 0.9
ADAM_B2 = 0.999
ADAM_EPS = 1e-08
ADAM_WD = 0.01
ADAM_STEP = 10

TM = 256
BAND = TM + 2 * WINDOW
VMEM_LIMIT = 56 * 1024 * 1024

NT_DIMS = (((1,), (1,)), ((), ()))
TN_DIMS = (((0,), (0,)), ((), ()))


def _pcall(body, **kw):
    return pl.pallas_call(body, **kw)


def _params(*sem):
    return pltpu.CompilerParams(dimension_semantics=sem, vmem_limit_bytes=VMEM_LIMIT)


def _sel(i):
    return jnp.minimum(i, 1)


def _sel_spec(d):
    return pl.BlockSpec((None, 1, d), lambda i: (_sel(i), 0, 0))


def _dot(a, b):
    return jnp.dot(a, b, preferred_element_type=F32)


def _dot_nt(a, b):
    return lax.dot_general(a, b, NT_DIMS, preferred_element_type=F32)


def _dot_tn(a, b):
    return lax.dot_general(a, b, TN_DIMS, preferred_element_type=F32)


def _sigmoid(x):
    return 1.0 / (1.0 + jnp.exp(-x))


def norm_mod_fwd(h, g, shift, scale, name):
    t, d = h.shape

    def body(h_ref, g_ref, sh_ref, sc_ref, o_ref):
        x = h_ref[...]
        r = lax.rsqrt(jnp.mean(x * x, axis=-1, keepdims=True) + EPS)
        y = x * r * g_ref[...]
        o_ref[...] = (y * (1.0 + sc_ref[...]) + sh_ref[...]).astype(BF16)

    return _pcall(
        body, name=name, grid=(t // TM,),
        in_specs=[pl.BlockSpec((TM, d), lambda i: (i, 0)), pl.BlockSpec((1, d), lambda i: (0, 0)),
                  _sel_spec(d), _sel_spec(d)],
        out_specs=pl.BlockSpec((TM, d), lambda i: (i, 0)),
        out_shape=jax.ShapeDtypeStruct((t, d), BF16),
        compiler_params=_params("parallel"),
    )(h, g, shift, scale)


def norm_mod_bwd(dy, h, g, scale, dres, name):
    t, d = h.shape

    def body(dy_ref, h_ref, g_ref, sc_ref, dres_ref, dh_ref, dsh_ref, dsc_ref, dg_ref):
        i = pl.program_id(0)

        @pl.when(i == 0)
        def _():
            dsh_ref[...] = jnp.zeros_like(dsh_ref)
            dsc_ref[...] = jnp.zeros_like(dsc_ref)
            dg_ref[...] = jnp.zeros_like(dg_ref)

        x = h_ref[...]
        gv = g_ref[...]
        r = lax.rsqrt(jnp.mean(x * x, axis=-1, keepdims=True) + EPS)
        xhat = x * r
        dyv = dy_ref[...]
        s = _sel(i)
        dsh_ref[pl.ds(s, 1)] += jnp.sum(dyv, axis=0, keepdims=True)[None]
        dsc_ref[pl.ds(s, 1)] += jnp.sum(dyv * xhat * gv, axis=0, keepdims=True)[None]
        tt = dyv * (1.0 + sc_ref[...])
        dg_ref[...] += jnp.sum(tt * xhat, axis=0, keepdims=True)
        dxhat = tt * gv
        dx = r * (dxhat - xhat * jnp.mean(dxhat * xhat, axis=-1, keepdims=True))
        dh_ref[...] = dx + dres_ref[...]

    row = pl.BlockSpec((TM, d), lambda i: (i, 0))
    acc2 = pl.BlockSpec((2, 1, d), lambda i: (0, 0, 0))
    return _pcall(
        body, name=name, grid=(t // TM,),
        in_specs=[row, row, pl.BlockSpec((1, d), lambda i: (0, 0)), _sel_spec(d), row],
        out_specs=[row, acc2, acc2, pl.BlockSpec((1, d), lambda i: (0, 0))],
        out_shape=[jax.ShapeDtypeStruct((t, d), F32), jax.ShapeDtypeStruct((2, 1, d), F32),
                   jax.ShapeDtypeStruct((2, 1, d), F32), jax.ShapeDtypeStruct((1, d), F32)],
        compiler_params=_params("arbitrary"),
    )(dy, h, g, scale, dres)


def gate_bwd(dh, f, gate, name):
    t, d = dh.shape

    def body(dh_ref, f_ref, gt_ref, df_ref, dgt_ref):
        i = pl.program_id(0)

        @pl.when(i == 0)
        def _():
            dgt_ref[...] = jnp.zeros_like(dgt_ref)

        dhv = dh_ref[...]
        df_ref[...] = (dhv * gt_ref[...]).astype(BF16)
        dgt_ref[pl.ds(_sel(i), 1)] += jnp.sum(dhv * f_ref[...], axis=0, keepdims=True)[None]

    row = pl.BlockSpec((TM, d), lambda i: (i, 0))
    return _pcall(
        body, name=name, grid=(t // TM,),
        in_specs=[row, row, _sel_spec(d)],
        out_specs=[row, pl.BlockSpec((2, 1, d), lambda i: (0, 0, 0))],
        out_shape=[jax.ShapeDtypeStruct((t, d), BF16), jax.ShapeDtypeStruct((2, 1, d), F32)],
        compiler_params=_params("arbitrary"),
    )(dh, f, gate)


def final_loss(h, fg, target):
    t, d = h.shape

    def body(h_ref, g_ref, tg_ref, loss_ref, dh_ref, dg_ref):
        i = pl.program_id(0)

        @pl.when(i == 0)
        def _():
            loss_ref[...] = jnp.zeros_like(loss_ref)
            dg_ref[...] = jnp.zeros_like(dg_ref)
            dh_ref[...] = jnp.zeros_like(dh_ref)

        @pl.when(i > 0)
        def _():
            x = h_ref[...]
            gv = g_ref[...]
            r = lax.rsqrt(jnp.mean(x * x, axis=-1, keepdims=True) + EPS)
            xhat = x * r
            diff = xhat * gv - tg_ref[...]
            loss_ref[...] += 0.5 * jnp.sum(jnp.mean(diff * diff, axis=-1, keepdims=True), axis=0, keepdims=True)
            dout = diff * (1.0 / d)
            dg_ref[...] += jnp.sum(dout * xhat, axis=0, keepdims=True)
            dxhat = dout * gv
            dh_ref[...] = r * (dxhat - xhat * jnp.mean(dxhat * xhat, axis=-1, keepdims=True))

    row = pl.BlockSpec((TM, d), lambda i: (i, 0))
    return _pcall(
        body, name="final_loss", grid=(t // TM,),
        in_specs=[row, pl.BlockSpec((1, d), lambda i: (0, 0)),
                  pl.BlockSpec((TM, d), lambda i: (jnp.maximum(i - 1, 0), 0))],
        out_specs=[pl.BlockSpec((8, 128), lambda i: (0, 0)), row, pl.BlockSpec((1, d), lambda i: (0, 0))],
        out_shape=[jax.ShapeDtypeStruct((8, 128), F32), jax.ShapeDtypeStruct((t, d), F32),
                   jax.ShapeDtypeStruct((1, d), F32)],
        compiler_params=_params("arbitrary"),
    )(h, fg, target)


def _w_nn_spec(w, k):
    if w.shape[1] == k:
        ns = w.shape[2]
        return w, pl.BlockSpec((None, k, ns), lambda j, i: (j, 0, 0)), w.shape[0], ns
    w2 = w.reshape(w.shape[0] * w.shape[1], w.shape[2])
    tn = 1024 if w2.shape[1] % 1024 == 0 else w2.shape[1]
    return w2, pl.BlockSpec((k, tn), lambda j, i: (0, j)), w2.shape[1] // tn, tn


def mm_nn(a, w, name, out_dtype=F32):
    t, k = a.shape
    w, wspec, nb, tn = _w_nn_spec(w, k)

    def body(a_ref, w_ref, o_ref):
        o_ref[...] = _dot(a_ref[...], w_ref[...]).astype(o_ref.dtype)

    return _pcall(
        body, name=name, grid=(nb, t // TM),
        in_specs=[pl.BlockSpec((TM, k), lambda j, i: (i, 0)), wspec],
        out_specs=pl.BlockSpec((TM, tn), lambda j, i: (i, j)),
        out_shape=jax.ShapeDtypeStruct((t, nb * tn), out_dtype),
        compiler_params=_params("parallel", "parallel"),
    )(a, w)


def mm_nn_residual(a, w, h, gate, name):
    t, k = a.shape
    w, wspec, nb, tn = _w_nn_spec(w, k)

    def body(a_ref, w_ref, h_ref, gt_ref, y_ref, o_ref):
        y = _dot(a_ref[...], w_ref[...])
        y_ref[...] = y
        o_ref[...] = h_ref[...] + gt_ref[...] * y

    out = pl.BlockSpec((TM, tn), lambda j, i: (i, j))
    return _pcall(
        body, name=name, grid=(nb, t // TM),
        in_specs=[pl.BlockSpec((TM, k), lambda j, i: (i, 0)), wspec, out,
                  pl.BlockSpec((None, 1, tn), lambda j, i: (_sel(i), 0, j))],
        out_specs=[out, out],
        out_shape=[jax.ShapeDtypeStruct((t, nb * tn), F32)] * 2,
        compiler_params=_params("parallel", "parallel"),
    )(a, w, h, gate)


def mm_swiglu(a, wg, wu, name):
    t, k = a.shape
    s, _, ns = wg.shape

    def body(a_ref, wg_ref, wu_ref, g_ref, u_ref, act_ref):
        av = a_ref[...]
        g = _dot(av, wg_ref[...])
        u = _dot(av, wu_ref[...])
        g_ref[...] = g
        u_ref[...] = u
        act_ref[...] = (g * _sigmoid(g) * u).astype(BF16)

    wspec = pl.BlockSpec((None, k, ns), lambda j, i: (j, 0, 0))
    out = pl.BlockSpec((TM, ns), lambda j, i: (i, j))
    return _pcall(
        body, name=name, grid=(s, t // TM),
        in_specs=[pl.BlockSpec((TM, k), lambda j, i: (i, 0)), wspec, wspec],
        out_specs=[out, out, out],
        out_shape=[jax.ShapeDtypeStruct((t, s * ns), F32), jax.ShapeDtypeStruct((t, s * ns), F32),
                   jax.ShapeDtypeStruct((t, s * ns), BF16)],
        compiler_params=_params("parallel", "parallel"),
    )(a, wg, wu)


def mm_nt_cols(dy, w, name, add=None):
    t, n = dy.shape
    s, k, ns = w.shape
    tk = 512

    def body(*refs):
        if add is None:
            dy_ref, w_ref, o_ref = refs
        else:
            dy_ref, w_ref, add_ref, o_ref = refs
        acc = _dot_nt(dy_ref[:, 0:ns], w_ref[0])
        for sh in range(1, s):
            acc += _dot_nt(dy_ref[:, sh * ns:(sh + 1) * ns], w_ref[sh])
        if add is not None:
            acc += add_ref[...]
        o_ref[...] = acc

    out = pl.BlockSpec((TM, tk), lambda j, i: (i, j))
    in_specs = [pl.BlockSpec((TM, n), lambda j, i: (i, 0)), pl.BlockSpec((s, tk, ns), lambda j, i: (0, j, 0))]
    args = [dy, w]
    if add is not None:
        in_specs.append(out)
        args.append(add)
    return _pcall(
        body, name=name, grid=(k // tk, t // TM),
        in_specs=in_specs, out_specs=out,
        out_shape=jax.ShapeDtypeStruct((t, k), F32),
        compiler_params=_params("parallel", "parallel"),
    )(*args)


def mm_nt_rows(dy, w, name):
    t, n = dy.shape
    s, ks, _ = w.shape

    def body(dy_ref, w_ref, o_ref):
        o_ref[...] = _dot_nt(dy_ref[...], w_ref[...])

    return _pcall(
        body, name=name, grid=(s, t // TM),
        in_specs=[pl.BlockSpec((TM, n), lambda j, i: (i, 0)), pl.BlockSpec((None, ks, n), lambda j, i: (j, 0, 0))],
        out_specs=pl.BlockSpec((TM, ks), lambda j, i: (i, j)),
        out_shape=jax.ShapeDtypeStruct((t, s * ks), F32),
        compiler_params=_params("parallel", "parallel"),
    )(dy, w)


def mm_nt_swiglu_bwd(df, wd, g, u, name):
    t, n = df.shape
    s, ks, _ = wd.shape

    def body(df_ref, w_ref, g_ref, u_ref, dg_ref, du_ref):
        da = _dot_nt(df_ref[...], w_ref[...])
        gv = g_ref[...]
        sg = _sigmoid(gv)
        silu = gv * sg
        dg_ref[...] = (da * u_ref[...] * (sg * (1.0 + gv * (1.0 - sg)))).astype(BF16)
        du_ref[...] = (da * silu).astype(BF16)

    blk = pl.BlockSpec((TM, ks), lambda j, i: (i, j))
    return _pcall(
        body, name=name, grid=(s, t // TM),
        in_specs=[pl.BlockSpec((TM, n), lambda j, i: (i, 0)), pl.BlockSpec((None, ks, n), lambda j, i: (j, 0, 0)),
                  blk, blk],
        out_specs=[blk, blk],
        out_shape=[jax.ShapeDtypeStruct((t, s * ks), BF16)] * 2,
        compiler_params=_params("parallel", "parallel"),
    )(df, wd, g, u)


def mm_tn(x, dy, name, col_sharded):
    t, k = x.shape
    n = dy.shape[1]
    tk = 512
    mc = t // 2 if (t // 2) % 128 == 0 else t
    if col_sharded:
        tn = n // N_SHARD
        out_shape = jax.ShapeDtypeStruct((N_SHARD, k, tn), F32)
        out_spec = pl.BlockSpec((None, tk, tn), lambda j, kb, m: (j, kb, 0))
    else:
        tn = 1024
        out_shape = jax.ShapeDtypeStruct((k, n), F32)
        out_spec = pl.BlockSpec((tk, tn), lambda j, kb, m: (kb, j))

    def body(x_ref, dy_ref, o_ref):
        m = pl.program_id(2)
        acc = _dot_tn(x_ref[...], dy_ref[...])

        @pl.when(m == 0)
        def _():
            o_ref[...] = acc

        @pl.when(m > 0)
        def _():
            o_ref[...] += acc

    out = _pcall(
        body, name=name, grid=(n // tn, k // tk, t // mc),
        in_specs=[pl.BlockSpec((mc, tk), lambda j, kb, m: (m, kb)), pl.BlockSpec((mc, tn), lambda j, kb, m: (m, j))],
        out_specs=out_spec, out_shape=out_shape,
        compiler_params=_params("parallel", "parallel", "arbitrary"),
    )(x, dy)
    return out if col_sharded else out.reshape(N_SHARD, k // N_SHARD, n)


def _swap32(y):
    right = pltpu.roll(y, 32, 1)
    left = pltpu.roll(y, 96, 1)
    lane = lax.broadcasted_iota(jnp.int32, y.shape, 1)
    return jnp.where((lane // 32) % 2 == 0, left, right)


def _rope_tables(t, ctx_len):
    s = t - ctx_len
    pos = np.arange(s)
    row = (pos // GRID_W).astype(np.float32)
    col = (pos % GRID_W).astype(np.float32)
    half = HEAD_DIM // 2
    inv = np.power(np.float32(ROPE_THETA), -np.arange(0, half, 2, dtype=np.float32) / np.float32(half))
    ar = row[:, None] * inv
    ac = col[:, None] * inv
    cos = np.concatenate([np.cos(ar), np.cos(ar), np.cos(ac), np.cos(ac)], axis=1)
    sin = np.concatenate([-np.sin(ar), np.sin(ar), -np.sin(ac), np.sin(ac)], axis=1)
    cos = np.concatenate([np.ones((ctx_len, HEAD_DIM), np.float32), cos.astype(np.float32)], axis=0)
    sin = np.concatenate([np.zeros((ctx_len, HEAD_DIM), np.float32), sin.astype(np.float32)], axis=0)
    return jnp.asarray(cos, F32), jnp.asarray(sin, F32)


def qk_prep_fwd(proj, qg, kg, cos, sin, name):
    t = proj.shape[0]

    def body(p_ref, qg_ref, kg_ref, c_ref, s_ref, q_ref, k_ref, v_ref):
        cv = c_ref[...]
        sv = s_ref[...]
        for hd in range(Q_HEADS + KV_HEADS):
            x = p_ref[:, hd * HEAD_DIM:(hd + 1) * HEAD_DIM]
            gv = qg_ref[...] if hd < Q_HEADS else kg_ref[...]
            y = x * lax.rsqrt(jnp.mean(x * x, axis=-1, keepdims=True) + EPS) * gv
            out = (y * cv + _swap32(y) * sv).astype(BF16)
            if hd < Q_HEADS:
                q_ref[:, hd * HEAD_DIM:(hd + 1) * HEAD_DIM] = out
            else:
                k_ref[:, (hd - Q_HEADS) * HEAD_DIM:(hd - Q_HEADS + 1) * HEAD_DIM] = out
        v_ref[...] = p_ref[:, ATT_WIDTH + KV_WIDTH:QKV_WIDTH].astype(BF16)

    vec = pl.BlockSpec((1, HEAD_DIM), lambda i: (0, 0))
    tab = pl.BlockSpec((TM, HEAD_DIM), lambda i: (i, 0))
    return _pcall(
        body, name=name, grid=(t // TM,),
        in_specs=[pl.BlockSpec((TM, QKV_WIDTH), lambda i: (i, 0)), vec, vec, tab, tab],
        out_specs=[pl.BlockSpec((TM, ATT_WIDTH), lambda i: (i, 0)), pl.BlockSpec((TM, KV_WIDTH), lambda i: (i, 0)),
                   pl.BlockSpec((TM, KV_WIDTH), lambda i: (i, 0))],
        out_shape=[jax.ShapeDtypeStruct((t, ATT_WIDTH), BF16), jax.ShapeDtypeStruct((t, KV_WIDTH), BF16),
                   jax.ShapeDtypeStruct((t, KV_WIDTH), BF16)],
        compiler_params=_params("parallel"),
    )(proj, qg, kg, cos, sin)


def qk_prep_bwd(dq, dk, dv, daux, proj, qg, kg, cos, sin, name):
    t, n_in = proj.shape
    n_aux = daux.shape[0]

    def body(dq_ref, dk_ref, dv_ref, da_ref, p_ref, qg_ref, kg_ref, c_ref, s_ref, o_ref, dqg_ref, dkg_ref):
        @pl.when(pl.program_id(0) == 0)
        def _():
            dqg_ref[...] = jnp.zeros_like(dqg_ref)
            dkg_ref[...] = jnp.zeros_like(dkg_ref)

        cv = c_ref[...]
        sv = s_ref[...]
        for hd in range(Q_HEADS + KV_HEADS):
            cols = slice(hd * HEAD_DIM, (hd + 1) * HEAD_DIM)
            x = p_ref[:, cols]
            if hd < Q_HEADS:
                gv, dyr, dg_ref = qg_ref[...], dq_ref[:, cols], dqg_ref
            else:
                kc = slice((hd - Q_HEADS) * HEAD_DIM, (hd - Q_HEADS + 1) * HEAD_DIM)
                gv, dyr, dg_ref = kg_ref[...], dk_ref[:, kc], dkg_ref
            r = lax.rsqrt(jnp.mean(x * x, axis=-1, keepdims=True) + EPS)
            xhat = x * r
            dy = dyr * cv + _swap32(dyr * sv)
            dg_ref[...] += jnp.sum(dy * xhat, axis=0, keepdims=True)
            dxhat = dy * gv
            o_ref[:, cols] = (r * (dxhat - xhat * jnp.mean(dxhat * xhat, axis=-1, keepdims=True))).astype(BF16)
        o_ref[:, ATT_WIDTH + KV_WIDTH:QKV_WIDTH] = dv_ref[...].astype(BF16)
        for a in range(n_aux):
            o_ref[:, QKV_WIDTH + a * AUX_WIDTH:QKV_WIDTH + (a + 1) * AUX_WIDTH] = da_ref[a]

    vec = pl.BlockSpec((1, HEAD_DIM), lambda i: (0, 0))
    tab = pl.BlockSpec((TM, HEAD_DIM), lambda i: (i, 0))
    return _pcall(
        body, name=name, grid=(t // TM,),
        in_specs=[pl.BlockSpec((TM, ATT_WIDTH), lambda i: (i, 0)), pl.BlockSpec((TM, KV_WIDTH), lambda i: (i, 0)),
                  pl.BlockSpec((TM, KV_WIDTH), lambda i: (i, 0)),
                  pl.BlockSpec((n_aux, TM, AUX_WIDTH), lambda i: (0, i, 0)),
                  pl.BlockSpec((TM, QKV_WIDTH), lambda i: (i, 0)), vec, vec, tab, tab],
        out_specs=[pl.BlockSpec((TM, n_in), lambda i: (i, 0)), vec, vec],
        out_shape=[jax.ShapeDtypeStruct((t, n_in), BF16), jax.ShapeDtypeStruct((1, HEAD_DIM), F32),
                   jax.ShapeDtypeStruct((1, HEAD_DIM), F32)],
        compiler_params=_params("arbitrary"),
    )(dq, dk, dv, daux, proj, qg, kg, cos, sin)


def _band_start(i, t):
    return pl.multiple_of(jnp.clip(i * TM - WINDOW, 0, t - BAND), WINDOW)


def _band_mask(i, start, ctx_len):
    shape = (TM, ctx_len + BAND)
    col = lax.broadcasted_iota(jnp.int32, shape, 1)
    qrow = i * TM + lax.broadcasted_iota(jnp.int32, shape, 0)
    krow = start + col - ctx_len
    band_ok = (krow >= ctx_len) & (jnp.abs(krow - qrow) <= WINDOW)
    return (col < ctx_len) | band_ok


def attention_fwd(q, k, v, sink, ctx_len, windowed, name):
    t = q.shape[0]
    d_model = ATT_WIDTH + AUX_WIDTH
    gw = GROUPS * HEAD_DIM

    def one_head(qh, kk, vv, mask, sink_val):
        s = _dot_nt(qh, kk) * ATT_SCALE
        if mask is not None:
            s = jnp.where(mask, s, NEG_INF)
        m = jnp.max(s, axis=-1, keepdims=True)
        if sink_val is not None:
            m = jnp.maximum(m, sink_val)
        p = jnp.exp(s - m)
        l = jnp.sum(p, axis=-1, keepdims=True)
        if sink_val is not None:
            l = l + jnp.exp(sink_val - m)
        o = _dot(p.astype(BF16), vv) / l
        return o, m + jnp.log(l)

    def body(sink_ref, q_ref, k_ref, v_ref, o_ref, lse_ref):
        kv = pl.program_id(0)
        i = pl.program_id(1)

        def run(kk, vv, mask):
            for g in range(GROUPS):
                sink_val = sink_ref[kv * GROUPS + g] if windowed else None
                o, lse = one_head(q_ref[:, g * HEAD_DIM:(g + 1) * HEAD_DIM], kk, vv, mask, sink_val)
                o_ref[:, g * HEAD_DIM:(g + 1) * HEAD_DIM] = o.astype(BF16)
                lse_ref[g] = lse

        @pl.when(i == 0)
        def _():
            if windowed:
                o_ref[...] = jnp.zeros_like(o_ref)
                lse_ref[...] = jnp.zeros_like(lse_ref)
            else:
                run(k_ref[0:ctx_len], v_ref[0:ctx_len], None)

        @pl.when(i > 0)
        def _():
            if windowed:
                start = _band_start(i, t)
                kk = jnp.concatenate([k_ref[0:ctx_len], k_ref[pl.ds(start, BAND)]], axis=0)
                vv = jnp.concatenate([v_ref[0:ctx_len], v_ref[pl.ds(start, BAND)]], axis=0)
                run(kk, vv, _band_mask(i, start, ctx_len))
            else:
                run(k_ref[...], v_ref[...], None)

    kvspec = pl.BlockSpec((t, HEAD_DIM), lambda kv, i: (0, kv))
    return _pcall(
        body, name=name, grid=(KV_HEADS, t // TM),
        in_specs=[pl.BlockSpec(memory_space=pltpu.SMEM), pl.BlockSpec((TM, gw), lambda kv, i: (i, kv)), kvspec, kvspec],
        out_specs=[pl.BlockSpec((TM, gw), lambda kv, i: (i, kv)),
                   pl.BlockSpec((GROUPS, TM, 1), lambda kv, i: (kv, i, 0))],
        out_shape=[jax.ShapeDtypeStruct((t, d_model), BF16), jax.ShapeDtypeStruct((Q_HEADS, t, 1), F32)],
        compiler_params=_params("parallel", "parallel"),
    )(sink, q, k, v)


def attention_bwd(q, k, v, cat, lse, dcat, sink, ctx_len, windowed, name):
    t = q.shape[0]
    gw = GROUPS * HEAD_DIM

    def body(sink_ref, q_ref, k_ref, v_ref, o_ref, lse_ref, do_ref, dq_ref, dk_ref, dv_ref, dsink_ref):
        kv = pl.program_id(0)
        i = pl.program_id(1)

        @pl.when(i == 0)
        def _():
            dk_ref[...] = jnp.zeros_like(dk_ref)
            dv_ref[...] = jnp.zeros_like(dv_ref)
            dsink_ref[...] = jnp.zeros_like(dsink_ref)

        def run(kk, vv, mask, accumulate):
            for g in range(GROUPS):
                cols = slice(g * HEAD_DIM, (g + 1) * HEAD_DIM)
                qh = q_ref[:, cols]
                doh = do_ref[:, cols]
                delta = jnp.sum(doh * o_ref[:, cols].astype(F32), axis=-1, keepdims=True)
                lse_g = lse_ref[g]
                s = _dot_nt(qh, kk) * ATT_SCALE
                if mask is not None:
                    s = jnp.where(mask, s, NEG_INF)
                p = jnp.exp(s - lse_g)
                dob = doh.astype(BF16)
                dp = _dot_nt(dob, vv)
                ds = (p * (dp - delta) * ATT_SCALE).astype(BF16)
                dq_ref[:, cols] = _dot(ds, kk)
                accumulate(_dot_tn(ds, qh), _dot_tn(p.astype(BF16), dob))
                if windowed:
                    p_sink = jnp.exp(sink_ref[kv * GROUPS + g] - lse_g)
                    dsink_ref[g:g + 1, :] += jnp.sum(-p_sink * delta, axis=0, keepdims=True)

        @pl.when(i == 0)
        def _():
            if windowed:
                dq_ref[...] = jnp.zeros_like(dq_ref)
            else:
                def acc_ctx(dkp, dvp):
                    dk_ref[0:ctx_len] += dkp
                    dv_ref[0:ctx_len] += dvp

                run(k_ref[0:ctx_len], v_ref[0:ctx_len], None, acc_ctx)

        @pl.when(i > 0)
        def _():
            if windowed:
                start = _band_start(i, t)
                kk = jnp.concatenate([k_ref[0:ctx_len], k_ref[pl.ds(start, BAND)]], axis=0)
                vv = jnp.concatenate([v_ref[0:ctx_len], v_ref[pl.ds(start, BAND)]], axis=0)

                def acc_band(dkp, dvp):
                    dk_ref[0:ctx_len] += dkp[0:ctx_len]
                    dv_ref[0:ctx_len] += dvp[0:ctx_len]
                    dk_ref[pl.ds(start, BAND)] += dkp[ctx_len:ctx_len + BAND]
                    dv_ref[pl.ds(start, BAND)] += dvp[ctx_len:ctx_len + BAND]

                run(kk, vv, _band_mask(i, start, ctx_len), acc_band)
            else:
                def acc_all(dkp, dvp):
                    dk_ref[...] += dkp
                    dv_ref[...] += dvp

                run(k_ref[...], v_ref[...], None, acc_all)

    kvspec = pl.BlockSpec((t, HEAD_DIM), lambda kv, i: (0, kv))
    grp = pl.BlockSpec((TM, gw), lambda kv, i: (i, kv))
    return _pcall(
        body, name=name, grid=(KV_HEADS, t // TM),
        in_specs=[pl.BlockSpec(memory_space=pltpu.SMEM), grp, kvspec, kvspec, grp,
                  pl.BlockSpec((GROUPS, TM, 1), lambda kv, i: (kv, i, 0)), grp],
        out_specs=[grp, kvspec, kvspec, pl.BlockSpec((None, 8, 128), lambda kv, i: (kv, 0, 0))],
        out_shape=[jax.ShapeDtypeStruct((t, ATT_WIDTH), F32), jax.ShapeDtypeStruct((t, KV_WIDTH), F32),
                   jax.ShapeDtypeStruct((t, KV_WIDTH), F32), jax.ShapeDtypeStruct((KV_HEADS, 8, 128), F32)],
        compiler_params=_params("parallel", "arbitrary"),
    )(sink, q, k, v, cat, lse, dcat)


def _segment_pos(t, ctx_len, width):
    row = lax.broadcasted_iota(jnp.int32, (t, width), 0)
    in_ctx = row < ctx_len
    return jnp.where(in_ctx, row, row - ctx_len), jnp.where(in_ctx, ctx_len, t - ctx_len)


def _shifted(x, offset, pos, seg_len):
    t = x.shape[0]
    moved = x if offset == 0 else pltpu.roll(x, (-offset) % t, 0)
    ok = (pos + offset >= 0) & (pos + offset < seg_len)
    return jnp.where(ok, moved, 0.0)


def conv_fwd(proj, conv_w, cat, ctx_len, name):
    t = proj.shape[0]
    base = QKV_WIDTH // 128
    per = AUX_WIDTH // 128

    def body(gb_ref, gc_ref, u_ref, w_ref, cat_in, o_ref):
        del cat_in
        pos, seg = _segment_pos(t, ctx_len, 128)
        z = gc_ref[...] * u_ref[...]
        conv = (w_ref[0:1, :] * _shifted(z, -1, pos, seg) + w_ref[1:2, :] * z
                + w_ref[2:3, :] * _shifted(z, 1, pos, seg))
        o_ref[...] = (gb_ref[...] * conv).astype(BF16)

    def col(off):
        return pl.BlockSpec((t, 128), lambda j: (0, base + off * per + j))

    return _pcall(
        body, name=name, grid=(per,),
        in_specs=[col(0), col(1), col(2), pl.BlockSpec((3, 128), lambda j: (0, j)),
                  pl.BlockSpec(memory_space=pl.ANY)],
        out_specs=pl.BlockSpec((t, 128), lambda j: (0, ATT_WIDTH // 128 + j)),
        out_shape=jax.ShapeDtypeStruct(cat.shape, BF16),
        input_output_aliases={4: 0},
        compiler_params=_params("parallel"),
    )(proj, proj, proj, conv_w, cat)


def conv_bwd(dcat, proj, conv_w, ctx_len, name):
    t = proj.shape[0]
    base = QKV_WIDTH // 128
    per = AUX_WIDTH // 128

    def body(do_ref, gb_ref, gc_ref, u_ref, w_ref, da_ref, dw_ref):
        pos, seg = _segment_pos(t, ctx_len, 128)
        gc = gc_ref[...]
        u = u_ref[...]
        z = gc * u
        zm = _shifted(z, -1, pos, seg)
        zp = _shifted(z, 1, pos, seg)
        conv = w_ref[0:1, :] * zm + w_ref[1:2, :] * z + w_ref[2:3, :] * zp
        dout = do_ref[...]
        da_ref[0] = (dout * conv).astype(BF16)
        dconv = dout * gb_ref[...]
        dw_ref[0:1, :] = jnp.sum(dconv * zm, axis=0, keepdims=True)
        dw_ref[1:2, :] = jnp.sum(dconv * z, axis=0, keepdims=True)
        dw_ref[2:3, :] = jnp.sum(dconv * zp, axis=0, keepdims=True)
        dz = (w_ref[1:2, :] * dconv + w_ref[0:1, :] * _shifted(dconv, 1, pos, seg)
              + w_ref[2:3, :] * _shifted(dconv, -1, pos, seg))
        da_ref[1] = (dz * u).astype(BF16)
        da_ref[2] = (dz * gc).astype(BF16)

    def col(off):
        return pl.BlockSpec((t, 128), lambda j: (0, base + off * per + j))

    return _pcall(
        body, name=name, grid=(per,),
        in_specs=[pl.BlockSpec((t, 128), lambda j: (0, ATT_WIDTH // 128 + j)), col(0), col(1), col(2),
                  pl.BlockSpec((3, 128), lambda j: (0, j))],
        out_specs=[pl.BlockSpec((3, t, 128), lambda j: (0, 0, j)), pl.BlockSpec((3, 128), lambda j: (0, j))],
        out_shape=[jax.ShapeDtypeStruct((3, t, AUX_WIDTH), BF16), jax.ShapeDtypeStruct((3, AUX_WIDTH), F32)],
        compiler_params=_params("parallel"),
    )(dcat, proj, proj, proj, conv_w)


def _pooled(u, w, pos, seg):
    lo = jnp.clip(pos - w // 2, 0, seg)
    hi = jnp.clip(pos - w // 2 + w, 0, seg)
    inv = 1.0 / (hi - lo).astype(F32)
    total = _shifted(u, -(w // 2), pos, seg)
    for o in range(-(w // 2) + 1, w // 2):
        total = total + _shifted(u, o, pos, seg)
    return total * inv - u, inv


def pool_fwd(proj, pool_w, pool_scale, cat, ctx_len, name):
    t = proj.shape[0]

    def body(u_ref, w_ref, sc_ref, cat_in, o_ref):
        del cat_in
        pos, seg = _segment_pos(t, ctx_len, 128)
        for g, w in enumerate(POOL_WINDOWS):
            cols = slice(g * 128, (g + 1) * 128)
            pooled, _ = _pooled(u_ref[:, cols], w, pos, seg)
            mixed = _dot(pooled.astype(BF16), w_ref[g].astype(BF16))
            o_ref[:, cols] = (mixed * sc_ref[:, cols]).astype(BF16)

    return _pcall(
        body, name=name, grid=(1,),
        in_specs=[pl.BlockSpec((t, AUX_WIDTH), lambda j: (0, QKV_WIDTH // AUX_WIDTH)),
                  pl.BlockSpec((AUX_GROUPS, 128, 128), lambda j: (0, 0, 0)),
                  pl.BlockSpec((1, AUX_WIDTH), lambda j: (0, 0)), pl.BlockSpec(memory_space=pl.ANY)],
        out_specs=pl.BlockSpec((t, AUX_WIDTH), lambda j: (0, ATT_WIDTH // AUX_WIDTH)),
        out_shape=jax.ShapeDtypeStruct(cat.shape, BF16),
        input_output_aliases={3: 0},
        compiler_params=_params("arbitrary"),
    )(proj, pool_w, pool_scale, cat)


def pool_bwd(dcat, proj, pool_w, pool_scale, ctx_len, name):
    t = proj.shape[0]

    def body(do_ref, u_ref, w_ref, sc_ref, da_ref, dw_ref, dsc_ref):
        pos, seg = _segment_pos(t, ctx_len, 128)
        for g, w in enumerate(POOL_WINDOWS):
            cols = slice(g * 128, (g + 1) * 128)
            pooled, inv = _pooled(u_ref[:, cols], w, pos, seg)
            pb = pooled.astype(BF16)
            wb = w_ref[g].astype(BF16)
            mixed = _dot(pb, wb)
            dout = do_ref[:, cols]
            dsc_ref[:, cols] = jnp.sum(dout * mixed, axis=0, keepdims=True)
            dmixed = (dout * sc_ref[:, cols]).astype(BF16)
            dw_ref[g] = _dot_tn(pb, dmixed)
            dpooled = _dot_nt(dmixed, wb)
            spread = dpooled * inv
            du = _shifted(spread, w // 2, pos, seg) - dpooled
            for o in range(-(w // 2) + 1, w // 2):
                du = du + _shifted(spread, -o, pos, seg)
            da_ref[0, :, cols] = du.astype(BF16)

    return _pcall(
        body, name=name, grid=(1,),
        in_specs=[pl.BlockSpec((t, AUX_WIDTH), lambda j: (0, ATT_WIDTH // AUX_WIDTH)),
                  pl.BlockSpec((t, AUX_WIDTH), lambda j: (0, QKV_WIDTH // AUX_WIDTH)),
                  pl.BlockSpec((AUX_GROUPS, 128, 128), lambda j: (0, 0, 0)),
                  pl.BlockSpec((1, AUX_WIDTH), lambda j: (0, 0))],
        out_specs=[pl.BlockSpec((1, t, AUX_WIDTH), lambda j: (0, 0, 0)),
                   pl.BlockSpec((AUX_GROUPS, 128, 128), lambda j: (0, 0, 0)),
                   pl.BlockSpec((1, AUX_WIDTH), lambda j: (0, 0))],
        out_shape=[jax.ShapeDtypeStruct((1, t, AUX_WIDTH), BF16), jax.ShapeDtypeStruct((AUX_GROUPS, 128, 128), F32),
                   jax.ShapeDtypeStruct((1, AUX_WIDTH), F32)],
        compiler_params=_params("arbitrary"),
    )(dcat, proj, pool_w, pool_scale)


MOD_NAMES = ("shift1", "scale1", "gate1", "shift2", "scale2", "gate2")


def _layer_fwd(h, lw, weight, mods, tabs, ctx_len, windowed, tag):
    cos, sin = tabs
    xn = norm_mod_fwd(h, lw["norm1_g"], mods["shift1"], mods["scale1"], tag + "norm1_fwd")
    lw["w_in"] = weight("w_in", xn)
    proj = mm_nn(xn, lw["w_in"], tag + "w_in_fwd")
    q, k, v = qk_prep_fwd(proj, lw["q_g"], lw["k_g"], cos, sin, tag + "qk_fwd")
    cat, lse = attention_fwd(q, k, v, lw["sink"], ctx_len, windowed, tag + "attn_fwd")
    if windowed:
        cat = pool_fwd(proj, lw["pool_w"], lw["pool_scale"], cat, ctx_len, tag + "pool_fwd")
    else:
        cat = conv_fwd(proj, lw["conv_w"], cat, ctx_len, tag + "conv_fwd")
    lw["w_out"] = weight("w_out", cat)
    y, h1 = mm_nn_residual(cat, lw["w_out"], h, mods["gate1"], tag + "w_out_fwd")
    hn = norm_mod_fwd(h1, lw["norm2_g"], mods["shift2"], mods["scale2"], tag + "norm2_fwd")
    lw["w_gate"], lw["w_up"] = weight("w_gate", hn), weight("w_up", hn)
    g, u, act = mm_swiglu(hn, lw["w_gate"], lw["w_up"], tag + "ffn_in_fwd")
    lw["w_down"] = weight("w_down", act)
    f, h2 = mm_nn_residual(act, lw["w_down"], h1, mods["gate2"], tag + "w_down_fwd")
    saved = dict(h=h, xn=xn, proj=proj, q=q, k=k, v=v, cat=cat, lse=lse, y=y, h1=h1, hn=hn, g=g, u=u, act=act, f=f)
    return h2, saved


def _layer_bwd_ffn(dh, sv, lw, mods, tag):
    big, small = {}, {}
    df, dgate2 = gate_bwd(dh, sv["f"], mods["gate2"], tag + "gate2_bwd")
    dgp, du = mm_nt_swiglu_bwd(df, lw["w_down"], sv["g"], sv["u"], tag + "w_down_dgrad")
    big["w_down"] = mm_tn(sv["act"], df, tag + "w_down_wgrad", col_sharded=False)
    dhn = mm_nt_cols(dgp, lw["w_gate"], tag + "w_gate_dgrad")
    dhn = mm_nt_cols(du, lw["w_up"], tag + "w_up_dgrad", add=dhn)
    big["w_gate"] = mm_tn(sv["hn"], dgp, tag + "w_gate_wgrad", col_sharded=True)
    big["w_up"] = mm_tn(sv["hn"], du, tag + "w_up_wgrad", col_sharded=True)
    dh1, dshift2, dscale2, small["norm2_g"] = norm_mod_bwd(dhn, sv["h1"], lw["norm2_g"], mods["scale2"], dh,
                                                           tag + "norm2_bwd")
    return dh1, big, small, (dshift2, dscale2, dgate2)


def _layer_bwd_mixer(dh1, sv, lw, mods, tabs, ctx_len, windowed, tag, between=None):
    cos, sin = tabs
    big, small = {}, {}
    dy, dgate1 = gate_bwd(dh1, sv["y"], mods["gate1"], tag + "gate1_bwd")
    dcat = mm_nt_rows(dy, lw["w_out"], tag + "w_out_dgrad")
    big["w_out"] = mm_tn(sv["cat"], dy, tag + "w_out_wgrad", col_sharded=False)
    if windowed:
        daux, small["pool_w"], small["pool_scale"] = pool_bwd(dcat, sv["proj"], lw["pool_w"], lw["pool_scale"],
                                                              ctx_len, tag + "pool_bwd")
    else:
        daux, small["conv_w"] = conv_bwd(dcat, sv["proj"], lw["conv_w"], ctx_len, tag + "conv_bwd")
    dq, dk, dv, dsink = attention_bwd(sv["q"], sv["k"], sv["v"], sv["cat"], sv["lse"], dcat, lw["sink"], ctx_len,
                                      windowed, tag + "attn_bwd")
    if windowed:
        small["sink"] = dsink[:, :GROUPS, 0].reshape(Q_HEADS)
    q_g = lw["q_g"] if between is None else lw["q_g"] + between(dq)[0, 0]
    dproj, small["q_g"], small["k_g"] = qk_prep_bwd(dq, dk, dv, daux, sv["proj"], q_g, lw["k_g"], cos, sin,
                                                    tag + "qk_bwd")
    dxn = mm_nt_cols(dproj, lw["w_in"], tag + "w_in_dgrad")
    big["w_in"] = mm_tn(sv["xn"], dproj, tag + "w_in_wgrad", col_sharded=True)
    dh0, dshift1, dscale1, small["norm1_g"] = norm_mod_bwd(dxn, sv["h"], lw["norm1_g"], mods["scale1"], dh1,
                                                           tag + "norm1_bwd")
    return dh0, big, small, (dshift1, dscale1, dgate1)


def _dmod_rows(mixer_part, ffn_part):
    return jnp.concatenate([m[:, 0, :] for m in (*mixer_part, *ffn_part)], axis=1)


def _tie(mods, name, token):
    return {**mods, name: mods[name] + token[0, 0]}


def _place():
    x, y, c = lax.axis_index("x"), lax.axis_index("y"), lax.axis_index("c")
    chips = [(1 - x, y), (x, 1 - y), (1 - x, 1 - y)]
    return x, y, c, chips


def _remote(src, dst, send_sem, recv_sem, to):
    return pltpu.make_async_remote_copy(src_ref=src, dst_ref=dst, send_sem=send_sem, recv_sem=recv_sem,
                                        device_id=to, device_id_type=MESH)


ANY = pl.BlockSpec(memory_space=pl.ANY)


def all_gather8(x_shard, name):
    m_per, n = x_shard.shape

    def body(x_ref, out_ref, send_sems, recv_sems, local_sem):
        x, y, c, chips = _place()
        me, sibling = (x, y, c), (x, y, 1 - c)

        def rows(px, py, pc):
            return out_ref.at[pl.ds((4 * px + 2 * py + pc) * m_per, m_per), :]

        def copy(k, block, to, src=None):
            return _remote(rows(*block) if src is None else src, rows(*block), send_sems.at[k], recv_sems.at[k], to)

        mine = pltpu.make_async_copy(x_ref, rows(*me), local_sem)
        mine.start()
        first = [copy(0, me, sibling, src=x_ref)]
        first += [copy(1 + j, me, (*chip, c), src=x_ref) for j, chip in enumerate(chips)]
        for cp in first:
            cp.start()
        passed = [copy(4 + j, (*chip, c), sibling) for j, chip in enumerate(chips)]
        for j, chip in enumerate(chips):
            copy(1 + j, (*chip, c), me).wait_recv()
            passed[j].start()
        copy(0, sibling, me).wait_recv()
        for j, chip in enumerate(chips):
            copy(4 + j, (*chip, 1 - c), me).wait_recv()
        for cp in first + passed:
            cp.wait_send()
        mine.wait()

    return _pcall(
        body, name=name,
        out_shape=jax.ShapeDtypeStruct((N_DEV * m_per, n), x_shard.dtype),
        in_specs=[pl.BlockSpec(memory_space=pltpu.VMEM)],
        out_specs=pl.BlockSpec(memory_space=pltpu.VMEM),
        scratch_shapes=[pltpu.SemaphoreType.DMA((7,)), pltpu.SemaphoreType.DMA((7,)), pltpu.SemaphoreType.DMA],
        compiler_params=pltpu.CompilerParams(vmem_limit_bytes=VMEM_LIMIT),
    )(x_shard)


def _row_tile(rows):
    for tr in (256, 128, 64, 32, 16):
        if rows % tr == 0:
            return tr
    return rows


def place_shard(w, p, name):
    r, cols = w.shape
    tr = _row_tile(r)

    def body(p_ref, w_ref, o_ref):
        del p_ref
        o_ref[...] = w_ref[...].astype(BF16)

    return _pcall(
        body, name=name,
        grid_spec=pltpu.PrefetchScalarGridSpec(
            num_scalar_prefetch=1, grid=(r // tr,),
            in_specs=[pl.BlockSpec((tr, cols), lambda i, p_ref: (i, 0))],
            out_specs=pl.BlockSpec((None, tr, cols), lambda i, p_ref: (p_ref[0], i, 0))),
        out_shape=jax.ShapeDtypeStruct((N_SHARD, r, cols), BF16),
        compiler_params=_params("parallel"),
    )(p.reshape(1).astype(jnp.int32), w)


def gather_weight_shards(stacks):
    n = len(stacks)

    def body(*refs):
        outs = refs[n:2 * n]
        send_sems, recv_sems = refs[2 * n:]
        x, y, c, chips = _place()
        p = 2 * x + y
        sibling = (x, y, 1 - c)
        started = []
        for i in range(n):
            hr = outs[i].shape[1] // 2
            mine = outs[i].at[p, pl.ds(c * hr, hr)]
            for j, chip in enumerate(chips):
                cp = _remote(mine, mine, send_sems.at[i, j], recv_sems.at[i, j], (*chip, c))
                cp.start()
                started.append(cp)
        for i in range(n):
            hr = outs[i].shape[1] // 2
            for j, (px, py) in enumerate(chips):
                half = outs[i].at[2 * px + py, pl.ds(c * hr, hr)]
                _remote(half, half, send_sems.at[i, j], recv_sems.at[i, j], (px, py, c)).wait_recv()
                fwd = _remote(half, half, send_sems.at[i, 3 + j], recv_sems.at[i, 3 + j], sibling)
                fwd.start()
                started.append(fwd)
        for i in range(n):
            hr = outs[i].shape[1] // 2
            for j, (px, py) in enumerate(chips):
                other = outs[i].at[2 * px + py, pl.ds((1 - c) * hr, hr)]
                _remote(other, other, send_sems.at[i, 3 + j], recv_sems.at[i, 3 + j], sibling).wait_recv()
        for cp in started:
            cp.wait_send()

    return _pcall(
        body, name="gather_weight_shards",
        out_shape=[jax.ShapeDtypeStruct(s.shape, s.dtype) for s in stacks],
        in_specs=[ANY] * n, out_specs=[ANY] * n,
        input_output_aliases={i: i for i in range(n)},
        scratch_shapes=[pltpu.SemaphoreType.DMA((n, 6)), pltpu.SemaphoreType.DMA((n, 6))],
    )(*stacks)


HBM_SPEC = pl.BlockSpec(memory_space=pltpu.HBM)
SEM_SPEC = pl.BlockSpec(memory_space=pltpu.SEMAPHORE)
EFFECT = pltpu.SideEffectType.DATAFLOW_SIDE_EFFECTING


def split_start(name, copies, n_sems, arrays, after):
    m = len(arrays)

    def body(*refs):
        for cp in copies(refs[:m], refs[m + 1], refs[m + 2]):
            cp.start()
        refs[-1][...] = jnp.zeros_like(refs[-1])

    res = _pcall(
        body, name=name,
        out_shape=(pltpu.SemaphoreType.DMA((n_sems,)), pltpu.SemaphoreType.DMA((n_sems,)),
                   *[pltpu.HBM(s.shape, s.dtype) for s in arrays], jax.ShapeDtypeStruct((8, 128), F32)),
        in_specs=[HBM_SPEC] * m + [ANY],
        out_specs=(SEM_SPEC, SEM_SPEC, *[HBM_SPEC] * m, pl.BlockSpec(memory_space=pltpu.VMEM)),
        input_output_aliases={i: 2 + i for i in range(m)},
        compiler_params=pltpu.CompilerParams(has_side_effects=EFFECT),
    )(*[pltpu.with_memory_space_constraint(s, pltpu.HBM) for s in arrays], after)
    return res[0], res[1], list(res[2:2 + m]), res[2 + m]


def split_wait(name, copies, send_sems, recv_sems, arrays, after):
    m = len(arrays)

    def body(*refs):
        for cp in copies(refs[:m], refs[m], refs[m + 1]):
            cp.wait_send()
            cp.wait_recv()

    return _pcall(
        body, name=name,
        out_shape=[pltpu.HBM(s.shape, s.dtype) for s in arrays],
        in_specs=[HBM_SPEC] * m + [SEM_SPEC, SEM_SPEC, ANY],
        out_specs=[HBM_SPEC] * m,
        input_output_aliases={i: i for i in range(m)},
        compiler_params=pltpu.CompilerParams(has_side_effects=EFFECT),
    )(*arrays, send_sems, recv_sems, after)


def gather_copies(n):
    def copies(stacks, send_sems, recv_sems):
        x, y, c, chips = _place()
        p = 2 * x + y
        return [_remote(stacks[i].at[p], stacks[i].at[p], send_sems.at[3 * i + j], recv_sems.at[3 * i + j], (*chip, c))
                for i in range(n) for j, chip in enumerate(chips)]
    return copies


def sibling_half_copies(n):
    def copies(refs, send_sems, recv_sems):
        x, y, c, _ = _place()
        out = []
        for i in range(n):
            hr = refs[n + i].shape[1]
            out.append(_remote(refs[i].at[:, pl.ds((1 - c) * hr, hr), :], refs[n + i], send_sems.at[i], recv_sems.at[i],
                               (x, y, 1 - c)))
        return out
    return copies


def chip_quarter_copies(n):
    def copies(refs, send_sems, recv_sems):
        _, _, c, chips = _place()
        return [_remote(refs[i].at[2 * px + py], refs[n + i].at[j], send_sems.at[3 * i + j], recv_sems.at[3 * i + j],
                        (px, py, c)) for i in range(n) for j, (px, py) in enumerate(chips)]
    return copies


def exchange_sibling_halves(grads):
    n = len(grads)

    def body(*refs):
        ins, outs = refs[:n], refs[n:2 * n]
        send_sems, recv_sems = refs[2 * n:]
        x, y, c, _ = _place()
        copies = []
        for i in range(n):
            hr = ins[i].shape[1] // 2
            cp = _remote(ins[i].at[:, pl.ds((1 - c) * hr, hr), :], outs[i], send_sems.at[i], recv_sems.at[i],
                         (x, y, 1 - c))
            cp.start()
            copies.append(cp)
        for cp in copies:
            cp.wait()

    return _pcall(
        body, name="exchange_sibling_halves",
        out_shape=[jax.ShapeDtypeStruct((g.shape[0], g.shape[1] // 2, g.shape[2]), g.dtype) for g in grads],
        in_specs=[ANY] * n, out_specs=[ANY] * n,
        scratch_shapes=[pltpu.SemaphoreType.DMA((n,)), pltpu.SemaphoreType.DMA((n,))],
    )(*grads)


def exchange_chip_quarters(parts):
    n = len(parts)

    def body(*refs):
        ins, outs = refs[:n], refs[n:2 * n]
        send_sems, recv_sems = refs[2 * n:]
        _, _, c, chips = _place()
        sends = []
        for i in range(n):
            for j, (px, py) in enumerate(chips):
                cp = _remote(ins[i].at[2 * px + py], outs[i].at[j], send_sems.at[i, j], recv_sems.at[i, j], (px, py, c))
                cp.start()
                sends.append(cp)
        for i in range(n):
            for j, (px, py) in enumerate(chips):
                slot = outs[i].at[j]
                _remote(slot, slot, send_sems.at[i, j], recv_sems.at[i, j], (px, py, c)).wait_recv()
        for cp in sends:
            cp.wait_send()

    return _pcall(
        body, name="exchange_chip_quarters",
        out_shape=[jax.ShapeDtypeStruct((3,) + g.shape[1:], g.dtype) for g in parts],
        in_specs=[ANY] * n, out_specs=[ANY] * n,
        scratch_shapes=[pltpu.SemaphoreType.DMA((n, 3)), pltpu.SemaphoreType.DMA((n, 3))],
    )(*parts)


def exchange_final_halves(grads):
    n = len(grads)

    def body(*refs):
        outs = refs[n:2 * n]
        send_sems, recv_sems = refs[2 * n:]
        x, y, c, _ = _place()
        sends = []
        for i in range(n):
            hr = outs[i].shape[0] // 2
            mine = outs[i].at[pl.ds(c * hr, hr)]
            cp = _remote(mine, mine, send_sems.at[i], recv_sems.at[i], (x, y, 1 - c))
            cp.start()
            sends.append(cp)
        for i in range(n):
            hr = outs[i].shape[0] // 2
            other = outs[i].at[pl.ds((1 - c) * hr, hr)]
            _remote(other, other, send_sems.at[i], recv_sems.at[i], (x, y, 1 - c)).wait_recv()
        for cp in sends:
            cp.wait_send()

    return _pcall(
        body, name="exchange_final_halves",
        out_shape=[jax.ShapeDtypeStruct(g.shape, g.dtype) for g in grads],
        in_specs=[ANY] * n, out_specs=[ANY] * n,
        input_output_aliases={i: i for i in range(n)},
        scratch_shapes=[pltpu.SemaphoreType.DMA((n,)), pltpu.SemaphoreType.DMA((n,))],
    )(*grads)


def add_own_half(g, recv, c, name):
    s, r, cols = g.shape
    hr = r // 2
    tr = _row_tile(hr)
    nb = hr // tr

    def body(c_ref, g_ref, r_ref, o_ref):
        del c_ref
        o_ref[...] = (g_ref[...] + r_ref[...]).astype(BF16)

    return _pcall(
        body, name=name,
        grid_spec=pltpu.PrefetchScalarGridSpec(
            num_scalar_prefetch=1, grid=(s, nb),
            in_specs=[pl.BlockSpec((None, tr, cols), lambda j, i, c_ref: (j, c_ref[0] * nb + i, 0)),
                      pl.BlockSpec((None, tr, cols), lambda j, i, c_ref: (j, i, 0))],
            out_specs=pl.BlockSpec((None, tr, cols), lambda j, i, c_ref: (j, i, 0))),
        out_shape=jax.ShapeDtypeStruct((s, hr, cols), BF16),
        compiler_params=_params("parallel", "parallel"),
    )(c.reshape(1).astype(jnp.int32), g, recv)


def add_quarters(own, others, p, c, name):
    _, hr, cols = own.shape
    tr = _row_tile(hr)
    nb = hr // tr

    def body(pc_ref, own_ref, oth_ref, o_ref):
        del pc_ref
        acc = own_ref[...].astype(F32) + oth_ref[0].astype(F32)
        acc = acc + oth_ref[1].astype(F32)
        o_ref[...] = acc + oth_ref[2].astype(F32)

    return _pcall(
        body, name=name,
        grid_spec=pltpu.PrefetchScalarGridSpec(
            num_scalar_prefetch=1, grid=(nb,),
            in_specs=[pl.BlockSpec((None, tr, cols), lambda i, pc: (pc[0], i, 0)),
                      pl.BlockSpec((3, tr, cols), lambda i, pc: (0, i, 0))],
            out_specs=pl.BlockSpec((tr, cols), lambda i, pc: (pc[1] * nb + i, 0))),
        out_shape=jax.ShapeDtypeStruct((2 * hr, cols), F32),
        compiler_params=_params("parallel"),
    )(jnp.stack([p, c]).astype(jnp.int32), own, others)


def _adamw_math(w, g, m, v):
    m2 = ADAM_B1 * m + (1.0 - ADAM_B1) * g
    v2 = ADAM_B2 * v + (1.0 - ADAM_B2) * (g * g)
    m_hat = m2 / (1.0 - ADAM_B1 ** ADAM_STEP)
    v_hat = v2 / (1.0 - ADAM_B2 ** ADAM_STEP)
    delta = -ADAM_LR * (m_hat / (jnp.sqrt(v_hat) + ADAM_EPS) + ADAM_WD * w)
    return delta, m2, v2


def adamw(w, g, m, v, name):
    r, cols = w.shape
    tr = _row_tile(r)

    def body(w_ref, g_ref, m_ref, v_ref, d_ref, m2_ref, v2_ref):
        d_ref[...], m2_ref[...], v2_ref[...] = _adamw_math(w_ref[...], g_ref[...], m_ref[...], v_ref[...])

    blk = pl.BlockSpec((tr, cols), lambda i: (i, 0))
    return _pcall(
        body, name=name, grid=(r // tr,),
        in_specs=[blk] * 4, out_specs=[blk] * 3,
        out_shape=[jax.ShapeDtypeStruct((r, cols), F32)] * 3,
        compiler_params=_params("parallel"),
    )(w, g, m, v)


def small_update(gathered, w, m, v):
    nd, r, lanes = gathered.shape

    def body(ga_ref, w_ref, m_ref, v_ref, g_ref, d_ref, m2_ref, v2_ref):
        g = ga_ref[0] + ga_ref[1]
        for dev in range(2, nd):
            g = g + ga_ref[dev]
        g_ref[...] = g
        d_ref[...], m2_ref[...], v2_ref[...] = _adamw_math(w_ref[...], g, m_ref[...], v_ref[...])

    return _pcall(
        body, name="small_update",
        out_shape=[jax.ShapeDtypeStruct((r, lanes), F32)] * 4,
        compiler_params=pltpu.CompilerParams(vmem_limit_bytes=VMEM_LIMIT),
    )(gathered, w, m, v)


def mod_fwd(c16, w_mod, b_mod, name):
    d, ns = w_mod.shape
    tn = 512

    def body(c_ref, w_ref, b_ref, o_ref):
        cv = c_ref[...]
        sc = (cv * _sigmoid(cv)).astype(BF16)
        o_ref[...] = _dot(sc, w_ref[...].astype(BF16)) + b_ref[...]

    return _pcall(
        body, name=name, grid=(ns // tn,),
        in_specs=[pl.BlockSpec((16, d), lambda j: (0, 0)), pl.BlockSpec((d, tn), lambda j: (0, j)),
                  pl.BlockSpec((1, tn), lambda j: (0, j))],
        out_specs=pl.BlockSpec((16, tn), lambda j: (0, j)),
        out_shape=jax.ShapeDtypeStruct((16, ns), F32),
        compiler_params=_params("parallel"),
    )(c16, w_mod, b_mod)


def wmod_update(c16, dmod16, w, m, v, name):
    d, ns = w.shape
    tn = 256

    def body(c_ref, dm_ref, w_ref, m_ref, v_ref, g_ref, d_ref, m2_ref, v2_ref):
        cv = c_ref[...]
        sc = (cv * _sigmoid(cv)).astype(BF16)
        g = _dot_tn(sc, dm_ref[...].astype(BF16))
        g_ref[...] = g
        d_ref[...], m2_ref[...], v2_ref[...] = _adamw_math(w_ref[...], g, m_ref[...], v_ref[...])

    blk = pl.BlockSpec((d, tn), lambda j: (0, j))
    return _pcall(
        body, name=name, grid=(ns // tn,),
        in_specs=[pl.BlockSpec((16, d), lambda j: (0, 0)), pl.BlockSpec((16, tn), lambda j: (0, j)), blk, blk, blk],
        out_specs=[blk] * 4,
        out_shape=[jax.ShapeDtypeStruct((d, ns), F32)] * 4,
        compiler_params=_params("parallel"),
    )(c16, dmod16, w, m, v)


def cctx_partial(dm0, w0, dm1, w1):
    d, ns = w0.shape
    tn = 512

    def body(dm0_ref, w0_ref, dm1_ref, w1_ref, o_ref):
        @pl.when(pl.program_id(0) == 0)
        def _():
            o_ref[...] = jnp.zeros_like(o_ref)

        for dm_ref, w_ref in ((dm0_ref, w0_ref), (dm1_ref, w1_ref)):
            tot = jnp.sum(dm_ref[...], axis=0, keepdims=True)
            lhs = jnp.broadcast_to(tot, (8, tn)).astype(BF16)
            o_ref[...] += _dot_nt(lhs, w_ref[...].astype(BF16))

    dspec = pl.BlockSpec((8, tn), lambda j: (0, j))
    wspec = pl.BlockSpec((d, tn), lambda j: (0, j))
    return _pcall(
        body, name="cctx_partial", grid=(ns // tn,),
        in_specs=[dspec, wspec, dspec, wspec],
        out_specs=pl.BlockSpec((8, d), lambda j: (0, 0)),
        out_shape=jax.ShapeDtypeStruct((8, d), F32),
        compiler_params=_params("arbitrary"),
    )(dm0, w0, dm1, w1)


def cctx_update(parts, c_ctx, m, v):
    d = c_ctx.shape[1]

    def body(p_ref, c_ref, m_ref, v_ref, g_ref, d_ref, m2_ref, v2_ref):
        tot = p_ref[0:1, :] + p_ref[1:2, :]
        tot = tot + p_ref[2:3, :]
        tot = tot + p_ref[3:4, :]
        cv = c_ref[...]
        sg = _sigmoid(cv)
        g = tot * (sg * (1.0 + cv * (1.0 - sg)))
        g_ref[...] = g
        d_ref[...], m2_ref[...], v2_ref[...] = _adamw_math(cv, g, m_ref[...], v_ref[...])

    return _pcall(
        body, name="cctx_update",
        out_shape=[jax.ShapeDtypeStruct((1, d), F32)] * 4,
    )(parts, c_ctx, m, v)


WEIGHT_NAMES = (
    "c_ctx", "l0_norm1_g", "l0_w_mod", "l0_b_mod", "l0_w_in", "l0_q_norm_g", "l0_k_norm_g", "l0_conv_w", "l0_w_out",
    "l0_norm2_g", "l0_w_gate", "l0_w_up", "l0_w_down", "l1_norm1_g", "l1_w_mod", "l1_b_mod", "l1_w_in",
    "l1_q_norm_g", "l1_k_norm_g", "l1_sink", "l1_pool_w", "l1_pool_scale", "l1_w_out", "l1_norm2_g", "l1_w_gate",
    "l1_w_up", "l1_w_down", "final_norm_g")
BIG_NAMES = ("w_in", "w_out", "w_gate", "w_up", "w_down")
SMALL_SLOTS = tuple(
    [(f"l{l}_{nm}", par and f"l{l}_{nm}") for l in (0, 1)
     for nm, par in (("dmod_lat", False), ("dmod_ctx", False), ("b_mod", True), ("norm1_g", True), ("norm2_g", True),
                     ("q_norm_g", True), ("k_norm_g", True))]
    + [("l0_conv_w", None), ("l1_sink", "l1_sink"), ("l1_pool_w", "l1_pool_w"), ("l1_pool_scale", "l1_pool_scale"),
       ("final_norm_g", "final_norm_g")])


def _pack(values, sizes):
    parts = []
    for (name, _), size in zip(SMALL_SLOTS, sizes):
        v = values.get(name)
        padded = -(-size // 128) * 128
        v = jnp.zeros((padded,), F32) if v is None else jnp.pad(v.reshape(-1).astype(F32), (0, padded - size))
        parts.append(v)
    total = sum(p.shape[0] for p in parts)
    parts.append(jnp.zeros((-(-total // 1024) * 1024 - total,), F32))
    return jnp.concatenate(parts).reshape(-1, 128)


def _offsets(sizes):
    offs, o = {}, 0
    for (name, _), size in zip(SMALL_SLOTS, sizes):
        offs[name] = (o, size)
        o += -(-size // 128) * 128
    return offs


def kernel(x, c, ctx, c_ctx, l0_norm1_g, l0_w_mod, l0_b_mod, l0_w_in, l0_q_norm_g, l0_k_norm_g, l0_conv_w, l0_w_out, l0_norm2_g, l0_w_gate, l0_w_up, l0_w_down, l1_norm1_g, l1_w_mod, l1_b_mod, l1_w_in, l1_q_norm_g, l1_k_norm_g, l1_sink, l1_pool_w, l1_pool_scale, l1_w_out, l1_norm2_g, l1_w_gate, l1_w_up, l1_w_down, final_norm_g, loss_target, m_c_ctx, m_l0_norm1_g, m_l0_w_mod, m_l0_b_mod, m_l0_w_in, m_l0_q_norm_g, m_l0_k_norm_g, m_l0_conv_w, m_l0_w_out, m_l0_norm2_g, m_l0_w_gate, m_l0_w_up, m_l0_w_down, m_l1_norm1_g, m_l1_w_mod, m_l1_b_mod, m_l1_w_in, m_l1_q_norm_g, m_l1_k_norm_g, m_l1_sink, m_l1_pool_w, m_l1_pool_scale, m_l1_w_out, m_l1_norm2_g, m_l1_w_gate, m_l1_w_up, m_l1_w_down, m_final_norm_g, v_c_ctx, v_l0_norm1_g, v_l0_w_mod, v_l0_b_mod, v_l0_w_in, v_l0_q_norm_g, v_l0_k_norm_g, v_l0_conv_w, v_l0_w_out, v_l0_norm2_g, v_l0_w_gate, v_l0_w_up, v_l0_w_down, v_l1_norm1_g, v_l1_w_mod, v_l1_b_mod, v_l1_w_in, v_l1_q_norm_g, v_l1_k_norm_g, v_l1_sink, v_l1_pool_w, v_l1_pool_scale, v_l1_w_out, v_l1_norm2_g, v_l1_w_gate, v_l1_w_up, v_l1_w_down, v_final_norm_g):
    a = dict(locals())
    xi, yi, ci = lax.axis_index("x"), lax.axis_index("y"), lax.axis_index("c")
    p = 2 * xi + yi
    me = 4 * xi + 2 * yi + ci
    d = x.shape[-1]
    conv_cols = l0_conv_w.shape[1]

    row0 = jnp.concatenate([c, jnp.pad(l0_conv_w, ((0, 0), (0, d - conv_cols))), jnp.zeros((4, d), F32)], axis=0)
    gathered = all_gather8(row0, "gather_cond").reshape(N_DEV, 8, d)
    c_all = gathered[:, 0]
    conv_w = gathered[0::2, 1:4, :conv_cols].transpose(1, 0, 2).reshape(3, N_SHARD * conv_cols)
    c16_fwd = jnp.concatenate([c_all, c_ctx[None], jnp.zeros((7, d), F32)], axis=0)
    c16_bwd = jnp.concatenate([c_all, jnp.broadcast_to(c_ctx[None], (8, d))], axis=0)

    ns_mod = l0_w_mod.shape[1]
    mod_parts = [mod_fwd(c16_fwd, a[f"l{l}_w_mod"], lax.dynamic_slice(a[f"l{l}_b_mod"], (p * ns_mod,), (ns_mod,))[None],
                         f"l{l}_mod_fwd") for l in (0, 1)]
    modg = all_gather8(jnp.concatenate(mod_parts, axis=0), "gather_mod").reshape(N_DEV, 2, 16, ns_mod)[0::2]
    mod_full = modg.transpose(1, 2, 0, 3).reshape(2, 16, N_SHARD * ns_mod)
    mods = []
    for l in (0, 1):
        lat = lax.dynamic_index_in_dim(mod_full[l], me, axis=0, keepdims=False)
        cx = mod_full[l, 8]
        mods.append({nm: jnp.stack([cx[j * d:(j + 1) * d], lat[j * d:(j + 1) * d]])[:, None, :]
                     for j, nm in enumerate(MOD_NAMES)})

    placed = {(l, nm): place_shard(a[f"l{l}_{nm}"], p, f"l{l}_{nm}_place") for l in (0, 1) for nm in BIG_NAMES}
    ready = {(0, "w_in"): gather_weight_shards([placed[0, "w_in"]])[0]}
    groups = (((0, "w_out"), (0, "w_gate"), (0, "w_up")), ((0, "w_down"),), ((1, "w_in"), (1, "w_out")),
              ((1, "w_gate"), (1, "w_up"), (1, "w_down")))
    flights, after = [], ready[0, "w_in"]
    for gi, grp in enumerate(groups):
        flights.append(split_start(f"gather_start_{gi}", gather_copies(len(grp)), 3 * len(grp),
                                   [placed[k] for k in grp], after))
        after = flights[-1][3]
    started_token = after

    def weight_of(l):
        def weight(nm, after):
            if (l, nm) not in ready:
                gi = next(i for i, grp in enumerate(groups) if (l, nm) in grp)
                send_sems, recv_sems, in_flight, _ = flights[gi]
                landed = split_wait(f"gather_wait_{gi}", gather_copies(len(groups[gi])), send_sems, recv_sems,
                                    in_flight, after)
                ready.update(zip(groups[gi], landed))
            return ready[l, nm]
        return weight

    layers = [dict(norm1_g=a[f"l{l}_norm1_g"][None], norm2_g=a[f"l{l}_norm2_g"][None],
                   q_g=a[f"l{l}_q_norm_g"][None], k_g=a[f"l{l}_k_norm_g"][None]) for l in (0, 1)]
    layers[0].update(conv_w=conv_w, sink=jnp.zeros((Q_HEADS,), F32))
    layers[1].update(sink=l1_sink, pool_w=l1_pool_w, pool_scale=l1_pool_scale[None])

    ctx_len = ctx.shape[1]
    h = jnp.concatenate([ctx[0], x[0]], axis=0)
    tabs = _rope_tables(h.shape[0], ctx_len)
    h, saved0 = _layer_fwd(h, layers[0], weight_of(0), _tie(mods[0], "shift1", started_token), tabs, ctx_len, False,
                           "l0_")
    h, saved1 = _layer_fwd(h, layers[1], weight_of(1), mods[1], tabs, ctx_len, True, "l1_")
    loss_tile, dh, dfg = final_loss(h, final_norm_g[None], loss_target[0])

    keys = [[f"l{l}_{nm}" for nm in BIG_NAMES] for l in (0, 1)]
    n_big = len(BIG_NAMES)
    dh, ffn_big, ffn_small, ffn_mod = _layer_bwd_ffn(dh, saved1, layers[1], mods[1], "l1_")
    dh, mix_big, mix_small, mix_mod = _layer_bwd_mixer(dh, saved1, layers[1], mods[1], tabs, ctx_len, True, "l1_")
    bigs, smalls, dmods = [None, {**ffn_big, **mix_big}], [None, {**ffn_small, **mix_small}], [None, None]
    dmods[1] = _dmod_rows(mix_mod, ffn_mod)
    grads1 = [bigs[1][nm] for nm in BIG_NAMES]
    lands = [lax.empty((g.shape[0], g.shape[1] // 2, g.shape[2]), g.dtype) for g in grads1]
    send_a, recv_a, fly_a, token_a = split_start("l1_sibling_halves_start", sibling_half_copies(n_big), n_big,
                                                 grads1 + lands, dh)
    dh, ffn_big, ffn_small, ffn_mod = _layer_bwd_ffn(dh, saved0, layers[0], _tie(mods[0], "gate2", token_a), "l0_")
    fly_a = split_wait("l1_sibling_halves_wait", sibling_half_copies(n_big), send_a, recv_a, fly_a, dh)
    chip_sums1 = [add_own_half(g, r, ci, k + "_add_halves")
                  for k, g, r in zip(keys[1], fly_a[:n_big], fly_a[n_big:])]
    lands = [lax.empty((3,) + s.shape[1:], s.dtype) for s in chip_sums1]
    send_b, recv_b, fly_b, token_b = split_start("l1_chip_quarters_start", chip_quarter_copies(n_big), 3 * n_big,
                                                 chip_sums1 + lands, dh)
    ffn_names = ("w_gate", "w_up", "w_down")
    n_ffn = len(ffn_names)
    ffn_keys = ["l0_" + nm for nm in ffn_names]
    grads0f = [ffn_big[nm] for nm in ffn_names]
    lands = [lax.empty((g.shape[0], g.shape[1] // 2, g.shape[2]), g.dtype) for g in grads0f]
    send_c, recv_c, fly_c, token_c = split_start("l0_ffn_sibling_halves_start", sibling_half_copies(n_ffn), n_ffn,
                                                 grads0f + lands, token_b)
    ffn_flight = []

    def between(dq):
        landed = split_wait("l0_ffn_sibling_halves_wait", sibling_half_copies(n_ffn), send_c, recv_c, fly_c, dq)
        sums = [add_own_half(g, r, ci, k + "_add_halves") for k, g, r in zip(ffn_keys, landed[:n_ffn], landed[n_ffn:])]
        slots = [lax.empty((3,) + s.shape[1:], s.dtype) for s in sums]
        ffn_flight.extend(split_start("l0_ffn_chip_quarters_start", chip_quarter_copies(n_ffn), 3 * n_ffn,
                                      sums + slots, dq))
        return ffn_flight[3]

    dh, mix_big, mix_small, mix_mod = _layer_bwd_mixer(dh, saved0, layers[0], _tie(mods[0], "gate1", token_c), tabs,
                                                       ctx_len, False, "l0_", between)
    fly_b = split_wait("l1_chip_quarters_wait", chip_quarter_copies(n_big), send_b, recv_b, fly_b, dh)
    fly_d = split_wait("l0_ffn_chip_quarters_wait", chip_quarter_copies(n_ffn), ffn_flight[0], ffn_flight[1],
                       ffn_flight[2], dh)
    bigs[0], smalls[0], dmods[0] = {**ffn_big, **mix_big}, {**ffn_small, **mix_small}, _dmod_rows(mix_mod, ffn_mod)
    dx = dh[ctx_len:]
    chip_sums = dict(zip(keys[1] + ffn_keys, list(fly_b[:n_big]) + list(fly_d[:n_ffn])))
    quarters = dict(zip(keys[1] + ffn_keys, list(fly_b[n_big:]) + list(fly_d[n_ffn:])))

    partial = {"l0_conv_w": smalls[0]["conv_w"], "l1_sink": smalls[1]["sink"], "l1_pool_w": smalls[1]["pool_w"],
               "l1_pool_scale": smalls[1]["pool_scale"], "final_norm_g": dfg}
    for l in (0, 1):
        partial.update({f"l{l}_dmod_lat": dmods[l][1], f"l{l}_dmod_ctx": dmods[l][0],
                        f"l{l}_b_mod": dmods[l][0] + dmods[l][1], f"l{l}_norm1_g": smalls[l]["norm1_g"],
                        f"l{l}_norm2_g": smalls[l]["norm2_g"], f"l{l}_q_norm_g": smalls[l]["q_g"],
                        f"l{l}_k_norm_g": smalls[l]["k_g"]})
    sizes = [int(np.prod(partial[name].shape)) for name, _ in SMALL_SLOTS]
    offs = _offsets(sizes)
    gpack = _pack(partial, sizes)
    rows = gpack.shape[0]
    small_all = all_gather8(gpack, "gather_small").reshape(N_DEV, rows, 128)
    packs = [_pack({name: a[pre + par] for name, par in SMALL_SLOTS if par}, sizes) for pre in ("", "m_", "v_")]
    small_out = [t.reshape(-1) for t in small_update(small_all, *packs)]
    flat_all = small_all.reshape(N_DEV, rows * 128)

    def slot(flat, name, shape):
        o, size = offs[name]
        return flat[o:o + size].reshape(shape)

    results = {}
    for name, par in SMALL_SLOTS:
        if par:
            results[par] = tuple(slot(t, name, a[par].shape) for t in small_out)

    ctx_rows = []
    for l in (0, 1):
        o_lat, o_ctx = offs[f"l{l}_dmod_lat"][0], offs[f"l{l}_dmod_ctx"][0]
        lat = lax.dynamic_slice(flat_all, (0, o_lat + p * ns_mod), (N_DEV, ns_mod))
        cxr = lax.dynamic_slice(flat_all, (0, o_ctx + p * ns_mod), (N_DEV, ns_mod))
        ctx_rows.append(cxr)
        results[f"l{l}_w_mod"] = tuple(wmod_update(c16_bwd, jnp.concatenate([lat, cxr], axis=0), a[f"l{l}_w_mod"],
                                                   a[f"m_l{l}_w_mod"], a[f"v_l{l}_w_mod"], f"l{l}_w_mod_update"))
    part = cctx_partial(ctx_rows[0], l0_w_mod, ctx_rows[1], l1_w_mod)
    parts4 = all_gather8(part, "gather_cctx").reshape(N_DEV, 8, d)[0::2, 0]
    results["c_ctx"] = tuple(t[0] for t in cctx_update(parts4, c_ctx[None], m_c_ctx[None], v_c_ctx[None]))

    gconv = lax.dynamic_slice(slot(small_out[0], "l0_conv_w", (3, N_SHARD * conv_cols)), (0, p * conv_cols),
                              (3, conv_cols))
    results["l0_conv_w"] = (gconv,) + tuple(adamw(l0_conv_w, gconv, m_l0_conv_w, v_l0_conv_w, "l0_conv_w_adamw"))

    last_keys = ["l0_w_in", "l0_w_out"]
    grads_last = [bigs[0]["w_in"], bigs[0]["w_out"]]
    recv_last = exchange_sibling_halves(grads_last)
    sums_last = [add_own_half(g, r, ci, k + "_add_halves") for k, g, r in zip(last_keys, grads_last, recv_last)]
    chip_sums.update(zip(last_keys, sums_last))
    quarters.update(zip(last_keys, exchange_chip_quarters(sums_last)))
    all_keys = keys[0] + keys[1]
    halves = [add_quarters(chip_sums[k], quarters[k], p, ci, k + "_add_quarters") for k in all_keys]
    fulls = exchange_final_halves(halves)
    for k, g in zip(all_keys, fulls):
        results[k] = (g,) + tuple(adamw(a[k], g, a["m_" + k], a["v_" + k], k + "_adamw"))

    loss = lax.psum(loss_tile[0, 0], ("x", "y", "c"))
    out = [loss, dx[None]]
    for j in range(4):
        out += [results[k][j] for k in WEIGHT_NAMES]
    return tuple(out)
```

```python
import numpy as np
import jax
import jax.numpy as jnp
from jax import lax
from jax.experimental import pallas as pl
from jax.experimental.pallas import tpu as pltpu

F32 = jnp.float32
BF16 = jnp.bfloat16
MESH = pl.DeviceIdType.MESH

HEAD_DIM = 128
Q_HEADS = 12
KV_HEADS = 4
GROUPS = Q_HEADS // KV_HEADS
ATT_WIDTH = Q_HEADS * HEAD_DIM
KV_WIDTH = KV_HEADS * HEAD_DIM
QKV_WIDTH = ATT_WIDTH + 2 * KV_WIDTH
AUX_WIDTH = 512
AUX_GROUPS = 4
POOL_WINDOWS = (2, 4, 8, 16)
WINDOW = 128
GRID_W = 64
ROPE_THETA = 10000.0
N_MOD = 6
EPS = 1e-6
NEG_INF = -1e30
ATT_SCALE = HEAD_DIM ** -0.5
N_SHARD = 4
N_DEV = 8

ADAM_LR = 0.001
ADAM_B1 = 0.9
ADAM_B2 = 0.999
ADAM_EPS = 1e-08
ADAM_WD = 0.01
ADAM_STEP = 10

TM = 256
BAND = TM + 2 * WINDOW
VMEM_LIMIT = 56 * 1024 * 1024

NT_DIMS = (((1,), (1,)), ((), ()))
TN_DIMS = (((0,), (0,)), ((), ()))


def _pcall(body, **kw):
    return pl.pallas_call(body, **kw)


def _params(*sem):
    return pltpu.CompilerParams(dimension_semantics=sem, vmem_limit_bytes=VMEM_LIMIT)


def _sel(i):
    return jnp.minimum(i, 1)


def _sel_spec(d):
    return pl.BlockSpec((None, 1, d), lambda i: (_sel(i), 0, 0))


def _dot(a, b):
    return jnp.dot(a, b, preferred_element_type=F32)


def _dot_nt(a, b):
    return lax.dot_general(a, b, NT_DIMS, preferred_element_type=F32)


def _dot_tn(a, b):
    return lax.dot_general(a, b, TN_DIMS, preferred_element_type=F32)


def _sigmoid(x):
    return 1.0 / (1.0 + jnp.exp(-x))


def norm_mod_fwd(h, g, shift, scale, name):
    t, d = h.shape

    def body(h_ref, g_ref, sh_ref, sc_ref, o_ref):
        x = h_ref[...]
        r = lax.rsqrt(jnp.mean(x * x, axis=-1, keepdims=True) + EPS)
        y = x * r * g_ref[...]
        o_ref[...] = (y * (1.0 + sc_ref[...]) + sh_ref[...]).astype(BF16)

    return _pcall(
        body, name=name, grid=(t // TM,),
        in_specs=[pl.BlockSpec((TM, d), lambda i: (i, 0)), pl.BlockSpec((1, d), lambda i: (0, 0)),
                  _sel_spec(d), _sel_spec(d)],
        out_specs=pl.BlockSpec((TM, d), lambda i: (i, 0)),
        out_shape=jax.ShapeDtypeStruct((t, d), BF16),
        compiler_params=_params("parallel"),
    )(h, g, shift, scale)


def norm_mod_bwd(dy, h, g, scale, dres, name):
    t, d = h.shape

    def body(dy_ref, h_ref, g_ref, sc_ref, dres_ref, dh_ref, dsh_ref, dsc_ref, dg_ref):
        i = pl.program_id(0)

        @pl.when(i == 0)
        def _():
            dsh_ref[...] = jnp.zeros_like(dsh_ref)
            dsc_ref[...] = jnp.zeros_like(dsc_ref)
            dg_ref[...] = jnp.zeros_like(dg_ref)

        x = h_ref[...]
        gv = g_ref[...]
        r = lax.rsqrt(jnp.mean(x * x, axis=-1, keepdims=True) + EPS)
        xhat = x * r
        dyv = dy_ref[...]
        s = _sel(i)
        dsh_ref[pl.ds(s, 1)] += jnp.sum(dyv, axis=0, keepdims=True)[None]
        dsc_ref[pl.ds(s, 1)] += jnp.sum(dyv * xhat * gv, axis=0, keepdims=True)[None]
        tt = dyv * (1.0 + sc_ref[...])
        dg_ref[...] += jnp.sum(tt * xhat, axis=0, keepdims=True)
        dxhat = tt * gv
        dx = r * (dxhat - xhat * jnp.mean(dxhat * xhat, axis=-1, keepdims=True))
        dh_ref[...] = dx + dres_ref[...]

    row = pl.BlockSpec((TM, d), lambda i: (i, 0))
    acc2 = pl.BlockSpec((2, 1, d), lambda i: (0, 0, 0))
    return _pcall(
        body, name=name, grid=(t // TM,),
        in_specs=[row, row, pl.BlockSpec((1, d), lambda i: (0, 0)), _sel_spec(d), row],
        out_specs=[row, acc2, acc2, pl.BlockSpec((1, d), lambda i: (0, 0))],
        out_shape=[jax.ShapeDtypeStruct((t, d), F32), jax.ShapeDtypeStruct((2, 1, d), F32),
                   jax.ShapeDtypeStruct((2, 1, d), F32), jax.ShapeDtypeStruct((1, d), F32)],
        compiler_params=_params("arbitrary"),
    )(dy, h, g, scale, dres)


def gate_bwd(dh, f, gate, name):
    t, d = dh.shape

    def body(dh_ref, f_ref, gt_ref, df_ref, dgt_ref):
        i = pl.program_id(0)

        @pl.when(i == 0)
        def _():
            dgt_ref[...] = jnp.zeros_like(dgt_ref)

        dhv = dh_ref[...]
        df_ref[...] = (dhv * gt_ref[...]).astype(BF16)
        dgt_ref[pl.ds(_sel(i), 1)] += jnp.sum(dhv * f_ref[...], axis=0, keepdims=True)[None]

    row = pl.BlockSpec((TM, d), lambda i: (i, 0))
    return _pcall(
        body, name=name, grid=(t // TM,),
        in_specs=[row, row, _sel_spec(d)],
        out_specs=[row, pl.BlockSpec((2, 1, d), lambda i: (0, 0, 0))],
        out_shape=[jax.ShapeDtypeStruct((t, d), BF16), jax.ShapeDtypeStruct((2, 1, d), F32)],
        compiler_params=_params("arbitrary"),
    )(dh, f, gate)


def final_loss(h, fg, target):
    t, d = h.shape

    def body(h_ref, g_ref, tg_ref, loss_ref, dh_ref, dg_ref):
        i = pl.program_id(0)

        @pl.when(i == 0)
        def _():
            loss_ref[...] = jnp.zeros_like(loss_ref)
            dg_ref[...] = jnp.zeros_like(dg_ref)
            dh_ref[...] = jnp.zeros_like(dh_ref)

        @pl.when(i > 0)
        def _():
            x = h_ref[...]
            gv = g_ref[...]
            r = lax.rsqrt(jnp.mean(x * x, axis=-1, keepdims=True) + EPS)
            xhat = x * r
            diff = xhat * gv - tg_ref[...]
            loss_ref[...] += 0.5 * jnp.sum(jnp.mean(diff * diff, axis=-1, keepdims=True), axis=0, keepdims=True)
            dout = diff * (1.0 / d)
            dg_ref[...] += jnp.sum(dout * xhat, axis=0, keepdims=True)
            dxhat = dout * gv
            dh_ref[...] = r * (dxhat - xhat * jnp.mean(dxhat * xhat, axis=-1, keepdims=True))

    row = pl.BlockSpec((TM, d), lambda i: (i, 0))
    return _pcall(
        body, name="final_loss", grid=(t // TM,),
        in_specs=[row, pl.BlockSpec((1, d), lambda i: (0, 0)),
                  pl.BlockSpec((TM, d), lambda i: (jnp.maximum(i - 1, 0), 0))],
        out_specs=[pl.BlockSpec((8, 128), lambda i: (0, 0)), row, pl.BlockSpec((1, d), lambda i: (0, 0))],
        out_shape=[jax.ShapeDtypeStruct((8, 128), F32), jax.ShapeDtypeStruct((t, d), F32),
                   jax.ShapeDtypeStruct((1, d), F32)],
        compiler_params=_params("arbitrary"),
    )(h, fg, target)


def _w_nn_spec(w, k):
    if w.shape[1] == k:
        ns = w.shape[2]
        return w, pl.BlockSpec((None, k, ns), lambda j, i: (j, 0, 0)), w.shape[0], ns
    w2 = w.reshape(w.shape[0] * w.shape[1], w.shape[2])
    tn = 1024 if w2.shape[1] % 1024 == 0 else w2.shape[1]
    return w2, pl.BlockSpec((k, tn), lambda j, i: (0, j)), w2.shape[1] // tn, tn


def mm_nn(a, w, name, out_dtype=F32):
    t, k = a.shape
    w, wspec, nb, tn = _w_nn_spec(w, k)

    def body(a_ref, w_ref, o_ref):
        o_ref[...] = _dot(a_ref[...], w_ref[...]).astype(o_ref.dtype)

    return _pcall(
        body, name=name, grid=(nb, t // TM),
        in_specs=[pl.BlockSpec((TM, k), lambda j, i: (i, 0)), wspec],
        out_specs=pl.BlockSpec((TM, tn), lambda j, i: (i, j)),
        out_shape=jax.ShapeDtypeStruct((t, nb * tn), out_dtype),
        compiler_params=_params("parallel", "parallel"),
    )(a, w)


def mm_nn_residual(a, w, h, gate, name):
    t, k = a.shape
    w, wspec, nb, tn = _w_nn_spec(w, k)

    def body(a_ref, w_ref, h_ref, gt_ref, y_ref, o_ref):
        y = _dot(a_ref[...], w_ref[...])
        y_ref[...] = y
        o_ref[...] = h_ref[...] + gt_ref[...] * y

    out = pl.BlockSpec((TM, tn), lambda j, i: (i, j))
    return _pcall(
        body, name=name, grid=(nb, t // TM),
        in_specs=[pl.BlockSpec((TM, k), lambda j, i: (i, 0)), wspec, out,
                  pl.BlockSpec((None, 1, tn), lambda j, i: (_sel(i), 0, j))],
        out_specs=[out, out],
        out_shape=[jax.ShapeDtypeStruct((t, nb * tn), F32)] * 2,
        compiler_params=_params("parallel", "parallel"),
    )(a, w, h, gate)


def mm_swiglu(a, wg, wu, name):
    t, k = a.shape
    s, _, ns = wg.shape

    def body(a_ref, wg_ref, wu_ref, g_ref, u_ref, act_ref):
        av = a_ref[...]
        g = _dot(av, wg_ref[...])
        u = _dot(av, wu_ref[...])
        g_ref[...] = g
        u_ref[...] = u
        act_ref[...] = (g * _sigmoid(g) * u).astype(BF16)

    wspec = pl.BlockSpec((None, k, ns), lambda j, i: (j, 0, 0))
    out = pl.BlockSpec((TM, ns), lambda j, i: (i, j))
    return _pcall(
        body, name=name, grid=(s, t // TM),
        in_specs=[pl.BlockSpec((TM, k), lambda j, i: (i, 0)), wspec, wspec],
        out_specs=[out, out, out],
        out_shape=[jax.ShapeDtypeStruct((t, s * ns), F32), jax.ShapeDtypeStruct((t, s * ns), F32),
                   jax.ShapeDtypeStruct((t, s * ns), BF16)],
        compiler_params=_params("parallel", "parallel"),
    )(a, wg, wu)


def mm_nt_cols(dy, w, name, add=None):
    t, n = dy.shape
    s, k, ns = w.shape
    tk = 512

    def body(*refs):
        if add is None:
            dy_ref, w_ref, o_ref = refs
        else:
            dy_ref, w_ref, add_ref, o_ref = refs
        acc = _dot_nt(dy_ref[:, 0:ns], w_ref[0])
        for sh in range(1, s):
            acc += _dot_nt(dy_ref[:, sh * ns:(sh + 1) * ns], w_ref[sh])
        if add is not None:
            acc += add_ref[...]
        o_ref[...] = acc

    out = pl.BlockSpec((TM, tk), lambda j, i: (i, j))
    in_specs = [pl.BlockSpec((TM, n), lambda j, i: (i, 0)), pl.BlockSpec((s, tk, ns), lambda j, i: (0, j, 0))]
    args = [dy, w]
    if add is not None:
        in_specs.append(out)
        args.append(add)
    return _pcall(
        body, name=name, grid=(k // tk, t // TM),
        in_specs=in_specs, out_specs=out,
        out_shape=jax.ShapeDtypeStruct((t, k), F32),
        compiler_params=_params("parallel", "parallel"),
    )(*args)


def mm_nt_rows(dy, w, name):
    t, n = dy.shape
    s, ks, _ = w.shape

    def body(dy_ref, w_ref, o_ref):
        o_ref[...] = _dot_nt(dy_ref[...], w_ref[...])

    return _pcall(
        body, name=name, grid=(s, t // TM),
        in_specs=[pl.BlockSpec((TM, n), lambda j, i: (i, 0)), pl.BlockSpec((None, ks, n), lambda j, i: (j, 0, 0))],
        out_specs=pl.BlockSpec((TM, ks), lambda j, i: (i, j)),
        out_shape=jax.ShapeDtypeStruct((t, s * ks), F32),
        compiler_params=_params("parallel", "parallel"),
    )(dy, w)


def mm_nt_swiglu_bwd(df, wd, g, u, name):
    t, n = df.shape
    s, ks, _ = wd.shape

    def body(df_ref, w_ref, g_ref, u_ref, dg_ref, du_ref):
        da = _dot_nt(df_ref[...], w_ref[...])
        gv = g_ref[...]
        sg = _sigmoid(gv)
        silu = gv * sg
        dg_ref[...] = (da * u_ref[...] * (sg * (1.0 + gv * (1.0 - sg)))).astype(BF16)
        du_ref[...] = (da * silu).astype(BF16)

    blk = pl.BlockSpec((TM, ks), lambda j, i: (i, j))
    return _pcall(
        body, name=name, grid=(s, t // TM),
        in_specs=[pl.BlockSpec((TM, n), lambda j, i: (i, 0)), pl.BlockSpec((None, ks, n), lambda j, i: (j, 0, 0)),
                  blk, blk],
        out_specs=[blk, blk],
        out_shape=[jax.ShapeDtypeStruct((t, s * ks), BF16)] * 2,
        compiler_params=_params("parallel", "parallel"),
    )(df, wd, g, u)


def mm_tn(x, dy, name, col_sharded):
    t, k = x.shape
    n = dy.shape[1]
    tk = 512
    mc = t // 2 if (t // 2) % 128 == 0 else t
    if col_sharded:
        tn = n // N_SHARD
        out_shape = jax.ShapeDtypeStruct((N_SHARD, k, tn), F32)
        out_spec = pl.BlockSpec((None, tk, tn), lambda j, kb, m: (j, kb, 0))
    else:
        tn = 1024
        out_shape = jax.ShapeDtypeStruct((k, n), F32)
        out_spec = pl.BlockSpec((tk, tn), lambda j, kb, m: (kb, j))

    def body(x_ref, dy_ref, o_ref):
        m = pl.program_id(2)
        acc = _dot_tn(x_ref[...], dy_ref[...])

        @pl.when(m == 0)
        def _():
            o_ref[...] = acc

        @pl.when(m > 0)
        def _():
            o_ref[...] += acc

    out = _pcall(
        body, name=name, grid=(n // tn, k // tk, t // mc),
        in_specs=[pl.BlockSpec((mc, tk), lambda j, kb, m: (m, kb)), pl.BlockSpec((mc, tn), lambda j, kb, m: (m, j))],
        out_specs=out_spec, out_shape=out_shape,
        compiler_params=_params("parallel", "parallel", "arbitrary"),
    )(x, dy)
    return out if col_sharded else out.reshape(N_SHARD, k // N_SHARD, n)


def _swap32(y):
    right = pltpu.roll(y, 32, 1)
    left = pltpu.roll(y, 96, 1)
    lane = lax.broadcasted_iota(jnp.int32, y.shape, 1)
    return jnp.where((lane // 32) % 2 == 0, left, right)


def _rope_tables(t, ctx_len):
    s = t - ctx_len
    pos = np.arange(s)
    row = (pos // GRID_W).astype(np.float32)
    col = (pos % GRID_W).astype(np.float32)
    half = HEAD_DIM // 2
    inv = np.power(np.float32(ROPE_THETA), -np.arange(0, half, 2, dtype=np.float32) / np.float32(half))
    ar = row[:, None] * inv
    ac = col[:, None] * inv
    cos = np.concatenate([np.cos(ar), np.cos(ar), np.cos(ac), np.cos(ac)], axis=1)
    sin = np.concatenate([-np.sin(ar), np.sin(ar), -np.sin(ac), np.sin(ac)], axis=1)
    cos = np.concatenate([np.ones((ctx_len, HEAD_DIM), np.float32), cos.astype(np.float32)], axis=0)
    sin = np.concatenate([np.zeros((ctx_len, HEAD_DIM), np.float32), sin.astype(np.float32)], axis=0)
    return jnp.asarray(cos, F32), jnp.asarray(sin, F32)


def qk_prep_fwd(proj, qg, kg, cos, sin, name):
    t = proj.shape[0]

    def body(p_ref, qg_ref, kg_ref, c_ref, s_ref, q_ref, k_ref, v_ref):
        cv = c_ref[...]
        sv = s_ref[...]
        for hd in range(Q_HEADS + KV_HEADS):
            x = p_ref[:, hd * HEAD_DIM:(hd + 1) * HEAD_DIM]
            gv = qg_ref[...] if hd < Q_HEADS else kg_ref[...]
            y = x * lax.rsqrt(jnp.mean(x * x, axis=-1, keepdims=True) + EPS) * gv
            out = (y * cv + _swap32(y) * sv).astype(BF16)
            if hd < Q_HEADS:
                q_ref[:, hd * HEAD_DIM:(hd + 1) * HEAD_DIM] = out
            else:
                k_ref[:, (hd - Q_HEADS) * HEAD_DIM:(hd - Q_HEADS + 1) * HEAD_DIM] = out
        v_ref[...] = p_ref[:, ATT_WIDTH + KV_WIDTH:QKV_WIDTH].astype(BF16)

    vec = pl.BlockSpec((1, HEAD_DIM), lambda i: (0, 0))
    tab = pl.BlockSpec((TM, HEAD_DIM), lambda i: (i, 0))
    return _pcall(
        body, name=name, grid=(t // TM,),
        in_specs=[pl.BlockSpec((TM, QKV_WIDTH), lambda i: (i, 0)), vec, vec, tab, tab],
        out_specs=[pl.BlockSpec((TM, ATT_WIDTH), lambda i: (i, 0)), pl.BlockSpec((TM, KV_WIDTH), lambda i: (i, 0)),
                   pl.BlockSpec((TM, KV_WIDTH), lambda i: (i, 0))],
        out_shape=[jax.ShapeDtypeStruct((t, ATT_WIDTH), BF16), jax.ShapeDtypeStruct((t, KV_WIDTH), BF16),
                   jax.ShapeDtypeStruct((t, KV_WIDTH), BF16)],
        compiler_params=_params("parallel"),
    )(proj, qg, kg, cos, sin)


def qk_prep_bwd(dq, dk, dv, daux, proj, qg, kg, cos, sin, name):
    t, n_in = proj.shape
    n_aux = daux.shape[0]

    def body(dq_ref, dk_ref, dv_ref, da_ref, p_ref, qg_ref, kg_ref, c_ref, s_ref, o_ref, dqg_ref, dkg_ref):
        @pl.when(pl.program_id(0) == 0)
        def _():
            dqg_ref[...] = jnp.zeros_like(dqg_ref)
            dkg_ref[...] = jnp.zeros_like(dkg_ref)

        cv = c_ref[...]
        sv = s_ref[...]
        for hd in range(Q_HEADS + KV_HEADS):
            cols = slice(hd * HEAD_DIM, (hd + 1) * HEAD_DIM)
            x = p_ref[:, cols]
            if hd < Q_HEADS:
                gv, dyr, dg_ref = qg_ref[...], dq_ref[:, cols], dqg_ref
            else:
                kc = slice((hd - Q_HEADS) * HEAD_DIM, (hd - Q_HEADS + 1) * HEAD_DIM)
                gv, dyr, dg_ref = kg_ref[...], dk_ref[:, kc], dkg_ref
            r = lax.rsqrt(jnp.mean(x * x, axis=-1, keepdims=True) + EPS)
            xhat = x * r
            dy = dyr * cv + _swap32(dyr * sv)
            dg_ref[...] += jnp.sum(dy * xhat, axis=0, keepdims=True)
            dxhat = dy * gv
            o_ref[:, cols] = (r * (dxhat - xhat * jnp.mean(dxhat * xhat, axis=-1, keepdims=True))).astype(BF16)
        o_ref[:, ATT_WIDTH + KV_WIDTH:QKV_WIDTH] = dv_ref[...].astype(BF16)
        for a in range(n_aux):
            o_ref[:, QKV_WIDTH + a * AUX_WIDTH:QKV_WIDTH + (a + 1) * AUX_WIDTH] = da_ref[a]

    vec = pl.BlockSpec((1, HEAD_DIM), lambda i: (0, 0))
    tab = pl.BlockSpec((TM, HEAD_DIM), lambda i: (i, 0))
    return _pcall(
        body, name=name, grid=(t // TM,),
        in_specs=[pl.BlockSpec((TM, ATT_WIDTH), lambda i: (i, 0)), pl.BlockSpec((TM, KV_WIDTH), lambda i: (i, 0)),
                  pl.BlockSpec((TM, KV_WIDTH), lambda i: (i, 0)),
                  pl.BlockSpec((n_aux, TM, AUX_WIDTH), lambda i: (0, i, 0)),
                  pl.BlockSpec((TM, QKV_WIDTH), lambda i: (i, 0)), vec, vec, tab, tab],
        out_specs=[pl.BlockSpec((TM, n_in), lambda i: (i, 0)), vec, vec],
        out_shape=[jax.ShapeDtypeStruct((t, n_in), BF16), jax.ShapeDtypeStruct((1, HEAD_DIM), F32),
                   jax.ShapeDtypeStruct((1, HEAD_DIM), F32)],
        compiler_params=_params("arbitrary"),
    )(dq, dk, dv, daux, proj, qg, kg, cos, sin)


def _band_start(i, t):
    return pl.multiple_of(jnp.clip(i * TM - WINDOW, 0, t - BAND), WINDOW)


def _band_mask(i, start, ctx_len):
    shape = (TM, ctx_len + BAND)
    col = lax.broadcasted_iota(jnp.int32, shape, 1)
    qrow = i * TM + lax.broadcasted_iota(jnp.int32, shape, 0)
    krow = start + col - ctx_len
    band_ok = (krow >= ctx_len) & (jnp.abs(krow - qrow) <= WINDOW)
    return (col < ctx_len) | band_ok


def attention_fwd(q, k, v, sink, ctx_len, windowed, name):
    t = q.shape[0]
    d_model = ATT_WIDTH + AUX_WIDTH
    gw = GROUPS * HEAD_DIM

    def one_head(qh, kk, vv, mask, sink_val):
        s = _dot_nt(qh, kk) * ATT_SCALE
        if mask is not None:
            s = jnp.where(mask, s, NEG_INF)
        m = jnp.max(s, axis=-1, keepdims=True)
        if sink_val is not None:
            m = jnp.maximum(m, sink_val)
        p = jnp.exp(s - m)
        l = jnp.sum(p, axis=-1, keepdims=True)
        if sink_val is not None:
            l = l + jnp.exp(sink_val - m)
        o = _dot(p.astype(BF16), vv) / l
        return o, m + jnp.log(l)

    def body(sink_ref, q_ref, k_ref, v_ref, o_ref, lse_ref):
        kv = pl.program_id(0)
        i = pl.program_id(1)

        def run(kk, vv, mask):
            for g in range(GROUPS):
                sink_val = sink_ref[kv * GROUPS + g] if windowed else None
                o, lse = one_head(q_ref[:, g * HEAD_DIM:(g + 1) * HEAD_DIM], kk, vv, mask, sink_val)
                o_ref[:, g * HEAD_DIM:(g + 1) * HEAD_DIM] = o.astype(BF16)
                lse_ref[g] = lse

        @pl.when(i == 0)
        def _():
            if windowed:
                o_ref[...] = jnp.zeros_like(o_ref)
                lse_ref[...] = jnp.zeros_like(lse_ref)
            else:
                run(k_ref[0:ctx_len], v_ref[0:ctx_len], None)

        @pl.when(i > 0)
        def _():
            if windowed:
                start = _band_start(i, t)
                kk = jnp.concatenate([k_ref[0:ctx_len], k_ref[pl.ds(start, BAND)]], axis=0)
                vv = jnp.concatenate([v_ref[0:ctx_len], v_ref[pl.ds(start, BAND)]], axis=0)
                run(kk, vv, _band_mask(i, start, ctx_len))
            else:
                run(k_ref[...], v_ref[...], None)

    kvspec = pl.BlockSpec((t, HEAD_DIM), lambda kv, i: (0, kv))
    return _pcall(
        body, name=name, grid=(KV_HEADS, t // TM),
        in_specs=[pl.BlockSpec(memory_space=pltpu.SMEM), pl.BlockSpec((TM, gw), lambda kv, i: (i, kv)), kvspec, kvspec],
        out_specs=[pl.BlockSpec((TM, gw), lambda kv, i: (i, kv)),
                   pl.BlockSpec((GROUPS, TM, 1), lambda kv, i: (kv, i, 0))],
        out_shape=[jax.ShapeDtypeStruct((t, d_model), BF16), jax.ShapeDtypeStruct((Q_HEADS, t, 1), F32)],
        compiler_params=_params("parallel", "parallel"),
    )(sink, q, k, v)


def attention_bwd(q, k, v, cat, lse, dcat, sink, ctx_len, windowed, name):
    t = q.shape[0]
    gw = GROUPS * HEAD_DIM

    def body(sink_ref, q_ref, k_ref, v_ref, o_ref, lse_ref, do_ref, dq_ref, dk_ref, dv_ref, dsink_ref):
        kv = pl.program_id(0)
        i = pl.program_id(1)

        @pl.when(i == 0)
        def _():
            dk_ref[...] = jnp.zeros_like(dk_ref)
            dv_ref[...] = jnp.zeros_like(dv_ref)
            dsink_ref[...] = jnp.zeros_like(dsink_ref)

        def run(kk, vv, mask, accumulate):
            for g in range(GROUPS):
                cols = slice(g * HEAD_DIM, (g + 1) * HEAD_DIM)
                qh = q_ref[:, cols]
                doh = do_ref[:, cols]
                delta = jnp.sum(doh * o_ref[:, cols].astype(F32), axis=-1, keepdims=True)
                lse_g = lse_ref[g]
                s = _dot_nt(qh, kk) * ATT_SCALE
                if mask is not None:
                    s = jnp.where(mask, s, NEG_INF)
                p = jnp.exp(s - lse_g)
                dob = doh.astype(BF16)
                dp = _dot_nt(dob, vv)
                ds = (p * (dp - delta) * ATT_SCALE).astype(BF16)
                dq_ref[:, cols] = _dot(ds, kk)
                accumulate(_dot_tn(ds, qh), _dot_tn(p.astype(BF16), dob))
                if windowed:
                    p_sink = jnp.exp(sink_ref[kv * GROUPS + g] - lse_g)
                    dsink_ref[g:g + 1, :] += jnp.sum(-p_sink * delta, axis=0, keepdims=True)

        @pl.when(i == 0)
        def _():
            if windowed:
                dq_ref[...] = jnp.zeros_like(dq_ref)
            else:
                def acc_ctx(dkp, dvp):
                    dk_ref[0:ctx_len] += dkp
                    dv_ref[0:ctx_len] += dvp

                run(k_ref[0:ctx_len], v_ref[0:ctx_len], None, acc_ctx)

        @pl.when(i > 0)
        def _():
            if windowed:
                start = _band_start(i, t)
                kk = jnp.concatenate([k_ref[0:ctx_len], k_ref[pl.ds(start, BAND)]], axis=0)
                vv = jnp.concatenate([v_ref[0:ctx_len], v_ref[pl.ds(start, BAND)]], axis=0)

                def acc_band(dkp, dvp):
                    dk_ref[0:ctx_len] += dkp[0:ctx_len]
                    dv_ref[0:ctx_len] += dvp[0:ctx_len]
                    dk_ref[pl.ds(start, BAND)] += dkp[ctx_len:ctx_len + BAND]
                    dv_ref[pl.ds(start, BAND)] += dvp[ctx_len:ctx_len + BAND]

                run(kk, vv, _band_mask(i, start, ctx_len), acc_band)
            else:
                def acc_all(dkp, dvp):
                    dk_ref[...] += dkp
                    dv_ref[...] += dvp

                run(k_ref[...], v_ref[...], None, acc_all)

    kvspec = pl.BlockSpec((t, HEAD_DIM), lambda kv, i: (0, kv))
    grp = pl.BlockSpec((TM, gw), lambda kv, i: (i, kv))
    return _pcall(
        body, name=name, grid=(KV_HEADS, t // TM),
        in_specs=[pl.BlockSpec(memory_space=pltpu.SMEM), grp, kvspec, kvspec, grp,
                  pl.BlockSpec((GROUPS, TM, 1), lambda kv, i: (kv, i, 0)), grp],
        out_specs=[grp, kvspec, kvspec, pl.BlockSpec((None, 8, 128), lambda kv, i: (kv, 0, 0))],
        out_shape=[jax.ShapeDtypeStruct((t, ATT_WIDTH), F32), jax.ShapeDtypeStruct((t, KV_WIDTH), F32),
                   jax.ShapeDtypeStruct((t, KV_WIDTH), F32), jax.ShapeDtypeStruct((KV_HEADS, 8, 128), F32)],
        compiler_params=_params("parallel", "arbitrary"),
    )(sink, q, k, v, cat, lse, dcat)


def _segment_pos(t, ctx_len, width):
    row = lax.broadcasted_iota(jnp.int32, (t, width), 0)
    in_ctx = row < ctx_len
    return jnp.where(in_ctx, row, row - ctx_len), jnp.where(in_ctx, ctx_len, t - ctx_len)


def _shifted(x, offset, pos, seg_len):
    t = x.shape[0]
    moved = x if offset == 0 else pltpu.roll(x, (-offset) % t, 0)
    ok = (pos + offset >= 0) & (pos + offset < seg_len)
    return jnp.where(ok, moved, 0.0)


def conv_fwd(proj, conv_w, cat, ctx_len, name):
    t = proj.shape[0]
    base = QKV_WIDTH // 128
    per = AUX_WIDTH // 128

    def body(gb_ref, gc_ref, u_ref, w_ref, cat_in, o_ref):
        del cat_in
        pos, seg = _segment_pos(t, ctx_len, 128)
        z = gc_ref[...] * u_ref[...]
        conv = (w_ref[0:1, :] * _shifted(z, -1, pos, seg) + w_ref[1:2, :] * z
                + w_ref[2:3, :] * _shifted(z, 1, pos, seg))
        o_ref[...] = (gb_ref[...] * conv).astype(BF16)

    def col(off):
        return pl.BlockSpec((t, 128), lambda j: (0, base + off * per + j))

    return _pcall(
        body, name=name, grid=(per,),
        in_specs=[col(0), col(1), col(2), pl.BlockSpec((3, 128), lambda j: (0, j)),
                  pl.BlockSpec(memory_space=pl.ANY)],
        out_specs=pl.BlockSpec((t, 128), lambda j: (0, ATT_WIDTH // 128 + j)),
        out_shape=jax.ShapeDtypeStruct(cat.shape, BF16),
        input_output_aliases={4: 0},
        compiler_params=_params("parallel"),
    )(proj, proj, proj, conv_w, cat)


def conv_bwd(dcat, proj, conv_w, ctx_len, name):
    t = proj.shape[0]
    base = QKV_WIDTH // 128
    per = AUX_WIDTH // 128

    def body(do_ref, gb_ref, gc_ref, u_ref, w_ref, da_ref, dw_ref):
        pos, seg = _segment_pos(t, ctx_len, 128)
        gc = gc_ref[...]
        u = u_ref[...]
        z = gc * u
        zm = _shifted(z, -1, pos, seg)
        zp = _shifted(z, 1, pos, seg)
        conv = w_ref[0:1, :] * zm + w_ref[1:2, :] * z + w_ref[2:3, :] * zp
        dout = do_ref[...]
        da_ref[0] = (dout * conv).astype(BF16)
        dconv = dout * gb_ref[...]
        dw_ref[0:1, :] = jnp.sum(dconv * zm, axis=0, keepdims=True)
        dw_ref[1:2, :] = jnp.sum(dconv * z, axis=0, keepdims=True)
        dw_ref[2:3, :] = jnp.sum(dconv * zp, axis=0, keepdims=True)
        dz = (w_ref[1:2, :] * dconv + w_ref[0:1, :] * _shifted(dconv, 1, pos, seg)
              + w_ref[2:3, :] * _shifted(dconv, -1, pos, seg))
        da_ref[1] = (dz * u).astype(BF16)
        da_ref[2] = (dz * gc).astype(BF16)

    def col(off):
        return pl.BlockSpec((t, 128), lambda j: (0, base + off * per + j))

    return _pcall(
        body, name=name, grid=(per,),
        in_specs=[pl.BlockSpec((t, 128), lambda j: (0, ATT_WIDTH // 128 + j)), col(0), col(1), col(2),
                  pl.BlockSpec((3, 128), lambda j: (0, j))],
        out_specs=[pl.BlockSpec((3, t, 128), lambda j: (0, 0, j)), pl.BlockSpec((3, 128), lambda j: (0, j))],
        out_shape=[jax.ShapeDtypeStruct((3, t, AUX_WIDTH), BF16), jax.ShapeDtypeStruct((3, AUX_WIDTH), F32)],
        compiler_params=_params("parallel"),
    )(dcat, proj, proj, proj, conv_w)


def _pooled(u, w, pos, seg):
    lo = jnp.clip(pos - w // 2, 0, seg)
    hi = jnp.clip(pos - w // 2 + w, 0, seg)
    inv = 1.0 / (hi - lo).astype(F32)
    total = _shifted(u, -(w // 2), pos, seg)
    for o in range(-(w // 2) + 1, w // 2):
        total = total + _shifted(u, o, pos, seg)
    return total * inv - u, inv


def pool_fwd(proj, pool_w, pool_scale, cat, ctx_len, name):
    t = proj.shape[0]

    def body(u_ref, w_ref, sc_ref, cat_in, o_ref):
        del cat_in
        pos, seg = _segment_pos(t, ctx_len, 128)
        for g, w in enumerate(POOL_WINDOWS):
            cols = slice(g * 128, (g + 1) * 128)
            pooled, _ = _pooled(u_ref[:, cols], w, pos, seg)
            mixed = _dot(pooled.astype(BF16), w_ref[g].astype(BF16))
            o_ref[:, cols] = (mixed * sc_ref[:, cols]).astype(BF16)

    return _pcall(
        body, name=name, grid=(1,),
        in_specs=[pl.BlockSpec((t, AUX_WIDTH), lambda j: (0, QKV_WIDTH // AUX_WIDTH)),
                  pl.BlockSpec((AUX_GROUPS, 128, 128), lambda j: (0, 0, 0)),
                  pl.BlockSpec((1, AUX_WIDTH), lambda j: (0, 0)), pl.BlockSpec(memory_space=pl.ANY)],
        out_specs=pl.BlockSpec((t, AUX_WIDTH), lambda j: (0, ATT_WIDTH // AUX_WIDTH)),
        out_shape=jax.ShapeDtypeStruct(cat.shape, BF16),
        input_output_aliases={3: 0},
        compiler_params=_params("arbitrary"),
    )(proj, pool_w, pool_scale, cat)


def pool_bwd(dcat, proj, pool_w, pool_scale, ctx_len, name):
    t = proj.shape[0]

    def body(do_ref, u_ref, w_ref, sc_ref, da_ref, dw_ref, dsc_ref):
        pos, seg = _segment_pos(t, ctx_len, 128)
        for g, w in enumerate(POOL_WINDOWS):
            cols = slice(g * 128, (g + 1) * 128)
            pooled, inv = _pooled(u_ref[:, cols], w, pos, seg)
            pb = pooled.astype(BF16)
            wb = w_ref[g].astype(BF16)
            mixed = _dot(pb, wb)
            dout = do_ref[:, cols]
            dsc_ref[:, cols] = jnp.sum(dout * mixed, axis=0, keepdims=True)
            dmixed = (dout * sc_ref[:, cols]).astype(BF16)
            dw_ref[g] = _dot_tn(pb, dmixed)
            dpooled = _dot_nt(dmixed, wb)
            spread = dpooled * inv
            du = _shifted(spread, w // 2, pos, seg) - dpooled
            for o in range(-(w // 2) + 1, w // 2):
                du = du + _shifted(spread, -o, pos, seg)
            da_ref[0, :, cols] = du.astype(BF16)

    return _pcall(
        body, name=name, grid=(1,),
        in_specs=[pl.BlockSpec((t, AUX_WIDTH), lambda j: (0, ATT_WIDTH // AUX_WIDTH)),
                  pl.BlockSpec((t, AUX_WIDTH), lambda j: (0, QKV_WIDTH // AUX_WIDTH)),
                  pl.BlockSpec((AUX_GROUPS, 128, 128), lambda j: (0, 0, 0)),
                  pl.BlockSpec((1, AUX_WIDTH), lambda j: (0, 0))],
        out_specs=[pl.BlockSpec((1, t, AUX_WIDTH), lambda j: (0, 0, 0)),
                   pl.BlockSpec((AUX_GROUPS, 128, 128), lambda j: (0, 0, 0)),
                   pl.BlockSpec((1, AUX_WIDTH), lambda j: (0, 0))],
        out_shape=[jax.ShapeDtypeStruct((1, t, AUX_WIDTH), BF16), jax.ShapeDtypeStruct((AUX_GROUPS, 128, 128), F32),
                   jax.ShapeDtypeStruct((1, AUX_WIDTH), F32)],
        compiler_params=_params("arbitrary"),
    )(dcat, proj, pool_w, pool_scale)


MOD_NAMES = ("shift1", "scale1", "gate1", "shift2", "scale2", "gate2")


def _layer_fwd(h, lw, weight, mods, tabs, ctx_len, windowed, tag, after_mixer=None):
    cos, sin = tabs
    xn = norm_mod_fwd(h, lw["norm1_g"], mods["shift1"], mods["scale1"], tag + "norm1_fwd")
    lw["w_in"] = weight("w_in", xn)
    proj = mm_nn(xn, lw["w_in"], tag + "w_in_fwd")
    q, k, v = qk_prep_fwd(proj, lw["q_g"], lw["k_g"], cos, sin, tag + "qk_fwd")
    cat, lse = attention_fwd(q, k, v, lw["sink"], ctx_len, windowed, tag + "attn_fwd")
    if windowed:
        cat = pool_fwd(proj, lw["pool_w"], lw["pool_scale"], cat, ctx_len, tag + "pool_fwd")
    else:
        cat = conv_fwd(proj, lw["conv_w"], cat, ctx_len, tag + "conv_fwd")
    lw["w_out"] = weight("w_out", cat)
    gate1 = mods["gate1"] if after_mixer is None else mods["gate1"] + after_mixer(cat)[0, 0]
    y, h1 = mm_nn_residual(cat, lw["w_out"], h, gate1, tag + "w_out_fwd")
    hn = norm_mod_fwd(h1, lw["norm2_g"], mods["shift2"], mods["scale2"], tag + "norm2_fwd")
    lw["w_gate"], lw["w_up"] = weight("w_gate", hn), weight("w_up", hn)
    g, u, act = mm_swiglu(hn, lw["w_gate"], lw["w_up"], tag + "ffn_in_fwd")
    lw["w_down"] = weight("w_down", act)
    f, h2 = mm_nn_residual(act, lw["w_down"], h1, mods["gate2"], tag + "w_down_fwd")
    saved = dict(h=h, xn=xn, proj=proj, q=q, k=k, v=v, cat=cat, lse=lse, y=y, h1=h1, hn=hn, g=g, u=u, act=act, f=f)
    return h2, saved


def _layer_bwd_ffn(dh, sv, lw, mods, tag):
    big, small = {}, {}
    df, dgate2 = gate_bwd(dh, sv["f"], mods["gate2"], tag + "gate2_bwd")
    dgp, du = mm_nt_swiglu_bwd(df, lw["w_down"], sv["g"], sv["u"], tag + "w_down_dgrad")
    big["w_down"] = mm_tn(sv["act"], df, tag + "w_down_wgrad", col_sharded=False)
    dhn = mm_nt_cols(dgp, lw["w_gate"], tag + "w_gate_dgrad")
    dhn = mm_nt_cols(du, lw["w_up"], tag + "w_up_dgrad", add=dhn)
    big["w_gate"] = mm_tn(sv["hn"], dgp, tag + "w_gate_wgrad", col_sharded=True)
    big["w_up"] = mm_tn(sv["hn"], du, tag + "w_up_wgrad", col_sharded=True)
    dh1, dshift2, dscale2, small["norm2_g"] = norm_mod_bwd(dhn, sv["h1"], lw["norm2_g"], mods["scale2"], dh,
                                                           tag + "norm2_bwd")
    return dh1, big, small, (dshift2, dscale2, dgate2)


def _layer_bwd_mixer(dh1, sv, lw, mods, tabs, ctx_len, windowed, tag, between=None):
    cos, sin = tabs
    big, small = {}, {}
    dy, dgate1 = gate_bwd(dh1, sv["y"], mods["gate1"], tag + "gate1_bwd")
    dcat = mm_nt_rows(dy, lw["w_out"], tag + "w_out_dgrad")
    big["w_out"] = mm_tn(sv["cat"], dy, tag + "w_out_wgrad", col_sharded=False)
    if windowed:
        daux, small["pool_w"], small["pool_scale"] = pool_bwd(dcat, sv["proj"], lw["pool_w"], lw["pool_scale"],
                                                              ctx_len, tag + "pool_bwd")
    else:
        daux, small["conv_w"] = conv_bwd(dcat, sv["proj"], lw["conv_w"], ctx_len, tag + "conv_bwd")
    dq, dk, dv, dsink = attention_bwd(sv["q"], sv["k"], sv["v"], sv["cat"], sv["lse"], dcat, lw["sink"], ctx_len,
                                      windowed, tag + "attn_bwd")
    if windowed:
        small["sink"] = dsink[:, :GROUPS, 0].reshape(Q_HEADS)
    q_g = lw["q_g"] if between is None else lw["q_g"] + between(dq)[0, 0]
    dproj, small["q_g"], small["k_g"] = qk_prep_bwd(dq, dk, dv, daux, sv["proj"], q_g, lw["k_g"], cos, sin,
                                                    tag + "qk_bwd")
    dxn = mm_nt_cols(dproj, lw["w_in"], tag + "w_in_dgrad")
    big["w_in"] = mm_tn(sv["xn"], dproj, tag + "w_in_wgrad", col_sharded=True)
    dh0, dshift1, dscale1, small["norm1_g"] = norm_mod_bwd(dxn, sv["h"], lw["norm1_g"], mods["scale1"], dh1,
                                                           tag + "norm1_bwd")
    return dh0, big, small, (dshift1, dscale1, dgate1)


def _dmod_rows(mixer_part, ffn_part):
    return jnp.concatenate([m[:, 0, :] for m in (*mixer_part, *ffn_part)], axis=1)


def _tie(mods, name, token):
    return {**mods, name: mods[name] + token[0, 0]}


def _place():
    x, y, c = lax.axis_index("x"), lax.axis_index("y"), lax.axis_index("c")
    chips = [(1 - x, y), (x, 1 - y), (1 - x, 1 - y)]
    return x, y, c, chips


def _remote(src, dst, send_sem, recv_sem, to):
    return pltpu.make_async_remote_copy(src_ref=src, dst_ref=dst, send_sem=send_sem, recv_sem=recv_sem,
                                        device_id=to, device_id_type=MESH)


ANY = pl.BlockSpec(memory_space=pl.ANY)


def all_gather8(x_shard, name):
    m_per, n = x_shard.shape

    def body(x_ref, out_ref, send_sems, recv_sems, local_sem):
        x, y, c, chips = _place()
        me, sibling = (x, y, c), (x, y, 1 - c)

        def rows(px, py, pc):
            return out_ref.at[pl.ds((4 * px + 2 * py + pc) * m_per, m_per), :]

        def copy(k, block, to, src=None):
            return _remote(rows(*block) if src is None else src, rows(*block), send_sems.at[k], recv_sems.at[k], to)

        mine = pltpu.make_async_copy(x_ref, rows(*me), local_sem)
        mine.start()
        first = [copy(0, me, sibling, src=x_ref)]
        first += [copy(1 + j, me, (*chip, c), src=x_ref) for j, chip in enumerate(chips)]
        for cp in first:
            cp.start()
        passed = [copy(4 + j, (*chip, c), sibling) for j, chip in enumerate(chips)]
        for j, chip in enumerate(chips):
            copy(1 + j, (*chip, c), me).wait_recv()
            passed[j].start()
        copy(0, sibling, me).wait_recv()
        for j, chip in enumerate(chips):
            copy(4 + j, (*chip, 1 - c), me).wait_recv()
        for cp in first + passed:
            cp.wait_send()
        mine.wait()

    return _pcall(
        body, name=name,
        out_shape=jax.ShapeDtypeStruct((N_DEV * m_per, n), x_shard.dtype),
        in_specs=[pl.BlockSpec(memory_space=pltpu.VMEM)],
        out_specs=pl.BlockSpec(memory_space=pltpu.VMEM),
        scratch_shapes=[pltpu.SemaphoreType.DMA((7,)), pltpu.SemaphoreType.DMA((7,)), pltpu.SemaphoreType.DMA],
        compiler_params=pltpu.CompilerParams(vmem_limit_bytes=VMEM_LIMIT),
    )(x_shard)


def _row_tile(rows):
    for tr in (256, 128, 64, 32, 16):
        if rows % tr == 0:
            return tr
    return rows


def place_shard(w, p, name):
    r, cols = w.shape
    tr = _row_tile(r)

    def body(p_ref, w_ref, o_ref):
        del p_ref
        o_ref[...] = w_ref[...].astype(BF16)

    return _pcall(
        body, name=name,
        grid_spec=pltpu.PrefetchScalarGridSpec(
            num_scalar_prefetch=1, grid=(r // tr,),
            in_specs=[pl.BlockSpec((tr, cols), lambda i, p_ref: (i, 0))],
            out_specs=pl.BlockSpec((None, tr, cols), lambda i, p_ref: (p_ref[0], i, 0))),
        out_shape=jax.ShapeDtypeStruct((N_SHARD, r, cols), BF16),
        compiler_params=_params("parallel"),
    )(p.reshape(1).astype(jnp.int32), w)


def gather_weight_shards(stacks):
    n = len(stacks)

    def body(*refs):
        outs = refs[n:2 * n]
        send_sems, recv_sems = refs[2 * n:]
        x, y, c, chips = _place()
        p = 2 * x + y
        sibling = (x, y, 1 - c)
        started = []
        for i in range(n):
            hr = outs[i].shape[1] // 2
            mine = outs[i].at[p, pl.ds(c * hr, hr)]
            for j, chip in enumerate(chips):
                cp = _remote(mine, mine, send_sems.at[i, j], recv_sems.at[i, j], (*chip, c))
                cp.start()
                started.append(cp)
        for i in range(n):
            hr = outs[i].shape[1] // 2
            for j, (px, py) in enumerate(chips):
                half = outs[i].at[2 * px + py, pl.ds(c * hr, hr)]
                _remote(half, half, send_sems.at[i, j], recv_sems.at[i, j], (px, py, c)).wait_recv()
                fwd = _remote(half, half, send_sems.at[i, 3 + j], recv_sems.at[i, 3 + j], sibling)
                fwd.start()
                started.append(fwd)
        for i in range(n):
            hr = outs[i].shape[1] // 2
            for j, (px, py) in enumerate(chips):
                other = outs[i].at[2 * px + py, pl.ds((1 - c) * hr, hr)]
                _remote(other, other, send_sems.at[i, 3 + j], recv_sems.at[i, 3 + j], sibling).wait_recv()
        for cp in started:
            cp.wait_send()

    return _pcall(
        body, name="gather_weight_shards",
        out_shape=[jax.ShapeDtypeStruct(s.shape, s.dtype) for s in stacks],
        in_specs=[ANY] * n, out_specs=[ANY] * n,
        input_output_aliases={i: i for i in range(n)},
        scratch_shapes=[pltpu.SemaphoreType.DMA((n, 6)), pltpu.SemaphoreType.DMA((n, 6))],
    )(*stacks)


HBM_SPEC = pl.BlockSpec(memory_space=pltpu.HBM)
SEM_SPEC = pl.BlockSpec(memory_space=pltpu.SEMAPHORE)
EFFECT = pltpu.SideEffectType.DATAFLOW_SIDE_EFFECTING


def split_start(name, copies, n_sems, arrays, after):
    m = len(arrays)

    def body(*refs):
        for cp in copies(refs[:m], refs[m + 1], refs[m + 2]):
            cp.start()
        refs[-1][...] = jnp.zeros_like(refs[-1])

    res = _pcall(
        body, name=name,
        out_shape=(pltpu.SemaphoreType.DMA((n_sems,)), pltpu.SemaphoreType.DMA((n_sems,)),
                   *[pltpu.HBM(s.shape, s.dtype) for s in arrays], jax.ShapeDtypeStruct((8, 128), F32)),
        in_specs=[HBM_SPEC] * m + [ANY],
        out_specs=(SEM_SPEC, SEM_SPEC, *[HBM_SPEC] * m, pl.BlockSpec(memory_space=pltpu.VMEM)),
        input_output_aliases={i: 2 + i for i in range(m)},
        compiler_params=pltpu.CompilerParams(has_side_effects=EFFECT),
    )(*[pltpu.with_memory_space_constraint(s, pltpu.HBM) for s in arrays], after)
    return res[0], res[1], list(res[2:2 + m]), res[2 + m]


def split_wait(name, copies, send_sems, recv_sems, arrays, after):
    m = len(arrays)

    def body(*refs):
        for cp in copies(refs[:m], refs[m], refs[m + 1]):
            cp.wait_send()
            cp.wait_recv()

    return _pcall(
        body, name=name,
        out_shape=[pltpu.HBM(s.shape, s.dtype) for s in arrays],
        in_specs=[HBM_SPEC] * m + [SEM_SPEC, SEM_SPEC, ANY],
        out_specs=[HBM_SPEC] * m,
        input_output_aliases={i: i for i in range(m)},
        compiler_params=pltpu.CompilerParams(has_side_effects=EFFECT),
    )(*arrays, send_sems, recv_sems, after)


def gather_copies(n):
    def copies(stacks, send_sems, recv_sems):
        x, y, c, chips = _place()
        p = 2 * x + y
        return [_remote(stacks[i].at[p], stacks[i].at[p], send_sems.at[3 * i + j], recv_sems.at[3 * i + j], (*chip, c))
                for i in range(n) for j, chip in enumerate(chips)]
    return copies


def gather_half_copies(n):
    def copies(stacks, send_sems, recv_sems):
        x, y, c, chips = _place()
        p = 2 * x + y
        out = []
        for i in range(n):
            hr = stacks[i].shape[1] // 2
            mine = stacks[i].at[p, pl.ds(c * hr, hr)]
            out += [_remote(mine, mine, send_sems.at[3 * i + j], recv_sems.at[3 * i + j], (*chip, c))
                    for j, chip in enumerate(chips)]
        return out
    return copies


def forward_half_copies(n):
    def copies(stacks, send_sems, recv_sems):
        x, y, c, chips = _place()
        out = []
        for i in range(n):
            hr = stacks[i].shape[1] // 2
            for j, (px, py) in enumerate(chips):
                half = stacks[i].at[2 * px + py, pl.ds(c * hr, hr)]
                out.append(_remote(half, half, send_sems.at[3 * i + j], recv_sems.at[3 * i + j], (x, y, 1 - c)))
        return out
    return copies


def sibling_half_copies(n):
    def copies(refs, send_sems, recv_sems):
        x, y, c, _ = _place()
        out = []
        for i in range(n):
            hr = refs[n + i].shape[1]
            out.append(_remote(refs[i].at[:, pl.ds((1 - c) * hr, hr), :], refs[n + i], send_sems.at[i], recv_sems.at[i],
                               (x, y, 1 - c)))
        return out
    return copies


def chip_quarter_copies(n):
    def copies(refs, send_sems, recv_sems):
        _, _, c, chips = _place()
        return [_remote(refs[i].at[2 * px + py], refs[n + i].at[j], send_sems.at[3 * i + j], recv_sems.at[3 * i + j],
                        (px, py, c)) for i in range(n) for j, (px, py) in enumerate(chips)]
    return copies


def exchange_final_halves(grads, name):
    n = len(grads)

    def body(*refs):
        outs = refs[n:2 * n]
        send_sems, recv_sems = refs[2 * n:]
        x, y, c, _ = _place()
        sends = []
        for i in range(n):
            hr = outs[i].shape[0] // 2
            mine = outs[i].at[pl.ds(c * hr, hr)]
            cp = _remote(mine, mine, send_sems.at[i], recv_sems.at[i], (x, y, 1 - c))
            cp.start()
            sends.append(cp)
        for i in range(n):
            hr = outs[i].shape[0] // 2
            other = outs[i].at[pl.ds((1 - c) * hr, hr)]
            _remote(other, other, send_sems.at[i], recv_sems.at[i], (x, y, 1 - c)).wait_recv()
        for cp in sends:
            cp.wait_send()

    return _pcall(
        body, name=name,
        out_shape=[jax.ShapeDtypeStruct(g.shape, g.dtype) for g in grads],
        in_specs=[ANY] * n, out_specs=[ANY] * n,
        input_output_aliases={i: i for i in range(n)},
        scratch_shapes=[pltpu.SemaphoreType.DMA((n,)), pltpu.SemaphoreType.DMA((n,))],
    )(*grads)


def add_own_half(g, recv, c, name):
    s, r, cols = g.shape
    hr = r // 2
    tr = _row_tile(hr)
    nb = hr // tr

    def body(c_ref, g_ref, r_ref, o_ref):
        del c_ref
        o_ref[...] = (g_ref[...] + r_ref[...]).astype(BF16)

    return _pcall(
        body, name=name,
        grid_spec=pltpu.PrefetchScalarGridSpec(
            num_scalar_prefetch=1, grid=(s, nb),
            in_specs=[pl.BlockSpec((None, tr, cols), lambda j, i, c_ref: (j, c_ref[0] * nb + i, 0)),
                      pl.BlockSpec((None, tr, cols), lambda j, i, c_ref: (j, i, 0))],
            out_specs=pl.BlockSpec((None, tr, cols), lambda j, i, c_ref: (j, i, 0))),
        out_shape=jax.ShapeDtypeStruct((s, hr, cols), BF16),
        compiler_params=_params("parallel", "parallel"),
    )(c.reshape(1).astype(jnp.int32), g, recv)


def add_quarters(own, others, p, c, name, deps=()):
    _, hr, cols = own.shape
    tr = _row_tile(hr)
    nb = hr // tr

    def body(pc_ref, own_ref, oth_ref, *rest):
        o_ref = rest[-1]
        acc = own_ref[...].astype(F32) + oth_ref[0].astype(F32)
        acc = acc + oth_ref[1].astype(F32)
        o_ref[...] = acc + oth_ref[2].astype(F32)

    return _pcall(
        body, name=name,
        grid_spec=pltpu.PrefetchScalarGridSpec(
            num_scalar_prefetch=1, grid=(nb,),
            in_specs=[pl.BlockSpec((None, tr, cols), lambda i, pc: (pc[0], i, 0)),
                      pl.BlockSpec((3, tr, cols), lambda i, pc: (0, i, 0))] + [ANY] * len(deps),
            out_specs=pl.BlockSpec((tr, cols), lambda i, pc: (pc[1] * nb + i, 0))),
        out_shape=jax.ShapeDtypeStruct((2 * hr, cols), F32),
        compiler_params=_params("parallel"),
    )(jnp.stack([p, c]).astype(jnp.int32), own, others, *deps)


def _adamw_math(w, g, m, v):
    m2 = ADAM_B1 * m + (1.0 - ADAM_B1) * g
    v2 = ADAM_B2 * v + (1.0 - ADAM_B2) * (g * g)
    m_hat = m2 / (1.0 - ADAM_B1 ** ADAM_STEP)
    v_hat = v2 / (1.0 - ADAM_B2 ** ADAM_STEP)
    delta = -ADAM_LR * (m_hat / (jnp.sqrt(v_hat) + ADAM_EPS) + ADAM_WD * w)
    return delta, m2, v2


def adamw(w, g, m, v, name):
    r, cols = w.shape
    tr = _row_tile(r)

    def body(w_ref, g_ref, m_ref, v_ref, d_ref, m2_ref, v2_ref):
        d_ref[...], m2_ref[...], v2_ref[...] = _adamw_math(w_ref[...], g_ref[...], m_ref[...], v_ref[...])

    blk = pl.BlockSpec((tr, cols), lambda i: (i, 0))
    return _pcall(
        body, name=name, grid=(r // tr,),
        in_specs=[blk] * 4, out_specs=[blk] * 3,
        out_shape=[jax.ShapeDtypeStruct((r, cols), F32)] * 3,
        compiler_params=_params("parallel"),
    )(w, g, m, v)


def small_update(gathered, w, m, v):
    nd, r, lanes = gathered.shape

    def body(ga_ref, w_ref, m_ref, v_ref, g_ref, d_ref, m2_ref, v2_ref):
        g = ga_ref[0] + ga_ref[1]
        for dev in range(2, nd):
            g = g + ga_ref[dev]
        g_ref[...] = g
        d_ref[...], m2_ref[...], v2_ref[...] = _adamw_math(w_ref[...], g, m_ref[...], v_ref[...])

    return _pcall(
        body, name="small_update",
        out_shape=[jax.ShapeDtypeStruct((r, lanes), F32)] * 4,
        compiler_params=pltpu.CompilerParams(vmem_limit_bytes=VMEM_LIMIT),
    )(gathered, w, m, v)


def mod_fwd(c16, w_mod, b_mod, name):
    d, ns = w_mod.shape
    tn = 512

    def body(c_ref, w_ref, b_ref, o_ref):
        cv = c_ref[...]
        sc = (cv * _sigmoid(cv)).astype(BF16)
        o_ref[...] = _dot(sc, w_ref[...].astype(BF16)) + b_ref[...]

    return _pcall(
        body, name=name, grid=(ns // tn,),
        in_specs=[pl.BlockSpec((16, d), lambda j: (0, 0)), pl.BlockSpec((d, tn), lambda j: (0, j)),
                  pl.BlockSpec((1, tn), lambda j: (0, j))],
        out_specs=pl.BlockSpec((16, tn), lambda j: (0, j)),
        out_shape=jax.ShapeDtypeStruct((16, ns), F32),
        compiler_params=_params("parallel"),
    )(c16, w_mod, b_mod)


def wmod_update(c16, dmod16, w, m, v, name):
    d, ns = w.shape
    tn = 256

    def body(c_ref, dm_ref, w_ref, m_ref, v_ref, g_ref, d_ref, m2_ref, v2_ref):
        cv = c_ref[...]
        sc = (cv * _sigmoid(cv)).astype(BF16)
        g = _dot_tn(sc, dm_ref[...].astype(BF16))
        g_ref[...] = g
        d_ref[...], m2_ref[...], v2_ref[...] = _adamw_math(w_ref[...], g, m_ref[...], v_ref[...])

    blk = pl.BlockSpec((d, tn), lambda j: (0, j))
    return _pcall(
        body, name=name, grid=(ns // tn,),
        in_specs=[pl.BlockSpec((16, d), lambda j: (0, 0)), pl.BlockSpec((16, tn), lambda j: (0, j)), blk, blk, blk],
        out_specs=[blk] * 4,
        out_shape=[jax.ShapeDtypeStruct((d, ns), F32)] * 4,
        compiler_params=_params("parallel"),
    )(c16, dmod16, w, m, v)


def cctx_partial(dm0, w0, dm1, w1):
    d, ns = w0.shape
    tn = 512

    def body(dm0_ref, w0_ref, dm1_ref, w1_ref, o_ref):
        @pl.when(pl.program_id(0) == 0)
        def _():
            o_ref[...] = jnp.zeros_like(o_ref)

        for dm_ref, w_ref in ((dm0_ref, w0_ref), (dm1_ref, w1_ref)):
            tot = jnp.sum(dm_ref[...], axis=0, keepdims=True)
            lhs = jnp.broadcast_to(tot, (8, tn)).astype(BF16)
            o_ref[...] += _dot_nt(lhs, w_ref[...].astype(BF16))

    dspec = pl.BlockSpec((8, tn), lambda j: (0, j))
    wspec = pl.BlockSpec((d, tn), lambda j: (0, j))
    return _pcall(
        body, name="cctx_partial", grid=(ns // tn,),
        in_specs=[dspec, wspec, dspec, wspec],
        out_specs=pl.BlockSpec((8, d), lambda j: (0, 0)),
        out_shape=jax.ShapeDtypeStruct((8, d), F32),
        compiler_params=_params("arbitrary"),
    )(dm0, w0, dm1, w1)


def cctx_update(parts, c_ctx, m, v):
    d = c_ctx.shape[1]

    def body(p_ref, c_ref, m_ref, v_ref, g_ref, d_ref, m2_ref, v2_ref):
        tot = p_ref[0:1, :] + p_ref[1:2, :]
        tot = tot + p_ref[2:3, :]
        tot = tot + p_ref[3:4, :]
        cv = c_ref[...]
        sg = _sigmoid(cv)
        g = tot * (sg * (1.0 + cv * (1.0 - sg)))
        g_ref[...] = g
        d_ref[...], m2_ref[...], v2_ref[...] = _adamw_math(cv, g, m_ref[...], v_ref[...])

    return _pcall(
        body, name="cctx_update",
        out_shape=[jax.ShapeDtypeStruct((1, d), F32)] * 4,
    )(parts, c_ctx, m, v)


WEIGHT_NAMES = (
    "c_ctx", "l0_norm1_g", "l0_w_mod", "l0_b_mod", "l0_w_in", "l0_q_norm_g", "l0_k_norm_g", "l0_conv_w", "l0_w_out",
    "l0_norm2_g", "l0_w_gate", "l0_w_up", "l0_w_down", "l1_norm1_g", "l1_w_mod", "l1_b_mod", "l1_w_in",
    "l1_q_norm_g", "l1_k_norm_g", "l1_sink", "l1_pool_w", "l1_pool_scale", "l1_w_out", "l1_norm2_g", "l1_w_gate",
    "l1_w_up", "l1_w_down", "final_norm_g")
BIG_NAMES = ("w_in", "w_out", "w_gate", "w_up", "w_down")
SMALL_SLOTS = tuple(
    [(f"l{l}_{nm}", par and f"l{l}_{nm}") for l in (0, 1)
     for nm, par in (("dmod_lat", False), ("dmod_ctx", False), ("b_mod", True), ("norm1_g", True), ("norm2_g", True),
                     ("q_norm_g", True), ("k_norm_g", True))]
    + [("l0_conv_w", None), ("l1_sink", "l1_sink"), ("l1_pool_w", "l1_pool_w"), ("l1_pool_scale", "l1_pool_scale"),
       ("final_norm_g", "final_norm_g")])


def _pack(values, sizes):
    parts = []
    for (name, _), size in zip(SMALL_SLOTS, sizes):
        v = values.get(name)
        padded = -(-size // 128) * 128
        v = jnp.zeros((padded,), F32) if v is None else jnp.pad(v.reshape(-1).astype(F32), (0, padded - size))
        parts.append(v)
    total = sum(p.shape[0] for p in parts)
    parts.append(jnp.zeros((-(-total // 1024) * 1024 - total,), F32))
    return jnp.concatenate(parts).reshape(-1, 128)


def _offsets(sizes):
    offs, o = {}, 0
    for (name, _), size in zip(SMALL_SLOTS, sizes):
        offs[name] = (o, size)
        o += -(-size // 128) * 128
    return offs


def kernel(x, c, ctx, c_ctx, l0_norm1_g, l0_w_mod, l0_b_mod, l0_w_in, l0_q_norm_g, l0_k_norm_g, l0_conv_w, l0_w_out, l0_norm2_g, l0_w_gate, l0_w_up, l0_w_down, l1_norm1_g, l1_w_mod, l1_b_mod, l1_w_in, l1_q_norm_g, l1_k_norm_g, l1_sink, l1_pool_w, l1_pool_scale, l1_w_out, l1_norm2_g, l1_w_gate, l1_w_up, l1_w_down, final_norm_g, loss_target, m_c_ctx, m_l0_norm1_g, m_l0_w_mod, m_l0_b_mod, m_l0_w_in, m_l0_q_norm_g, m_l0_k_norm_g, m_l0_conv_w, m_l0_w_out, m_l0_norm2_g, m_l0_w_gate, m_l0_w_up, m_l0_w_down, m_l1_norm1_g, m_l1_w_mod, m_l1_b_mod, m_l1_w_in, m_l1_q_norm_g, m_l1_k_norm_g, m_l1_sink, m_l1_pool_w, m_l1_pool_scale, m_l1_w_out, m_l1_norm2_g, m_l1_w_gate, m_l1_w_up, m_l1_w_down, m_final_norm_g, v_c_ctx, v_l0_norm1_g, v_l0_w_mod, v_l0_b_mod, v_l0_w_in, v_l0_q_norm_g, v_l0_k_norm_g, v_l0_conv_w, v_l0_w_out, v_l0_norm2_g, v_l0_w_gate, v_l0_w_up, v_l0_w_down, v_l1_norm1_g, v_l1_w_mod, v_l1_b_mod, v_l1_w_in, v_l1_q_norm_g, v_l1_k_norm_g, v_l1_sink, v_l1_pool_w, v_l1_pool_scale, v_l1_w_out, v_l1_norm2_g, v_l1_w_gate, v_l1_w_up, v_l1_w_down, v_final_norm_g):
    a = dict(locals())
    xi, yi, ci = lax.axis_index("x"), lax.axis_index("y"), lax.axis_index("c")
    p = 2 * xi + yi
    me = 4 * xi + 2 * yi + ci
    d = x.shape[-1]
    conv_cols = l0_conv_w.shape[1]

    placed = {(l, nm): place_shard(a[f"l{l}_{nm}"], p, f"l{l}_{nm}_place") for l in (0, 1) for nm in BIG_NAMES}
    ready = {(0, "w_in"): gather_weight_shards([placed[0, "w_in"]])[0]}
    groups = (((0, "w_out"), (0, "w_gate"), (0, "w_up")), ((0, "w_down"),), ((1, "w_in"), (1, "w_out")),
              ((1, "w_gate"), (1, "w_up"), (1, "w_down")))
    staged = len(groups) - 1
    flights, after = [], ready[0, "w_in"]
    for gi, grp in enumerate(groups):
        first_copies = gather_half_copies if gi == staged else gather_copies
        flights.append(split_start(f"gather_start_{gi}", first_copies(len(grp)), 3 * len(grp),
                                   [placed[k] for k in grp], after))
        after = flights[-1][3]
    started_token = after

    def forward_staged(after):
        n = len(groups[staged])
        send_sems, recv_sems, in_flight, _ = flights[staged]
        landed = split_wait(f"gather_wait_{staged}", gather_half_copies(n), send_sems, recv_sems, in_flight, after)
        flights[staged] = split_start("gather_forward_start", forward_half_copies(n), 3 * n, landed, after)
        return flights[staged][3]

    def weight_of(l):
        def weight(nm, after):
            if (l, nm) not in ready:
                gi = next(i for i, grp in enumerate(groups) if (l, nm) in grp)
                send_sems, recv_sems, in_flight, _ = flights[gi]
                name, copies = ("gather_forward_wait", forward_half_copies) if gi == staged else (
                    f"gather_wait_{gi}", gather_copies)
                landed = split_wait(name, copies(len(groups[gi])), send_sems, recv_sems, in_flight, after)
                ready.update(zip(groups[gi], landed))
            return ready[l, nm]
        return weight

    row0 = jnp.concatenate([c, jnp.pad(l0_conv_w, ((0, 0), (0, d - conv_cols))), jnp.zeros((4, d), F32)], axis=0)
    gathered = all_gather8(row0 + started_token[0:1, 0:1], "gather_cond").reshape(N_DEV, 8, d)
    c_all = gathered[:, 0]
    conv_w = gathered[0::2, 1:4, :conv_cols].transpose(1, 0, 2).reshape(3, N_SHARD * conv_cols)
    c16_fwd = jnp.concatenate([c_all, c_ctx[None], jnp.zeros((7, d), F32)], axis=0)
    c16_bwd = jnp.concatenate([c_all, jnp.broadcast_to(c_ctx[None], (8, d))], axis=0)

    ns_mod = l0_w_mod.shape[1]
    mod_parts = [mod_fwd(c16_fwd, a[f"l{l}_w_mod"], lax.dynamic_slice(a[f"l{l}_b_mod"], (p * ns_mod,), (ns_mod,))[None],
                         f"l{l}_mod_fwd") for l in (0, 1)]
    modg = all_gather8(jnp.concatenate(mod_parts, axis=0), "gather_mod").reshape(N_DEV, 2, 16, ns_mod)[0::2]
    mod_full = modg.transpose(1, 2, 0, 3).reshape(2, 16, N_SHARD * ns_mod)
    mods = []
    for l in (0, 1):
        lat = lax.dynamic_index_in_dim(mod_full[l], me, axis=0, keepdims=False)
        cx = mod_full[l, 8]
        mods.append({nm: jnp.stack([cx[j * d:(j + 1) * d], lat[j * d:(j + 1) * d]])[:, None, :]
                     for j, nm in enumerate(MOD_NAMES)})

    layers = [dict(norm1_g=a[f"l{l}_norm1_g"][None], norm2_g=a[f"l{l}_norm2_g"][None],
                   q_g=a[f"l{l}_q_norm_g"][None], k_g=a[f"l{l}_k_norm_g"][None]) for l in (0, 1)]
    layers[0].update(conv_w=conv_w, sink=jnp.zeros((Q_HEADS,), F32))
    layers[1].update(sink=l1_sink, pool_w=l1_pool_w, pool_scale=l1_pool_scale[None])

    ctx_len = ctx.shape[1]
    h = jnp.concatenate([ctx[0], x[0]], axis=0)
    tabs = _rope_tables(h.shape[0], ctx_len)
    h, saved0 = _layer_fwd(h, layers[0], weight_of(0), mods[0], tabs, ctx_len, False, "l0_")
    h, saved1 = _layer_fwd(h, layers[1], weight_of(1), mods[1], tabs, ctx_len, True, "l1_", forward_staged)
    loss_tile, dh, dfg = final_loss(h, final_norm_g[None], loss_target[0])

    keys = [[f"l{l}_{nm}" for nm in BIG_NAMES] for l in (0, 1)]
    n_big = len(BIG_NAMES)
    dh, ffn_big, ffn_small, ffn_mod = _layer_bwd_ffn(dh, saved1, layers[1], mods[1], "l1_")
    dh, mix_big, mix_small, mix_mod = _layer_bwd_mixer(dh, saved1, layers[1], mods[1], tabs, ctx_len, True, "l1_")
    bigs, smalls, dmods = [None, {**ffn_big, **mix_big}], [None, {**ffn_small, **mix_small}], [None, None]
    dmods[1] = _dmod_rows(mix_mod, ffn_mod)
    grads1 = [bigs[1][nm] for nm in BIG_NAMES]
    lands = [lax.empty((g.shape[0], g.shape[1] // 2, g.shape[2]), g.dtype) for g in grads1]
    send_a, recv_a, fly_a, token_a = split_start("l1_sibling_halves_start", sibling_half_copies(n_big), n_big,
                                                 grads1 + lands, dh)
    dh, ffn_big, ffn_small, ffn_mod = _layer_bwd_ffn(dh, saved0, layers[0], _tie(mods[0], "gate2", token_a), "l0_")
    fly_a = split_wait("l1_sibling_halves_wait", sibling_half_copies(n_big), send_a, recv_a, fly_a, dh)
    chip_sums1 = [add_own_half(g, r, ci, k + "_add_halves")
                  for k, g, r in zip(keys[1], fly_a[:n_big], fly_a[n_big:])]
    lands = [lax.empty((3,) + s.shape[1:], s.dtype) for s in chip_sums1]
    send_b, recv_b, fly_b, token_b = split_start("l1_chip_quarters_start", chip_quarter_copies(n_big), 3 * n_big,
                                                 chip_sums1 + lands, dh)
    ffn_names = ("w_gate", "w_up", "w_down")
    n_ffn = len(ffn_names)
    ffn_keys = ["l0_" + nm for nm in ffn_names]
    grads0f = [ffn_big[nm] for nm in ffn_names]
    lands = [lax.empty((g.shape[0], g.shape[1] // 2, g.shape[2]), g.dtype) for g in grads0f]
    send_c, recv_c, fly_c, token_c = split_start("l0_ffn_sibling_halves_start", sibling_half_copies(n_ffn), n_ffn,
                                                 grads0f + lands, token_b)
    ffn_flight = []

    def between(dq):
        landed = split_wait("l0_ffn_sibling_halves_wait", sibling_half_copies(n_ffn), send_c, recv_c, fly_c, dq)
        sums = [add_own_half(g, r, ci, k + "_add_halves") for k, g, r in zip(ffn_keys, landed[:n_ffn], landed[n_ffn:])]
        slots = [lax.empty((3,) + s.shape[1:], s.dtype) for s in sums]
        ffn_flight.extend(split_start("l0_ffn_chip_quarters_start", chip_quarter_copies(n_ffn), 3 * n_ffn,
                                      sums + slots, dq))
        return ffn_flight[3]

    dh, mix_big, mix_small, mix_mod = _layer_bwd_mixer(dh, saved0, layers[0], _tie(mods[0], "gate1", token_c), tabs,
                                                       ctx_len, False, "l0_", between)
    fly_b = split_wait("l1_chip_quarters_wait", chip_quarter_copies(n_big), send_b, recv_b, fly_b, dh)
    fly_d = split_wait("l0_ffn_chip_quarters_wait", chip_quarter_copies(n_ffn), ffn_flight[0], ffn_flight[1],
                       ffn_flight[2], dh)
    bigs[0], smalls[0], dmods[0] = {**ffn_big, **mix_big}, {**ffn_small, **mix_small}, _dmod_rows(mix_mod, ffn_mod)
    dx = dh[ctx_len:]
    chip_sums = dict(zip(keys[1] + ffn_keys, list(fly_b[:n_big]) + list(fly_d[:n_ffn])))
    quarters = dict(zip(keys[1] + ffn_keys, list(fly_b[n_big:]) + list(fly_d[n_ffn:])))

    partial = {"l0_conv_w": smalls[0]["conv_w"], "l1_sink": smalls[1]["sink"], "l1_pool_w": smalls[1]["pool_w"],
               "l1_pool_scale": smalls[1]["pool_scale"], "final_norm_g": dfg}
    for l in (0, 1):
        partial.update({f"l{l}_dmod_lat": dmods[l][1], f"l{l}_dmod_ctx": dmods[l][0],
                        f"l{l}_b_mod": dmods[l][0] + dmods[l][1], f"l{l}_norm1_g": smalls[l]["norm1_g"],
                        f"l{l}_norm2_g": smalls[l]["norm2_g"], f"l{l}_q_norm_g": smalls[l]["q_g"],
                        f"l{l}_k_norm_g": smalls[l]["k_g"]})
    sizes = [int(np.prod(partial[name].shape)) for name, _ in SMALL_SLOTS]
    offs = _offsets(sizes)

    last_keys = ["l0_w_in", "l0_w_out"]
    grads_last = [bigs[0]["w_in"], bigs[0]["w_out"]]
    lands = [lax.empty((g.shape[0], g.shape[1] // 2, g.shape[2]), g.dtype) for g in grads_last]
    send_e, recv_e, fly_e, token_e = split_start("l0_last_sibling_halves_start", sibling_half_copies(2), 2,
                                                 grads_last + lands, dh)
    gpack = _pack(partial, sizes) + token_e[0:1, 0:1]
    rows = gpack.shape[0]
    small_all = all_gather8(gpack, "gather_small").reshape(N_DEV, rows, 128)
    packs = [_pack({name: a[pre + par] for name, par in SMALL_SLOTS if par}, sizes) for pre in ("", "m_", "v_")]
    small_out = [t.reshape(-1) for t in small_update(small_all, *packs)]
    flat_all = small_all.reshape(N_DEV, rows * 128)

    def slot(flat, name, shape):
        o, size = offs[name]
        return flat[o:o + size].reshape(shape)

    results = {}
    for name, par in SMALL_SLOTS:
        if par:
            results[par] = tuple(slot(t, name, a[par].shape) for t in small_out)

    ctx_rows = []
    for l in (0, 1):
        o_lat, o_ctx = offs[f"l{l}_dmod_lat"][0], offs[f"l{l}_dmod_ctx"][0]
        lat = lax.dynamic_slice(flat_all, (0, o_lat + p * ns_mod), (N_DEV, ns_mod))
        cxr = lax.dynamic_slice(flat_all, (0, o_ctx + p * ns_mod), (N_DEV, ns_mod))
        ctx_rows.append(cxr)
        results[f"l{l}_w_mod"] = tuple(wmod_update(c16_bwd, jnp.concatenate([lat, cxr], axis=0), a[f"l{l}_w_mod"],
                                                   a[f"m_l{l}_w_mod"], a[f"v_l{l}_w_mod"], f"l{l}_w_mod_update"))
    part = cctx_partial(ctx_rows[0], l0_w_mod, ctx_rows[1], l1_w_mod)
    parts4 = all_gather8(part, "gather_cctx").reshape(N_DEV, 8, d)[0::2, 0]
    results["c_ctx"] = tuple(t[0] for t in cctx_update(parts4, c_ctx[None], m_c_ctx[None], v_c_ctx[None]))

    gconv = lax.dynamic_slice(slot(small_out[0], "l0_conv_w", (3, N_SHARD * conv_cols)), (0, p * conv_cols),
                              (3, conv_cols))
    results["l0_conv_w"] = (gconv,) + tuple(adamw(l0_conv_w, gconv, m_l0_conv_w, v_l0_conv_w, "l0_conv_w_adamw"))

    fly_e = split_wait("l0_last_sibling_halves_wait", sibling_half_copies(2), send_e, recv_e, fly_e,
                       results["c_ctx"][0])
    sums_last = [add_own_half(g, r, ci, k + "_add_halves") for k, g, r in zip(last_keys, fly_e[:2], fly_e[2:])]
    slots = [lax.empty((3,) + s.shape[1:], s.dtype) for s in sums_last]
    send_f, recv_f, fly_f, token_f = split_start("l0_last_chip_quarters_start", chip_quarter_copies(2), 6,
                                                 sums_last + slots, sums_last[0])

    def finish(names, tag, deps):
        halves = [add_quarters(chip_sums[k], quarters[k], p, ci, k + "_add_quarters", deps) for k in names]
        for k, g in zip(names, exchange_final_halves(halves, "exchange_final_halves_" + tag)):
            results[k] = (g,) + tuple(adamw(a[k], g, a["m_" + k], a["v_" + k], k + "_adamw"))

    done_keys = keys[1] + ffn_keys
    finish(done_keys, "first", (token_f,))
    fly_f = split_wait("l0_last_chip_quarters_wait", chip_quarter_copies(2), send_f, recv_f, fly_f,
                       results[done_keys[-1]][1])
    chip_sums.update(zip(last_keys, fly_f[:2]))
    quarters.update(zip(last_keys, fly_f[2:]))
    finish(last_keys, "last", ())

    loss = lax.psum(loss_tile[0, 0], ("x", "y", "c"))
    out = [loss, dx[None]]
    for j in range(4):
        out += [results[k][j] for k in WEIGHT_NAMES]
    return tuple(out)
```

```python
import numpy as np
import jax
import jax.numpy as jnp
from jax import lax
from jax.experimental import pallas as pl
from jax.experimental.pallas import tpu as pltpu

F32 = jnp.float32
BF16 = jnp.bfloat16
MESH = pl.DeviceIdType.MESH

HEAD_DIM = 128
Q_HEADS = 12
KV_HEADS = 4
GROUPS = Q_HEADS // KV_HEADS
ATT_WIDTH = Q_HEADS * HEAD_DIM
KV_WIDTH = KV_HEADS * HEAD_DIM
QKV_WIDTH = ATT_WIDTH + 2 * KV_WIDTH
AUX_WIDTH = 512
AUX_GROUPS = 4
POOL_WINDOWS = (2, 4, 8, 16)
WINDOW = 128
GRID_W = 64
ROPE_THETA = 10000.0
N_MOD = 6
EPS = 1e-6
NEG_INF = -1e30
ATT_SCALE = HEAD_DIM ** -0.5
N_SHARD = 4
N_DEV = 8

ADAM_LR = 0.001
ADAM_B1 = 0.9
ADAM_B2 = 0.999
ADAM_EPS = 1e-08
ADAM_WD = 0.01
ADAM_STEP = 10

TM = 256
BAND = TM + 2 * WINDOW
VMEM_LIMIT = 56 * 1024 * 1024

NT_DIMS = (((1,), (1,)), ((), ()))
TN_DIMS = (((0,), (0,)), ((), ()))


def _pcall(body, **kw):
    return pl.pallas_call(body, **kw)


def _params(*sem):
    return pltpu.CompilerParams(dimension_semantics=sem, vmem_limit_bytes=VMEM_LIMIT)


def _sel(i):
    return jnp.minimum(i, 1)


def _sel_spec(d):
    return pl.BlockSpec((None, 1, d), lambda i: (_sel(i), 0, 0))


def _dot(a, b):
    return jnp.dot(a, b, preferred_element_type=F32)


def _dot_nt(a, b):
    return lax.dot_general(a, b, NT_DIMS, preferred_element_type=F32)


def _dot_tn(a, b):
    return lax.dot_general(a, b, TN_DIMS, preferred_element_type=F32)


def _sigmoid(x):
    return 1.0 / (1.0 + jnp.exp(-x))


def norm_mod_fwd(h, g, shift, scale, name):
    t, d = h.shape

    def body(h_ref, g_ref, sh_ref, sc_ref, o_ref):
        x = h_ref[...]
        r = lax.rsqrt(jnp.mean(x * x, axis=-1, keepdims=True) + EPS)
        y = x * r * g_ref[...]
        o_ref[...] = (y * (1.0 + sc_ref[...]) + sh_ref[...]).astype(BF16)

    return _pcall(
        body, name=name, grid=(t // TM,),
        in_specs=[pl.BlockSpec((TM, d), lambda i: (i, 0)), pl.BlockSpec((1, d), lambda i: (0, 0)),
                  _sel_spec(d), _sel_spec(d)],
        out_specs=pl.BlockSpec((TM, d), lambda i: (i, 0)),
        out_shape=jax.ShapeDtypeStruct((t, d), BF16),
        compiler_params=_params("parallel"),
    )(h, g, shift, scale)


def norm_mod_bwd(dy, h, g, scale, dres, name):
    t, d = h.shape

    def body(dy_ref, h_ref, g_ref, sc_ref, dres_ref, dh_ref, dsh_ref, dsc_ref, dg_ref):
        i = pl.program_id(0)

        @pl.when(i == 0)
        def _():
            dsh_ref[...] = jnp.zeros_like(dsh_ref)
            dsc_ref[...] = jnp.zeros_like(dsc_ref)
            dg_ref[...] = jnp.zeros_like(dg_ref)

        x = h_ref[...]
        gv = g_ref[...]
        r = lax.rsqrt(jnp.mean(x * x, axis=-1, keepdims=True) + EPS)
        xhat = x * r
        dyv = dy_ref[...]
        s = _sel(i)
        dsh_ref[pl.ds(s, 1)] += jnp.sum(dyv, axis=0, keepdims=True)[None]
        dsc_ref[pl.ds(s, 1)] += jnp.sum(dyv * xhat * gv, axis=0, keepdims=True)[None]
        tt = dyv * (1.0 + sc_ref[...])
        dg_ref[...] += jnp.sum(tt * xhat, axis=0, keepdims=True)
        dxhat = tt * gv
        dx = r * (dxhat - xhat * jnp.mean(dxhat * xhat, axis=-1, keepdims=True))
        dh_ref[...] = dx + dres_ref[...]

    row = pl.BlockSpec((TM, d), lambda i: (i, 0))
    acc2 = pl.BlockSpec((2, 1, d), lambda i: (0, 0, 0))
    return _pcall(
        body, name=name, grid=(t // TM,),
        in_specs=[row, row, pl.BlockSpec((1, d), lambda i: (0, 0)), _sel_spec(d), row],
        out_specs=[row, acc2, acc2, pl.BlockSpec((1, d), lambda i: (0, 0))],
        out_shape=[jax.ShapeDtypeStruct((t, d), F32), jax.ShapeDtypeStruct((2, 1, d), F32),
                   jax.ShapeDtypeStruct((2, 1, d), F32), jax.ShapeDtypeStruct((1, d), F32)],
        compiler_params=_params("arbitrary"),
    )(dy, h, g, scale, dres)


def gate_bwd(dh, f, gate, name):
    t, d = dh.shape

    def body(dh_ref, f_ref, gt_ref, df_ref, dgt_ref):
        i = pl.program_id(0)

        @pl.when(i == 0)
        def _():
            dgt_ref[...] = jnp.zeros_like(dgt_ref)

        dhv = dh_ref[...]
        df_ref[...] = (dhv * gt_ref[...]).astype(BF16)
        dgt_ref[pl.ds(_sel(i), 1)] += jnp.sum(dhv * f_ref[...], axis=0, keepdims=True)[None]

    row = pl.BlockSpec((TM, d), lambda i: (i, 0))
    return _pcall(
        body, name=name, grid=(t // TM,),
        in_specs=[row, row, _sel_spec(d)],
        out_specs=[row, pl.BlockSpec((2, 1, d), lambda i: (0, 0, 0))],
        out_shape=[jax.ShapeDtypeStruct((t, d), BF16), jax.ShapeDtypeStruct((2, 1, d), F32)],
        compiler_params=_params("arbitrary"),
    )(dh, f, gate)


def final_loss(h, fg, target):
    t, d = h.shape

    def body(h_ref, g_ref, tg_ref, loss_ref, dh_ref, dg_ref):
        i = pl.program_id(0)

        @pl.when(i == 0)
        def _():
            loss_ref[...] = jnp.zeros_like(loss_ref)
            dg_ref[...] = jnp.zeros_like(dg_ref)
            dh_ref[...] = jnp.zeros_like(dh_ref)

        @pl.when(i > 0)
        def _():
            x = h_ref[...]
            gv = g_ref[...]
            r = lax.rsqrt(jnp.mean(x * x, axis=-1, keepdims=True) + EPS)
            xhat = x * r
            diff = xhat * gv - tg_ref[...]
            loss_ref[...] += 0.5 * jnp.sum(jnp.mean(diff * diff, axis=-1, keepdims=True), axis=0, keepdims=True)
            dout = diff * (1.0 / d)
            dg_ref[...] += jnp.sum(dout * xhat, axis=0, keepdims=True)
            dxhat = dout * gv
            dh_ref[...] = r * (dxhat - xhat * jnp.mean(dxhat * xhat, axis=-1, keepdims=True))

    row = pl.BlockSpec((TM, d), lambda i: (i, 0))
    return _pcall(
        body, name="final_loss", grid=(t // TM,),
        in_specs=[row, pl.BlockSpec((1, d), lambda i: (0, 0)),
                  pl.BlockSpec((TM, d), lambda i: (jnp.maximum(i - 1, 0), 0))],
        out_specs=[pl.BlockSpec((8, 128), lambda i: (0, 0)), row, pl.BlockSpec((1, d), lambda i: (0, 0))],
        out_shape=[jax.ShapeDtypeStruct((8, 128), F32), jax.ShapeDtypeStruct((t, d), F32),
                   jax.ShapeDtypeStruct((1, d), F32)],
        compiler_params=_params("arbitrary"),
    )(h, fg, target)


def _w_nn_spec(w, k):
    if w.shape[1] == k:
        ns = w.shape[2]
        return w, pl.BlockSpec((None, k, ns), lambda j, i: (j, 0, 0)), w.shape[0], ns
    w2 = w.reshape(w.shape[0] * w.shape[1], w.shape[2])
    tn = 1024 if w2.shape[1] % 1024 == 0 else w2.shape[1]
    return w2, pl.BlockSpec((k, tn), lambda j, i: (0, j)), w2.shape[1] // tn, tn


def mm_nn(a, w, name, out_dtype=F32):
    t, k = a.shape
    w, wspec, nb, tn = _w_nn_spec(w, k)

    def body(a_ref, w_ref, o_ref):
        o_ref[...] = _dot(a_ref[...], w_ref[...]).astype(o_ref.dtype)

    return _pcall(
        body, name=name, grid=(nb, t // TM),
        in_specs=[pl.BlockSpec((TM, k), lambda j, i: (i, 0)), wspec],
        out_specs=pl.BlockSpec((TM, tn), lambda j, i: (i, j)),
        out_shape=jax.ShapeDtypeStruct((t, nb * tn), out_dtype),
        compiler_params=_params("parallel", "parallel"),
    )(a, w)


def mm_nn_residual(a, w, h, gate, name):
    t, k = a.shape
    w, wspec, nb, tn = _w_nn_spec(w, k)

    def body(a_ref, w_ref, h_ref, gt_ref, y_ref, o_ref):
        y = _dot(a_ref[...], w_ref[...])
        y_ref[...] = y
        o_ref[...] = h_ref[...] + gt_ref[...] * y

    out = pl.BlockSpec((TM, tn), lambda j, i: (i, j))
    return _pcall(
        body, name=name, grid=(nb, t // TM),
        in_specs=[pl.BlockSpec((TM, k), lambda j, i: (i, 0)), wspec, out,
                  pl.BlockSpec((None, 1, tn), lambda j, i: (_sel(i), 0, j))],
        out_specs=[out, out],
        out_shape=[jax.ShapeDtypeStruct((t, nb * tn), F32)] * 2,
        compiler_params=_params("parallel", "parallel"),
    )(a, w, h, gate)


def mm_swiglu(a, wg, wu, name):
    t, k = a.shape
    s, _, ns = wg.shape

    def body(a_ref, wg_ref, wu_ref, g_ref, u_ref, act_ref):
        av = a_ref[...]
        g = _dot(av, wg_ref[...])
        u = _dot(av, wu_ref[...])
        g_ref[...] = g
        u_ref[...] = u
        act_ref[...] = (g * _sigmoid(g) * u).astype(BF16)

    wspec = pl.BlockSpec((None, k, ns), lambda j, i: (j, 0, 0))
    out = pl.BlockSpec((TM, ns), lambda j, i: (i, j))
    return _pcall(
        body, name=name, grid=(s, t // TM),
        in_specs=[pl.BlockSpec((TM, k), lambda j, i: (i, 0)), wspec, wspec],
        out_specs=[out, out, out],
        out_shape=[jax.ShapeDtypeStruct((t, s * ns), F32), jax.ShapeDtypeStruct((t, s * ns), F32),
                   jax.ShapeDtypeStruct((t, s * ns), BF16)],
        compiler_params=_params("parallel", "parallel"),
    )(a, wg, wu)


def mm_nt_cols(dy, w, name, add=None):
    t, n = dy.shape
    s, k, ns = w.shape
    tk = 512

    def body(*refs):
        if add is None:
            dy_ref, w_ref, o_ref = refs
        else:
            dy_ref, w_ref, add_ref, o_ref = refs
        acc = _dot_nt(dy_ref[:, 0:ns], w_ref[0])
        for sh in range(1, s):
            acc += _dot_nt(dy_ref[:, sh * ns:(sh + 1) * ns], w_ref[sh])
        if add is not None:
            acc += add_ref[...]
        o_ref[...] = acc

    out = pl.BlockSpec((TM, tk), lambda j, i: (i, j))
    in_specs = [pl.BlockSpec((TM, n), lambda j, i: (i, 0)), pl.BlockSpec((s, tk, ns), lambda j, i: (0, j, 0))]
    args = [dy, w]
    if add is not None:
        in_specs.append(out)
        args.append(add)
    return _pcall(
        body, name=name, grid=(k // tk, t // TM),
        in_specs=in_specs, out_specs=out,
        out_shape=jax.ShapeDtypeStruct((t, k), F32),
        compiler_params=_params("parallel", "parallel"),
    )(*args)


def mm_nt_rows(dy, w, name):
    t, n = dy.shape
    s, ks, _ = w.shape

    def body(dy_ref, w_ref, o_ref):
        o_ref[...] = _dot_nt(dy_ref[...], w_ref[...])

    return _pcall(
        body, name=name, grid=(s, t // TM),
        in_specs=[pl.BlockSpec((TM, n), lambda j, i: (i, 0)), pl.BlockSpec((None, ks, n), lambda j, i: (j, 0, 0))],
        out_specs=pl.BlockSpec((TM, ks), lambda j, i: (i, j)),
        out_shape=jax.ShapeDtypeStruct((t, s * ks), F32),
        compiler_params=_params("parallel", "parallel"),
    )(dy, w)


def mm_nt_swiglu_bwd(df, wd, g, u, name):
    t, n = df.shape
    s, ks, _ = wd.shape

    def body(df_ref, w_ref, g_ref, u_ref, dg_ref, du_ref):
        da = _dot_nt(df_ref[...], w_ref[...])
        gv = g_ref[...]
        sg = _sigmoid(gv)
        silu = gv * sg
        dg_ref[...] = (da * u_ref[...] * (sg * (1.0 + gv * (1.0 - sg)))).astype(BF16)
        du_ref[...] = (da * silu).astype(BF16)

    blk = pl.BlockSpec((TM, ks), lambda j, i: (i, j))
    return _pcall(
        body, name=name, grid=(s, t // TM),
        in_specs=[pl.BlockSpec((TM, n), lambda j, i: (i, 0)), pl.BlockSpec((None, ks, n), lambda j, i: (j, 0, 0)),
                  blk, blk],
        out_specs=[blk, blk],
        out_shape=[jax.ShapeDtypeStruct((t, s * ks), BF16)] * 2,
        compiler_params=_params("parallel", "parallel"),
    )(df, wd, g, u)


def mm_tn(x, dy, name, col_sharded):
    t, k = x.shape
    n = dy.shape[1]
    tk = 512
    mc = t // 2 if (t // 2) % 128 == 0 else t
    if col_sharded:
        tn = n // N_SHARD
        out_shape = jax.ShapeDtypeStruct((N_SHARD, k, tn), F32)
        out_spec = pl.BlockSpec((None, tk, tn), lambda j, kb, m: (j, kb, 0))
    else:
        tn = 1024
        out_shape = jax.ShapeDtypeStruct((k, n), F32)
        out_spec = pl.BlockSpec((tk, tn), lambda j, kb, m: (kb, j))

    def body(x_ref, dy_ref, o_ref):
        m = pl.program_id(2)
        acc = _dot_tn(x_ref[...], dy_ref[...])

        @pl.when(m == 0)
        def _():
            o_ref[...] = acc

        @pl.when(m > 0)
        def _():
            o_ref[...] += acc

    out = _pcall(
        body, name=name, grid=(n // tn, k // tk, t // mc),
        in_specs=[pl.BlockSpec((mc, tk), lambda j, kb, m: (m, kb)), pl.BlockSpec((mc, tn), lambda j, kb, m: (m, j))],
        out_specs=out_spec, out_shape=out_shape,
        compiler_params=_params("parallel", "parallel", "arbitrary"),
    )(x, dy)
    return out if col_sharded else out.reshape(N_SHARD, k // N_SHARD, n)


def _swap32(y):
    right = pltpu.roll(y, 32, 1)
    left = pltpu.roll(y, 96, 1)
    lane = lax.broadcasted_iota(jnp.int32, y.shape, 1)
    return jnp.where((lane // 32) % 2 == 0, left, right)


def _rope_tables(t, ctx_len):
    s = t - ctx_len
    pos = np.arange(s)
    row = (pos // GRID_W).astype(np.float32)
    col = (pos % GRID_W).astype(np.float32)
    half = HEAD_DIM // 2
    inv = np.power(np.float32(ROPE_THETA), -np.arange(0, half, 2, dtype=np.float32) / np.float32(half))
    ar = row[:, None] * inv
    ac = col[:, None] * inv
    cos = np.concatenate([np.cos(ar), np.cos(ar), np.cos(ac), np.cos(ac)], axis=1)
    sin = np.concatenate([-np.sin(ar), np.sin(ar), -np.sin(ac), np.sin(ac)], axis=1)
    cos = np.concatenate([np.ones((ctx_len, HEAD_DIM), np.float32), cos.astype(np.float32)], axis=0)
    sin = np.concatenate([np.zeros((ctx_len, HEAD_DIM), np.float32), sin.astype(np.float32)], axis=0)
    return jnp.asarray(cos, F32), jnp.asarray(sin, F32)


def qk_prep_fwd(proj, qg, kg, cos, sin, name):
    t = proj.shape[0]

    def body(p_ref, qg_ref, kg_ref, c_ref, s_ref, q_ref, k_ref, v_ref):
        cv = c_ref[...]
        sv = s_ref[...]
        for hd in range(Q_HEADS + KV_HEADS):
            x = p_ref[:, hd * HEAD_DIM:(hd + 1) * HEAD_DIM]
            gv = qg_ref[...] if hd < Q_HEADS else kg_ref[...]
            y = x * lax.rsqrt(jnp.mean(x * x, axis=-1, keepdims=True) + EPS) * gv
            out = (y * cv + _swap32(y) * sv).astype(BF16)
            if hd < Q_HEADS:
                q_ref[:, hd * HEAD_DIM:(hd + 1) * HEAD_DIM] = out
            else:
                k_ref[:, (hd - Q_HEADS) * HEAD_DIM:(hd - Q_HEADS + 1) * HEAD_DIM] = out
        v_ref[...] = p_ref[:, ATT_WIDTH + KV_WIDTH:QKV_WIDTH].astype(BF16)

    vec = pl.BlockSpec((1, HEAD_DIM), lambda i: (0, 0))
    tab = pl.BlockSpec((TM, HEAD_DIM), lambda i: (i, 0))
    return _pcall(
        body, name=name, grid=(t // TM,),
        in_specs=[pl.BlockSpec((TM, QKV_WIDTH), lambda i: (i, 0)), vec, vec, tab, tab],
        out_specs=[pl.BlockSpec((TM, ATT_WIDTH), lambda i: (i, 0)), pl.BlockSpec((TM, KV_WIDTH), lambda i: (i, 0)),
                   pl.BlockSpec((TM, KV_WIDTH), lambda i: (i, 0))],
        out_shape=[jax.ShapeDtypeStruct((t, ATT_WIDTH), BF16), jax.ShapeDtypeStruct((t, KV_WIDTH), BF16),
                   jax.ShapeDtypeStruct((t, KV_WIDTH), BF16)],
        compiler_params=_params("parallel"),
    )(proj, qg, kg, cos, sin)


def qk_prep_bwd(dq, dk, dv, daux, proj, qg, kg, cos, sin, name):
    t, n_in = proj.shape
    n_aux = daux.shape[0]

    def body(dq_ref, dk_ref, dv_ref, da_ref, p_ref, qg_ref, kg_ref, c_ref, s_ref, o_ref, dqg_ref, dkg_ref):
        @pl.when(pl.program_id(0) == 0)
        def _():
            dqg_ref[...] = jnp.zeros_like(dqg_ref)
            dkg_ref[...] = jnp.zeros_like(dkg_ref)

        cv = c_ref[...]
        sv = s_ref[...]
        for hd in range(Q_HEADS + KV_HEADS):
            cols = slice(hd * HEAD_DIM, (hd + 1) * HEAD_DIM)
            x = p_ref[:, cols]
            if hd < Q_HEADS:
                gv, dyr, dg_ref = qg_ref[...], dq_ref[:, cols], dqg_ref
            else:
                kc = slice((hd - Q_HEADS) * HEAD_DIM, (hd - Q_HEADS + 1) * HEAD_DIM)
                gv, dyr, dg_ref = kg_ref[...], dk_ref[:, kc], dkg_ref
            r = lax.rsqrt(jnp.mean(x * x, axis=-1, keepdims=True) + EPS)
            xhat = x * r
            dy = dyr * cv + _swap32(dyr * sv)
            dg_ref[...] += jnp.sum(dy * xhat, axis=0, keepdims=True)
            dxhat = dy * gv
            o_ref[:, cols] = (r * (dxhat - xhat * jnp.mean(dxhat * xhat, axis=-1, keepdims=True))).astype(BF16)
        o_ref[:, ATT_WIDTH + KV_WIDTH:QKV_WIDTH] = dv_ref[...].astype(BF16)
        for a in range(n_aux):
            o_ref[:, QKV_WIDTH + a * AUX_WIDTH:QKV_WIDTH + (a + 1) * AUX_WIDTH] = da_ref[a]

    vec = pl.BlockSpec((1, HEAD_DIM), lambda i: (0, 0))
    tab = pl.BlockSpec((TM, HEAD_DIM), lambda i: (i, 0))
    return _pcall(
        body, name=name, grid=(t // TM,),
        in_specs=[pl.BlockSpec((TM, ATT_WIDTH), lambda i: (i, 0)), pl.BlockSpec((TM, KV_WIDTH), lambda i: (i, 0)),
                  pl.BlockSpec((TM, KV_WIDTH), lambda i: (i, 0)),
                  pl.BlockSpec((n_aux, TM, AUX_WIDTH), lambda i: (0, i, 0)),
                  pl.BlockSpec((TM, QKV_WIDTH), lambda i: (i, 0)), vec, vec, tab, tab],
        out_specs=[pl.BlockSpec((TM, n_in), lambda i: (i, 0)), vec, vec],
        out_shape=[jax.ShapeDtypeStruct((t, n_in), BF16), jax.ShapeDtypeStruct((1, HEAD_DIM), F32),
                   jax.ShapeDtypeStruct((1, HEAD_DIM), F32)],
        compiler_params=_params("arbitrary"),
    )(dq, dk, dv, daux, proj, qg, kg, cos, sin)


def _band_start(i, t):
    return pl.multiple_of(jnp.clip(i * TM - WINDOW, 0, t - BAND), WINDOW)


def _band_mask(i, start, ctx_len):
    shape = (TM, ctx_len + BAND)
    col = lax.broadcasted_iota(jnp.int32, shape, 1)
    qrow = i * TM + lax.broadcasted_iota(jnp.int32, shape, 0)
    krow = start + col - ctx_len
    band_ok = (krow >= ctx_len) & (jnp.abs(krow - qrow) <= WINDOW)
    return (col < ctx_len) | band_ok


def attention_fwd(q, k, v, sink, ctx_len, windowed, name):
    t = q.shape[0]
    d_model = ATT_WIDTH + AUX_WIDTH
    gw = GROUPS * HEAD_DIM

    def one_head(qh, kk, vv, mask, sink_val):
        s = _dot_nt(qh, kk) * ATT_SCALE
        if mask is not None:
            s = jnp.where(mask, s, NEG_INF)
        m = jnp.max(s, axis=-1, keepdims=True)
        if sink_val is not None:
            m = jnp.maximum(m, sink_val)
        p = jnp.exp(s - m)
        l = jnp.sum(p, axis=-1, keepdims=True)
        if sink_val is not None:
            l = l + jnp.exp(sink_val - m)
        o = _dot(p.astype(BF16), vv) / l
        return o, m + jnp.log(l)

    def body(sink_ref, q_ref, k_ref, v_ref, o_ref, lse_ref):
        kv = pl.program_id(0)
        i = pl.program_id(1)

        def run(kk, vv, mask):
            for g in range(GROUPS):
                sink_val = sink_ref[kv * GROUPS + g] if windowed else None
                o, lse = one_head(q_ref[:, g * HEAD_DIM:(g + 1) * HEAD_DIM], kk, vv, mask, sink_val)
                o_ref[:, g * HEAD_DIM:(g + 1) * HEAD_DIM] = o.astype(BF16)
                lse_ref[g] = lse

        @pl.when(i == 0)
        def _():
            if windowed:
                o_ref[...] = jnp.zeros_like(o_ref)
                lse_ref[...] = jnp.zeros_like(lse_ref)
            else:
                run(k_ref[0:ctx_len], v_ref[0:ctx_len], None)

        @pl.when(i > 0)
        def _():
            if windowed:
                start = _band_start(i, t)
                kk = jnp.concatenate([k_ref[0:ctx_len], k_ref[pl.ds(start, BAND)]], axis=0)
                vv = jnp.concatenate([v_ref[0:ctx_len], v_ref[pl.ds(start, BAND)]], axis=0)
                run(kk, vv, _band_mask(i, start, ctx_len))
            else:
                run(k_ref[...], v_ref[...], None)

    kvspec = pl.BlockSpec((t, HEAD_DIM), lambda kv, i: (0, kv))
    return _pcall(
        body, name=name, grid=(KV_HEADS, t // TM),
        in_specs=[pl.BlockSpec(memory_space=pltpu.SMEM), pl.BlockSpec((TM, gw), lambda kv, i: (i, kv)), kvspec, kvspec],
        out_specs=[pl.BlockSpec((TM, gw), lambda kv, i: (i, kv)),
                   pl.BlockSpec((GROUPS, TM, 1), lambda kv, i: (kv, i, 0))],
        out_shape=[jax.ShapeDtypeStruct((t, d_model), BF16), jax.ShapeDtypeStruct((Q_HEADS, t, 1), F32)],
        compiler_params=_params("parallel", "parallel"),
    )(sink, q, k, v)


def attention_bwd(q, k, v, cat, lse, dcat, sink, ctx_len, windowed, name):
    t = q.shape[0]
    gw = GROUPS * HEAD_DIM

    def body(sink_ref, q_ref, k_ref, v_ref, o_ref, lse_ref, do_ref, dq_ref, dk_ref, dv_ref, dsink_ref):
        kv = pl.program_id(0)
        i = pl.program_id(1)

        @pl.when(i == 0)
        def _():
            dk_ref[...] = jnp.zeros_like(dk_ref)
            dv_ref[...] = jnp.zeros_like(dv_ref)
            dsink_ref[...] = jnp.zeros_like(dsink_ref)

        def run(kk, vv, mask, accumulate):
            for g in range(GROUPS):
                cols = slice(g * HEAD_DIM, (g + 1) * HEAD_DIM)
                qh = q_ref[:, cols]
                doh = do_ref[:, cols]
                delta = jnp.sum(doh * o_ref[:, cols].astype(F32), axis=-1, keepdims=True)
                lse_g = lse_ref[g]
                s = _dot_nt(qh, kk) * ATT_SCALE
                if mask is not None:
                    s = jnp.where(mask, s, NEG_INF)
                p = jnp.exp(s - lse_g)
                dob = doh.astype(BF16)
                dp = _dot_nt(dob, vv)
                ds = (p * (dp - delta) * ATT_SCALE).astype(BF16)
                dq_ref[:, cols] = _dot(ds, kk)
                accumulate(_dot_tn(ds, qh), _dot_tn(p.astype(BF16), dob))
                if windowed:
                    p_sink = jnp.exp(sink_ref[kv * GROUPS + g] - lse_g)
                    dsink_ref[g:g + 1, :] += jnp.sum(-p_sink * delta, axis=0, keepdims=True)

        @pl.when(i == 0)
        def _():
            if windowed:
                dq_ref[...] = jnp.zeros_like(dq_ref)
            else:
                def acc_ctx(dkp, dvp):
                    dk_ref[0:ctx_len] += dkp
                    dv_ref[0:ctx_len] += dvp

                run(k_ref[0:ctx_len], v_ref[0:ctx_len], None, acc_ctx)

        @pl.when(i > 0)
        def _():
            if windowed:
                start = _band_start(i, t)
                kk = jnp.concatenate([k_ref[0:ctx_len], k_ref[pl.ds(start, BAND)]], axis=0)
                vv = jnp.concatenate([v_ref[0:ctx_len], v_ref[pl.ds(start, BAND)]], axis=0)

                def acc_band(dkp, dvp):
                    dk_ref[0:ctx_len] += dkp[0:ctx_len]
                    dv_ref[0:ctx_len] += dvp[0:ctx_len]
                    dk_ref[pl.ds(start, BAND)] += dkp[ctx_len:ctx_len + BAND]
                    dv_ref[pl.ds(start, BAND)] += dvp[ctx_len:ctx_len + BAND]

                run(kk, vv, _band_mask(i, start, ctx_len), acc_band)
            else:
                def acc_all(dkp, dvp):
                    dk_ref[...] += dkp
                    dv_ref[...] += dvp

                run(k_ref[...], v_ref[...], None, acc_all)

    kvspec = pl.BlockSpec((t, HEAD_DIM), lambda kv, i: (0, kv))
    grp = pl.BlockSpec((TM, gw), lambda kv, i: (i, kv))
    return _pcall(
        body, name=name, grid=(KV_HEADS, t // TM),
        in_specs=[pl.BlockSpec(memory_space=pltpu.SMEM), grp, kvspec, kvspec, grp,
                  pl.BlockSpec((GROUPS, TM, 1), lambda kv, i: (kv, i, 0)), grp],
        out_specs=[grp, kvspec, kvspec, pl.BlockSpec((None, 8, 128), lambda kv, i: (kv, 0, 0))],
        out_shape=[jax.ShapeDtypeStruct((t, ATT_WIDTH), F32), jax.ShapeDtypeStruct((t, KV_WIDTH), F32),
                   jax.ShapeDtypeStruct((t, KV_WIDTH), F32), jax.ShapeDtypeStruct((KV_HEADS, 8, 128), F32)],
        compiler_params=_params("parallel", "arbitrary"),
    )(sink, q, k, v, cat, lse, dcat)


def _segment_pos(t, ctx_len, width):
    row = lax.broadcasted_iota(jnp.int32, (t, width), 0)
    in_ctx = row < ctx_len
    return jnp.where(in_ctx, row, row - ctx_len), jnp.where(in_ctx, ctx_len, t - ctx_len)


def _shifted(x, offset, pos, seg_len):
    t = x.shape[0]
    moved = x if offset == 0 else pltpu.roll(x, (-offset) % t, 0)
    ok = (pos + offset >= 0) & (pos + offset < seg_len)
    return jnp.where(ok, moved, 0.0)


def conv_fwd(proj, conv_w, cat, ctx_len, name):
    t = proj.shape[0]
    base = QKV_WIDTH // 128
    per = AUX_WIDTH // 128

    def body(gb_ref, gc_ref, u_ref, w_ref, cat_in, o_ref):
        del cat_in
        pos, seg = _segment_pos(t, ctx_len, 128)
        z = gc_ref[...] * u_ref[...]
        conv = (w_ref[0:1, :] * _shifted(z, -1, pos, seg) + w_ref[1:2, :] * z
                + w_ref[2:3, :] * _shifted(z, 1, pos, seg))
        o_ref[...] = (gb_ref[...] * conv).astype(BF16)

    def col(off):
        return pl.BlockSpec((t, 128), lambda j: (0, base + off * per + j))

    return _pcall(
        body, name=name, grid=(per,),
        in_specs=[col(0), col(1), col(2), pl.BlockSpec((3, 128), lambda j: (0, j)),
                  pl.BlockSpec(memory_space=pl.ANY)],
        out_specs=pl.BlockSpec((t, 128), lambda j: (0, ATT_WIDTH // 128 + j)),
        out_shape=jax.ShapeDtypeStruct(cat.shape, BF16),
        input_output_aliases={4: 0},
        compiler_params=_params("parallel"),
    )(proj, proj, proj, conv_w, cat)


def conv_bwd(dcat, proj, conv_w, ctx_len, name):
    t = proj.shape[0]
    base = QKV_WIDTH // 128
    per = AUX_WIDTH // 128

    def body(do_ref, gb_ref, gc_ref, u_ref, w_ref, da_ref, dw_ref):
        pos, seg = _segment_pos(t, ctx_len, 128)
        gc = gc_ref[...]
        u = u_ref[...]
        z = gc * u
        zm = _shifted(z, -1, pos, seg)
        zp = _shifted(z, 1, pos, seg)
        conv = w_ref[0:1, :] * zm + w_ref[1:2, :] * z + w_ref[2:3, :] * zp
        dout = do_ref[...]
        da_ref[0] = (dout * conv).astype(BF16)
        dconv = dout * gb_ref[...]
        dw_ref[0:1, :] = jnp.sum(dconv * zm, axis=0, keepdims=True)
        dw_ref[1:2, :] = jnp.sum(dconv * z, axis=0, keepdims=True)
        dw_ref[2:3, :] = jnp.sum(dconv * zp, axis=0, keepdims=True)
        dz = (w_ref[1:2, :] * dconv + w_ref[0:1, :] * _shifted(dconv, 1, pos, seg)
              + w_ref[2:3, :] * _shifted(dconv, -1, pos, seg))
        da_ref[1] = (dz * u).astype(BF16)
        da_ref[2] = (dz * gc).astype(BF16)

    def col(off):
        return pl.BlockSpec((t, 128), lambda j: (0, base + off * per + j))

    return _pcall(
        body, name=name, grid=(per,),
        in_specs=[pl.BlockSpec((t, 128), lambda j: (0, ATT_WIDTH // 128 + j)), col(0), col(1), col(2),
                  pl.BlockSpec((3, 128), lambda j: (0, j))],
        out_specs=[pl.BlockSpec((3, t, 128), lambda j: (0, 0, j)), pl.BlockSpec((3, 128), lambda j: (0, j))],
        out_shape=[jax.ShapeDtypeStruct((3, t, AUX_WIDTH), BF16), jax.ShapeDtypeStruct((3, AUX_WIDTH), F32)],
        compiler_params=_params("parallel"),
    )(dcat, proj, proj, proj, conv_w)


def _pooled(u, w, pos, seg):
    lo = jnp.clip(pos - w // 2, 0, seg)
    hi = jnp.clip(pos - w // 2 + w, 0, seg)
    inv = 1.0 / (hi - lo).astype(F32)
    total = _shifted(u, -(w // 2), pos, seg)
    for o in range(-(w // 2) + 1, w // 2):
        total = total + _shifted(u, o, pos, seg)
    return total * inv - u, inv


def pool_fwd(proj, pool_w, pool_scale, cat, ctx_len, name):
    t = proj.shape[0]

    def body(u_ref, w_ref, sc_ref, cat_in, o_ref):
        del cat_in
        pos, seg = _segment_pos(t, ctx_len, 128)
        for g, w in enumerate(POOL_WINDOWS):
            cols = slice(g * 128, (g + 1) * 128)
            pooled, _ = _pooled(u_ref[:, cols], w, pos, seg)
            mixed = _dot(pooled.astype(BF16), w_ref[g].astype(BF16))
            o_ref[:, cols] = (mixed * sc_ref[:, cols]).astype(BF16)

    return _pcall(
        body, name=name, grid=(1,),
        in_specs=[pl.BlockSpec((t, AUX_WIDTH), lambda j: (0, QKV_WIDTH // AUX_WIDTH)),
                  pl.BlockSpec((AUX_GROUPS, 128, 128), lambda j: (0, 0, 0)),
                  pl.BlockSpec((1, AUX_WIDTH), lambda j: (0, 0)), pl.BlockSpec(memory_space=pl.ANY)],
        out_specs=pl.BlockSpec((t, AUX_WIDTH), lambda j: (0, ATT_WIDTH // AUX_WIDTH)),
        out_shape=jax.ShapeDtypeStruct(cat.shape, BF16),
        input_output_aliases={3: 0},
        compiler_params=_params("arbitrary"),
    )(proj, pool_w, pool_scale, cat)


def pool_bwd(dcat, proj, pool_w, pool_scale, ctx_len, name):
    t = proj.shape[0]

    def body(do_ref, u_ref, w_ref, sc_ref, da_ref, dw_ref, dsc_ref):
        pos, seg = _segment_pos(t, ctx_len, 128)
        for g, w in enumerate(POOL_WINDOWS):
            cols = slice(g * 128, (g + 1) * 128)
            pooled, inv = _pooled(u_ref[:, cols], w, pos, seg)
            pb = pooled.astype(BF16)
            wb = w_ref[g].astype(BF16)
            mixed = _dot(pb, wb)
            dout = do_ref[:, cols]
            dsc_ref[:, cols] = jnp.sum(dout * mixed, axis=0, keepdims=True)
            dmixed = (dout * sc_ref[:, cols]).astype(BF16)
            dw_ref[g] = _dot_tn(pb, dmixed)
            dpooled = _dot_nt(dmixed, wb)
            spread = dpooled * inv
            du = _shifted(spread, w // 2, pos, seg) - dpooled
            for o in range(-(w // 2) + 1, w // 2):
                du = du + _shifted(spread, -o, pos, seg)
            da_ref[0, :, cols] = du.astype(BF16)

    return _pcall(
        body, name=name, grid=(1,),
        in_specs=[pl.BlockSpec((t, AUX_WIDTH), lambda j: (0, ATT_WIDTH // AUX_WIDTH)),
                  pl.BlockSpec((t, AUX_WIDTH), lambda j: (0, QKV_WIDTH // AUX_WIDTH)),
                  pl.BlockSpec((AUX_GROUPS, 128, 128), lambda j: (0, 0, 0)),
                  pl.BlockSpec((1, AUX_WIDTH), lambda j: (0, 0))],
        out_specs=[pl.BlockSpec((1, t, AUX_WIDTH), lambda j: (0, 0, 0)),
                   pl.BlockSpec((AUX_GROUPS, 128, 128), lambda j: (0, 0, 0)),
                   pl.BlockSpec((1, AUX_WIDTH), lambda j: (0, 0))],
        out_shape=[jax.ShapeDtypeStruct((1, t, AUX_WIDTH), BF16), jax.ShapeDtypeStruct((AUX_GROUPS, 128, 128), F32),
                   jax.ShapeDtypeStruct((1, AUX_WIDTH), F32)],
        compiler_params=_params("arbitrary"),
    )(dcat, proj, pool_w, pool_scale)


MOD_NAMES = ("shift1", "scale1", "gate1", "shift2", "scale2", "gate2")


def _layer_fwd(h, lw, weight, mods, tabs, ctx_len, windowed, tag, after_mixer=None):
    cos, sin = tabs
    xn = norm_mod_fwd(h, lw["norm1_g"], mods["shift1"], mods["scale1"], tag + "norm1_fwd")
    lw["w_in"] = weight("w_in", xn)
    proj = mm_nn(xn, lw["w_in"], tag + "w_in_fwd")
    q, k, v = qk_prep_fwd(proj, lw["q_g"], lw["k_g"], cos, sin, tag + "qk_fwd")
    cat, lse = attention_fwd(q, k, v, lw["sink"], ctx_len, windowed, tag + "attn_fwd")
    if windowed:
        cat = pool_fwd(proj, lw["pool_w"], lw["pool_scale"], cat, ctx_len, tag + "pool_fwd")
    else:
        cat = conv_fwd(proj, lw["conv_w"], cat, ctx_len, tag + "conv_fwd")
    lw["w_out"] = weight("w_out", cat)
    gate1 = mods["gate1"] if after_mixer is None else mods["gate1"] + after_mixer(cat)[0, 0]
    y, h1 = mm_nn_residual(cat, lw["w_out"], h, gate1, tag + "w_out_fwd")
    hn = norm_mod_fwd(h1, lw["norm2_g"], mods["shift2"], mods["scale2"], tag + "norm2_fwd")
    lw["w_gate"], lw["w_up"] = weight("w_gate", hn), weight("w_up", hn)
    g, u, act = mm_swiglu(hn, lw["w_gate"], lw["w_up"], tag + "ffn_in_fwd")
    lw["w_down"] = weight("w_down", act)
    f, h2 = mm_nn_residual(act, lw["w_down"], h1, mods["gate2"], tag + "w_down_fwd")
    saved = dict(h=h, xn=xn, proj=proj, q=q, k=k, v=v, cat=cat, lse=lse, y=y, h1=h1, hn=hn, g=g, u=u, act=act, f=f)
    return h2, saved


def _layer_bwd_ffn(dh, sv, lw, mods, tag):
    big, small = {}, {}
    df, dgate2 = gate_bwd(dh, sv["f"], mods["gate2"], tag + "gate2_bwd")
    dgp, du = mm_nt_swiglu_bwd(df, lw["w_down"], sv["g"], sv["u"], tag + "w_down_dgrad")
    big["w_down"] = mm_tn(sv["act"], df, tag + "w_down_wgrad", col_sharded=False)
    dhn = mm_nt_cols(dgp, lw["w_gate"], tag + "w_gate_dgrad")
    dhn = mm_nt_cols(du, lw["w_up"], tag + "w_up_dgrad", add=dhn)
    big["w_gate"] = mm_tn(sv["hn"], dgp, tag + "w_gate_wgrad", col_sharded=True)
    big["w_up"] = mm_tn(sv["hn"], du, tag + "w_up_wgrad", col_sharded=True)
    dh1, dshift2, dscale2, small["norm2_g"] = norm_mod_bwd(dhn, sv["h1"], lw["norm2_g"], mods["scale2"], dh,
                                                           tag + "norm2_bwd")
    return dh1, big, small, (dshift2, dscale2, dgate2)


def _layer_bwd_mixer(dh1, sv, lw, mods, tabs, ctx_len, windowed, tag, between=None):
    cos, sin = tabs
    big, small = {}, {}
    dy, dgate1 = gate_bwd(dh1, sv["y"], mods["gate1"], tag + "gate1_bwd")
    dcat = mm_nt_rows(dy, lw["w_out"], tag + "w_out_dgrad")
    big["w_out"] = mm_tn(sv["cat"], dy, tag + "w_out_wgrad", col_sharded=False)
    if windowed:
        daux, small["pool_w"], small["pool_scale"] = pool_bwd(dcat, sv["proj"], lw["pool_w"], lw["pool_scale"],
                                                              ctx_len, tag + "pool_bwd")
    else:
        daux, small["conv_w"] = conv_bwd(dcat, sv["proj"], lw["conv_w"], ctx_len, tag + "conv_bwd")
    dq, dk, dv, dsink = attention_bwd(sv["q"], sv["k"], sv["v"], sv["cat"], sv["lse"], dcat, lw["sink"], ctx_len,
                                      windowed, tag + "attn_bwd")
    if windowed:
        small["sink"] = dsink[:, :GROUPS, 0].reshape(Q_HEADS)
    q_g = lw["q_g"] if between is None else lw["q_g"] + between(dq)[0, 0]
    dproj, small["q_g"], small["k_g"] = qk_prep_bwd(dq, dk, dv, daux, sv["proj"], q_g, lw["k_g"], cos, sin,
                                                    tag + "qk_bwd")
    dxn = mm_nt_cols(dproj, lw["w_in"], tag + "w_in_dgrad")
    big["w_in"] = mm_tn(sv["xn"], dproj, tag + "w_in_wgrad", col_sharded=True)
    dh0, dshift1, dscale1, small["norm1_g"] = norm_mod_bwd(dxn, sv["h"], lw["norm1_g"], mods["scale1"], dh1,
                                                           tag + "norm1_bwd")
    return dh0, big, small, (dshift1, dscale1, dgate1)


def _dmod_rows(mixer_part, ffn_part):
    return jnp.concatenate([m[:, 0, :] for m in (*mixer_part, *ffn_part)], axis=1)


def _tie(mods, name, token):
    return {**mods, name: mods[name] + token[0, 0]}


def _place():
    x, y, c = lax.axis_index("x"), lax.axis_index("y"), lax.axis_index("c")
    chips = [(1 - x, y), (x, 1 - y), (1 - x, 1 - y)]
    return x, y, c, chips


def _remote(src, dst, send_sem, recv_sem, to):
    return pltpu.make_async_remote_copy(src_ref=src, dst_ref=dst, send_sem=send_sem, recv_sem=recv_sem,
                                        device_id=to, device_id_type=MESH)


ANY = pl.BlockSpec(memory_space=pl.ANY)


def all_gather8(x_shard, name):
    m_per, n = x_shard.shape

    def body(x_ref, out_ref, send_sems, recv_sems, local_sem):
        x, y, c, chips = _place()
        me, sibling = (x, y, c), (x, y, 1 - c)

        def rows(px, py, pc):
            return out_ref.at[pl.ds((4 * px + 2 * py + pc) * m_per, m_per), :]

        def copy(k, block, to, src=None):
            return _remote(rows(*block) if src is None else src, rows(*block), send_sems.at[k], recv_sems.at[k], to)

        mine = pltpu.make_async_copy(x_ref, rows(*me), local_sem)
        mine.start()
        first = [copy(0, me, sibling, src=x_ref)]
        first += [copy(1 + j, me, (*chip, c), src=x_ref) for j, chip in enumerate(chips)]
        for cp in first:
            cp.start()
        passed = [copy(4 + j, (*chip, c), sibling) for j, chip in enumerate(chips)]
        for j, chip in enumerate(chips):
            copy(1 + j, (*chip, c), me).wait_recv()
            passed[j].start()
        copy(0, sibling, me).wait_recv()
        for j, chip in enumerate(chips):
            copy(4 + j, (*chip, 1 - c), me).wait_recv()
        for cp in first + passed:
            cp.wait_send()
        mine.wait()

    return _pcall(
        body, name=name,
        out_shape=jax.ShapeDtypeStruct((N_DEV * m_per, n), x_shard.dtype),
        in_specs=[pl.BlockSpec(memory_space=pltpu.VMEM)],
        out_specs=pl.BlockSpec(memory_space=pltpu.VMEM),
        scratch_shapes=[pltpu.SemaphoreType.DMA((7,)), pltpu.SemaphoreType.DMA((7,)), pltpu.SemaphoreType.DMA],
        compiler_params=pltpu.CompilerParams(vmem_limit_bytes=VMEM_LIMIT),
    )(x_shard)


def _row_tile(rows):
    for tr in (256, 128, 64, 32, 16):
        if rows % tr == 0:
            return tr
    return rows


def place_shard(w, p, name):
    r, cols = w.shape
    tr = _row_tile(r)

    def body(p_ref, w_ref, o_ref):
        del p_ref
        o_ref[...] = w_ref[...].astype(BF16)

    return _pcall(
        body, name=name,
        grid_spec=pltpu.PrefetchScalarGridSpec(
            num_scalar_prefetch=1, grid=(r // tr,),
            in_specs=[pl.BlockSpec((tr, cols), lambda i, p_ref: (i, 0))],
            out_specs=pl.BlockSpec((None, tr, cols), lambda i, p_ref: (p_ref[0], i, 0))),
        out_shape=jax.ShapeDtypeStruct((N_SHARD, r, cols), BF16),
        compiler_params=_params("parallel"),
    )(p.reshape(1).astype(jnp.int32), w)


def gather_weight_shards(stacks, after):
    n = len(stacks)

    def body(*refs):
        outs = refs[n + 1:2 * n + 1]
        send_sems, recv_sems = refs[2 * n + 1:]
        x, y, c, chips = _place()
        p = 2 * x + y
        sibling = (x, y, 1 - c)
        started = []
        for i in range(n):
            hr = outs[i].shape[1] // 2
            mine = outs[i].at[p, pl.ds(c * hr, hr)]
            for j, chip in enumerate(chips):
                cp = _remote(mine, mine, send_sems.at[i, j], recv_sems.at[i, j], (*chip, c))
                cp.start()
                started.append(cp)
        for i in range(n):
            hr = outs[i].shape[1] // 2
            for j, (px, py) in enumerate(chips):
                half = outs[i].at[2 * px + py, pl.ds(c * hr, hr)]
                _remote(half, half, send_sems.at[i, j], recv_sems.at[i, j], (px, py, c)).wait_recv()
                fwd = _remote(half, half, send_sems.at[i, 3 + j], recv_sems.at[i, 3 + j], sibling)
                fwd.start()
                started.append(fwd)
        for i in range(n):
            hr = outs[i].shape[1] // 2
            for j, (px, py) in enumerate(chips):
                other = outs[i].at[2 * px + py, pl.ds((1 - c) * hr, hr)]
                _remote(other, other, send_sems.at[i, 3 + j], recv_sems.at[i, 3 + j], sibling).wait_recv()
        for cp in started:
            cp.wait_send()

    return _pcall(
        body, name="gather_weight_shards",
        out_shape=[jax.ShapeDtypeStruct(s.shape, s.dtype) for s in stacks],
        in_specs=[ANY] * (n + 1), out_specs=[ANY] * n,
        input_output_aliases={i: i for i in range(n)},
        scratch_shapes=[pltpu.SemaphoreType.DMA((n, 6)), pltpu.SemaphoreType.DMA((n, 6))],
    )(*stacks, after)


HBM_SPEC = pl.BlockSpec(memory_space=pltpu.HBM)
SEM_SPEC = pl.BlockSpec(memory_space=pltpu.SEMAPHORE)
EFFECT = pltpu.SideEffectType.DATAFLOW_SIDE_EFFECTING


def split_start(name, copies, n_sems, arrays, after):
    m = len(arrays)

    def body(*refs):
        for cp in copies(refs[:m], refs[m + 1], refs[m + 2]):
            cp.start()
        refs[-1][...] = jnp.zeros_like(refs[-1])

    res = _pcall(
        body, name=name,
        out_shape=(pltpu.SemaphoreType.DMA((n_sems,)), pltpu.SemaphoreType.DMA((n_sems,)),
                   *[pltpu.HBM(s.shape, s.dtype) for s in arrays], jax.ShapeDtypeStruct((8, 128), F32)),
        in_specs=[HBM_SPEC] * m + [ANY],
        out_specs=(SEM_SPEC, SEM_SPEC, *[HBM_SPEC] * m, pl.BlockSpec(memory_space=pltpu.VMEM)),
        input_output_aliases={i: 2 + i for i in range(m)},
        compiler_params=pltpu.CompilerParams(has_side_effects=EFFECT),
    )(*[pltpu.with_memory_space_constraint(s, pltpu.HBM) for s in arrays], after)
    return res[0], res[1], list(res[2:2 + m]), res[2 + m]


def split_wait(name, copies, send_sems, recv_sems, arrays, after):
    m = len(arrays)

    def body(*refs):
        for cp in copies(refs[:m], refs[m], refs[m + 1]):
            cp.wait_send()
            cp.wait_recv()

    return _pcall(
        body, name=name,
        out_shape=[pltpu.HBM(s.shape, s.dtype) for s in arrays],
        in_specs=[HBM_SPEC] * m + [SEM_SPEC, SEM_SPEC, ANY],
        out_specs=[HBM_SPEC] * m,
        input_output_aliases={i: i for i in range(m)},
        compiler_params=pltpu.CompilerParams(has_side_effects=EFFECT),
    )(*arrays, send_sems, recv_sems, after)


def gather_copies(n):
    def copies(stacks, send_sems, recv_sems):
        x, y, c, chips = _place()
        p = 2 * x + y
        return [_remote(stacks[i].at[p], stacks[i].at[p], send_sems.at[3 * i + j], recv_sems.at[3 * i + j], (*chip, c))
                for i in range(n) for j, chip in enumerate(chips)]
    return copies


def gather_half_copies(n):
    def copies(stacks, send_sems, recv_sems):
        x, y, c, chips = _place()
        p = 2 * x + y
        out = []
        for i in range(n):
            hr = stacks[i].shape[1] // 2
            mine = stacks[i].at[p, pl.ds(c * hr, hr)]
            out += [_remote(mine, mine, send_sems.at[3 * i + j], recv_sems.at[3 * i + j], (*chip, c))
                    for j, chip in enumerate(chips)]
        return out
    return copies


def forward_half_copies(n):
    def copies(stacks, send_sems, recv_sems):
        x, y, c, chips = _place()
        out = []
        for i in range(n):
            hr = stacks[i].shape[1] // 2
            for j, (px, py) in enumerate(chips):
                half = stacks[i].at[2 * px + py, pl.ds(c * hr, hr)]
                out.append(_remote(half, half, send_sems.at[3 * i + j], recv_sems.at[3 * i + j], (x, y, 1 - c)))
        return out
    return copies


def sibling_half_copies(n):
    def copies(refs, send_sems, recv_sems):
        x, y, c, _ = _place()
        out = []
        for i in range(n):
            hr = refs[n + i].shape[1]
            out.append(_remote(refs[i].at[:, pl.ds((1 - c) * hr, hr), :], refs[n + i], send_sems.at[i], recv_sems.at[i],
                               (x, y, 1 - c)))
        return out
    return copies


def chip_quarter_copies(n):
    def copies(refs, send_sems, recv_sems):
        _, _, c, chips = _place()
        return [_remote(refs[i].at[2 * px + py], refs[n + i].at[j], send_sems.at[3 * i + j], recv_sems.at[3 * i + j],
                        (px, py, c)) for i in range(n) for j, (px, py) in enumerate(chips)]
    return copies


def exchange_final_halves(grads, name):
    n = len(grads)

    def body(*refs):
        outs = refs[n:2 * n]
        send_sems, recv_sems = refs[2 * n:]
        x, y, c, _ = _place()
        sends = []
        for i in range(n):
            hr = outs[i].shape[0] // 2
            mine = outs[i].at[pl.ds(c * hr, hr)]
            cp = _remote(mine, mine, send_sems.at[i], recv_sems.at[i], (x, y, 1 - c))
            cp.start()
            sends.append(cp)
        for i in range(n):
            hr = outs[i].shape[0] // 2
            other = outs[i].at[pl.ds((1 - c) * hr, hr)]
            _remote(other, other, send_sems.at[i], recv_sems.at[i], (x, y, 1 - c)).wait_recv()
        for cp in sends:
            cp.wait_send()

    return _pcall(
        body, name=name,
        out_shape=[jax.ShapeDtypeStruct(g.shape, g.dtype) for g in grads],
        in_specs=[ANY] * n, out_specs=[ANY] * n,
        input_output_aliases={i: i for i in range(n)},
        scratch_shapes=[pltpu.SemaphoreType.DMA((n,)), pltpu.SemaphoreType.DMA((n,))],
    )(*grads)


def add_own_half(g, recv, c, name):
    s, r, cols = g.shape
    hr = r // 2
    tr = _row_tile(hr)
    nb = hr // tr

    def body(c_ref, g_ref, r_ref, o_ref):
        del c_ref
        o_ref[...] = (g_ref[...] + r_ref[...]).astype(BF16)

    return _pcall(
        body, name=name,
        grid_spec=pltpu.PrefetchScalarGridSpec(
            num_scalar_prefetch=1, grid=(s, nb),
            in_specs=[pl.BlockSpec((None, tr, cols), lambda j, i, c_ref: (j, c_ref[0] * nb + i, 0)),
                      pl.BlockSpec((None, tr, cols), lambda j, i, c_ref: (j, i, 0))],
            out_specs=pl.BlockSpec((None, tr, cols), lambda j, i, c_ref: (j, i, 0))),
        out_shape=jax.ShapeDtypeStruct((s, hr, cols), BF16),
        compiler_params=_params("parallel", "parallel"),
    )(c.reshape(1).astype(jnp.int32), g, recv)


def add_quarters(own, others, p, c, name, deps=()):
    _, hr, cols = own.shape
    tr = _row_tile(hr)
    nb = hr // tr

    def body(pc_ref, own_ref, oth_ref, *rest):
        o_ref = rest[-1]
        acc = own_ref[...].astype(F32) + oth_ref[0].astype(F32)
        acc = acc + oth_ref[1].astype(F32)
        o_ref[...] = acc + oth_ref[2].astype(F32)

    return _pcall(
        body, name=name,
        grid_spec=pltpu.PrefetchScalarGridSpec(
            num_scalar_prefetch=1, grid=(nb,),
            in_specs=[pl.BlockSpec((None, tr, cols), lambda i, pc: (pc[0], i, 0)),
                      pl.BlockSpec((3, tr, cols), lambda i, pc: (0, i, 0))] + [ANY] * len(deps),
            out_specs=pl.BlockSpec((tr, cols), lambda i, pc: (pc[1] * nb + i, 0))),
        out_shape=jax.ShapeDtypeStruct((2 * hr, cols), F32),
        compiler_params=_params("parallel"),
    )(jnp.stack([p, c]).astype(jnp.int32), own, others, *deps)


def _adamw_math(w, g, m, v):
    m2 = ADAM_B1 * m + (1.0 - ADAM_B1) * g
    v2 = ADAM_B2 * v + (1.0 - ADAM_B2) * (g * g)
    m_hat = m2 / (1.0 - ADAM_B1 ** ADAM_STEP)
    v_hat = v2 / (1.0 - ADAM_B2 ** ADAM_STEP)
    delta = -ADAM_LR * (m_hat / (jnp.sqrt(v_hat) + ADAM_EPS) + ADAM_WD * w)
    return delta, m2, v2


def adamw(w, g, m, v, name):
    r, cols = w.shape
    tr = _row_tile(r)

    def body(w_ref, g_ref, m_ref, v_ref, d_ref, m2_ref, v2_ref):
        d_ref[...], m2_ref[...], v2_ref[...] = _adamw_math(w_ref[...], g_ref[...], m_ref[...], v_ref[...])

    blk = pl.BlockSpec((tr, cols), lambda i: (i, 0))
    return _pcall(
        body, name=name, grid=(r // tr,),
        in_specs=[blk] * 4, out_specs=[blk] * 3,
        out_shape=[jax.ShapeDtypeStruct((r, cols), F32)] * 3,
        compiler_params=_params("parallel"),
    )(w, g, m, v)


def small_update(gathered, w, m, v):
    nd, r, lanes = gathered.shape

    def body(ga_ref, w_ref, m_ref, v_ref, g_ref, d_ref, m2_ref, v2_ref):
        g = ga_ref[0] + ga_ref[1]
        for dev in range(2, nd):
            g = g + ga_ref[dev]
        g_ref[...] = g
        d_ref[...], m2_ref[...], v2_ref[...] = _adamw_math(w_ref[...], g, m_ref[...], v_ref[...])

    return _pcall(
        body, name="small_update",
        out_shape=[jax.ShapeDtypeStruct((r, lanes), F32)] * 4,
        compiler_params=pltpu.CompilerParams(vmem_limit_bytes=VMEM_LIMIT),
    )(gathered, w, m, v)


def mod_fwd(c16, w_mod, b_mod, name):
    d, ns = w_mod.shape
    tn = 512

    def body(c_ref, w_ref, b_ref, o_ref):
        cv = c_ref[...]
        sc = (cv * _sigmoid(cv)).astype(BF16)
        o_ref[...] = _dot(sc, w_ref[...].astype(BF16)) + b_ref[...]

    return _pcall(
        body, name=name, grid=(ns // tn,),
        in_specs=[pl.BlockSpec((16, d), lambda j: (0, 0)), pl.BlockSpec((d, tn), lambda j: (0, j)),
                  pl.BlockSpec((1, tn), lambda j: (0, j))],
        out_specs=pl.BlockSpec((16, tn), lambda j: (0, j)),
        out_shape=jax.ShapeDtypeStruct((16, ns), F32),
        compiler_params=_params("parallel"),
    )(c16, w_mod, b_mod)


def wmod_update(c16, dmod16, w, m, v, name):
    d, ns = w.shape
    tn = 256

    def body(c_ref, dm_ref, w_ref, m_ref, v_ref, g_ref, d_ref, m2_ref, v2_ref):
        cv = c_ref[...]
        sc = (cv * _sigmoid(cv)).astype(BF16)
        g = _dot_tn(sc, dm_ref[...].astype(BF16))
        g_ref[...] = g
        d_ref[...], m2_ref[...], v2_ref[...] = _adamw_math(w_ref[...], g, m_ref[...], v_ref[...])

    blk = pl.BlockSpec((d, tn), lambda j: (0, j))
    return _pcall(
        body, name=name, grid=(ns // tn,),
        in_specs=[pl.BlockSpec((16, d), lambda j: (0, 0)), pl.BlockSpec((16, tn), lambda j: (0, j)), blk, blk, blk],
        out_specs=[blk] * 4,
        out_shape=[jax.ShapeDtypeStruct((d, ns), F32)] * 4,
        compiler_params=_params("parallel"),
    )(c16, dmod16, w, m, v)


def cctx_partial(dm0, w0, dm1, w1):
    d, ns = w0.shape
    tn = 512

    def body(dm0_ref, w0_ref, dm1_ref, w1_ref, o_ref):
        @pl.when(pl.program_id(0) == 0)
        def _():
            o_ref[...] = jnp.zeros_like(o_ref)

        for dm_ref, w_ref in ((dm0_ref, w0_ref), (dm1_ref, w1_ref)):
            tot = jnp.sum(dm_ref[...], axis=0, keepdims=True)
            lhs = jnp.broadcast_to(tot, (8, tn)).astype(BF16)
            o_ref[...] += _dot_nt(lhs, w_ref[...].astype(BF16))

    dspec = pl.BlockSpec((8, tn), lambda j: (0, j))
    wspec = pl.BlockSpec((d, tn), lambda j: (0, j))
    return _pcall(
        body, name="cctx_partial", grid=(ns // tn,),
        in_specs=[dspec, wspec, dspec, wspec],
        out_specs=pl.BlockSpec((8, d), lambda j: (0, 0)),
        out_shape=jax.ShapeDtypeStruct((8, d), F32),
        compiler_params=_params("arbitrary"),
    )(dm0, w0, dm1, w1)


def cctx_update(parts, c_ctx, m, v):
    d = c_ctx.shape[1]

    def body(p_ref, c_ref, m_ref, v_ref, g_ref, d_ref, m2_ref, v2_ref):
        tot = p_ref[0:1, :] + p_ref[1:2, :]
        tot = tot + p_ref[2:3, :]
        tot = tot + p_ref[3:4, :]
        cv = c_ref[...]
        sg = _sigmoid(cv)
        g = tot * (sg * (1.0 + cv * (1.0 - sg)))
        g_ref[...] = g
        d_ref[...], m2_ref[...], v2_ref[...] = _adamw_math(cv, g, m_ref[...], v_ref[...])

    return _pcall(
        body, name="cctx_update",
        out_shape=[jax.ShapeDtypeStruct((1, d), F32)] * 4,
    )(parts, c_ctx, m, v)


WEIGHT_NAMES = (
    "c_ctx", "l0_norm1_g", "l0_w_mod", "l0_b_mod", "l0_w_in", "l0_q_norm_g", "l0_k_norm_g", "l0_conv_w", "l0_w_out",
    "l0_norm2_g", "l0_w_gate", "l0_w_up", "l0_w_down", "l1_norm1_g", "l1_w_mod", "l1_b_mod", "l1_w_in",
    "l1_q_norm_g", "l1_k_norm_g", "l1_sink", "l1_pool_w", "l1_pool_scale", "l1_w_out", "l1_norm2_g", "l1_w_gate",
    "l1_w_up", "l1_w_down", "final_norm_g")
BIG_NAMES = ("w_in", "w_out", "w_gate", "w_up", "w_down")
SMALL_SLOTS = tuple(
    [(f"l{l}_{nm}", par and f"l{l}_{nm}") for l in (0, 1)
     for nm, par in (("dmod_lat", False), ("dmod_ctx", False), ("b_mod", True), ("norm1_g", True), ("norm2_g", True),
                     ("q_norm_g", True), ("k_norm_g", True))]
    + [("l0_conv_w", None), ("l1_sink", "l1_sink"), ("l1_pool_w", "l1_pool_w"), ("l1_pool_scale", "l1_pool_scale"),
       ("final_norm_g", "final_norm_g")])


def _pack(values, sizes):
    parts = []
    for (name, _), size in zip(SMALL_SLOTS, sizes):
        v = values.get(name)
        padded = -(-size // 128) * 128
        v = jnp.zeros((padded,), F32) if v is None else jnp.pad(v.reshape(-1).astype(F32), (0, padded - size))
        parts.append(v)
    total = sum(p.shape[0] for p in parts)
    parts.append(jnp.zeros((-(-total // 1024) * 1024 - total,), F32))
    return jnp.concatenate(parts).reshape(-1, 128)


def _offsets(sizes):
    offs, o = {}, 0
    for (name, _), size in zip(SMALL_SLOTS, sizes):
        offs[name] = (o, size)
        o += -(-size // 128) * 128
    return offs


def kernel(x, c, ctx, c_ctx, l0_norm1_g, l0_w_mod, l0_b_mod, l0_w_in, l0_q_norm_g, l0_k_norm_g, l0_conv_w, l0_w_out, l0_norm2_g, l0_w_gate, l0_w_up, l0_w_down, l1_norm1_g, l1_w_mod, l1_b_mod, l1_w_in, l1_q_norm_g, l1_k_norm_g, l1_sink, l1_pool_w, l1_pool_scale, l1_w_out, l1_norm2_g, l1_w_gate, l1_w_up, l1_w_down, final_norm_g, loss_target, m_c_ctx, m_l0_norm1_g, m_l0_w_mod, m_l0_b_mod, m_l0_w_in, m_l0_q_norm_g, m_l0_k_norm_g, m_l0_conv_w, m_l0_w_out, m_l0_norm2_g, m_l0_w_gate, m_l0_w_up, m_l0_w_down, m_l1_norm1_g, m_l1_w_mod, m_l1_b_mod, m_l1_w_in, m_l1_q_norm_g, m_l1_k_norm_g, m_l1_sink, m_l1_pool_w, m_l1_pool_scale, m_l1_w_out, m_l1_norm2_g, m_l1_w_gate, m_l1_w_up, m_l1_w_down, m_final_norm_g, v_c_ctx, v_l0_norm1_g, v_l0_w_mod, v_l0_b_mod, v_l0_w_in, v_l0_q_norm_g, v_l0_k_norm_g, v_l0_conv_w, v_l0_w_out, v_l0_norm2_g, v_l0_w_gate, v_l0_w_up, v_l0_w_down, v_l1_norm1_g, v_l1_w_mod, v_l1_b_mod, v_l1_w_in, v_l1_q_norm_g, v_l1_k_norm_g, v_l1_sink, v_l1_pool_w, v_l1_pool_scale, v_l1_w_out, v_l1_norm2_g, v_l1_w_gate, v_l1_w_up, v_l1_w_down, v_final_norm_g):
    a = dict(locals())
    xi, yi, ci = lax.axis_index("x"), lax.axis_index("y"), lax.axis_index("c")
    p = 2 * xi + yi
    me = 4 * xi + 2 * yi + ci
    d = x.shape[-1]
    conv_cols = l0_conv_w.shape[1]

    row0 = jnp.concatenate([c, jnp.pad(l0_conv_w, ((0, 0), (0, d - conv_cols))), jnp.zeros((4, d), F32)], axis=0)
    gathered = all_gather8(row0, "gather_cond").reshape(N_DEV, 8, d)
    c_all = gathered[:, 0]
    conv_w = gathered[0::2, 1:4, :conv_cols].transpose(1, 0, 2).reshape(3, N_SHARD * conv_cols)
    c16_fwd = jnp.concatenate([c_all, c_ctx[None], jnp.zeros((7, d), F32)], axis=0)
    c16_bwd = jnp.concatenate([c_all, jnp.broadcast_to(c_ctx[None], (8, d))], axis=0)

    ns_mod = l0_w_mod.shape[1]
    mod_parts = [mod_fwd(c16_fwd, a[f"l{l}_w_mod"], lax.dynamic_slice(a[f"l{l}_b_mod"], (p * ns_mod,), (ns_mod,))[None],
                         f"l{l}_mod_fwd") for l in (0, 1)]
    modg = all_gather8(jnp.concatenate(mod_parts, axis=0), "gather_mod").reshape(N_DEV, 2, 16, ns_mod)[0::2]
    mod_full = modg.transpose(1, 2, 0, 3).reshape(2, 16, N_SHARD * ns_mod)
    mods = []
    for l in (0, 1):
        lat = lax.dynamic_index_in_dim(mod_full[l], me, axis=0, keepdims=False)
        cx = mod_full[l, 8]
        mods.append({nm: jnp.stack([cx[j * d:(j + 1) * d], lat[j * d:(j + 1) * d]])[:, None, :]
                     for j, nm in enumerate(MOD_NAMES)})

    placed = {(l, nm): place_shard(a[f"l{l}_{nm}"], p, f"l{l}_{nm}_place") for l in (0, 1) for nm in BIG_NAMES}
    ready = {(0, "w_in"): gather_weight_shards([placed[0, "w_in"]], mod_full)[0]}
    groups = (((0, "w_out"), (0, "w_gate"), (0, "w_up")), ((0, "w_down"),), ((1, "w_in"), (1, "w_out")),
              ((1, "w_gate"), (1, "w_up"), (1, "w_down")))
    staged = len(groups) - 1
    flights, after = [], ready[0, "w_in"]
    for gi, grp in enumerate(groups):
        first_copies = gather_half_copies if gi == staged else gather_copies
        flights.append(split_start(f"gather_start_{gi}", first_copies(len(grp)), 3 * len(grp),
                                   [placed[k] for k in grp], after))
        after = flights[-1][3]

    def forward_staged(after):
        n = len(groups[staged])
        send_sems, recv_sems, in_flight, _ = flights[staged]
        landed = split_wait(f"gather_wait_{staged}", gather_half_copies(n), send_sems, recv_sems, in_flight, after)
        flights[staged] = split_start("gather_forward_start", forward_half_copies(n), 3 * n, landed, after)
        return flights[staged][3]

    def weight_of(l):
        def weight(nm, after):
            if (l, nm) not in ready:
                gi = next(i for i, grp in enumerate(groups) if (l, nm) in grp)
                send_sems, recv_sems, in_flight, _ = flights[gi]
                name, copies = ("gather_forward_wait", forward_half_copies) if gi == staged else (
                    f"gather_wait_{gi}", gather_copies)
                landed = split_wait(name, copies(len(groups[gi])), send_sems, recv_sems, in_flight, after)
                ready.update(zip(groups[gi], landed))
            return ready[l, nm]
        return weight

    layers = [dict(norm1_g=a[f"l{l}_norm1_g"][None], norm2_g=a[f"l{l}_norm2_g"][None],
                   q_g=a[f"l{l}_q_norm_g"][None], k_g=a[f"l{l}_k_norm_g"][None]) for l in (0, 1)]
    layers[0].update(conv_w=conv_w, sink=jnp.zeros((Q_HEADS,), F32))
    layers[1].update(sink=l1_sink, pool_w=l1_pool_w, pool_scale=l1_pool_scale[None])

    ctx_len = ctx.shape[1]
    h = jnp.concatenate([ctx[0], x[0]], axis=0)
    tabs = _rope_tables(h.shape[0], ctx_len)
    h, saved0 = _layer_fwd(h, layers[0], weight_of(0), _tie(mods[0], "shift1", flights[-1][3]), tabs, ctx_len, False,
                           "l0_")
    h, saved1 = _layer_fwd(h, layers[1], weight_of(1), mods[1], tabs, ctx_len, True, "l1_", forward_staged)
    loss_tile, dh, dfg = final_loss(h, final_norm_g[None], loss_target[0])

    keys = [[f"l{l}_{nm}" for nm in BIG_NAMES] for l in (0, 1)]
    n_big = len(BIG_NAMES)
    dh, ffn_big, ffn_small, ffn_mod = _layer_bwd_ffn(dh, saved1, layers[1], mods[1], "l1_")
    dh, mix_big, mix_small, mix_mod = _layer_bwd_mixer(dh, saved1, layers[1], mods[1], tabs, ctx_len, True, "l1_")
    bigs, smalls, dmods = [None, {**ffn_big, **mix_big}], [None, {**ffn_small, **mix_small}], [None, None]
    dmods[1] = _dmod_rows(mix_mod, ffn_mod)
    grads1 = [bigs[1][nm] for nm in BIG_NAMES]
    lands = [lax.empty((g.shape[0], g.shape[1] // 2, g.shape[2]), g.dtype) for g in grads1]
    send_a, recv_a, fly_a, token_a = split_start("l1_sibling_halves_start", sibling_half_copies(n_big), n_big,
                                                 grads1 + lands, dh)
    dh, ffn_big, ffn_small, ffn_mod = _layer_bwd_ffn(dh, saved0, layers[0], _tie(mods[0], "gate2", token_a), "l0_")
    fly_a = split_wait("l1_sibling_halves_wait", sibling_half_copies(n_big), send_a, recv_a, fly_a, dh)
    chip_sums1 = [add_own_half(g, r, ci, k + "_add_halves")
                  for k, g, r in zip(keys[1], fly_a[:n_big], fly_a[n_big:])]
    lands = [lax.empty((3,) + s.shape[1:], s.dtype) for s in chip_sums1]
    send_b, recv_b, fly_b, token_b = split_start("l1_chip_quarters_start", chip_quarter_copies(n_big), 3 * n_big,
                                                 chip_sums1 + lands, dh)
    ffn_names = ("w_gate", "w_up", "w_down")
    n_ffn = len(ffn_names)
    ffn_keys = ["l0_" + nm for nm in ffn_names]
    grads0f = [ffn_big[nm] for nm in ffn_names]
    lands = [lax.empty((g.shape[0], g.shape[1] // 2, g.shape[2]), g.dtype) for g in grads0f]
    send_c, recv_c, fly_c, token_c = split_start("l0_ffn_sibling_halves_start", sibling_half_copies(n_ffn), n_ffn,
                                                 grads0f + lands, token_b)
    ffn_flight = []

    def between(dq):
        landed = split_wait("l0_ffn_sibling_halves_wait", sibling_half_copies(n_ffn), send_c, recv_c, fly_c, dq)
        sums = [add_own_half(g, r, ci, k + "_add_halves") for k, g, r in zip(ffn_keys, landed[:n_ffn], landed[n_ffn:])]
        slots = [lax.empty((3,) + s.shape[1:], s.dtype) for s in sums]
        ffn_flight.extend(split_start("l0_ffn_chip_quarters_start", chip_quarter_copies(n_ffn), 3 * n_ffn,
                                      sums + slots, dq))
        return ffn_flight[3]

    dh, mix_big, mix_small, mix_mod = _layer_bwd_mixer(dh, saved0, layers[0], _tie(mods[0], "gate1", token_c), tabs,
                                                       ctx_len, False, "l0_", between)
    fly_b = split_wait("l1_chip_quarters_wait", chip_quarter_copies(n_big), send_b, recv_b, fly_b, dh)
    fly_d = split_wait("l0_ffn_chip_quarters_wait", chip_quarter_copies(n_ffn), ffn_flight[0], ffn_flight[1],
                       ffn_flight[2], dh)
    bigs[0], smalls[0], dmods[0] = {**ffn_big, **mix_big}, {**ffn_small, **mix_small}, _dmod_rows(mix_mod, ffn_mod)
    dx = dh[ctx_len:]
    chip_sums = dict(zip(keys[1] + ffn_keys, list(fly_b[:n_big]) + list(fly_d[:n_ffn])))
    quarters = dict(zip(keys[1] + ffn_keys, list(fly_b[n_big:]) + list(fly_d[n_ffn:])))

    partial = {"l0_conv_w": smalls[0]["conv_w"], "l1_sink": smalls[1]["sink"], "l1_pool_w": smalls[1]["pool_w"],
               "l1_pool_scale": smalls[1]["pool_scale"], "final_norm_g": dfg}
    for l in (0, 1):
        partial.update({f"l{l}_dmod_lat": dmods[l][1], f"l{l}_dmod_ctx": dmods[l][0],
                        f"l{l}_b_mod": dmods[l][0] + dmods[l][1], f"l{l}_norm1_g": smalls[l]["norm1_g"],
                        f"l{l}_norm2_g": smalls[l]["norm2_g"], f"l{l}_q_norm_g": smalls[l]["q_g"],
                        f"l{l}_k_norm_g": smalls[l]["k_g"]})
    sizes = [int(np.prod(partial[name].shape)) for name, _ in SMALL_SLOTS]
    offs = _offsets(sizes)

    last_keys = ["l0_w_in", "l0_w_out"]
    grads_last = [bigs[0]["w_in"], bigs[0]["w_out"]]
    lands = [lax.empty((g.shape[0], g.shape[1] // 2, g.shape[2]), g.dtype) for g in grads_last]
    send_e, recv_e, fly_e, token_e = split_start("l0_last_sibling_halves_start", sibling_half_copies(2), 2,
                                                 grads_last + lands, dh)
    gpack = _pack(partial, sizes) + token_e[0:1, 0:1]
    rows = gpack.shape[0]
    small_all = all_gather8(gpack, "gather_small").reshape(N_DEV, rows, 128)
    packs = [_pack({name: a[pre + par] for name, par in SMALL_SLOTS if par}, sizes) for pre in ("", "m_", "v_")]
    small_out = [t.reshape(-1) for t in small_update(small_all, *packs)]
    flat_all = small_all.reshape(N_DEV, rows * 128)

    def slot(flat, name, shape):
        o, size = offs[name]
        return flat[o:o + size].reshape(shape)

    results = {}
    for name, par in SMALL_SLOTS:
        if par:
            results[par] = tuple(slot(t, name, a[par].shape) for t in small_out)

    ctx_rows = []
    for l in (0, 1):
        o_lat, o_ctx = offs[f"l{l}_dmod_lat"][0], offs[f"l{l}_dmod_ctx"][0]
        lat = lax.dynamic_slice(flat_all, (0, o_lat + p * ns_mod), (N_DEV, ns_mod))
        cxr = lax.dynamic_slice(flat_all, (0, o_ctx + p * ns_mod), (N_DEV, ns_mod))
        ctx_rows.append(cxr)
        results[f"l{l}_w_mod"] = tuple(wmod_update(c16_bwd, jnp.concatenate([lat, cxr], axis=0), a[f"l{l}_w_mod"],
                                                   a[f"m_l{l}_w_mod"], a[f"v_l{l}_w_mod"], f"l{l}_w_mod_update"))
    part = cctx_partial(ctx_rows[0], l0_w_mod, ctx_rows[1], l1_w_mod)
    parts4 = all_gather8(part, "gather_cctx").reshape(N_DEV, 8, d)[0::2, 0]
    results["c_ctx"] = tuple(t[0] for t in cctx_update(parts4, c_ctx[None], m_c_ctx[None], v_c_ctx[None]))

    gconv = lax.dynamic_slice(slot(small_out[0], "l0_conv_w", (3, N_SHARD * conv_cols)), (0, p * conv_cols),
                              (3, conv_cols))
    results["l0_conv_w"] = (gconv,) + tuple(adamw(l0_conv_w, gconv, m_l0_conv_w, v_l0_conv_w, "l0_conv_w_adamw"))

    fly_e = split_wait("l0_last_sibling_halves_wait", sibling_half_copies(2), send_e, recv_e, fly_e,
                       results["c_ctx"][0])
    sums_last = [add_own_half(g, r, ci, k + "_add_halves") for k, g, r in zip(last_keys, fly_e[:2], fly_e[2:])]
    slots = [lax.empty((3,) + s.shape[1:], s.dtype) for s in sums_last]
    send_f, recv_f, fly_f, token_f = split_start("l0_last_chip_quarters_start", chip_quarter_copies(2), 6,
                                                 sums_last + slots, sums_last[0])

    def finish(names, tag, deps):
        halves = [add_quarters(chip_sums[k], quarters[k], p, ci, k + "_add_quarters", deps) for k in names]
        for k, g in zip(names, exchange_final_halves(halves, "exchange_final_halves_" + tag)):
            results[k] = (g,) + tuple(adamw(a[k], g, a["m_" + k], a["v_" + k], k + "_adamw"))

    done_keys = keys[1] + ffn_keys
    finish(done_keys, "first", (token_f,))
    fly_f = split_wait("l0_last_chip_quarters_wait", chip_quarter_copies(2), send_f, recv_f, fly_f,
                       results[done_keys[-1]][1])
    chip_sums.update(zip(last_keys, fly_f[:2]))
    quarters.update(zip(last_keys, fly_f[2:]))
    finish(last_keys, "last", ())

    loss = lax.psum(loss_tile[0, 0], ("x", "y", "c"))
    out = [loss, dx[None]]
    for j in range(4):
        out += [results[k][j] for k in WEIGHT_NAMES]
    return tuple(out)
```

```python
import numpy as np
import jax
import jax.numpy as jnp
from jax import lax
from jax.experimental import pallas as pl
from jax.experimental.pallas import tpu as pltpu

F32 = jnp.float32
BF16 = jnp.bfloat16
MESH = pl.DeviceIdType.MESH

HEAD_DIM = 128
Q_HEADS = 12
KV_HEADS = 4
GROUPS = Q_HEADS // KV_HEADS
ATT_WIDTH = Q_HEADS * HEAD_DIM
KV_WIDTH = KV_HEADS * HEAD_DIM
QKV_WIDTH = ATT_WIDTH + 2 * KV_WIDTH
AUX_WIDTH = 512
AUX_GROUPS = 4
POOL_WINDOWS = (2, 4, 8, 16)
WINDOW = 128
GRID_W = 64
ROPE_THETA = 10000.0
N_MOD = 6
EPS = 1e-6
NEG_INF = -1e30
ATT_SCALE = HEAD_DIM ** -0.5
N_SHARD = 4
N_DEV = 8

ADAM_LR = 0.001
ADAM_B1 = 0.9
ADAM_B2 = 0.999
ADAM_EPS = 1e-08
ADAM_WD = 0.01
ADAM_STEP = 10

TM = 256
BAND = TM + 2 * WINDOW
VMEM_LIMIT = 56 * 1024 * 1024

NT_DIMS = (((1,), (1,)), ((), ()))
TN_DIMS = (((0,), (0,)), ((), ()))


def _pcall(body, **kw):
    return pl.pallas_call(body, **kw)


def _params(*sem):
    return pltpu.CompilerParams(dimension_semantics=sem, vmem_limit_bytes=VMEM_LIMIT)


def _sel(i):
    return jnp.minimum(i, 1)


def _sel_spec(d):
    return pl.BlockSpec((None, 1, d), lambda i: (_sel(i), 0, 0))


def _dot(a, b):
    return jnp.dot(a, b, preferred_element_type=F32)


def _dot_nt(a, b):
    return lax.dot_general(a, b, NT_DIMS, preferred_element_type=F32)


def _dot_tn(a, b):
    return lax.dot_general(a, b, TN_DIMS, preferred_element_type=F32)


def _sigmoid(x):
    return 1.0 / (1.0 + jnp.exp(-x))


def norm_mod_fwd(h, g, shift, scale, name):
    t, d = h.shape

    def body(h_ref, g_ref, sh_ref, sc_ref, o_ref):
        x = h_ref[...]
        r = lax.rsqrt(jnp.mean(x * x, axis=-1, keepdims=True) + EPS)
        y = x * r * g_ref[...]
        o_ref[...] = (y * (1.0 + sc_ref[...]) + sh_ref[...]).astype(BF16)

    return _pcall(
        body, name=name, grid=(t // TM,),
        in_specs=[pl.BlockSpec((TM, d), lambda i: (i, 0)), pl.BlockSpec((1, d), lambda i: (0, 0)),
                  _sel_spec(d), _sel_spec(d)],
        out_specs=pl.BlockSpec((TM, d), lambda i: (i, 0)),
        out_shape=jax.ShapeDtypeStruct((t, d), BF16),
        compiler_params=_params("parallel"),
    )(h, g, shift, scale)


def norm_mod_bwd(dy, h, g, scale, dres, name):
    t, d = h.shape

    def body(dy_ref, h_ref, g_ref, sc_ref, dres_ref, dh_ref, dsh_ref, dsc_ref, dg_ref):
        i = pl.program_id(0)

        @pl.when(i == 0)
        def _():
            dsh_ref[...] = jnp.zeros_like(dsh_ref)
            dsc_ref[...] = jnp.zeros_like(dsc_ref)
            dg_ref[...] = jnp.zeros_like(dg_ref)

        x = h_ref[...]
        gv = g_ref[...]
        r = lax.rsqrt(jnp.mean(x * x, axis=-1, keepdims=True) + EPS)
        xhat = x * r
        dyv = dy_ref[...]
        s = _sel(i)
        dsh_ref[pl.ds(s, 1)] += jnp.sum(dyv, axis=0, keepdims=True)[None]
        dsc_ref[pl.ds(s, 1)] += jnp.sum(dyv * xhat * gv, axis=0, keepdims=True)[None]
        tt = dyv * (1.0 + sc_ref[...])
        dg_ref[...] += jnp.sum(tt * xhat, axis=0, keepdims=True)
        dxhat = tt * gv
        dx = r * (dxhat - xhat * jnp.mean(dxhat * xhat, axis=-1, keepdims=True))
        dh_ref[...] = dx + dres_ref[...]

    row = pl.BlockSpec((TM, d), lambda i: (i, 0))
    acc2 = pl.BlockSpec((2, 1, d), lambda i: (0, 0, 0))
    return _pcall(
        body, name=name, grid=(t // TM,),
        in_specs=[row, row, pl.BlockSpec((1, d), lambda i: (0, 0)), _sel_spec(d), row],
        out_specs=[row, acc2, acc2, pl.BlockSpec((1, d), lambda i: (0, 0))],
        out_shape=[jax.ShapeDtypeStruct((t, d), F32), jax.ShapeDtypeStruct((2, 1, d), F32),
                   jax.ShapeDtypeStruct((2, 1, d), F32), jax.ShapeDtypeStruct((1, d), F32)],
        compiler_params=_params("arbitrary"),
    )(dy, h, g, scale, dres)


def gate_bwd(dh, f, gate, name):
    t, d = dh.shape

    def body(dh_ref, f_ref, gt_ref, df_ref, dgt_ref):
        i = pl.program_id(0)

        @pl.when(i == 0)
        def _():
            dgt_ref[...] = jnp.zeros_like(dgt_ref)

        dhv = dh_ref[...]
        df_ref[...] = (dhv * gt_ref[...]).astype(BF16)
        dgt_ref[pl.ds(_sel(i), 1)] += jnp.sum(dhv * f_ref[...], axis=0, keepdims=True)[None]

    row = pl.BlockSpec((TM, d), lambda i: (i, 0))
    return _pcall(
        body, name=name, grid=(t // TM,),
        in_specs=[row, row, _sel_spec(d)],
        out_specs=[row, pl.BlockSpec((2, 1, d), lambda i: (0, 0, 0))],
        out_shape=[jax.ShapeDtypeStruct((t, d), BF16), jax.ShapeDtypeStruct((2, 1, d), F32)],
        compiler_params=_params("arbitrary"),
    )(dh, f, gate)


def final_loss(h, fg, target):
    t, d = h.shape

    def body(h_ref, g_ref, tg_ref, loss_ref, dh_ref, dg_ref):
        i = pl.program_id(0)

        @pl.when(i == 0)
        def _():
            loss_ref[...] = jnp.zeros_like(loss_ref)
            dg_ref[...] = jnp.zeros_like(dg_ref)
            dh_ref[...] = jnp.zeros_like(dh_ref)

        @pl.when(i > 0)
        def _():
            x = h_ref[...]
            gv = g_ref[...]
            r = lax.rsqrt(jnp.mean(x * x, axis=-1, keepdims=True) + EPS)
            xhat = x * r
            diff = xhat * gv - tg_ref[...]
            loss_ref[...] += 0.5 * jnp.sum(jnp.mean(diff * diff, axis=-1, keepdims=True), axis=0, keepdims=True)
            dout = diff * (1.0 / d)
            dg_ref[...] += jnp.sum(dout * xhat, axis=0, keepdims=True)
            dxhat = dout * gv
            dh_ref[...] = r * (dxhat - xhat * jnp.mean(dxhat * xhat, axis=-1, keepdims=True))

    row = pl.BlockSpec((TM, d), lambda i: (i, 0))
    return _pcall(
        body, name="final_loss", grid=(t // TM,),
        in_specs=[row, pl.BlockSpec((1, d), lambda i: (0, 0)),
                  pl.BlockSpec((TM, d), lambda i: (jnp.maximum(i - 1, 0), 0))],
        out_specs=[pl.BlockSpec((8, 128), lambda i: (0, 0)), row, pl.BlockSpec((1, d), lambda i: (0, 0))],
        out_shape=[jax.ShapeDtypeStruct((8, 128), F32), jax.ShapeDtypeStruct((t, d), F32),
                   jax.ShapeDtypeStruct((1, d), F32)],
        compiler_params=_params("arbitrary"),
    )(h, fg, target)


def _w_nn_spec(w, k):
    if w.shape[1] == k:
        ns = w.shape[2]
        return w, pl.BlockSpec((None, k, ns), lambda j, i: (j, 0, 0)), w.shape[0], ns
    w2 = w.reshape(w.shape[0] * w.shape[1], w.shape[2])
    tn = 1024 if w2.shape[1] % 1024 == 0 else w2.shape[1]
    return w2, pl.BlockSpec((k, tn), lambda j, i: (0, j)), w2.shape[1] // tn, tn


def mm_nn(a, w, name, out_dtype=F32):
    t, k = a.shape
    w, wspec, nb, tn = _w_nn_spec(w, k)

    def body(a_ref, w_ref, o_ref):
        o_ref[...] = _dot(a_ref[...], w_ref[...]).astype(o_ref.dtype)

    return _pcall(
        body, name=name, grid=(nb, t // TM),
        in_specs=[pl.BlockSpec((TM, k), lambda j, i: (i, 0)), wspec],
        out_specs=pl.BlockSpec((TM, tn), lambda j, i: (i, j)),
        out_shape=jax.ShapeDtypeStruct((t, nb * tn), out_dtype),
        compiler_params=_params("parallel", "parallel"),
    )(a, w)


def mm_nn_residual(a, w, h, gate, name):
    t, k = a.shape
    w, wspec, nb, tn = _w_nn_spec(w, k)

    def body(a_ref, w_ref, h_ref, gt_ref, y_ref, o_ref):
        y = _dot(a_ref[...], w_ref[...])
        y_ref[...] = y
        o_ref[...] = h_ref[...] + gt_ref[...] * y

    out = pl.BlockSpec((TM, tn), lambda j, i: (i, j))
    return _pcall(
        body, name=name, grid=(nb, t // TM),
        in_specs=[pl.BlockSpec((TM, k), lambda j, i: (i, 0)), wspec, out,
                  pl.BlockSpec((None, 1, tn), lambda j, i: (_sel(i), 0, j))],
        out_specs=[out, out],
        out_shape=[jax.ShapeDtypeStruct((t, nb * tn), F32)] * 2,
        compiler_params=_params("parallel", "parallel"),
    )(a, w, h, gate)


def mm_swiglu(a, wg, wu, name):
    t, k = a.shape
    s, _, ns = wg.shape

    def body(a_ref, wg_ref, wu_ref, g_ref, u_ref, act_ref):
        av = a_ref[...]
        g = _dot(av, wg_ref[...])
        u = _dot(av, wu_ref[...])
        g_ref[...] = g.astype(BF16)
        u_ref[...] = u.astype(BF16)
        act_ref[...] = (g * _sigmoid(g) * u).astype(BF16)

    wspec = pl.BlockSpec((None, k, ns), lambda j, i: (j, 0, 0))
    out = pl.BlockSpec((TM, ns), lambda j, i: (i, j))
    return _pcall(
        body, name=name, grid=(s, t // TM),
        in_specs=[pl.BlockSpec((TM, k), lambda j, i: (i, 0)), wspec, wspec],
        out_specs=[out, out, out],
        out_shape=[jax.ShapeDtypeStruct((t, s * ns), BF16)] * 3,
        compiler_params=_params("parallel", "parallel"),
    )(a, wg, wu)


def mm_nt_cols(dy, w, name, add=None):
    t, n = dy.shape
    s, k, ns = w.shape
    tk = 1024

    def body(*refs):
        if add is None:
            dy_ref, w_ref, o_ref = refs
        else:
            dy_ref, w_ref, add_ref, o_ref = refs
        acc = _dot_nt(dy_ref[:, 0:ns], w_ref[0])
        for sh in range(1, s):
            acc += _dot_nt(dy_ref[:, sh * ns:(sh + 1) * ns], w_ref[sh])
        if add is not None:
            acc += add_ref[...]
        o_ref[...] = acc

    out = pl.BlockSpec((TM, tk), lambda j, i: (i, j))
    in_specs = [pl.BlockSpec((TM, n), lambda j, i: (i, 0)), pl.BlockSpec((s, tk, ns), lambda j, i: (0, j, 0))]
    args = [dy, w]
    if add is not None:
        in_specs.append(out)
        args.append(add)
    return _pcall(
        body, name=name, grid=(k // tk, t // TM),
        in_specs=in_specs, out_specs=out,
        out_shape=jax.ShapeDtypeStruct((t, k), F32),
        compiler_params=_params("parallel", "parallel"),
    )(*args)


def mm_nt_rows(dy, w, name):
    t, n = dy.shape
    s, ks, _ = w.shape

    def body(dy_ref, w_ref, o_ref):
        o_ref[...] = _dot_nt(dy_ref[...], w_ref[...])

    return _pcall(
        body, name=name, grid=(s, t // TM),
        in_specs=[pl.BlockSpec((TM, n), lambda j, i: (i, 0)), pl.BlockSpec((None, ks, n), lambda j, i: (j, 0, 0))],
        out_specs=pl.BlockSpec((TM, ks), lambda j, i: (i, j)),
        out_shape=jax.ShapeDtypeStruct((t, s * ks), F32),
        compiler_params=_params("parallel", "parallel"),
    )(dy, w)


def mm_nt_swiglu_bwd(df, wd, g, u, name):
    t, n = df.shape
    s, ks, _ = wd.shape

    def body(df_ref, w_ref, g_ref, u_ref, dg_ref, du_ref):
        da = _dot_nt(df_ref[...], w_ref[...])
        gv = g_ref[...].astype(F32)
        sg = _sigmoid(gv)
        silu = gv * sg
        dg_ref[...] = (da * u_ref[...].astype(F32) * (sg * (1.0 + gv * (1.0 - sg)))).astype(BF16)
        du_ref[...] = (da * silu).astype(BF16)

    blk = pl.BlockSpec((TM, ks), lambda j, i: (i, j))
    return _pcall(
        body, name=name, grid=(s, t // TM),
        in_specs=[pl.BlockSpec((TM, n), lambda j, i: (i, 0)), pl.BlockSpec((None, ks, n), lambda j, i: (j, 0, 0)),
                  blk, blk],
        out_specs=[blk, blk],
        out_shape=[jax.ShapeDtypeStruct((t, s * ks), BF16)] * 2,
        compiler_params=_params("parallel", "parallel"),
    )(df, wd, g, u)


def mm_tn(x, dy, name, col_sharded):
    t, k = x.shape
    n = dy.shape[1]
    tk = next(c for c in (1024, 1408, 512, k) if k % c == 0)
    mc = t // 2 if (t // 2) % 128 == 0 else t
    if col_sharded:
        tn = n // N_SHARD
        out_shape = jax.ShapeDtypeStruct((N_SHARD, k, tn), F32)
        out_spec = pl.BlockSpec((None, tk, tn), lambda j, kb, m: (j, kb, 0))
    else:
        tn = 1024
        out_shape = jax.ShapeDtypeStruct((k, n), F32)
        out_spec = pl.BlockSpec((tk, tn), lambda j, kb, m: (kb, j))

    def body(x_ref, dy_ref, o_ref):
        m = pl.program_id(2)
        acc = _dot_tn(x_ref[...], dy_ref[...])

        @pl.when(m == 0)
        def _():
            o_ref[...] = acc

        @pl.when(m > 0)
        def _():
            o_ref[...] += acc

    out = _pcall(
        body, name=name, grid=(n // tn, k // tk, t // mc),
        in_specs=[pl.BlockSpec((mc, tk), lambda j, kb, m: (m, kb)), pl.BlockSpec((mc, tn), lambda j, kb, m: (m, j))],
        out_specs=out_spec, out_shape=out_shape,
        compiler_params=_params("parallel", "parallel", "arbitrary"),
    )(x, dy)
    return out if col_sharded else out.reshape(N_SHARD, k // N_SHARD, n)


def _swap32(y):
    right = pltpu.roll(y, 32, 1)
    left = pltpu.roll(y, 96, 1)
    lane = lax.broadcasted_iota(jnp.int32, y.shape, 1)
    return jnp.where((lane // 32) % 2 == 0, left, right)


def _rope_tables(t, ctx_len):
    s = t - ctx_len
    pos = np.arange(s)
    row = (pos // GRID_W).astype(np.float32)
    col = (pos % GRID_W).astype(np.float32)
    half = HEAD_DIM // 2
    inv = np.power(np.float32(ROPE_THETA), -np.arange(0, half, 2, dtype=np.float32) / np.float32(half))
    ar = row[:, None] * inv
    ac = col[:, None] * inv
    cos = np.concatenate([np.cos(ar), np.cos(ar), np.cos(ac), np.cos(ac)], axis=1)
    sin = np.concatenate([-np.sin(ar), np.sin(ar), -np.sin(ac), np.sin(ac)], axis=1)
    cos = np.concatenate([np.ones((ctx_len, HEAD_DIM), np.float32), cos.astype(np.float32)], axis=0)
    sin = np.concatenate([np.zeros((ctx_len, HEAD_DIM), np.float32), sin.astype(np.float32)], axis=0)
    return jnp.asarray(cos, F32), jnp.asarray(sin, F32)


def qk_prep_fwd(proj, qg, kg, cos, sin, name):
    t = proj.shape[0]

    def body(p_ref, qg_ref, kg_ref, c_ref, s_ref, q_ref, k_ref, v_ref):
        cv = c_ref[...]
        sv = s_ref[...]
        for hd in range(Q_HEADS + KV_HEADS):
            x = p_ref[:, hd * HEAD_DIM:(hd + 1) * HEAD_DIM]
            gv = qg_ref[...] if hd < Q_HEADS else kg_ref[...]
            y = x * lax.rsqrt(jnp.mean(x * x, axis=-1, keepdims=True) + EPS) * gv
            out = y * cv + _swap32(y) * sv
            if hd < Q_HEADS:
                q_ref[:, hd * HEAD_DIM:(hd + 1) * HEAD_DIM] = (out * ATT_SCALE).astype(BF16)
            else:
                k_ref[:, (hd - Q_HEADS) * HEAD_DIM:(hd - Q_HEADS + 1) * HEAD_DIM] = out.astype(BF16)
        v_ref[...] = p_ref[:, ATT_WIDTH + KV_WIDTH:QKV_WIDTH].astype(BF16)

    vec = pl.BlockSpec((1, HEAD_DIM), lambda i: (0, 0))
    tab = pl.BlockSpec((TM, HEAD_DIM), lambda i: (i, 0))
    return _pcall(
        body, name=name, grid=(t // TM,),
        in_specs=[pl.BlockSpec((TM, QKV_WIDTH), lambda i: (i, 0)), vec, vec, tab, tab],
        out_specs=[pl.BlockSpec((TM, ATT_WIDTH), lambda i: (i, 0)), pl.BlockSpec((TM, KV_WIDTH), lambda i: (i, 0)),
                   pl.BlockSpec((TM, KV_WIDTH), lambda i: (i, 0))],
        out_shape=[jax.ShapeDtypeStruct((t, ATT_WIDTH), BF16), jax.ShapeDtypeStruct((t, KV_WIDTH), BF16),
                   jax.ShapeDtypeStruct((t, KV_WIDTH), BF16)],
        compiler_params=_params("parallel"),
    )(proj, qg, kg, cos, sin)


def qk_prep_bwd(dq, dk, dv, daux, proj, qg, kg, cos, sin, name):
    t, n_in = proj.shape
    n_aux = daux.shape[0]

    def body(dq_ref, dk_ref, dv_ref, da_ref, p_ref, qg_ref, kg_ref, c_ref, s_ref, o_ref, dqg_ref, dkg_ref):
        @pl.when(pl.program_id(0) == 0)
        def _():
            dqg_ref[...] = jnp.zeros_like(dqg_ref)
            dkg_ref[...] = jnp.zeros_like(dkg_ref)

        cv = c_ref[...]
        sv = s_ref[...]
        for hd in range(Q_HEADS + KV_HEADS):
            cols = slice(hd * HEAD_DIM, (hd + 1) * HEAD_DIM)
            x = p_ref[:, cols]
            if hd < Q_HEADS:
                gv, dyr, dg_ref = qg_ref[...], dq_ref[:, cols] * ATT_SCALE, dqg_ref
            else:
                kc = slice((hd - Q_HEADS) * HEAD_DIM, (hd - Q_HEADS + 1) * HEAD_DIM)
                gv, dyr, dg_ref = kg_ref[...], dk_ref[:, kc], dkg_ref
            r = lax.rsqrt(jnp.mean(x * x, axis=-1, keepdims=True) + EPS)
            xhat = x * r
            dy = dyr * cv + _swap32(dyr * sv)
            dg_ref[...] += jnp.sum(dy * xhat, axis=0, keepdims=True)
            dxhat = dy * gv
            o_ref[:, cols] = (r * (dxhat - xhat * jnp.mean(dxhat * xhat, axis=-1, keepdims=True))).astype(BF16)
        o_ref[:, ATT_WIDTH + KV_WIDTH:QKV_WIDTH] = dv_ref[...].astype(BF16)
        for a in range(n_aux):
            o_ref[:, QKV_WIDTH + a * AUX_WIDTH:QKV_WIDTH + (a + 1) * AUX_WIDTH] = da_ref[a]

    vec = pl.BlockSpec((1, HEAD_DIM), lambda i: (0, 0))
    tab = pl.BlockSpec((TM, HEAD_DIM), lambda i: (i, 0))
    return _pcall(
        body, name=name, grid=(t // TM,),
        in_specs=[pl.BlockSpec((TM, ATT_WIDTH), lambda i: (i, 0)), pl.BlockSpec((TM, KV_WIDTH), lambda i: (i, 0)),
                  pl.BlockSpec((TM, KV_WIDTH), lambda i: (i, 0)),
                  pl.BlockSpec((n_aux, TM, AUX_WIDTH), lambda i: (0, i, 0)),
                  pl.BlockSpec((TM, QKV_WIDTH), lambda i: (i, 0)), vec, vec, tab, tab],
        out_specs=[pl.BlockSpec((TM, n_in), lambda i: (i, 0)), vec, vec],
        out_shape=[jax.ShapeDtypeStruct((t, n_in), BF16), jax.ShapeDtypeStruct((1, HEAD_DIM), F32),
                   jax.ShapeDtypeStruct((1, HEAD_DIM), F32)],
        compiler_params=_params("arbitrary"),
    )(dq, dk, dv, daux, proj, qg, kg, cos, sin)


def _band_start(i, t):
    return pl.multiple_of(jnp.clip(i * TM - WINDOW, 0, t - BAND), WINDOW)


def _band_mask(i, start, ctx_len):
    shape = (TM, ctx_len + BAND)
    col = lax.broadcasted_iota(jnp.int32, shape, 1)
    qrow = i * TM + lax.broadcasted_iota(jnp.int32, shape, 0)
    krow = start + col - ctx_len
    band_ok = (krow >= ctx_len) & (jnp.abs(krow - qrow) <= WINDOW)
    return (col < ctx_len) | band_ok


def attention_fwd(q, k, v, sink, ctx_len, windowed, name):
    t = q.shape[0]
    d_model = ATT_WIDTH + AUX_WIDTH
    gw = GROUPS * HEAD_DIM

    def one_head(qh, kk, vv, mask, sink_val):
        s = _dot_nt(qh, kk)
        if mask is not None:
            s = jnp.where(mask, s, NEG_INF)
        m = jnp.max(s, axis=-1, keepdims=True)
        if sink_val is not None:
            m = jnp.maximum(m, sink_val)
        p = jnp.exp(s - m)
        l = jnp.sum(p, axis=-1, keepdims=True)
        if sink_val is not None:
            l = l + jnp.exp(sink_val - m)
        o = _dot(p.astype(BF16), vv) / l
        return o, m + jnp.log(l)

    def body(sink_ref, q_ref, k_ref, v_ref, o_ref, lse_ref):
        kv = pl.program_id(0)
        i = pl.program_id(1)

        def run(kk, vv, mask):
            for g in range(GROUPS):
                sink_val = sink_ref[kv * GROUPS + g] if windowed else None
                o, lse = one_head(q_ref[:, g * HEAD_DIM:(g + 1) * HEAD_DIM], kk, vv, mask, sink_val)
                o_ref[:, g * HEAD_DIM:(g + 1) * HEAD_DIM] = o.astype(BF16)
                lse_ref[g] = lse

        @pl.when(i == 0)
        def _():
            if windowed:
                o_ref[...] = jnp.zeros_like(o_ref)
                lse_ref[...] = jnp.zeros_like(lse_ref)
            else:
                run(k_ref[0:ctx_len], v_ref[0:ctx_len], None)

        @pl.when(i > 0)
        def _():
            if windowed:
                start = _band_start(i, t)
                kk = jnp.concatenate([k_ref[0:ctx_len], k_ref[pl.ds(start, BAND)]], axis=0)
                vv = jnp.concatenate([v_ref[0:ctx_len], v_ref[pl.ds(start, BAND)]], axis=0)
                run(kk, vv, _band_mask(i, start, ctx_len))
            else:
                run(k_ref[...], v_ref[...], None)

    kvspec = pl.BlockSpec((t, HEAD_DIM), lambda kv, i: (0, kv))
    return _pcall(
        body, name=name, grid=(KV_HEADS, t // TM),
        in_specs=[pl.BlockSpec(memory_space=pltpu.SMEM), pl.BlockSpec((TM, gw), lambda kv, i: (i, kv)), kvspec, kvspec],
        out_specs=[pl.BlockSpec((TM, gw), lambda kv, i: (i, kv)),
                   pl.BlockSpec((GROUPS, TM, 1), lambda kv, i: (kv, i, 0))],
        out_shape=[jax.ShapeDtypeStruct((t, d_model), BF16), jax.ShapeDtypeStruct((Q_HEADS, t, 1), F32)],
        compiler_params=_params("parallel", "parallel"),
    )(sink, q, k, v)


def attention_bwd(q, k, v, cat, lse, dcat, sink, ctx_len, windowed, name):
    t = q.shape[0]
    gw = GROUPS * HEAD_DIM

    def body(sink_ref, q_ref, k_ref, v_ref, o_ref, lse_ref, do_ref, dq_ref, dk_ref, dv_ref, dsink_ref):
        kv = pl.program_id(0)
        i = pl.program_id(1)

        @pl.when(i == 0)
        def _():
            dk_ref[...] = jnp.zeros_like(dk_ref)
            dv_ref[...] = jnp.zeros_like(dv_ref)
            dsink_ref[...] = jnp.zeros_like(dsink_ref)

        def run(kk, vv, mask, accumulate):
            for g in range(GROUPS):
                cols = slice(g * HEAD_DIM, (g + 1) * HEAD_DIM)
                qh = q_ref[:, cols]
                doh = do_ref[:, cols]
                delta = jnp.sum(doh * o_ref[:, cols].astype(F32), axis=-1, keepdims=True)
                lse_g = lse_ref[g]
                s = _dot_nt(qh, kk)
                if mask is not None:
                    s = jnp.where(mask, s, NEG_INF)
                p = jnp.exp(s - lse_g)
                dob = doh.astype(BF16)
                dp = _dot_nt(dob, vv)
                ds = (p * (dp - delta)).astype(BF16)
                dq_ref[:, cols] = _dot(ds, kk)
                accumulate(_dot_tn(ds, qh), _dot_tn(p.astype(BF16), dob))
                if windowed:
                    p_sink = jnp.exp(sink_ref[kv * GROUPS + g] - lse_g)
                    dsink_ref[g:g + 1, :] += jnp.sum(-p_sink * delta, axis=0, keepdims=True)

        @pl.when(i == 0)
        def _():
            if windowed:
                dq_ref[...] = jnp.zeros_like(dq_ref)
            else:
                def acc_ctx(dkp, dvp):
                    dk_ref[0:ctx_len] += dkp
                    dv_ref[0:ctx_len] += dvp

                run(k_ref[0:ctx_len], v_ref[0:ctx_len], None, acc_ctx)

        @pl.when(i > 0)
        def _():
            if windowed:
                start = _band_start(i, t)
                kk = jnp.concatenate([k_ref[0:ctx_len], k_ref[pl.ds(start, BAND)]], axis=0)
                vv = jnp.concatenate([v_ref[0:ctx_len], v_ref[pl.ds(start, BAND)]], axis=0)

                def acc_band(dkp, dvp):
                    dk_ref[0:ctx_len] += dkp[0:ctx_len]
                    dv_ref[0:ctx_len] += dvp[0:ctx_len]
                    dk_ref[pl.ds(start, BAND)] += dkp[ctx_len:ctx_len + BAND]
                    dv_ref[pl.ds(start, BAND)] += dvp[ctx_len:ctx_len + BAND]

                run(kk, vv, _band_mask(i, start, ctx_len), acc_band)
            else:
                def acc_all(dkp, dvp):
                    dk_ref[...] += dkp
                    dv_ref[...] += dvp

                run(k_ref[...], v_ref[...], None, acc_all)

    kvspec = pl.BlockSpec((t, HEAD_DIM), lambda kv, i: (0, kv))
    grp = pl.BlockSpec((TM, gw), lambda kv, i: (i, kv))
    return _pcall(
        body, name=name, grid=(KV_HEADS, t // TM),
        in_specs=[pl.BlockSpec(memory_space=pltpu.SMEM), grp, kvspec, kvspec, grp,
                  pl.BlockSpec((GROUPS, TM, 1), lambda kv, i: (kv, i, 0)), grp],
        out_specs=[grp, kvspec, kvspec, pl.BlockSpec((None, 8, 128), lambda kv, i: (kv, 0, 0))],
        out_shape=[jax.ShapeDtypeStruct((t, ATT_WIDTH), F32), jax.ShapeDtypeStruct((t, KV_WIDTH), F32),
                   jax.ShapeDtypeStruct((t, KV_WIDTH), F32), jax.ShapeDtypeStruct((KV_HEADS, 8, 128), F32)],
        compiler_params=_params("parallel", "arbitrary"),
    )(sink, q, k, v, cat, lse, dcat)


def _segment_pos(t, ctx_len, width):
    row = lax.broadcasted_iota(jnp.int32, (t, width), 0)
    in_ctx = row < ctx_len
    return jnp.where(in_ctx, row, row - ctx_len), jnp.where(in_ctx, ctx_len, t - ctx_len)


def _shifted(x, offset, pos, seg_len):
    t = x.shape[0]
    moved = x if offset == 0 else pltpu.roll(x, (-offset) % t, 0)
    ok = (pos + offset >= 0) & (pos + offset < seg_len)
    return jnp.where(ok, moved, 0.0)


def conv_fwd(proj, conv_w, cat, ctx_len, name):
    t = proj.shape[0]
    base = QKV_WIDTH // 128
    per = AUX_WIDTH // 128

    def body(gb_ref, gc_ref, u_ref, w_ref, cat_in, o_ref):
        del cat_in
        pos, seg = _segment_pos(t, ctx_len, 128)
        z = gc_ref[...] * u_ref[...]
        conv = (w_ref[0:1, :] * _shifted(z, -1, pos, seg) + w_ref[1:2, :] * z
                + w_ref[2:3, :] * _shifted(z, 1, pos, seg))
        o_ref[...] = (gb_ref[...] * conv).astype(BF16)

    def col(off):
        return pl.BlockSpec((t, 128), lambda j: (0, base + off * per + j))

    return _pcall(
        body, name=name, grid=(per,),
        in_specs=[col(0), col(1), col(2), pl.BlockSpec((3, 128), lambda j: (0, j)),
                  pl.BlockSpec(memory_space=pl.ANY)],
        out_specs=pl.BlockSpec((t, 128), lambda j: (0, ATT_WIDTH // 128 + j)),
        out_shape=jax.ShapeDtypeStruct(cat.shape, BF16),
        input_output_aliases={4: 0},
        compiler_params=_params("parallel"),
    )(proj, proj, proj, conv_w, cat)


def conv_bwd(dcat, proj, conv_w, ctx_len, name):
    t = proj.shape[0]
    base = QKV_WIDTH // 128
    per = AUX_WIDTH // 128

    def body(do_ref, gb_ref, gc_ref, u_ref, w_ref, da_ref, dw_ref):
        pos, seg = _segment_pos(t, ctx_len, 128)
        gc = gc_ref[...]
        u = u_ref[...]
        z = gc * u
        zm = _shifted(z, -1, pos, seg)
        zp = _shifted(z, 1, pos, seg)
        conv = w_ref[0:1, :] * zm + w_ref[1:2, :] * z + w_ref[2:3, :] * zp
        dout = do_ref[...]
        da_ref[0] = (dout * conv).astype(BF16)
        dconv = dout * gb_ref[...]
        dw_ref[0:1, :] = jnp.sum(dconv * zm, axis=0, keepdims=True)
        dw_ref[1:2, :] = jnp.sum(dconv * z, axis=0, keepdims=True)
        dw_ref[2:3, :] = jnp.sum(dconv * zp, axis=0, keepdims=True)
        dz = (w_ref[1:2, :] * dconv + w_ref[0:1, :] * _shifted(dconv, 1, pos, seg)
              + w_ref[2:3, :] * _shifted(dconv, -1, pos, seg))
        da_ref[1] = (dz * u).astype(BF16)
        da_ref[2] = (dz * gc).astype(BF16)

    def col(off):
        return pl.BlockSpec((t, 128), lambda j: (0, base + off * per + j))

    return _pcall(
        body, name=name, grid=(per,),
        in_specs=[pl.BlockSpec((t, 128), lambda j: (0, ATT_WIDTH // 128 + j)), col(0), col(1), col(2),
                  pl.BlockSpec((3, 128), lambda j: (0, j))],
        out_specs=[pl.BlockSpec((3, t, 128), lambda j: (0, 0, j)), pl.BlockSpec((3, 128), lambda j: (0, j))],
        out_shape=[jax.ShapeDtypeStruct((3, t, AUX_WIDTH), BF16), jax.ShapeDtypeStruct((3, AUX_WIDTH), F32)],
        compiler_params=_params("parallel"),
    )(dcat, proj, proj, proj, conv_w)


def _pooled(u, w, pos, seg):
    lo = jnp.clip(pos - w // 2, 0, seg)
    hi = jnp.clip(pos - w // 2 + w, 0, seg)
    inv = 1.0 / (hi - lo).astype(F32)
    total = _shifted(u, -(w // 2), pos, seg)
    for o in range(-(w // 2) + 1, w // 2):
        total = total + _shifted(u, o, pos, seg)
    return total * inv - u, inv


def pool_fwd(proj, pool_w, pool_scale, cat, ctx_len, name):
    t = proj.shape[0]

    def body(u_ref, w_ref, sc_ref, cat_in, o_ref):
        del cat_in
        pos, seg = _segment_pos(t, ctx_len, 128)
        for g, w in enumerate(POOL_WINDOWS):
            cols = slice(g * 128, (g + 1) * 128)
            pooled, _ = _pooled(u_ref[:, cols], w, pos, seg)
            mixed = _dot(pooled.astype(BF16), w_ref[g].astype(BF16))
            o_ref[:, cols] = (mixed * sc_ref[:, cols]).astype(BF16)

    return _pcall(
        body, name=name, grid=(1,),
        in_specs=[pl.BlockSpec((t, AUX_WIDTH), lambda j: (0, QKV_WIDTH // AUX_WIDTH)),
                  pl.BlockSpec((AUX_GROUPS, 128, 128), lambda j: (0, 0, 0)),
                  pl.BlockSpec((1, AUX_WIDTH), lambda j: (0, 0)), pl.BlockSpec(memory_space=pl.ANY)],
        out_specs=pl.BlockSpec((t, AUX_WIDTH), lambda j: (0, ATT_WIDTH // AUX_WIDTH)),
        out_shape=jax.ShapeDtypeStruct(cat.shape, BF16),
        input_output_aliases={3: 0},
        compiler_params=_params("arbitrary"),
    )(proj, pool_w, pool_scale, cat)


def pool_bwd(dcat, proj, pool_w, pool_scale, ctx_len, name):
    t = proj.shape[0]

    def body(do_ref, u_ref, w_ref, sc_ref, da_ref, dw_ref, dsc_ref):
        pos, seg = _segment_pos(t, ctx_len, 128)
        for g, w in enumerate(POOL_WINDOWS):
            cols = slice(g * 128, (g + 1) * 128)
            pooled, inv = _pooled(u_ref[:, cols], w, pos, seg)
            pb = pooled.astype(BF16)
            wb = w_ref[g].astype(BF16)
            mixed = _dot(pb, wb)
            dout = do_ref[:, cols]
            dsc_ref[:, cols] = jnp.sum(dout * mixed, axis=0, keepdims=True)
            dmixed = (dout * sc_ref[:, cols]).astype(BF16)
            dw_ref[g] = _dot_tn(pb, dmixed)
            dpooled = _dot_nt(dmixed, wb)
            spread = dpooled * inv
            du = _shifted(spread, w // 2, pos, seg) - dpooled
            for o in range(-(w // 2) + 1, w // 2):
                du = du + _shifted(spread, -o, pos, seg)
            da_ref[0, :, cols] = du.astype(BF16)

    return _pcall(
        body, name=name, grid=(1,),
        in_specs=[pl.BlockSpec((t, AUX_WIDTH), lambda j: (0, ATT_WIDTH // AUX_WIDTH)),
                  pl.BlockSpec((t, AUX_WIDTH), lambda j: (0, QKV_WIDTH // AUX_WIDTH)),
                  pl.BlockSpec((AUX_GROUPS, 128, 128), lambda j: (0, 0, 0)),
                  pl.BlockSpec((1, AUX_WIDTH), lambda j: (0, 0))],
        out_specs=[pl.BlockSpec((1, t, AUX_WIDTH), lambda j: (0, 0, 0)),
                   pl.BlockSpec((AUX_GROUPS, 128, 128), lambda j: (0, 0, 0)),
                   pl.BlockSpec((1, AUX_WIDTH), lambda j: (0, 0))],
        out_shape=[jax.ShapeDtypeStruct((1, t, AUX_WIDTH), BF16), jax.ShapeDtypeStruct((AUX_GROUPS, 128, 128), F32),
                   jax.ShapeDtypeStruct((1, AUX_WIDTH), F32)],
        compiler_params=_params("arbitrary"),
    )(dcat, proj, pool_w, pool_scale)


MOD_NAMES = ("shift1", "scale1", "gate1", "shift2", "scale2", "gate2")


def _layer_fwd(h, lw, weight, mods, tabs, ctx_len, windowed, tag, after_mixer=None):
    cos, sin = tabs
    xn = norm_mod_fwd(h, lw["norm1_g"], mods["shift1"], mods["scale1"], tag + "norm1_fwd")
    lw["w_in"] = weight("w_in", xn)
    proj = mm_nn(xn, lw["w_in"], tag + "w_in_fwd")
    q, k, v = qk_prep_fwd(proj, lw["q_g"], lw["k_g"], cos, sin, tag + "qk_fwd")
    cat, lse = attention_fwd(q, k, v, lw["sink"], ctx_len, windowed, tag + "attn_fwd")
    if windowed:
        cat = pool_fwd(proj, lw["pool_w"], lw["pool_scale"], cat, ctx_len, tag + "pool_fwd")
    else:
        cat = conv_fwd(proj, lw["conv_w"], cat, ctx_len, tag + "conv_fwd")
    lw["w_out"] = weight("w_out", cat)
    gate1 = mods["gate1"] if after_mixer is None else mods["gate1"] + after_mixer(cat)[0, 0]
    y, h1 = mm_nn_residual(cat, lw["w_out"], h, gate1, tag + "w_out_fwd")
    hn = norm_mod_fwd(h1, lw["norm2_g"], mods["shift2"], mods["scale2"], tag + "norm2_fwd")
    lw["w_gate"], lw["w_up"] = weight("w_gate", hn), weight("w_up", hn)
    g, u, act = mm_swiglu(hn, lw["w_gate"], lw["w_up"], tag + "ffn_in_fwd")
    lw["w_down"] = weight("w_down", act)
    f, h2 = mm_nn_residual(act, lw["w_down"], h1, mods["gate2"], tag + "w_down_fwd")
    saved = dict(h=h, xn=xn, proj=proj, q=q, k=k, v=v, cat=cat, lse=lse, y=y, h1=h1, hn=hn, g=g, u=u, act=act, f=f)
    return h2, saved


def _layer_bwd_ffn(dh, sv, lw, mods, tag):
    big, small = {}, {}
    df, dgate2 = gate_bwd(dh, sv["f"], mods["gate2"], tag + "gate2_bwd")
    dgp, du = mm_nt_swiglu_bwd(df, lw["w_down"], sv["g"], sv["u"], tag + "w_down_dgrad")
    big["w_down"] = mm_tn(sv["act"], df, tag + "w_down_wgrad", col_sharded=False)
    dhn = mm_nt_cols(dgp, lw["w_gate"], tag + "w_gate_dgrad")
    dhn = mm_nt_cols(du, lw["w_up"], tag + "w_up_dgrad", add=dhn)
    big["w_gate"] = mm_tn(sv["hn"], dgp, tag + "w_gate_wgrad", col_sharded=True)
    big["w_up"] = mm_tn(sv["hn"], du, tag + "w_up_wgrad", col_sharded=True)
    dh1, dshift2, dscale2, small["norm2_g"] = norm_mod_bwd(dhn, sv["h1"], lw["norm2_g"], mods["scale2"], dh,
                                                           tag + "norm2_bwd")
    return dh1, big, small, (dshift2, dscale2, dgate2)


def _layer_bwd_mixer(dh1, sv, lw, mods, tabs, ctx_len, windowed, tag, between=None):
    cos, sin = tabs
    big, small = {}, {}
    dy, dgate1 = gate_bwd(dh1, sv["y"], mods["gate1"], tag + "gate1_bwd")
    dcat = mm_nt_rows(dy, lw["w_out"], tag + "w_out_dgrad")
    big["w_out"] = mm_tn(sv["cat"], dy, tag + "w_out_wgrad", col_sharded=False)
    if windowed:
        daux, small["pool_w"], small["pool_scale"] = pool_bwd(dcat, sv["proj"], lw["pool_w"], lw["pool_scale"],
                                                              ctx_len, tag + "pool_bwd")
    else:
        daux, small["conv_w"] = conv_bwd(dcat, sv["proj"], lw["conv_w"], ctx_len, tag + "conv_bwd")
    dq, dk, dv, dsink = attention_bwd(sv["q"], sv["k"], sv["v"], sv["cat"], sv["lse"], dcat, lw["sink"], ctx_len,
                                      windowed, tag + "attn_bwd")
    if windowed:
        small["sink"] = dsink[:, :GROUPS, 0].reshape(Q_HEADS)
    q_g = lw["q_g"] if between is None else lw["q_g"] + between(dq)[0, 0]
    dproj, small["q_g"], small["k_g"] = qk_prep_bwd(dq, dk, dv, daux, sv["proj"], q_g, lw["k_g"], cos, sin,
                                                    tag + "qk_bwd")
    dxn = mm_nt_cols(dproj, lw["w_in"], tag + "w_in_dgrad")
    big["w_in"] = mm_tn(sv["xn"], dproj, tag + "w_in_wgrad", col_sharded=True)
    dh0, dshift1, dscale1, small["norm1_g"] = norm_mod_bwd(dxn, sv["h"], lw["norm1_g"], mods["scale1"], dh1,
                                                           tag + "norm1_bwd")
    return dh0, big, small, (dshift1, dscale1, dgate1)


def _dmod_rows(mixer_part, ffn_part):
    return jnp.concatenate([m[:, 0, :] for m in (*mixer_part, *ffn_part)], axis=1)


def _tie(mods, name, token):
    return {**mods, name: mods[name] + token[0, 0]}


def _place():
    x, y, c = lax.axis_index("x"), lax.axis_index("y"), lax.axis_index("c")
    chips = [(1 - x, y), (x, 1 - y), (1 - x, 1 - y)]
    return x, y, c, chips


def _remote(src, dst, send_sem, recv_sem, to):
    return pltpu.make_async_remote_copy(src_ref=src, dst_ref=dst, send_sem=send_sem, recv_sem=recv_sem,
                                        device_id=to, device_id_type=MESH)


ANY = pl.BlockSpec(memory_space=pl.ANY)


def all_gather8(x_shard, name):
    m_per, n = x_shard.shape

    def body(x_ref, out_ref, send_sems, recv_sems, local_sem):
        x, y, c, chips = _place()
        me, sibling = (x, y, c), (x, y, 1 - c)

        def rows(px, py, pc):
            return out_ref.at[pl.ds((4 * px + 2 * py + pc) * m_per, m_per), :]

        def copy(k, block, to, src=None):
            return _remote(rows(*block) if src is None else src, rows(*block), send_sems.at[k], recv_sems.at[k], to)

        mine = pltpu.make_async_copy(x_ref, rows(*me), local_sem)
        mine.start()
        first = [copy(0, me, sibling, src=x_ref)]
        first += [copy(1 + j, me, (*chip, c), src=x_ref) for j, chip in enumerate(chips)]
        for cp in first:
            cp.start()
        passed = [copy(4 + j, (*chip, c), sibling) for j, chip in enumerate(chips)]
        for j, chip in enumerate(chips):
            copy(1 + j, (*chip, c), me).wait_recv()
            passed[j].start()
        copy(0, sibling, me).wait_recv()
        for j, chip in enumerate(chips):
            copy(4 + j, (*chip, 1 - c), me).wait_recv()
        for cp in first + passed:
            cp.wait_send()
        mine.wait()

    return _pcall(
        body, name=name,
        out_shape=jax.ShapeDtypeStruct((N_DEV * m_per, n), x_shard.dtype),
        in_specs=[pl.BlockSpec(memory_space=pltpu.VMEM)],
        out_specs=pl.BlockSpec(memory_space=pltpu.VMEM),
        scratch_shapes=[pltpu.SemaphoreType.DMA((7,)), pltpu.SemaphoreType.DMA((7,)), pltpu.SemaphoreType.DMA],
        compiler_params=pltpu.CompilerParams(vmem_limit_bytes=VMEM_LIMIT),
    )(x_shard)


def _row_tile(rows):
    for tr in (256, 128, 64, 32, 16):
        if rows % tr == 0:
            return tr
    return rows


def place_shard(w, p, name):
    r, cols = w.shape
    tr = _row_tile(r)

    def body(p_ref, w_ref, o_ref):
        del p_ref
        o_ref[...] = w_ref[...].astype(BF16)

    return _pcall(
        body, name=name,
        grid_spec=pltpu.PrefetchScalarGridSpec(
            num_scalar_prefetch=1, grid=(r // tr,),
            in_specs=[pl.BlockSpec((tr, cols), lambda i, p_ref: (i, 0))],
            out_specs=pl.BlockSpec((None, tr, cols), lambda i, p_ref: (p_ref[0], i, 0))),
        out_shape=jax.ShapeDtypeStruct((N_SHARD, r, cols), BF16),
        compiler_params=_params("parallel"),
    )(p.reshape(1).astype(jnp.int32), w)


def gather_weight_shards(stacks, after):
    n = len(stacks)

    def body(*refs):
        outs = refs[n + 1:2 * n + 1]
        send_sems, recv_sems = refs[2 * n + 1:]
        x, y, c, chips = _place()
        p = 2 * x + y
        sibling = (x, y, 1 - c)
        started = []
        for i in range(n):
            hr = outs[i].shape[1] // 2
            mine = outs[i].at[p, pl.ds(c * hr, hr)]
            for j, chip in enumerate(chips):
                cp = _remote(mine, mine, send_sems.at[i, j], recv_sems.at[i, j], (*chip, c))
                cp.start()
                started.append(cp)
        for i in range(n):
            hr = outs[i].shape[1] // 2
            for j, (px, py) in enumerate(chips):
                half = outs[i].at[2 * px + py, pl.ds(c * hr, hr)]
                _remote(half, half, send_sems.at[i, j], recv_sems.at[i, j], (px, py, c)).wait_recv()
                fwd = _remote(half, half, send_sems.at[i, 3 + j], recv_sems.at[i, 3 + j], sibling)
                fwd.start()
                started.append(fwd)
        for i in range(n):
            hr = outs[i].shape[1] // 2
            for j, (px, py) in enumerate(chips):
                other = outs[i].at[2 * px + py, pl.ds((1 - c) * hr, hr)]
                _remote(other, other, send_sems.at[i, 3 + j], recv_sems.at[i, 3 + j], sibling).wait_recv()
        for cp in started:
            cp.wait_send()

    return _pcall(
        body, name="gather_weight_shards",
        out_shape=[jax.ShapeDtypeStruct(s.shape, s.dtype) for s in stacks],
        in_specs=[ANY] * (n + 1), out_specs=[ANY] * n,
        input_output_aliases={i: i for i in range(n)},
        scratch_shapes=[pltpu.SemaphoreType.DMA((n, 6)), pltpu.SemaphoreType.DMA((n, 6))],
    )(*stacks, after)


HBM_SPEC = pl.BlockSpec(memory_space=pltpu.HBM)
SEM_SPEC = pl.BlockSpec(memory_space=pltpu.SEMAPHORE)
EFFECT = pltpu.SideEffectType.DATAFLOW_SIDE_EFFECTING


def split_start(name, copies, n_sems, arrays, after):
    m = len(arrays)

    def body(*refs):
        for cp in copies(refs[:m], refs[m + 1], refs[m + 2]):
            cp.start()
        refs[-1][...] = jnp.zeros_like(refs[-1])

    res = _pcall(
        body, name=name,
        out_shape=(pltpu.SemaphoreType.DMA((n_sems,)), pltpu.SemaphoreType.DMA((n_sems,)),
                   *[pltpu.HBM(s.shape, s.dtype) for s in arrays], jax.ShapeDtypeStruct((8, 128), F32)),
        in_specs=[HBM_SPEC] * m + [ANY],
        out_specs=(SEM_SPEC, SEM_SPEC, *[HBM_SPEC] * m, pl.BlockSpec(memory_space=pltpu.VMEM)),
        input_output_aliases={i: 2 + i for i in range(m)},
        compiler_params=pltpu.CompilerParams(has_side_effects=EFFECT),
    )(*[pltpu.with_memory_space_constraint(s, pltpu.HBM) for s in arrays], after)
    return res[0], res[1], list(res[2:2 + m]), res[2 + m]


def split_wait(name, copies, send_sems, recv_sems, arrays, after):
    m = len(arrays)

    def body(*refs):
        for cp in copies(refs[:m], refs[m], refs[m + 1]):
            cp.wait_send()
            cp.wait_recv()

    return _pcall(
        body, name=name,
        out_shape=[pltpu.HBM(s.shape, s.dtype) for s in arrays],
        in_specs=[HBM_SPEC] * m + [SEM_SPEC, SEM_SPEC, ANY],
        out_specs=[HBM_SPEC] * m,
        input_output_aliases={i: i for i in range(m)},
        compiler_params=pltpu.CompilerParams(has_side_effects=EFFECT),
    )(*arrays, send_sems, recv_sems, after)


def gather_copies(n):
    def copies(stacks, send_sems, recv_sems):
        x, y, c, chips = _place()
        p = 2 * x + y
        return [_remote(stacks[i].at[p], stacks[i].at[p], send_sems.at[3 * i + j], recv_sems.at[3 * i + j], (*chip, c))
                for i in range(n) for j, chip in enumerate(chips)]
    return copies


def gather_half_copies(n):
    def copies(stacks, send_sems, recv_sems):
        x, y, c, chips = _place()
        p = 2 * x + y
        out = []
        for i in range(n):
            hr = stacks[i].shape[1] // 2
            mine = stacks[i].at[p, pl.ds(c * hr, hr)]
            out += [_remote(mine, mine, send_sems.at[3 * i + j], recv_sems.at[3 * i + j], (*chip, c))
                    for j, chip in enumerate(chips)]
        return out
    return copies


def forward_half_copies(n):
    def copies(stacks, send_sems, recv_sems):
        x, y, c, chips = _place()
        out = []
        for i in range(n):
            hr = stacks[i].shape[1] // 2
            for j, (px, py) in enumerate(chips):
                half = stacks[i].at[2 * px + py, pl.ds(c * hr, hr)]
                out.append(_remote(half, half, send_sems.at[3 * i + j], recv_sems.at[3 * i + j], (x, y, 1 - c)))
        return out
    return copies


def sibling_half_copies(n):
    def copies(refs, send_sems, recv_sems):
        x, y, c, _ = _place()
        out = []
        for i in range(n):
            hr = refs[n + i].shape[1]
            out.append(_remote(refs[i].at[:, pl.ds((1 - c) * hr, hr), :], refs[n + i], send_sems.at[i], recv_sems.at[i],
                               (x, y, 1 - c)))
        return out
    return copies


def chip_quarter_copies(n):
    def copies(refs, send_sems, recv_sems):
        _, _, c, chips = _place()
        return [_remote(refs[i].at[2 * px + py], refs[n + i].at[j], send_sems.at[3 * i + j], recv_sems.at[3 * i + j],
                        (px, py, c)) for i in range(n) for j, (px, py) in enumerate(chips)]
    return copies


def exchange_final_halves(grads, name):
    n = len(grads)

    def body(*refs):
        outs = refs[n:2 * n]
        send_sems, recv_sems = refs[2 * n:]
        x, y, c, _ = _place()
        sends = []
        for i in range(n):
            hr = outs[i].shape[0] // 2
            mine = outs[i].at[pl.ds(c * hr, hr)]
            cp = _remote(mine, mine, send_sems.at[i], recv_sems.at[i], (x, y, 1 - c))
            cp.start()
            sends.append(cp)
        for i in range(n):
            hr = outs[i].shape[0] // 2
            other = outs[i].at[pl.ds((1 - c) * hr, hr)]
            _remote(other, other, send_sems.at[i], recv_sems.at[i], (x, y, 1 - c)).wait_recv()
        for cp in sends:
            cp.wait_send()

    return _pcall(
        body, name=name,
        out_shape=[jax.ShapeDtypeStruct(g.shape, g.dtype) for g in grads],
        in_specs=[ANY] * n, out_specs=[ANY] * n,
        input_output_aliases={i: i for i in range(n)},
        scratch_shapes=[pltpu.SemaphoreType.DMA((n,)), pltpu.SemaphoreType.DMA((n,))],
    )(*grads)


def add_own_half(g, recv, c, name):
    s, r, cols = g.shape
    hr = r // 2
    tr = _row_tile(hr)
    nb = hr // tr

    def body(c_ref, g_ref, r_ref, o_ref):
        del c_ref
        o_ref[...] = (g_ref[...] + r_ref[...]).astype(BF16)

    return _pcall(
        body, name=name,
        grid_spec=pltpu.PrefetchScalarGridSpec(
            num_scalar_prefetch=1, grid=(s, nb),
            in_specs=[pl.BlockSpec((None, tr, cols), lambda j, i, c_ref: (j, c_ref[0] * nb + i, 0)),
                      pl.BlockSpec((None, tr, cols), lambda j, i, c_ref: (j, i, 0))],
            out_specs=pl.BlockSpec((None, tr, cols), lambda j, i, c_ref: (j, i, 0))),
        out_shape=jax.ShapeDtypeStruct((s, hr, cols), BF16),
        compiler_params=_params("parallel", "parallel"),
    )(c.reshape(1).astype(jnp.int32), g, recv)


def add_quarters(own, others, p, c, name, deps=()):
    _, hr, cols = own.shape
    tr = _row_tile(hr)
    nb = hr // tr

    def body(pc_ref, own_ref, oth_ref, *rest):
        o_ref = rest[-1]
        acc = own_ref[...].astype(F32) + oth_ref[0].astype(F32)
        acc = acc + oth_ref[1].astype(F32)
        o_ref[...] = acc + oth_ref[2].astype(F32)

    return _pcall(
        body, name=name,
        grid_spec=pltpu.PrefetchScalarGridSpec(
            num_scalar_prefetch=1, grid=(nb,),
            in_specs=[pl.BlockSpec((None, tr, cols), lambda i, pc: (pc[0], i, 0)),
                      pl.BlockSpec((3, tr, cols), lambda i, pc: (0, i, 0))] + [ANY] * len(deps),
            out_specs=pl.BlockSpec((tr, cols), lambda i, pc: (pc[1] * nb + i, 0))),
        out_shape=jax.ShapeDtypeStruct((2 * hr, cols), F32),
        compiler_params=_params("parallel"),
    )(jnp.stack([p, c]).astype(jnp.int32), own, others, *deps)


def _adamw_math(w, g, m, v):
    m2 = ADAM_B1 * m + (1.0 - ADAM_B1) * g
    v2 = ADAM_B2 * v + (1.0 - ADAM_B2) * (g * g)
    m_hat = m2 / (1.0 - ADAM_B1 ** ADAM_STEP)
    v_hat = v2 / (1.0 - ADAM_B2 ** ADAM_STEP)
    delta = -ADAM_LR * (m_hat / (jnp.sqrt(v_hat) + ADAM_EPS) + ADAM_WD * w)
    return delta, m2, v2


def adamw(w, g, m, v, name):
    r, cols = w.shape
    tr = _row_tile(r)

    def body(w_ref, g_ref, m_ref, v_ref, go_ref, d_ref, m2_ref, v2_ref):
        g = g_ref[...]
        go_ref[...] = g
        d_ref[...], m2_ref[...], v2_ref[...] = _adamw_math(w_ref[...], g, m_ref[...], v_ref[...])

    blk = pl.BlockSpec((tr, cols), lambda i: (i, 0))
    return _pcall(
        body, name=name, grid=(r // tr,),
        in_specs=[blk] * 4, out_specs=[blk] * 4,
        out_shape=[jax.ShapeDtypeStruct((r, cols), F32)] * 4,
        compiler_params=_params("parallel"),
    )(w, g, m, v)


def small_update(gathered, w, m, v):
    nd, r, lanes = gathered.shape

    def body(ga_ref, w_ref, m_ref, v_ref, g_ref, d_ref, m2_ref, v2_ref):
        g = ga_ref[0] + ga_ref[1]
        for dev in range(2, nd):
            g = g + ga_ref[dev]
        g_ref[...] = g
        d_ref[...], m2_ref[...], v2_ref[...] = _adamw_math(w_ref[...], g, m_ref[...], v_ref[...])

    return _pcall(
        body, name="small_update",
        out_shape=[jax.ShapeDtypeStruct((r, lanes), F32)] * 4,
        compiler_params=pltpu.CompilerParams(vmem_limit_bytes=VMEM_LIMIT),
    )(gathered, w, m, v)


def mod_fwd(c16, w_mod, b_mod, name):
    d, ns = w_mod.shape
    tn = 512

    def body(c_ref, w_ref, b_ref, o_ref):
        cv = c_ref[...]
        sc = (cv * _sigmoid(cv)).astype(BF16)
        o_ref[...] = _dot(sc, w_ref[...].astype(BF16)) + b_ref[...]

    return _pcall(
        body, name=name, grid=(ns // tn,),
        in_specs=[pl.BlockSpec((16, d), lambda j: (0, 0)), pl.BlockSpec((d, tn), lambda j: (0, j)),
                  pl.BlockSpec((1, tn), lambda j: (0, j))],
        out_specs=pl.BlockSpec((16, tn), lambda j: (0, j)),
        out_shape=jax.ShapeDtypeStruct((16, ns), F32),
        compiler_params=_params("parallel"),
    )(c16, w_mod, b_mod)


def wmod_update(c16, dmod16, w, m, v, name):
    d, ns = w.shape
    tn = 256

    def body(c_ref, dm_ref, w_ref, m_ref, v_ref, g_ref, d_ref, m2_ref, v2_ref):
        cv = c_ref[...]
        sc = (cv * _sigmoid(cv)).astype(BF16)
        g = _dot_tn(sc, dm_ref[...].astype(BF16))
        g_ref[...] = g
        d_ref[...], m2_ref[...], v2_ref[...] = _adamw_math(w_ref[...], g, m_ref[...], v_ref[...])

    blk = pl.BlockSpec((d, tn), lambda j: (0, j))
    return _pcall(
        body, name=name, grid=(ns // tn,),
        in_specs=[pl.BlockSpec((16, d), lambda j: (0, 0)), pl.BlockSpec((16, tn), lambda j: (0, j)), blk, blk, blk],
        out_specs=[blk] * 4,
        out_shape=[jax.ShapeDtypeStruct((d, ns), F32)] * 4,
        compiler_params=_params("parallel"),
    )(c16, dmod16, w, m, v)


def cctx_partial(dm0, w0, dm1, w1):
    d, ns = w0.shape
    tn = 512

    def body(dm0_ref, w0_ref, dm1_ref, w1_ref, o_ref):
        @pl.when(pl.program_id(0) == 0)
        def _():
            o_ref[...] = jnp.zeros_like(o_ref)

        for dm_ref, w_ref in ((dm0_ref, w0_ref), (dm1_ref, w1_ref)):
            tot = jnp.sum(dm_ref[...], axis=0, keepdims=True)
            lhs = jnp.broadcast_to(tot, (8, tn)).astype(BF16)
            o_ref[...] += _dot_nt(lhs, w_ref[...].astype(BF16))

    dspec = pl.BlockSpec((8, tn), lambda j: (0, j))
    wspec = pl.BlockSpec((d, tn), lambda j: (0, j))
    return _pcall(
        body, name="cctx_partial", grid=(ns // tn,),
        in_specs=[dspec, wspec, dspec, wspec],
        out_specs=pl.BlockSpec((8, d), lambda j: (0, 0)),
        out_shape=jax.ShapeDtypeStruct((8, d), F32),
        compiler_params=_params("arbitrary"),
    )(dm0, w0, dm1, w1)


def cctx_update(parts, c_ctx, m, v):
    d = c_ctx.shape[1]

    def body(p_ref, c_ref, m_ref, v_ref, g_ref, d_ref, m2_ref, v2_ref):
        tot = p_ref[0:1, :] + p_ref[1:2, :]
        tot = tot + p_ref[2:3, :]
        tot = tot + p_ref[3:4, :]
        cv = c_ref[...]
        sg = _sigmoid(cv)
        g = tot * (sg * (1.0 + cv * (1.0 - sg)))
        g_ref[...] = g
        d_ref[...], m2_ref[...], v2_ref[...] = _adamw_math(cv, g, m_ref[...], v_ref[...])

    return _pcall(
        body, name="cctx_update",
        out_shape=[jax.ShapeDtypeStruct((1, d), F32)] * 4,
    )(parts, c_ctx, m, v)


WEIGHT_NAMES = (
    "c_ctx", "l0_norm1_g", "l0_w_mod", "l0_b_mod", "l0_w_in", "l0_q_norm_g", "l0_k_norm_g", "l0_conv_w", "l0_w_out",
    "l0_norm2_g", "l0_w_gate", "l0_w_up", "l0_w_down", "l1_norm1_g", "l1_w_mod", "l1_b_mod", "l1_w_in",
    "l1_q_norm_g", "l1_k_norm_g", "l1_sink", "l1_pool_w", "l1_pool_scale", "l1_w_out", "l1_norm2_g", "l1_w_gate",
    "l1_w_up", "l1_w_down", "final_norm_g")
BIG_NAMES = ("w_in", "w_out", "w_gate", "w_up", "w_down")
SMALL_SLOTS = tuple(
    [(f"l{l}_{nm}", par and f"l{l}_{nm}") for l in (0, 1)
     for nm, par in (("dmod_lat", False), ("dmod_ctx", False), ("b_mod", True), ("norm1_g", True), ("norm2_g", True),
                     ("q_norm_g", True), ("k_norm_g", True))]
    + [("l0_conv_w", None), ("l1_sink", "l1_sink"), ("l1_pool_w", "l1_pool_w"), ("l1_pool_scale", "l1_pool_scale"),
       ("final_norm_g", "final_norm_g")])


def _pack(values, sizes):
    parts = []
    for (name, _), size in zip(SMALL_SLOTS, sizes):
        v = values.get(name)
        padded = -(-size // 128) * 128
        v = jnp.zeros((padded,), F32) if v is None else jnp.pad(v.reshape(-1).astype(F32), (0, padded - size))
        parts.append(v)
    total = sum(p.shape[0] for p in parts)
    parts.append(jnp.zeros((-(-total // 1024) * 1024 - total,), F32))
    return jnp.concatenate(parts).reshape(-1, 128)


def _offsets(sizes):
    offs, o = {}, 0
    for (name, _), size in zip(SMALL_SLOTS, sizes):
        offs[name] = (o, size)
        o += -(-size // 128) * 128
    return offs


def kernel(x, c, ctx, c_ctx, l0_norm1_g, l0_w_mod, l0_b_mod, l0_w_in, l0_q_norm_g, l0_k_norm_g, l0_conv_w, l0_w_out, l0_norm2_g, l0_w_gate, l0_w_up, l0_w_down, l1_norm1_g, l1_w_mod, l1_b_mod, l1_w_in, l1_q_norm_g, l1_k_norm_g, l1_sink, l1_pool_w, l1_pool_scale, l1_w_out, l1_norm2_g, l1_w_gate, l1_w_up, l1_w_down, final_norm_g, loss_target, m_c_ctx, m_l0_norm1_g, m_l0_w_mod, m_l0_b_mod, m_l0_w_in, m_l0_q_norm_g, m_l0_k_norm_g, m_l0_conv_w, m_l0_w_out, m_l0_norm2_g, m_l0_w_gate, m_l0_w_up, m_l0_w_down, m_l1_norm1_g, m_l1_w_mod, m_l1_b_mod, m_l1_w_in, m_l1_q_norm_g, m_l1_k_norm_g, m_l1_sink, m_l1_pool_w, m_l1_pool_scale, m_l1_w_out, m_l1_norm2_g, m_l1_w_gate, m_l1_w_up, m_l1_w_down, m_final_norm_g, v_c_ctx, v_l0_norm1_g, v_l0_w_mod, v_l0_b_mod, v_l0_w_in, v_l0_q_norm_g, v_l0_k_norm_g, v_l0_conv_w, v_l0_w_out, v_l0_norm2_g, v_l0_w_gate, v_l0_w_up, v_l0_w_down, v_l1_norm1_g, v_l1_w_mod, v_l1_b_mod, v_l1_w_in, v_l1_q_norm_g, v_l1_k_norm_g, v_l1_sink, v_l1_pool_w, v_l1_pool_scale, v_l1_w_out, v_l1_norm2_g, v_l1_w_gate, v_l1_w_up, v_l1_w_down, v_final_norm_g):
    a = dict(locals())
    xi, yi, ci = lax.axis_index("x"), lax.axis_index("y"), lax.axis_index("c")
    p = 2 * xi + yi
    me = 4 * xi + 2 * yi + ci
    d = x.shape[-1]
    conv_cols = l0_conv_w.shape[1]

    row0 = jnp.concatenate([c, jnp.pad(l0_conv_w, ((0, 0), (0, d - conv_cols))), jnp.zeros((4, d), F32)], axis=0)
    gathered = all_gather8(row0, "gather_cond").reshape(N_DEV, 8, d)
    c_all = gathered[:, 0]
    conv_w = gathered[0::2, 1:4, :conv_cols].transpose(1, 0, 2).reshape(3, N_SHARD * conv_cols)
    c16_fwd = jnp.concatenate([c_all, c_ctx[None], jnp.zeros((7, d), F32)], axis=0)
    c16_bwd = jnp.concatenate([c_all, jnp.broadcast_to(c_ctx[None], (8, d))], axis=0)

    ns_mod = l0_w_mod.shape[1]
    mod_parts = [mod_fwd(c16_fwd, a[f"l{l}_w_mod"], lax.dynamic_slice(a[f"l{l}_b_mod"], (p * ns_mod,), (ns_mod,))[None],
                         f"l{l}_mod_fwd") for l in (0, 1)]
    modg = all_gather8(jnp.concatenate(mod_parts, axis=0), "gather_mod").reshape(N_DEV, 2, 16, ns_mod)[0::2]
    mod_full = modg.transpose(1, 2, 0, 3).reshape(2, 16, N_SHARD * ns_mod)
    mods = []
    for l in (0, 1):
        lat = lax.dynamic_index_in_dim(mod_full[l], me, axis=0, keepdims=False)
        cx = mod_full[l, 8]
        mods.append({nm: jnp.stack([cx[j * d:(j + 1) * d], lat[j * d:(j + 1) * d]])[:, None, :]
                     for j, nm in enumerate(MOD_NAMES)})

    placed = {(l, nm): place_shard(a[f"l{l}_{nm}"], p, f"l{l}_{nm}_place") for l in (0, 1) for nm in BIG_NAMES}
    ready = {(0, "w_in"): gather_weight_shards([placed[0, "w_in"]], mod_full)[0]}
    groups = (((0, "w_out"), (0, "w_gate"), (0, "w_up")), ((0, "w_down"),), ((1, "w_in"), (1, "w_out")),
              ((1, "w_gate"), (1, "w_up"), (1, "w_down")))
    staged = len(groups) - 1
    flights, after = [], ready[0, "w_in"]
    for gi, grp in enumerate(groups):
        first_copies = gather_half_copies if gi == staged else gather_copies
        flights.append(split_start(f"gather_start_{gi}", first_copies(len(grp)), 3 * len(grp),
                                   [placed[k] for k in grp], after))
        after = flights[-1][3]

    def forward_staged(after):
        n = len(groups[staged])
        send_sems, recv_sems, in_flight, _ = flights[staged]
        landed = split_wait(f"gather_wait_{staged}", gather_half_copies(n), send_sems, recv_sems, in_flight, after)
        flights[staged] = split_start("gather_forward_start", forward_half_copies(n), 3 * n, landed, after)
        return flights[staged][3]

    def weight_of(l):
        def weight(nm, after):
            if (l, nm) not in ready:
                gi = next(i for i, grp in enumerate(groups) if (l, nm) in grp)
                send_sems, recv_sems, in_flight, _ = flights[gi]
                name, copies = ("gather_forward_wait", forward_half_copies) if gi == staged else (
                    f"gather_wait_{gi}", gather_copies)
                landed = split_wait(name, copies(len(groups[gi])), send_sems, recv_sems, in_flight, after)
                ready.update(zip(groups[gi], landed))
            return ready[l, nm]
        return weight

    layers = [dict(norm1_g=a[f"l{l}_norm1_g"][None], norm2_g=a[f"l{l}_norm2_g"][None],
                   q_g=a[f"l{l}_q_norm_g"][None], k_g=a[f"l{l}_k_norm_g"][None]) for l in (0, 1)]
    layers[0].update(conv_w=conv_w, sink=jnp.zeros((Q_HEADS,), F32))
    layers[1].update(sink=l1_sink, pool_w=l1_pool_w, pool_scale=l1_pool_scale[None])

    ctx_len = ctx.shape[1]
    h = jnp.concatenate([ctx[0], x[0]], axis=0)
    tabs = _rope_tables(h.shape[0], ctx_len)
    h, saved0 = _layer_fwd(h, layers[0], weight_of(0), _tie(mods[0], "shift1", flights[-1][3]), tabs, ctx_len, False,
                           "l0_")
    h, saved1 = _layer_fwd(h, layers[1], weight_of(1), mods[1], tabs, ctx_len, True, "l1_", forward_staged)
    loss_tile, dh, dfg = final_loss(h, final_norm_g[None], loss_target[0])

    keys = [[f"l{l}_{nm}" for nm in BIG_NAMES] for l in (0, 1)]
    n_big = len(BIG_NAMES)
    dh, ffn_big, ffn_small, ffn_mod = _layer_bwd_ffn(dh, saved1, layers[1], mods[1], "l1_")
    dh, mix_big, mix_small, mix_mod = _layer_bwd_mixer(dh, saved1, layers[1], mods[1], tabs, ctx_len, True, "l1_")
    bigs, smalls, dmods = [None, {**ffn_big, **mix_big}], [None, {**ffn_small, **mix_small}], [None, None]
    dmods[1] = _dmod_rows(mix_mod, ffn_mod)
    grads1 = [bigs[1][nm] for nm in BIG_NAMES]
    lands = [lax.empty((g.shape[0], g.shape[1] // 2, g.shape[2]), g.dtype) for g in grads1]
    send_a, recv_a, fly_a, token_a = split_start("l1_sibling_halves_start", sibling_half_copies(n_big), n_big,
                                                 grads1 + lands, dh)
    dh, ffn_big, ffn_small, ffn_mod = _layer_bwd_ffn(dh, saved0, layers[0], _tie(mods[0], "gate2", token_a), "l0_")
    fly_a = split_wait("l1_sibling_halves_wait", sibling_half_copies(n_big), send_a, recv_a, fly_a, dh)
    chip_sums1 = [add_own_half(g, r, ci, k + "_add_halves")
                  for k, g, r in zip(keys[1], fly_a[:n_big], fly_a[n_big:])]
    lands = [lax.empty((3,) + s.shape[1:], s.dtype) for s in chip_sums1]
    send_b, recv_b, fly_b, token_b = split_start("l1_chip_quarters_start", chip_quarter_copies(n_big), 3 * n_big,
                                                 chip_sums1 + lands, dh)
    ffn_names = ("w_gate", "w_up", "w_down")
    n_ffn = len(ffn_names)
    ffn_keys = ["l0_" + nm for nm in ffn_names]
    grads0f = [ffn_big[nm] for nm in ffn_names]
    lands = [lax.empty((g.shape[0], g.shape[1] // 2, g.shape[2]), g.dtype) for g in grads0f]
    send_c, recv_c, fly_c, token_c = split_start("l0_ffn_sibling_halves_start", sibling_half_copies(n_ffn), n_ffn,
                                                 grads0f + lands, token_b)
    ffn_flight = []

    def between(dq):
        landed = split_wait("l0_ffn_sibling_halves_wait", sibling_half_copies(n_ffn), send_c, recv_c, fly_c, dq)
        sums = [add_own_half(g, r, ci, k + "_add_halves") for k, g, r in zip(ffn_keys, landed[:n_ffn], landed[n_ffn:])]
        slots = [lax.empty((3,) + s.shape[1:], s.dtype) for s in sums]
        ffn_flight.extend(split_start("l0_ffn_chip_quarters_start", chip_quarter_copies(n_ffn), 3 * n_ffn,
                                      sums + slots, dq))
        return ffn_flight[3]

    dh, mix_big, mix_small, mix_mod = _layer_bwd_mixer(dh, saved0, layers[0], _tie(mods[0], "gate1", token_c), tabs,
                                                       ctx_len, False, "l0_", between)
    fly_b = split_wait("l1_chip_quarters_wait", chip_quarter_copies(n_big), send_b, recv_b, fly_b, dh)
    fly_d = split_wait("l0_ffn_chip_quarters_wait", chip_quarter_copies(n_ffn), ffn_flight[0], ffn_flight[1],
                       ffn_flight[2], dh)
    bigs[0], smalls[0], dmods[0] = {**ffn_big, **mix_big}, {**ffn_small, **mix_small}, _dmod_rows(mix_mod, ffn_mod)
    dx = dh[ctx_len:]
    chip_sums = dict(zip(keys[1] + ffn_keys, list(fly_b[:n_big]) + list(fly_d[:n_ffn])))
    quarters = dict(zip(keys[1] + ffn_keys, list(fly_b[n_big:]) + list(fly_d[n_ffn:])))

    partial = {"l0_conv_w": smalls[0]["conv_w"], "l1_sink": smalls[1]["sink"], "l1_pool_w": smalls[1]["pool_w"],
               "l1_pool_scale": smalls[1]["pool_scale"], "final_norm_g": dfg}
    for l in (0, 1):
        partial.update({f"l{l}_dmod_lat": dmods[l][1], f"l{l}_dmod_ctx": dmods[l][0],
                        f"l{l}_b_mod": dmods[l][0] + dmods[l][1], f"l{l}_norm1_g": smalls[l]["norm1_g"],
                        f"l{l}_norm2_g": smalls[l]["norm2_g"], f"l{l}_q_norm_g": smalls[l]["q_g"],
                        f"l{l}_k_norm_g": smalls[l]["k_g"]})
    sizes = [int(np.prod(partial[name].shape)) for name, _ in SMALL_SLOTS]
    offs = _offsets(sizes)

    last_keys = ["l0_w_in", "l0_w_out"]
    grads_last = [bigs[0]["w_in"], bigs[0]["w_out"]]
    lands = [lax.empty((g.shape[0], g.shape[1] // 2, g.shape[2]), g.dtype) for g in grads_last]
    send_e, recv_e, fly_e, token_e = split_start("l0_last_sibling_halves_start", sibling_half_copies(2), 2,
                                                 grads_last + lands, dh)
    gpack = _pack(partial, sizes) + token_e[0:1, 0:1]
    rows = gpack.shape[0]
    small_all = all_gather8(gpack, "gather_small").reshape(N_DEV, rows, 128)
    packs = [_pack({name: a[pre + par] for name, par in SMALL_SLOTS if par}, sizes) for pre in ("", "m_", "v_")]
    small_out = [t.reshape(-1) for t in small_update(small_all, *packs)]
    flat_all = small_all.reshape(N_DEV, rows * 128)

    def slot(flat, name, shape):
        o, size = offs[name]
        return flat[o:o + size].reshape(shape)

    results = {}
    for name, par in SMALL_SLOTS:
        if par:
            results[par] = tuple(slot(t, name, a[par].shape) for t in small_out)

    ctx_rows = []
    for l in (0, 1):
        o_lat, o_ctx = offs[f"l{l}_dmod_lat"][0], offs[f"l{l}_dmod_ctx"][0]
        lat = lax.dynamic_slice(flat_all, (0, o_lat + p * ns_mod), (N_DEV, ns_mod))
        cxr = lax.dynamic_slice(flat_all, (0, o_ctx + p * ns_mod), (N_DEV, ns_mod))
        ctx_rows.append(cxr)
        results[f"l{l}_w_mod"] = tuple(wmod_update(c16_bwd, jnp.concatenate([lat, cxr], axis=0), a[f"l{l}_w_mod"],
                                                   a[f"m_l{l}_w_mod"], a[f"v_l{l}_w_mod"], f"l{l}_w_mod_update"))
    part = cctx_partial(ctx_rows[0], l0_w_mod, ctx_rows[1], l1_w_mod)
    parts4 = all_gather8(part, "gather_cctx").reshape(N_DEV, 8, d)[0::2, 0]
    results["c_ctx"] = tuple(t[0] for t in cctx_update(parts4, c_ctx[None], m_c_ctx[None], v_c_ctx[None]))

    gconv = lax.dynamic_slice(slot(small_out[0], "l0_conv_w", (3, N_SHARD * conv_cols)), (0, p * conv_cols),
                              (3, conv_cols))
    results["l0_conv_w"] = tuple(adamw(l0_conv_w, gconv, m_l0_conv_w, v_l0_conv_w, "l0_conv_w_adamw"))

    fly_e = split_wait("l0_last_sibling_halves_wait", sibling_half_copies(2), send_e, recv_e, fly_e,
                       results["c_ctx"][0])
    sums_last = [add_own_half(g, r, ci, k + "_add_halves") for k, g, r in zip(last_keys, fly_e[:2], fly_e[2:])]
    slots = [lax.empty((3,) + s.shape[1:], s.dtype) for s in sums_last]
    send_f, recv_f, fly_f, token_f = split_start("l0_last_chip_quarters_start", chip_quarter_copies(2), 6,
                                                 sums_last + slots, sums_last[0])

    def finish(names, tag, deps):
        halves = [add_quarters(chip_sums[k], quarters[k], p, ci, k + "_add_quarters", deps) for k in names]
        for k, g in zip(names, exchange_final_halves(halves, "exchange_final_halves_" + tag)):
            results[k] = tuple(adamw(a[k], g, a["m_" + k], a["v_" + k], k + "_adamw"))

    done_keys = keys[1] + ffn_keys
    finish(done_keys, "first", (token_f,))
    fly_f = split_wait("l0_last_chip_quarters_wait", chip_quarter_copies(2), send_f, recv_f, fly_f,
                       results[done_keys[-1]][1])
    chip_sums.update(zip(last_keys, fly_f[:2]))
    quarters.update(zip(last_keys, fly_f[2:]))
    finish(last_keys, "last", ())

    loss = lax.psum(loss_tile[0, 0], ("x", "y", "c"))
    out = [loss, dx[None]]
    for j in range(4):
        out += [results[k][j] for k in WEIGHT_NAMES]
    return tuple(out)
```

```python
import numpy as np
import jax
import jax.numpy as jnp
from jax import lax
from jax.experimental import pallas as pl
from jax.experimental.pallas import tpu as pltpu

F32 = jnp.float32
BF16 = jnp.bfloat16
MESH = pl.DeviceIdType.MESH

HEAD_DIM = 128
Q_HEADS = 12
KV_HEADS = 4
GROUPS = Q_HEADS // KV_HEADS
ATT_WIDTH = Q_HEADS * HEAD_DIM
KV_WIDTH = KV_HEADS * HEAD_DIM
QKV_WIDTH = ATT_WIDTH + 2 * KV_WIDTH
AUX_WIDTH = 512
AUX_GROUPS = 4
POOL_WINDOWS = (2, 4, 8, 16)
WINDOW = 128
GRID_W = 64
ROPE_THETA = 10000.0
N_MOD = 6
EPS = 1e-6
NEG_INF = -1e30
ATT_SCALE = HEAD_DIM ** -0.5
N_SHARD = 4
N_DEV = 8

ADAM_LR = 0.001
ADAM_B1 = 0.9
ADAM_B2 = 0.999
ADAM_EPS = 1e-08
ADAM_WD = 0.01
ADAM_STEP = 10

TM = 256
TM_MATMUL = 544
BAND = TM + 2 * WINDOW
VMEM_LIMIT = 56 * 1024 * 1024

NT_DIMS = (((1,), (1,)), ((), ()))
TN_DIMS = (((0,), (0,)), ((), ()))


def _pcall(body, **kw):
    return pl.pallas_call(body, **kw)


def _params(*sem):
    return pltpu.CompilerParams(dimension_semantics=sem, vmem_limit_bytes=VMEM_LIMIT)


def _sel(i):
    return jnp.minimum(i, 1)


def _sel_spec(d):
    return pl.BlockSpec((None, 1, d), lambda i: (_sel(i), 0, 0))


def _dot(a, b):
    return jnp.dot(a, b, preferred_element_type=F32)


def _dot_nt(a, b):
    return lax.dot_general(a, b, NT_DIMS, preferred_element_type=F32)


def _dot_tn(a, b):
    return lax.dot_general(a, b, TN_DIMS, preferred_element_type=F32)


def _sigmoid(x):
    return 0.5 * jnp.tanh(0.5 * x) + 0.5


def norm_mod_fwd(h, g, shift, scale, name):
    t, d = h.shape

    def body(h_ref, g_ref, sh_ref, sc_ref, o_ref):
        x = h_ref[...]
        r = lax.rsqrt(jnp.mean(x * x, axis=-1, keepdims=True) + EPS)
        y = x * r * g_ref[...]
        o_ref[...] = (y * (1.0 + sc_ref[...]) + sh_ref[...]).astype(BF16)

    return _pcall(
        body, name=name, grid=(t // TM,),
        in_specs=[pl.BlockSpec((TM, d), lambda i: (i, 0)), pl.BlockSpec((1, d), lambda i: (0, 0)),
                  _sel_spec(d), _sel_spec(d)],
        out_specs=pl.BlockSpec((TM, d), lambda i: (i, 0)),
        out_shape=jax.ShapeDtypeStruct((t, d), BF16),
        compiler_params=_params("parallel"),
    )(h, g, shift, scale)


def norm_mod_bwd(dy, h, g, scale, dres, name):
    t, d = h.shape

    def body(dy_ref, h_ref, g_ref, sc_ref, dres_ref, dh_ref, dsh_ref, dsc_ref, dg_ref):
        i = pl.program_id(0)

        @pl.when(i == 0)
        def _():
            dsh_ref[...] = jnp.zeros_like(dsh_ref)
            dsc_ref[...] = jnp.zeros_like(dsc_ref)
            dg_ref[...] = jnp.zeros_like(dg_ref)

        x = h_ref[...]
        gv = g_ref[...]
        r = lax.rsqrt(jnp.mean(x * x, axis=-1, keepdims=True) + EPS)
        xhat = x * r
        dyv = dy_ref[...]
        s = _sel(i)
        dsh_ref[pl.ds(s, 1)] += jnp.sum(dyv, axis=0, keepdims=True)[None]
        dsc_ref[pl.ds(s, 1)] += jnp.sum(dyv * xhat * gv, axis=0, keepdims=True)[None]
        tt = dyv * (1.0 + sc_ref[...])
        dg_ref[...] += jnp.sum(tt * xhat, axis=0, keepdims=True)
        dxhat = tt * gv
        dx = r * (dxhat - xhat * jnp.mean(dxhat * xhat, axis=-1, keepdims=True))
        dh_ref[...] = dx + dres_ref[...]

    row = pl.BlockSpec((TM, d), lambda i: (i, 0))
    acc2 = pl.BlockSpec((2, 1, d), lambda i: (0, 0, 0))
    return _pcall(
        body, name=name, grid=(t // TM,),
        in_specs=[row, row, pl.BlockSpec((1, d), lambda i: (0, 0)), _sel_spec(d), row],
        out_specs=[row, acc2, acc2, pl.BlockSpec((1, d), lambda i: (0, 0))],
        out_shape=[jax.ShapeDtypeStruct((t, d), F32), jax.ShapeDtypeStruct((2, 1, d), F32),
                   jax.ShapeDtypeStruct((2, 1, d), F32), jax.ShapeDtypeStruct((1, d), F32)],
        compiler_params=_params("arbitrary"),
    )(dy, h, g, scale, dres)


def gate_bwd(dh, f, gate, name):
    t, d = dh.shape

    def body(dh_ref, f_ref, gt_ref, df_ref, dgt_ref):
        i = pl.program_id(0)

        @pl.when(i == 0)
        def _():
            dgt_ref[...] = jnp.zeros_like(dgt_ref)

        dhv = dh_ref[...]
        df_ref[...] = (dhv * gt_ref[...]).astype(BF16)
        dgt_ref[pl.ds(_sel(i), 1)] += jnp.sum(dhv * f_ref[...], axis=0, keepdims=True)[None]

    row = pl.BlockSpec((TM, d), lambda i: (i, 0))
    return _pcall(
        body, name=name, grid=(t // TM,),
        in_specs=[row, row, _sel_spec(d)],
        out_specs=[row, pl.BlockSpec((2, 1, d), lambda i: (0, 0, 0))],
        out_shape=[jax.ShapeDtypeStruct((t, d), BF16), jax.ShapeDtypeStruct((2, 1, d), F32)],
        compiler_params=_params("arbitrary"),
    )(dh, f, gate)


def final_loss(h, fg, target):
    t, d = h.shape

    def body(h_ref, g_ref, tg_ref, loss_ref, dh_ref, dg_ref):
        i = pl.program_id(0)

        @pl.when(i == 0)
        def _():
            loss_ref[...] = jnp.zeros_like(loss_ref)
            dg_ref[...] = jnp.zeros_like(dg_ref)
            dh_ref[...] = jnp.zeros_like(dh_ref)

        @pl.when(i > 0)
        def _():
            x = h_ref[...]
            gv = g_ref[...]
            r = lax.rsqrt(jnp.mean(x * x, axis=-1, keepdims=True) + EPS)
            xhat = x * r
            diff = xhat * gv - tg_ref[...]
            loss_ref[...] += 0.5 * jnp.sum(jnp.mean(diff * diff, axis=-1, keepdims=True), axis=0, keepdims=True)
            dout = diff * (1.0 / d)
            dg_ref[...] += jnp.sum(dout * xhat, axis=0, keepdims=True)
            dxhat = dout * gv
            dh_ref[...] = r * (dxhat - xhat * jnp.mean(dxhat * xhat, axis=-1, keepdims=True))

    row = pl.BlockSpec((TM, d), lambda i: (i, 0))
    return _pcall(
        body, name="final_loss", grid=(t // TM,),
        in_specs=[row, pl.BlockSpec((1, d), lambda i: (0, 0)),
                  pl.BlockSpec((TM, d), lambda i: (jnp.maximum(i - 1, 0), 0))],
        out_specs=[pl.BlockSpec((8, 128), lambda i: (0, 0)), row, pl.BlockSpec((1, d), lambda i: (0, 0))],
        out_shape=[jax.ShapeDtypeStruct((8, 128), F32), jax.ShapeDtypeStruct((t, d), F32),
                   jax.ShapeDtypeStruct((1, d), F32)],
        compiler_params=_params("arbitrary"),
    )(h, fg, target)


def _matmul_rows(t):
    return TM_MATMUL if t % TM_MATMUL == 0 else TM


def _w_nn_spec(w, k):
    if w.shape[1] == k:
        ns = w.shape[2]
        return w, pl.BlockSpec((None, k, ns), lambda j, i: (j, 0, 0)), w.shape[0], ns
    w2 = w.reshape(w.shape[0] * w.shape[1], w.shape[2])
    tn = 1024 if w2.shape[1] % 1024 == 0 else w2.shape[1]
    return w2, pl.BlockSpec((k, tn), lambda j, i: (0, j)), w2.shape[1] // tn, tn


def mm_nn(a, w, name, out_dtype=F32):
    t, k = a.shape
    w, wspec, nb, tn = _w_nn_spec(w, k)
    tm = _matmul_rows(t)

    def body(a_ref, w_ref, o_ref):
        o_ref[...] = _dot(a_ref[...], w_ref[...]).astype(o_ref.dtype)

    return _pcall(
        body, name=name, grid=(nb, t // tm),
        in_specs=[pl.BlockSpec((tm, k), lambda j, i: (i, 0)), wspec],
        out_specs=pl.BlockSpec((tm, tn), lambda j, i: (i, j)),
        out_shape=jax.ShapeDtypeStruct((t, nb * tn), out_dtype),
        compiler_params=_params("parallel", "parallel"),
    )(a, w)


def mm_nn_residual(a, w, h, gate, name):
    t, k = a.shape
    w, wspec, nb, tn = _w_nn_spec(w, k)

    def body(a_ref, w_ref, h_ref, gt_ref, y_ref, o_ref):
        y = _dot(a_ref[...], w_ref[...])
        y_ref[...] = y
        o_ref[...] = h_ref[...] + gt_ref[...] * y

    out = pl.BlockSpec((TM, tn), lambda j, i: (i, j))
    return _pcall(
        body, name=name, grid=(nb, t // TM),
        in_specs=[pl.BlockSpec((TM, k), lambda j, i: (i, 0)), wspec, out,
                  pl.BlockSpec((None, 1, tn), lambda j, i: (_sel(i), 0, j))],
        out_specs=[out, out],
        out_shape=[jax.ShapeDtypeStruct((t, nb * tn), F32)] * 2,
        compiler_params=_params("parallel", "parallel"),
    )(a, w, h, gate)


def mm_swiglu(a, wg, wu, name):
    t, k = a.shape
    s, _, ns = wg.shape

    def body(a_ref, wg_ref, wu_ref, g_ref, u_ref, act_ref):
        av = a_ref[...]
        g = _dot(av, wg_ref[...])
        u = _dot(av, wu_ref[...])
        g_ref[...] = g.astype(BF16)
        u_ref[...] = u.astype(BF16)
        act_ref[...] = (g * _sigmoid(g) * u).astype(BF16)

    tm = _matmul_rows(t)
    wspec = pl.BlockSpec((None, k, ns), lambda j, i: (j, 0, 0))
    out = pl.BlockSpec((tm, ns), lambda j, i: (i, j))
    return _pcall(
        body, name=name, grid=(s, t // tm),
        in_specs=[pl.BlockSpec((tm, k), lambda j, i: (i, 0)), wspec, wspec],
        out_specs=[out, out, out],
        out_shape=[jax.ShapeDtypeStruct((t, s * ns), BF16)] * 3,
        compiler_params=_params("parallel", "parallel"),
    )(a, wg, wu)


def mm_nt_cols(dy, w, name, add=None):
    t, n = dy.shape
    s, k, ns = w.shape
    tk = 1024

    def body(*refs):
        if add is None:
            dy_ref, w_ref, o_ref = refs
        else:
            dy_ref, w_ref, add_ref, o_ref = refs
        acc = _dot_nt(dy_ref[:, 0:ns], w_ref[0])
        for sh in range(1, s):
            acc += _dot_nt(dy_ref[:, sh * ns:(sh + 1) * ns], w_ref[sh])
        if add is not None:
            acc += add_ref[...]
        o_ref[...] = acc

    tm = _matmul_rows(t)
    out = pl.BlockSpec((tm, tk), lambda j, i: (i, j))
    in_specs = [pl.BlockSpec((tm, n), lambda j, i: (i, 0)), pl.BlockSpec((s, tk, ns), lambda j, i: (0, j, 0))]
    args = [dy, w]
    if add is not None:
        in_specs.append(out)
        args.append(add)
    return _pcall(
        body, name=name, grid=(k // tk, t // tm),
        in_specs=in_specs, out_specs=out,
        out_shape=jax.ShapeDtypeStruct((t, k), F32),
        compiler_params=_params("parallel", "parallel"),
    )(*args)


def mm_nt_rows(dy, w, name):
    t, n = dy.shape
    s, ks, _ = w.shape

    tm = _matmul_rows(t)

    def body(dy_ref, w_ref, o_ref):
        o_ref[...] = _dot_nt(dy_ref[...], w_ref[...])

    return _pcall(
        body, name=name, grid=(s, t // tm),
        in_specs=[pl.BlockSpec((tm, n), lambda j, i: (i, 0)), pl.BlockSpec((None, ks, n), lambda j, i: (j, 0, 0))],
        out_specs=pl.BlockSpec((tm, ks), lambda j, i: (i, j)),
        out_shape=jax.ShapeDtypeStruct((t, s * ks), F32),
        compiler_params=_params("parallel", "parallel"),
    )(dy, w)


def mm_nt_swiglu_bwd(df, wd, g, u, name):
    t, n = df.shape
    s, ks, _ = wd.shape

    def body(df_ref, w_ref, g_ref, u_ref, dg_ref, du_ref):
        da = _dot_nt(df_ref[...], w_ref[...])
        gv = g_ref[...].astype(F32)
        sg = _sigmoid(gv)
        silu = gv * sg
        dg_ref[...] = (da * u_ref[...].astype(F32) * (sg * (1.0 + gv * (1.0 - sg)))).astype(BF16)
        du_ref[...] = (da * silu).astype(BF16)

    tm = _matmul_rows(t)
    blk = pl.BlockSpec((tm, ks), lambda j, i: (i, j))
    return _pcall(
        body, name=name, grid=(s, t // tm),
        in_specs=[pl.BlockSpec((tm, n), lambda j, i: (i, 0)), pl.BlockSpec((None, ks, n), lambda j, i: (j, 0, 0)),
                  blk, blk],
        out_specs=[blk, blk],
        out_shape=[jax.ShapeDtypeStruct((t, s * ks), BF16)] * 2,
        compiler_params=_params("parallel", "parallel"),
    )(df, wd, g, u)


def mm_tn(x, dy, name, col_sharded):
    t, k = x.shape
    n = dy.shape[1]
    tk = next(c for c in (1024, 1408, 512, k) if k % c == 0)
    mc = t // 2 if (t // 2) % 128 == 0 else t
    n_m = t // mc
    if col_sharded:
        tn = n // N_SHARD
        out_shape = jax.ShapeDtypeStruct((N_SHARD, k, tn), BF16)
        out_spec = pl.BlockSpec((None, tk, tn), lambda j, kb, m: (j, kb, 0))
    else:
        tn = 1024
        out_shape = jax.ShapeDtypeStruct((k, n), BF16)
        out_spec = pl.BlockSpec((tk, tn), lambda j, kb, m: (kb, j))

    def body(x_ref, dy_ref, o_ref, acc_ref):
        m = pl.program_id(2)
        acc = _dot_tn(x_ref[...], dy_ref[...])

        @pl.when(m == 0)
        def _():
            acc_ref[...] = acc

        @pl.when((m > 0) & (m < n_m - 1))
        def _():
            acc_ref[...] += acc

        @pl.when(m == n_m - 1)
        def _():
            o_ref[...] = (acc if n_m == 1 else acc_ref[...] + acc).astype(BF16)

    out = _pcall(
        body, name=name, grid=(n // tn, k // tk, n_m),
        in_specs=[pl.BlockSpec((mc, tk), lambda j, kb, m: (m, kb)), pl.BlockSpec((mc, tn), lambda j, kb, m: (m, j))],
        out_specs=out_spec, out_shape=out_shape,
        scratch_shapes=[pltpu.VMEM((tk, tn), F32)],
        compiler_params=_params("parallel", "parallel", "arbitrary"),
    )(x, dy)
    return out if col_sharded else out.reshape(N_SHARD, k // N_SHARD, n)


def _swap32(y):
    right = pltpu.roll(y, 32, 1)
    left = pltpu.roll(y, 96, 1)
    lane = lax.broadcasted_iota(jnp.int32, y.shape, 1)
    return jnp.where((lane // 32) % 2 == 0, left, right)


def _rope_tables(t, ctx_len):
    s = t - ctx_len
    pos = np.arange(s)
    row = (pos // GRID_W).astype(np.float32)
    col = (pos % GRID_W).astype(np.float32)
    half = HEAD_DIM // 2
    inv = np.power(np.float32(ROPE_THETA), -np.arange(0, half, 2, dtype=np.float32) / np.float32(half))
    ar = row[:, None] * inv
    ac = col[:, None] * inv
    cos = np.concatenate([np.cos(ar), np.cos(ar), np.cos(ac), np.cos(ac)], axis=1)
    sin = np.concatenate([-np.sin(ar), np.sin(ar), -np.sin(ac), np.sin(ac)], axis=1)
    cos = np.concatenate([np.ones((ctx_len, HEAD_DIM), np.float32), cos.astype(np.float32)], axis=0)
    sin = np.concatenate([np.zeros((ctx_len, HEAD_DIM), np.float32), sin.astype(np.float32)], axis=0)
    return jnp.asarray(cos, F32), jnp.asarray(sin, F32)


def qk_prep_fwd(proj, qg, kg, cos, sin, name):
    t = proj.shape[0]

    def body(p_ref, qg_ref, kg_ref, c_ref, s_ref, q_ref, k_ref, v_ref):
        cv = c_ref[...]
        sv = s_ref[...]
        for hd in range(Q_HEADS + KV_HEADS):
            x = p_ref[:, hd * HEAD_DIM:(hd + 1) * HEAD_DIM]
            gv = qg_ref[...] if hd < Q_HEADS else kg_ref[...]
            y = x * lax.rsqrt(jnp.mean(x * x, axis=-1, keepdims=True) + EPS) * gv
            out = y * cv + _swap32(y) * sv
            if hd < Q_HEADS:
                q_ref[:, hd * HEAD_DIM:(hd + 1) * HEAD_DIM] = (out * ATT_SCALE).astype(BF16)
            else:
                k_ref[:, (hd - Q_HEADS) * HEAD_DIM:(hd - Q_HEADS + 1) * HEAD_DIM] = out.astype(BF16)
        v_ref[...] = p_ref[:, ATT_WIDTH + KV_WIDTH:QKV_WIDTH].astype(BF16)

    vec = pl.BlockSpec((1, HEAD_DIM), lambda i: (0, 0))
    tab = pl.BlockSpec((TM, HEAD_DIM), lambda i: (i, 0))
    return _pcall(
        body, name=name, grid=(t // TM,),
        in_specs=[pl.BlockSpec((TM, QKV_WIDTH), lambda i: (i, 0)), vec, vec, tab, tab],
        out_specs=[pl.BlockSpec((TM, ATT_WIDTH), lambda i: (i, 0)), pl.BlockSpec((TM, KV_WIDTH), lambda i: (i, 0)),
                   pl.BlockSpec((TM, KV_WIDTH), lambda i: (i, 0))],
        out_shape=[jax.ShapeDtypeStruct((t, ATT_WIDTH), BF16), jax.ShapeDtypeStruct((t, KV_WIDTH), BF16),
                   jax.ShapeDtypeStruct((t, KV_WIDTH), BF16)],
        compiler_params=_params("parallel"),
    )(proj, qg, kg, cos, sin)


def qk_prep_bwd(dq, dk, dv, daux, proj, qg, kg, cos, sin, name):
    t, n_in = proj.shape
    n_aux = daux.shape[0]

    def body(dq_ref, dk_ref, dv_ref, da_ref, p_ref, qg_ref, kg_ref, c_ref, s_ref, o_ref, dqg_ref, dkg_ref):
        @pl.when(pl.program_id(0) == 0)
        def _():
            dqg_ref[...] = jnp.zeros_like(dqg_ref)
            dkg_ref[...] = jnp.zeros_like(dkg_ref)

        cv = c_ref[...]
        sv = s_ref[...]
        for hd in range(Q_HEADS + KV_HEADS):
            cols = slice(hd * HEAD_DIM, (hd + 1) * HEAD_DIM)
            x = p_ref[:, cols]
            if hd < Q_HEADS:
                gv, dyr, dg_ref = qg_ref[...], dq_ref[:, cols] * ATT_SCALE, dqg_ref
            else:
                kc = slice((hd - Q_HEADS) * HEAD_DIM, (hd - Q_HEADS + 1) * HEAD_DIM)
                gv, dyr, dg_ref = kg_ref[...], dk_ref[:, kc], dkg_ref
            r = lax.rsqrt(jnp.mean(x * x, axis=-1, keepdims=True) + EPS)
            xhat = x * r
            dy = dyr * cv + _swap32(dyr * sv)
            dg_ref[...] += jnp.sum(dy * xhat, axis=0, keepdims=True)
            dxhat = dy * gv
            o_ref[:, cols] = (r * (dxhat - xhat * jnp.mean(dxhat * xhat, axis=-1, keepdims=True))).astype(BF16)
        o_ref[:, ATT_WIDTH + KV_WIDTH:QKV_WIDTH] = dv_ref[...].astype(BF16)
        for a in range(n_aux):
            o_ref[:, QKV_WIDTH + a * AUX_WIDTH:QKV_WIDTH + (a + 1) * AUX_WIDTH] = da_ref[a]

    vec = pl.BlockSpec((1, HEAD_DIM), lambda i: (0, 0))
    tab = pl.BlockSpec((TM, HEAD_DIM), lambda i: (i, 0))
    return _pcall(
        body, name=name, grid=(t // TM,),
        in_specs=[pl.BlockSpec((TM, ATT_WIDTH), lambda i: (i, 0)), pl.BlockSpec((TM, KV_WIDTH), lambda i: (i, 0)),
                  pl.BlockSpec((TM, KV_WIDTH), lambda i: (i, 0)),
                  pl.BlockSpec((n_aux, TM, AUX_WIDTH), lambda i: (0, i, 0)),
                  pl.BlockSpec((TM, QKV_WIDTH), lambda i: (i, 0)), vec, vec, tab, tab],
        out_specs=[pl.BlockSpec((TM, n_in), lambda i: (i, 0)), vec, vec],
        out_shape=[jax.ShapeDtypeStruct((t, n_in), BF16), jax.ShapeDtypeStruct((1, HEAD_DIM), F32),
                   jax.ShapeDtypeStruct((1, HEAD_DIM), F32)],
        compiler_params=_params("arbitrary"),
    )(dq, dk, dv, daux, proj, qg, kg, cos, sin)


def _band_start(i, t):
    return pl.multiple_of(jnp.clip(i * TM - WINDOW, 0, t - BAND), WINDOW)


def _band_mask(i, start, ctx_len):
    shape = (TM, ctx_len + BAND)
    col = lax.broadcasted_iota(jnp.int32, shape, 1)
    qrow = i * TM + lax.broadcasted_iota(jnp.int32, shape, 0)
    krow = start + col - ctx_len
    band_ok = (krow >= ctx_len) & (jnp.abs(krow - qrow) <= WINDOW)
    return (col < ctx_len) | band_ok


def attention_fwd(q, k, v, sink, ctx_len, windowed, name):
    t = q.shape[0]
    d_model = ATT_WIDTH + AUX_WIDTH
    gw = GROUPS * HEAD_DIM

    def one_head(qh, kk, vv, mask, sink_val):
        s = _dot_nt(qh, kk)
        if mask is not None:
            s = jnp.where(mask, s, NEG_INF)
        m = jnp.max(s, axis=-1, keepdims=True)
        if sink_val is not None:
            m = jnp.maximum(m, sink_val)
        p = jnp.exp(s - m)
        l = jnp.sum(p, axis=-1, keepdims=True)
        if sink_val is not None:
            l = l + jnp.exp(sink_val - m)
        o = _dot(p.astype(BF16), vv) / l
        return o, m + jnp.log(l)

    def body(sink_ref, q_ref, k_ref, v_ref, o_ref, lse_ref):
        kv = pl.program_id(0)
        i = pl.program_id(1)

        def run(kk, vv, mask):
            for g in range(GROUPS):
                sink_val = sink_ref[kv * GROUPS + g] if windowed else None
                o, lse = one_head(q_ref[:, g * HEAD_DIM:(g + 1) * HEAD_DIM], kk, vv, mask, sink_val)
                o_ref[:, g * HEAD_DIM:(g + 1) * HEAD_DIM] = o.astype(BF16)
                lse_ref[g] = lse

        @pl.when(i == 0)
        def _():
            if windowed:
                o_ref[...] = jnp.zeros_like(o_ref)
                lse_ref[...] = jnp.zeros_like(lse_ref)
            else:
                run(k_ref[0:ctx_len], v_ref[0:ctx_len], None)

        @pl.when(i > 0)
        def _():
            if windowed:
                start = _band_start(i, t)
                kk = jnp.concatenate([k_ref[0:ctx_len], k_ref[pl.ds(start, BAND)]], axis=0)
                vv = jnp.concatenate([v_ref[0:ctx_len], v_ref[pl.ds(start, BAND)]], axis=0)
                run(kk, vv, _band_mask(i, start, ctx_len))
            else:
                run(k_ref[...], v_ref[...], None)

    kvspec = pl.BlockSpec((t, HEAD_DIM), lambda kv, i: (0, kv))
    return _pcall(
        body, name=name, grid=(KV_HEADS, t // TM),
        in_specs=[pl.BlockSpec(memory_space=pltpu.SMEM), pl.BlockSpec((TM, gw), lambda kv, i: (i, kv)), kvspec, kvspec],
        out_specs=[pl.BlockSpec((TM, gw), lambda kv, i: (i, kv)),
                   pl.BlockSpec((GROUPS, TM, 1), lambda kv, i: (kv, i, 0))],
        out_shape=[jax.ShapeDtypeStruct((t, d_model), BF16), jax.ShapeDtypeStruct((Q_HEADS, t, 1), F32)],
        compiler_params=_params("parallel", "parallel"),
    )(sink, q, k, v)


def attention_bwd(q, k, v, cat, lse, dcat, sink, ctx_len, windowed, name):
    t = q.shape[0]
    gw = GROUPS * HEAD_DIM

    def body(sink_ref, q_ref, k_ref, v_ref, o_ref, lse_ref, do_ref, dq_ref, dk_ref, dv_ref, dsink_ref):
        kv = pl.program_id(0)
        i = pl.program_id(1)

        @pl.when(i == 0)
        def _():
            dk_ref[...] = jnp.zeros_like(dk_ref)
            dv_ref[...] = jnp.zeros_like(dv_ref)
            dsink_ref[...] = jnp.zeros_like(dsink_ref)

        def run(kk, vv, mask, accumulate):
            for g in range(GROUPS):
                cols = slice(g * HEAD_DIM, (g + 1) * HEAD_DIM)
                qh = q_ref[:, cols]
                doh = do_ref[:, cols]
                delta = jnp.sum(doh * o_ref[:, cols].astype(F32), axis=-1, keepdims=True)
                lse_g = lse_ref[g]
                s = _dot_nt(qh, kk)
                if mask is not None:
                    s = jnp.where(mask, s, NEG_INF)
                p = jnp.exp(s - lse_g)
                dob = doh.astype(BF16)
                dp = _dot_nt(dob, vv)
                ds = (p * (dp - delta)).astype(BF16)
                dq_ref[:, cols] = _dot(ds, kk)
                accumulate(_dot_tn(ds, qh), _dot_tn(p.astype(BF16), dob))
                if windowed:
                    p_sink = jnp.exp(sink_ref[kv * GROUPS + g] - lse_g)
                    dsink_ref[g:g + 1, :] += jnp.sum(-p_sink * delta, axis=0, keepdims=True)

        @pl.when(i == 0)
        def _():
            if windowed:
                dq_ref[...] = jnp.zeros_like(dq_ref)
            else:
                def acc_ctx(dkp, dvp):
                    dk_ref[0:ctx_len] += dkp
                    dv_ref[0:ctx_len] += dvp

                run(k_ref[0:ctx_len], v_ref[0:ctx_len], None, acc_ctx)

        @pl.when(i > 0)
        def _():
            if windowed:
                start = _band_start(i, t)
                kk = jnp.concatenate([k_ref[0:ctx_len], k_ref[pl.ds(start, BAND)]], axis=0)
                vv = jnp.concatenate([v_ref[0:ctx_len], v_ref[pl.ds(start, BAND)]], axis=0)

                def acc_band(dkp, dvp):
                    dk_ref[0:ctx_len] += dkp[0:ctx_len]
                    dv_ref[0:ctx_len] += dvp[0:ctx_len]
                    dk_ref[pl.ds(start, BAND)] += dkp[ctx_len:ctx_len + BAND]
                    dv_ref[pl.ds(start, BAND)] += dvp[ctx_len:ctx_len + BAND]

                run(kk, vv, _band_mask(i, start, ctx_len), acc_band)
            else:
                def acc_all(dkp, dvp):
                    dk_ref[...] += dkp
                    dv_ref[...] += dvp

                run(k_ref[...], v_ref[...], None, acc_all)

    kvspec = pl.BlockSpec((t, HEAD_DIM), lambda kv, i: (0, kv))
    grp = pl.BlockSpec((TM, gw), lambda kv, i: (i, kv))
    return _pcall(
        body, name=name, grid=(KV_HEADS, t // TM),
        in_specs=[pl.BlockSpec(memory_space=pltpu.SMEM), grp, kvspec, kvspec, grp,
                  pl.BlockSpec((GROUPS, TM, 1), lambda kv, i: (kv, i, 0)), grp],
        out_specs=[grp, kvspec, kvspec, pl.BlockSpec((None, 8, 128), lambda kv, i: (kv, 0, 0))],
        out_shape=[jax.ShapeDtypeStruct((t, ATT_WIDTH), F32), jax.ShapeDtypeStruct((t, KV_WIDTH), F32),
                   jax.ShapeDtypeStruct((t, KV_WIDTH), F32), jax.ShapeDtypeStruct((KV_HEADS, 8, 128), F32)],
        compiler_params=_params("parallel", "arbitrary"),
    )(sink, q, k, v, cat, lse, dcat)


def _segment_pos(t, ctx_len, width):
    row = lax.broadcasted_iota(jnp.int32, (t, width), 0)
    in_ctx = row < ctx_len
    return jnp.where(in_ctx, row, row - ctx_len), jnp.where(in_ctx, ctx_len, t - ctx_len)


def _shifted(x, offset, pos, seg_len):
    t = x.shape[0]
    moved = x if offset == 0 else pltpu.roll(x, (-offset) % t, 0)
    ok = (pos + offset >= 0) & (pos + offset < seg_len)
    return jnp.where(ok, moved, 0.0)


def conv_fwd(proj, conv_w, cat, ctx_len, name):
    t = proj.shape[0]
    base = QKV_WIDTH // 128
    per = AUX_WIDTH // 128

    def body(gb_ref, gc_ref, u_ref, w_ref, cat_in, o_ref):
        del cat_in
        pos, seg = _segment_pos(t, ctx_len, 128)
        z = gc_ref[...] * u_ref[...]
        conv = (w_ref[0:1, :] * _shifted(z, -1, pos, seg) + w_ref[1:2, :] * z
                + w_ref[2:3, :] * _shifted(z, 1, pos, seg))
        o_ref[...] = (gb_ref[...] * conv).astype(BF16)

    def col(off):
        return pl.BlockSpec((t, 128), lambda j: (0, base + off * per + j))

    return _pcall(
        body, name=name, grid=(per,),
        in_specs=[col(0), col(1), col(2), pl.BlockSpec((3, 128), lambda j: (0, j)),
                  pl.BlockSpec(memory_space=pl.ANY)],
        out_specs=pl.BlockSpec((t, 128), lambda j: (0, ATT_WIDTH // 128 + j)),
        out_shape=jax.ShapeDtypeStruct(cat.shape, BF16),
        input_output_aliases={4: 0},
        compiler_params=_params("parallel"),
    )(proj, proj, proj, conv_w, cat)


def conv_bwd(dcat, proj, conv_w, ctx_len, name):
    t = proj.shape[0]
    base = QKV_WIDTH // 128
    per = AUX_WIDTH // 128

    def body(do_ref, gb_ref, gc_ref, u_ref, w_ref, da_ref, dw_ref):
        pos, seg = _segment_pos(t, ctx_len, 128)
        gc = gc_ref[...]
        u = u_ref[...]
        z = gc * u
        zm = _shifted(z, -1, pos, seg)
        zp = _shifted(z, 1, pos, seg)
        conv = w_ref[0:1, :] * zm + w_ref[1:2, :] * z + w_ref[2:3, :] * zp
        dout = do_ref[...]
        da_ref[0] = (dout * conv).astype(BF16)
        dconv = dout * gb_ref[...]
        dw_ref[0:1, :] = jnp.sum(dconv * zm, axis=0, keepdims=True)
        dw_ref[1:2, :] = jnp.sum(dconv * z, axis=0, keepdims=True)
        dw_ref[2:3, :] = jnp.sum(dconv * zp, axis=0, keepdims=True)
        dz = (w_ref[1:2, :] * dconv + w_ref[0:1, :] * _shifted(dconv, 1, pos, seg)
              + w_ref[2:3, :] * _shifted(dconv, -1, pos, seg))
        da_ref[1] = (dz * u).astype(BF16)
        da_ref[2] = (dz * gc).astype(BF16)

    def col(off):
        return pl.BlockSpec((t, 128), lambda j: (0, base + off * per + j))

    return _pcall(
        body, name=name, grid=(per,),
        in_specs=[pl.BlockSpec((t, 128), lambda j: (0, ATT_WIDTH // 128 + j)), col(0), col(1), col(2),
                  pl.BlockSpec((3, 128), lambda j: (0, j))],
        out_specs=[pl.BlockSpec((3, t, 128), lambda j: (0, 0, j)), pl.BlockSpec((3, 128), lambda j: (0, j))],
        out_shape=[jax.ShapeDtypeStruct((3, t, AUX_WIDTH), BF16), jax.ShapeDtypeStruct((3, AUX_WIDTH), F32)],
        compiler_params=_params("parallel"),
    )(dcat, proj, proj, proj, conv_w)


def _pooled(u, w, pos, seg):
    lo = jnp.clip(pos - w // 2, 0, seg)
    hi = jnp.clip(pos - w // 2 + w, 0, seg)
    inv = 1.0 / (hi - lo).astype(F32)
    total = _shifted(u, -(w // 2), pos, seg)
    for o in range(-(w // 2) + 1, w // 2):
        total = total + _shifted(u, o, pos, seg)
    return total * inv - u, inv


def pool_fwd(proj, pool_w, pool_scale, cat, ctx_len, name):
    t = proj.shape[0]

    def body(u_ref, w_ref, sc_ref, cat_in, o_ref):
        del cat_in
        pos, seg = _segment_pos(t, ctx_len, 128)
        for g, w in enumerate(POOL_WINDOWS):
            cols = slice(g * 128, (g + 1) * 128)
            pooled, _ = _pooled(u_ref[:, cols], w, pos, seg)
            mixed = _dot(pooled.astype(BF16), w_ref[g].astype(BF16))
            o_ref[:, cols] = (mixed * sc_ref[:, cols]).astype(BF16)

    return _pcall(
        body, name=name, grid=(1,),
        in_specs=[pl.BlockSpec((t, AUX_WIDTH), lambda j: (0, QKV_WIDTH // AUX_WIDTH)),
                  pl.BlockSpec((AUX_GROUPS, 128, 128), lambda j: (0, 0, 0)),
                  pl.BlockSpec((1, AUX_WIDTH), lambda j: (0, 0)), pl.BlockSpec(memory_space=pl.ANY)],
        out_specs=pl.BlockSpec((t, AUX_WIDTH), lambda j: (0, ATT_WIDTH // AUX_WIDTH)),
        out_shape=jax.ShapeDtypeStruct(cat.shape, BF16),
        input_output_aliases={3: 0},
        compiler_params=_params("arbitrary"),
    )(proj, pool_w, pool_scale, cat)


def pool_bwd(dcat, proj, pool_w, pool_scale, ctx_len, name):
    t = proj.shape[0]

    def body(do_ref, u_ref, w_ref, sc_ref, da_ref, dw_ref, dsc_ref):
        pos, seg = _segment_pos(t, ctx_len, 128)
        for g, w in enumerate(POOL_WINDOWS):
            cols = slice(g * 128, (g + 1) * 128)
            pooled, inv = _pooled(u_ref[:, cols], w, pos, seg)
            pb = pooled.astype(BF16)
            wb = w_ref[g].astype(BF16)
            mixed = _dot(pb, wb)
            dout = do_ref[:, cols]
            dsc_ref[:, cols] = jnp.sum(dout * mixed, axis=0, keepdims=True)
            dmixed = (dout * sc_ref[:, cols]).astype(BF16)
            dw_ref[g] = _dot_tn(pb, dmixed)
            dpooled = _dot_nt(dmixed, wb)
            spread = dpooled * inv
            du = _shifted(spread, w // 2, pos, seg) - dpooled
            for o in range(-(w // 2) + 1, w // 2):
                du = du + _shifted(spread, -o, pos, seg)
            da_ref[0, :, cols] = du.astype(BF16)

    return _pcall(
        body, name=name, grid=(1,),
        in_specs=[pl.BlockSpec((t, AUX_WIDTH), lambda j: (0, ATT_WIDTH // AUX_WIDTH)),
                  pl.BlockSpec((t, AUX_WIDTH), lambda j: (0, QKV_WIDTH // AUX_WIDTH)),
                  pl.BlockSpec((AUX_GROUPS, 128, 128), lambda j: (0, 0, 0)),
                  pl.BlockSpec((1, AUX_WIDTH), lambda j: (0, 0))],
        out_specs=[pl.BlockSpec((1, t, AUX_WIDTH), lambda j: (0, 0, 0)),
                   pl.BlockSpec((AUX_GROUPS, 128, 128), lambda j: (0, 0, 0)),
                   pl.BlockSpec((1, AUX_WIDTH), lambda j: (0, 0))],
        out_shape=[jax.ShapeDtypeStruct((1, t, AUX_WIDTH), BF16), jax.ShapeDtypeStruct((AUX_GROUPS, 128, 128), F32),
                   jax.ShapeDtypeStruct((1, AUX_WIDTH), F32)],
        compiler_params=_params("arbitrary"),
    )(dcat, proj, pool_w, pool_scale)


MOD_NAMES = ("shift1", "scale1", "gate1", "shift2", "scale2", "gate2")


def _layer_fwd(h, lw, weight, mods, tabs, ctx_len, windowed, tag, after_mixer=None):
    cos, sin = tabs
    xn = norm_mod_fwd(h, lw["norm1_g"], mods["shift1"], mods["scale1"], tag + "norm1_fwd")
    lw["w_in"] = weight("w_in", xn)
    proj = mm_nn(xn, lw["w_in"], tag + "w_in_fwd")
    q, k, v = qk_prep_fwd(proj, lw["q_g"], lw["k_g"], cos, sin, tag + "qk_fwd")
    cat, lse = attention_fwd(q, k, v, lw["sink"], ctx_len, windowed, tag + "attn_fwd")
    if windowed:
        cat = pool_fwd(proj, lw["pool_w"], lw["pool_scale"], cat, ctx_len, tag + "pool_fwd")
    else:
        cat = conv_fwd(proj, lw["conv_w"], cat, ctx_len, tag + "conv_fwd")
    lw["w_out"] = weight("w_out", cat)
    gate1 = mods["gate1"] if after_mixer is None else mods["gate1"] + after_mixer(cat)[0, 0]
    y, h1 = mm_nn_residual(cat, lw["w_out"], h, gate1, tag + "w_out_fwd")
    hn = norm_mod_fwd(h1, lw["norm2_g"], mods["shift2"], mods["scale2"], tag + "norm2_fwd")
    lw["w_gate"], lw["w_up"] = weight("w_gate", hn), weight("w_up", hn)
    g, u, act = mm_swiglu(hn, lw["w_gate"], lw["w_up"], tag + "ffn_in_fwd")
    lw["w_down"] = weight("w_down", act)
    f, h2 = mm_nn_residual(act, lw["w_down"], h1, mods["gate2"], tag + "w_down_fwd")
    saved = dict(h=h, xn=xn, proj=proj, q=q, k=k, v=v, cat=cat, lse=lse, y=y, h1=h1, hn=hn, g=g, u=u, act=act, f=f)
    return h2, saved


def _layer_bwd_ffn(dh, sv, lw, mods, tag):
    big, small = {}, {}
    df, dgate2 = gate_bwd(dh, sv["f"], mods["gate2"], tag + "gate2_bwd")
    dgp, du = mm_nt_swiglu_bwd(df, lw["w_down"], sv["g"], sv["u"], tag + "w_down_dgrad")
    big["w_down"] = mm_tn(sv["act"], df, tag + "w_down_wgrad", col_sharded=False)
    dhn = mm_nt_cols(dgp, lw["w_gate"], tag + "w_gate_dgrad")
    dhn = mm_nt_cols(du, lw["w_up"], tag + "w_up_dgrad", add=dhn)
    big["w_gate"] = mm_tn(sv["hn"], dgp, tag + "w_gate_wgrad", col_sharded=True)
    big["w_up"] = mm_tn(sv["hn"], du, tag + "w_up_wgrad", col_sharded=True)
    dh1, dshift2, dscale2, small["norm2_g"] = norm_mod_bwd(dhn, sv["h1"], lw["norm2_g"], mods["scale2"], dh,
                                                           tag + "norm2_bwd")
    return dh1, big, small, (dshift2, dscale2, dgate2)


def _layer_bwd_mixer(dh1, sv, lw, mods, tabs, ctx_len, windowed, tag, between=None):
    cos, sin = tabs
    big, small = {}, {}
    dy, dgate1 = gate_bwd(dh1, sv["y"], mods["gate1"], tag + "gate1_bwd")
    dcat = mm_nt_rows(dy, lw["w_out"], tag + "w_out_dgrad")
    big["w_out"] = mm_tn(sv["cat"], dy, tag + "w_out_wgrad", col_sharded=False)
    if windowed:
        daux, small["pool_w"], small["pool_scale"] = pool_bwd(dcat, sv["proj"], lw["pool_w"], lw["pool_scale"],
                                                              ctx_len, tag + "pool_bwd")
    else:
        daux, small["conv_w"] = conv_bwd(dcat, sv["proj"], lw["conv_w"], ctx_len, tag + "conv_bwd")
    dq, dk, dv, dsink = attention_bwd(sv["q"], sv["k"], sv["v"], sv["cat"], sv["lse"], dcat, lw["sink"], ctx_len,
                                      windowed, tag + "attn_bwd")
    if windowed:
        small["sink"] = dsink[:, :GROUPS, 0].reshape(Q_HEADS)
    q_g = lw["q_g"] if between is None else lw["q_g"] + between(dq)[0, 0]
    dproj, small["q_g"], small["k_g"] = qk_prep_bwd(dq, dk, dv, daux, sv["proj"], q_g, lw["k_g"], cos, sin,
                                                    tag + "qk_bwd")
    dxn = mm_nt_cols(dproj, lw["w_in"], tag + "w_in_dgrad")
    big["w_in"] = mm_tn(sv["xn"], dproj, tag + "w_in_wgrad", col_sharded=True)
    dh0, dshift1, dscale1, small["norm1_g"] = norm_mod_bwd(dxn, sv["h"], lw["norm1_g"], mods["scale1"], dh1,
                                                           tag + "norm1_bwd")
    return dh0, big, small, (dshift1, dscale1, dgate1)


def _dmod_rows(mixer_part, ffn_part):
    return jnp.concatenate([m[:, 0, :] for m in (*mixer_part, *ffn_part)], axis=1)


def _tie(mods, name, token):
    return {**mods, name: mods[name] + token[0, 0]}


def _place():
    x, y, c = lax.axis_index("x"), lax.axis_index("y"), lax.axis_index("c")
    chips = [(1 - x, y), (x, 1 - y), (1 - x, 1 - y)]
    return x, y, c, chips


def _remote(src, dst, send_sem, recv_sem, to):
    return pltpu.make_async_remote_copy(src_ref=src, dst_ref=dst, send_sem=send_sem, recv_sem=recv_sem,
                                        device_id=to, device_id_type=MESH)


ANY = pl.BlockSpec(memory_space=pl.ANY)


def all_gather8(x_shard, name):
    m_per, n = x_shard.shape

    def body(x_ref, out_ref, send_sems, recv_sems, local_sem):
        x, y, c, chips = _place()
        me, sibling = (x, y, c), (x, y, 1 - c)

        def rows(px, py, pc):
            return out_ref.at[pl.ds((4 * px + 2 * py + pc) * m_per, m_per), :]

        def copy(k, block, to, src=None):
            return _remote(rows(*block) if src is None else src, rows(*block), send_sems.at[k], recv_sems.at[k], to)

        mine = pltpu.make_async_copy(x_ref, rows(*me), local_sem)
        mine.start()
        first = [copy(0, me, sibling, src=x_ref)]
        first += [copy(1 + j, me, (*chip, c), src=x_ref) for j, chip in enumerate(chips)]
        for cp in first:
            cp.start()
        passed = [copy(4 + j, (*chip, c), sibling) for j, chip in enumerate(chips)]
        for j, chip in enumerate(chips):
            copy(1 + j, (*chip, c), me).wait_recv()
            passed[j].start()
        copy(0, sibling, me).wait_recv()
        for j, chip in enumerate(chips):
            copy(4 + j, (*chip, 1 - c), me).wait_recv()
        for cp in first + passed:
            cp.wait_send()
        mine.wait()

    return _pcall(
        body, name=name,
        out_shape=jax.ShapeDtypeStruct((N_DEV * m_per, n), x_shard.dtype),
        in_specs=[pl.BlockSpec(memory_space=pltpu.VMEM)],
        out_specs=pl.BlockSpec(memory_space=pltpu.VMEM),
        scratch_shapes=[pltpu.SemaphoreType.DMA((7,)), pltpu.SemaphoreType.DMA((7,)), pltpu.SemaphoreType.DMA],
        compiler_params=pltpu.CompilerParams(vmem_limit_bytes=VMEM_LIMIT),
    )(x_shard)


def _row_tile(rows):
    for tr in (256, 128, 64, 32, 16):
        if rows % tr == 0:
            return tr
    return rows


def place_shard(w, p, name):
    r, cols = w.shape
    tr = _row_tile(r)

    def body(p_ref, w_ref, o_ref):
        del p_ref
        o_ref[...] = w_ref[...].astype(BF16)

    return _pcall(
        body, name=name,
        grid_spec=pltpu.PrefetchScalarGridSpec(
            num_scalar_prefetch=1, grid=(r // tr,),
            in_specs=[pl.BlockSpec((tr, cols), lambda i, p_ref: (i, 0))],
            out_specs=pl.BlockSpec((None, tr, cols), lambda i, p_ref: (p_ref[0], i, 0))),
        out_shape=jax.ShapeDtypeStruct((N_SHARD, r, cols), BF16),
        compiler_params=_params("parallel"),
    )(p.reshape(1).astype(jnp.int32), w)


def gather_weight_shards(stacks, after):
    n = len(stacks)

    def body(*refs):
        outs = refs[n + 1:2 * n + 1]
        send_sems, recv_sems = refs[2 * n + 1:]
        x, y, c, chips = _place()
        p = 2 * x + y
        sibling = (x, y, 1 - c)
        started = []
        for i in range(n):
            hr = outs[i].shape[1] // 2
            mine = outs[i].at[p, pl.ds(c * hr, hr)]
            for j, chip in enumerate(chips):
                cp = _remote(mine, mine, send_sems.at[i, j], recv_sems.at[i, j], (*chip, c))
                cp.start()
                started.append(cp)
        for i in range(n):
            hr = outs[i].shape[1] // 2
            for j, (px, py) in enumerate(chips):
                half = outs[i].at[2 * px + py, pl.ds(c * hr, hr)]
                _remote(half, half, send_sems.at[i, j], recv_sems.at[i, j], (px, py, c)).wait_recv()
                fwd = _remote(half, half, send_sems.at[i, 3 + j], recv_sems.at[i, 3 + j], sibling)
                fwd.start()
                started.append(fwd)
        for i in range(n):
            hr = outs[i].shape[1] // 2
            for j, (px, py) in enumerate(chips):
                other = outs[i].at[2 * px + py, pl.ds((1 - c) * hr, hr)]
                _remote(other, other, send_sems.at[i, 3 + j], recv_sems.at[i, 3 + j], sibling).wait_recv()
        for cp in started:
            cp.wait_send()

    return _pcall(
        body, name="gather_weight_shards",
        out_shape=[jax.ShapeDtypeStruct(s.shape, s.dtype) for s in stacks],
        in_specs=[ANY] * (n + 1), out_specs=[ANY] * n,
        input_output_aliases={i: i for i in range(n)},
        scratch_shapes=[pltpu.SemaphoreType.DMA((n, 6)), pltpu.SemaphoreType.DMA((n, 6))],
    )(*stacks, after)


HBM_SPEC = pl.BlockSpec(memory_space=pltpu.HBM)
SEM_SPEC = pl.BlockSpec(memory_space=pltpu.SEMAPHORE)
EFFECT = pltpu.SideEffectType.DATAFLOW_SIDE_EFFECTING


def split_start(name, copies, n_sems, arrays, after):
    m = len(arrays)

    def body(*refs):
        for cp in copies(refs[:m], refs[m + 1], refs[m + 2]):
            cp.start()
        refs[-1][...] = jnp.zeros_like(refs[-1])

    res = _pcall(
        body, name=name,
        out_shape=(pltpu.SemaphoreType.DMA((n_sems,)), pltpu.SemaphoreType.DMA((n_sems,)),
                   *[pltpu.HBM(s.shape, s.dtype) for s in arrays], jax.ShapeDtypeStruct((8, 128), F32)),
        in_specs=[HBM_SPEC] * m + [ANY],
        out_specs=(SEM_SPEC, SEM_SPEC, *[HBM_SPEC] * m, pl.BlockSpec(memory_space=pltpu.VMEM)),
        input_output_aliases={i: 2 + i for i in range(m)},
        compiler_params=pltpu.CompilerParams(has_side_effects=EFFECT),
    )(*[pltpu.with_memory_space_constraint(s, pltpu.HBM) for s in arrays], after)
    return res[0], res[1], list(res[2:2 + m]), res[2 + m]


def split_wait(name, copies, send_sems, recv_sems, arrays, after):
    m = len(arrays)

    def body(*refs):
        for cp in copies(refs[:m], refs[m], refs[m + 1]):
            cp.wait_send()
            cp.wait_recv()

    return _pcall(
        body, name=name,
        out_shape=[pltpu.HBM(s.shape, s.dtype) for s in arrays],
        in_specs=[HBM_SPEC] * m + [SEM_SPEC, SEM_SPEC, ANY],
        out_specs=[HBM_SPEC] * m,
        input_output_aliases={i: i for i in range(m)},
        compiler_params=pltpu.CompilerParams(has_side_effects=EFFECT),
    )(*arrays, send_sems, recv_sems, after)


def gather_copies(n):
    def copies(stacks, send_sems, recv_sems):
        x, y, c, chips = _place()
        p = 2 * x + y
        return [_remote(stacks[i].at[p], stacks[i].at[p], send_sems.at[3 * i + j], recv_sems.at[3 * i + j], (*chip, c))
                for i in range(n) for j, chip in enumerate(chips)]
    return copies


def gather_half_copies(n):
    def copies(stacks, send_sems, recv_sems):
        x, y, c, chips = _place()
        p = 2 * x + y
        out = []
        for i in range(n):
            hr = stacks[i].shape[1] // 2
            mine = stacks[i].at[p, pl.ds(c * hr, hr)]
            out += [_remote(mine, mine, send_sems.at[3 * i + j], recv_sems.at[3 * i + j], (*chip, c))
                    for j, chip in enumerate(chips)]
        return out
    return copies


def forward_half_copies(n):
    def copies(stacks, send_sems, recv_sems):
        x, y, c, chips = _place()
        out = []
        for i in range(n):
            hr = stacks[i].shape[1] // 2
            for j, (px, py) in enumerate(chips):
                half = stacks[i].at[2 * px + py, pl.ds(c * hr, hr)]
                out.append(_remote(half, half, send_sems.at[3 * i + j], recv_sems.at[3 * i + j], (x, y, 1 - c)))
        return out
    return copies


def sibling_half_copies(n):
    def copies(refs, send_sems, recv_sems):
        x, y, c, _ = _place()
        out = []
        for i in range(n):
            hr = refs[n + i].shape[1]
            out.append(_remote(refs[i].at[:, pl.ds((1 - c) * hr, hr), :], refs[n + i], send_sems.at[i], recv_sems.at[i],
                               (x, y, 1 - c)))
        return out
    return copies


def chip_quarter_copies(n):
    def copies(refs, send_sems, recv_sems):
        _, _, c, chips = _place()
        return [_remote(refs[i].at[2 * px + py], refs[n + i].at[j], send_sems.at[3 * i + j], recv_sems.at[3 * i + j],
                        (px, py, c)) for i in range(n) for j, (px, py) in enumerate(chips)]
    return copies


def exchange_final_halves(grads, name):
    n = len(grads)

    def body(*refs):
        outs = refs[n:2 * n]
        send_sems, recv_sems = refs[2 * n:]
        x, y, c, _ = _place()
        sends = []
        for i in range(n):
            hr = outs[i].shape[0] // 2
            mine = outs[i].at[pl.ds(c * hr, hr)]
            cp = _remote(mine, mine, send_sems.at[i], recv_sems.at[i], (x, y, 1 - c))
            cp.start()
            sends.append(cp)
        for i in range(n):
            hr = outs[i].shape[0] // 2
            other = outs[i].at[pl.ds((1 - c) * hr, hr)]
            _remote(other, other, send_sems.at[i], recv_sems.at[i], (x, y, 1 - c)).wait_recv()
        for cp in sends:
            cp.wait_send()

    return _pcall(
        body, name=name,
        out_shape=[jax.ShapeDtypeStruct(g.shape, g.dtype) for g in grads],
        in_specs=[ANY] * n, out_specs=[ANY] * n,
        input_output_aliases={i: i for i in range(n)},
        scratch_shapes=[pltpu.SemaphoreType.DMA((n,)), pltpu.SemaphoreType.DMA((n,))],
    )(*grads)


def add_own_half(g, recv, c, name):
    s, r, cols = g.shape
    hr = r // 2
    tr = _row_tile(hr)
    nb = hr // tr

    def body(c_ref, g_ref, r_ref, o_ref):
        del c_ref
        o_ref[...] = (g_ref[...].astype(F32) + r_ref[...].astype(F32)).astype(BF16)

    return _pcall(
        body, name=name,
        grid_spec=pltpu.PrefetchScalarGridSpec(
            num_scalar_prefetch=1, grid=(s, nb),
            in_specs=[pl.BlockSpec((None, tr, cols), lambda j, i, c_ref: (j, c_ref[0] * nb + i, 0)),
                      pl.BlockSpec((None, tr, cols), lambda j, i, c_ref: (j, i, 0))],
            out_specs=pl.BlockSpec((None, tr, cols), lambda j, i, c_ref: (j, i, 0))),
        out_shape=jax.ShapeDtypeStruct((s, hr, cols), BF16),
        compiler_params=_params("parallel", "parallel"),
    )(c.reshape(1).astype(jnp.int32), g, recv)


def add_quarters(own, others, p, c, name, deps=()):
    _, hr, cols = own.shape
    tr = _row_tile(hr)
    nb = hr // tr

    def body(pc_ref, own_ref, oth_ref, *rest):
        o_ref = rest[-1]
        acc = own_ref[...].astype(F32) + oth_ref[0].astype(F32)
        acc = acc + oth_ref[1].astype(F32)
        o_ref[...] = acc + oth_ref[2].astype(F32)

    return _pcall(
        body, name=name,
        grid_spec=pltpu.PrefetchScalarGridSpec(
            num_scalar_prefetch=1, grid=(nb,),
            in_specs=[pl.BlockSpec((None, tr, cols), lambda i, pc: (pc[0], i, 0)),
                      pl.BlockSpec((3, tr, cols), lambda i, pc: (0, i, 0))] + [ANY] * len(deps),
            out_specs=pl.BlockSpec((tr, cols), lambda i, pc: (pc[1] * nb + i, 0))),
        out_shape=jax.ShapeDtypeStruct((2 * hr, cols), F32),
        compiler_params=_params("parallel"),
    )(jnp.stack([p, c]).astype(jnp.int32), own, others, *deps)


def _adamw_math(w, g, m, v):
    m2 = ADAM_B1 * m + (1.0 - ADAM_B1) * g
    v2 = ADAM_B2 * v + (1.0 - ADAM_B2) * (g * g)
    m_hat = m2 / (1.0 - ADAM_B1 ** ADAM_STEP)
    v_hat = v2 / (1.0 - ADAM_B2 ** ADAM_STEP)
    delta = -ADAM_LR * (m_hat / (jnp.sqrt(v_hat) + ADAM_EPS) + ADAM_WD * w)
    return delta, m2, v2


def adamw(w, g, m, v, name):
    r, cols = w.shape
    tr = _row_tile(r)

    def body(w_ref, g_ref, m_ref, v_ref, go_ref, d_ref, m2_ref, v2_ref):
        g = g_ref[...]
        go_ref[...] = g
        d_ref[...], m2_ref[...], v2_ref[...] = _adamw_math(w_ref[...], g, m_ref[...], v_ref[...])

    blk = pl.BlockSpec((tr, cols), lambda i: (i, 0))
    return _pcall(
        body, name=name, grid=(r // tr,),
        in_specs=[blk] * 4, out_specs=[blk] * 4,
        out_shape=[jax.ShapeDtypeStruct((r, cols), F32)] * 4,
        compiler_params=_params("parallel"),
    )(w, g, m, v)


def small_update(gathered, w, m, v):
    nd, r, lanes = gathered.shape

    def body(ga_ref, w_ref, m_ref, v_ref, g_ref, d_ref, m2_ref, v2_ref):
        g = ga_ref[0] + ga_ref[1]
        for dev in range(2, nd):
            g = g + ga_ref[dev]
        g_ref[...] = g
        d_ref[...], m2_ref[...], v2_ref[...] = _adamw_math(w_ref[...], g, m_ref[...], v_ref[...])

    return _pcall(
        body, name="small_update",
        out_shape=[jax.ShapeDtypeStruct((r, lanes), F32)] * 4,
        compiler_params=pltpu.CompilerParams(vmem_limit_bytes=VMEM_LIMIT),
    )(gathered, w, m, v)


def mod_fwd(c16, w_mod, b_mod, name):
    d, ns = w_mod.shape
    tn = 512

    def body(c_ref, w_ref, b_ref, o_ref):
        cv = c_ref[...]
        sc = (cv * _sigmoid(cv)).astype(BF16)
        o_ref[...] = _dot(sc, w_ref[...].astype(BF16)) + b_ref[...]

    return _pcall(
        body, name=name, grid=(ns // tn,),
        in_specs=[pl.BlockSpec((16, d), lambda j: (0, 0)), pl.BlockSpec((d, tn), lambda j: (0, j)),
                  pl.BlockSpec((1, tn), lambda j: (0, j))],
        out_specs=pl.BlockSpec((16, tn), lambda j: (0, j)),
        out_shape=jax.ShapeDtypeStruct((16, ns), F32),
        compiler_params=_params("parallel"),
    )(c16, w_mod, b_mod)


def wmod_update(c16, dmod16, w, m, v, name):
    d, ns = w.shape
    tn = 256

    def body(c_ref, dm_ref, w_ref, m_ref, v_ref, g_ref, d_ref, m2_ref, v2_ref):
        cv = c_ref[...]
        sc = (cv * _sigmoid(cv)).astype(BF16)
        g = _dot_tn(sc, dm_ref[...].astype(BF16))
        g_ref[...] = g
        d_ref[...], m2_ref[...], v2_ref[...] = _adamw_math(w_ref[...], g, m_ref[...], v_ref[...])

    blk = pl.BlockSpec((d, tn), lambda j: (0, j))
    return _pcall(
        body, name=name, grid=(ns // tn,),
        in_specs=[pl.BlockSpec((16, d), lambda j: (0, 0)), pl.BlockSpec((16, tn), lambda j: (0, j)), blk, blk, blk],
        out_specs=[blk] * 4,
        out_shape=[jax.ShapeDtypeStruct((d, ns), F32)] * 4,
        compiler_params=_params("parallel"),
    )(c16, dmod16, w, m, v)


def cctx_partial(dm0, w0, dm1, w1):
    d, ns = w0.shape
    tn = 512

    def body(dm0_ref, w0_ref, dm1_ref, w1_ref, o_ref):
        @pl.when(pl.program_id(0) == 0)
        def _():
            o_ref[...] = jnp.zeros_like(o_ref)

        for dm_ref, w_ref in ((dm0_ref, w0_ref), (dm1_ref, w1_ref)):
            tot = jnp.sum(dm_ref[...], axis=0, keepdims=True)
            lhs = jnp.broadcast_to(tot, (8, tn)).astype(BF16)
            o_ref[...] += _dot_nt(lhs, w_ref[...].astype(BF16))

    dspec = pl.BlockSpec((8, tn), lambda j: (0, j))
    wspec = pl.BlockSpec((d, tn), lambda j: (0, j))
    return _pcall(
        body, name="cctx_partial", grid=(ns // tn,),
        in_specs=[dspec, wspec, dspec, wspec],
        out_specs=pl.BlockSpec((8, d), lambda j: (0, 0)),
        out_shape=jax.ShapeDtypeStruct((8, d), F32),
        compiler_params=_params("arbitrary"),
    )(dm0, w0, dm1, w1)


def cctx_update(parts, c_ctx, m, v):
    d = c_ctx.shape[1]

    def body(p_ref, c_ref, m_ref, v_ref, g_ref, d_ref, m2_ref, v2_ref):
        tot = p_ref[0:1, :] + p_ref[1:2, :]
        tot = tot + p_ref[2:3, :]
        tot = tot + p_ref[3:4, :]
        cv = c_ref[...]
        sg = _sigmoid(cv)
        g = tot * (sg * (1.0 + cv * (1.0 - sg)))
        g_ref[...] = g
        d_ref[...], m2_ref[...], v2_ref[...] = _adamw_math(cv, g, m_ref[...], v_ref[...])

    return _pcall(
        body, name="cctx_update",
        out_shape=[jax.ShapeDtypeStruct((1, d), F32)] * 4,
    )(parts, c_ctx, m, v)


WEIGHT_NAMES = (
    "c_ctx", "l0_norm1_g", "l0_w_mod", "l0_b_mod", "l0_w_in", "l0_q_norm_g", "l0_k_norm_g", "l0_conv_w", "l0_w_out",
    "l0_norm2_g", "l0_w_gate", "l0_w_up", "l0_w_down", "l1_norm1_g", "l1_w_mod", "l1_b_mod", "l1_w_in",
    "l1_q_norm_g", "l1_k_norm_g", "l1_sink", "l1_pool_w", "l1_pool_scale", "l1_w_out", "l1_norm2_g", "l1_w_gate",
    "l1_w_up", "l1_w_down", "final_norm_g")
BIG_NAMES = ("w_in", "w_out", "w_gate", "w_up", "w_down")
SMALL_SLOTS = tuple(
    [(f"l{l}_{nm}", par and f"l{l}_{nm}") for l in (0, 1)
     for nm, par in (("dmod_lat", False), ("dmod_ctx", False), ("b_mod", True), ("norm1_g", True), ("norm2_g", True),
                     ("q_norm_g", True), ("k_norm_g", True))]
    + [("l0_conv_w", None), ("l1_sink", "l1_sink"), ("l1_pool_w", "l1_pool_w"), ("l1_pool_scale", "l1_pool_scale"),
       ("final_norm_g", "final_norm_g")])


def _pack(values, sizes):
    parts = []
    for (name, _), size in zip(SMALL_SLOTS, sizes):
        v = values.get(name)
        padded = -(-size // 128) * 128
        v = jnp.zeros((padded,), F32) if v is None else jnp.pad(v.reshape(-1).astype(F32), (0, padded - size))
        parts.append(v)
    total = sum(p.shape[0] for p in parts)
    parts.append(jnp.zeros((-(-total // 1024) * 1024 - total,), F32))
    return jnp.concatenate(parts).reshape(-1, 128)


def _offsets(sizes):
    offs, o = {}, 0
    for (name, _), size in zip(SMALL_SLOTS, sizes):
        offs[name] = (o, size)
        o += -(-size // 128) * 128
    return offs


def kernel(x, c, ctx, c_ctx, l0_norm1_g, l0_w_mod, l0_b_mod, l0_w_in, l0_q_norm_g, l0_k_norm_g, l0_conv_w, l0_w_out, l0_norm2_g, l0_w_gate, l0_w_up, l0_w_down, l1_norm1_g, l1_w_mod, l1_b_mod, l1_w_in, l1_q_norm_g, l1_k_norm_g, l1_sink, l1_pool_w, l1_pool_scale, l1_w_out, l1_norm2_g, l1_w_gate, l1_w_up, l1_w_down, final_norm_g, loss_target, m_c_ctx, m_l0_norm1_g, m_l0_w_mod, m_l0_b_mod, m_l0_w_in, m_l0_q_norm_g, m_l0_k_norm_g, m_l0_conv_w, m_l0_w_out, m_l0_norm2_g, m_l0_w_gate, m_l0_w_up, m_l0_w_down, m_l1_norm1_g, m_l1_w_mod, m_l1_b_mod, m_l1_w_in, m_l1_q_norm_g, m_l1_k_norm_g, m_l1_sink, m_l1_pool_w, m_l1_pool_scale, m_l1_w_out, m_l1_norm2_g, m_l1_w_gate, m_l1_w_up, m_l1_w_down, m_final_norm_g, v_c_ctx, v_l0_norm1_g, v_l0_w_mod, v_l0_b_mod, v_l0_w_in, v_l0_q_norm_g, v_l0_k_norm_g, v_l0_conv_w, v_l0_w_out, v_l0_norm2_g, v_l0_w_gate, v_l0_w_up, v_l0_w_down, v_l1_norm1_g, v_l1_w_mod, v_l1_b_mod, v_l1_w_in, v_l1_q_norm_g, v_l1_k_norm_g, v_l1_sink, v_l1_pool_w, v_l1_pool_scale, v_l1_w_out, v_l1_norm2_g, v_l1_w_gate, v_l1_w_up, v_l1_w_down, v_final_norm_g):
    a = dict(locals())
    xi, yi, ci = lax.axis_index("x"), lax.axis_index("y"), lax.axis_index("c")
    p = 2 * xi + yi
    me = 4 * xi + 2 * yi + ci
    d = x.shape[-1]
    conv_cols = l0_conv_w.shape[1]

    row0 = jnp.concatenate([c, jnp.pad(l0_conv_w, ((0, 0), (0, d - conv_cols))), jnp.zeros((4, d), F32)], axis=0)
    gathered = all_gather8(row0, "gather_cond").reshape(N_DEV, 8, d)
    c_all = gathered[:, 0]
    conv_w = gathered[0::2, 1:4, :conv_cols].transpose(1, 0, 2).reshape(3, N_SHARD * conv_cols)
    c16_fwd = jnp.concatenate([c_all, c_ctx[None], jnp.zeros((7, d), F32)], axis=0)
    c16_bwd = jnp.concatenate([c_all, jnp.broadcast_to(c_ctx[None], (8, d))], axis=0)

    ns_mod = l0_w_mod.shape[1]
    mod_parts = [mod_fwd(c16_fwd, a[f"l{l}_w_mod"], lax.dynamic_slice(a[f"l{l}_b_mod"], (p * ns_mod,), (ns_mod,))[None],
                         f"l{l}_mod_fwd") for l in (0, 1)]
    modg = all_gather8(jnp.concatenate(mod_parts, axis=0), "gather_mod").reshape(N_DEV, 2, 16, ns_mod)[0::2]
    mod_full = modg.transpose(1, 2, 0, 3).reshape(2, 16, N_SHARD * ns_mod)
    mods = []
    for l in (0, 1):
        lat = lax.dynamic_index_in_dim(mod_full[l], me, axis=0, keepdims=False)
        cx = mod_full[l, 8]
        mods.append({nm: jnp.stack([cx[j * d:(j + 1) * d], lat[j * d:(j + 1) * d]])[:, None, :]
                     for j, nm in enumerate(MOD_NAMES)})

    placed = {(l, nm): place_shard(a[f"l{l}_{nm}"], p, f"l{l}_{nm}_place") for l in (0, 1) for nm in BIG_NAMES}
    ready = {(0, "w_in"): gather_weight_shards([placed[0, "w_in"]], mod_full)[0]}
    groups = (((0, "w_out"), (0, "w_gate"), (0, "w_up")), ((0, "w_down"),), ((1, "w_in"), (1, "w_out")),
              ((1, "w_gate"), (1, "w_up"), (1, "w_down")))
    staged = len(groups) - 1
    flights, after = [], ready[0, "w_in"]
    for gi, grp in enumerate(groups):
        first_copies = gather_half_copies if gi == staged else gather_copies
        flights.append(split_start(f"gather_start_{gi}", first_copies(len(grp)), 3 * len(grp),
                                   [placed[k] for k in grp], after))
        after = flights[-1][3]

    def forward_staged(after):
        n = len(groups[staged])
        send_sems, recv_sems, in_flight, _ = flights[staged]
        landed = split_wait(f"gather_wait_{staged}", gather_half_copies(n), send_sems, recv_sems, in_flight, after)
        flights[staged] = split_start("gather_forward_start", forward_half_copies(n), 3 * n, landed, after)
        return flights[staged][3]

    def weight_of(l):
        def weight(nm, after):
            if (l, nm) not in ready:
                gi = next(i for i, grp in enumerate(groups) if (l, nm) in grp)
                send_sems, recv_sems, in_flight, _ = flights[gi]
                name, copies = ("gather_forward_wait", forward_half_copies) if gi == staged else (
                    f"gather_wait_{gi}", gather_copies)
                landed = split_wait(name, copies(len(groups[gi])), send_sems, recv_sems, in_flight, after)
                ready.update(zip(groups[gi], landed))
            return ready[l, nm]
        return weight

    layers = [dict(norm1_g=a[f"l{l}_norm1_g"][None], norm2_g=a[f"l{l}_norm2_g"][None],
                   q_g=a[f"l{l}_q_norm_g"][None], k_g=a[f"l{l}_k_norm_g"][None]) for l in (0, 1)]
    layers[0].update(conv_w=conv_w, sink=jnp.zeros((Q_HEADS,), F32))
    layers[1].update(sink=l1_sink, pool_w=l1_pool_w, pool_scale=l1_pool_scale[None])

    ctx_len = ctx.shape[1]
    h = jnp.concatenate([ctx[0], x[0]], axis=0)
    tabs = _rope_tables(h.shape[0], ctx_len)
    h, saved0 = _layer_fwd(h, layers[0], weight_of(0), _tie(mods[0], "shift1", flights[-1][3]), tabs, ctx_len, False,
                           "l0_")
    h, saved1 = _layer_fwd(h, layers[1], weight_of(1), mods[1], tabs, ctx_len, True, "l1_", forward_staged)
    loss_tile, dh, dfg = final_loss(h, final_norm_g[None], loss_target[0])

    keys = [[f"l{l}_{nm}" for nm in BIG_NAMES] for l in (0, 1)]
    n_big = len(BIG_NAMES)
    dh, ffn_big, ffn_small, ffn_mod = _layer_bwd_ffn(dh, saved1, layers[1], mods[1], "l1_")
    dh, mix_big, mix_small, mix_mod = _layer_bwd_mixer(dh, saved1, layers[1], mods[1], tabs, ctx_len, True, "l1_")
    bigs, smalls, dmods = [None, {**ffn_big, **mix_big}], [None, {**ffn_small, **mix_small}], [None, None]
    dmods[1] = _dmod_rows(mix_mod, ffn_mod)
    grads1 = [bigs[1][nm] for nm in BIG_NAMES]
    lands = [lax.empty((g.shape[0], g.shape[1] // 2, g.shape[2]), g.dtype) for g in grads1]
    send_a, recv_a, fly_a, token_a = split_start("l1_sibling_halves_start", sibling_half_copies(n_big), n_big,
                                                 grads1 + lands, dh)
    dh, ffn_big, ffn_small, ffn_mod = _layer_bwd_ffn(dh, saved0, layers[0], _tie(mods[0], "gate2", token_a), "l0_")
    fly_a = split_wait("l1_sibling_halves_wait", sibling_half_copies(n_big), send_a, recv_a, fly_a, dh)
    chip_sums1 = [add_own_half(g, r, ci, k + "_add_halves")
                  for k, g, r in zip(keys[1], fly_a[:n_big], fly_a[n_big:])]
    lands = [lax.empty((3,) + s.shape[1:], s.dtype) for s in chip_sums1]
    send_b, recv_b, fly_b, token_b = split_start("l1_chip_quarters_start", chip_quarter_copies(n_big), 3 * n_big,
                                                 chip_sums1 + lands, dh)
    ffn_names = ("w_gate", "w_up", "w_down")
    n_ffn = len(ffn_names)
    ffn_keys = ["l0_" + nm for nm in ffn_names]
    grads0f = [ffn_big[nm] for nm in ffn_names]
    lands = [lax.empty((g.shape[0], g.shape[1] // 2, g.shape[2]), g.dtype) for g in grads0f]
    send_c, recv_c, fly_c, token_c = split_start("l0_ffn_sibling_halves_start", sibling_half_copies(n_ffn), n_ffn,
                                                 grads0f + lands, token_b)
    ffn_flight = []

    def between(dq):
        landed = split_wait("l0_ffn_sibling_halves_wait", sibling_half_copies(n_ffn), send_c, recv_c, fly_c, dq)
        sums = [add_own_half(g, r, ci, k + "_add_halves") for k, g, r in zip(ffn_keys, landed[:n_ffn], landed[n_ffn:])]
        slots = [lax.empty((3,) + s.shape[1:], s.dtype) for s in sums]
        ffn_flight.extend(split_start("l0_ffn_chip_quarters_start", chip_quarter_copies(n_ffn), 3 * n_ffn,
                                      sums + slots, dq))
        return ffn_flight[3]

    dh, mix_big, mix_small, mix_mod = _layer_bwd_mixer(dh, saved0, layers[0], _tie(mods[0], "gate1", token_c), tabs,
                                                       ctx_len, False, "l0_", between)
    fly_b = split_wait("l1_chip_quarters_wait", chip_quarter_copies(n_big), send_b, recv_b, fly_b, dh)
    fly_d = split_wait("l0_ffn_chip_quarters_wait", chip_quarter_copies(n_ffn), ffn_flight[0], ffn_flight[1],
                       ffn_flight[2], dh)
    bigs[0], smalls[0], dmods[0] = {**ffn_big, **mix_big}, {**ffn_small, **mix_small}, _dmod_rows(mix_mod, ffn_mod)
    dx = dh[ctx_len:]
    chip_sums = dict(zip(keys[1] + ffn_keys, list(fly_b[:n_big]) + list(fly_d[:n_ffn])))
    quarters = dict(zip(keys[1] + ffn_keys, list(fly_b[n_big:]) + list(fly_d[n_ffn:])))

    partial = {"l0_conv_w": smalls[0]["conv_w"], "l1_sink": smalls[1]["sink"], "l1_pool_w": smalls[1]["pool_w"],
               "l1_pool_scale": smalls[1]["pool_scale"], "final_norm_g": dfg}
    for l in (0, 1):
        partial.update({f"l{l}_dmod_lat": dmods[l][1], f"l{l}_dmod_ctx": dmods[l][0],
                        f"l{l}_b_mod": dmods[l][0] + dmods[l][1], f"l{l}_norm1_g": smalls[l]["norm1_g"],
                        f"l{l}_norm2_g": smalls[l]["norm2_g"], f"l{l}_q_norm_g": smalls[l]["q_g"],
                        f"l{l}_k_norm_g": smalls[l]["k_g"]})
    sizes = [int(np.prod(partial[name].shape)) for name, _ in SMALL_SLOTS]
    offs = _offsets(sizes)

    last_keys = ["l0_w_in", "l0_w_out"]
    grads_last = [bigs[0]["w_in"], bigs[0]["w_out"]]
    lands = [lax.empty((g.shape[0], g.shape[1] // 2, g.shape[2]), g.dtype) for g in grads_last]
    send_e, recv_e, fly_e, token_e = split_start("l0_last_sibling_halves_start", sibling_half_copies(2), 2,
                                                 grads_last + lands, dh)
    gpack = _pack(partial, sizes) + token_e[0:1, 0:1]
    rows = gpack.shape[0]
    small_all = all_gather8(gpack, "gather_small").reshape(N_DEV, rows, 128)
    packs = [_pack({name: a[pre + par] for name, par in SMALL_SLOTS if par}, sizes) for pre in ("", "m_", "v_")]
    small_out = [t.reshape(-1) for t in small_update(small_all, *packs)]
    flat_all = small_all.reshape(N_DEV, rows * 128)

    def slot(flat, name, shape):
        o, size = offs[name]
        return flat[o:o + size].reshape(shape)

    results = {}
    for name, par in SMALL_SLOTS:
        if par:
            results[par] = tuple(slot(t, name, a[par].shape) for t in small_out)

    ctx_rows = []
    for l in (0, 1):
        o_lat, o_ctx = offs[f"l{l}_dmod_lat"][0], offs[f"l{l}_dmod_ctx"][0]
        lat = lax.dynamic_slice(flat_all, (0, o_lat + p * ns_mod), (N_DEV, ns_mod))
        cxr = lax.dynamic_slice(flat_all, (0, o_ctx + p * ns_mod), (N_DEV, ns_mod))
        ctx_rows.append(cxr)
        results[f"l{l}_w_mod"] = tuple(wmod_update(c16_bwd, jnp.concatenate([lat, cxr], axis=0), a[f"l{l}_w_mod"],
                                                   a[f"m_l{l}_w_mod"], a[f"v_l{l}_w_mod"], f"l{l}_w_mod_update"))
    part = cctx_partial(ctx_rows[0], l0_w_mod, ctx_rows[1], l1_w_mod)
    parts4 = all_gather8(part, "gather_cctx").reshape(N_DEV, 8, d)[0::2, 0]
    results["c_ctx"] = tuple(t[0] for t in cctx_update(parts4, c_ctx[None], m_c_ctx[None], v_c_ctx[None]))

    gconv = lax.dynamic_slice(slot(small_out[0], "l0_conv_w", (3, N_SHARD * conv_cols)), (0, p * conv_cols),
                              (3, conv_cols))
    results["l0_conv_w"] = tuple(adamw(l0_conv_w, gconv, m_l0_conv_w, v_l0_conv_w, "l0_conv_w_adamw"))

    fly_e = split_wait("l0_last_sibling_halves_wait", sibling_half_copies(2), send_e, recv_e, fly_e,
                       results["c_ctx"][0])
    sums_last = [add_own_half(g, r, ci, k + "_add_halves") for k, g, r in zip(last_keys, fly_e[:2], fly_e[2:])]
    slots = [lax.empty((3,) + s.shape[1:], s.dtype) for s in sums_last]
    send_f, recv_f, fly_f, token_f = split_start("l0_last_chip_quarters_start", chip_quarter_copies(2), 6,
                                                 sums_last + slots, sums_last[0])

    def finish(names, tag, deps):
        halves = [add_quarters(chip_sums[k], quarters[k], p, ci, k + "_add_quarters", deps) for k in names]
        for k, g in zip(names, exchange_final_halves(halves, "exchange_final_halves_" + tag)):
            results[k] = tuple(adamw(a[k], g, a["m_" + k], a["v_" + k], k + "_adamw"))

    done_keys = keys[1] + ffn_keys
    finish(done_keys, "first", (token_f,))
    fly_f = split_wait("l0_last_chip_quarters_wait", chip_quarter_copies(2), send_f, recv_f, fly_f,
                       results[done_keys[-1]][1])
    chip_sums.update(zip(last_keys, fly_f[:2]))
    quarters.update(zip(last_keys, fly_f[2:]))
    finish(last_keys, "last", ())

    loss = lax.psum(loss_tile[0, 0], ("x", "y", "c"))
    out = [loss, dx[None]]
    for j in range(4):
        out += [results[k][j] for k in WEIGHT_NAMES]
    return tuple(out)
```

```python
import numpy as np
import jax
import jax.numpy as jnp
from jax import lax
from jax.experimental import pallas as pl
from jax.experimental.pallas import tpu as pltpu

F32 = jnp.float32
BF16 = jnp.bfloat16
MESH = pl.DeviceIdType.MESH

HEAD_DIM = 128
Q_HEADS = 12
KV_HEADS = 4
GROUPS = Q_HEADS // KV_HEADS
ATT_WIDTH = Q_HEADS * HEAD_DIM
KV_WIDTH = KV_HEADS * HEAD_DIM
QKV_WIDTH = ATT_WIDTH + 2 * KV_WIDTH
AUX_WIDTH = 512
AUX_GROUPS = 4
POOL_WINDOWS = (2, 4, 8, 16)
WINDOW = 128
GRID_W = 64
ROPE_THETA = 10000.0
N_MOD = 6
EPS = 1e-6
NEG_INF = -1e30
ATT_SCALE = HEAD_DIM ** -0.5
N_SHARD = 4
N_DEV = 8

ADAM_LR = 0.001
ADAM_B1 = 0.9
ADAM_B2 = 0.999
ADAM_EPS = 1e-08
ADAM_WD = 0.01
ADAM_STEP = 10

TM = 256
TM_MATMUL = 544
BAND = TM + 2 * WINDOW
VMEM_LIMIT = 56 * 1024 * 1024

NT_DIMS = (((1,), (1,)), ((), ()))
TN_DIMS = (((0,), (0,)), ((), ()))


def _pcall(body, **kw):
    return pl.pallas_call(body, **kw)


def _params(*sem):
    return pltpu.CompilerParams(dimension_semantics=sem, vmem_limit_bytes=VMEM_LIMIT)


def _sel(i):
    return jnp.minimum(i, 1)


def _sel_spec(d):
    return pl.BlockSpec((None, 1, d), lambda i: (_sel(i), 0, 0))


def _dot(a, b):
    return jnp.dot(a, b, preferred_element_type=F32)


def _dot_nt(a, b):
    return lax.dot_general(a, b, NT_DIMS, preferred_element_type=F32)


def _dot_tn(a, b):
    return lax.dot_general(a, b, TN_DIMS, preferred_element_type=F32)


def _sigmoid(x):
    return 0.5 * jnp.tanh(0.5 * x) + 0.5


def norm_mod_fwd(h, g, shift, scale, name):
    t, d = h.shape

    def body(h_ref, g_ref, sh_ref, sc_ref, o_ref):
        x = h_ref[...]
        r = lax.rsqrt(jnp.mean(x * x, axis=-1, keepdims=True) + EPS)
        y = x * r * g_ref[...]
        o_ref[...] = (y * (1.0 + sc_ref[...]) + sh_ref[...]).astype(BF16)

    return _pcall(
        body, name=name, grid=(t // TM,),
        in_specs=[pl.BlockSpec((TM, d), lambda i: (i, 0)), pl.BlockSpec((1, d), lambda i: (0, 0)),
                  _sel_spec(d), _sel_spec(d)],
        out_specs=pl.BlockSpec((TM, d), lambda i: (i, 0)),
        out_shape=jax.ShapeDtypeStruct((t, d), BF16),
        compiler_params=_params("parallel"),
    )(h, g, shift, scale)


def norm_mod_bwd(dy, h, g, scale, dres, name):
    t, d = h.shape

    def body(dy_ref, h_ref, g_ref, sc_ref, dres_ref, dh_ref, dsh_ref, dsc_ref, dg_ref):
        i = pl.program_id(0)

        @pl.when(i == 0)
        def _():
            dsh_ref[...] = jnp.zeros_like(dsh_ref)
            dsc_ref[...] = jnp.zeros_like(dsc_ref)
            dg_ref[...] = jnp.zeros_like(dg_ref)

        x = h_ref[...]
        gv = g_ref[...]
        r = lax.rsqrt(jnp.mean(x * x, axis=-1, keepdims=True) + EPS)
        xhat = x * r
        dyv = dy_ref[...]
        s = _sel(i)
        dsh_ref[pl.ds(s, 1)] += jnp.sum(dyv, axis=0, keepdims=True)[None]
        dsc_ref[pl.ds(s, 1)] += jnp.sum(dyv * xhat * gv, axis=0, keepdims=True)[None]
        tt = dyv * (1.0 + sc_ref[...])
        dg_ref[...] += jnp.sum(tt * xhat, axis=0, keepdims=True)
        dxhat = tt * gv
        dx = r * (dxhat - xhat * jnp.mean(dxhat * xhat, axis=-1, keepdims=True))
        dh_ref[...] = dx + dres_ref[...]

    row = pl.BlockSpec((TM, d), lambda i: (i, 0))
    acc2 = pl.BlockSpec((2, 1, d), lambda i: (0, 0, 0))
    return _pcall(
        body, name=name, grid=(t // TM,),
        in_specs=[row, row, pl.BlockSpec((1, d), lambda i: (0, 0)), _sel_spec(d), row],
        out_specs=[row, acc2, acc2, pl.BlockSpec((1, d), lambda i: (0, 0))],
        out_shape=[jax.ShapeDtypeStruct((t, d), F32), jax.ShapeDtypeStruct((2, 1, d), F32),
                   jax.ShapeDtypeStruct((2, 1, d), F32), jax.ShapeDtypeStruct((1, d), F32)],
        compiler_params=_params("arbitrary"),
    )(dy, h, g, scale, dres)


def gate_bwd(dh, f, gate, name):
    t, d = dh.shape

    def body(dh_ref, f_ref, gt_ref, df_ref, dgt_ref):
        i = pl.program_id(0)

        @pl.when(i == 0)
        def _():
            dgt_ref[...] = jnp.zeros_like(dgt_ref)

        dhv = dh_ref[...]
        df_ref[...] = (dhv * gt_ref[...]).astype(BF16)
        dgt_ref[pl.ds(_sel(i), 1)] += jnp.sum(dhv * f_ref[...], axis=0, keepdims=True)[None]

    row = pl.BlockSpec((TM, d), lambda i: (i, 0))
    return _pcall(
        body, name=name, grid=(t // TM,),
        in_specs=[row, row, _sel_spec(d)],
        out_specs=[row, pl.BlockSpec((2, 1, d), lambda i: (0, 0, 0))],
        out_shape=[jax.ShapeDtypeStruct((t, d), BF16), jax.ShapeDtypeStruct((2, 1, d), F32)],
        compiler_params=_params("arbitrary"),
    )(dh, f, gate)


def final_loss(h, fg, target):
    t, d = h.shape

    def body(h_ref, g_ref, tg_ref, loss_ref, dh_ref, dg_ref):
        i = pl.program_id(0)

        @pl.when(i == 0)
        def _():
            loss_ref[...] = jnp.zeros_like(loss_ref)
            dg_ref[...] = jnp.zeros_like(dg_ref)
            dh_ref[...] = jnp.zeros_like(dh_ref)

        @pl.when(i > 0)
        def _():
            x = h_ref[...]
            gv = g_ref[...]
            r = lax.rsqrt(jnp.mean(x * x, axis=-1, keepdims=True) + EPS)
            xhat = x * r
            diff = xhat * gv - tg_ref[...]
            loss_ref[...] += 0.5 * jnp.sum(jnp.mean(diff * diff, axis=-1, keepdims=True), axis=0, keepdims=True)
            dout = diff * (1.0 / d)
            dg_ref[...] += jnp.sum(dout * xhat, axis=0, keepdims=True)
            dxhat = dout * gv
            dh_ref[...] = r * (dxhat - xhat * jnp.mean(dxhat * xhat, axis=-1, keepdims=True))

    row = pl.BlockSpec((TM, d), lambda i: (i, 0))
    return _pcall(
        body, name="final_loss", grid=(t // TM,),
        in_specs=[row, pl.BlockSpec((1, d), lambda i: (0, 0)),
                  pl.BlockSpec((TM, d), lambda i: (jnp.maximum(i - 1, 0), 0))],
        out_specs=[pl.BlockSpec((8, 128), lambda i: (0, 0)), row, pl.BlockSpec((1, d), lambda i: (0, 0))],
        out_shape=[jax.ShapeDtypeStruct((8, 128), F32), jax.ShapeDtypeStruct((t, d), F32),
                   jax.ShapeDtypeStruct((1, d), F32)],
        compiler_params=_params("arbitrary"),
    )(h, fg, target)


def _matmul_rows(t):
    return TM_MATMUL if t % TM_MATMUL == 0 else TM


def _w_nn_spec(w, k):
    if w.shape[1] == k:
        ns = w.shape[2]
        return w, pl.BlockSpec((None, k, ns), lambda j, i: (j, 0, 0)), w.shape[0], ns
    w2 = w.reshape(w.shape[0] * w.shape[1], w.shape[2])
    tn = 1024 if w2.shape[1] % 1024 == 0 else w2.shape[1]
    return w2, pl.BlockSpec((k, tn), lambda j, i: (0, j)), w2.shape[1] // tn, tn


def mm_nn(a, w, name, out_dtype=F32):
    t, k = a.shape
    w, wspec, nb, tn = _w_nn_spec(w, k)
    tm = _matmul_rows(t)

    def body(a_ref, w_ref, o_ref):
        o_ref[...] = _dot(a_ref[...], w_ref[...]).astype(o_ref.dtype)

    return _pcall(
        body, name=name, grid=(nb, t // tm),
        in_specs=[pl.BlockSpec((tm, k), lambda j, i: (i, 0)), wspec],
        out_specs=pl.BlockSpec((tm, tn), lambda j, i: (i, j)),
        out_shape=jax.ShapeDtypeStruct((t, nb * tn), out_dtype),
        compiler_params=_params("parallel", "parallel"),
    )(a, w)


def mm_nn_residual(a, w, h, gate, name):
    t, k = a.shape
    w, wspec, nb, tn = _w_nn_spec(w, k)
    tm = _matmul_rows(t)

    def body(a_ref, w_ref, h_ref, gt_ref, y_ref, o_ref):
        y = _dot(a_ref[...], w_ref[...])
        y_ref[...] = y
        row = pl.program_id(1) * tm + lax.broadcasted_iota(jnp.int32, (tm, tn), 0)
        o_ref[...] = h_ref[...] + jnp.where(row < TM, gt_ref[0], gt_ref[1]) * y

    out = pl.BlockSpec((tm, tn), lambda j, i: (i, j))
    return _pcall(
        body, name=name, grid=(nb, t // tm),
        in_specs=[pl.BlockSpec((tm, k), lambda j, i: (i, 0)), wspec, out,
                  pl.BlockSpec((2, 1, tn), lambda j, i: (0, 0, j))],
        out_specs=[out, out],
        out_shape=[jax.ShapeDtypeStruct((t, nb * tn), F32)] * 2,
        compiler_params=_params("parallel", "parallel"),
    )(a, w, h, gate)


def mm_swiglu(a, wg, wu, name):
    t, k = a.shape
    s, _, ns = wg.shape

    def body(a_ref, wg_ref, wu_ref, g_ref, u_ref, act_ref):
        av = a_ref[...]
        g = _dot(av, wg_ref[...])
        u = _dot(av, wu_ref[...])
        g_ref[...] = g.astype(BF16)
        u_ref[...] = u.astype(BF16)
        act_ref[...] = (g * _sigmoid(g) * u).astype(BF16)

    tm = _matmul_rows(t)
    wspec = pl.BlockSpec((None, k, ns), lambda j, i: (j, 0, 0))
    out = pl.BlockSpec((tm, ns), lambda j, i: (i, j))
    return _pcall(
        body, name=name, grid=(s, t // tm),
        in_specs=[pl.BlockSpec((tm, k), lambda j, i: (i, 0)), wspec, wspec],
        out_specs=[out, out, out],
        out_shape=[jax.ShapeDtypeStruct((t, s * ns), BF16)] * 3,
        compiler_params=_params("parallel", "parallel"),
    )(a, wg, wu)


def mm_nt_cols(dy, w, name, add=None):
    t, n = dy.shape
    s, k, ns = w.shape
    tk = 1024

    def body(*refs):
        if add is None:
            dy_ref, w_ref, o_ref = refs
        else:
            dy_ref, w_ref, add_ref, o_ref = refs
        acc = _dot_nt(dy_ref[:, 0:ns], w_ref[0])
        for sh in range(1, s):
            acc += _dot_nt(dy_ref[:, sh * ns:(sh + 1) * ns], w_ref[sh])
        if add is not None:
            acc += add_ref[...]
        o_ref[...] = acc

    tm = _matmul_rows(t)
    out = pl.BlockSpec((tm, tk), lambda j, i: (i, j))
    in_specs = [pl.BlockSpec((tm, n), lambda j, i: (i, 0)), pl.BlockSpec((s, tk, ns), lambda j, i: (0, j, 0))]
    args = [dy, w]
    if add is not None:
        in_specs.append(out)
        args.append(add)
    return _pcall(
        body, name=name, grid=(k // tk, t // tm),
        in_specs=in_specs, out_specs=out,
        out_shape=jax.ShapeDtypeStruct((t, k), F32),
        compiler_params=_params("parallel", "parallel"),
    )(*args)


def mm_nt_rows(dy, w, name):
    t, n = dy.shape
    s, ks, _ = w.shape

    tm = _matmul_rows(t)

    def body(dy_ref, w_ref, o_ref):
        o_ref[...] = _dot_nt(dy_ref[...], w_ref[...])

    return _pcall(
        body, name=name, grid=(s, t // tm),
        in_specs=[pl.BlockSpec((tm, n), lambda j, i: (i, 0)), pl.BlockSpec((None, ks, n), lambda j, i: (j, 0, 0))],
        out_specs=pl.BlockSpec((tm, ks), lambda j, i: (i, j)),
        out_shape=jax.ShapeDtypeStruct((t, s * ks), F32),
        compiler_params=_params("parallel", "parallel"),
    )(dy, w)


def mm_nt_swiglu_bwd(df, wd, g, u, name):
    t, n = df.shape
    s, ks, _ = wd.shape

    def body(df_ref, w_ref, g_ref, u_ref, dg_ref, du_ref):
        da = _dot_nt(df_ref[...], w_ref[...])
        gv = g_ref[...].astype(F32)
        sg = _sigmoid(gv)
        silu = gv * sg
        dg_ref[...] = (da * u_ref[...].astype(F32) * (sg * (1.0 + gv * (1.0 - sg)))).astype(BF16)
        du_ref[...] = (da * silu).astype(BF16)

    tm = _matmul_rows(t)
    blk = pl.BlockSpec((tm, ks), lambda j, i: (i, j))
    return _pcall(
        body, name=name, grid=(s, t // tm),
        in_specs=[pl.BlockSpec((tm, n), lambda j, i: (i, 0)), pl.BlockSpec((None, ks, n), lambda j, i: (j, 0, 0)),
                  blk, blk],
        out_specs=[blk, blk],
        out_shape=[jax.ShapeDtypeStruct((t, s * ks), BF16)] * 2,
        compiler_params=_params("parallel", "parallel"),
    )(df, wd, g, u)


def mm_tn(x, dy, name, col_sharded):
    t, k = x.shape
    n = dy.shape[1]
    tk = next(c for c in (1024, 1408, 512, k) if k % c == 0)
    mc = t // 2 if (t // 2) % 128 == 0 else t
    n_m = t // mc
    if col_sharded:
        tn = n // N_SHARD
        out_shape = jax.ShapeDtypeStruct((N_SHARD, k, tn), BF16)
        out_spec = pl.BlockSpec((None, tk, tn), lambda j, kb, m: (j, kb, 0))
    else:
        tn = 1024
        out_shape = jax.ShapeDtypeStruct((k, n), BF16)
        out_spec = pl.BlockSpec((tk, tn), lambda j, kb, m: (kb, j))

    def body(x_ref, dy_ref, o_ref, acc_ref):
        m = pl.program_id(2)
        acc = _dot_tn(x_ref[...], dy_ref[...])

        @pl.when(m == 0)
        def _():
            acc_ref[...] = acc

        @pl.when((m > 0) & (m < n_m - 1))
        def _():
            acc_ref[...] += acc

        @pl.when(m == n_m - 1)
        def _():
            o_ref[...] = (acc if n_m == 1 else acc_ref[...] + acc).astype(BF16)

    out = _pcall(
        body, name=name, grid=(n // tn, k // tk, n_m),
        in_specs=[pl.BlockSpec((mc, tk), lambda j, kb, m: (m, kb)), pl.BlockSpec((mc, tn), lambda j, kb, m: (m, j))],
        out_specs=out_spec, out_shape=out_shape,
        scratch_shapes=[pltpu.VMEM((tk, tn), F32)],
        compiler_params=_params("parallel", "parallel", "arbitrary"),
    )(x, dy)
    return out if col_sharded else out.reshape(N_SHARD, k // N_SHARD, n)


def _swap32(y):
    right = pltpu.roll(y, 32, 1)
    left = pltpu.roll(y, 96, 1)
    lane = lax.broadcasted_iota(jnp.int32, y.shape, 1)
    return jnp.where((lane // 32) % 2 == 0, left, right)


def _rope_tables(t, ctx_len):
    s = t - ctx_len
    pos = np.arange(s)
    row = (pos // GRID_W).astype(np.float32)
    col = (pos % GRID_W).astype(np.float32)
    half = HEAD_DIM // 2
    inv = np.power(np.float32(ROPE_THETA), -np.arange(0, half, 2, dtype=np.float32) / np.float32(half))
    ar = row[:, None] * inv
    ac = col[:, None] * inv
    cos = np.concatenate([np.cos(ar), np.cos(ar), np.cos(ac), np.cos(ac)], axis=1)
    sin = np.concatenate([-np.sin(ar), np.sin(ar), -np.sin(ac), np.sin(ac)], axis=1)
    cos = np.concatenate([np.ones((ctx_len, HEAD_DIM), np.float32), cos.astype(np.float32)], axis=0)
    sin = np.concatenate([np.zeros((ctx_len, HEAD_DIM), np.float32), sin.astype(np.float32)], axis=0)
    return jnp.asarray(cos, F32), jnp.asarray(sin, F32)


def qk_prep_fwd(proj, qg, kg, cos, sin, name):
    t = proj.shape[0]

    def body(p_ref, qg_ref, kg_ref, c_ref, s_ref, q_ref, k_ref, v_ref):
        cv = c_ref[...]
        sv = s_ref[...]
        for hd in range(Q_HEADS + KV_HEADS):
            x = p_ref[:, hd * HEAD_DIM:(hd + 1) * HEAD_DIM]
            gv = qg_ref[...] if hd < Q_HEADS else kg_ref[...]
            y = x * lax.rsqrt(jnp.mean(x * x, axis=-1, keepdims=True) + EPS) * gv
            out = y * cv + _swap32(y) * sv
            if hd < Q_HEADS:
                q_ref[:, hd * HEAD_DIM:(hd + 1) * HEAD_DIM] = (out * ATT_SCALE).astype(BF16)
            else:
                k_ref[:, (hd - Q_HEADS) * HEAD_DIM:(hd - Q_HEADS + 1) * HEAD_DIM] = out.astype(BF16)
        v_ref[...] = p_ref[:, ATT_WIDTH + KV_WIDTH:QKV_WIDTH].astype(BF16)

    vec = pl.BlockSpec((1, HEAD_DIM), lambda i: (0, 0))
    tab = pl.BlockSpec((TM, HEAD_DIM), lambda i: (i, 0))
    return _pcall(
        body, name=name, grid=(t // TM,),
        in_specs=[pl.BlockSpec((TM, QKV_WIDTH), lambda i: (i, 0)), vec, vec, tab, tab],
        out_specs=[pl.BlockSpec((TM, ATT_WIDTH), lambda i: (i, 0)), pl.BlockSpec((TM, KV_WIDTH), lambda i: (i, 0)),
                   pl.BlockSpec((TM, KV_WIDTH), lambda i: (i, 0))],
        out_shape=[jax.ShapeDtypeStruct((t, ATT_WIDTH), BF16), jax.ShapeDtypeStruct((t, KV_WIDTH), BF16),
                   jax.ShapeDtypeStruct((t, KV_WIDTH), BF16)],
        compiler_params=_params("parallel"),
    )(proj, qg, kg, cos, sin)


def qk_prep_bwd(dq, dk, dv, daux, proj, qg, kg, cos, sin, name):
    t, n_in = proj.shape
    n_aux = daux.shape[0]

    def body(dq_ref, dk_ref, dv_ref, da_ref, p_ref, qg_ref, kg_ref, c_ref, s_ref, o_ref, dqg_ref, dkg_ref):
        @pl.when(pl.program_id(0) == 0)
        def _():
            dqg_ref[...] = jnp.zeros_like(dqg_ref)
            dkg_ref[...] = jnp.zeros_like(dkg_ref)

        cv = c_ref[...]
        sv = s_ref[...]
        for hd in range(Q_HEADS + KV_HEADS):
            cols = slice(hd * HEAD_DIM, (hd + 1) * HEAD_DIM)
            x = p_ref[:, cols]
            if hd < Q_HEADS:
                gv, dyr, dg_ref = qg_ref[...], dq_ref[:, cols] * ATT_SCALE, dqg_ref
            else:
                kc = slice((hd - Q_HEADS) * HEAD_DIM, (hd - Q_HEADS + 1) * HEAD_DIM)
                gv, dyr, dg_ref = kg_ref[...], dk_ref[:, kc], dkg_ref
            r = lax.rsqrt(jnp.mean(x * x, axis=-1, keepdims=True) + EPS)
            xhat = x * r
            dy = dyr * cv + _swap32(dyr * sv)
            dg_ref[...] += jnp.sum(dy * xhat, axis=0, keepdims=True)
            dxhat = dy * gv
            o_ref[:, cols] = (r * (dxhat - xhat * jnp.mean(dxhat * xhat, axis=-1, keepdims=True))).astype(BF16)
        o_ref[:, ATT_WIDTH + KV_WIDTH:QKV_WIDTH] = dv_ref[...].astype(BF16)
        for a in range(n_aux):
            o_ref[:, QKV_WIDTH + a * AUX_WIDTH:QKV_WIDTH + (a + 1) * AUX_WIDTH] = da_ref[a]

    vec = pl.BlockSpec((1, HEAD_DIM), lambda i: (0, 0))
    tab = pl.BlockSpec((TM, HEAD_DIM), lambda i: (i, 0))
    return _pcall(
        body, name=name, grid=(t // TM,),
        in_specs=[pl.BlockSpec((TM, ATT_WIDTH), lambda i: (i, 0)), pl.BlockSpec((TM, KV_WIDTH), lambda i: (i, 0)),
                  pl.BlockSpec((TM, KV_WIDTH), lambda i: (i, 0)),
                  pl.BlockSpec((n_aux, TM, AUX_WIDTH), lambda i: (0, i, 0)),
                  pl.BlockSpec((TM, QKV_WIDTH), lambda i: (i, 0)), vec, vec, tab, tab],
        out_specs=[pl.BlockSpec((TM, n_in), lambda i: (i, 0)), vec, vec],
        out_shape=[jax.ShapeDtypeStruct((t, n_in), BF16), jax.ShapeDtypeStruct((1, HEAD_DIM), F32),
                   jax.ShapeDtypeStruct((1, HEAD_DIM), F32)],
        compiler_params=_params("arbitrary"),
    )(dq, dk, dv, daux, proj, qg, kg, cos, sin)


def _band_start(i, t):
    return pl.multiple_of(jnp.clip(i * TM - WINDOW, 0, t - BAND), WINDOW)


def _band_mask(i, start, ctx_len):
    shape = (TM, ctx_len + BAND)
    col = lax.broadcasted_iota(jnp.int32, shape, 1)
    qrow = i * TM + lax.broadcasted_iota(jnp.int32, shape, 0)
    krow = start + col - ctx_len
    band_ok = (krow >= ctx_len) & (jnp.abs(krow - qrow) <= WINDOW)
    return (col < ctx_len) | band_ok


def attention_fwd(q, k, v, sink, ctx_len, windowed, name):
    t = q.shape[0]
    d_model = ATT_WIDTH + AUX_WIDTH
    gw = GROUPS * HEAD_DIM

    def one_head(qh, kk, vv, mask, sink_val):
        s = _dot_nt(qh, kk)
        if mask is not None:
            s = jnp.where(mask, s, NEG_INF)
        m = jnp.max(s, axis=-1, keepdims=True)
        if sink_val is not None:
            m = jnp.maximum(m, sink_val)
        p = jnp.exp(s - m)
        l = jnp.sum(p, axis=-1, keepdims=True)
        if sink_val is not None:
            l = l + jnp.exp(sink_val - m)
        o = _dot(p.astype(BF16), vv) / l
        return o, m + jnp.log(l)

    def body(sink_ref, q_ref, k_ref, v_ref, o_ref, lse_ref):
        kv = pl.program_id(0)
        i = pl.program_id(1)

        def run(kk, vv, mask):
            for g in range(GROUPS):
                sink_val = sink_ref[kv * GROUPS + g] if windowed else None
                o, lse = one_head(q_ref[:, g * HEAD_DIM:(g + 1) * HEAD_DIM], kk, vv, mask, sink_val)
                o_ref[:, g * HEAD_DIM:(g + 1) * HEAD_DIM] = o.astype(BF16)
                lse_ref[g] = lse

        @pl.when(i == 0)
        def _():
            if windowed:
                o_ref[...] = jnp.zeros_like(o_ref)
                lse_ref[...] = jnp.zeros_like(lse_ref)
            else:
                run(k_ref[0:ctx_len], v_ref[0:ctx_len], None)

        @pl.when(i > 0)
        def _():
            if windowed:
                start = _band_start(i, t)
                kk = jnp.concatenate([k_ref[0:ctx_len], k_ref[pl.ds(start, BAND)]], axis=0)
                vv = jnp.concatenate([v_ref[0:ctx_len], v_ref[pl.ds(start, BAND)]], axis=0)
                run(kk, vv, _band_mask(i, start, ctx_len))
            else:
                run(k_ref[...], v_ref[...], None)

    kvspec = pl.BlockSpec((t, HEAD_DIM), lambda kv, i: (0, kv))
    return _pcall(
        body, name=name, grid=(KV_HEADS, t // TM),
        in_specs=[pl.BlockSpec(memory_space=pltpu.SMEM), pl.BlockSpec((TM, gw), lambda kv, i: (i, kv)), kvspec, kvspec],
        out_specs=[pl.BlockSpec((TM, gw), lambda kv, i: (i, kv)),
                   pl.BlockSpec((GROUPS, TM, 1), lambda kv, i: (kv, i, 0))],
        out_shape=[jax.ShapeDtypeStruct((t, d_model), BF16), jax.ShapeDtypeStruct((Q_HEADS, t, 1), F32)],
        compiler_params=_params("parallel", "parallel"),
    )(sink, q, k, v)


def attention_bwd(q, k, v, cat, lse, dcat, sink, ctx_len, windowed, name):
    t = q.shape[0]
    gw = GROUPS * HEAD_DIM

    def body(sink_ref, q_ref, k_ref, v_ref, o_ref, lse_ref, do_ref, dq_ref, dk_ref, dv_ref, dsink_ref):
        kv = pl.program_id(0)
        i = pl.program_id(1)

        @pl.when(i == 0)
        def _():
            dk_ref[...] = jnp.zeros_like(dk_ref)
            dv_ref[...] = jnp.zeros_like(dv_ref)
            dsink_ref[...] = jnp.zeros_like(dsink_ref)

        def run(kk, vv, mask, accumulate):
            for g in range(GROUPS):
                cols = slice(g * HEAD_DIM, (g + 1) * HEAD_DIM)
                qh = q_ref[:, cols]
                doh = do_ref[:, cols]
                delta = jnp.sum(doh * o_ref[:, cols].astype(F32), axis=-1, keepdims=True)
                lse_g = lse_ref[g]
                s = _dot_nt(qh, kk)
                if mask is not None:
                    s = jnp.where(mask, s, NEG_INF)
                p = jnp.exp(s - lse_g)
                dob = doh.astype(BF16)
                dp = _dot_nt(dob, vv)
                ds = (p * (dp - delta)).astype(BF16)
                dq_ref[:, cols] = _dot(ds, kk)
                accumulate(_dot_tn(ds, qh), _dot_tn(p.astype(BF16), dob))
                if windowed:
                    p_sink = jnp.exp(sink_ref[kv * GROUPS + g] - lse_g)
                    dsink_ref[g:g + 1, :] += jnp.sum(-p_sink * delta, axis=0, keepdims=True)

        @pl.when(i == 0)
        def _():
            if windowed:
                dq_ref[...] = jnp.zeros_like(dq_ref)
            else:
                def acc_ctx(dkp, dvp):
                    dk_ref[0:ctx_len] += dkp
                    dv_ref[0:ctx_len] += dvp

                run(k_ref[0:ctx_len], v_ref[0:ctx_len], None, acc_ctx)

        @pl.when(i > 0)
        def _():
            if windowed:
                start = _band_start(i, t)
                kk = jnp.concatenate([k_ref[0:ctx_len], k_ref[pl.ds(start, BAND)]], axis=0)
                vv = jnp.concatenate([v_ref[0:ctx_len], v_ref[pl.ds(start, BAND)]], axis=0)

                def acc_band(dkp, dvp):
                    dk_ref[0:ctx_len] += dkp[0:ctx_len]
                    dv_ref[0:ctx_len] += dvp[0:ctx_len]
                    dk_ref[pl.ds(start, BAND)] += dkp[ctx_len:ctx_len + BAND]
                    dv_ref[pl.ds(start, BAND)] += dvp[ctx_len:ctx_len + BAND]

                run(kk, vv, _band_mask(i, start, ctx_len), acc_band)
            else:
                def acc_all(dkp, dvp):
                    dk_ref[...] += dkp
                    dv_ref[...] += dvp

                run(k_ref[...], v_ref[...], None, acc_all)

    kvspec = pl.BlockSpec((t, HEAD_DIM), lambda kv, i: (0, kv))
    grp = pl.BlockSpec((TM, gw), lambda kv, i: (i, kv))
    return _pcall(
        body, name=name, grid=(KV_HEADS, t // TM),
        in_specs=[pl.BlockSpec(memory_space=pltpu.SMEM), grp, kvspec, kvspec, grp,
                  pl.BlockSpec((GROUPS, TM, 1), lambda kv, i: (kv, i, 0)), grp],
        out_specs=[grp, kvspec, kvspec, pl.BlockSpec((None, 8, 128), lambda kv, i: (kv, 0, 0))],
        out_shape=[jax.ShapeDtypeStruct((t, ATT_WIDTH), F32), jax.ShapeDtypeStruct((t, KV_WIDTH), F32),
                   jax.ShapeDtypeStruct((t, KV_WIDTH), F32), jax.ShapeDtypeStruct((KV_HEADS, 8, 128), F32)],
        compiler_params=_params("parallel", "arbitrary"),
    )(sink, q, k, v, cat, lse, dcat)


def _segment_pos(t, ctx_len, width):
    row = lax.broadcasted_iota(jnp.int32, (t, width), 0)
    in_ctx = row < ctx_len
    return jnp.where(in_ctx, row, row - ctx_len), jnp.where(in_ctx, ctx_len, t - ctx_len)


def _shifted(x, offset, pos, seg_len):
    t = x.shape[0]
    moved = x if offset == 0 else pltpu.roll(x, (-offset) % t, 0)
    ok = (pos + offset >= 0) & (pos + offset < seg_len)
    return jnp.where(ok, moved, 0.0)


def conv_fwd(proj, conv_w, cat, ctx_len, name):
    t = proj.shape[0]
    base = QKV_WIDTH // 128
    per = AUX_WIDTH // 128

    def body(gb_ref, gc_ref, u_ref, w_ref, cat_in, o_ref):
        del cat_in
        pos, seg = _segment_pos(t, ctx_len, 128)
        z = gc_ref[...] * u_ref[...]
        conv = (w_ref[0:1, :] * _shifted(z, -1, pos, seg) + w_ref[1:2, :] * z
                + w_ref[2:3, :] * _shifted(z, 1, pos, seg))
        o_ref[...] = (gb_ref[...] * conv).astype(BF16)

    def col(off):
        return pl.BlockSpec((t, 128), lambda j: (0, base + off * per + j))

    return _pcall(
        body, name=name, grid=(per,),
        in_specs=[col(0), col(1), col(2), pl.BlockSpec((3, 128), lambda j: (0, j)),
                  pl.BlockSpec(memory_space=pl.ANY)],
        out_specs=pl.BlockSpec((t, 128), lambda j: (0, ATT_WIDTH // 128 + j)),
        out_shape=jax.ShapeDtypeStruct(cat.shape, BF16),
        input_output_aliases={4: 0},
        compiler_params=_params("parallel"),
    )(proj, proj, proj, conv_w, cat)


def conv_bwd(dcat, proj, conv_w, ctx_len, name):
    t = proj.shape[0]
    base = QKV_WIDTH // 128
    per = AUX_WIDTH // 128

    def body(do_ref, gb_ref, gc_ref, u_ref, w_ref, da_ref, dw_ref):
        pos, seg = _segment_pos(t, ctx_len, 128)
        gc = gc_ref[...]
        u = u_ref[...]
        z = gc * u
        zm = _shifted(z, -1, pos, seg)
        zp = _shifted(z, 1, pos, seg)
        conv = w_ref[0:1, :] * zm + w_ref[1:2, :] * z + w_ref[2:3, :] * zp
        dout = do_ref[...]
        da_ref[0] = (dout * conv).astype(BF16)
        dconv = dout * gb_ref[...]
        dw_ref[0:1, :] = jnp.sum(dconv * zm, axis=0, keepdims=True)
        dw_ref[1:2, :] = jnp.sum(dconv * z, axis=0, keepdims=True)
        dw_ref[2:3, :] = jnp.sum(dconv * zp, axis=0, keepdims=True)
        dz = (w_ref[1:2, :] * dconv + w_ref[0:1, :] * _shifted(dconv, 1, pos, seg)
              + w_ref[2:3, :] * _shifted(dconv, -1, pos, seg))
        da_ref[1] = (dz * u).astype(BF16)
        da_ref[2] = (dz * gc).astype(BF16)

    def col(off):
        return pl.BlockSpec((t, 128), lambda j: (0, base + off * per + j))

    return _pcall(
        body, name=name, grid=(per,),
        in_specs=[pl.BlockSpec((t, 128), lambda j: (0, ATT_WIDTH // 128 + j)), col(0), col(1), col(2),
                  pl.BlockSpec((3, 128), lambda j: (0, j))],
        out_specs=[pl.BlockSpec((3, t, 128), lambda j: (0, 0, j)), pl.BlockSpec((3, 128), lambda j: (0, j))],
        out_shape=[jax.ShapeDtypeStruct((3, t, AUX_WIDTH), BF16), jax.ShapeDtypeStruct((3, AUX_WIDTH), F32)],
        compiler_params=_params("parallel"),
    )(dcat, proj, proj, proj, conv_w)


def _pooled(u, w, pos, seg):
    lo = jnp.clip(pos - w // 2, 0, seg)
    hi = jnp.clip(pos - w // 2 + w, 0, seg)
    inv = 1.0 / (hi - lo).astype(F32)
    total = _shifted(u, -(w // 2), pos, seg)
    for o in range(-(w // 2) + 1, w // 2):
        total = total + _shifted(u, o, pos, seg)
    return total * inv - u, inv


def pool_fwd(proj, pool_w, pool_scale, cat, ctx_len, name):
    t = proj.shape[0]

    def body(u_ref, w_ref, sc_ref, cat_in, o_ref):
        del cat_in
        pos, seg = _segment_pos(t, ctx_len, 128)
        for g, w in enumerate(POOL_WINDOWS):
            cols = slice(g * 128, (g + 1) * 128)
            pooled, _ = _pooled(u_ref[:, cols], w, pos, seg)
            mixed = _dot(pooled.astype(BF16), w_ref[g].astype(BF16))
            o_ref[:, cols] = (mixed * sc_ref[:, cols]).astype(BF16)

    return _pcall(
        body, name=name, grid=(1,),
        in_specs=[pl.BlockSpec((t, AUX_WIDTH), lambda j: (0, QKV_WIDTH // AUX_WIDTH)),
                  pl.BlockSpec((AUX_GROUPS, 128, 128), lambda j: (0, 0, 0)),
                  pl.BlockSpec((1, AUX_WIDTH), lambda j: (0, 0)), pl.BlockSpec(memory_space=pl.ANY)],
        out_specs=pl.BlockSpec((t, AUX_WIDTH), lambda j: (0, ATT_WIDTH // AUX_WIDTH)),
        out_shape=jax.ShapeDtypeStruct(cat.shape, BF16),
        input_output_aliases={3: 0},
        compiler_params=_params("arbitrary"),
    )(proj, pool_w, pool_scale, cat)


def pool_bwd(dcat, proj, pool_w, pool_scale, ctx_len, name):
    t = proj.shape[0]

    def body(do_ref, u_ref, w_ref, sc_ref, da_ref, dw_ref, dsc_ref):
        pos, seg = _segment_pos(t, ctx_len, 128)
        for g, w in enumerate(POOL_WINDOWS):
            cols = slice(g * 128, (g + 1) * 128)
            pooled, inv = _pooled(u_ref[:, cols], w, pos, seg)
            pb = pooled.astype(BF16)
            wb = w_ref[g].astype(BF16)
            mixed = _dot(pb, wb)
            dout = do_ref[:, cols]
            dsc_ref[:, cols] = jnp.sum(dout * mixed, axis=0, keepdims=True)
            dmixed = (dout * sc_ref[:, cols]).astype(BF16)
            dw_ref[g] = _dot_tn(pb, dmixed)
            dpooled = _dot_nt(dmixed, wb)
            spread = dpooled * inv
            du = _shifted(spread, w // 2, pos, seg) - dpooled
            for o in range(-(w // 2) + 1, w // 2):
                du = du + _shifted(spread, -o, pos, seg)
            da_ref[0, :, cols] = du.astype(BF16)

    return _pcall(
        body, name=name, grid=(1,),
        in_specs=[pl.BlockSpec((t, AUX_WIDTH), lambda j: (0, ATT_WIDTH // AUX_WIDTH)),
                  pl.BlockSpec((t, AUX_WIDTH), lambda j: (0, QKV_WIDTH // AUX_WIDTH)),
                  pl.BlockSpec((AUX_GROUPS, 128, 128), lambda j: (0, 0, 0)),
                  pl.BlockSpec((1, AUX_WIDTH), lambda j: (0, 0))],
        out_specs=[pl.BlockSpec((1, t, AUX_WIDTH), lambda j: (0, 0, 0)),
                   pl.BlockSpec((AUX_GROUPS, 128, 128), lambda j: (0, 0, 0)),
                   pl.BlockSpec((1, AUX_WIDTH), lambda j: (0, 0))],
        out_shape=[jax.ShapeDtypeStruct((1, t, AUX_WIDTH), BF16), jax.ShapeDtypeStruct((AUX_GROUPS, 128, 128), F32),
                   jax.ShapeDtypeStruct((1, AUX_WIDTH), F32)],
        compiler_params=_params("arbitrary"),
    )(dcat, proj, pool_w, pool_scale)


MOD_NAMES = ("shift1", "scale1", "gate1", "shift2", "scale2", "gate2")


def _layer_fwd(h, lw, weight, mods, tabs, ctx_len, windowed, tag, after_mixer=None):
    cos, sin = tabs
    xn = norm_mod_fwd(h, lw["norm1_g"], mods["shift1"], mods["scale1"], tag + "norm1_fwd")
    lw["w_in"] = weight("w_in", xn)
    proj = mm_nn(xn, lw["w_in"], tag + "w_in_fwd")
    q, k, v = qk_prep_fwd(proj, lw["q_g"], lw["k_g"], cos, sin, tag + "qk_fwd")
    cat, lse = attention_fwd(q, k, v, lw["sink"], ctx_len, windowed, tag + "attn_fwd")
    if windowed:
        cat = pool_fwd(proj, lw["pool_w"], lw["pool_scale"], cat, ctx_len, tag + "pool_fwd")
    else:
        cat = conv_fwd(proj, lw["conv_w"], cat, ctx_len, tag + "conv_fwd")
    lw["w_out"] = weight("w_out", cat)
    gate1 = mods["gate1"] if after_mixer is None else mods["gate1"] + after_mixer(cat)[0, 0]
    y, h1 = mm_nn_residual(cat, lw["w_out"], h, gate1, tag + "w_out_fwd")
    hn = norm_mod_fwd(h1, lw["norm2_g"], mods["shift2"], mods["scale2"], tag + "norm2_fwd")
    lw["w_gate"], lw["w_up"] = weight("w_gate", hn), weight("w_up", hn)
    g, u, act = mm_swiglu(hn, lw["w_gate"], lw["w_up"], tag + "ffn_in_fwd")
    lw["w_down"] = weight("w_down", act)
    f, h2 = mm_nn_residual(act, lw["w_down"], h1, mods["gate2"], tag + "w_down_fwd")
    saved = dict(h=h, xn=xn, proj=proj, q=q, k=k, v=v, cat=cat, lse=lse, y=y, h1=h1, hn=hn, g=g, u=u, act=act, f=f)
    return h2, saved


def _layer_bwd_ffn(dh, sv, lw, mods, tag):
    big, small = {}, {}
    df, dgate2 = gate_bwd(dh, sv["f"], mods["gate2"], tag + "gate2_bwd")
    dgp, du = mm_nt_swiglu_bwd(df, lw["w_down"], sv["g"], sv["u"], tag + "w_down_dgrad")
    big["w_down"] = mm_tn(sv["act"], df, tag + "w_down_wgrad", col_sharded=False)
    dhn = mm_nt_cols(dgp, lw["w_gate"], tag + "w_gate_dgrad")
    dhn = mm_nt_cols(du, lw["w_up"], tag + "w_up_dgrad", add=dhn)
    big["w_gate"] = mm_tn(sv["hn"], dgp, tag + "w_gate_wgrad", col_sharded=True)
    big["w_up"] = mm_tn(sv["hn"], du, tag + "w_up_wgrad", col_sharded=True)
    dh1, dshift2, dscale2, small["norm2_g"] = norm_mod_bwd(dhn, sv["h1"], lw["norm2_g"], mods["scale2"], dh,
                                                           tag + "norm2_bwd")
    return dh1, big, small, (dshift2, dscale2, dgate2)


def _layer_bwd_mixer(dh1, sv, lw, mods, tabs, ctx_len, windowed, tag, between=None):
    cos, sin = tabs
    big, small = {}, {}
    dy, dgate1 = gate_bwd(dh1, sv["y"], mods["gate1"], tag + "gate1_bwd")
    dcat = mm_nt_rows(dy, lw["w_out"], tag + "w_out_dgrad")
    big["w_out"] = mm_tn(sv["cat"], dy, tag + "w_out_wgrad", col_sharded=False)
    if windowed:
        daux, small["pool_w"], small["pool_scale"] = pool_bwd(dcat, sv["proj"], lw["pool_w"], lw["pool_scale"],
                                                              ctx_len, tag + "pool_bwd")
    else:
        daux, small["conv_w"] = conv_bwd(dcat, sv["proj"], lw["conv_w"], ctx_len, tag + "conv_bwd")
    dq, dk, dv, dsink = attention_bwd(sv["q"], sv["k"], sv["v"], sv["cat"], sv["lse"], dcat, lw["sink"], ctx_len,
                                      windowed, tag + "attn_bwd")
    if windowed:
        small["sink"] = dsink[:, :GROUPS, 0].reshape(Q_HEADS)
    q_g = lw["q_g"] if between is None else lw["q_g"] + between(dq)[0, 0]
    dproj, small["q_g"], small["k_g"] = qk_prep_bwd(dq, dk, dv, daux, sv["proj"], q_g, lw["k_g"], cos, sin,
                                                    tag + "qk_bwd")
    dxn = mm_nt_cols(dproj, lw["w_in"], tag + "w_in_dgrad")
    big["w_in"] = mm_tn(sv["xn"], dproj, tag + "w_in_wgrad", col_sharded=True)
    dh0, dshift1, dscale1, small["norm1_g"] = norm_mod_bwd(dxn, sv["h"], lw["norm1_g"], mods["scale1"], dh1,
                                                           tag + "norm1_bwd")
    return dh0, big, small, (dshift1, dscale1, dgate1)


def _dmod_rows(mixer_part, ffn_part):
    return jnp.concatenate([m[:, 0, :] for m in (*mixer_part, *ffn_part)], axis=1)


def _tie(mods, name, token):
    return {**mods, name: mods[name] + token[0, 0]}


def _place():
    x, y, c = lax.axis_index("x"), lax.axis_index("y"), lax.axis_index("c")
    chips = [(1 - x, y), (x, 1 - y), (1 - x, 1 - y)]
    return x, y, c, chips


def _remote(src, dst, send_sem, recv_sem, to):
    return pltpu.make_async_remote_copy(src_ref=src, dst_ref=dst, send_sem=send_sem, recv_sem=recv_sem,
                                        device_id=to, device_id_type=MESH)


ANY = pl.BlockSpec(memory_space=pl.ANY)


def all_gather8(x_shard, name):
    m_per, n = x_shard.shape

    def body(x_ref, out_ref, send_sems, recv_sems, local_sem):
        x, y, c, chips = _place()
        me, sibling = (x, y, c), (x, y, 1 - c)

        def rows(px, py, pc):
            return out_ref.at[pl.ds((4 * px + 2 * py + pc) * m_per, m_per), :]

        def copy(k, block, to, src=None):
            return _remote(rows(*block) if src is None else src, rows(*block), send_sems.at[k], recv_sems.at[k], to)

        mine = pltpu.make_async_copy(x_ref, rows(*me), local_sem)
        mine.start()
        first = [copy(0, me, sibling, src=x_ref)]
        first += [copy(1 + j, me, (*chip, c), src=x_ref) for j, chip in enumerate(chips)]
        for cp in first:
            cp.start()
        passed = [copy(4 + j, (*chip, c), sibling) for j, chip in enumerate(chips)]
        for j, chip in enumerate(chips):
            copy(1 + j, (*chip, c), me).wait_recv()
            passed[j].start()
        copy(0, sibling, me).wait_recv()
        for j, chip in enumerate(chips):
            copy(4 + j, (*chip, 1 - c), me).wait_recv()
        for cp in first + passed:
            cp.wait_send()
        mine.wait()

    return _pcall(
        body, name=name,
        out_shape=jax.ShapeDtypeStruct((N_DEV * m_per, n), x_shard.dtype),
        in_specs=[pl.BlockSpec(memory_space=pltpu.VMEM)],
        out_specs=pl.BlockSpec(memory_space=pltpu.VMEM),
        scratch_shapes=[pltpu.SemaphoreType.DMA((7,)), pltpu.SemaphoreType.DMA((7,)), pltpu.SemaphoreType.DMA],
        compiler_params=pltpu.CompilerParams(vmem_limit_bytes=VMEM_LIMIT),
    )(x_shard)


def _row_tile(rows, largest=256):
    for tr in (1024, 704, 512, 256, 128, 64, 32, 16):
        if tr <= largest and rows % tr == 0:
            return tr
    return rows


def place_shard(w, p, name):
    r, cols = w.shape
    tr = _row_tile(r, 1024)

    def body(p_ref, w_ref, o_ref):
        del p_ref
        o_ref[...] = w_ref[...].astype(BF16)

    return _pcall(
        body, name=name,
        grid_spec=pltpu.PrefetchScalarGridSpec(
            num_scalar_prefetch=1, grid=(r // tr,),
            in_specs=[pl.BlockSpec((tr, cols), lambda i, p_ref: (i, 0))],
            out_specs=pl.BlockSpec((None, tr, cols), lambda i, p_ref: (p_ref[0], i, 0))),
        out_shape=jax.ShapeDtypeStruct((N_SHARD, r, cols), BF16),
        compiler_params=_params("parallel"),
    )(p.reshape(1).astype(jnp.int32), w)


def gather_weight_shards(stacks, after):
    n = len(stacks)

    def body(*refs):
        outs = refs[n + 1:2 * n + 1]
        send_sems, recv_sems = refs[2 * n + 1:]
        x, y, c, chips = _place()
        p = 2 * x + y
        sibling = (x, y, 1 - c)
        started = []
        for i in range(n):
            hr = outs[i].shape[1] // 2
            mine = outs[i].at[p, pl.ds(c * hr, hr)]
            for j, chip in enumerate(chips):
                cp = _remote(mine, mine, send_sems.at[i, j], recv_sems.at[i, j], (*chip, c))
                cp.start()
                started.append(cp)
        for i in range(n):
            hr = outs[i].shape[1] // 2
            for j, (px, py) in enumerate(chips):
                half = outs[i].at[2 * px + py, pl.ds(c * hr, hr)]
                _remote(half, half, send_sems.at[i, j], recv_sems.at[i, j], (px, py, c)).wait_recv()
                fwd = _remote(half, half, send_sems.at[i, 3 + j], recv_sems.at[i, 3 + j], sibling)
                fwd.start()
                started.append(fwd)
        for i in range(n):
            hr = outs[i].shape[1] // 2
            for j, (px, py) in enumerate(chips):
                other = outs[i].at[2 * px + py, pl.ds((1 - c) * hr, hr)]
                _remote(other, other, send_sems.at[i, 3 + j], recv_sems.at[i, 3 + j], sibling).wait_recv()
        for cp in started:
            cp.wait_send()

    return _pcall(
        body, name="gather_weight_shards",
        out_shape=[jax.ShapeDtypeStruct(s.shape, s.dtype) for s in stacks],
        in_specs=[ANY] * (n + 1), out_specs=[ANY] * n,
        input_output_aliases={i: i for i in range(n)},
        scratch_shapes=[pltpu.SemaphoreType.DMA((n, 6)), pltpu.SemaphoreType.DMA((n, 6))],
    )(*stacks, after)


HBM_SPEC = pl.BlockSpec(memory_space=pltpu.HBM)
SEM_SPEC = pl.BlockSpec(memory_space=pltpu.SEMAPHORE)
EFFECT = pltpu.SideEffectType.DATAFLOW_SIDE_EFFECTING


def split_start(name, copies, n_sems, arrays, after):
    m = len(arrays)

    def body(*refs):
        for cp in copies(refs[:m], refs[m + 1], refs[m + 2]):
            cp.start()
        refs[-1][...] = jnp.zeros_like(refs[-1])

    res = _pcall(
        body, name=name,
        out_shape=(pltpu.SemaphoreType.DMA((n_sems,)), pltpu.SemaphoreType.DMA((n_sems,)),
                   *[pltpu.HBM(s.shape, s.dtype) for s in arrays], jax.ShapeDtypeStruct((8, 128), F32)),
        in_specs=[HBM_SPEC] * m + [ANY],
        out_specs=(SEM_SPEC, SEM_SPEC, *[HBM_SPEC] * m, pl.BlockSpec(memory_space=pltpu.VMEM)),
        input_output_aliases={i: 2 + i for i in range(m)},
        compiler_params=pltpu.CompilerParams(has_side_effects=EFFECT),
    )(*[pltpu.with_memory_space_constraint(s, pltpu.HBM) for s in arrays], after)
    return res[0], res[1], list(res[2:2 + m]), res[2 + m]


def split_wait(name, copies, send_sems, recv_sems, arrays, after):
    m = len(arrays)

    def body(*refs):
        for cp in copies(refs[:m], refs[m], refs[m + 1]):
            cp.wait_send()
            cp.wait_recv()

    return _pcall(
        body, name=name,
        out_shape=[pltpu.HBM(s.shape, s.dtype) for s in arrays],
        in_specs=[HBM_SPEC] * m + [SEM_SPEC, SEM_SPEC, ANY],
        out_specs=[HBM_SPEC] * m,
        input_output_aliases={i: i for i in range(m)},
        compiler_params=pltpu.CompilerParams(has_side_effects=EFFECT),
    )(*arrays, send_sems, recv_sems, after)


def gather_copies(n):
    def copies(stacks, send_sems, recv_sems):
        x, y, c, chips = _place()
        p = 2 * x + y
        return [_remote(stacks[i].at[p], stacks[i].at[p], send_sems.at[3 * i + j], recv_sems.at[3 * i + j], (*chip, c))
                for i in range(n) for j, chip in enumerate(chips)]
    return copies


def gather_half_copies(n):
    def copies(stacks, send_sems, recv_sems):
        x, y, c, chips = _place()
        p = 2 * x + y
        out = []
        for i in range(n):
            hr = stacks[i].shape[1] // 2
            mine = stacks[i].at[p, pl.ds(c * hr, hr)]
            out += [_remote(mine, mine, send_sems.at[3 * i + j], recv_sems.at[3 * i + j], (*chip, c))
                    for j, chip in enumerate(chips)]
        return out
    return copies


def forward_half_copies(n):
    def copies(stacks, send_sems, recv_sems):
        x, y, c, chips = _place()
        out = []
        for i in range(n):
            hr = stacks[i].shape[1] // 2
            for j, (px, py) in enumerate(chips):
                half = stacks[i].at[2 * px + py, pl.ds(c * hr, hr)]
                out.append(_remote(half, half, send_sems.at[3 * i + j], recv_sems.at[3 * i + j], (x, y, 1 - c)))
        return out
    return copies


def sibling_half_copies(n):
    def copies(refs, send_sems, recv_sems):
        x, y, c, _ = _place()
        out = []
        for i in range(n):
            hr = refs[n + i].shape[1]
            out.append(_remote(refs[i].at[:, pl.ds((1 - c) * hr, hr), :], refs[n + i], send_sems.at[i], recv_sems.at[i],
                               (x, y, 1 - c)))
        return out
    return copies


def chip_quarter_copies(n):
    def copies(refs, send_sems, recv_sems):
        _, _, c, chips = _place()
        return [_remote(refs[i].at[2 * px + py], refs[n + i].at[j], send_sems.at[3 * i + j], recv_sems.at[3 * i + j],
                        (px, py, c)) for i in range(n) for j, (px, py) in enumerate(chips)]
    return copies


def exchange_final_halves(grads, name):
    n = len(grads)

    def body(*refs):
        outs = refs[n:2 * n]
        send_sems, recv_sems = refs[2 * n:]
        x, y, c, _ = _place()
        sends = []
        for i in range(n):
            hr = outs[i].shape[0] // 2
            mine = outs[i].at[pl.ds(c * hr, hr)]
            cp = _remote(mine, mine, send_sems.at[i], recv_sems.at[i], (x, y, 1 - c))
            cp.start()
            sends.append(cp)
        for i in range(n):
            hr = outs[i].shape[0] // 2
            other = outs[i].at[pl.ds((1 - c) * hr, hr)]
            _remote(other, other, send_sems.at[i], recv_sems.at[i], (x, y, 1 - c)).wait_recv()
        for cp in sends:
            cp.wait_send()

    return _pcall(
        body, name=name,
        out_shape=[jax.ShapeDtypeStruct(g.shape, g.dtype) for g in grads],
        in_specs=[ANY] * n, out_specs=[ANY] * n,
        input_output_aliases={i: i for i in range(n)},
        scratch_shapes=[pltpu.SemaphoreType.DMA((n,)), pltpu.SemaphoreType.DMA((n,))],
    )(*grads)


def add_own_half(g, recv, c, name):
    s, r, cols = g.shape
    hr = r // 2
    tr = _row_tile(hr, 1024)
    nb = hr // tr

    def body(c_ref, g_ref, r_ref, o_ref):
        del c_ref
        o_ref[...] = (g_ref[...].astype(F32) + r_ref[...].astype(F32)).astype(BF16)

    return _pcall(
        body, name=name,
        grid_spec=pltpu.PrefetchScalarGridSpec(
            num_scalar_prefetch=1, grid=(s, nb),
            in_specs=[pl.BlockSpec((None, tr, cols), lambda j, i, c_ref: (j, c_ref[0] * nb + i, 0)),
                      pl.BlockSpec((None, tr, cols), lambda j, i, c_ref: (j, i, 0))],
            out_specs=pl.BlockSpec((None, tr, cols), lambda j, i, c_ref: (j, i, 0))),
        out_shape=jax.ShapeDtypeStruct((s, hr, cols), BF16),
        compiler_params=_params("parallel", "parallel"),
    )(c.reshape(1).astype(jnp.int32), g, recv)


def add_quarters(own, others, p, c, name, deps=()):
    _, hr, cols = own.shape
    tr = _row_tile(hr, 1024)
    nb = hr // tr

    def body(pc_ref, own_ref, oth_ref, *rest):
        o_ref = rest[-1]
        acc = own_ref[...].astype(F32) + oth_ref[0].astype(F32)
        acc = acc + oth_ref[1].astype(F32)
        o_ref[...] = acc + oth_ref[2].astype(F32)

    return _pcall(
        body, name=name,
        grid_spec=pltpu.PrefetchScalarGridSpec(
            num_scalar_prefetch=1, grid=(nb,),
            in_specs=[pl.BlockSpec((None, tr, cols), lambda i, pc: (pc[0], i, 0)),
                      pl.BlockSpec((3, tr, cols), lambda i, pc: (0, i, 0))] + [ANY] * len(deps),
            out_specs=pl.BlockSpec((tr, cols), lambda i, pc: (pc[1] * nb + i, 0))),
        out_shape=jax.ShapeDtypeStruct((2 * hr, cols), F32),
        compiler_params=_params("parallel"),
    )(jnp.stack([p, c]).astype(jnp.int32), own, others, *deps)


def _adamw_math(w, g, m, v):
    m2 = ADAM_B1 * m + (1.0 - ADAM_B1) * g
    v2 = ADAM_B2 * v + (1.0 - ADAM_B2) * (g * g)
    m_hat = m2 / (1.0 - ADAM_B1 ** ADAM_STEP)
    v_hat = v2 / (1.0 - ADAM_B2 ** ADAM_STEP)
    delta = -ADAM_LR * (m_hat / (jnp.sqrt(v_hat) + ADAM_EPS) + ADAM_WD * w)
    return delta, m2, v2


def adamw(w, g, m, v, name):
    r, cols = w.shape
    tr = _row_tile(r)

    def body(w_ref, g_ref, m_ref, v_ref, go_ref, d_ref, m2_ref, v2_ref):
        g = g_ref[...]
        go_ref[...] = g
        d_ref[...], m2_ref[...], v2_ref[...] = _adamw_math(w_ref[...], g, m_ref[...], v_ref[...])

    blk = pl.BlockSpec((tr, cols), lambda i: (i, 0))
    return _pcall(
        body, name=name, grid=(r // tr,),
        in_specs=[blk] * 4, out_specs=[blk] * 4,
        out_shape=[jax.ShapeDtypeStruct((r, cols), F32)] * 4,
        compiler_params=_params("parallel"),
    )(w, g, m, v)


def small_update(gathered, w, m, v):
    nd, r, lanes = gathered.shape

    def body(ga_ref, w_ref, m_ref, v_ref, g_ref, d_ref, m2_ref, v2_ref):
        g = ga_ref[0] + ga_ref[1]
        for dev in range(2, nd):
            g = g + ga_ref[dev]
        g_ref[...] = g
        d_ref[...], m2_ref[...], v2_ref[...] = _adamw_math(w_ref[...], g, m_ref[...], v_ref[...])

    return _pcall(
        body, name="small_update",
        out_shape=[jax.ShapeDtypeStruct((r, lanes), F32)] * 4,
        compiler_params=pltpu.CompilerParams(vmem_limit_bytes=VMEM_LIMIT),
    )(gathered, w, m, v)


def mod_fwd(c16, w_mod, b_mod, name):
    d, ns = w_mod.shape
    tn = 512

    def body(c_ref, w_ref, b_ref, o_ref):
        cv = c_ref[...]
        sc = (cv * _sigmoid(cv)).astype(BF16)
        o_ref[...] = _dot(sc, w_ref[...].astype(BF16)) + b_ref[...]

    return _pcall(
        body, name=name, grid=(ns // tn,),
        in_specs=[pl.BlockSpec((16, d), lambda j: (0, 0)), pl.BlockSpec((d, tn), lambda j: (0, j)),
                  pl.BlockSpec((1, tn), lambda j: (0, j))],
        out_specs=pl.BlockSpec((16, tn), lambda j: (0, j)),
        out_shape=jax.ShapeDtypeStruct((16, ns), F32),
        compiler_params=_params("parallel"),
    )(c16, w_mod, b_mod)


def wmod_update(c16, dmod16, w, m, v, name):
    d, ns = w.shape
    tn = 256

    def body(c_ref, dm_ref, w_ref, m_ref, v_ref, g_ref, d_ref, m2_ref, v2_ref):
        cv = c_ref[...]
        sc = (cv * _sigmoid(cv)).astype(BF16)
        g = _dot_tn(sc, dm_ref[...].astype(BF16))
        g_ref[...] = g
        d_ref[...], m2_ref[...], v2_ref[...] = _adamw_math(w_ref[...], g, m_ref[...], v_ref[...])

    blk = pl.BlockSpec((d, tn), lambda j: (0, j))
    return _pcall(
        body, name=name, grid=(ns // tn,),
        in_specs=[pl.BlockSpec((16, d), lambda j: (0, 0)), pl.BlockSpec((16, tn), lambda j: (0, j)), blk, blk, blk],
        out_specs=[blk] * 4,
        out_shape=[jax.ShapeDtypeStruct((d, ns), F32)] * 4,
        compiler_params=_params("parallel"),
    )(c16, dmod16, w, m, v)


def cctx_partial(dm0, w0, dm1, w1):
    d, ns = w0.shape
    tn = 512

    def body(dm0_ref, w0_ref, dm1_ref, w1_ref, o_ref):
        @pl.when(pl.program_id(0) == 0)
        def _():
            o_ref[...] = jnp.zeros_like(o_ref)

        for dm_ref, w_ref in ((dm0_ref, w0_ref), (dm1_ref, w1_ref)):
            tot = jnp.sum(dm_ref[...], axis=0, keepdims=True)
            lhs = jnp.broadcast_to(tot, (8, tn)).astype(BF16)
            o_ref[...] += _dot_nt(lhs, w_ref[...].astype(BF16))

    dspec = pl.BlockSpec((8, tn), lambda j: (0, j))
    wspec = pl.BlockSpec((d, tn), lambda j: (0, j))
    return _pcall(
        body, name="cctx_partial", grid=(ns // tn,),
        in_specs=[dspec, wspec, dspec, wspec],
        out_specs=pl.BlockSpec((8, d), lambda j: (0, 0)),
        out_shape=jax.ShapeDtypeStruct((8, d), F32),
        compiler_params=_params("arbitrary"),
    )(dm0, w0, dm1, w1)


def cctx_update(parts, c_ctx, m, v):
    d = c_ctx.shape[1]

    def body(p_ref, c_ref, m_ref, v_ref, g_ref, d_ref, m2_ref, v2_ref):
        tot = p_ref[0:1, :] + p_ref[1:2, :]
        tot = tot + p_ref[2:3, :]
        tot = tot + p_ref[3:4, :]
        cv = c_ref[...]
        sg = _sigmoid(cv)
        g = tot * (sg * (1.0 + cv * (1.0 - sg)))
        g_ref[...] = g
        d_ref[...], m2_ref[...], v2_ref[...] = _adamw_math(cv, g, m_ref[...], v_ref[...])

    return _pcall(
        body, name="cctx_update",
        out_shape=[jax.ShapeDtypeStruct((1, d), F32)] * 4,
    )(parts, c_ctx, m, v)


WEIGHT_NAMES = (
    "c_ctx", "l0_norm1_g", "l0_w_mod", "l0_b_mod", "l0_w_in", "l0_q_norm_g", "l0_k_norm_g", "l0_conv_w", "l0_w_out",
    "l0_norm2_g", "l0_w_gate", "l0_w_up", "l0_w_down", "l1_norm1_g", "l1_w_mod", "l1_b_mod", "l1_w_in",
    "l1_q_norm_g", "l1_k_norm_g", "l1_sink", "l1_pool_w", "l1_pool_scale", "l1_w_out", "l1_norm2_g", "l1_w_gate",
    "l1_w_up", "l1_w_down", "final_norm_g")
BIG_NAMES = ("w_in", "w_out", "w_gate", "w_up", "w_down")
SMALL_SLOTS = tuple(
    [(f"l{l}_{nm}", par and f"l{l}_{nm}") for l in (0, 1)
     for nm, par in (("dmod_lat", False), ("dmod_ctx", False), ("b_mod", True), ("norm1_g", True), ("norm2_g", True),
                     ("q_norm_g", True), ("k_norm_g", True))]
    + [("l0_conv_w", None), ("l1_sink", "l1_sink"), ("l1_pool_w", "l1_pool_w"), ("l1_pool_scale", "l1_pool_scale"),
       ("final_norm_g", "final_norm_g")])


def _pack(values, sizes):
    parts = []
    for (name, _), size in zip(SMALL_SLOTS, sizes):
        v = values.get(name)
        padded = -(-size // 128) * 128
        v = jnp.zeros((padded,), F32) if v is None else jnp.pad(v.reshape(-1).astype(F32), (0, padded - size))
        parts.append(v)
    total = sum(p.shape[0] for p in parts)
    parts.append(jnp.zeros((-(-total // 1024) * 1024 - total,), F32))
    return jnp.concatenate(parts).reshape(-1, 128)


def _offsets(sizes):
    offs, o = {}, 0
    for (name, _), size in zip(SMALL_SLOTS, sizes):
        offs[name] = (o, size)
        o += -(-size // 128) * 128
    return offs


def kernel(x, c, ctx, c_ctx, l0_norm1_g, l0_w_mod, l0_b_mod, l0_w_in, l0_q_norm_g, l0_k_norm_g, l0_conv_w, l0_w_out, l0_norm2_g, l0_w_gate, l0_w_up, l0_w_down, l1_norm1_g, l1_w_mod, l1_b_mod, l1_w_in, l1_q_norm_g, l1_k_norm_g, l1_sink, l1_pool_w, l1_pool_scale, l1_w_out, l1_norm2_g, l1_w_gate, l1_w_up, l1_w_down, final_norm_g, loss_target, m_c_ctx, m_l0_norm1_g, m_l0_w_mod, m_l0_b_mod, m_l0_w_in, m_l0_q_norm_g, m_l0_k_norm_g, m_l0_conv_w, m_l0_w_out, m_l0_norm2_g, m_l0_w_gate, m_l0_w_up, m_l0_w_down, m_l1_norm1_g, m_l1_w_mod, m_l1_b_mod, m_l1_w_in, m_l1_q_norm_g, m_l1_k_norm_g, m_l1_sink, m_l1_pool_w, m_l1_pool_scale, m_l1_w_out, m_l1_norm2_g, m_l1_w_gate, m_l1_w_up, m_l1_w_down, m_final_norm_g, v_c_ctx, v_l0_norm1_g, v_l0_w_mod, v_l0_b_mod, v_l0_w_in, v_l0_q_norm_g, v_l0_k_norm_g, v_l0_conv_w, v_l0_w_out, v_l0_norm2_g, v_l0_w_gate, v_l0_w_up, v_l0_w_down, v_l1_norm1_g, v_l1_w_mod, v_l1_b_mod, v_l1_w_in, v_l1_q_norm_g, v_l1_k_norm_g, v_l1_sink, v_l1_pool_w, v_l1_pool_scale, v_l1_w_out, v_l1_norm2_g, v_l1_w_gate, v_l1_w_up, v_l1_w_down, v_final_norm_g):
    a = dict(locals())
    xi, yi, ci = lax.axis_index("x"), lax.axis_index("y"), lax.axis_index("c")
    p = 2 * xi + yi
    me = 4 * xi + 2 * yi + ci
    d = x.shape[-1]
    conv_cols = l0_conv_w.shape[1]

    row0 = jnp.concatenate([c, jnp.pad(l0_conv_w, ((0, 0), (0, d - conv_cols))), jnp.zeros((4, d), F32)], axis=0)
    gathered = all_gather8(row0, "gather_cond").reshape(N_DEV, 8, d)
    c_all = gathered[:, 0]
    conv_w = gathered[0::2, 1:4, :conv_cols].transpose(1, 0, 2).reshape(3, N_SHARD * conv_cols)
    c16_fwd = jnp.concatenate([c_all, c_ctx[None], jnp.zeros((7, d), F32)], axis=0)
    c16_bwd = jnp.concatenate([c_all, jnp.broadcast_to(c_ctx[None], (8, d))], axis=0)

    ns_mod = l0_w_mod.shape[1]
    mod_parts = [mod_fwd(c16_fwd, a[f"l{l}_w_mod"], lax.dynamic_slice(a[f"l{l}_b_mod"], (p * ns_mod,), (ns_mod,))[None],
                         f"l{l}_mod_fwd") for l in (0, 1)]
    modg = all_gather8(jnp.concatenate(mod_parts, axis=0), "gather_mod").reshape(N_DEV, 2, 16, ns_mod)[0::2]
    mod_full = modg.transpose(1, 2, 0, 3).reshape(2, 16, N_SHARD * ns_mod)
    mods = []
    for l in (0, 1):
        lat = lax.dynamic_index_in_dim(mod_full[l], me, axis=0, keepdims=False)
        cx = mod_full[l, 8]
        mods.append({nm: jnp.stack([cx[j * d:(j + 1) * d], lat[j * d:(j + 1) * d]])[:, None, :]
                     for j, nm in enumerate(MOD_NAMES)})

    placed = {(l, nm): place_shard(a[f"l{l}_{nm}"], p, f"l{l}_{nm}_place") for l in (0, 1) for nm in BIG_NAMES}
    ready = {(0, "w_in"): gather_weight_shards([placed[0, "w_in"]], mod_full)[0]}
    groups = (((0, "w_out"), (0, "w_gate"), (0, "w_up")), ((0, "w_down"),), ((1, "w_in"), (1, "w_out")),
              ((1, "w_gate"), (1, "w_up"), (1, "w_down")))
    staged = len(groups) - 1
    flights, after = [], ready[0, "w_in"]
    for gi, grp in enumerate(groups):
        first_copies = gather_half_copies if gi == staged else gather_copies
        flights.append(split_start(f"gather_start_{gi}", first_copies(len(grp)), 3 * len(grp),
                                   [placed[k] for k in grp], after))
        after = flights[-1][3]

    def forward_staged(after):
        n = len(groups[staged])
        send_sems, recv_sems, in_flight, _ = flights[staged]
        landed = split_wait(f"gather_wait_{staged}", gather_half_copies(n), send_sems, recv_sems, in_flight, after)
        flights[staged] = split_start("gather_forward_start", forward_half_copies(n), 3 * n, landed, after)
        return flights[staged][3]

    def weight_of(l):
        def weight(nm, after):
            if (l, nm) not in ready:
                gi = next(i for i, grp in enumerate(groups) if (l, nm) in grp)
                send_sems, recv_sems, in_flight, _ = flights[gi]
                name, copies = ("gather_forward_wait", forward_half_copies) if gi == staged else (
                    f"gather_wait_{gi}", gather_copies)
                landed = split_wait(name, copies(len(groups[gi])), send_sems, recv_sems, in_flight, after)
                ready.update(zip(groups[gi], landed))
            return ready[l, nm]
        return weight

    layers = [dict(norm1_g=a[f"l{l}_norm1_g"][None], norm2_g=a[f"l{l}_norm2_g"][None],
                   q_g=a[f"l{l}_q_norm_g"][None], k_g=a[f"l{l}_k_norm_g"][None]) for l in (0, 1)]
    layers[0].update(conv_w=conv_w, sink=jnp.zeros((Q_HEADS,), F32))
    layers[1].update(sink=l1_sink, pool_w=l1_pool_w, pool_scale=l1_pool_scale[None])

    ctx_len = ctx.shape[1]
    h = jnp.concatenate([ctx[0], x[0]], axis=0)
    tabs = _rope_tables(h.shape[0], ctx_len)
    h, saved0 = _layer_fwd(h, layers[0], weight_of(0), _tie(mods[0], "shift1", flights[-1][3]), tabs, ctx_len, False,
                           "l0_")
    h, saved1 = _layer_fwd(h, layers[1], weight_of(1), mods[1], tabs, ctx_len, True, "l1_", forward_staged)
    loss_tile, dh, dfg = final_loss(h, final_norm_g[None], loss_target[0])

    keys = [[f"l{l}_{nm}" for nm in BIG_NAMES] for l in (0, 1)]
    n_big = len(BIG_NAMES)
    dh, ffn_big, ffn_small, ffn_mod = _layer_bwd_ffn(dh, saved1, layers[1], mods[1], "l1_")
    dh, mix_big, mix_small, mix_mod = _layer_bwd_mixer(dh, saved1, layers[1], mods[1], tabs, ctx_len, True, "l1_")
    bigs, smalls, dmods = [None, {**ffn_big, **mix_big}], [None, {**ffn_small, **mix_small}], [None, None]
    dmods[1] = _dmod_rows(mix_mod, ffn_mod)
    grads1 = [bigs[1][nm] for nm in BIG_NAMES]
    lands = [lax.empty((g.shape[0], g.shape[1] // 2, g.shape[2]), g.dtype) for g in grads1]
    send_a, recv_a, fly_a, token_a = split_start("l1_sibling_halves_start", sibling_half_copies(n_big), n_big,
                                                 grads1 + lands, dh)
    dh, ffn_big, ffn_small, ffn_mod = _layer_bwd_ffn(dh, saved0, layers[0], _tie(mods[0], "gate2", token_a), "l0_")
    fly_a = split_wait("l1_sibling_halves_wait", sibling_half_copies(n_big), send_a, recv_a, fly_a, dh)
    chip_sums1 = [add_own_half(g, r, ci, k + "_add_halves")
                  for k, g, r in zip(keys[1], fly_a[:n_big], fly_a[n_big:])]
    lands = [lax.empty((3,) + s.shape[1:], s.dtype) for s in chip_sums1]
    send_b, recv_b, fly_b, token_b = split_start("l1_chip_quarters_start", chip_quarter_copies(n_big), 3 * n_big,
                                                 chip_sums1 + lands, dh)
    ffn_names = ("w_gate", "w_up", "w_down")
    n_ffn = len(ffn_names)
    ffn_keys = ["l0_" + nm for nm in ffn_names]
    grads0f = [ffn_big[nm] for nm in ffn_names]
    lands = [lax.empty((g.shape[0], g.shape[1] // 2, g.shape[2]), g.dtype) for g in grads0f]
    send_c, recv_c, fly_c, token_c = split_start("l0_ffn_sibling_halves_start", sibling_half_copies(n_ffn), n_ffn,
                                                 grads0f + lands, token_b)
    ffn_flight = []

    def between(dq):
        landed = split_wait("l0_ffn_sibling_halves_wait", sibling_half_copies(n_ffn), send_c, recv_c, fly_c, dq)
        sums = [add_own_half(g, r, ci, k + "_add_halves") for k, g, r in zip(ffn_keys, landed[:n_ffn], landed[n_ffn:])]
        slots = [lax.empty((3,) + s.shape[1:], s.dtype) for s in sums]
        ffn_flight.extend(split_start("l0_ffn_chip_quarters_start", chip_quarter_copies(n_ffn), 3 * n_ffn,
                                      sums + slots, dq))
        return ffn_flight[3]

    dh, mix_big, mix_small, mix_mod = _layer_bwd_mixer(dh, saved0, layers[0], _tie(mods[0], "gate1", token_c), tabs,
                                                       ctx_len, False, "l0_", between)
    fly_b = split_wait("l1_chip_quarters_wait", chip_quarter_copies(n_big), send_b, recv_b, fly_b, dh)
    fly_d = split_wait("l0_ffn_chip_quarters_wait", chip_quarter_copies(n_ffn), ffn_flight[0], ffn_flight[1],
                       ffn_flight[2], dh)
    bigs[0], smalls[0], dmods[0] = {**ffn_big, **mix_big}, {**ffn_small, **mix_small}, _dmod_rows(mix_mod, ffn_mod)
    dx = dh[ctx_len:]
    chip_sums = dict(zip(keys[1] + ffn_keys, list(fly_b[:n_big]) + list(fly_d[:n_ffn])))
    quarters = dict(zip(keys[1] + ffn_keys, list(fly_b[n_big:]) + list(fly_d[n_ffn:])))

    partial = {"l0_conv_w": smalls[0]["conv_w"], "l1_sink": smalls[1]["sink"], "l1_pool_w": smalls[1]["pool_w"],
               "l1_pool_scale": smalls[1]["pool_scale"], "final_norm_g": dfg}
    for l in (0, 1):
        partial.update({f"l{l}_dmod_lat": dmods[l][1], f"l{l}_dmod_ctx": dmods[l][0],
                        f"l{l}_b_mod": dmods[l][0] + dmods[l][1], f"l{l}_norm1_g": smalls[l]["norm1_g"],
                        f"l{l}_norm2_g": smalls[l]["norm2_g"], f"l{l}_q_norm_g": smalls[l]["q_g"],
                        f"l{l}_k_norm_g": smalls[l]["k_g"]})
    sizes = [int(np.prod(partial[name].shape)) for name, _ in SMALL_SLOTS]
    offs = _offsets(sizes)

    last_keys = ["l0_w_in", "l0_w_out"]
    grads_last = [bigs[0]["w_in"], bigs[0]["w_out"]]
    lands = [lax.empty((g.shape[0], g.shape[1] // 2, g.shape[2]), g.dtype) for g in grads_last]
    send_e, recv_e, fly_e, token_e = split_start("l0_last_sibling_halves_start", sibling_half_copies(2), 2,
                                                 grads_last + lands, dh)
    gpack = _pack(partial, sizes) + token_e[0:1, 0:1]
    rows = gpack.shape[0]
    small_all = all_gather8(gpack, "gather_small").reshape(N_DEV, rows, 128)
    packs = [_pack({name: a[pre + par] for name, par in SMALL_SLOTS if par}, sizes) for pre in ("", "m_", "v_")]
    small_out = [t.reshape(-1) for t in small_update(small_all, *packs)]
    flat_all = small_all.reshape(N_DEV, rows * 128)

    def slot(flat, name, shape):
        o, size = offs[name]
        return flat[o:o + size].reshape(shape)

    results = {}
    for name, par in SMALL_SLOTS:
        if par:
            results[par] = tuple(slot(t, name, a[par].shape) for t in small_out)

    ctx_rows = []
    for l in (0, 1):
        o_lat, o_ctx = offs[f"l{l}_dmod_lat"][0], offs[f"l{l}_dmod_ctx"][0]
        lat = lax.dynamic_slice(flat_all, (0, o_lat + p * ns_mod), (N_DEV, ns_mod))
        cxr = lax.dynamic_slice(flat_all, (0, o_ctx + p * ns_mod), (N_DEV, ns_mod))
        ctx_rows.append(cxr)
        results[f"l{l}_w_mod"] = tuple(wmod_update(c16_bwd, jnp.concatenate([lat, cxr], axis=0), a[f"l{l}_w_mod"],
                                                   a[f"m_l{l}_w_mod"], a[f"v_l{l}_w_mod"], f"l{l}_w_mod_update"))
    part = cctx_partial(ctx_rows[0], l0_w_mod, ctx_rows[1], l1_w_mod)
    parts4 = all_gather8(part, "gather_cctx").reshape(N_DEV, 8, d)[0::2, 0]
    results["c_ctx"] = tuple(t[0] for t in cctx_update(parts4, c_ctx[None], m_c_ctx[None], v_c_ctx[None]))

    gconv = lax.dynamic_slice(slot(small_out[0], "l0_conv_w", (3, N_SHARD * conv_cols)), (0, p * conv_cols),
                              (3, conv_cols))
    results["l0_conv_w"] = tuple(adamw(l0_conv_w, gconv, m_l0_conv_w, v_l0_conv_w, "l0_conv_w_adamw"))

    fly_e = split_wait("l0_last_sibling_halves_wait", sibling_half_copies(2), send_e, recv_e, fly_e,
                       results["c_ctx"][0])
    sums_last = [add_own_half(g, r, ci, k + "_add_halves") for k, g, r in zip(last_keys, fly_e[:2], fly_e[2:])]
    slots = [lax.empty((3,) + s.shape[1:], s.dtype) for s in sums_last]
    send_f, recv_f, fly_f, token_f = split_start("l0_last_chip_quarters_start", chip_quarter_copies(2), 6,
                                                 sums_last + slots, sums_last[0])

    def finish(names, tag, deps):
        halves = [add_quarters(chip_sums[k], quarters[k], p, ci, k + "_add_quarters", deps) for k in names]
        for k, g in zip(names, exchange_final_halves(halves, "exchange_final_halves_" + tag)):
            results[k] = tuple(adamw(a[k], g, a["m_" + k], a["v_" + k], k + "_adamw"))

    done_keys = keys[1] + ffn_keys
    finish(done_keys, "first", (token_f,))
    fly_f = split_wait("l0_last_chip_quarters_wait", chip_quarter_copies(2), send_f, recv_f, fly_f,
                       results[done_keys[-1]][1])
    chip_sums.update(zip(last_keys, fly_f[:2]))
    quarters.update(zip(last_keys, fly_f[2:]))
    finish(last_keys, "last", ())

    loss = lax.psum(loss_tile[0, 0], ("x", "y", "c"))
    out = [loss, dx[None]]
    for j in range(4):
        out += [results[k][j] for k in WEIGHT_NAMES]
    return tuple(out)
```

```python
import numpy as np
import jax
import jax.numpy as jnp
from jax import lax
from jax.experimental import pallas as pl
from jax.experimental.pallas import tpu as pltpu

F32 = jnp.float32
BF16 = jnp.bfloat16
MESH = pl.DeviceIdType.MESH

HEAD_DIM = 128
Q_HEADS = 12
KV_HEADS = 4
GROUPS = Q_HEADS // KV_HEADS
ATT_WIDTH = Q_HEADS * HEAD_DIM
KV_WIDTH = KV_HEADS * HEAD_DIM
QKV_WIDTH = ATT_WIDTH + 2 * KV_WIDTH
AUX_WIDTH = 512
AUX_GROUPS = 4
POOL_WINDOWS = (2, 4, 8, 16)
WINDOW = 128
GRID_W = 64
ROPE_THETA = 10000.0
N_MOD = 6
EPS = 1e-6
NEG_INF = -1e30
ATT_SCALE = HEAD_DIM ** -0.5
N_SHARD = 4
N_DEV = 8

ADAM_LR = 0.001
ADAM_B1 = 0.9
ADAM_B2 = 0.999
ADAM_EPS = 1e-08
ADAM_WD = 0.01
ADAM_STEP = 10

TM = 256
TM_MATMUL = 544
BAND = TM + 2 * WINDOW
VMEM_LIMIT = 56 * 1024 * 1024

NT_DIMS = (((1,), (1,)), ((), ()))
TN_DIMS = (((0,), (0,)), ((), ()))


def _pcall(body, **kw):
    return pl.pallas_call(body, **kw)


def _params(*sem):
    return pltpu.CompilerParams(dimension_semantics=sem, vmem_limit_bytes=VMEM_LIMIT)


def _sel(i):
    return jnp.minimum(i, 1)


def _sel_spec(d):
    return pl.BlockSpec((None, 1, d), lambda i: (_sel(i), 0, 0))


def _dot(a, b):
    return jnp.dot(a, b, preferred_element_type=F32)


def _dot_nt(a, b):
    return lax.dot_general(a, b, NT_DIMS, preferred_element_type=F32)


def _dot_tn(a, b):
    return lax.dot_general(a, b, TN_DIMS, preferred_element_type=F32)


def _sigmoid(x):
    return 0.5 * jnp.tanh(0.5 * x) + 0.5


def norm_mod_fwd(h, g, shift, scale, name):
    t, d = h.shape

    def body(h_ref, g_ref, sh_ref, sc_ref, o_ref):
        x = h_ref[...]
        r = lax.rsqrt(jnp.mean(x * x, axis=-1, keepdims=True) + EPS)
        y = x * r * g_ref[...]
        o_ref[...] = (y * (1.0 + sc_ref[...]) + sh_ref[...]).astype(BF16)

    return _pcall(
        body, name=name, grid=(t // TM,),
        in_specs=[pl.BlockSpec((TM, d), lambda i: (i, 0)), pl.BlockSpec((1, d), lambda i: (0, 0)),
                  _sel_spec(d), _sel_spec(d)],
        out_specs=pl.BlockSpec((TM, d), lambda i: (i, 0)),
        out_shape=jax.ShapeDtypeStruct((t, d), BF16),
        compiler_params=_params("parallel"),
    )(h, g, shift, scale)


def norm_mod_bwd(dy, h, g, scale, dres, name):
    t, d = h.shape

    def body(dy_ref, h_ref, g_ref, sc_ref, dres_ref, dh_ref, dsh_ref, dsc_ref, dg_ref):
        i = pl.program_id(0)

        @pl.when(i == 0)
        def _():
            dsh_ref[...] = jnp.zeros_like(dsh_ref)
            dsc_ref[...] = jnp.zeros_like(dsc_ref)
            dg_ref[...] = jnp.zeros_like(dg_ref)

        x = h_ref[...]
        gv = g_ref[...]
        r = lax.rsqrt(jnp.mean(x * x, axis=-1, keepdims=True) + EPS)
        xhat = x * r
        dyv = dy_ref[...]
        s = _sel(i)
        dsh_ref[pl.ds(s, 1)] += jnp.sum(dyv, axis=0, keepdims=True)[None]
        dsc_ref[pl.ds(s, 1)] += jnp.sum(dyv * xhat * gv, axis=0, keepdims=True)[None]
        tt = dyv * (1.0 + sc_ref[...])
        dg_ref[...] += jnp.sum(tt * xhat, axis=0, keepdims=True)
        dxhat = tt * gv
        dx = r * (dxhat - xhat * jnp.mean(dxhat * xhat, axis=-1, keepdims=True))
        dh_ref[...] = dx + dres_ref[...]

    row = pl.BlockSpec((TM, d), lambda i: (i, 0))
    acc2 = pl.BlockSpec((2, 1, d), lambda i: (0, 0, 0))
    return _pcall(
        body, name=name, grid=(t // TM,),
        in_specs=[row, row, pl.BlockSpec((1, d), lambda i: (0, 0)), _sel_spec(d), row],
        out_specs=[row, acc2, acc2, pl.BlockSpec((1, d), lambda i: (0, 0))],
        out_shape=[jax.ShapeDtypeStruct((t, d), F32), jax.ShapeDtypeStruct((2, 1, d), F32),
                   jax.ShapeDtypeStruct((2, 1, d), F32), jax.ShapeDtypeStruct((1, d), F32)],
        compiler_params=_params("arbitrary"),
    )(dy, h, g, scale, dres)


def gate_bwd(dh, f, gate, name):
    t, d = dh.shape

    def body(dh_ref, f_ref, gt_ref, df_ref, dgt_ref):
        i = pl.program_id(0)

        @pl.when(i == 0)
        def _():
            dgt_ref[...] = jnp.zeros_like(dgt_ref)

        dhv = dh_ref[...]
        df_ref[...] = (dhv * gt_ref[...]).astype(BF16)
        dgt_ref[pl.ds(_sel(i), 1)] += jnp.sum(dhv * f_ref[...].astype(F32), axis=0, keepdims=True)[None]

    row = pl.BlockSpec((TM, d), lambda i: (i, 0))
    return _pcall(
        body, name=name, grid=(t // TM,),
        in_specs=[row, row, _sel_spec(d)],
        out_specs=[row, pl.BlockSpec((2, 1, d), lambda i: (0, 0, 0))],
        out_shape=[jax.ShapeDtypeStruct((t, d), BF16), jax.ShapeDtypeStruct((2, 1, d), F32)],
        compiler_params=_params("arbitrary"),
    )(dh, f, gate)


def final_loss(h, fg, target):
    t, d = h.shape

    def body(h_ref, g_ref, tg_ref, loss_ref, dh_ref, dg_ref):
        i = pl.program_id(0)

        @pl.when(i == 0)
        def _():
            loss_ref[...] = jnp.zeros_like(loss_ref)
            dg_ref[...] = jnp.zeros_like(dg_ref)
            dh_ref[...] = jnp.zeros_like(dh_ref)

        @pl.when(i > 0)
        def _():
            x = h_ref[...]
            gv = g_ref[...]
            r = lax.rsqrt(jnp.mean(x * x, axis=-1, keepdims=True) + EPS)
            xhat = x * r
            diff = xhat * gv - tg_ref[...]
            loss_ref[...] += 0.5 * jnp.sum(jnp.mean(diff * diff, axis=-1, keepdims=True), axis=0, keepdims=True)
            dout = diff * (1.0 / d)
            dg_ref[...] += jnp.sum(dout * xhat, axis=0, keepdims=True)
            dxhat = dout * gv
            dh_ref[...] = r * (dxhat - xhat * jnp.mean(dxhat * xhat, axis=-1, keepdims=True))

    row = pl.BlockSpec((TM, d), lambda i: (i, 0))
    return _pcall(
        body, name="final_loss", grid=(t // TM,),
        in_specs=[row, pl.BlockSpec((1, d), lambda i: (0, 0)),
                  pl.BlockSpec((TM, d), lambda i: (jnp.maximum(i - 1, 0), 0))],
        out_specs=[pl.BlockSpec((8, 128), lambda i: (0, 0)), row, pl.BlockSpec((1, d), lambda i: (0, 0))],
        out_shape=[jax.ShapeDtypeStruct((8, 128), F32), jax.ShapeDtypeStruct((t, d), F32),
                   jax.ShapeDtypeStruct((1, d), F32)],
        compiler_params=_params("arbitrary"),
    )(h, fg, target)


def _matmul_rows(t):
    return TM_MATMUL if t % TM_MATMUL == 0 else TM


def _w_nn_spec(w, k):
    if w.shape[1] == k:
        ns = w.shape[2]
        return w, pl.BlockSpec((None, k, ns), lambda j, i: (j, 0, 0)), w.shape[0], ns
    w2 = w.reshape(w.shape[0] * w.shape[1], w.shape[2])
    tn = 1024 if w2.shape[1] % 1024 == 0 else w2.shape[1]
    return w2, pl.BlockSpec((k, tn), lambda j, i: (0, j)), w2.shape[1] // tn, tn


def mm_nn(a, w, name, out_dtype=F32):
    t, k = a.shape
    w, wspec, nb, tn = _w_nn_spec(w, k)
    tm = _matmul_rows(t)

    def body(a_ref, w_ref, o_ref):
        o_ref[...] = _dot(a_ref[...], w_ref[...]).astype(o_ref.dtype)

    return _pcall(
        body, name=name, grid=(nb, t // tm),
        in_specs=[pl.BlockSpec((tm, k), lambda j, i: (i, 0)), wspec],
        out_specs=pl.BlockSpec((tm, tn), lambda j, i: (i, j)),
        out_shape=jax.ShapeDtypeStruct((t, nb * tn), out_dtype),
        compiler_params=_params("parallel", "parallel"),
    )(a, w)


def mm_nn_residual(a, w, h, gate, name):
    t, k = a.shape
    w, wspec, nb, tn = _w_nn_spec(w, k)
    tm = _matmul_rows(t)

    def body(a_ref, w_ref, h_ref, gt_ref, y_ref, o_ref):
        y = _dot(a_ref[...], w_ref[...])
        y_ref[...] = y.astype(BF16)
        row = pl.program_id(1) * tm + lax.broadcasted_iota(jnp.int32, (tm, tn), 0)
        o_ref[...] = h_ref[...] + jnp.where(row < TM, gt_ref[0], gt_ref[1]) * y

    out = pl.BlockSpec((tm, tn), lambda j, i: (i, j))
    return _pcall(
        body, name=name, grid=(nb, t // tm),
        in_specs=[pl.BlockSpec((tm, k), lambda j, i: (i, 0)), wspec, out,
                  pl.BlockSpec((2, 1, tn), lambda j, i: (0, 0, j))],
        out_specs=[out, out],
        out_shape=[jax.ShapeDtypeStruct((t, nb * tn), BF16), jax.ShapeDtypeStruct((t, nb * tn), F32)],
        compiler_params=_params("parallel", "parallel"),
    )(a, w, h, gate)


def mm_swiglu(a, wg, wu, name):
    t, k = a.shape
    s, _, ns = wg.shape

    def body(a_ref, wg_ref, wu_ref, g_ref, u_ref, act_ref):
        av = a_ref[...]
        g = _dot(av, wg_ref[...])
        u = _dot(av, wu_ref[...])
        g_ref[...] = g.astype(BF16)
        u_ref[...] = u.astype(BF16)
        act_ref[...] = (g * _sigmoid(g) * u).astype(BF16)

    tm = _matmul_rows(t)
    wspec = pl.BlockSpec((None, k, ns), lambda j, i: (j, 0, 0))
    out = pl.BlockSpec((tm, ns), lambda j, i: (i, j))
    return _pcall(
        body, name=name, grid=(s, t // tm),
        in_specs=[pl.BlockSpec((tm, k), lambda j, i: (i, 0)), wspec, wspec],
        out_specs=[out, out, out],
        out_shape=[jax.ShapeDtypeStruct((t, s * ns), BF16)] * 3,
        compiler_params=_params("parallel", "parallel"),
    )(a, wg, wu)


def mm_nt_cols(dy, w, name, add=None):
    t, n = dy.shape
    s, k, ns = w.shape
    tk = 1024

    def body(*refs):
        if add is None:
            dy_ref, w_ref, o_ref = refs
        else:
            dy_ref, w_ref, add_ref, o_ref = refs
        acc = _dot_nt(dy_ref[:, 0:ns], w_ref[0])
        for sh in range(1, s):
            acc += _dot_nt(dy_ref[:, sh * ns:(sh + 1) * ns], w_ref[sh])
        if add is not None:
            acc += add_ref[...]
        o_ref[...] = acc

    tm = _matmul_rows(t)
    out = pl.BlockSpec((tm, tk), lambda j, i: (i, j))
    in_specs = [pl.BlockSpec((tm, n), lambda j, i: (i, 0)), pl.BlockSpec((s, tk, ns), lambda j, i: (0, j, 0))]
    args = [dy, w]
    if add is not None:
        in_specs.append(out)
        args.append(add)
    return _pcall(
        body, name=name, grid=(k // tk, t // tm),
        in_specs=in_specs, out_specs=out,
        out_shape=jax.ShapeDtypeStruct((t, k), F32),
        compiler_params=_params("parallel", "parallel"),
    )(*args)


def mm_nt_rows(dy, w, name):
    t, n = dy.shape
    s, ks, _ = w.shape

    tm = _matmul_rows(t)

    def body(dy_ref, w_ref, o_ref):
        o_ref[...] = _dot_nt(dy_ref[...], w_ref[...]).astype(BF16)

    return _pcall(
        body, name=name, grid=(s, t // tm),
        in_specs=[pl.BlockSpec((tm, n), lambda j, i: (i, 0)), pl.BlockSpec((None, ks, n), lambda j, i: (j, 0, 0))],
        out_specs=pl.BlockSpec((tm, ks), lambda j, i: (i, j)),
        out_shape=jax.ShapeDtypeStruct((t, s * ks), BF16),
        compiler_params=_params("parallel", "parallel"),
    )(dy, w)


def mm_nt_swiglu_bwd(df, wd, g, u, name):
    t, n = df.shape
    s, ks, _ = wd.shape

    def body(df_ref, w_ref, g_ref, u_ref, dg_ref, du_ref):
        da = _dot_nt(df_ref[...], w_ref[...])
        gv = g_ref[...].astype(F32)
        sg = _sigmoid(gv)
        silu = gv * sg
        dg_ref[...] = (da * u_ref[...].astype(F32) * (sg * (1.0 + gv * (1.0 - sg)))).astype(BF16)
        du_ref[...] = (da * silu).astype(BF16)

    tm = _matmul_rows(t)
    blk = pl.BlockSpec((tm, ks), lambda j, i: (i, j))
    return _pcall(
        body, name=name, grid=(s, t // tm),
        in_specs=[pl.BlockSpec((tm, n), lambda j, i: (i, 0)), pl.BlockSpec((None, ks, n), lambda j, i: (j, 0, 0)),
                  blk, blk],
        out_specs=[blk, blk],
        out_shape=[jax.ShapeDtypeStruct((t, s * ks), BF16)] * 2,
        compiler_params=_params("parallel", "parallel"),
    )(df, wd, g, u)


def mm_tn(x, dy, name, col_sharded):
    t, k = x.shape
    n = dy.shape[1]
    tk = next(c for c in (1024, 1408, 512, k) if k % c == 0)
    mc = t // 2 if (t // 2) % 128 == 0 else t
    n_m = t // mc
    if col_sharded:
        tn = n // N_SHARD
        out_shape = jax.ShapeDtypeStruct((N_SHARD, k, tn), BF16)
        out_spec = pl.BlockSpec((None, tk, tn), lambda j, kb, m: (j, kb, 0))
    else:
        tn = 1024
        out_shape = jax.ShapeDtypeStruct((k, n), BF16)
        out_spec = pl.BlockSpec((tk, tn), lambda j, kb, m: (kb, j))

    def body(x_ref, dy_ref, o_ref, acc_ref):
        m = pl.program_id(2)
        acc = _dot_tn(x_ref[...], dy_ref[...])

        @pl.when(m == 0)
        def _():
            acc_ref[...] = acc

        @pl.when((m > 0) & (m < n_m - 1))
        def _():
            acc_ref[...] += acc

        @pl.when(m == n_m - 1)
        def _():
            o_ref[...] = (acc if n_m == 1 else acc_ref[...] + acc).astype(BF16)

    out = _pcall(
        body, name=name, grid=(n // tn, k // tk, n_m),
        in_specs=[pl.BlockSpec((mc, tk), lambda j, kb, m: (m, kb)), pl.BlockSpec((mc, tn), lambda j, kb, m: (m, j))],
        out_specs=out_spec, out_shape=out_shape,
        scratch_shapes=[pltpu.VMEM((tk, tn), F32)],
        compiler_params=_params("parallel", "parallel", "arbitrary"),
    )(x, dy)
    return out if col_sharded else out.reshape(N_SHARD, k // N_SHARD, n)


def _swap32(y):
    right = pltpu.roll(y, 32, 1)
    left = pltpu.roll(y, 96, 1)
    lane = lax.broadcasted_iota(jnp.int32, y.shape, 1)
    return jnp.where((lane // 32) % 2 == 0, left, right)


def _rope_tables(t, ctx_len):
    s = t - ctx_len
    pos = np.arange(s)
    row = (pos // GRID_W).astype(np.float32)
    col = (pos % GRID_W).astype(np.float32)
    half = HEAD_DIM // 2
    inv = np.power(np.float32(ROPE_THETA), -np.arange(0, half, 2, dtype=np.float32) / np.float32(half))
    ar = row[:, None] * inv
    ac = col[:, None] * inv
    cos = np.concatenate([np.cos(ar), np.cos(ar), np.cos(ac), np.cos(ac)], axis=1)
    sin = np.concatenate([-np.sin(ar), np.sin(ar), -np.sin(ac), np.sin(ac)], axis=1)
    cos = np.concatenate([np.ones((ctx_len, HEAD_DIM), np.float32), cos.astype(np.float32)], axis=0)
    sin = np.concatenate([np.zeros((ctx_len, HEAD_DIM), np.float32), sin.astype(np.float32)], axis=0)
    return jnp.asarray(cos, F32), jnp.asarray(sin, F32)


def qk_prep_fwd(proj, qg, kg, cos, sin, name):
    t = proj.shape[0]

    def body(p_ref, qg_ref, kg_ref, c_ref, s_ref, q_ref, k_ref, v_ref):
        cv = c_ref[...]
        sv = s_ref[...]
        for hd in range(Q_HEADS + KV_HEADS):
            x = p_ref[:, hd * HEAD_DIM:(hd + 1) * HEAD_DIM]
            gv = qg_ref[...] if hd < Q_HEADS else kg_ref[...]
            y = x * lax.rsqrt(jnp.mean(x * x, axis=-1, keepdims=True) + EPS) * gv
            out = y * cv + _swap32(y) * sv
            if hd < Q_HEADS:
                q_ref[:, hd * HEAD_DIM:(hd + 1) * HEAD_DIM] = (out * ATT_SCALE).astype(BF16)
            else:
                k_ref[:, (hd - Q_HEADS) * HEAD_DIM:(hd - Q_HEADS + 1) * HEAD_DIM] = out.astype(BF16)
        v_ref[...] = p_ref[:, ATT_WIDTH + KV_WIDTH:QKV_WIDTH].astype(BF16)

    vec = pl.BlockSpec((1, HEAD_DIM), lambda i: (0, 0))
    tab = pl.BlockSpec((TM, HEAD_DIM), lambda i: (i, 0))
    return _pcall(
        body, name=name, grid=(t // TM,),
        in_specs=[pl.BlockSpec((TM, QKV_WIDTH), lambda i: (i, 0)), vec, vec, tab, tab],
        out_specs=[pl.BlockSpec((TM, ATT_WIDTH), lambda i: (i, 0)), pl.BlockSpec((TM, KV_WIDTH), lambda i: (i, 0)),
                   pl.BlockSpec((TM, KV_WIDTH), lambda i: (i, 0))],
        out_shape=[jax.ShapeDtypeStruct((t, ATT_WIDTH), BF16), jax.ShapeDtypeStruct((t, KV_WIDTH), BF16),
                   jax.ShapeDtypeStruct((t, KV_WIDTH), BF16)],
        compiler_params=_params("parallel"),
    )(proj, qg, kg, cos, sin)


def qk_prep_bwd(dq, dk, dv, daux, proj, qg, kg, cos, sin, name):
    t, n_in = proj.shape
    n_aux = daux.shape[0]

    def body(dq_ref, dk_ref, dv_ref, da_ref, p_ref, qg_ref, kg_ref, c_ref, s_ref, o_ref, dqg_ref, dkg_ref):
        @pl.when(pl.program_id(0) == 0)
        def _():
            dqg_ref[...] = jnp.zeros_like(dqg_ref)
            dkg_ref[...] = jnp.zeros_like(dkg_ref)

        cv = c_ref[...]
        sv = s_ref[...]
        for hd in range(Q_HEADS + KV_HEADS):
            cols = slice(hd * HEAD_DIM, (hd + 1) * HEAD_DIM)
            x = p_ref[:, cols]
            if hd < Q_HEADS:
                gv, dyr, dg_ref = qg_ref[...], dq_ref[:, cols] * ATT_SCALE, dqg_ref
            else:
                kc = slice((hd - Q_HEADS) * HEAD_DIM, (hd - Q_HEADS + 1) * HEAD_DIM)
                gv, dyr, dg_ref = kg_ref[...], dk_ref[:, kc], dkg_ref
            r = lax.rsqrt(jnp.mean(x * x, axis=-1, keepdims=True) + EPS)
            xhat = x * r
            dy = dyr * cv + _swap32(dyr * sv)
            dg_ref[...] += jnp.sum(dy * xhat, axis=0, keepdims=True)
            dxhat = dy * gv
            o_ref[:, cols] = (r * (dxhat - xhat * jnp.mean(dxhat * xhat, axis=-1, keepdims=True))).astype(BF16)
        o_ref[:, ATT_WIDTH + KV_WIDTH:QKV_WIDTH] = dv_ref[...].astype(BF16)
        for a in range(n_aux):
            o_ref[:, QKV_WIDTH + a * AUX_WIDTH:QKV_WIDTH + (a + 1) * AUX_WIDTH] = da_ref[a]

    vec = pl.BlockSpec((1, HEAD_DIM), lambda i: (0, 0))
    tab = pl.BlockSpec((TM, HEAD_DIM), lambda i: (i, 0))
    return _pcall(
        body, name=name, grid=(t // TM,),
        in_specs=[pl.BlockSpec((TM, ATT_WIDTH), lambda i: (i, 0)), pl.BlockSpec((TM, KV_WIDTH), lambda i: (i, 0)),
                  pl.BlockSpec((TM, KV_WIDTH), lambda i: (i, 0)),
                  pl.BlockSpec((n_aux, TM, AUX_WIDTH), lambda i: (0, i, 0)),
                  pl.BlockSpec((TM, QKV_WIDTH), lambda i: (i, 0)), vec, vec, tab, tab],
        out_specs=[pl.BlockSpec((TM, n_in), lambda i: (i, 0)), vec, vec],
        out_shape=[jax.ShapeDtypeStruct((t, n_in), BF16), jax.ShapeDtypeStruct((1, HEAD_DIM), F32),
                   jax.ShapeDtypeStruct((1, HEAD_DIM), F32)],
        compiler_params=_params("arbitrary"),
    )(dq, dk, dv, daux, proj, qg, kg, cos, sin)


def _band_start(i, t):
    return pl.multiple_of(jnp.clip(i * TM - WINDOW, 0, t - BAND), WINDOW)


def _band_mask(i, start, ctx_len):
    shape = (TM, ctx_len + BAND)
    col = lax.broadcasted_iota(jnp.int32, shape, 1)
    qrow = i * TM + lax.broadcasted_iota(jnp.int32, shape, 0)
    krow = start + col - ctx_len
    band_ok = (krow >= ctx_len) & (jnp.abs(krow - qrow) <= WINDOW)
    return (col < ctx_len) | band_ok


def attention_fwd(q, k, v, sink, ctx_len, windowed, name):
    t = q.shape[0]
    d_model = ATT_WIDTH + AUX_WIDTH
    gw = GROUPS * HEAD_DIM

    def one_head(qh, kk, vv, mask, sink_val):
        s = _dot_nt(qh, kk)
        if mask is not None:
            s = jnp.where(mask, s, NEG_INF)
        m = jnp.max(s, axis=-1, keepdims=True)
        if sink_val is not None:
            m = jnp.maximum(m, sink_val)
        p = jnp.exp(s - m)
        l = jnp.sum(p, axis=-1, keepdims=True)
        if sink_val is not None:
            l = l + jnp.exp(sink_val - m)
        o = _dot(p.astype(BF16), vv) / l
        return o, m + jnp.log(l)

    def body(sink_ref, q_ref, k_ref, v_ref, o_ref, lse_ref):
        kv = pl.program_id(0)
        i = pl.program_id(1)

        def run(kk, vv, mask):
            for g in range(GROUPS):
                sink_val = sink_ref[kv * GROUPS + g] if windowed else None
                o, lse = one_head(q_ref[:, g * HEAD_DIM:(g + 1) * HEAD_DIM], kk, vv, mask, sink_val)
                o_ref[:, g * HEAD_DIM:(g + 1) * HEAD_DIM] = o.astype(BF16)
                lse_ref[g] = lse

        @pl.when(i == 0)
        def _():
            if windowed:
                o_ref[...] = jnp.zeros_like(o_ref)
                lse_ref[...] = jnp.zeros_like(lse_ref)
            else:
                run(k_ref[0:ctx_len], v_ref[0:ctx_len], None)

        @pl.when(i > 0)
        def _():
            if windowed:
                start = _band_start(i, t)
                kk = jnp.concatenate([k_ref[0:ctx_len], k_ref[pl.ds(start, BAND)]], axis=0)
                vv = jnp.concatenate([v_ref[0:ctx_len], v_ref[pl.ds(start, BAND)]], axis=0)
                run(kk, vv, _band_mask(i, start, ctx_len))
            else:
                run(k_ref[...], v_ref[...], None)

    kvspec = pl.BlockSpec((t, HEAD_DIM), lambda kv, i: (0, kv))
    return _pcall(
        body, name=name, grid=(KV_HEADS, t // TM),
        in_specs=[pl.BlockSpec(memory_space=pltpu.SMEM), pl.BlockSpec((TM, gw), lambda kv, i: (i, kv)), kvspec, kvspec],
        out_specs=[pl.BlockSpec((TM, gw), lambda kv, i: (i, kv)),
                   pl.BlockSpec((GROUPS, TM, 1), lambda kv, i: (kv, i, 0))],
        out_shape=[jax.ShapeDtypeStruct((t, d_model), BF16), jax.ShapeDtypeStruct((Q_HEADS, t, 1), F32)],
        compiler_params=_params("parallel", "parallel"),
    )(sink, q, k, v)


def attention_bwd(q, k, v, cat, lse, dcat, sink, ctx_len, windowed, name):
    t = q.shape[0]
    gw = GROUPS * HEAD_DIM

    def body(sink_ref, q_ref, k_ref, v_ref, o_ref, lse_ref, do_ref, dq_ref, dk_ref, dv_ref, dsink_ref):
        kv = pl.program_id(0)
        i = pl.program_id(1)

        @pl.when(i == 0)
        def _():
            dk_ref[...] = jnp.zeros_like(dk_ref)
            dv_ref[...] = jnp.zeros_like(dv_ref)
            dsink_ref[...] = jnp.zeros_like(dsink_ref)

        def run(kk, vv, mask, accumulate):
            for g in range(GROUPS):
                cols = slice(g * HEAD_DIM, (g + 1) * HEAD_DIM)
                qh = q_ref[:, cols]
                dob = do_ref[:, cols]
                delta = jnp.sum(dob.astype(F32) * o_ref[:, cols].astype(F32), axis=-1, keepdims=True)
                lse_g = lse_ref[g]
                s = _dot_nt(qh, kk)
                if mask is not None:
                    s = jnp.where(mask, s, NEG_INF)
                p = jnp.exp(s - lse_g)
                dp = _dot_nt(dob, vv)
                ds = (p * (dp - delta)).astype(BF16)
                dq_ref[:, cols] = _dot(ds, kk)
                accumulate(_dot_tn(ds, qh), _dot_tn(p.astype(BF16), dob))
                if windowed:
                    p_sink = jnp.exp(sink_ref[kv * GROUPS + g] - lse_g)
                    dsink_ref[g:g + 1, :] += jnp.sum(-p_sink * delta, axis=0, keepdims=True)

        @pl.when(i == 0)
        def _():
            if windowed:
                dq_ref[...] = jnp.zeros_like(dq_ref)
            else:
                def acc_ctx(dkp, dvp):
                    dk_ref[0:ctx_len] += dkp
                    dv_ref[0:ctx_len] += dvp

                run(k_ref[0:ctx_len], v_ref[0:ctx_len], None, acc_ctx)

        @pl.when(i > 0)
        def _():
            if windowed:
                start = _band_start(i, t)
                kk = jnp.concatenate([k_ref[0:ctx_len], k_ref[pl.ds(start, BAND)]], axis=0)
                vv = jnp.concatenate([v_ref[0:ctx_len], v_ref[pl.ds(start, BAND)]], axis=0)

                def acc_band(dkp, dvp):
                    dk_ref[0:ctx_len] += dkp[0:ctx_len]
                    dv_ref[0:ctx_len] += dvp[0:ctx_len]
                    dk_ref[pl.ds(start, BAND)] += dkp[ctx_len:ctx_len + BAND]
                    dv_ref[pl.ds(start, BAND)] += dvp[ctx_len:ctx_len + BAND]

                run(kk, vv, _band_mask(i, start, ctx_len), acc_band)
            else:
                def acc_all(dkp, dvp):
                    dk_ref[...] += dkp
                    dv_ref[...] += dvp

                run(k_ref[...], v_ref[...], None, acc_all)

    kvspec = pl.BlockSpec((t, HEAD_DIM), lambda kv, i: (0, kv))
    grp = pl.BlockSpec((TM, gw), lambda kv, i: (i, kv))
    return _pcall(
        body, name=name, grid=(KV_HEADS, t // TM),
        in_specs=[pl.BlockSpec(memory_space=pltpu.SMEM), grp, kvspec, kvspec, grp,
                  pl.BlockSpec((GROUPS, TM, 1), lambda kv, i: (kv, i, 0)), grp],
        out_specs=[grp, kvspec, kvspec, pl.BlockSpec((None, 8, 128), lambda kv, i: (kv, 0, 0))],
        out_shape=[jax.ShapeDtypeStruct((t, ATT_WIDTH), F32), jax.ShapeDtypeStruct((t, KV_WIDTH), F32),
                   jax.ShapeDtypeStruct((t, KV_WIDTH), F32), jax.ShapeDtypeStruct((KV_HEADS, 8, 128), F32)],
        compiler_params=_params("parallel", "arbitrary"),
    )(sink, q, k, v, cat, lse, dcat)


def _segment_pos(t, ctx_len, width):
    row = lax.broadcasted_iota(jnp.int32, (t, width), 0)
    in_ctx = row < ctx_len
    return jnp.where(in_ctx, row, row - ctx_len), jnp.where(in_ctx, ctx_len, t - ctx_len)


def _shifted(x, offset, pos, seg_len):
    t = x.shape[0]
    moved = x if offset == 0 else pltpu.roll(x, (-offset) % t, 0)
    ok = (pos + offset >= 0) & (pos + offset < seg_len)
    return jnp.where(ok, moved, 0.0)


def conv_fwd(proj, conv_w, cat, ctx_len, name):
    t = proj.shape[0]
    base = QKV_WIDTH // 128
    per = AUX_WIDTH // 128

    def body(gb_ref, gc_ref, u_ref, w_ref, cat_in, o_ref):
        del cat_in
        pos, seg = _segment_pos(t, ctx_len, 128)
        z = gc_ref[...] * u_ref[...]
        conv = (w_ref[0:1, :] * _shifted(z, -1, pos, seg) + w_ref[1:2, :] * z
                + w_ref[2:3, :] * _shifted(z, 1, pos, seg))
        o_ref[...] = (gb_ref[...] * conv).astype(BF16)

    def col(off):
        return pl.BlockSpec((t, 128), lambda j: (0, base + off * per + j))

    return _pcall(
        body, name=name, grid=(per,),
        in_specs=[col(0), col(1), col(2), pl.BlockSpec((3, 128), lambda j: (0, j)),
                  pl.BlockSpec(memory_space=pl.ANY)],
        out_specs=pl.BlockSpec((t, 128), lambda j: (0, ATT_WIDTH // 128 + j)),
        out_shape=jax.ShapeDtypeStruct(cat.shape, BF16),
        input_output_aliases={4: 0},
        compiler_params=_params("parallel"),
    )(proj, proj, proj, conv_w, cat)


def conv_bwd(dcat, proj, conv_w, ctx_len, name):
    t = proj.shape[0]
    base = QKV_WIDTH // 128
    per = AUX_WIDTH // 128

    def body(do_ref, gb_ref, gc_ref, u_ref, w_ref, da_ref, dw_ref):
        pos, seg = _segment_pos(t, ctx_len, 128)
        gc = gc_ref[...]
        u = u_ref[...]
        z = gc * u
        zm = _shifted(z, -1, pos, seg)
        zp = _shifted(z, 1, pos, seg)
        conv = w_ref[0:1, :] * zm + w_ref[1:2, :] * z + w_ref[2:3, :] * zp
        dout = do_ref[...].astype(F32)
        da_ref[0] = (dout * conv).astype(BF16)
        dconv = dout * gb_ref[...]
        dw_ref[0:1, :] = jnp.sum(dconv * zm, axis=0, keepdims=True)
        dw_ref[1:2, :] = jnp.sum(dconv * z, axis=0, keepdims=True)
        dw_ref[2:3, :] = jnp.sum(dconv * zp, axis=0, keepdims=True)
        dz = (w_ref[1:2, :] * dconv + w_ref[0:1, :] * _shifted(dconv, 1, pos, seg)
              + w_ref[2:3, :] * _shifted(dconv, -1, pos, seg))
        da_ref[1] = (dz * u).astype(BF16)
        da_ref[2] = (dz * gc).astype(BF16)

    def col(off):
        return pl.BlockSpec((t, 128), lambda j: (0, base + off * per + j))

    return _pcall(
        body, name=name, grid=(per,),
        in_specs=[pl.BlockSpec((t, 128), lambda j: (0, ATT_WIDTH // 128 + j)), col(0), col(1), col(2),
                  pl.BlockSpec((3, 128), lambda j: (0, j))],
        out_specs=[pl.BlockSpec((3, t, 128), lambda j: (0, 0, j)), pl.BlockSpec((3, 128), lambda j: (0, j))],
        out_shape=[jax.ShapeDtypeStruct((3, t, AUX_WIDTH), BF16), jax.ShapeDtypeStruct((3, AUX_WIDTH), F32)],
        compiler_params=_params("parallel"),
    )(dcat, proj, proj, proj, conv_w)


def _pooled(u, w, pos, seg):
    lo = jnp.clip(pos - w // 2, 0, seg)
    hi = jnp.clip(pos - w // 2 + w, 0, seg)
    inv = 1.0 / (hi - lo).astype(F32)
    total = _shifted(u, -(w // 2), pos, seg)
    for o in range(-(w // 2) + 1, w // 2):
        total = total + _shifted(u, o, pos, seg)
    return total * inv - u, inv


def pool_fwd(proj, pool_w, pool_scale, cat, ctx_len, name):
    t = proj.shape[0]

    def body(u_ref, w_ref, sc_ref, cat_in, o_ref):
        del cat_in
        pos, seg = _segment_pos(t, ctx_len, 128)
        for g, w in enumerate(POOL_WINDOWS):
            cols = slice(g * 128, (g + 1) * 128)
            pooled, _ = _pooled(u_ref[:, cols], w, pos, seg)
            mixed = _dot(pooled.astype(BF16), w_ref[g].astype(BF16))
            o_ref[:, cols] = (mixed * sc_ref[:, cols]).astype(BF16)

    return _pcall(
        body, name=name, grid=(1,),
        in_specs=[pl.BlockSpec((t, AUX_WIDTH), lambda j: (0, QKV_WIDTH // AUX_WIDTH)),
                  pl.BlockSpec((AUX_GROUPS, 128, 128), lambda j: (0, 0, 0)),
                  pl.BlockSpec((1, AUX_WIDTH), lambda j: (0, 0)), pl.BlockSpec(memory_space=pl.ANY)],
        out_specs=pl.BlockSpec((t, AUX_WIDTH), lambda j: (0, ATT_WIDTH // AUX_WIDTH)),
        out_shape=jax.ShapeDtypeStruct(cat.shape, BF16),
        input_output_aliases={3: 0},
        compiler_params=_params("arbitrary"),
    )(proj, pool_w, pool_scale, cat)


def pool_bwd(dcat, proj, pool_w, pool_scale, ctx_len, name):
    t = proj.shape[0]

    def body(do_ref, u_ref, w_ref, sc_ref, da_ref, dw_ref, dsc_ref):
        pos, seg = _segment_pos(t, ctx_len, 128)
        for g, w in enumerate(POOL_WINDOWS):
            cols = slice(g * 128, (g + 1) * 128)
            pooled, inv = _pooled(u_ref[:, cols], w, pos, seg)
            pb = pooled.astype(BF16)
            wb = w_ref[g].astype(BF16)
            mixed = _dot(pb, wb)
            dout = do_ref[:, cols].astype(F32)
            dsc_ref[:, cols] = jnp.sum(dout * mixed, axis=0, keepdims=True)
            dmixed = (dout * sc_ref[:, cols]).astype(BF16)
            dw_ref[g] = _dot_tn(pb, dmixed)
            dpooled = _dot_nt(dmixed, wb)
            spread = dpooled * inv
            du = _shifted(spread, w // 2, pos, seg) - dpooled
            for o in range(-(w // 2) + 1, w // 2):
                du = du + _shifted(spread, -o, pos, seg)
            da_ref[0, :, cols] = du.astype(BF16)

    return _pcall(
        body, name=name, grid=(1,),
        in_specs=[pl.BlockSpec((t, AUX_WIDTH), lambda j: (0, ATT_WIDTH // AUX_WIDTH)),
                  pl.BlockSpec((t, AUX_WIDTH), lambda j: (0, QKV_WIDTH // AUX_WIDTH)),
                  pl.BlockSpec((AUX_GROUPS, 128, 128), lambda j: (0, 0, 0)),
                  pl.BlockSpec((1, AUX_WIDTH), lambda j: (0, 0))],
        out_specs=[pl.BlockSpec((1, t, AUX_WIDTH), lambda j: (0, 0, 0)),
                   pl.BlockSpec((AUX_GROUPS, 128, 128), lambda j: (0, 0, 0)),
                   pl.BlockSpec((1, AUX_WIDTH), lambda j: (0, 0))],
        out_shape=[jax.ShapeDtypeStruct((1, t, AUX_WIDTH), BF16), jax.ShapeDtypeStruct((AUX_GROUPS, 128, 128), F32),
                   jax.ShapeDtypeStruct((1, AUX_WIDTH), F32)],
        compiler_params=_params("arbitrary"),
    )(dcat, proj, pool_w, pool_scale)


MOD_NAMES = ("shift1", "scale1", "gate1", "shift2", "scale2", "gate2")


def _layer_fwd(h, lw, weight, mods, tabs, ctx_len, windowed, tag, after_mixer=None):
    cos, sin = tabs
    xn = norm_mod_fwd(h, lw["norm1_g"], mods["shift1"], mods["scale1"], tag + "norm1_fwd")
    lw["w_in"] = weight("w_in", xn)
    proj = mm_nn(xn, lw["w_in"], tag + "w_in_fwd")
    q, k, v = qk_prep_fwd(proj, lw["q_g"], lw["k_g"], cos, sin, tag + "qk_fwd")
    cat, lse = attention_fwd(q, k, v, lw["sink"], ctx_len, windowed, tag + "attn_fwd")
    if windowed:
        cat = pool_fwd(proj, lw["pool_w"], lw["pool_scale"], cat, ctx_len, tag + "pool_fwd")
    else:
        cat = conv_fwd(proj, lw["conv_w"], cat, ctx_len, tag + "conv_fwd")
    lw["w_out"] = weight("w_out", cat)
    gate1 = mods["gate1"] if after_mixer is None else mods["gate1"] + after_mixer(cat)[0, 0]
    y, h1 = mm_nn_residual(cat, lw["w_out"], h, gate1, tag + "w_out_fwd")
    hn = norm_mod_fwd(h1, lw["norm2_g"], mods["shift2"], mods["scale2"], tag + "norm2_fwd")
    lw["w_gate"], lw["w_up"] = weight("w_gate", hn), weight("w_up", hn)
    g, u, act = mm_swiglu(hn, lw["w_gate"], lw["w_up"], tag + "ffn_in_fwd")
    lw["w_down"] = weight("w_down", act)
    f, h2 = mm_nn_residual(act, lw["w_down"], h1, mods["gate2"], tag + "w_down_fwd")
    saved = dict(h=h, xn=xn, proj=proj, q=q, k=k, v=v, cat=cat, lse=lse, y=y, h1=h1, hn=hn, g=g, u=u, act=act, f=f)
    return h2, saved


def _layer_bwd_ffn(dh, sv, lw, mods, tag):
    big, small = {}, {}
    df, dgate2 = gate_bwd(dh, sv["f"], mods["gate2"], tag + "gate2_bwd")
    dgp, du = mm_nt_swiglu_bwd(df, lw["w_down"], sv["g"], sv["u"], tag + "w_down_dgrad")
    big["w_down"] = mm_tn(sv["act"], df, tag + "w_down_wgrad", col_sharded=False)
    dhn = mm_nt_cols(dgp, lw["w_gate"], tag + "w_gate_dgrad")
    dhn = mm_nt_cols(du, lw["w_up"], tag + "w_up_dgrad", add=dhn)
    big["w_gate"] = mm_tn(sv["hn"], dgp, tag + "w_gate_wgrad", col_sharded=True)
    big["w_up"] = mm_tn(sv["hn"], du, tag + "w_up_wgrad", col_sharded=True)
    dh1, dshift2, dscale2, small["norm2_g"] = norm_mod_bwd(dhn, sv["h1"], lw["norm2_g"], mods["scale2"], dh,
                                                           tag + "norm2_bwd")
    return dh1, big, small, (dshift2, dscale2, dgate2)


def _layer_bwd_mixer(dh1, sv, lw, mods, tabs, ctx_len, windowed, tag, between=None):
    cos, sin = tabs
    big, small = {}, {}
    dy, dgate1 = gate_bwd(dh1, sv["y"], mods["gate1"], tag + "gate1_bwd")
    dcat = mm_nt_rows(dy, lw["w_out"], tag + "w_out_dgrad")
    big["w_out"] = mm_tn(sv["cat"], dy, tag + "w_out_wgrad", col_sharded=False)
    if windowed:
        daux, small["pool_w"], small["pool_scale"] = pool_bwd(dcat, sv["proj"], lw["pool_w"], lw["pool_scale"],
                                                              ctx_len, tag + "pool_bwd")
    else:
        daux, small["conv_w"] = conv_bwd(dcat, sv["proj"], lw["conv_w"], ctx_len, tag + "conv_bwd")
    dq, dk, dv, dsink = attention_bwd(sv["q"], sv["k"], sv["v"], sv["cat"], sv["lse"], dcat, lw["sink"], ctx_len,
                                      windowed, tag + "attn_bwd")
    if windowed:
        small["sink"] = dsink[:, :GROUPS, 0].reshape(Q_HEADS)
    q_g = lw["q_g"] if between is None else lw["q_g"] + between(dq)[0, 0]
    dproj, small["q_g"], small["k_g"] = qk_prep_bwd(dq, dk, dv, daux, sv["proj"], q_g, lw["k_g"], cos, sin,
                                                    tag + "qk_bwd")
    dxn = mm_nt_cols(dproj, lw["w_in"], tag + "w_in_dgrad")
    big["w_in"] = mm_tn(sv["xn"], dproj, tag + "w_in_wgrad", col_sharded=True)
    dh0, dshift1, dscale1, small["norm1_g"] = norm_mod_bwd(dxn, sv["h"], lw["norm1_g"], mods["scale1"], dh1,
                                                           tag + "norm1_bwd")
    return dh0, big, small, (dshift1, dscale1, dgate1)


def _dmod_rows(mixer_part, ffn_part):
    return jnp.concatenate([m[:, 0, :] for m in (*mixer_part, *ffn_part)], axis=1)


def _tie(mods, name, token):
    return {**mods, name: mods[name] + token[0, 0]}


def _place():
    x, y, c = lax.axis_index("x"), lax.axis_index("y"), lax.axis_index("c")
    chips = [(1 - x, y), (x, 1 - y), (1 - x, 1 - y)]
    return x, y, c, chips


def _remote(src, dst, send_sem, recv_sem, to):
    return pltpu.make_async_remote_copy(src_ref=src, dst_ref=dst, send_sem=send_sem, recv_sem=recv_sem,
                                        device_id=to, device_id_type=MESH)


ANY = pl.BlockSpec(memory_space=pl.ANY)


def all_gather8(x_shard, name):
    m_per, n = x_shard.shape

    def body(x_ref, out_ref, send_sems, recv_sems, local_sem):
        x, y, c, chips = _place()
        me, sibling = (x, y, c), (x, y, 1 - c)

        def rows(px, py, pc):
            return out_ref.at[pl.ds((4 * px + 2 * py + pc) * m_per, m_per), :]

        def copy(k, block, to, src=None):
            return _remote(rows(*block) if src is None else src, rows(*block), send_sems.at[k], recv_sems.at[k], to)

        mine = pltpu.make_async_copy(x_ref, rows(*me), local_sem)
        mine.start()
        first = [copy(0, me, sibling, src=x_ref)]
        first += [copy(1 + j, me, (*chip, c), src=x_ref) for j, chip in enumerate(chips)]
        for cp in first:
            cp.start()
        passed = [copy(4 + j, (*chip, c), sibling) for j, chip in enumerate(chips)]
        for j, chip in enumerate(chips):
            copy(1 + j, (*chip, c), me).wait_recv()
            passed[j].start()
        copy(0, sibling, me).wait_recv()
        for j, chip in enumerate(chips):
            copy(4 + j, (*chip, 1 - c), me).wait_recv()
        for cp in first + passed:
            cp.wait_send()
        mine.wait()

    return _pcall(
        body, name=name,
        out_shape=jax.ShapeDtypeStruct((N_DEV * m_per, n), x_shard.dtype),
        in_specs=[pl.BlockSpec(memory_space=pltpu.VMEM)],
        out_specs=pl.BlockSpec(memory_space=pltpu.VMEM),
        scratch_shapes=[pltpu.SemaphoreType.DMA((7,)), pltpu.SemaphoreType.DMA((7,)), pltpu.SemaphoreType.DMA],
        compiler_params=pltpu.CompilerParams(vmem_limit_bytes=VMEM_LIMIT),
    )(x_shard)


def _row_tile(rows, largest=256):
    for tr in (1024, 704, 512, 256, 128, 64, 32, 16):
        if tr <= largest and rows % tr == 0:
            return tr
    return rows


def place_shard(w, p, name):
    r, cols = w.shape
    tr = _row_tile(r, 1024)

    def body(p_ref, w_ref, o_ref):
        del p_ref
        o_ref[...] = w_ref[...].astype(BF16)

    return _pcall(
        body, name=name,
        grid_spec=pltpu.PrefetchScalarGridSpec(
            num_scalar_prefetch=1, grid=(r // tr,),
            in_specs=[pl.BlockSpec((tr, cols), lambda i, p_ref: (i, 0))],
            out_specs=pl.BlockSpec((None, tr, cols), lambda i, p_ref: (p_ref[0], i, 0))),
        out_shape=jax.ShapeDtypeStruct((N_SHARD, r, cols), BF16),
        compiler_params=_params("parallel"),
    )(p.reshape(1).astype(jnp.int32), w)


def gather_weight_shards(stacks, after):
    n = len(stacks)

    def body(*refs):
        outs = refs[n + 1:2 * n + 1]
        send_sems, recv_sems = refs[2 * n + 1:]
        x, y, c, chips = _place()
        p = 2 * x + y
        sibling = (x, y, 1 - c)
        started = []
        for i in range(n):
            hr = outs[i].shape[1] // 2
            mine = outs[i].at[p, pl.ds(c * hr, hr)]
            for j, chip in enumerate(chips):
                cp = _remote(mine, mine, send_sems.at[i, j], recv_sems.at[i, j], (*chip, c))
                cp.start()
                started.append(cp)
        for i in range(n):
            hr = outs[i].shape[1] // 2
            for j, (px, py) in enumerate(chips):
                half = outs[i].at[2 * px + py, pl.ds(c * hr, hr)]
                _remote(half, half, send_sems.at[i, j], recv_sems.at[i, j], (px, py, c)).wait_recv()
                fwd = _remote(half, half, send_sems.at[i, 3 + j], recv_sems.at[i, 3 + j], sibling)
                fwd.start()
                started.append(fwd)
        for i in range(n):
            hr = outs[i].shape[1] // 2
            for j, (px, py) in enumerate(chips):
                other = outs[i].at[2 * px + py, pl.ds((1 - c) * hr, hr)]
                _remote(other, other, send_sems.at[i, 3 + j], recv_sems.at[i, 3 + j], sibling).wait_recv()
        for cp in started:
            cp.wait_send()

    return _pcall(
        body, name="gather_weight_shards",
        out_shape=[jax.ShapeDtypeStruct(s.shape, s.dtype) for s in stacks],
        in_specs=[ANY] * (n + 1), out_specs=[ANY] * n,
        input_output_aliases={i: i for i in range(n)},
        scratch_shapes=[pltpu.SemaphoreType.DMA((n, 6)), pltpu.SemaphoreType.DMA((n, 6))],
    )(*stacks, after)


HBM_SPEC = pl.BlockSpec(memory_space=pltpu.HBM)
SEM_SPEC = pl.BlockSpec(memory_space=pltpu.SEMAPHORE)
EFFECT = pltpu.SideEffectType.DATAFLOW_SIDE_EFFECTING


def split_start(name, copies, n_sems, arrays, after):
    m = len(arrays)

    def body(*refs):
        for cp in copies(refs[:m], refs[m + 1], refs[m + 2]):
            cp.start()
        refs[-1][...] = jnp.zeros_like(refs[-1])

    res = _pcall(
        body, name=name,
        out_shape=(pltpu.SemaphoreType.DMA((n_sems,)), pltpu.SemaphoreType.DMA((n_sems,)),
                   *[pltpu.HBM(s.shape, s.dtype) for s in arrays], jax.ShapeDtypeStruct((8, 128), F32)),
        in_specs=[HBM_SPEC] * m + [ANY],
        out_specs=(SEM_SPEC, SEM_SPEC, *[HBM_SPEC] * m, pl.BlockSpec(memory_space=pltpu.VMEM)),
        input_output_aliases={i: 2 + i for i in range(m)},
        compiler_params=pltpu.CompilerParams(has_side_effects=EFFECT),
    )(*[pltpu.with_memory_space_constraint(s, pltpu.HBM) for s in arrays], after)
    return res[0], res[1], list(res[2:2 + m]), res[2 + m]


def split_wait(name, copies, send_sems, recv_sems, arrays, after):
    m = len(arrays)

    def body(*refs):
        for cp in copies(refs[:m], refs[m], refs[m + 1]):
            cp.wait_send()
            cp.wait_recv()

    return _pcall(
        body, name=name,
        out_shape=[pltpu.HBM(s.shape, s.dtype) for s in arrays],
        in_specs=[HBM_SPEC] * m + [SEM_SPEC, SEM_SPEC, ANY],
        out_specs=[HBM_SPEC] * m,
        input_output_aliases={i: i for i in range(m)},
        compiler_params=pltpu.CompilerParams(has_side_effects=EFFECT),
    )(*arrays, send_sems, recv_sems, after)


def gather_copies(n):
    def copies(stacks, send_sems, recv_sems):
        x, y, c, chips = _place()
        p = 2 * x + y
        return [_remote(stacks[i].at[p], stacks[i].at[p], send_sems.at[3 * i + j], recv_sems.at[3 * i + j], (*chip, c))
                for i in range(n) for j, chip in enumerate(chips)]
    return copies


def gather_half_copies(n):
    def copies(stacks, send_sems, recv_sems):
        x, y, c, chips = _place()
        p = 2 * x + y
        out = []
        for i in range(n):
            hr = stacks[i].shape[1] // 2
            mine = stacks[i].at[p, pl.ds(c * hr, hr)]
            out += [_remote(mine, mine, send_sems.at[3 * i + j], recv_sems.at[3 * i + j], (*chip, c))
                    for j, chip in enumerate(chips)]
        return out
    return copies


def forward_half_copies(n):
    def copies(stacks, send_sems, recv_sems):
        x, y, c, chips = _place()
        out = []
        for i in range(n):
            hr = stacks[i].shape[1] // 2
            for j, (px, py) in enumerate(chips):
                half = stacks[i].at[2 * px + py, pl.ds(c * hr, hr)]
                out.append(_remote(half, half, send_sems.at[3 * i + j], recv_sems.at[3 * i + j], (x, y, 1 - c)))
        return out
    return copies


def sibling_half_copies(n):
    def copies(refs, send_sems, recv_sems):
        x, y, c, _ = _place()
        out = []
        for i in range(n):
            hr = refs[n + i].shape[1]
            out.append(_remote(refs[i].at[:, pl.ds((1 - c) * hr, hr), :], refs[n + i], send_sems.at[i], recv_sems.at[i],
                               (x, y, 1 - c)))
        return out
    return copies


def chip_quarter_copies(n):
    def copies(refs, send_sems, recv_sems):
        _, _, c, chips = _place()
        return [_remote(refs[i].at[2 * px + py], refs[n + i].at[j], send_sems.at[3 * i + j], recv_sems.at[3 * i + j],
                        (px, py, c)) for i in range(n) for j, (px, py) in enumerate(chips)]
    return copies


def exchange_final_halves(grads, name):
    n = len(grads)

    def body(*refs):
        outs = refs[n:2 * n]
        send_sems, recv_sems = refs[2 * n:]
        x, y, c, _ = _place()
        sends = []
        for i in range(n):
            hr = outs[i].shape[0] // 2
            mine = outs[i].at[pl.ds(c * hr, hr)]
            cp = _remote(mine, mine, send_sems.at[i], recv_sems.at[i], (x, y, 1 - c))
            cp.start()
            sends.append(cp)
        for i in range(n):
            hr = outs[i].shape[0] // 2
            other = outs[i].at[pl.ds((1 - c) * hr, hr)]
            _remote(other, other, send_sems.at[i], recv_sems.at[i], (x, y, 1 - c)).wait_recv()
        for cp in sends:
            cp.wait_send()

    return _pcall(
        body, name=name,
        out_shape=[jax.ShapeDtypeStruct(g.shape, g.dtype) for g in grads],
        in_specs=[ANY] * n, out_specs=[ANY] * n,
        input_output_aliases={i: i for i in range(n)},
        scratch_shapes=[pltpu.SemaphoreType.DMA((n,)), pltpu.SemaphoreType.DMA((n,))],
    )(*grads)


def add_own_half(g, recv, c, name):
    s, r, cols = g.shape
    hr = r // 2
    tr = _row_tile(hr, 1024)
    nb = hr // tr

    def body(c_ref, g_ref, r_ref, o_ref):
        del c_ref
        o_ref[...] = (g_ref[...].astype(F32) + r_ref[...].astype(F32)).astype(BF16)

    return _pcall(
        body, name=name,
        grid_spec=pltpu.PrefetchScalarGridSpec(
            num_scalar_prefetch=1, grid=(s, nb),
            in_specs=[pl.BlockSpec((None, tr, cols), lambda j, i, c_ref: (j, c_ref[0] * nb + i, 0)),
                      pl.BlockSpec((None, tr, cols), lambda j, i, c_ref: (j, i, 0))],
            out_specs=pl.BlockSpec((None, tr, cols), lambda j, i, c_ref: (j, i, 0))),
        out_shape=jax.ShapeDtypeStruct((s, hr, cols), BF16),
        compiler_params=_params("parallel", "parallel"),
    )(c.reshape(1).astype(jnp.int32), g, recv)


def add_quarters(own, others, p, c, name, deps=()):
    _, hr, cols = own.shape
    tr = _row_tile(hr, 1024)
    nb = hr // tr

    def body(pc_ref, own_ref, oth_ref, *rest):
        o_ref = rest[-1]
        acc = own_ref[...].astype(F32) + oth_ref[0].astype(F32)
        acc = acc + oth_ref[1].astype(F32)
        o_ref[...] = acc + oth_ref[2].astype(F32)

    return _pcall(
        body, name=name,
        grid_spec=pltpu.PrefetchScalarGridSpec(
            num_scalar_prefetch=1, grid=(nb,),
            in_specs=[pl.BlockSpec((None, tr, cols), lambda i, pc: (pc[0], i, 0)),
                      pl.BlockSpec((3, tr, cols), lambda i, pc: (0, i, 0))] + [ANY] * len(deps),
            out_specs=pl.BlockSpec((tr, cols), lambda i, pc: (pc[1] * nb + i, 0))),
        out_shape=jax.ShapeDtypeStruct((2 * hr, cols), F32),
        compiler_params=_params("parallel"),
    )(jnp.stack([p, c]).astype(jnp.int32), own, others, *deps)


def _adamw_math(w, g, m, v):
    m2 = ADAM_B1 * m + (1.0 - ADAM_B1) * g
    v2 = ADAM_B2 * v + (1.0 - ADAM_B2) * (g * g)
    m_hat = m2 / (1.0 - ADAM_B1 ** ADAM_STEP)
    v_hat = v2 / (1.0 - ADAM_B2 ** ADAM_STEP)
    delta = -ADAM_LR * (m_hat / (jnp.sqrt(v_hat) + ADAM_EPS) + ADAM_WD * w)
    return delta, m2, v2


def adamw(w, g, m, v, name):
    r, cols = w.shape
    tr = _row_tile(r)

    def body(w_ref, g_ref, m_ref, v_ref, go_ref, d_ref, m2_ref, v2_ref):
        g = g_ref[...]
        go_ref[...] = g
        d_ref[...], m2_ref[...], v2_ref[...] = _adamw_math(w_ref[...], g, m_ref[...], v_ref[...])

    blk = pl.BlockSpec((tr, cols), lambda i: (i, 0))
    return _pcall(
        body, name=name, grid=(r // tr,),
        in_specs=[blk] * 4, out_specs=[blk] * 4,
        out_shape=[jax.ShapeDtypeStruct((r, cols), F32)] * 4,
        compiler_params=_params("parallel"),
    )(w, g, m, v)


def small_update(gathered, w, m, v):
    nd, r, lanes = gathered.shape

    def body(ga_ref, w_ref, m_ref, v_ref, g_ref, d_ref, m2_ref, v2_ref):
        g = ga_ref[0] + ga_ref[1]
        for dev in range(2, nd):
            g = g + ga_ref[dev]
        g_ref[...] = g
        d_ref[...], m2_ref[...], v2_ref[...] = _adamw_math(w_ref[...], g, m_ref[...], v_ref[...])

    return _pcall(
        body, name="small_update",
        out_shape=[jax.ShapeDtypeStruct((r, lanes), F32)] * 4,
        compiler_params=pltpu.CompilerParams(vmem_limit_bytes=VMEM_LIMIT),
    )(gathered, w, m, v)


def mod_fwd(c16, w_mod, b_mod, name):
    d, ns = w_mod.shape
    tn = 512

    def body(c_ref, w_ref, b_ref, o_ref):
        cv = c_ref[...]
        sc = (cv * _sigmoid(cv)).astype(BF16)
        o_ref[...] = _dot(sc, w_ref[...].astype(BF16)) + b_ref[...]

    return _pcall(
        body, name=name, grid=(ns // tn,),
        in_specs=[pl.BlockSpec((16, d), lambda j: (0, 0)), pl.BlockSpec((d, tn), lambda j: (0, j)),
                  pl.BlockSpec((1, tn), lambda j: (0, j))],
        out_specs=pl.BlockSpec((16, tn), lambda j: (0, j)),
        out_shape=jax.ShapeDtypeStruct((16, ns), F32),
        compiler_params=_params("parallel"),
    )(c16, w_mod, b_mod)


def wmod_update(c16, dmod16, w, m, v, name):
    d, ns = w.shape
    tn = 256

    def body(c_ref, dm_ref, w_ref, m_ref, v_ref, g_ref, d_ref, m2_ref, v2_ref):
        cv = c_ref[...]
        sc = (cv * _sigmoid(cv)).astype(BF16)
        g = _dot_tn(sc, dm_ref[...].astype(BF16))
        g_ref[...] = g
        d_ref[...], m2_ref[...], v2_ref[...] = _adamw_math(w_ref[...], g, m_ref[...], v_ref[...])

    blk = pl.BlockSpec((d, tn), lambda j: (0, j))
    return _pcall(
        body, name=name, grid=(ns // tn,),
        in_specs=[pl.BlockSpec((16, d), lambda j: (0, 0)), pl.BlockSpec((16, tn), lambda j: (0, j)), blk, blk, blk],
        out_specs=[blk] * 4,
        out_shape=[jax.ShapeDtypeStruct((d, ns), F32)] * 4,
        compiler_params=_params("parallel"),
    )(c16, dmod16, w, m, v)


def cctx_partial(dm0, w0, dm1, w1):
    d, ns = w0.shape
    tn = 512

    def body(dm0_ref, w0_ref, dm1_ref, w1_ref, o_ref):
        @pl.when(pl.program_id(0) == 0)
        def _():
            o_ref[...] = jnp.zeros_like(o_ref)

        for dm_ref, w_ref in ((dm0_ref, w0_ref), (dm1_ref, w1_ref)):
            tot = jnp.sum(dm_ref[...], axis=0, keepdims=True)
            lhs = jnp.broadcast_to(tot, (8, tn)).astype(BF16)
            o_ref[...] += _dot_nt(lhs, w_ref[...].astype(BF16))

    dspec = pl.BlockSpec((8, tn), lambda j: (0, j))
    wspec = pl.BlockSpec((d, tn), lambda j: (0, j))
    return _pcall(
        body, name="cctx_partial", grid=(ns // tn,),
        in_specs=[dspec, wspec, dspec, wspec],
        out_specs=pl.BlockSpec((8, d), lambda j: (0, 0)),
        out_shape=jax.ShapeDtypeStruct((8, d), F32),
        compiler_params=_params("arbitrary"),
    )(dm0, w0, dm1, w1)


def cctx_update(parts, c_ctx, m, v):
    d = c_ctx.shape[1]

    def body(p_ref, c_ref, m_ref, v_ref, g_ref, d_ref, m2_ref, v2_ref):
        tot = p_ref[0:1, :] + p_ref[1:2, :]
        tot = tot + p_ref[2:3, :]
        tot = tot + p_ref[3:4, :]
        cv = c_ref[...]
        sg = _sigmoid(cv)
        g = tot * (sg * (1.0 + cv * (1.0 - sg)))
        g_ref[...] = g
        d_ref[...], m2_ref[...], v2_ref[...] = _adamw_math(cv, g, m_ref[...], v_ref[...])

    return _pcall(
        body, name="cctx_update",
        out_shape=[jax.ShapeDtypeStruct((1, d), F32)] * 4,
    )(parts, c_ctx, m, v)


WEIGHT_NAMES = (
    "c_ctx", "l0_norm1_g", "l0_w_mod", "l0_b_mod", "l0_w_in", "l0_q_norm_g", "l0_k_norm_g", "l0_conv_w", "l0_w_out",
    "l0_norm2_g", "l0_w_gate", "l0_w_up", "l0_w_down", "l1_norm1_g", "l1_w_mod", "l1_b_mod", "l1_w_in",
    "l1_q_norm_g", "l1_k_norm_g", "l1_sink", "l1_pool_w", "l1_pool_scale", "l1_w_out", "l1_norm2_g", "l1_w_gate",
    "l1_w_up", "l1_w_down", "final_norm_g")
BIG_NAMES = ("w_in", "w_out", "w_gate", "w_up", "w_down")
SMALL_SLOTS = tuple(
    [(f"l{l}_{nm}", par and f"l{l}_{nm}") for l in (0, 1)
     for nm, par in (("dmod_lat", False), ("dmod_ctx", False), ("b_mod", True), ("norm1_g", True), ("norm2_g", True),
                     ("q_norm_g", True), ("k_norm_g", True))]
    + [("l0_conv_w", None), ("l1_sink", "l1_sink"), ("l1_pool_w", "l1_pool_w"), ("l1_pool_scale", "l1_pool_scale"),
       ("final_norm_g", "final_norm_g")])


def _pack(values, sizes):
    parts = []
    for (name, _), size in zip(SMALL_SLOTS, sizes):
        v = values.get(name)
        padded = -(-size // 128) * 128
        v = jnp.zeros((padded,), F32) if v is None else jnp.pad(v.reshape(-1).astype(F32), (0, padded - size))
        parts.append(v)
    total = sum(p.shape[0] for p in parts)
    parts.append(jnp.zeros((-(-total // 1024) * 1024 - total,), F32))
    return jnp.concatenate(parts).reshape(-1, 128)


def _offsets(sizes):
    offs, o = {}, 0
    for (name, _), size in zip(SMALL_SLOTS, sizes):
        offs[name] = (o, size)
        o += -(-size // 128) * 128
    return offs


def kernel(x, c, ctx, c_ctx, l0_norm1_g, l0_w_mod, l0_b_mod, l0_w_in, l0_q_norm_g, l0_k_norm_g, l0_conv_w, l0_w_out, l0_norm2_g, l0_w_gate, l0_w_up, l0_w_down, l1_norm1_g, l1_w_mod, l1_b_mod, l1_w_in, l1_q_norm_g, l1_k_norm_g, l1_sink, l1_pool_w, l1_pool_scale, l1_w_out, l1_norm2_g, l1_w_gate, l1_w_up, l1_w_down, final_norm_g, loss_target, m_c_ctx, m_l0_norm1_g, m_l0_w_mod, m_l0_b_mod, m_l0_w_in, m_l0_q_norm_g, m_l0_k_norm_g, m_l0_conv_w, m_l0_w_out, m_l0_norm2_g, m_l0_w_gate, m_l0_w_up, m_l0_w_down, m_l1_norm1_g, m_l1_w_mod, m_l1_b_mod, m_l1_w_in, m_l1_q_norm_g, m_l1_k_norm_g, m_l1_sink, m_l1_pool_w, m_l1_pool_scale, m_l1_w_out, m_l1_norm2_g, m_l1_w_gate, m_l1_w_up, m_l1_w_down, m_final_norm_g, v_c_ctx, v_l0_norm1_g, v_l0_w_mod, v_l0_b_mod, v_l0_w_in, v_l0_q_norm_g, v_l0_k_norm_g, v_l0_conv_w, v_l0_w_out, v_l0_norm2_g, v_l0_w_gate, v_l0_w_up, v_l0_w_down, v_l1_norm1_g, v_l1_w_mod, v_l1_b_mod, v_l1_w_in, v_l1_q_norm_g, v_l1_k_norm_g, v_l1_sink, v_l1_pool_w, v_l1_pool_scale, v_l1_w_out, v_l1_norm2_g, v_l1_w_gate, v_l1_w_up, v_l1_w_down, v_final_norm_g):
    a = dict(locals())
    xi, yi, ci = lax.axis_index("x"), lax.axis_index("y"), lax.axis_index("c")
    p = 2 * xi + yi
    me = 4 * xi + 2 * yi + ci
    d = x.shape[-1]
    conv_cols = l0_conv_w.shape[1]

    row0 = jnp.concatenate([c, jnp.pad(l0_conv_w, ((0, 0), (0, d - conv_cols))), jnp.zeros((4, d), F32)], axis=0)
    gathered = all_gather8(row0, "gather_cond").reshape(N_DEV, 8, d)
    c_all = gathered[:, 0]
    conv_w = gathered[0::2, 1:4, :conv_cols].transpose(1, 0, 2).reshape(3, N_SHARD * conv_cols)
    c16_fwd = jnp.concatenate([c_all, c_ctx[None], jnp.zeros((7, d), F32)], axis=0)
    c16_bwd = jnp.concatenate([c_all, jnp.broadcast_to(c_ctx[None], (8, d))], axis=0)

    ns_mod = l0_w_mod.shape[1]
    mod_parts = [mod_fwd(c16_fwd, a[f"l{l}_w_mod"], lax.dynamic_slice(a[f"l{l}_b_mod"], (p * ns_mod,), (ns_mod,))[None],
                         f"l{l}_mod_fwd") for l in (0, 1)]
    modg = all_gather8(jnp.concatenate(mod_parts, axis=0), "gather_mod").reshape(N_DEV, 2, 16, ns_mod)[0::2]
    mod_full = modg.transpose(1, 2, 0, 3).reshape(2, 16, N_SHARD * ns_mod)
    mods = []
    for l in (0, 1):
        lat = lax.dynamic_index_in_dim(mod_full[l], me, axis=0, keepdims=False)
        cx = mod_full[l, 8]
        mods.append({nm: jnp.stack([cx[j * d:(j + 1) * d], lat[j * d:(j + 1) * d]])[:, None, :]
                     for j, nm in enumerate(MOD_NAMES)})

    placed = {(l, nm): place_shard(a[f"l{l}_{nm}"], p, f"l{l}_{nm}_place") for l in (0, 1) for nm in BIG_NAMES}
    ready = {(0, "w_in"): gather_weight_shards([placed[0, "w_in"]], mod_full)[0]}
    groups = (((0, "w_out"), (0, "w_gate"), (0, "w_up")), ((0, "w_down"),), ((1, "w_in"), (1, "w_out")),
              ((1, "w_gate"), (1, "w_up"), (1, "w_down")))
    staged = len(groups) - 1
    flights, after = [], ready[0, "w_in"]
    for gi, grp in enumerate(groups):
        first_copies = gather_half_copies if gi == staged else gather_copies
        flights.append(split_start(f"gather_start_{gi}", first_copies(len(grp)), 3 * len(grp),
                                   [placed[k] for k in grp], after))
        after = flights[-1][3]

    def forward_staged(after):
        n = len(groups[staged])
        send_sems, recv_sems, in_flight, _ = flights[staged]
        landed = split_wait(f"gather_wait_{staged}", gather_half_copies(n), send_sems, recv_sems, in_flight, after)
        flights[staged] = split_start("gather_forward_start", forward_half_copies(n), 3 * n, landed, after)
        return flights[staged][3]

    def weight_of(l):
        def weight(nm, after):
            if (l, nm) not in ready:
                gi = next(i for i, grp in enumerate(groups) if (l, nm) in grp)
                send_sems, recv_sems, in_flight, _ = flights[gi]
                name, copies = ("gather_forward_wait", forward_half_copies) if gi == staged else (
                    f"gather_wait_{gi}", gather_copies)
                landed = split_wait(name, copies(len(groups[gi])), send_sems, recv_sems, in_flight, after)
                ready.update(zip(groups[gi], landed))
            return ready[l, nm]
        return weight

    layers = [dict(norm1_g=a[f"l{l}_norm1_g"][None], norm2_g=a[f"l{l}_norm2_g"][None],
                   q_g=a[f"l{l}_q_norm_g"][None], k_g=a[f"l{l}_k_norm_g"][None]) for l in (0, 1)]
    layers[0].update(conv_w=conv_w, sink=jnp.zeros((Q_HEADS,), F32))
    layers[1].update(sink=l1_sink, pool_w=l1_pool_w, pool_scale=l1_pool_scale[None])

    ctx_len = ctx.shape[1]
    h = jnp.concatenate([ctx[0], x[0]], axis=0)
    tabs = _rope_tables(h.shape[0], ctx_len)
    h, saved0 = _layer_fwd(h, layers[0], weight_of(0), _tie(mods[0], "shift1", flights[-1][3]), tabs, ctx_len, False,
                           "l0_")
    h, saved1 = _layer_fwd(h, layers[1], weight_of(1), mods[1], tabs, ctx_len, True, "l1_", forward_staged)
    loss_tile, dh, dfg = final_loss(h, final_norm_g[None], loss_target[0])

    keys = [[f"l{l}_{nm}" for nm in BIG_NAMES] for l in (0, 1)]
    n_big = len(BIG_NAMES)
    dh, ffn_big, ffn_small, ffn_mod = _layer_bwd_ffn(dh, saved1, layers[1], mods[1], "l1_")
    dh, mix_big, mix_small, mix_mod = _layer_bwd_mixer(dh, saved1, layers[1], mods[1], tabs, ctx_len, True, "l1_")
    bigs, smalls, dmods = [None, {**ffn_big, **mix_big}], [None, {**ffn_small, **mix_small}], [None, None]
    dmods[1] = _dmod_rows(mix_mod, ffn_mod)
    grads1 = [bigs[1][nm] for nm in BIG_NAMES]
    lands = [lax.empty((g.shape[0], g.shape[1] // 2, g.shape[2]), g.dtype) for g in grads1]
    send_a, recv_a, fly_a, token_a = split_start("l1_sibling_halves_start", sibling_half_copies(n_big), n_big,
                                                 grads1 + lands, dh)
    dh, ffn_big, ffn_small, ffn_mod = _layer_bwd_ffn(dh, saved0, layers[0], _tie(mods[0], "gate2", token_a), "l0_")
    fly_a = split_wait("l1_sibling_halves_wait", sibling_half_copies(n_big), send_a, recv_a, fly_a, dh)
    chip_sums1 = [add_own_half(g, r, ci, k + "_add_halves")
                  for k, g, r in zip(keys[1], fly_a[:n_big], fly_a[n_big:])]
    lands = [lax.empty((3,) + s.shape[1:], s.dtype) for s in chip_sums1]
    send_b, recv_b, fly_b, token_b = split_start("l1_chip_quarters_start", chip_quarter_copies(n_big), 3 * n_big,
                                                 chip_sums1 + lands, dh)
    ffn_names = ("w_gate", "w_up", "w_down")
    n_ffn = len(ffn_names)
    ffn_keys = ["l0_" + nm for nm in ffn_names]
    grads0f = [ffn_big[nm] for nm in ffn_names]
    lands = [lax.empty((g.shape[0], g.shape[1] // 2, g.shape[2]), g.dtype) for g in grads0f]
    send_c, recv_c, fly_c, token_c = split_start("l0_ffn_sibling_halves_start", sibling_half_copies(n_ffn), n_ffn,
                                                 grads0f + lands, token_b)
    ffn_flight = []

    def between(dq):
        landed = split_wait("l0_ffn_sibling_halves_wait", sibling_half_copies(n_ffn), send_c, recv_c, fly_c, dq)
        sums = [add_own_half(g, r, ci, k + "_add_halves") for k, g, r in zip(ffn_keys, landed[:n_ffn], landed[n_ffn:])]
        slots = [lax.empty((3,) + s.shape[1:], s.dtype) for s in sums]
        ffn_flight.extend(split_start("l0_ffn_chip_quarters_start", chip_quarter_copies(n_ffn), 3 * n_ffn,
                                      sums + slots, dq))
        return ffn_flight[3]

    dh, mix_big, mix_small, mix_mod = _layer_bwd_mixer(dh, saved0, layers[0], _tie(mods[0], "gate1", token_c), tabs,
                                                       ctx_len, False, "l0_", between)
    fly_b = split_wait("l1_chip_quarters_wait", chip_quarter_copies(n_big), send_b, recv_b, fly_b, dh)
    fly_d = split_wait("l0_ffn_chip_quarters_wait", chip_quarter_copies(n_ffn), ffn_flight[0], ffn_flight[1],
                       ffn_flight[2], dh)
    bigs[0], smalls[0], dmods[0] = {**ffn_big, **mix_big}, {**ffn_small, **mix_small}, _dmod_rows(mix_mod, ffn_mod)
    dx = dh[ctx_len:]
    chip_sums = dict(zip(keys[1] + ffn_keys, list(fly_b[:n_big]) + list(fly_d[:n_ffn])))
    quarters = dict(zip(keys[1] + ffn_keys, list(fly_b[n_big:]) + list(fly_d[n_ffn:])))

    partial = {"l0_conv_w": smalls[0]["conv_w"], "l1_sink": smalls[1]["sink"], "l1_pool_w": smalls[1]["pool_w"],
               "l1_pool_scale": smalls[1]["pool_scale"], "final_norm_g": dfg}
    for l in (0, 1):
        partial.update({f"l{l}_dmod_lat": dmods[l][1], f"l{l}_dmod_ctx": dmods[l][0],
                        f"l{l}_b_mod": dmods[l][0] + dmods[l][1], f"l{l}_norm1_g": smalls[l]["norm1_g"],
                        f"l{l}_norm2_g": smalls[l]["norm2_g"], f"l{l}_q_norm_g": smalls[l]["q_g"],
                        f"l{l}_k_norm_g": smalls[l]["k_g"]})
    sizes = [int(np.prod(partial[name].shape)) for name, _ in SMALL_SLOTS]
    offs = _offsets(sizes)

    last_keys = ["l0_w_in", "l0_w_out"]
    grads_last = [bigs[0]["w_in"], bigs[0]["w_out"]]
    lands = [lax.empty((g.shape[0], g.shape[1] // 2, g.shape[2]), g.dtype) for g in grads_last]
    send_e, recv_e, fly_e, token_e = split_start("l0_last_sibling_halves_start", sibling_half_copies(2), 2,
                                                 grads_last + lands, dh)
    gpack = _pack(partial, sizes) + token_e[0:1, 0:1]
    rows = gpack.shape[0]
    small_all = all_gather8(gpack, "gather_small").reshape(N_DEV, rows, 128)
    packs = [_pack({name: a[pre + par] for name, par in SMALL_SLOTS if par}, sizes) for pre in ("", "m_", "v_")]
    small_out = [t.reshape(-1) for t in small_update(small_all, *packs)]
    flat_all = small_all.reshape(N_DEV, rows * 128)

    def slot(flat, name, shape):
        o, size = offs[name]
        return flat[o:o + size].reshape(shape)

    results = {}
    for name, par in SMALL_SLOTS:
        if par:
            results[par] = tuple(slot(t, name, a[par].shape) for t in small_out)

    ctx_rows = []
    for l in (0, 1):
        o_lat, o_ctx = offs[f"l{l}_dmod_lat"][0], offs[f"l{l}_dmod_ctx"][0]
        lat = lax.dynamic_slice(flat_all, (0, o_lat + p * ns_mod), (N_DEV, ns_mod))
        cxr = lax.dynamic_slice(flat_all, (0, o_ctx + p * ns_mod), (N_DEV, ns_mod))
        ctx_rows.append(cxr)
        results[f"l{l}_w_mod"] = tuple(wmod_update(c16_bwd, jnp.concatenate([lat, cxr], axis=0), a[f"l{l}_w_mod"],
                                                   a[f"m_l{l}_w_mod"], a[f"v_l{l}_w_mod"], f"l{l}_w_mod_update"))
    part = cctx_partial(ctx_rows[0], l0_w_mod, ctx_rows[1], l1_w_mod)
    parts4 = all_gather8(part, "gather_cctx").reshape(N_DEV, 8, d)[0::2, 0]
    results["c_ctx"] = tuple(t[0] for t in cctx_update(parts4, c_ctx[None], m_c_ctx[None], v_c_ctx[None]))

    gconv = lax.dynamic_slice(slot(small_out[0], "l0_conv_w", (3, N_SHARD * conv_cols)), (0, p * conv_cols),
                              (3, conv_cols))
    results["l0_conv_w"] = tuple(adamw(l0_conv_w, gconv, m_l0_conv_w, v_l0_conv_w, "l0_conv_w_adamw"))

    fly_e = split_wait("l0_last_sibling_halves_wait", sibling_half_copies(2), send_e, recv_e, fly_e,
                       results["c_ctx"][0])
    sums_last = [add_own_half(g, r, ci, k + "_add_halves") for k, g, r in zip(last_keys, fly_e[:2], fly_e[2:])]
    slots = [lax.empty((3,) + s.shape[1:], s.dtype) for s in sums_last]
    send_f, recv_f, fly_f, token_f = split_start("l0_last_chip_quarters_start", chip_quarter_copies(2), 6,
                                                 sums_last + slots, sums_last[0])

    def finish(names, tag, deps):
        halves = [add_quarters(chip_sums[k], quarters[k], p, ci, k + "_add_quarters", deps) for k in names]
        for k, g in zip(names, exchange_final_halves(halves, "exchange_final_halves_" + tag)):
            results[k] = tuple(adamw(a[k], g, a["m_" + k], a["v_" + k], k + "_adamw"))

    done_keys = keys[1] + ffn_keys
    finish(done_keys, "first", (token_f,))
    fly_f = split_wait("l0_last_chip_quarters_wait", chip_quarter_copies(2), send_f, recv_f, fly_f,
                       results[done_keys[-1]][1])
    chip_sums.update(zip(last_keys, fly_f[:2]))
    quarters.update(zip(last_keys, fly_f[2:]))
    finish(last_keys, "last", ())

    loss = lax.psum(loss_tile[0, 0], ("x", "y", "c"))
    out = [loss, dx[None]]
    for j in range(4):
        out += [results[k][j] for k in WEIGHT_NAMES]
    return tuple(out)
```

```python
import numpy as np
import jax
import jax.numpy as jnp
from jax import lax
from jax.experimental import pallas as pl
from jax.experimental.pallas import tpu as pltpu

F32 = jnp.float32
BF16 = jnp.bfloat16
MESH = pl.DeviceIdType.MESH

HEAD_DIM = 128
Q_HEADS = 12
KV_HEADS = 4
GROUPS = Q_HEADS // KV_HEADS
ATT_WIDTH = Q_HEADS * HEAD_DIM
KV_WIDTH = KV_HEADS * HEAD_DIM
QKV_WIDTH = ATT_WIDTH + 2 * KV_WIDTH
AUX_WIDTH = 512
AUX_GROUPS = 4
POOL_WINDOWS = (2, 4, 8, 16)
WINDOW = 128
GRID_W = 64
ROPE_THETA = 10000.0
N_MOD = 6
EPS = 1e-6
NEG_INF = -1e30
ATT_SCALE = HEAD_DIM ** -0.5
N_SHARD = 4
N_DEV = 8

ADAM_LR = 0.001
ADAM_B1 = 0.9
ADAM_B2 = 0.999
ADAM_EPS = 1e-08
ADAM_WD = 0.01
ADAM_STEP = 10

TM = 256
TM_MATMUL = 544
BAND = TM + 2 * WINDOW
VMEM_LIMIT = 56 * 1024 * 1024

NT_DIMS = (((1,), (1,)), ((), ()))
TN_DIMS = (((0,), (0,)), ((), ()))


def _pcall(body, **kw):
    return pl.pallas_call(body, **kw)


def _params(*sem):
    return pltpu.CompilerParams(dimension_semantics=sem, vmem_limit_bytes=VMEM_LIMIT)


def _sel(i):
    return jnp.minimum(i, 1)


def _sel_spec(d):
    return pl.BlockSpec((None, 1, d), lambda i: (_sel(i), 0, 0))


def _dot(a, b):
    return jnp.dot(a, b, preferred_element_type=F32)


def _dot_nt(a, b):
    return lax.dot_general(a, b, NT_DIMS, preferred_element_type=F32)


def _dot_tn(a, b):
    return lax.dot_general(a, b, TN_DIMS, preferred_element_type=F32)


def _sigmoid(x):
    return 0.5 * jnp.tanh(0.5 * x) + 0.5


def norm_mod_fwd(h, g, shift, scale, name):
    t, d = h.shape

    def body(h_ref, g_ref, sh_ref, sc_ref, o_ref):
        x = h_ref[...]
        r = lax.rsqrt(jnp.mean(x * x, axis=-1, keepdims=True) + EPS)
        y = x * r * g_ref[...]
        o_ref[...] = (y * (1.0 + sc_ref[...]) + sh_ref[...]).astype(BF16)

    return _pcall(
        body, name=name, grid=(t // TM,),
        in_specs=[pl.BlockSpec((TM, d), lambda i: (i, 0)), pl.BlockSpec((1, d), lambda i: (0, 0)),
                  _sel_spec(d), _sel_spec(d)],
        out_specs=pl.BlockSpec((TM, d), lambda i: (i, 0)),
        out_shape=jax.ShapeDtypeStruct((t, d), BF16),
        compiler_params=_params("parallel"),
    )(h, g, shift, scale)


def norm_mod_bwd(dy, h, g, scale, dres, name):
    t, d = h.shape

    def body(dy_ref, h_ref, g_ref, sc_ref, dres_ref, dh_ref, dsh_ref, dsc_ref, dg_ref):
        i = pl.program_id(0)

        @pl.when(i == 0)
        def _():
            dsh_ref[...] = jnp.zeros_like(dsh_ref)
            dsc_ref[...] = jnp.zeros_like(dsc_ref)
            dg_ref[...] = jnp.zeros_like(dg_ref)

        x = h_ref[...]
        gv = g_ref[...]
        r = lax.rsqrt(jnp.mean(x * x, axis=-1, keepdims=True) + EPS)
        xhat = x * r
        dyv = dy_ref[...]
        s = _sel(i)
        dsh_ref[pl.ds(s, 1)] += jnp.sum(dyv, axis=0, keepdims=True)[None]
        dsc_ref[pl.ds(s, 1)] += jnp.sum(dyv * xhat * gv, axis=0, keepdims=True)[None]
        tt = dyv * (1.0 + sc_ref[...])
        dg_ref[...] += jnp.sum(tt * xhat, axis=0, keepdims=True)
        dxhat = tt * gv
        dx = r * (dxhat - xhat * jnp.mean(dxhat * xhat, axis=-1, keepdims=True))
        dh_ref[...] = dx + dres_ref[...]

    row = pl.BlockSpec((TM, d), lambda i: (i, 0))
    acc2 = pl.BlockSpec((2, 1, d), lambda i: (0, 0, 0))
    return _pcall(
        body, name=name, grid=(t // TM,),
        in_specs=[row, row, pl.BlockSpec((1, d), lambda i: (0, 0)), _sel_spec(d), row],
        out_specs=[row, acc2, acc2, pl.BlockSpec((1, d), lambda i: (0, 0))],
        out_shape=[jax.ShapeDtypeStruct((t, d), F32), jax.ShapeDtypeStruct((2, 1, d), F32),
                   jax.ShapeDtypeStruct((2, 1, d), F32), jax.ShapeDtypeStruct((1, d), F32)],
        compiler_params=_params("arbitrary"),
    )(dy, h, g, scale, dres)


def gate_bwd(dh, f, gate, name):
    t, d = dh.shape

    def body(dh_ref, f_ref, gt_ref, df_ref, dgt_ref):
        i = pl.program_id(0)

        @pl.when(i == 0)
        def _():
            dgt_ref[...] = jnp.zeros_like(dgt_ref)

        dhv = dh_ref[...]
        df_ref[...] = (dhv * gt_ref[...]).astype(BF16)
        dgt_ref[pl.ds(_sel(i), 1)] += jnp.sum(dhv * f_ref[...].astype(F32), axis=0, keepdims=True)[None]

    row = pl.BlockSpec((TM, d), lambda i: (i, 0))
    return _pcall(
        body, name=name, grid=(t // TM,),
        in_specs=[row, row, _sel_spec(d)],
        out_specs=[row, pl.BlockSpec((2, 1, d), lambda i: (0, 0, 0))],
        out_shape=[jax.ShapeDtypeStruct((t, d), BF16), jax.ShapeDtypeStruct((2, 1, d), F32)],
        compiler_params=_params("arbitrary"),
    )(dh, f, gate)


def final_loss(h, fg, target):
    t, d = h.shape

    def body(h_ref, g_ref, tg_ref, loss_ref, dh_ref, dg_ref):
        i = pl.program_id(0)

        @pl.when(i == 0)
        def _():
            loss_ref[...] = jnp.zeros_like(loss_ref)
            dg_ref[...] = jnp.zeros_like(dg_ref)
            dh_ref[...] = jnp.zeros_like(dh_ref)

        @pl.when(i > 0)
        def _():
            x = h_ref[...]
            gv = g_ref[...]
            r = lax.rsqrt(jnp.mean(x * x, axis=-1, keepdims=True) + EPS)
            xhat = x * r
            diff = xhat * gv - tg_ref[...]
            loss_ref[...] += 0.5 * jnp.sum(jnp.mean(diff * diff, axis=-1, keepdims=True), axis=0, keepdims=True)
            dout = diff * (1.0 / d)
            dg_ref[...] += jnp.sum(dout * xhat, axis=0, keepdims=True)
            dxhat = dout * gv
            dh_ref[...] = r * (dxhat - xhat * jnp.mean(dxhat * xhat, axis=-1, keepdims=True))

    row = pl.BlockSpec((TM, d), lambda i: (i, 0))
    return _pcall(
        body, name="final_loss", grid=(t // TM,),
        in_specs=[row, pl.BlockSpec((1, d), lambda i: (0, 0)),
                  pl.BlockSpec((TM, d), lambda i: (jnp.maximum(i - 1, 0), 0))],
        out_specs=[pl.BlockSpec((8, 128), lambda i: (0, 0)), row, pl.BlockSpec((1, d), lambda i: (0, 0))],
        out_shape=[jax.ShapeDtypeStruct((8, 128), F32), jax.ShapeDtypeStruct((t, d), F32),
                   jax.ShapeDtypeStruct((1, d), F32)],
        compiler_params=_params("arbitrary"),
    )(h, fg, target)


def _matmul_rows(t):
    return TM_MATMUL if t % TM_MATMUL == 0 else TM


def _w_nn_spec(w, k):
    if w.shape[1] == k:
        ns = w.shape[2]
        return w, pl.BlockSpec((None, k, ns), lambda j, i: (j, 0, 0)), w.shape[0], ns
    w2 = w.reshape(w.shape[0] * w.shape[1], w.shape[2])
    tn = 1024 if w2.shape[1] % 1024 == 0 else w2.shape[1]
    return w2, pl.BlockSpec((k, tn), lambda j, i: (0, j)), w2.shape[1] // tn, tn


def mm_nn(a, w, name, out_dtype=F32):
    t, k = a.shape
    w, wspec, nb, tn = _w_nn_spec(w, k)
    tm = _matmul_rows(t)

    def body(a_ref, w_ref, o_ref):
        o_ref[...] = _dot(a_ref[...], w_ref[...]).astype(o_ref.dtype)

    return _pcall(
        body, name=name, grid=(nb, t // tm),
        in_specs=[pl.BlockSpec((tm, k), lambda j, i: (i, 0)), wspec],
        out_specs=pl.BlockSpec((tm, tn), lambda j, i: (i, j)),
        out_shape=jax.ShapeDtypeStruct((t, nb * tn), out_dtype),
        compiler_params=_params("parallel", "parallel"),
    )(a, w)


def mm_nn_residual(a, w, h, gate, name):
    t, k = a.shape
    w, wspec, nb, tn = _w_nn_spec(w, k)
    tm = _matmul_rows(t)

    def body(a_ref, w_ref, h_ref, gt_ref, y_ref, o_ref):
        y = _dot(a_ref[...], w_ref[...])
        y_ref[...] = y.astype(BF16)
        row = pl.program_id(1) * tm + lax.broadcasted_iota(jnp.int32, (tm, tn), 0)
        o_ref[...] = h_ref[...] + jnp.where(row < TM, gt_ref[0], gt_ref[1]) * y

    out = pl.BlockSpec((tm, tn), lambda j, i: (i, j))
    return _pcall(
        body, name=name, grid=(nb, t // tm),
        in_specs=[pl.BlockSpec((tm, k), lambda j, i: (i, 0)), wspec, out,
                  pl.BlockSpec((2, 1, tn), lambda j, i: (0, 0, j))],
        out_specs=[out, out],
        out_shape=[jax.ShapeDtypeStruct((t, nb * tn), BF16), jax.ShapeDtypeStruct((t, nb * tn), F32)],
        compiler_params=_params("parallel", "parallel"),
    )(a, w, h, gate)


def mm_swiglu(a, wg, wu, name):
    t, k = a.shape
    s, _, ns = wg.shape

    def body(a_ref, wg_ref, wu_ref, dg_ref, du_ref, act_ref):
        av = a_ref[...]
        g = _dot(av, wg_ref[...])
        u = _dot(av, wu_ref[...])
        sg = _sigmoid(g)
        silu = g * sg
        dg_ref[...] = (u * (sg * (1.0 + g * (1.0 - sg)))).astype(BF16)
        du_ref[...] = silu.astype(BF16)
        act_ref[...] = (silu * u).astype(BF16)

    tm = _matmul_rows(t)
    wspec = pl.BlockSpec((None, k, ns), lambda j, i: (j, 0, 0))
    out = pl.BlockSpec((tm, ns), lambda j, i: (i, j))
    return _pcall(
        body, name=name, grid=(s, t // tm),
        in_specs=[pl.BlockSpec((tm, k), lambda j, i: (i, 0)), wspec, wspec],
        out_specs=[out, out, out],
        out_shape=[jax.ShapeDtypeStruct((t, s * ns), BF16)] * 3,
        compiler_params=_params("parallel", "parallel"),
    )(a, wg, wu)


def mm_nt_cols(dy, w, name, add=None):
    t, n = dy.shape
    s, k, ns = w.shape
    tk = 1024

    def body(*refs):
        if add is None:
            dy_ref, w_ref, o_ref = refs
        else:
            dy_ref, w_ref, add_ref, o_ref = refs
        acc = _dot_nt(dy_ref[:, 0:ns], w_ref[0])
        for sh in range(1, s):
            acc += _dot_nt(dy_ref[:, sh * ns:(sh + 1) * ns], w_ref[sh])
        if add is not None:
            acc += add_ref[...]
        o_ref[...] = acc

    tm = _matmul_rows(t)
    out = pl.BlockSpec((tm, tk), lambda j, i: (i, j))
    in_specs = [pl.BlockSpec((tm, n), lambda j, i: (i, 0)), pl.BlockSpec((s, tk, ns), lambda j, i: (0, j, 0))]
    args = [dy, w]
    if add is not None:
        in_specs.append(out)
        args.append(add)
    return _pcall(
        body, name=name, grid=(k // tk, t // tm),
        in_specs=in_specs, out_specs=out,
        out_shape=jax.ShapeDtypeStruct((t, k), F32),
        compiler_params=_params("parallel", "parallel"),
    )(*args)


def mm_nt_rows(dy, w, name):
    t, n = dy.shape
    s, ks, _ = w.shape

    tm = _matmul_rows(t)

    def body(dy_ref, w_ref, o_ref):
        o_ref[...] = _dot_nt(dy_ref[...], w_ref[...]).astype(BF16)

    return _pcall(
        body, name=name, grid=(s, t // tm),
        in_specs=[pl.BlockSpec((tm, n), lambda j, i: (i, 0)), pl.BlockSpec((None, ks, n), lambda j, i: (j, 0, 0))],
        out_specs=pl.BlockSpec((tm, ks), lambda j, i: (i, j)),
        out_shape=jax.ShapeDtypeStruct((t, s * ks), BF16),
        compiler_params=_params("parallel", "parallel"),
    )(dy, w)


def mm_nt_swiglu_bwd(df, wd, g, u, name):
    t, n = df.shape
    s, ks, _ = wd.shape

    def body(df_ref, w_ref, g_ref, u_ref, dg_ref, du_ref):
        da = _dot_nt(df_ref[...], w_ref[...])
        dg_ref[...] = (da * g_ref[...].astype(F32)).astype(BF16)
        du_ref[...] = (da * u_ref[...].astype(F32)).astype(BF16)

    tm = _matmul_rows(t)
    blk = pl.BlockSpec((tm, ks), lambda j, i: (i, j))
    return _pcall(
        body, name=name, grid=(s, t // tm),
        in_specs=[pl.BlockSpec((tm, n), lambda j, i: (i, 0)), pl.BlockSpec((None, ks, n), lambda j, i: (j, 0, 0)),
                  blk, blk],
        out_specs=[blk, blk],
        out_shape=[jax.ShapeDtypeStruct((t, s * ks), BF16)] * 2,
        compiler_params=_params("parallel", "parallel"),
    )(df, wd, g, u)


def mm_tn(x, dy, name, col_sharded):
    t, k = x.shape
    n = dy.shape[1]
    tk = next(c for c in (1024, 1408, 512, k) if k % c == 0)
    mc = t // 2 if (t // 2) % 128 == 0 else t
    n_m = t // mc
    if col_sharded:
        tn = n // N_SHARD
        out_shape = jax.ShapeDtypeStruct((N_SHARD, k, tn), BF16)
        out_spec = pl.BlockSpec((None, tk, tn), lambda j, kb, m: (j, kb, 0))
    else:
        tn = 1024
        out_shape = jax.ShapeDtypeStruct((k, n), BF16)
        out_spec = pl.BlockSpec((tk, tn), lambda j, kb, m: (kb, j))

    def body(x_ref, dy_ref, o_ref, acc_ref):
        m = pl.program_id(2)
        acc = _dot_tn(x_ref[...], dy_ref[...])

        @pl.when(m == 0)
        def _():
            acc_ref[...] = acc

        @pl.when((m > 0) & (m < n_m - 1))
        def _():
            acc_ref[...] += acc

        @pl.when(m == n_m - 1)
        def _():
            o_ref[...] = (acc if n_m == 1 else acc_ref[...] + acc).astype(BF16)

    out = _pcall(
        body, name=name, grid=(n // tn, k // tk, n_m),
        in_specs=[pl.BlockSpec((mc, tk), lambda j, kb, m: (m, kb)), pl.BlockSpec((mc, tn), lambda j, kb, m: (m, j))],
        out_specs=out_spec, out_shape=out_shape,
        scratch_shapes=[pltpu.VMEM((tk, tn), F32)],
        compiler_params=_params("parallel", "parallel", "arbitrary"),
    )(x, dy)
    return out if col_sharded else out.reshape(N_SHARD, k // N_SHARD, n)


def _swap32(y):
    right = pltpu.roll(y, 32, 1)
    left = pltpu.roll(y, 96, 1)
    lane = lax.broadcasted_iota(jnp.int32, y.shape, 1)
    return jnp.where((lane // 32) % 2 == 0, left, right)


def _rope_tables(t, ctx_len):
    s = t - ctx_len
    pos = np.arange(s)
    row = (pos // GRID_W).astype(np.float32)
    col = (pos % GRID_W).astype(np.float32)
    half = HEAD_DIM // 2
    inv = np.power(np.float32(ROPE_THETA), -np.arange(0, half, 2, dtype=np.float32) / np.float32(half))
    ar = row[:, None] * inv
    ac = col[:, None] * inv
    cos = np.concatenate([np.cos(ar), np.cos(ar), np.cos(ac), np.cos(ac)], axis=1)
    sin = np.concatenate([-np.sin(ar), np.sin(ar), -np.sin(ac), np.sin(ac)], axis=1)
    cos = np.concatenate([np.ones((ctx_len, HEAD_DIM), np.float32), cos.astype(np.float32)], axis=0)
    sin = np.concatenate([np.zeros((ctx_len, HEAD_DIM), np.float32), sin.astype(np.float32)], axis=0)
    return jnp.asarray(cos, F32), jnp.asarray(sin, F32)


def qk_prep_fwd(proj, qg, kg, cos, sin, name):
    t = proj.shape[0]

    def body(p_ref, qg_ref, kg_ref, c_ref, s_ref, q_ref, k_ref, v_ref):
        cv = c_ref[...]
        sv = s_ref[...]
        for hd in range(Q_HEADS + KV_HEADS):
            x = p_ref[:, hd * HEAD_DIM:(hd + 1) * HEAD_DIM]
            gv = qg_ref[...] if hd < Q_HEADS else kg_ref[...]
            y = x * lax.rsqrt(jnp.mean(x * x, axis=-1, keepdims=True) + EPS) * gv
            out = y * cv + _swap32(y) * sv
            if hd < Q_HEADS:
                q_ref[:, hd * HEAD_DIM:(hd + 1) * HEAD_DIM] = (out * ATT_SCALE).astype(BF16)
            else:
                k_ref[:, (hd - Q_HEADS) * HEAD_DIM:(hd - Q_HEADS + 1) * HEAD_DIM] = out.astype(BF16)
        v_ref[...] = p_ref[:, ATT_WIDTH + KV_WIDTH:QKV_WIDTH].astype(BF16)

    vec = pl.BlockSpec((1, HEAD_DIM), lambda i: (0, 0))
    tab = pl.BlockSpec((TM, HEAD_DIM), lambda i: (i, 0))
    return _pcall(
        body, name=name, grid=(t // TM,),
        in_specs=[pl.BlockSpec((TM, QKV_WIDTH), lambda i: (i, 0)), vec, vec, tab, tab],
        out_specs=[pl.BlockSpec((TM, ATT_WIDTH), lambda i: (i, 0)), pl.BlockSpec((TM, KV_WIDTH), lambda i: (i, 0)),
                   pl.BlockSpec((TM, KV_WIDTH), lambda i: (i, 0))],
        out_shape=[jax.ShapeDtypeStruct((t, ATT_WIDTH), BF16), jax.ShapeDtypeStruct((t, KV_WIDTH), BF16),
                   jax.ShapeDtypeStruct((t, KV_WIDTH), BF16)],
        compiler_params=_params("parallel"),
    )(proj, qg, kg, cos, sin)


def qk_prep_bwd(dq, dk, dv, daux, proj, qg, kg, cos, sin, name):
    t, n_in = proj.shape
    n_aux = daux.shape[0]

    def body(dq_ref, dk_ref, dv_ref, da_ref, p_ref, qg_ref, kg_ref, c_ref, s_ref, o_ref, dqg_ref, dkg_ref):
        @pl.when(pl.program_id(0) == 0)
        def _():
            dqg_ref[...] = jnp.zeros_like(dqg_ref)
            dkg_ref[...] = jnp.zeros_like(dkg_ref)

        cv = c_ref[...]
        sv = s_ref[...]
        for hd in range(Q_HEADS + KV_HEADS):
            cols = slice(hd * HEAD_DIM, (hd + 1) * HEAD_DIM)
            x = p_ref[:, cols]
            if hd < Q_HEADS:
                gv, dyr, dg_ref = qg_ref[...], dq_ref[:, cols] * ATT_SCALE, dqg_ref
            else:
                kc = slice((hd - Q_HEADS) * HEAD_DIM, (hd - Q_HEADS + 1) * HEAD_DIM)
                gv, dyr, dg_ref = kg_ref[...], dk_ref[:, kc], dkg_ref
            r = lax.rsqrt(jnp.mean(x * x, axis=-1, keepdims=True) + EPS)
            xhat = x * r
            dy = dyr * cv + _swap32(dyr * sv)
            dg_ref[...] += jnp.sum(dy * xhat, axis=0, keepdims=True)
            dxhat = dy * gv
            o_ref[:, cols] = (r * (dxhat - xhat * jnp.mean(dxhat * xhat, axis=-1, keepdims=True))).astype(BF16)
        o_ref[:, ATT_WIDTH + KV_WIDTH:QKV_WIDTH] = dv_ref[...].astype(BF16)
        for a in range(n_aux):
            o_ref[:, QKV_WIDTH + a * AUX_WIDTH:QKV_WIDTH + (a + 1) * AUX_WIDTH] = da_ref[a]

    vec = pl.BlockSpec((1, HEAD_DIM), lambda i: (0, 0))
    tab = pl.BlockSpec((TM, HEAD_DIM), lambda i: (i, 0))
    return _pcall(
        body, name=name, grid=(t // TM,),
        in_specs=[pl.BlockSpec((TM, ATT_WIDTH), lambda i: (i, 0)), pl.BlockSpec((TM, KV_WIDTH), lambda i: (i, 0)),
                  pl.BlockSpec((TM, KV_WIDTH), lambda i: (i, 0)),
                  pl.BlockSpec((n_aux, TM, AUX_WIDTH), lambda i: (0, i, 0)),
                  pl.BlockSpec((TM, QKV_WIDTH), lambda i: (i, 0)), vec, vec, tab, tab],
        out_specs=[pl.BlockSpec((TM, n_in), lambda i: (i, 0)), vec, vec],
        out_shape=[jax.ShapeDtypeStruct((t, n_in), BF16), jax.ShapeDtypeStruct((1, HEAD_DIM), F32),
                   jax.ShapeDtypeStruct((1, HEAD_DIM), F32)],
        compiler_params=_params("arbitrary"),
    )(dq, dk, dv, daux, proj, qg, kg, cos, sin)


def _band_start(i, t):
    return pl.multiple_of(jnp.clip(i * TM - WINDOW, 0, t - BAND), WINDOW)


def _band_mask(i, start, ctx_len):
    shape = (TM, ctx_len + BAND)
    col = lax.broadcasted_iota(jnp.int32, shape, 1)
    qrow = i * TM + lax.broadcasted_iota(jnp.int32, shape, 0)
    krow = start + col - ctx_len
    band_ok = (krow >= ctx_len) & (jnp.abs(krow - qrow) <= WINDOW)
    return (col < ctx_len) | band_ok


def attention_fwd(q, k, v, sink, ctx_len, windowed, name):
    t = q.shape[0]
    d_model = ATT_WIDTH + AUX_WIDTH
    gw = GROUPS * HEAD_DIM

    def one_head(qh, kk, vv, mask, sink_val):
        s = _dot_nt(qh, kk)
        if mask is not None:
            s = jnp.where(mask, s, NEG_INF)
        m = jnp.max(s, axis=-1, keepdims=True)
        if sink_val is not None:
            m = jnp.maximum(m, sink_val)
        p = jnp.exp(s - m)
        l = jnp.sum(p, axis=-1, keepdims=True)
        if sink_val is not None:
            l = l + jnp.exp(sink_val - m)
        o = _dot(p.astype(BF16), vv) / l
        return o, m + jnp.log(l)

    def body(sink_ref, q_ref, k_ref, v_ref, o_ref, lse_ref):
        kv = pl.program_id(0)
        i = pl.program_id(1)

        def run(kk, vv, mask):
            for g in range(GROUPS):
                sink_val = sink_ref[kv * GROUPS + g] if windowed else None
                o, lse = one_head(q_ref[:, g * HEAD_DIM:(g + 1) * HEAD_DIM], kk, vv, mask, sink_val)
                o_ref[:, g * HEAD_DIM:(g + 1) * HEAD_DIM] = o.astype(BF16)
                lse_ref[g] = lse

        @pl.when(i == 0)
        def _():
            if windowed:
                o_ref[...] = jnp.zeros_like(o_ref)
                lse_ref[...] = jnp.zeros_like(lse_ref)
            else:
                run(k_ref[0:ctx_len], v_ref[0:ctx_len], None)

        @pl.when(i > 0)
        def _():
            if windowed:
                start = _band_start(i, t)
                kk = jnp.concatenate([k_ref[0:ctx_len], k_ref[pl.ds(start, BAND)]], axis=0)
                vv = jnp.concatenate([v_ref[0:ctx_len], v_ref[pl.ds(start, BAND)]], axis=0)
                run(kk, vv, _band_mask(i, start, ctx_len))
            else:
                run(k_ref[...], v_ref[...], None)

    kvspec = pl.BlockSpec((t, HEAD_DIM), lambda kv, i: (0, kv))
    return _pcall(
        body, name=name, grid=(KV_HEADS, t // TM),
        in_specs=[pl.BlockSpec(memory_space=pltpu.SMEM), pl.BlockSpec((TM, gw), lambda kv, i: (i, kv)), kvspec, kvspec],
        out_specs=[pl.BlockSpec((TM, gw), lambda kv, i: (i, kv)),
                   pl.BlockSpec((GROUPS, TM, 1), lambda kv, i: (kv, i, 0))],
        out_shape=[jax.ShapeDtypeStruct((t, d_model), BF16), jax.ShapeDtypeStruct((Q_HEADS, t, 1), F32)],
        compiler_params=_params("parallel", "parallel"),
    )(sink, q, k, v)


def attention_bwd(q, k, v, cat, lse, dcat, sink, ctx_len, windowed, name):
    t = q.shape[0]
    gw = GROUPS * HEAD_DIM

    def body(sink_ref, q_ref, k_ref, v_ref, o_ref, lse_ref, do_ref, dq_ref, dk_ref, dv_ref, dsink_ref):
        kv = pl.program_id(0)
        i = pl.program_id(1)

        @pl.when(i == 0)
        def _():
            dk_ref[...] = jnp.zeros_like(dk_ref)
            dv_ref[...] = jnp.zeros_like(dv_ref)
            dsink_ref[...] = jnp.zeros_like(dsink_ref)

        def run(kk, vv, mask, accumulate):
            for g in range(GROUPS):
                cols = slice(g * HEAD_DIM, (g + 1) * HEAD_DIM)
                qh = q_ref[:, cols]
                dob = do_ref[:, cols]
                delta = jnp.sum(dob.astype(F32) * o_ref[:, cols].astype(F32), axis=-1, keepdims=True)
                lse_g = lse_ref[g]
                s = _dot_nt(qh, kk)
                if mask is not None:
                    s = jnp.where(mask, s, NEG_INF)
                p = jnp.exp(s - lse_g)
                dp = _dot_nt(dob, vv)
                ds = (p * (dp - delta)).astype(BF16)
                dq_ref[:, cols] = _dot(ds, kk)
                accumulate(_dot_tn(ds, qh), _dot_tn(p.astype(BF16), dob))
                if windowed:
                    p_sink = jnp.exp(sink_ref[kv * GROUPS + g] - lse_g)
                    dsink_ref[g:g + 1, :] += jnp.sum(-p_sink * delta, axis=0, keepdims=True)

        @pl.when(i == 0)
        def _():
            if windowed:
                dq_ref[...] = jnp.zeros_like(dq_ref)
            else:
                def acc_ctx(dkp, dvp):
                    dk_ref[0:ctx_len] += dkp
                    dv_ref[0:ctx_len] += dvp

                run(k_ref[0:ctx_len], v_ref[0:ctx_len], None, acc_ctx)

        @pl.when(i > 0)
        def _():
            if windowed:
                start = _band_start(i, t)
                kk = jnp.concatenate([k_ref[0:ctx_len], k_ref[pl.ds(start, BAND)]], axis=0)
                vv = jnp.concatenate([v_ref[0:ctx_len], v_ref[pl.ds(start, BAND)]], axis=0)

                def acc_band(dkp, dvp):
                    dk_ref[0:ctx_len] += dkp[0:ctx_len]
                    dv_ref[0:ctx_len] += dvp[0:ctx_len]
                    dk_ref[pl.ds(start, BAND)] += dkp[ctx_len:ctx_len + BAND]
                    dv_ref[pl.ds(start, BAND)] += dvp[ctx_len:ctx_len + BAND]

                run(kk, vv, _band_mask(i, start, ctx_len), acc_band)
            else:
                def acc_all(dkp, dvp):
                    dk_ref[...] += dkp
                    dv_ref[...] += dvp

                run(k_ref[...], v_ref[...], None, acc_all)

    kvspec = pl.BlockSpec((t, HEAD_DIM), lambda kv, i: (0, kv))
    grp = pl.BlockSpec((TM, gw), lambda kv, i: (i, kv))
    return _pcall(
        body, name=name, grid=(KV_HEADS, t // TM),
        in_specs=[pl.BlockSpec(memory_space=pltpu.SMEM), grp, kvspec, kvspec, grp,
                  pl.BlockSpec((GROUPS, TM, 1), lambda kv, i: (kv, i, 0)), grp],
        out_specs=[grp, kvspec, kvspec, pl.BlockSpec((None, 8, 128), lambda kv, i: (kv, 0, 0))],
        out_shape=[jax.ShapeDtypeStruct((t, ATT_WIDTH), F32), jax.ShapeDtypeStruct((t, KV_WIDTH), F32),
                   jax.ShapeDtypeStruct((t, KV_WIDTH), F32), jax.ShapeDtypeStruct((KV_HEADS, 8, 128), F32)],
        compiler_params=_params("parallel", "arbitrary"),
    )(sink, q, k, v, cat, lse, dcat)


def _segment_pos(t, ctx_len, width):
    row = lax.broadcasted_iota(jnp.int32, (t, width), 0)
    in_ctx = row < ctx_len
    return jnp.where(in_ctx, row, row - ctx_len), jnp.where(in_ctx, ctx_len, t - ctx_len)


def _shifted(x, offset, pos, seg_len):
    t = x.shape[0]
    moved = x if offset == 0 else pltpu.roll(x, (-offset) % t, 0)
    ok = (pos + offset >= 0) & (pos + offset < seg_len)
    return jnp.where(ok, moved, 0.0)


def conv_fwd(proj, conv_w, cat, ctx_len, name):
    t = proj.shape[0]
    base = QKV_WIDTH // 128
    per = AUX_WIDTH // 128

    def body(gb_ref, gc_ref, u_ref, w_ref, cat_in, o_ref):
        del cat_in
        pos, seg = _segment_pos(t, ctx_len, 128)
        z = gc_ref[...] * u_ref[...]
        conv = (w_ref[0:1, :] * _shifted(z, -1, pos, seg) + w_ref[1:2, :] * z
                + w_ref[2:3, :] * _shifted(z, 1, pos, seg))
        o_ref[...] = (gb_ref[...] * conv).astype(BF16)

    def col(off):
        return pl.BlockSpec((t, 128), lambda j: (0, base + off * per + j))

    return _pcall(
        body, name=name, grid=(per,),
        in_specs=[col(0), col(1), col(2), pl.BlockSpec((3, 128), lambda j: (0, j)),
                  pl.BlockSpec(memory_space=pl.ANY)],
        out_specs=pl.BlockSpec((t, 128), lambda j: (0, ATT_WIDTH // 128 + j)),
        out_shape=jax.ShapeDtypeStruct(cat.shape, BF16),
        input_output_aliases={4: 0},
        compiler_params=_params("parallel"),
    )(proj, proj, proj, conv_w, cat)


def conv_bwd(dcat, proj, conv_w, ctx_len, name):
    t = proj.shape[0]
    base = QKV_WIDTH // 128
    per = AUX_WIDTH // 128

    def body(do_ref, gb_ref, gc_ref, u_ref, w_ref, da_ref, dw_ref):
        pos, seg = _segment_pos(t, ctx_len, 128)
        gc = gc_ref[...]
        u = u_ref[...]
        z = gc * u
        zm = _shifted(z, -1, pos, seg)
        zp = _shifted(z, 1, pos, seg)
        conv = w_ref[0:1, :] * zm + w_ref[1:2, :] * z + w_ref[2:3, :] * zp
        dout = do_ref[...].astype(F32)
        da_ref[0] = (dout * conv).astype(BF16)
        dconv = dout * gb_ref[...]
        dw_ref[0:1, :] = jnp.sum(dconv * zm, axis=0, keepdims=True)
        dw_ref[1:2, :] = jnp.sum(dconv * z, axis=0, keepdims=True)
        dw_ref[2:3, :] = jnp.sum(dconv * zp, axis=0, keepdims=True)
        dz = (w_ref[1:2, :] * dconv + w_ref[0:1, :] * _shifted(dconv, 1, pos, seg)
              + w_ref[2:3, :] * _shifted(dconv, -1, pos, seg))
        da_ref[1] = (dz * u).astype(BF16)
        da_ref[2] = (dz * gc).astype(BF16)

    def col(off):
        return pl.BlockSpec((t, 128), lambda j: (0, base + off * per + j))

    return _pcall(
        body, name=name, grid=(per,),
        in_specs=[pl.BlockSpec((t, 128), lambda j: (0, ATT_WIDTH // 128 + j)), col(0), col(1), col(2),
                  pl.BlockSpec((3, 128), lambda j: (0, j))],
        out_specs=[pl.BlockSpec((3, t, 128), lambda j: (0, 0, j)), pl.BlockSpec((3, 128), lambda j: (0, j))],
        out_shape=[jax.ShapeDtypeStruct((3, t, AUX_WIDTH), BF16), jax.ShapeDtypeStruct((3, AUX_WIDTH), F32)],
        compiler_params=_params("parallel"),
    )(dcat, proj, proj, proj, conv_w)


def _pooled(u, w, pos, seg):
    lo = jnp.clip(pos - w // 2, 0, seg)
    hi = jnp.clip(pos - w // 2 + w, 0, seg)
    inv = 1.0 / (hi - lo).astype(F32)
    total = _shifted(u, -(w // 2), pos, seg)
    for o in range(-(w // 2) + 1, w // 2):
        total = total + _shifted(u, o, pos, seg)
    return total * inv - u, inv


def pool_fwd(proj, pool_w, pool_scale, cat, ctx_len, name):
    t = proj.shape[0]

    def body(u_ref, w_ref, sc_ref, cat_in, o_ref):
        del cat_in
        pos, seg = _segment_pos(t, ctx_len, 128)
        for g, w in enumerate(POOL_WINDOWS):
            cols = slice(g * 128, (g + 1) * 128)
            pooled, _ = _pooled(u_ref[:, cols], w, pos, seg)
            mixed = _dot(pooled.astype(BF16), w_ref[g].astype(BF16))
            o_ref[:, cols] = (mixed * sc_ref[:, cols]).astype(BF16)

    return _pcall(
        body, name=name, grid=(1,),
        in_specs=[pl.BlockSpec((t, AUX_WIDTH), lambda j: (0, QKV_WIDTH // AUX_WIDTH)),
                  pl.BlockSpec((AUX_GROUPS, 128, 128), lambda j: (0, 0, 0)),
                  pl.BlockSpec((1, AUX_WIDTH), lambda j: (0, 0)), pl.BlockSpec(memory_space=pl.ANY)],
        out_specs=pl.BlockSpec((t, AUX_WIDTH), lambda j: (0, ATT_WIDTH // AUX_WIDTH)),
        out_shape=jax.ShapeDtypeStruct(cat.shape, BF16),
        input_output_aliases={3: 0},
        compiler_params=_params("arbitrary"),
    )(proj, pool_w, pool_scale, cat)


def pool_bwd(dcat, proj, pool_w, pool_scale, ctx_len, name):
    t = proj.shape[0]

    def body(do_ref, u_ref, w_ref, sc_ref, da_ref, dw_ref, dsc_ref):
        pos, seg = _segment_pos(t, ctx_len, 128)
        for g, w in enumerate(POOL_WINDOWS):
            cols = slice(g * 128, (g + 1) * 128)
            pooled, inv = _pooled(u_ref[:, cols], w, pos, seg)
            pb = pooled.astype(BF16)
            wb = w_ref[g].astype(BF16)
            mixed = _dot(pb, wb)
            dout = do_ref[:, cols].astype(F32)
            dsc_ref[:, cols] = jnp.sum(dout * mixed, axis=0, keepdims=True)
            dmixed = (dout * sc_ref[:, cols]).astype(BF16)
            dw_ref[g] = _dot_tn(pb, dmixed)
            dpooled = _dot_nt(dmixed, wb)
            spread = dpooled * inv
            du = _shifted(spread, w // 2, pos, seg) - dpooled
            for o in range(-(w // 2) + 1, w // 2):
                du = du + _shifted(spread, -o, pos, seg)
            da_ref[0, :, cols] = du.astype(BF16)

    return _pcall(
        body, name=name, grid=(1,),
        in_specs=[pl.BlockSpec((t, AUX_WIDTH), lambda j: (0, ATT_WIDTH // AUX_WIDTH)),
                  pl.BlockSpec((t, AUX_WIDTH), lambda j: (0, QKV_WIDTH // AUX_WIDTH)),
                  pl.BlockSpec((AUX_GROUPS, 128, 128), lambda j: (0, 0, 0)),
                  pl.BlockSpec((1, AUX_WIDTH), lambda j: (0, 0))],
        out_specs=[pl.BlockSpec((1, t, AUX_WIDTH), lambda j: (0, 0, 0)),
                   pl.BlockSpec((AUX_GROUPS, 128, 128), lambda j: (0, 0, 0)),
                   pl.BlockSpec((1, AUX_WIDTH), lambda j: (0, 0))],
        out_shape=[jax.ShapeDtypeStruct((1, t, AUX_WIDTH), BF16), jax.ShapeDtypeStruct((AUX_GROUPS, 128, 128), F32),
                   jax.ShapeDtypeStruct((1, AUX_WIDTH), F32)],
        compiler_params=_params("arbitrary"),
    )(dcat, proj, pool_w, pool_scale)


MOD_NAMES = ("shift1", "scale1", "gate1", "shift2", "scale2", "gate2")


def _layer_fwd(h, lw, weight, mods, tabs, ctx_len, windowed, tag, after_mixer=None):
    cos, sin = tabs
    xn = norm_mod_fwd(h, lw["norm1_g"], mods["shift1"], mods["scale1"], tag + "norm1_fwd")
    lw["w_in"] = weight("w_in", xn)
    proj = mm_nn(xn, lw["w_in"], tag + "w_in_fwd")
    q, k, v = qk_prep_fwd(proj, lw["q_g"], lw["k_g"], cos, sin, tag + "qk_fwd")
    cat, lse = attention_fwd(q, k, v, lw["sink"], ctx_len, windowed, tag + "attn_fwd")
    if windowed:
        cat = pool_fwd(proj, lw["pool_w"], lw["pool_scale"], cat, ctx_len, tag + "pool_fwd")
    else:
        cat = conv_fwd(proj, lw["conv_w"], cat, ctx_len, tag + "conv_fwd")
    lw["w_out"] = weight("w_out", cat)
    gate1 = mods["gate1"] if after_mixer is None else mods["gate1"] + after_mixer(cat)[0, 0]
    y, h1 = mm_nn_residual(cat, lw["w_out"], h, gate1, tag + "w_out_fwd")
    hn = norm_mod_fwd(h1, lw["norm2_g"], mods["shift2"], mods["scale2"], tag + "norm2_fwd")
    lw["w_gate"], lw["w_up"] = weight("w_gate", hn), weight("w_up", hn)
    dact_dg, dact_du, act = mm_swiglu(hn, lw["w_gate"], lw["w_up"], tag + "ffn_in_fwd")
    lw["w_down"] = weight("w_down", act)
    f, h2 = mm_nn_residual(act, lw["w_down"], h1, mods["gate2"], tag + "w_down_fwd")
    saved = dict(h=h, xn=xn, proj=proj, q=q, k=k, v=v, cat=cat, lse=lse, y=y, h1=h1, hn=hn, dact_dg=dact_dg,
                 dact_du=dact_du, act=act, f=f)
    return h2, saved


def _layer_bwd_ffn(dh, sv, lw, mods, tag):
    big, small = {}, {}
    df, dgate2 = gate_bwd(dh, sv["f"], mods["gate2"], tag + "gate2_bwd")
    dgp, du = mm_nt_swiglu_bwd(df, lw["w_down"], sv["dact_dg"], sv["dact_du"], tag + "w_down_dgrad")
    big["w_down"] = mm_tn(sv["act"], df, tag + "w_down_wgrad", col_sharded=False)
    dhn = mm_nt_cols(dgp, lw["w_gate"], tag + "w_gate_dgrad")
    dhn = mm_nt_cols(du, lw["w_up"], tag + "w_up_dgrad", add=dhn)
    big["w_gate"] = mm_tn(sv["hn"], dgp, tag + "w_gate_wgrad", col_sharded=True)
    big["w_up"] = mm_tn(sv["hn"], du, tag + "w_up_wgrad", col_sharded=True)
    dh1, dshift2, dscale2, small["norm2_g"] = norm_mod_bwd(dhn, sv["h1"], lw["norm2_g"], mods["scale2"], dh,
                                                           tag + "norm2_bwd")
    return dh1, big, small, (dshift2, dscale2, dgate2)


def _layer_bwd_mixer(dh1, sv, lw, mods, tabs, ctx_len, windowed, tag, between=None):
    cos, sin = tabs
    big, small = {}, {}
    dy, dgate1 = gate_bwd(dh1, sv["y"], mods["gate1"], tag + "gate1_bwd")
    dcat = mm_nt_rows(dy, lw["w_out"], tag + "w_out_dgrad")
    big["w_out"] = mm_tn(sv["cat"], dy, tag + "w_out_wgrad", col_sharded=False)
    if windowed:
        daux, small["pool_w"], small["pool_scale"] = pool_bwd(dcat, sv["proj"], lw["pool_w"], lw["pool_scale"],
                                                              ctx_len, tag + "pool_bwd")
    else:
        daux, small["conv_w"] = conv_bwd(dcat, sv["proj"], lw["conv_w"], ctx_len, tag + "conv_bwd")
    dq, dk, dv, dsink = attention_bwd(sv["q"], sv["k"], sv["v"], sv["cat"], sv["lse"], dcat, lw["sink"], ctx_len,
                                      windowed, tag + "attn_bwd")
    if windowed:
        small["sink"] = dsink[:, :GROUPS, 0].reshape(Q_HEADS)
    q_g = lw["q_g"] if between is None else lw["q_g"] + between(dq)[0, 0]
    dproj, small["q_g"], small["k_g"] = qk_prep_bwd(dq, dk, dv, daux, sv["proj"], q_g, lw["k_g"], cos, sin,
                                                    tag + "qk_bwd")
    dxn = mm_nt_cols(dproj, lw["w_in"], tag + "w_in_dgrad")
    big["w_in"] = mm_tn(sv["xn"], dproj, tag + "w_in_wgrad", col_sharded=True)
    dh0, dshift1, dscale1, small["norm1_g"] = norm_mod_bwd(dxn, sv["h"], lw["norm1_g"], mods["scale1"], dh1,
                                                           tag + "norm1_bwd")
    return dh0, big, small, (dshift1, dscale1, dgate1)


def _dmod_rows(mixer_part, ffn_part):
    return jnp.concatenate([m[:, 0, :] for m in (*mixer_part, *ffn_part)], axis=1)


def _tie(mods, name, token):
    return {**mods, name: mods[name] + token[0, 0]}


def _place():
    x, y, c = lax.axis_index("x"), lax.axis_index("y"), lax.axis_index("c")
    chips = [(1 - x, y), (x, 1 - y), (1 - x, 1 - y)]
    return x, y, c, chips


def _remote(src, dst, send_sem, recv_sem, to):
    return pltpu.make_async_remote_copy(src_ref=src, dst_ref=dst, send_sem=send_sem, recv_sem=recv_sem,
                                        device_id=to, device_id_type=MESH)


ANY = pl.BlockSpec(memory_space=pl.ANY)


def all_gather8(x_shard, name):
    m_per, n = x_shard.shape

    def body(x_ref, out_ref, send_sems, recv_sems, local_sem):
        x, y, c, chips = _place()
        me, sibling = (x, y, c), (x, y, 1 - c)

        def rows(px, py, pc):
            return out_ref.at[pl.ds((4 * px + 2 * py + pc) * m_per, m_per), :]

        def copy(k, block, to, src=None):
            return _remote(rows(*block) if src is None else src, rows(*block), send_sems.at[k], recv_sems.at[k], to)

        mine = pltpu.make_async_copy(x_ref, rows(*me), local_sem)
        mine.start()
        first = [copy(0, me, sibling, src=x_ref)]
        first += [copy(1 + j, me, (*chip, c), src=x_ref) for j, chip in enumerate(chips)]
        for cp in first:
            cp.start()
        passed = [copy(4 + j, (*chip, c), sibling) for j, chip in enumerate(chips)]
        for j, chip in enumerate(chips):
            copy(1 + j, (*chip, c), me).wait_recv()
            passed[j].start()
        copy(0, sibling, me).wait_recv()
        for j, chip in enumerate(chips):
            copy(4 + j, (*chip, 1 - c), me).wait_recv()
        for cp in first + passed:
            cp.wait_send()
        mine.wait()

    return _pcall(
        body, name=name,
        out_shape=jax.ShapeDtypeStruct((N_DEV * m_per, n), x_shard.dtype),
        in_specs=[pl.BlockSpec(memory_space=pltpu.VMEM)],
        out_specs=pl.BlockSpec(memory_space=pltpu.VMEM),
        scratch_shapes=[pltpu.SemaphoreType.DMA((7,)), pltpu.SemaphoreType.DMA((7,)), pltpu.SemaphoreType.DMA],
        compiler_params=pltpu.CompilerParams(vmem_limit_bytes=VMEM_LIMIT),
    )(x_shard)


def _row_tile(rows, largest=256):
    for tr in (1024, 704, 512, 256, 128, 64, 32, 16):
        if tr <= largest and rows % tr == 0:
            return tr
    return rows


def place_shard(w, p, name):
    r, cols = w.shape
    tr = _row_tile(r, 1024)

    def body(p_ref, w_ref, o_ref):
        del p_ref
        o_ref[...] = w_ref[...].astype(BF16)

    return _pcall(
        body, name=name,
        grid_spec=pltpu.PrefetchScalarGridSpec(
            num_scalar_prefetch=1, grid=(r // tr,),
            in_specs=[pl.BlockSpec((tr, cols), lambda i, p_ref: (i, 0))],
            out_specs=pl.BlockSpec((None, tr, cols), lambda i, p_ref: (p_ref[0], i, 0))),
        out_shape=jax.ShapeDtypeStruct((N_SHARD, r, cols), BF16),
        compiler_params=_params("parallel"),
    )(p.reshape(1).astype(jnp.int32), w)


def gather_weight_shards(stacks, after):
    n = len(stacks)

    def body(*refs):
        outs = refs[n + 1:2 * n + 1]
        send_sems, recv_sems = refs[2 * n + 1:]
        x, y, c, chips = _place()
        p = 2 * x + y
        sibling = (x, y, 1 - c)
        started = []
        for i in range(n):
            hr = outs[i].shape[1] // 2
            mine = outs[i].at[p, pl.ds(c * hr, hr)]
            for j, chip in enumerate(chips):
                cp = _remote(mine, mine, send_sems.at[i, j], recv_sems.at[i, j], (*chip, c))
                cp.start()
                started.append(cp)
        for i in range(n):
            hr = outs[i].shape[1] // 2
            for j, (px, py) in enumerate(chips):
                half = outs[i].at[2 * px + py, pl.ds(c * hr, hr)]
                _remote(half, half, send_sems.at[i, j], recv_sems.at[i, j], (px, py, c)).wait_recv()
                fwd = _remote(half, half, send_sems.at[i, 3 + j], recv_sems.at[i, 3 + j], sibling)
                fwd.start()
                started.append(fwd)
        for i in range(n):
            hr = outs[i].shape[1] // 2
            for j, (px, py) in enumerate(chips):
                other = outs[i].at[2 * px + py, pl.ds((1 - c) * hr, hr)]
                _remote(other, other, send_sems.at[i, 3 + j], recv_sems.at[i, 3 + j], sibling).wait_recv()
        for cp in started:
            cp.wait_send()

    return _pcall(
        body, name="gather_weight_shards",
        out_shape=[jax.ShapeDtypeStruct(s.shape, s.dtype) for s in stacks],
        in_specs=[ANY] * (n + 1), out_specs=[ANY] * n,
        input_output_aliases={i: i for i in range(n)},
        scratch_shapes=[pltpu.SemaphoreType.DMA((n, 6)), pltpu.SemaphoreType.DMA((n, 6))],
    )(*stacks, after)


HBM_SPEC = pl.BlockSpec(memory_space=pltpu.HBM)
SEM_SPEC = pl.BlockSpec(memory_space=pltpu.SEMAPHORE)
EFFECT = pltpu.SideEffectType.DATAFLOW_SIDE_EFFECTING


def split_start(name, copies, n_sems, arrays, after):
    m = len(arrays)

    def body(*refs):
        for cp in copies(refs[:m], refs[m + 1], refs[m + 2]):
            cp.start()
        refs[-1][...] = jnp.zeros_like(refs[-1])

    res = _pcall(
        body, name=name,
        out_shape=(pltpu.SemaphoreType.DMA((n_sems,)), pltpu.SemaphoreType.DMA((n_sems,)),
                   *[pltpu.HBM(s.shape, s.dtype) for s in arrays], jax.ShapeDtypeStruct((8, 128), F32)),
        in_specs=[HBM_SPEC] * m + [ANY],
        out_specs=(SEM_SPEC, SEM_SPEC, *[HBM_SPEC] * m, pl.BlockSpec(memory_space=pltpu.VMEM)),
        input_output_aliases={i: 2 + i for i in range(m)},
        compiler_params=pltpu.CompilerParams(has_side_effects=EFFECT),
    )(*[pltpu.with_memory_space_constraint(s, pltpu.HBM) for s in arrays], after)
    return res[0], res[1], list(res[2:2 + m]), res[2 + m]


def split_wait(name, copies, send_sems, recv_sems, arrays, after):
    m = len(arrays)

    def body(*refs):
        for cp in copies(refs[:m], refs[m], refs[m + 1]):
            cp.wait_send()
            cp.wait_recv()

    return _pcall(
        body, name=name,
        out_shape=[pltpu.HBM(s.shape, s.dtype) for s in arrays],
        in_specs=[HBM_SPEC] * m + [SEM_SPEC, SEM_SPEC, ANY],
        out_specs=[HBM_SPEC] * m,
        input_output_aliases={i: i for i in range(m)},
        compiler_params=pltpu.CompilerParams(has_side_effects=EFFECT),
    )(*arrays, send_sems, recv_sems, after)


def gather_copies(n):
    def copies(stacks, send_sems, recv_sems):
        x, y, c, chips = _place()
        p = 2 * x + y
        return [_remote(stacks[i].at[p], stacks[i].at[p], send_sems.at[3 * i + j], recv_sems.at[3 * i + j], (*chip, c))
                for i in range(n) for j, chip in enumerate(chips)]
    return copies


def gather_half_copies(n):
    def copies(stacks, send_sems, recv_sems):
        x, y, c, chips = _place()
        p = 2 * x + y
        out = []
        for i in range(n):
            hr = stacks[i].shape[1] // 2
            mine = stacks[i].at[p, pl.ds(c * hr, hr)]
            out += [_remote(mine, mine, send_sems.at[3 * i + j], recv_sems.at[3 * i + j], (*chip, c))
                    for j, chip in enumerate(chips)]
        return out
    return copies


def forward_half_copies(n):
    def copies(stacks, send_sems, recv_sems):
        x, y, c, chips = _place()
        out = []
        for i in range(n):
            hr = stacks[i].shape[1] // 2
            for j, (px, py) in enumerate(chips):
                half = stacks[i].at[2 * px + py, pl.ds(c * hr, hr)]
                out.append(_remote(half, half, send_sems.at[3 * i + j], recv_sems.at[3 * i + j], (x, y, 1 - c)))
        return out
    return copies


def sibling_half_copies(n):
    def copies(refs, send_sems, recv_sems):
        x, y, c, _ = _place()
        out = []
        for i in range(n):
            hr = refs[n + i].shape[1]
            out.append(_remote(refs[i].at[:, pl.ds((1 - c) * hr, hr), :], refs[n + i], send_sems.at[i], recv_sems.at[i],
                               (x, y, 1 - c)))
        return out
    return copies


def chip_quarter_copies(n):
    def copies(refs, send_sems, recv_sems):
        _, _, c, chips = _place()
        return [_remote(refs[i].at[2 * px + py], refs[n + i].at[j], send_sems.at[3 * i + j], recv_sems.at[3 * i + j],
                        (px, py, c)) for i in range(n) for j, (px, py) in enumerate(chips)]
    return copies


def exchange_final_halves(grads, name):
    n = len(grads)

    def body(*refs):
        outs = refs[n:2 * n]
        send_sems, recv_sems = refs[2 * n:]
        x, y, c, _ = _place()
        sends = []
        for i in range(n):
            hr = outs[i].shape[0] // 2
            mine = outs[i].at[pl.ds(c * hr, hr)]
            cp = _remote(mine, mine, send_sems.at[i], recv_sems.at[i], (x, y, 1 - c))
            cp.start()
            sends.append(cp)
        for i in range(n):
            hr = outs[i].shape[0] // 2
            other = outs[i].at[pl.ds((1 - c) * hr, hr)]
            _remote(other, other, send_sems.at[i], recv_sems.at[i], (x, y, 1 - c)).wait_recv()
        for cp in sends:
            cp.wait_send()

    return _pcall(
        body, name=name,
        out_shape=[jax.ShapeDtypeStruct(g.shape, g.dtype) for g in grads],
        in_specs=[ANY] * n, out_specs=[ANY] * n,
        input_output_aliases={i: i for i in range(n)},
        scratch_shapes=[pltpu.SemaphoreType.DMA((n,)), pltpu.SemaphoreType.DMA((n,))],
    )(*grads)


def add_own_half(g, recv, c, name):
    s, r, cols = g.shape
    hr = r // 2
    tr = _row_tile(hr, 1024)
    nb = hr // tr

    def body(c_ref, g_ref, r_ref, o_ref):
        del c_ref
        o_ref[...] = (g_ref[...].astype(F32) + r_ref[...].astype(F32)).astype(BF16)

    return _pcall(
        body, name=name,
        grid_spec=pltpu.PrefetchScalarGridSpec(
            num_scalar_prefetch=1, grid=(s, nb),
            in_specs=[pl.BlockSpec((None, tr, cols), lambda j, i, c_ref: (j, c_ref[0] * nb + i, 0)),
                      pl.BlockSpec((None, tr, cols), lambda j, i, c_ref: (j, i, 0))],
            out_specs=pl.BlockSpec((None, tr, cols), lambda j, i, c_ref: (j, i, 0))),
        out_shape=jax.ShapeDtypeStruct((s, hr, cols), BF16),
        compiler_params=_params("parallel", "parallel"),
    )(c.reshape(1).astype(jnp.int32), g, recv)


def add_quarters(own, others, p, c, name, deps=()):
    _, hr, cols = own.shape
    tr = _row_tile(hr, 1024)
    nb = hr // tr

    def body(pc_ref, own_ref, oth_ref, *rest):
        o_ref = rest[-1]
        acc = own_ref[...].astype(F32) + oth_ref[0].astype(F32)
        acc = acc + oth_ref[1].astype(F32)
        o_ref[...] = acc + oth_ref[2].astype(F32)

    return _pcall(
        body, name=name,
        grid_spec=pltpu.PrefetchScalarGridSpec(
            num_scalar_prefetch=1, grid=(nb,),
            in_specs=[pl.BlockSpec((None, tr, cols), lambda i, pc: (pc[0], i, 0)),
                      pl.BlockSpec((3, tr, cols), lambda i, pc: (0, i, 0))] + [ANY] * len(deps),
            out_specs=pl.BlockSpec((tr, cols), lambda i, pc: (pc[1] * nb + i, 0))),
        out_shape=jax.ShapeDtypeStruct((2 * hr, cols), F32),
        compiler_params=_params("parallel"),
    )(jnp.stack([p, c]).astype(jnp.int32), own, others, *deps)


def _adamw_math(w, g, m, v):
    m2 = ADAM_B1 * m + (1.0 - ADAM_B1) * g
    v2 = ADAM_B2 * v + (1.0 - ADAM_B2) * (g * g)
    m_hat = m2 / (1.0 - ADAM_B1 ** ADAM_STEP)
    v_hat = v2 / (1.0 - ADAM_B2 ** ADAM_STEP)
    delta = -ADAM_LR * (m_hat / (jnp.sqrt(v_hat) + ADAM_EPS) + ADAM_WD * w)
    return delta, m2, v2


def adamw(w, g, m, v, name):
    r, cols = w.shape
    tr = _row_tile(r)

    def body(w_ref, g_ref, m_ref, v_ref, go_ref, d_ref, m2_ref, v2_ref):
        g = g_ref[...]
        go_ref[...] = g
        d_ref[...], m2_ref[...], v2_ref[...] = _adamw_math(w_ref[...], g, m_ref[...], v_ref[...])

    blk = pl.BlockSpec((tr, cols), lambda i: (i, 0))
    return _pcall(
        body, name=name, grid=(r // tr,),
        in_specs=[blk] * 4, out_specs=[blk] * 4,
        out_shape=[jax.ShapeDtypeStruct((r, cols), F32)] * 4,
        compiler_params=_params("parallel"),
    )(w, g, m, v)


def small_update(gathered, w, m, v):
    nd, r, lanes = gathered.shape

    def body(ga_ref, w_ref, m_ref, v_ref, g_ref, d_ref, m2_ref, v2_ref):
        g = ga_ref[0] + ga_ref[1]
        for dev in range(2, nd):
            g = g + ga_ref[dev]
        g_ref[...] = g
        d_ref[...], m2_ref[...], v2_ref[...] = _adamw_math(w_ref[...], g, m_ref[...], v_ref[...])

    return _pcall(
        body, name="small_update",
        out_shape=[jax.ShapeDtypeStruct((r, lanes), F32)] * 4,
        compiler_params=pltpu.CompilerParams(vmem_limit_bytes=VMEM_LIMIT),
    )(gathered, w, m, v)


def mod_fwd(c16, w_mod, b_mod, name):
    d, ns = w_mod.shape
    tn = 512

    def body(c_ref, w_ref, b_ref, o_ref):
        cv = c_ref[...]
        sc = (cv * _sigmoid(cv)).astype(BF16)
        o_ref[...] = _dot(sc, w_ref[...].astype(BF16)) + b_ref[...]

    return _pcall(
        body, name=name, grid=(ns // tn,),
        in_specs=[pl.BlockSpec((16, d), lambda j: (0, 0)), pl.BlockSpec((d, tn), lambda j: (0, j)),
                  pl.BlockSpec((1, tn), lambda j: (0, j))],
        out_specs=pl.BlockSpec((16, tn), lambda j: (0, j)),
        out_shape=jax.ShapeDtypeStruct((16, ns), F32),
        compiler_params=_params("parallel"),
    )(c16, w_mod, b_mod)


def wmod_update(c16, dmod16, w, m, v, name):
    d, ns = w.shape
    tn = 256

    def body(c_ref, dm_ref, w_ref, m_ref, v_ref, g_ref, d_ref, m2_ref, v2_ref):
        cv = c_ref[...]
        sc = (cv * _sigmoid(cv)).astype(BF16)
        g = _dot_tn(sc, dm_ref[...].astype(BF16))
        g_ref[...] = g
        d_ref[...], m2_ref[...], v2_ref[...] = _adamw_math(w_ref[...], g, m_ref[...], v_ref[...])

    blk = pl.BlockSpec((d, tn), lambda j: (0, j))
    return _pcall(
        body, name=name, grid=(ns // tn,),
        in_specs=[pl.BlockSpec((16, d), lambda j: (0, 0)), pl.BlockSpec((16, tn), lambda j: (0, j)), blk, blk, blk],
        out_specs=[blk] * 4,
        out_shape=[jax.ShapeDtypeStruct((d, ns), F32)] * 4,
        compiler_params=_params("parallel"),
    )(c16, dmod16, w, m, v)


def cctx_partial(dm0, w0, dm1, w1):
    d, ns = w0.shape
    tn = 512

    def body(dm0_ref, w0_ref, dm1_ref, w1_ref, o_ref):
        @pl.when(pl.program_id(0) == 0)
        def _():
            o_ref[...] = jnp.zeros_like(o_ref)

        for dm_ref, w_ref in ((dm0_ref, w0_ref), (dm1_ref, w1_ref)):
            tot = jnp.sum(dm_ref[...], axis=0, keepdims=True)
            lhs = jnp.broadcast_to(tot, (8, tn)).astype(BF16)
            o_ref[...] += _dot_nt(lhs, w_ref[...].astype(BF16))

    dspec = pl.BlockSpec((8, tn), lambda j: (0, j))
    wspec = pl.BlockSpec((d, tn), lambda j: (0, j))
    return _pcall(
        body, name="cctx_partial", grid=(ns // tn,),
        in_specs=[dspec, wspec, dspec, wspec],
        out_specs=pl.BlockSpec((8, d), lambda j: (0, 0)),
        out_shape=jax.ShapeDtypeStruct((8, d), F32),
        compiler_params=_params("arbitrary"),
    )(dm0, w0, dm1, w1)


def cctx_update(parts, c_ctx, m, v):
    d = c_ctx.shape[1]

    def body(p_ref, c_ref, m_ref, v_ref, g_ref, d_ref, m2_ref, v2_ref):
        tot = p_ref[0:1, :] + p_ref[1:2, :]
        tot = tot + p_ref[2:3, :]
        tot = tot + p_ref[3:4, :]
        cv = c_ref[...]
        sg = _sigmoid(cv)
        g = tot * (sg * (1.0 + cv * (1.0 - sg)))
        g_ref[...] = g
        d_ref[...], m2_ref[...], v2_ref[...] = _adamw_math(cv, g, m_ref[...], v_ref[...])

    return _pcall(
        body, name="cctx_update",
        out_shape=[jax.ShapeDtypeStruct((1, d), F32)] * 4,
    )(parts, c_ctx, m, v)


WEIGHT_NAMES = (
    "c_ctx", "l0_norm1_g", "l0_w_mod", "l0_b_mod", "l0_w_in", "l0_q_norm_g", "l0_k_norm_g", "l0_conv_w", "l0_w_out",
    "l0_norm2_g", "l0_w_gate", "l0_w_up", "l0_w_down", "l1_norm1_g", "l1_w_mod", "l1_b_mod", "l1_w_in",
    "l1_q_norm_g", "l1_k_norm_g", "l1_sink", "l1_pool_w", "l1_pool_scale", "l1_w_out", "l1_norm2_g", "l1_w_gate",
    "l1_w_up", "l1_w_down", "final_norm_g")
BIG_NAMES = ("w_in", "w_out", "w_gate", "w_up", "w_down")
SMALL_SLOTS = tuple(
    [(f"l{l}_{nm}", par and f"l{l}_{nm}") for l in (0, 1)
     for nm, par in (("dmod_lat", False), ("dmod_ctx", False), ("b_mod", True), ("norm1_g", True), ("norm2_g", True),
                     ("q_norm_g", True), ("k_norm_g", True))]
    + [("l0_conv_w", None), ("l1_sink", "l1_sink"), ("l1_pool_w", "l1_pool_w"), ("l1_pool_scale", "l1_pool_scale"),
       ("final_norm_g", "final_norm_g")])


def _pack(values, sizes):
    parts = []
    for (name, _), size in zip(SMALL_SLOTS, sizes):
        v = values.get(name)
        padded = -(-size // 128) * 128
        v = jnp.zeros((padded,), F32) if v is None else jnp.pad(v.reshape(-1).astype(F32), (0, padded - size))
        parts.append(v)
    total = sum(p.shape[0] for p in parts)
    parts.append(jnp.zeros((-(-total // 1024) * 1024 - total,), F32))
    return jnp.concatenate(parts).reshape(-1, 128)


def _offsets(sizes):
    offs, o = {}, 0
    for (name, _), size in zip(SMALL_SLOTS, sizes):
        offs[name] = (o, size)
        o += -(-size // 128) * 128
    return offs


def kernel(x, c, ctx, c_ctx, l0_norm1_g, l0_w_mod, l0_b_mod, l0_w_in, l0_q_norm_g, l0_k_norm_g, l0_conv_w, l0_w_out, l0_norm2_g, l0_w_gate, l0_w_up, l0_w_down, l1_norm1_g, l1_w_mod, l1_b_mod, l1_w_in, l1_q_norm_g, l1_k_norm_g, l1_sink, l1_pool_w, l1_pool_scale, l1_w_out, l1_norm2_g, l1_w_gate, l1_w_up, l1_w_down, final_norm_g, loss_target, m_c_ctx, m_l0_norm1_g, m_l0_w_mod, m_l0_b_mod, m_l0_w_in, m_l0_q_norm_g, m_l0_k_norm_g, m_l0_conv_w, m_l0_w_out, m_l0_norm2_g, m_l0_w_gate, m_l0_w_up, m_l0_w_down, m_l1_norm1_g, m_l1_w_mod, m_l1_b_mod, m_l1_w_in, m_l1_q_norm_g, m_l1_k_norm_g, m_l1_sink, m_l1_pool_w, m_l1_pool_scale, m_l1_w_out, m_l1_norm2_g, m_l1_w_gate, m_l1_w_up, m_l1_w_down, m_final_norm_g, v_c_ctx, v_l0_norm1_g, v_l0_w_mod, v_l0_b_mod, v_l0_w_in, v_l0_q_norm_g, v_l0_k_norm_g, v_l0_conv_w, v_l0_w_out, v_l0_norm2_g, v_l0_w_gate, v_l0_w_up, v_l0_w_down, v_l1_norm1_g, v_l1_w_mod, v_l1_b_mod, v_l1_w_in, v_l1_q_norm_g, v_l1_k_norm_g, v_l1_sink, v_l1_pool_w, v_l1_pool_scale, v_l1_w_out, v_l1_norm2_g, v_l1_w_gate, v_l1_w_up, v_l1_w_down, v_final_norm_g):
    a = dict(locals())
    xi, yi, ci = lax.axis_index("x"), lax.axis_index("y"), lax.axis_index("c")
    p = 2 * xi + yi
    me = 4 * xi + 2 * yi + ci
    d = x.shape[-1]
    conv_cols = l0_conv_w.shape[1]

    row0 = jnp.concatenate([c, jnp.pad(l0_conv_w, ((0, 0), (0, d - conv_cols))), jnp.zeros((4, d), F32)], axis=0)
    gathered = all_gather8(row0, "gather_cond").reshape(N_DEV, 8, d)
    c_all = gathered[:, 0]
    conv_w = gathered[0::2, 1:4, :conv_cols].transpose(1, 0, 2).reshape(3, N_SHARD * conv_cols)
    c16_fwd = jnp.concatenate([c_all, c_ctx[None], jnp.zeros((7, d), F32)], axis=0)
    c16_bwd = jnp.concatenate([c_all, jnp.broadcast_to(c_ctx[None], (8, d))], axis=0)

    ns_mod = l0_w_mod.shape[1]
    mod_parts = [mod_fwd(c16_fwd, a[f"l{l}_w_mod"], lax.dynamic_slice(a[f"l{l}_b_mod"], (p * ns_mod,), (ns_mod,))[None],
                         f"l{l}_mod_fwd") for l in (0, 1)]
    modg = all_gather8(jnp.concatenate(mod_parts, axis=0), "gather_mod").reshape(N_DEV, 2, 16, ns_mod)[0::2]
    mod_full = modg.transpose(1, 2, 0, 3).reshape(2, 16, N_SHARD * ns_mod)
    mods = []
    for l in (0, 1):
        lat = lax.dynamic_index_in_dim(mod_full[l], me, axis=0, keepdims=False)
        cx = mod_full[l, 8]
        mods.append({nm: jnp.stack([cx[j * d:(j + 1) * d], lat[j * d:(j + 1) * d]])[:, None, :]
                     for j, nm in enumerate(MOD_NAMES)})

    placed = {(l, nm): place_shard(a[f"l{l}_{nm}"], p, f"l{l}_{nm}_place") for l in (0, 1) for nm in BIG_NAMES}
    ready = {(0, "w_in"): gather_weight_shards([placed[0, "w_in"]], mod_full)[0]}
    groups = (((0, "w_out"), (0, "w_gate"), (0, "w_up")), ((0, "w_down"),), ((1, "w_in"), (1, "w_out")),
              ((1, "w_gate"), (1, "w_up"), (1, "w_down")))
    staged = len(groups) - 1
    flights, after = [], ready[0, "w_in"]
    for gi, grp in enumerate(groups):
        first_copies = gather_half_copies if gi == staged else gather_copies
        flights.append(split_start(f"gather_start_{gi}", first_copies(len(grp)), 3 * len(grp),
                                   [placed[k] for k in grp], after))
        after = flights[-1][3]

    def forward_staged(after):
        n = len(groups[staged])
        send_sems, recv_sems, in_flight, _ = flights[staged]
        landed = split_wait(f"gather_wait_{staged}", gather_half_copies(n), send_sems, recv_sems, in_flight, after)
        flights[staged] = split_start("gather_forward_start", forward_half_copies(n), 3 * n, landed, after)
        return flights[staged][3]

    def weight_of(l):
        def weight(nm, after):
            if (l, nm) not in ready:
                gi = next(i for i, grp in enumerate(groups) if (l, nm) in grp)
                send_sems, recv_sems, in_flight, _ = flights[gi]
                name, copies = ("gather_forward_wait", forward_half_copies) if gi == staged else (
                    f"gather_wait_{gi}", gather_copies)
                landed = split_wait(name, copies(len(groups[gi])), send_sems, recv_sems, in_flight, after)
                ready.update(zip(groups[gi], landed))
            return ready[l, nm]
        return weight

    layers = [dict(norm1_g=a[f"l{l}_norm1_g"][None], norm2_g=a[f"l{l}_norm2_g"][None],
                   q_g=a[f"l{l}_q_norm_g"][None], k_g=a[f"l{l}_k_norm_g"][None]) for l in (0, 1)]
    layers[0].update(conv_w=conv_w, sink=jnp.zeros((Q_HEADS,), F32))
    layers[1].update(sink=l1_sink, pool_w=l1_pool_w, pool_scale=l1_pool_scale[None])

    ctx_len = ctx.shape[1]
    h = jnp.concatenate([ctx[0], x[0]], axis=0)
    tabs = _rope_tables(h.shape[0], ctx_len)
    h, saved0 = _layer_fwd(h, layers[0], weight_of(0), _tie(mods[0], "shift1", flights[-1][3]), tabs, ctx_len, False,
                           "l0_")
    h, saved1 = _layer_fwd(h, layers[1], weight_of(1), mods[1], tabs, ctx_len, True, "l1_", forward_staged)
    loss_tile, dh, dfg = final_loss(h, final_norm_g[None], loss_target[0])

    keys = [[f"l{l}_{nm}" for nm in BIG_NAMES] for l in (0, 1)]
    n_big = len(BIG_NAMES)
    dh, ffn_big, ffn_small, ffn_mod = _layer_bwd_ffn(dh, saved1, layers[1], mods[1], "l1_")
    dh, mix_big, mix_small, mix_mod = _layer_bwd_mixer(dh, saved1, layers[1], mods[1], tabs, ctx_len, True, "l1_")
    bigs, smalls, dmods = [None, {**ffn_big, **mix_big}], [None, {**ffn_small, **mix_small}], [None, None]
    dmods[1] = _dmod_rows(mix_mod, ffn_mod)
    grads1 = [bigs[1][nm] for nm in BIG_NAMES]
    lands = [lax.empty((g.shape[0], g.shape[1] // 2, g.shape[2]), g.dtype) for g in grads1]
    send_a, recv_a, fly_a, token_a = split_start("l1_sibling_halves_start", sibling_half_copies(n_big), n_big,
                                                 grads1 + lands, dh)
    dh, ffn_big, ffn_small, ffn_mod = _layer_bwd_ffn(dh, saved0, layers[0], _tie(mods[0], "gate2", token_a), "l0_")
    fly_a = split_wait("l1_sibling_halves_wait", sibling_half_copies(n_big), send_a, recv_a, fly_a, dh)
    chip_sums1 = [add_own_half(g, r, ci, k + "_add_halves")
                  for k, g, r in zip(keys[1], fly_a[:n_big], fly_a[n_big:])]
    lands = [lax.empty((3,) + s.shape[1:], s.dtype) for s in chip_sums1]
    send_b, recv_b, fly_b, token_b = split_start("l1_chip_quarters_start", chip_quarter_copies(n_big), 3 * n_big,
                                                 chip_sums1 + lands, dh)
    ffn_names = ("w_gate", "w_up", "w_down")
    n_ffn = len(ffn_names)
    ffn_keys = ["l0_" + nm for nm in ffn_names]
    grads0f = [ffn_big[nm] for nm in ffn_names]
    lands = [lax.empty((g.shape[0], g.shape[1] // 2, g.shape[2]), g.dtype) for g in grads0f]
    send_c, recv_c, fly_c, token_c = split_start("l0_ffn_sibling_halves_start", sibling_half_copies(n_ffn), n_ffn,
                                                 grads0f + lands, token_b)
    ffn_flight = []

    def between(dq):
        landed = split_wait("l0_ffn_sibling_halves_wait", sibling_half_copies(n_ffn), send_c, recv_c, fly_c, dq)
        sums = [add_own_half(g, r, ci, k + "_add_halves") for k, g, r in zip(ffn_keys, landed[:n_ffn], landed[n_ffn:])]
        slots = [lax.empty((3,) + s.shape[1:], s.dtype) for s in sums]
        ffn_flight.extend(split_start("l0_ffn_chip_quarters_start", chip_quarter_copies(n_ffn), 3 * n_ffn,
                                      sums + slots, dq))
        return ffn_flight[3]

    dh, mix_big, mix_small, mix_mod = _layer_bwd_mixer(dh, saved0, layers[0], _tie(mods[0], "gate1", token_c), tabs,
                                                       ctx_len, False, "l0_", between)
    fly_b = split_wait("l1_chip_quarters_wait", chip_quarter_copies(n_big), send_b, recv_b, fly_b, dh)
    fly_d = split_wait("l0_ffn_chip_quarters_wait", chip_quarter_copies(n_ffn), ffn_flight[0], ffn_flight[1],
                       ffn_flight[2], dh)
    bigs[0], smalls[0], dmods[0] = {**ffn_big, **mix_big}, {**ffn_small, **mix_small}, _dmod_rows(mix_mod, ffn_mod)
    dx = dh[ctx_len:]
    chip_sums = dict(zip(keys[1] + ffn_keys, list(fly_b[:n_big]) + list(fly_d[:n_ffn])))
    quarters = dict(zip(keys[1] + ffn_keys, list(fly_b[n_big:]) + list(fly_d[n_ffn:])))

    partial = {"l0_conv_w": smalls[0]["conv_w"], "l1_sink": smalls[1]["sink"], "l1_pool_w": smalls[1]["pool_w"],
               "l1_pool_scale": smalls[1]["pool_scale"], "final_norm_g": dfg}
    for l in (0, 1):
        partial.update({f"l{l}_dmod_lat": dmods[l][1], f"l{l}_dmod_ctx": dmods[l][0],
                        f"l{l}_b_mod": dmods[l][0] + dmods[l][1], f"l{l}_norm1_g": smalls[l]["norm1_g"],
                        f"l{l}_norm2_g": smalls[l]["norm2_g"], f"l{l}_q_norm_g": smalls[l]["q_g"],
                        f"l{l}_k_norm_g": smalls[l]["k_g"]})
    sizes = [int(np.prod(partial[name].shape)) for name, _ in SMALL_SLOTS]
    offs = _offsets(sizes)

    last_keys = ["l0_w_in", "l0_w_out"]
    grads_last = [bigs[0]["w_in"], bigs[0]["w_out"]]
    lands = [lax.empty((g.shape[0], g.shape[1] // 2, g.shape[2]), g.dtype) for g in grads_last]
    send_e, recv_e, fly_e, token_e = split_start("l0_last_sibling_halves_start", sibling_half_copies(2), 2,
                                                 grads_last + lands, dh)
    gpack = _pack(partial, sizes) + token_e[0:1, 0:1]
    rows = gpack.shape[0]
    small_all = all_gather8(gpack, "gather_small").reshape(N_DEV, rows, 128)
    packs = [_pack({name: a[pre + par] for name, par in SMALL_SLOTS if par}, sizes) for pre in ("", "m_", "v_")]
    small_out = [t.reshape(-1) for t in small_update(small_all, *packs)]
    flat_all = small_all.reshape(N_DEV, rows * 128)

    def slot(flat, name, shape):
        o, size = offs[name]
        return flat[o:o + size].reshape(shape)

    results = {}
    for name, par in SMALL_SLOTS:
        if par:
            results[par] = tuple(slot(t, name, a[par].shape) for t in small_out)

    ctx_rows = []
    for l in (0, 1):
        o_lat, o_ctx = offs[f"l{l}_dmod_lat"][0], offs[f"l{l}_dmod_ctx"][0]
        lat = lax.dynamic_slice(flat_all, (0, o_lat + p * ns_mod), (N_DEV, ns_mod))
        cxr = lax.dynamic_slice(flat_all, (0, o_ctx + p * ns_mod), (N_DEV, ns_mod))
        ctx_rows.append(cxr)
        results[f"l{l}_w_mod"] = tuple(wmod_update(c16_bwd, jnp.concatenate([lat, cxr], axis=0), a[f"l{l}_w_mod"],
                                                   a[f"m_l{l}_w_mod"], a[f"v_l{l}_w_mod"], f"l{l}_w_mod_update"))
    part = cctx_partial(ctx_rows[0], l0_w_mod, ctx_rows[1], l1_w_mod)
    parts4 = all_gather8(part, "gather_cctx").reshape(N_DEV, 8, d)[0::2, 0]
    results["c_ctx"] = tuple(t[0] for t in cctx_update(parts4, c_ctx[None], m_c_ctx[None], v_c_ctx[None]))

    gconv = lax.dynamic_slice(slot(small_out[0], "l0_conv_w", (3, N_SHARD * conv_cols)), (0, p * conv_cols),
                              (3, conv_cols))
    results["l0_conv_w"] = tuple(adamw(l0_conv_w, gconv, m_l0_conv_w, v_l0_conv_w, "l0_conv_w_adamw"))

    fly_e = split_wait("l0_last_sibling_halves_wait", sibling_half_copies(2), send_e, recv_e, fly_e,
                       results["c_ctx"][0])
    sums_last = [add_own_half(g, r, ci, k + "_add_halves") for k, g, r in zip(last_keys, fly_e[:2], fly_e[2:])]
    slots = [lax.empty((3,) + s.shape[1:], s.dtype) for s in sums_last]
    send_f, recv_f, fly_f, token_f = split_start("l0_last_chip_quarters_start", chip_quarter_copies(2), 6,
                                                 sums_last + slots, sums_last[0])

    def finish(names, tag, deps):
        halves = [add_quarters(chip_sums[k], quarters[k], p, ci, k + "_add_quarters", deps) for k in names]
        for k, g in zip(names, exchange_final_halves(halves, "exchange_final_halves_" + tag)):
            results[k] = tuple(adamw(a[k], g, a["m_" + k], a["v_" + k], k + "_adamw"))

    done_keys = keys[1] + ffn_keys
    finish(done_keys, "first", (token_f,))
    fly_f = split_wait("l0_last_chip_quarters_wait", chip_quarter_copies(2), send_f, recv_f, fly_f,
                       results[done_keys[-1]][1])
    chip_sums.update(zip(last_keys, fly_f[:2]))
    quarters.update(zip(last_keys, fly_f[2:]))
    finish(last_keys, "last", ())

    loss = lax.psum(loss_tile[0, 0], ("x", "y", "c"))
    out = [loss, dx[None]]
    for j in range(4):
        out += [results[k][j] for k in WEIGHT_NAMES]
    return tuple(out)
```

```python
import numpy as np
import jax
import jax.numpy as jnp
from jax import lax
from jax.experimental import pallas as pl
from jax.experimental.pallas import tpu as pltpu

F32 = jnp.float32
BF16 = jnp.bfloat16
MESH = pl.DeviceIdType.MESH

HEAD_DIM = 128
Q_HEADS = 12
KV_HEADS = 4
GROUPS = Q_HEADS // KV_HEADS
ATT_WIDTH = Q_HEADS * HEAD_DIM
KV_WIDTH = KV_HEADS * HEAD_DIM
QKV_WIDTH = ATT_WIDTH + 2 * KV_WIDTH
AUX_WIDTH = 512
AUX_GROUPS = 4
POOL_WINDOWS = (2, 4, 8, 16)
WINDOW = 128
GRID_W = 64
ROPE_THETA = 10000.0
N_MOD = 6
EPS = 1e-6
NEG_INF = -1e30
ATT_SCALE = HEAD_DIM ** -0.5
N_SHARD = 4
N_DEV = 8

ADAM_LR = 0.001
ADAM_B1 = 0.9
ADAM_B2 = 0.999
ADAM_EPS = 1e-08
ADAM_WD = 0.01
ADAM_STEP = 10

TM = 256
TM_MATMUL = 544
BAND = TM + 2 * WINDOW
VMEM_LIMIT = 56 * 1024 * 1024

NT_DIMS = (((1,), (1,)), ((), ()))
TN_DIMS = (((0,), (0,)), ((), ()))


def _pcall(body, **kw):
    return pl.pallas_call(body, **kw)


def _params(*sem):
    return pltpu.CompilerParams(dimension_semantics=sem, vmem_limit_bytes=VMEM_LIMIT)


def _sel(i):
    return jnp.minimum(i, 1)


def _sel_spec(d):
    return pl.BlockSpec((None, 1, d), lambda i: (_sel(i), 0, 0))


def _dot(a, b):
    return jnp.dot(a, b, preferred_element_type=F32)


def _dot_nt(a, b):
    return lax.dot_general(a, b, NT_DIMS, preferred_element_type=F32)


def _dot_tn(a, b):
    return lax.dot_general(a, b, TN_DIMS, preferred_element_type=F32)


def _sigmoid(x):
    return 0.5 * jnp.tanh(0.5 * x) + 0.5


def norm_mod_fwd(h, g, shift, scale, name):
    t, d = h.shape

    def body(h_ref, g_ref, sh_ref, sc_ref, o_ref):
        x = h_ref[...]
        r = lax.rsqrt(jnp.mean(x * x, axis=-1, keepdims=True) + EPS)
        y = x * r * g_ref[...]
        o_ref[...] = (y * (1.0 + sc_ref[...]) + sh_ref[...]).astype(BF16)

    return _pcall(
        body, name=name, grid=(t // TM,),
        in_specs=[pl.BlockSpec((TM, d), lambda i: (i, 0)), pl.BlockSpec((1, d), lambda i: (0, 0)),
                  _sel_spec(d), _sel_spec(d)],
        out_specs=pl.BlockSpec((TM, d), lambda i: (i, 0)),
        out_shape=jax.ShapeDtypeStruct((t, d), BF16),
        compiler_params=_params("parallel"),
    )(h, g, shift, scale)


def norm_mod_bwd(dy, h, g, scale, dres, name, latent_only=False):
    t, d = h.shape

    def body(dy_ref, h_ref, g_ref, sc_ref, dres_ref, dh_ref, dsh_ref, dsc_ref, dg_ref):
        i = pl.program_id(0)

        @pl.when(i == 0)
        def _():
            dsh_ref[...] = jnp.zeros_like(dsh_ref)
            dsc_ref[...] = jnp.zeros_like(dsc_ref)
            dg_ref[...] = jnp.zeros_like(dg_ref)

        x = h_ref[...]
        gv = g_ref[...]
        r = lax.rsqrt(jnp.mean(x * x, axis=-1, keepdims=True) + EPS)
        xhat = x * r
        dyv = dy_ref[...]
        s = _sel(i)
        dsh_ref[pl.ds(s, 1)] += jnp.sum(dyv, axis=0, keepdims=True)[None]
        dsc_ref[pl.ds(s, 1)] += jnp.sum(dyv * xhat * gv, axis=0, keepdims=True)[None]
        tt = dyv * (1.0 + sc_ref[...])
        dg_ref[...] += jnp.sum(tt * xhat, axis=0, keepdims=True)
        dxhat = tt * gv
        dx = r * (dxhat - xhat * jnp.mean(dxhat * xhat, axis=-1, keepdims=True))
        dh_ref[...] = dx + dres_ref[...]

    row = pl.BlockSpec((TM, d), lambda i: (i, 0))
    acc2 = pl.BlockSpec((2, 1, d), lambda i: (0, 0, 0))
    return _pcall(
        body, name=name, grid=(t // TM,),
        in_specs=[row, row, pl.BlockSpec((1, d), lambda i: (0, 0)), _sel_spec(d), row],
        out_specs=[pl.BlockSpec((TM, d), lambda i: (jnp.maximum(i - 1, 0), 0)) if latent_only else row, acc2, acc2,
                   pl.BlockSpec((1, d), lambda i: (0, 0))],
        out_shape=[jax.ShapeDtypeStruct((t - TM if latent_only else t, d), F32), jax.ShapeDtypeStruct((2, 1, d), F32),
                   jax.ShapeDtypeStruct((2, 1, d), F32), jax.ShapeDtypeStruct((1, d), F32)],
        compiler_params=_params("arbitrary"),
    )(dy, h, g, scale, dres)


def gate_bwd(dh, f, gate, name):
    t, d = dh.shape

    def body(dh_ref, f_ref, gt_ref, df_ref, dgt_ref):
        i = pl.program_id(0)

        @pl.when(i == 0)
        def _():
            dgt_ref[...] = jnp.zeros_like(dgt_ref)

        dhv = dh_ref[...]
        df_ref[...] = (dhv * gt_ref[...]).astype(BF16)
        dgt_ref[pl.ds(_sel(i), 1)] += jnp.sum(dhv * f_ref[...].astype(F32), axis=0, keepdims=True)[None]

    row = pl.BlockSpec((TM, d), lambda i: (i, 0))
    return _pcall(
        body, name=name, grid=(t // TM,),
        in_specs=[row, row, _sel_spec(d)],
        out_specs=[row, pl.BlockSpec((2, 1, d), lambda i: (0, 0, 0))],
        out_shape=[jax.ShapeDtypeStruct((t, d), BF16), jax.ShapeDtypeStruct((2, 1, d), F32)],
        compiler_params=_params("arbitrary"),
    )(dh, f, gate)


def final_loss(h, fg, target):
    t, d = h.shape

    def body(h_ref, g_ref, tg_ref, loss_ref, dh_ref, dg_ref):
        i = pl.program_id(0)

        @pl.when(i == 0)
        def _():
            loss_ref[...] = jnp.zeros_like(loss_ref)
            dg_ref[...] = jnp.zeros_like(dg_ref)
            dh_ref[...] = jnp.zeros_like(dh_ref)

        @pl.when(i > 0)
        def _():
            x = h_ref[...]
            gv = g_ref[...]
            r = lax.rsqrt(jnp.mean(x * x, axis=-1, keepdims=True) + EPS)
            xhat = x * r
            diff = xhat * gv - tg_ref[...]
            loss_ref[...] += 0.5 * jnp.sum(jnp.mean(diff * diff, axis=-1, keepdims=True), axis=0, keepdims=True)
            dout = diff * (1.0 / d)
            dg_ref[...] += jnp.sum(dout * xhat, axis=0, keepdims=True)
            dxhat = dout * gv
            dh_ref[...] = r * (dxhat - xhat * jnp.mean(dxhat * xhat, axis=-1, keepdims=True))

    row = pl.BlockSpec((TM, d), lambda i: (i, 0))
    return _pcall(
        body, name="final_loss", grid=(t // TM,),
        in_specs=[row, pl.BlockSpec((1, d), lambda i: (0, 0)),
                  pl.BlockSpec((TM, d), lambda i: (jnp.maximum(i - 1, 0), 0))],
        out_specs=[pl.BlockSpec((8, 128), lambda i: (0, 0)), row, pl.BlockSpec((1, d), lambda i: (0, 0))],
        out_shape=[jax.ShapeDtypeStruct((8, 128), F32), jax.ShapeDtypeStruct((t, d), F32),
                   jax.ShapeDtypeStruct((1, d), F32)],
        compiler_params=_params("arbitrary"),
    )(h, fg, target)


def _matmul_rows(t):
    return TM_MATMUL if t % TM_MATMUL == 0 else TM


def _w_nn_spec(w, k):
    if w.shape[1] == k:
        ns = w.shape[2]
        return w, pl.BlockSpec((None, k, ns), lambda j, i: (j, 0, 0)), w.shape[0], ns
    w2 = w.reshape(w.shape[0] * w.shape[1], w.shape[2])
    tn = 1024 if w2.shape[1] % 1024 == 0 else w2.shape[1]
    return w2, pl.BlockSpec((k, tn), lambda j, i: (0, j)), w2.shape[1] // tn, tn


def mm_nn(a, w, name, out_dtype=F32):
    t, k = a.shape
    w, wspec, nb, tn = _w_nn_spec(w, k)
    tm = _matmul_rows(t)

    def body(a_ref, w_ref, o_ref):
        o_ref[...] = _dot(a_ref[...], w_ref[...]).astype(o_ref.dtype)

    return _pcall(
        body, name=name, grid=(nb, t // tm),
        in_specs=[pl.BlockSpec((tm, k), lambda j, i: (i, 0)), wspec],
        out_specs=pl.BlockSpec((tm, tn), lambda j, i: (i, j)),
        out_shape=jax.ShapeDtypeStruct((t, nb * tn), out_dtype),
        compiler_params=_params("parallel", "parallel"),
    )(a, w)


def mm_nn_residual(a, w, h, gate, name):
    t, k = a.shape
    w, wspec, nb, tn = _w_nn_spec(w, k)
    tm = _matmul_rows(t)

    def body(a_ref, w_ref, h_ref, gt_ref, y_ref, o_ref):
        y = _dot(a_ref[...], w_ref[...])
        y_ref[...] = y.astype(BF16)
        row = pl.program_id(1) * tm + lax.broadcasted_iota(jnp.int32, (tm, tn), 0)
        o_ref[...] = h_ref[...] + jnp.where(row < TM, gt_ref[0], gt_ref[1]) * y

    out = pl.BlockSpec((tm, tn), lambda j, i: (i, j))
    return _pcall(
        body, name=name, grid=(nb, t // tm),
        in_specs=[pl.BlockSpec((tm, k), lambda j, i: (i, 0)), wspec, out,
                  pl.BlockSpec((2, 1, tn), lambda j, i: (0, 0, j))],
        out_specs=[out, out],
        out_shape=[jax.ShapeDtypeStruct((t, nb * tn), BF16), jax.ShapeDtypeStruct((t, nb * tn), F32)],
        compiler_params=_params("parallel", "parallel"),
    )(a, w, h, gate)


def mm_swiglu(a, wg, wu, name):
    t, k = a.shape
    s, _, ns = wg.shape

    def body(a_ref, wg_ref, wu_ref, dg_ref, du_ref, act_ref):
        av = a_ref[...]
        g = _dot(av, wg_ref[...])
        u = _dot(av, wu_ref[...])
        sg = _sigmoid(g)
        silu = g * sg
        dg_ref[...] = (u * (sg * (1.0 + g * (1.0 - sg)))).astype(BF16)
        du_ref[...] = silu.astype(BF16)
        act_ref[...] = (silu * u).astype(BF16)

    tm = _matmul_rows(t)
    wspec = pl.BlockSpec((None, k, ns), lambda j, i: (j, 0, 0))
    out = pl.BlockSpec((tm, ns), lambda j, i: (i, j))
    return _pcall(
        body, name=name, grid=(s, t // tm),
        in_specs=[pl.BlockSpec((tm, k), lambda j, i: (i, 0)), wspec, wspec],
        out_specs=[out, out, out],
        out_shape=[jax.ShapeDtypeStruct((t, s * ns), BF16)] * 3,
        compiler_params=_params("parallel", "parallel"),
    )(a, wg, wu)


def mm_nt_cols(dy, w, name, add=None):
    t, n = dy.shape
    s, k, ns = w.shape
    tk = 1024

    def body(*refs):
        if add is None:
            dy_ref, w_ref, o_ref = refs
        else:
            dy_ref, w_ref, add_ref, o_ref = refs
        acc = _dot_nt(dy_ref[:, 0:ns], w_ref[0])
        for sh in range(1, s):
            acc += _dot_nt(dy_ref[:, sh * ns:(sh + 1) * ns], w_ref[sh])
        if add is not None:
            acc += add_ref[...]
        o_ref[...] = acc

    tm = _matmul_rows(t)
    out = pl.BlockSpec((tm, tk), lambda j, i: (i, j))
    in_specs = [pl.BlockSpec((tm, n), lambda j, i: (i, 0)), pl.BlockSpec((s, tk, ns), lambda j, i: (0, j, 0))]
    args = [dy, w]
    if add is not None:
        in_specs.append(out)
        args.append(add)
    return _pcall(
        body, name=name, grid=(k // tk, t // tm),
        in_specs=in_specs, out_specs=out,
        out_shape=jax.ShapeDtypeStruct((t, k), F32),
        compiler_params=_params("parallel", "parallel"),
    )(*args)


def mm_nt_rows(dy, w, name):
    t, n = dy.shape
    s, ks, _ = w.shape

    tm = _matmul_rows(t)

    def body(dy_ref, w_ref, o_ref):
        o_ref[...] = _dot_nt(dy_ref[...], w_ref[...]).astype(BF16)

    return _pcall(
        body, name=name, grid=(s, t // tm),
        in_specs=[pl.BlockSpec((tm, n), lambda j, i: (i, 0)), pl.BlockSpec((None, ks, n), lambda j, i: (j, 0, 0))],
        out_specs=pl.BlockSpec((tm, ks), lambda j, i: (i, j)),
        out_shape=jax.ShapeDtypeStruct((t, s * ks), BF16),
        compiler_params=_params("parallel", "parallel"),
    )(dy, w)


def mm_nt_swiglu_bwd(df, wd, g, u, name):
    t, n = df.shape
    s, ks, _ = wd.shape

    def body(df_ref, w_ref, g_ref, u_ref, dg_ref, du_ref):
        da = _dot_nt(df_ref[...], w_ref[...])
        dg_ref[...] = (da * g_ref[...].astype(F32)).astype(BF16)
        du_ref[...] = (da * u_ref[...].astype(F32)).astype(BF16)

    tm = _matmul_rows(t)
    blk = pl.BlockSpec((tm, ks), lambda j, i: (i, j))
    return _pcall(
        body, name=name, grid=(s, t // tm),
        in_specs=[pl.BlockSpec((tm, n), lambda j, i: (i, 0)), pl.BlockSpec((None, ks, n), lambda j, i: (j, 0, 0)),
                  blk, blk],
        out_specs=[blk, blk],
        out_shape=[jax.ShapeDtypeStruct((t, s * ks), BF16)] * 2,
        compiler_params=_params("parallel", "parallel"),
    )(df, wd, g, u)


def mm_tn(x, dy, name, col_sharded):
    t, k = x.shape
    n = dy.shape[1]
    tk = next(c for c in (1024, 1408, 512, k) if k % c == 0)
    mc = t // 2 if (t // 2) % 128 == 0 else t
    n_m = t // mc
    if col_sharded:
        tn = n // N_SHARD
        out_shape = jax.ShapeDtypeStruct((N_SHARD, k, tn), BF16)
        out_spec = pl.BlockSpec((None, tk, tn), lambda j, kb, m: (j, kb, 0))
    else:
        tn = 1024
        out_shape = jax.ShapeDtypeStruct((k, n), BF16)
        out_spec = pl.BlockSpec((tk, tn), lambda j, kb, m: (kb, j))

    def body(x_ref, dy_ref, o_ref, acc_ref):
        m = pl.program_id(2)
        acc = _dot_tn(x_ref[...], dy_ref[...])

        @pl.when(m == 0)
        def _():
            acc_ref[...] = acc

        @pl.when((m > 0) & (m < n_m - 1))
        def _():
            acc_ref[...] += acc

        @pl.when(m == n_m - 1)
        def _():
            o_ref[...] = (acc if n_m == 1 else acc_ref[...] + acc).astype(BF16)

    out = _pcall(
        body, name=name, grid=(n // tn, k // tk, n_m),
        in_specs=[pl.BlockSpec((mc, tk), lambda j, kb, m: (m, kb)), pl.BlockSpec((mc, tn), lambda j, kb, m: (m, j))],
        out_specs=out_spec, out_shape=out_shape,
        scratch_shapes=[pltpu.VMEM((tk, tn), F32)],
        compiler_params=_params("parallel", "parallel", "arbitrary"),
    )(x, dy)
    return out if col_sharded else out.reshape(N_SHARD, k // N_SHARD, n)


def _swap32(y):
    right = pltpu.roll(y, 32, 1)
    left = pltpu.roll(y, 96, 1)
    lane = lax.broadcasted_iota(jnp.int32, y.shape, 1)
    return jnp.where((lane // 32) % 2 == 0, left, right)


def _rope_tables(t, ctx_len):
    s = t - ctx_len
    pos = np.arange(s)
    row = (pos // GRID_W).astype(np.float32)
    col = (pos % GRID_W).astype(np.float32)
    half = HEAD_DIM // 2
    inv = np.power(np.float32(ROPE_THETA), -np.arange(0, half, 2, dtype=np.float32) / np.float32(half))
    ar = row[:, None] * inv
    ac = col[:, None] * inv
    cos = np.concatenate([np.cos(ar), np.cos(ar), np.cos(ac), np.cos(ac)], axis=1)
    sin = np.concatenate([-np.sin(ar), np.sin(ar), -np.sin(ac), np.sin(ac)], axis=1)
    cos = np.concatenate([np.ones((ctx_len, HEAD_DIM), np.float32), cos.astype(np.float32)], axis=0)
    sin = np.concatenate([np.zeros((ctx_len, HEAD_DIM), np.float32), sin.astype(np.float32)], axis=0)
    return jnp.asarray(cos, F32), jnp.asarray(sin, F32)


def qk_prep_fwd(proj, qg, kg, cos, sin, name):
    t = proj.shape[0]

    def body(p_ref, qg_ref, kg_ref, c_ref, s_ref, q_ref, k_ref, v_ref):
        cv = c_ref[...]
        sv = s_ref[...]
        for hd in range(Q_HEADS + KV_HEADS):
            x = p_ref[:, hd * HEAD_DIM:(hd + 1) * HEAD_DIM]
            gv = qg_ref[...] if hd < Q_HEADS else kg_ref[...]
            y = x * lax.rsqrt(jnp.mean(x * x, axis=-1, keepdims=True) + EPS) * gv
            out = y * cv + _swap32(y) * sv
            if hd < Q_HEADS:
                q_ref[:, hd * HEAD_DIM:(hd + 1) * HEAD_DIM] = (out * ATT_SCALE).astype(BF16)
            else:
                k_ref[:, (hd - Q_HEADS) * HEAD_DIM:(hd - Q_HEADS + 1) * HEAD_DIM] = out.astype(BF16)
        v_ref[...] = p_ref[:, ATT_WIDTH + KV_WIDTH:QKV_WIDTH].astype(BF16)

    vec = pl.BlockSpec((1, HEAD_DIM), lambda i: (0, 0))
    tab = pl.BlockSpec((TM, HEAD_DIM), lambda i: (i, 0))
    return _pcall(
        body, name=name, grid=(t // TM,),
        in_specs=[pl.BlockSpec((TM, QKV_WIDTH), lambda i: (i, 0)), vec, vec, tab, tab],
        out_specs=[pl.BlockSpec((TM, ATT_WIDTH), lambda i: (i, 0)), pl.BlockSpec((TM, KV_WIDTH), lambda i: (i, 0)),
                   pl.BlockSpec((TM, KV_WIDTH), lambda i: (i, 0))],
        out_shape=[jax.ShapeDtypeStruct((t, ATT_WIDTH), BF16), jax.ShapeDtypeStruct((t, KV_WIDTH), BF16),
                   jax.ShapeDtypeStruct((t, KV_WIDTH), BF16)],
        compiler_params=_params("parallel"),
    )(proj, qg, kg, cos, sin)


def qk_prep_bwd(dq, dk, dv, daux, proj, qg, kg, cos, sin, name):
    t, n_in = proj.shape
    n_aux = daux.shape[0]

    def body(dq_ref, dk_ref, dv_ref, da_ref, p_ref, qg_ref, kg_ref, c_ref, s_ref, o_ref, dqg_ref, dkg_ref):
        @pl.when(pl.program_id(0) == 0)
        def _():
            dqg_ref[...] = jnp.zeros_like(dqg_ref)
            dkg_ref[...] = jnp.zeros_like(dkg_ref)

        cv = c_ref[...]
        sv = s_ref[...]
        for hd in range(Q_HEADS + KV_HEADS):
            cols = slice(hd * HEAD_DIM, (hd + 1) * HEAD_DIM)
            x = p_ref[:, cols]
            if hd < Q_HEADS:
                gv, dyr, dg_ref = qg_ref[...], dq_ref[:, cols] * ATT_SCALE, dqg_ref
            else:
                kc = slice((hd - Q_HEADS) * HEAD_DIM, (hd - Q_HEADS + 1) * HEAD_DIM)
                gv, dyr, dg_ref = kg_ref[...], dk_ref[:, kc], dkg_ref
            r = lax.rsqrt(jnp.mean(x * x, axis=-1, keepdims=True) + EPS)
            xhat = x * r
            dy = dyr * cv + _swap32(dyr * sv)
            dg_ref[...] += jnp.sum(dy * xhat, axis=0, keepdims=True)
            dxhat = dy * gv
            o_ref[:, cols] = (r * (dxhat - xhat * jnp.mean(dxhat * xhat, axis=-1, keepdims=True))).astype(BF16)
        o_ref[:, ATT_WIDTH + KV_WIDTH:QKV_WIDTH] = dv_ref[...].astype(BF16)
        for a in range(n_aux):
            o_ref[:, QKV_WIDTH + a * AUX_WIDTH:QKV_WIDTH + (a + 1) * AUX_WIDTH] = da_ref[a]

    vec = pl.BlockSpec((1, HEAD_DIM), lambda i: (0, 0))
    tab = pl.BlockSpec((TM, HEAD_DIM), lambda i: (i, 0))
    return _pcall(
        body, name=name, grid=(t // TM,),
        in_specs=[pl.BlockSpec((TM, ATT_WIDTH), lambda i: (i, 0)), pl.BlockSpec((TM, KV_WIDTH), lambda i: (i, 0)),
                  pl.BlockSpec((TM, KV_WIDTH), lambda i: (i, 0)),
                  pl.BlockSpec((n_aux, TM, AUX_WIDTH), lambda i: (0, i, 0)),
                  pl.BlockSpec((TM, QKV_WIDTH), lambda i: (i, 0)), vec, vec, tab, tab],
        out_specs=[pl.BlockSpec((TM, n_in), lambda i: (i, 0)), vec, vec],
        out_shape=[jax.ShapeDtypeStruct((t, n_in), BF16), jax.ShapeDtypeStruct((1, HEAD_DIM), F32),
                   jax.ShapeDtypeStruct((1, HEAD_DIM), F32)],
        compiler_params=_params("arbitrary"),
    )(dq, dk, dv, daux, proj, qg, kg, cos, sin)


def _band_start(i, t):
    return pl.multiple_of(jnp.clip(i * TM - WINDOW, 0, t - BAND), WINDOW)


def _band_mask(i, start, ctx_len):
    shape = (TM, ctx_len + BAND)
    col = lax.broadcasted_iota(jnp.int32, shape, 1)
    qrow = i * TM + lax.broadcasted_iota(jnp.int32, shape, 0)
    krow = start + col - ctx_len
    band_ok = (krow >= ctx_len) & (jnp.abs(krow - qrow) <= WINDOW)
    return (col < ctx_len) | band_ok


def attention_fwd(q, k, v, sink, ctx_len, windowed, name):
    t = q.shape[0]
    d_model = ATT_WIDTH + AUX_WIDTH
    gw = GROUPS * HEAD_DIM

    def one_head(qh, kk, vv, mask, sink_val):
        s = _dot_nt(qh, kk)
        if mask is not None:
            s = jnp.where(mask, s, NEG_INF)
        m = jnp.max(s, axis=-1, keepdims=True)
        if sink_val is not None:
            m = jnp.maximum(m, sink_val)
        p = jnp.exp(s - m)
        l = jnp.sum(p, axis=-1, keepdims=True)
        if sink_val is not None:
            l = l + jnp.exp(sink_val - m)
        o = _dot(p.astype(BF16), vv) / l
        return o, m + jnp.log(l)

    def body(sink_ref, q_ref, k_ref, v_ref, o_ref, lse_ref):
        kv = pl.program_id(0)
        i = pl.program_id(1)

        def run(kk, vv, mask):
            for g in range(GROUPS):
                sink_val = sink_ref[kv * GROUPS + g] if windowed else None
                o, lse = one_head(q_ref[:, g * HEAD_DIM:(g + 1) * HEAD_DIM], kk, vv, mask, sink_val)
                o_ref[:, g * HEAD_DIM:(g + 1) * HEAD_DIM] = o.astype(BF16)
                lse_ref[g] = lse

        @pl.when(i == 0)
        def _():
            if windowed:
                o_ref[...] = jnp.zeros_like(o_ref)
                lse_ref[...] = jnp.zeros_like(lse_ref)
            else:
                run(k_ref[0:ctx_len], v_ref[0:ctx_len], None)

        @pl.when(i > 0)
        def _():
            if windowed:
                start = _band_start(i, t)
                kk = jnp.concatenate([k_ref[0:ctx_len], k_ref[pl.ds(start, BAND)]], axis=0)
                vv = jnp.concatenate([v_ref[0:ctx_len], v_ref[pl.ds(start, BAND)]], axis=0)
                run(kk, vv, _band_mask(i, start, ctx_len))
            else:
                run(k_ref[...], v_ref[...], None)

    kvspec = pl.BlockSpec((t, HEAD_DIM), lambda kv, i: (0, kv))
    return _pcall(
        body, name=name, grid=(KV_HEADS, t // TM),
        in_specs=[pl.BlockSpec(memory_space=pltpu.SMEM), pl.BlockSpec((TM, gw), lambda kv, i: (i, kv)), kvspec, kvspec],
        out_specs=[pl.BlockSpec((TM, gw), lambda kv, i: (i, kv)),
                   pl.BlockSpec((GROUPS, TM, 1), lambda kv, i: (kv, i, 0))],
        out_shape=[jax.ShapeDtypeStruct((t, d_model), BF16), jax.ShapeDtypeStruct((Q_HEADS, t, 1), F32)],
        compiler_params=_params("parallel", "parallel"),
    )(sink, q, k, v)


def attention_bwd(q, k, v, cat, lse, dcat, sink, ctx_len, windowed, name):
    t = q.shape[0]
    gw = GROUPS * HEAD_DIM

    def body(sink_ref, q_ref, k_ref, v_ref, o_ref, lse_ref, do_ref, dq_ref, dk_ref, dv_ref, dsink_ref):
        kv = pl.program_id(0)
        i = pl.program_id(1)

        @pl.when(i == 0)
        def _():
            dk_ref[...] = jnp.zeros_like(dk_ref)
            dv_ref[...] = jnp.zeros_like(dv_ref)
            dsink_ref[...] = jnp.zeros_like(dsink_ref)

        def run(kk, vv, mask, accumulate):
            for g in range(GROUPS):
                cols = slice(g * HEAD_DIM, (g + 1) * HEAD_DIM)
                qh = q_ref[:, cols]
                dob = do_ref[:, cols]
                delta = jnp.sum(dob.astype(F32) * o_ref[:, cols].astype(F32), axis=-1, keepdims=True)
                lse_g = lse_ref[g]
                s = _dot_nt(qh, kk)
                if mask is not None:
                    s = jnp.where(mask, s, NEG_INF)
                p = jnp.exp(s - lse_g)
                dp = _dot_nt(dob, vv)
                ds = (p * (dp - delta)).astype(BF16)
                dq_ref[:, cols] = _dot(ds, kk)
                accumulate(_dot_tn(ds, qh), _dot_tn(p.astype(BF16), dob))
                if windowed:
                    p_sink = jnp.exp(sink_ref[kv * GROUPS + g] - lse_g)
                    dsink_ref[g:g + 1, :] += jnp.sum(-p_sink * delta, axis=0, keepdims=True)

        @pl.when(i == 0)
        def _():
            if windowed:
                dq_ref[...] = jnp.zeros_like(dq_ref)
            else:
                def acc_ctx(dkp, dvp):
                    dk_ref[0:ctx_len] += dkp
                    dv_ref[0:ctx_len] += dvp

                run(k_ref[0:ctx_len], v_ref[0:ctx_len], None, acc_ctx)

        @pl.when(i > 0)
        def _():
            if windowed:
                start = _band_start(i, t)
                kk = jnp.concatenate([k_ref[0:ctx_len], k_ref[pl.ds(start, BAND)]], axis=0)
                vv = jnp.concatenate([v_ref[0:ctx_len], v_ref[pl.ds(start, BAND)]], axis=0)

                def acc_band(dkp, dvp):
                    dk_ref[0:ctx_len] += dkp[0:ctx_len]
                    dv_ref[0:ctx_len] += dvp[0:ctx_len]
                    dk_ref[pl.ds(start, BAND)] += dkp[ctx_len:ctx_len + BAND]
                    dv_ref[pl.ds(start, BAND)] += dvp[ctx_len:ctx_len + BAND]

                run(kk, vv, _band_mask(i, start, ctx_len), acc_band)
            else:
                def acc_all(dkp, dvp):
                    dk_ref[...] += dkp
                    dv_ref[...] += dvp

                run(k_ref[...], v_ref[...], None, acc_all)

    kvspec = pl.BlockSpec((t, HEAD_DIM), lambda kv, i: (0, kv))
    grp = pl.BlockSpec((TM, gw), lambda kv, i: (i, kv))
    return _pcall(
        body, name=name, grid=(KV_HEADS, t // TM),
        in_specs=[pl.BlockSpec(memory_space=pltpu.SMEM), grp, kvspec, kvspec, grp,
                  pl.BlockSpec((GROUPS, TM, 1), lambda kv, i: (kv, i, 0)), grp],
        out_specs=[grp, kvspec, kvspec, pl.BlockSpec((None, 8, 128), lambda kv, i: (kv, 0, 0))],
        out_shape=[jax.ShapeDtypeStruct((t, ATT_WIDTH), F32), jax.ShapeDtypeStruct((t, KV_WIDTH), F32),
                   jax.ShapeDtypeStruct((t, KV_WIDTH), F32), jax.ShapeDtypeStruct((KV_HEADS, 8, 128), F32)],
        compiler_params=_params("parallel", "arbitrary"),
    )(sink, q, k, v, cat, lse, dcat)


def _segment_pos(t, ctx_len, width):
    row = lax.broadcasted_iota(jnp.int32, (t, width), 0)
    in_ctx = row < ctx_len
    return jnp.where(in_ctx, row, row - ctx_len), jnp.where(in_ctx, ctx_len, t - ctx_len)


def _shifted(x, offset, pos, seg_len):
    t = x.shape[0]
    moved = x if offset == 0 else pltpu.roll(x, (-offset) % t, 0)
    ok = (pos + offset >= 0) & (pos + offset < seg_len)
    return jnp.where(ok, moved, 0.0)


def conv_fwd(proj, conv_w, cat, ctx_len, name):
    t = proj.shape[0]
    base = QKV_WIDTH // 128
    per = AUX_WIDTH // 128

    def body(gb_ref, gc_ref, u_ref, w_ref, cat_in, o_ref):
        del cat_in
        pos, seg = _segment_pos(t, ctx_len, 128)
        z = gc_ref[...] * u_ref[...]
        conv = (w_ref[0:1, :] * _shifted(z, -1, pos, seg) + w_ref[1:2, :] * z
                + w_ref[2:3, :] * _shifted(z, 1, pos, seg))
        o_ref[...] = (gb_ref[...] * conv).astype(BF16)

    def col(off):
        return pl.BlockSpec((t, 128), lambda j: (0, base + off * per + j))

    return _pcall(
        body, name=name, grid=(per,),
        in_specs=[col(0), col(1), col(2), pl.BlockSpec((3, 128), lambda j: (0, j)),
                  pl.BlockSpec(memory_space=pl.ANY)],
        out_specs=pl.BlockSpec((t, 128), lambda j: (0, ATT_WIDTH // 128 + j)),
        out_shape=jax.ShapeDtypeStruct(cat.shape, BF16),
        input_output_aliases={4: 0},
        compiler_params=_params("parallel"),
    )(proj, proj, proj, conv_w, cat)


def conv_bwd(dcat, proj, conv_w, ctx_len, name):
    t = proj.shape[0]
    base = QKV_WIDTH // 128
    per = AUX_WIDTH // 128

    def body(do_ref, gb_ref, gc_ref, u_ref, w_ref, da_ref, dw_ref):
        pos, seg = _segment_pos(t, ctx_len, 128)
        gc = gc_ref[...]
        u = u_ref[...]
        z = gc * u
        zm = _shifted(z, -1, pos, seg)
        zp = _shifted(z, 1, pos, seg)
        conv = w_ref[0:1, :] * zm + w_ref[1:2, :] * z + w_ref[2:3, :] * zp
        dout = do_ref[...].astype(F32)
        da_ref[0] = (dout * conv).astype(BF16)
        dconv = dout * gb_ref[...]
        dw_ref[0:1, :] = jnp.sum(dconv * zm, axis=0, keepdims=True)
        dw_ref[1:2, :] = jnp.sum(dconv * z, axis=0, keepdims=True)
        dw_ref[2:3, :] = jnp.sum(dconv * zp, axis=0, keepdims=True)
        dz = (w_ref[1:2, :] * dconv + w_ref[0:1, :] * _shifted(dconv, 1, pos, seg)
              + w_ref[2:3, :] * _shifted(dconv, -1, pos, seg))
        da_ref[1] = (dz * u).astype(BF16)
        da_ref[2] = (dz * gc).astype(BF16)

    def col(off):
        return pl.BlockSpec((t, 128), lambda j: (0, base + off * per + j))

    return _pcall(
        body, name=name, grid=(per,),
        in_specs=[pl.BlockSpec((t, 128), lambda j: (0, ATT_WIDTH // 128 + j)), col(0), col(1), col(2),
                  pl.BlockSpec((3, 128), lambda j: (0, j))],
        out_specs=[pl.BlockSpec((3, t, 128), lambda j: (0, 0, j)), pl.BlockSpec((3, 128), lambda j: (0, j))],
        out_shape=[jax.ShapeDtypeStruct((3, t, AUX_WIDTH), BF16), jax.ShapeDtypeStruct((3, AUX_WIDTH), F32)],
        compiler_params=_params("parallel"),
    )(dcat, proj, proj, proj, conv_w)


def _pooled(u, w, pos, seg):
    lo = jnp.clip(pos - w // 2, 0, seg)
    hi = jnp.clip(pos - w // 2 + w, 0, seg)
    inv = 1.0 / (hi - lo).astype(F32)
    total = _shifted(u, -(w // 2), pos, seg)
    for o in range(-(w // 2) + 1, w // 2):
        total = total + _shifted(u, o, pos, seg)
    return total * inv - u, inv


def pool_fwd(proj, pool_w, pool_scale, cat, ctx_len, name):
    t = proj.shape[0]

    def body(u_ref, w_ref, sc_ref, cat_in, o_ref):
        del cat_in
        pos, seg = _segment_pos(t, ctx_len, 128)
        for g, w in enumerate(POOL_WINDOWS):
            cols = slice(g * 128, (g + 1) * 128)
            pooled, _ = _pooled(u_ref[:, cols], w, pos, seg)
            mixed = _dot(pooled.astype(BF16), w_ref[g].astype(BF16))
            o_ref[:, cols] = (mixed * sc_ref[:, cols]).astype(BF16)

    return _pcall(
        body, name=name, grid=(1,),
        in_specs=[pl.BlockSpec((t, AUX_WIDTH), lambda j: (0, QKV_WIDTH // AUX_WIDTH)),
                  pl.BlockSpec((AUX_GROUPS, 128, 128), lambda j: (0, 0, 0)),
                  pl.BlockSpec((1, AUX_WIDTH), lambda j: (0, 0)), pl.BlockSpec(memory_space=pl.ANY)],
        out_specs=pl.BlockSpec((t, AUX_WIDTH), lambda j: (0, ATT_WIDTH // AUX_WIDTH)),
        out_shape=jax.ShapeDtypeStruct(cat.shape, BF16),
        input_output_aliases={3: 0},
        compiler_params=_params("arbitrary"),
    )(proj, pool_w, pool_scale, cat)


def pool_bwd(dcat, proj, pool_w, pool_scale, ctx_len, name):
    t = proj.shape[0]

    def body(do_ref, u_ref, w_ref, sc_ref, da_ref, dw_ref, dsc_ref):
        pos, seg = _segment_pos(t, ctx_len, 128)
        for g, w in enumerate(POOL_WINDOWS):
            cols = slice(g * 128, (g + 1) * 128)
            pooled, inv = _pooled(u_ref[:, cols], w, pos, seg)
            pb = pooled.astype(BF16)
            wb = w_ref[g].astype(BF16)
            mixed = _dot(pb, wb)
            dout = do_ref[:, cols].astype(F32)
            dsc_ref[:, cols] = jnp.sum(dout * mixed, axis=0, keepdims=True)
            dmixed = (dout * sc_ref[:, cols]).astype(BF16)
            dw_ref[g] = _dot_tn(pb, dmixed)
            dpooled = _dot_nt(dmixed, wb)
            spread = dpooled * inv
            du = _shifted(spread, w // 2, pos, seg) - dpooled
            for o in range(-(w // 2) + 1, w // 2):
                du = du + _shifted(spread, -o, pos, seg)
            da_ref[0, :, cols] = du.astype(BF16)

    return _pcall(
        body, name=name, grid=(1,),
        in_specs=[pl.BlockSpec((t, AUX_WIDTH), lambda j: (0, ATT_WIDTH // AUX_WIDTH)),
                  pl.BlockSpec((t, AUX_WIDTH), lambda j: (0, QKV_WIDTH // AUX_WIDTH)),
                  pl.BlockSpec((AUX_GROUPS, 128, 128), lambda j: (0, 0, 0)),
                  pl.BlockSpec((1, AUX_WIDTH), lambda j: (0, 0))],
        out_specs=[pl.BlockSpec((1, t, AUX_WIDTH), lambda j: (0, 0, 0)),
                   pl.BlockSpec((AUX_GROUPS, 128, 128), lambda j: (0, 0, 0)),
                   pl.BlockSpec((1, AUX_WIDTH), lambda j: (0, 0))],
        out_shape=[jax.ShapeDtypeStruct((1, t, AUX_WIDTH), BF16), jax.ShapeDtypeStruct((AUX_GROUPS, 128, 128), F32),
                   jax.ShapeDtypeStruct((1, AUX_WIDTH), F32)],
        compiler_params=_params("arbitrary"),
    )(dcat, proj, pool_w, pool_scale)


MOD_NAMES = ("shift1", "scale1", "gate1", "shift2", "scale2", "gate2")


def _layer_fwd(h, lw, weight, mods, tabs, ctx_len, windowed, tag, after_mixer=None):
    cos, sin = tabs
    xn = norm_mod_fwd(h, lw["norm1_g"], mods["shift1"], mods["scale1"], tag + "norm1_fwd")
    lw["w_in"] = weight("w_in", xn)
    proj = mm_nn(xn, lw["w_in"], tag + "w_in_fwd")
    q, k, v = qk_prep_fwd(proj, lw["q_g"], lw["k_g"], cos, sin, tag + "qk_fwd")
    cat, lse = attention_fwd(q, k, v, lw["sink"], ctx_len, windowed, tag + "attn_fwd")
    if windowed:
        cat = pool_fwd(proj, lw["pool_w"], lw["pool_scale"], cat, ctx_len, tag + "pool_fwd")
    else:
        cat = conv_fwd(proj, lw["conv_w"], cat, ctx_len, tag + "conv_fwd")
    lw["w_out"] = weight("w_out", cat)
    gate1 = mods["gate1"] if after_mixer is None else mods["gate1"] + after_mixer(cat)[0, 0]
    y, h1 = mm_nn_residual(cat, lw["w_out"], h, gate1, tag + "w_out_fwd")
    hn = norm_mod_fwd(h1, lw["norm2_g"], mods["shift2"], mods["scale2"], tag + "norm2_fwd")
    lw["w_gate"], lw["w_up"] = weight("w_gate", hn), weight("w_up", hn)
    dact_dg, dact_du, act = mm_swiglu(hn, lw["w_gate"], lw["w_up"], tag + "ffn_in_fwd")
    lw["w_down"] = weight("w_down", act)
    f, h2 = mm_nn_residual(act, lw["w_down"], h1, mods["gate2"], tag + "w_down_fwd")
    saved = dict(h=h, xn=xn, proj=proj, q=q, k=k, v=v, cat=cat, lse=lse, y=y, h1=h1, hn=hn, dact_dg=dact_dg,
                 dact_du=dact_du, act=act, f=f)
    return h2, saved


def _layer_bwd_ffn(dh, sv, lw, mods, tag):
    big, small = {}, {}
    df, dgate2 = gate_bwd(dh, sv["f"], mods["gate2"], tag + "gate2_bwd")
    dgp, du = mm_nt_swiglu_bwd(df, lw["w_down"], sv["dact_dg"], sv["dact_du"], tag + "w_down_dgrad")
    big["w_down"] = mm_tn(sv["act"], df, tag + "w_down_wgrad", col_sharded=False)
    dhn = mm_nt_cols(dgp, lw["w_gate"], tag + "w_gate_dgrad")
    dhn = mm_nt_cols(du, lw["w_up"], tag + "w_up_dgrad", add=dhn)
    big["w_gate"] = mm_tn(sv["hn"], dgp, tag + "w_gate_wgrad", col_sharded=True)
    big["w_up"] = mm_tn(sv["hn"], du, tag + "w_up_wgrad", col_sharded=True)
    dh1, dshift2, dscale2, small["norm2_g"] = norm_mod_bwd(dhn, sv["h1"], lw["norm2_g"], mods["scale2"], dh,
                                                           tag + "norm2_bwd")
    return dh1, big, small, (dshift2, dscale2, dgate2)


def _layer_bwd_mixer(dh1, sv, lw, mods, tabs, ctx_len, windowed, tag, between=None, latent_only=False):
    cos, sin = tabs
    big, small = {}, {}
    dy, dgate1 = gate_bwd(dh1, sv["y"], mods["gate1"], tag + "gate1_bwd")
    dcat = mm_nt_rows(dy, lw["w_out"], tag + "w_out_dgrad")
    big["w_out"] = mm_tn(sv["cat"], dy, tag + "w_out_wgrad", col_sharded=False)
    if windowed:
        daux, small["pool_w"], small["pool_scale"] = pool_bwd(dcat, sv["proj"], lw["pool_w"], lw["pool_scale"],
                                                              ctx_len, tag + "pool_bwd")
    else:
        daux, small["conv_w"] = conv_bwd(dcat, sv["proj"], lw["conv_w"], ctx_len, tag + "conv_bwd")
    dq, dk, dv, dsink = attention_bwd(sv["q"], sv["k"], sv["v"], sv["cat"], sv["lse"], dcat, lw["sink"], ctx_len,
                                      windowed, tag + "attn_bwd")
    if windowed:
        small["sink"] = dsink[:, :GROUPS, 0].reshape(Q_HEADS)
    q_g = lw["q_g"] if between is None else lw["q_g"] + between(dq)[0, 0]
    dproj, small["q_g"], small["k_g"] = qk_prep_bwd(dq, dk, dv, daux, sv["proj"], q_g, lw["k_g"], cos, sin,
                                                    tag + "qk_bwd")
    dxn = mm_nt_cols(dproj, lw["w_in"], tag + "w_in_dgrad")
    big["w_in"] = mm_tn(sv["xn"], dproj, tag + "w_in_wgrad", col_sharded=True)
    dh0, dshift1, dscale1, small["norm1_g"] = norm_mod_bwd(dxn, sv["h"], lw["norm1_g"], mods["scale1"], dh1,
                                                           tag + "norm1_bwd", latent_only)
    return dh0, big, small, (dshift1, dscale1, dgate1)


def _dmod_rows(mixer_part, ffn_part):
    return jnp.concatenate([m[:, 0, :] for m in (*mixer_part, *ffn_part)], axis=1)


def _tie(mods, name, token):
    return {**mods, name: mods[name] + token[0, 0]}


def _place():
    x, y, c = lax.axis_index("x"), lax.axis_index("y"), lax.axis_index("c")
    chips = [(1 - x, y), (x, 1 - y), (1 - x, 1 - y)]
    return x, y, c, chips


def _remote(src, dst, send_sem, recv_sem, to):
    return pltpu.make_async_remote_copy(src_ref=src, dst_ref=dst, send_sem=send_sem, recv_sem=recv_sem,
                                        device_id=to, device_id_type=MESH)


ANY = pl.BlockSpec(memory_space=pl.ANY)


def all_gather8(x_shard, name):
    m_per, n = x_shard.shape

    def body(x_ref, out_ref, send_sems, recv_sems, local_sem):
        x, y, c, chips = _place()
        me, sibling = (x, y, c), (x, y, 1 - c)

        def rows(px, py, pc):
            return out_ref.at[pl.ds((4 * px + 2 * py + pc) * m_per, m_per), :]

        def copy(k, block, to, src=None):
            return _remote(rows(*block) if src is None else src, rows(*block), send_sems.at[k], recv_sems.at[k], to)

        mine = pltpu.make_async_copy(x_ref, rows(*me), local_sem)
        mine.start()
        first = [copy(0, me, sibling, src=x_ref)]
        first += [copy(1 + j, me, (*chip, c), src=x_ref) for j, chip in enumerate(chips)]
        for cp in first:
            cp.start()
        passed = [copy(4 + j, (*chip, c), sibling) for j, chip in enumerate(chips)]
        for j, chip in enumerate(chips):
            copy(1 + j, (*chip, c), me).wait_recv()
            passed[j].start()
        copy(0, sibling, me).wait_recv()
        for j, chip in enumerate(chips):
            copy(4 + j, (*chip, 1 - c), me).wait_recv()
        for cp in first + passed:
            cp.wait_send()
        mine.wait()

    return _pcall(
        body, name=name,
        out_shape=jax.ShapeDtypeStruct((N_DEV * m_per, n), x_shard.dtype),
        in_specs=[pl.BlockSpec(memory_space=pltpu.VMEM)],
        out_specs=pl.BlockSpec(memory_space=pltpu.VMEM),
        scratch_shapes=[pltpu.SemaphoreType.DMA((7,)), pltpu.SemaphoreType.DMA((7,)), pltpu.SemaphoreType.DMA],
        compiler_params=pltpu.CompilerParams(vmem_limit_bytes=VMEM_LIMIT),
    )(x_shard)


def _row_tile(rows, largest=256):
    for tr in (1024, 704, 512, 256, 128, 64, 32, 16):
        if tr <= largest and rows % tr == 0:
            return tr
    return rows


def place_shard(w, p, name):
    r, cols = w.shape
    tr = _row_tile(r, 1024)

    def body(p_ref, w_ref, o_ref):
        del p_ref
        o_ref[...] = w_ref[...].astype(BF16)

    return _pcall(
        body, name=name,
        grid_spec=pltpu.PrefetchScalarGridSpec(
            num_scalar_prefetch=1, grid=(r // tr,),
            in_specs=[pl.BlockSpec((tr, cols), lambda i, p_ref: (i, 0))],
            out_specs=pl.BlockSpec((None, tr, cols), lambda i, p_ref: (p_ref[0], i, 0))),
        out_shape=jax.ShapeDtypeStruct((N_SHARD, r, cols), BF16),
        compiler_params=_params("parallel"),
    )(p.reshape(1).astype(jnp.int32), w)


def gather_weight_shards(stacks, after):
    n = len(stacks)

    def body(*refs):
        outs = refs[n + 1:2 * n + 1]
        send_sems, recv_sems = refs[2 * n + 1:]
        x, y, c, chips = _place()
        p = 2 * x + y
        sibling = (x, y, 1 - c)
        started = []
        for i in range(n):
            hr = outs[i].shape[1] // 2
            mine = outs[i].at[p, pl.ds(c * hr, hr)]
            for j, chip in enumerate(chips):
                cp = _remote(mine, mine, send_sems.at[i, j], recv_sems.at[i, j], (*chip, c))
                cp.start()
                started.append(cp)
        for i in range(n):
            hr = outs[i].shape[1] // 2
            for j, (px, py) in enumerate(chips):
                half = outs[i].at[2 * px + py, pl.ds(c * hr, hr)]
                _remote(half, half, send_sems.at[i, j], recv_sems.at[i, j], (px, py, c)).wait_recv()
                fwd = _remote(half, half, send_sems.at[i, 3 + j], recv_sems.at[i, 3 + j], sibling)
                fwd.start()
                started.append(fwd)
        for i in range(n):
            hr = outs[i].shape[1] // 2
            for j, (px, py) in enumerate(chips):
                other = outs[i].at[2 * px + py, pl.ds((1 - c) * hr, hr)]
                _remote(other, other, send_sems.at[i, 3 + j], recv_sems.at[i, 3 + j], sibling).wait_recv()
        for cp in started:
            cp.wait_send()

    return _pcall(
        body, name="gather_weight_shards",
        out_shape=[jax.ShapeDtypeStruct(s.shape, s.dtype) for s in stacks],
        in_specs=[ANY] * (n + 1), out_specs=[ANY] * n,
        input_output_aliases={i: i for i in range(n)},
        scratch_shapes=[pltpu.SemaphoreType.DMA((n, 6)), pltpu.SemaphoreType.DMA((n, 6))],
    )(*stacks, after)


HBM_SPEC = pl.BlockSpec(memory_space=pltpu.HBM)
SEM_SPEC = pl.BlockSpec(memory_space=pltpu.SEMAPHORE)
EFFECT = pltpu.SideEffectType.DATAFLOW_SIDE_EFFECTING


def split_start(name, copies, n_sems, arrays, after):
    m = len(arrays)

    def body(*refs):
        for cp in copies(refs[:m], refs[m + 1], refs[m + 2]):
            cp.start()
        refs[-1][...] = jnp.zeros_like(refs[-1])

    res = _pcall(
        body, name=name,
        out_shape=(pltpu.SemaphoreType.DMA((n_sems,)), pltpu.SemaphoreType.DMA((n_sems,)),
                   *[pltpu.HBM(s.shape, s.dtype) for s in arrays], jax.ShapeDtypeStruct((8, 128), F32)),
        in_specs=[HBM_SPEC] * m + [ANY],
        out_specs=(SEM_SPEC, SEM_SPEC, *[HBM_SPEC] * m, pl.BlockSpec(memory_space=pltpu.VMEM)),
        input_output_aliases={i: 2 + i for i in range(m)},
        compiler_params=pltpu.CompilerParams(has_side_effects=EFFECT),
    )(*[pltpu.with_memory_space_constraint(s, pltpu.HBM) for s in arrays], after)
    return res[0], res[1], list(res[2:2 + m]), res[2 + m]


def split_wait(name, copies, send_sems, recv_sems, arrays, after):
    m = len(arrays)

    def body(*refs):
        for cp in copies(refs[:m], refs[m], refs[m + 1]):
            cp.wait_send()
            cp.wait_recv()

    return _pcall(
        body, name=name,
        out_shape=[pltpu.HBM(s.shape, s.dtype) for s in arrays],
        in_specs=[HBM_SPEC] * m + [SEM_SPEC, SEM_SPEC, ANY],
        out_specs=[HBM_SPEC] * m,
        input_output_aliases={i: i for i in range(m)},
        compiler_params=pltpu.CompilerParams(has_side_effects=EFFECT),
    )(*arrays, send_sems, recv_sems, after)


def gather_copies(n):
    def copies(stacks, send_sems, recv_sems):
        x, y, c, chips = _place()
        p = 2 * x + y
        return [_remote(stacks[i].at[p], stacks[i].at[p], send_sems.at[3 * i + j], recv_sems.at[3 * i + j], (*chip, c))
                for i in range(n) for j, chip in enumerate(chips)]
    return copies


def gather_half_copies(n):
    def copies(stacks, send_sems, recv_sems):
        x, y, c, chips = _place()
        p = 2 * x + y
        out = []
        for i in range(n):
            hr = stacks[i].shape[1] // 2
            mine = stacks[i].at[p, pl.ds(c * hr, hr)]
            out += [_remote(mine, mine, send_sems.at[3 * i + j], recv_sems.at[3 * i + j], (*chip, c))
                    for j, chip in enumerate(chips)]
        return out
    return copies


def forward_half_copies(n):
    def copies(stacks, send_sems, recv_sems):
        x, y, c, chips = _place()
        out = []
        for i in range(n):
            hr = stacks[i].shape[1] // 2
            for j, (px, py) in enumerate(chips):
                half = stacks[i].at[2 * px + py, pl.ds(c * hr, hr)]
                out.append(_remote(half, half, send_sems.at[3 * i + j], recv_sems.at[3 * i + j], (x, y, 1 - c)))
        return out
    return copies


def sibling_half_copies(n):
    def copies(refs, send_sems, recv_sems):
        x, y, c, _ = _place()
        out = []
        for i in range(n):
            hr = refs[n + i].shape[1]
            out.append(_remote(refs[i].at[:, pl.ds((1 - c) * hr, hr), :], refs[n + i], send_sems.at[i], recv_sems.at[i],
                               (x, y, 1 - c)))
        return out
    return copies


def chip_quarter_copies(n):
    def copies(refs, send_sems, recv_sems):
        _, _, c, chips = _place()
        return [_remote(refs[i].at[2 * px + py], refs[n + i].at[j], send_sems.at[3 * i + j], recv_sems.at[3 * i + j],
                        (px, py, c)) for i in range(n) for j, (px, py) in enumerate(chips)]
    return copies


def exchange_final_halves(grads, name):
    n = len(grads)

    def body(*refs):
        outs = refs[n:2 * n]
        send_sems, recv_sems = refs[2 * n:]
        x, y, c, _ = _place()
        sends = []
        for i in range(n):
            hr = outs[i].shape[0] // 2
            mine = outs[i].at[pl.ds(c * hr, hr)]
            cp = _remote(mine, mine, send_sems.at[i], recv_sems.at[i], (x, y, 1 - c))
            cp.start()
            sends.append(cp)
        for i in range(n):
            hr = outs[i].shape[0] // 2
            other = outs[i].at[pl.ds((1 - c) * hr, hr)]
            _remote(other, other, send_sems.at[i], recv_sems.at[i], (x, y, 1 - c)).wait_recv()
        for cp in sends:
            cp.wait_send()

    return _pcall(
        body, name=name,
        out_shape=[jax.ShapeDtypeStruct(g.shape, g.dtype) for g in grads],
        in_specs=[ANY] * n, out_specs=[ANY] * n,
        input_output_aliases={i: i for i in range(n)},
        scratch_shapes=[pltpu.SemaphoreType.DMA((n,)), pltpu.SemaphoreType.DMA((n,))],
    )(*grads)


def add_own_half(g, recv, c, name):
    s, r, cols = g.shape
    hr = r // 2
    tr = _row_tile(hr, 1024)
    nb = hr // tr

    def body(c_ref, g_ref, r_ref, o_ref):
        del c_ref
        o_ref[...] = (g_ref[...].astype(F32) + r_ref[...].astype(F32)).astype(BF16)

    return _pcall(
        body, name=name,
        grid_spec=pltpu.PrefetchScalarGridSpec(
            num_scalar_prefetch=1, grid=(s, nb),
            in_specs=[pl.BlockSpec((None, tr, cols), lambda j, i, c_ref: (j, c_ref[0] * nb + i, 0)),
                      pl.BlockSpec((None, tr, cols), lambda j, i, c_ref: (j, i, 0))],
            out_specs=pl.BlockSpec((None, tr, cols), lambda j, i, c_ref: (j, i, 0))),
        out_shape=jax.ShapeDtypeStruct((s, hr, cols), BF16),
        compiler_params=_params("parallel", "parallel"),
    )(c.reshape(1).astype(jnp.int32), g, recv)


def add_quarters(own, others, p, c, name, deps=()):
    _, hr, cols = own.shape
    tr = _row_tile(hr, 1024)
    nb = hr // tr

    def body(pc_ref, own_ref, oth_ref, *rest):
        o_ref = rest[-1]
        acc = own_ref[...].astype(F32) + oth_ref[0].astype(F32)
        acc = acc + oth_ref[1].astype(F32)
        o_ref[...] = acc + oth_ref[2].astype(F32)

    return _pcall(
        body, name=name,
        grid_spec=pltpu.PrefetchScalarGridSpec(
            num_scalar_prefetch=1, grid=(nb,),
            in_specs=[pl.BlockSpec((None, tr, cols), lambda i, pc: (pc[0], i, 0)),
                      pl.BlockSpec((3, tr, cols), lambda i, pc: (0, i, 0))] + [ANY] * len(deps),
            out_specs=pl.BlockSpec((tr, cols), lambda i, pc: (pc[1] * nb + i, 0))),
        out_shape=jax.ShapeDtypeStruct((2 * hr, cols), F32),
        compiler_params=_params("parallel"),
    )(jnp.stack([p, c]).astype(jnp.int32), own, others, *deps)


def _adamw_math(w, g, m, v):
    m2 = ADAM_B1 * m + (1.0 - ADAM_B1) * g
    v2 = ADAM_B2 * v + (1.0 - ADAM_B2) * (g * g)
    m_hat = m2 / (1.0 - ADAM_B1 ** ADAM_STEP)
    v_hat = v2 / (1.0 - ADAM_B2 ** ADAM_STEP)
    delta = -ADAM_LR * (m_hat / (jnp.sqrt(v_hat) + ADAM_EPS) + ADAM_WD * w)
    return delta, m2, v2


def adamw(w, g, m, v, name):
    r, cols = w.shape
    tr = _row_tile(r)

    def body(w_ref, g_ref, m_ref, v_ref, go_ref, d_ref, m2_ref, v2_ref):
        g = g_ref[...]
        go_ref[...] = g
        d_ref[...], m2_ref[...], v2_ref[...] = _adamw_math(w_ref[...], g, m_ref[...], v_ref[...])

    blk = pl.BlockSpec((tr, cols), lambda i: (i, 0))
    return _pcall(
        body, name=name, grid=(r // tr,),
        in_specs=[blk] * 4, out_specs=[blk] * 4,
        out_shape=[jax.ShapeDtypeStruct((r, cols), F32)] * 4,
        compiler_params=_params("parallel"),
    )(w, g, m, v)


def small_update(gathered, w, m, v):
    nd, r, lanes = gathered.shape

    def body(ga_ref, w_ref, m_ref, v_ref, g_ref, d_ref, m2_ref, v2_ref):
        g = ga_ref[0] + ga_ref[1]
        for dev in range(2, nd):
            g = g + ga_ref[dev]
        g_ref[...] = g
        d_ref[...], m2_ref[...], v2_ref[...] = _adamw_math(w_ref[...], g, m_ref[...], v_ref[...])

    return _pcall(
        body, name="small_update",
        out_shape=[jax.ShapeDtypeStruct((r, lanes), F32)] * 4,
        compiler_params=pltpu.CompilerParams(vmem_limit_bytes=VMEM_LIMIT),
    )(gathered, w, m, v)


def mod_fwd(c16, w_mod, b_mod, name):
    d, ns = w_mod.shape
    tn = 512

    def body(c_ref, w_ref, b_ref, o_ref):
        cv = c_ref[...]
        sc = (cv * _sigmoid(cv)).astype(BF16)
        o_ref[...] = _dot(sc, w_ref[...].astype(BF16)) + b_ref[...]

    return _pcall(
        body, name=name, grid=(ns // tn,),
        in_specs=[pl.BlockSpec((16, d), lambda j: (0, 0)), pl.BlockSpec((d, tn), lambda j: (0, j)),
                  pl.BlockSpec((1, tn), lambda j: (0, j))],
        out_specs=pl.BlockSpec((16, tn), lambda j: (0, j)),
        out_shape=jax.ShapeDtypeStruct((16, ns), F32),
        compiler_params=_params("parallel"),
    )(c16, w_mod, b_mod)


def wmod_update(c16, dmod16, w, m, v, name):
    d, ns = w.shape
    tn = 256

    def body(c_ref, dm_ref, w_ref, m_ref, v_ref, g_ref, d_ref, m2_ref, v2_ref):
        cv = c_ref[...]
        sc = (cv * _sigmoid(cv)).astype(BF16)
        g = _dot_tn(sc, dm_ref[...].astype(BF16))
        g_ref[...] = g
        d_ref[...], m2_ref[...], v2_ref[...] = _adamw_math(w_ref[...], g, m_ref[...], v_ref[...])

    blk = pl.BlockSpec((d, tn), lambda j: (0, j))
    return _pcall(
        body, name=name, grid=(ns // tn,),
        in_specs=[pl.BlockSpec((16, d), lambda j: (0, 0)), pl.BlockSpec((16, tn), lambda j: (0, j)), blk, blk, blk],
        out_specs=[blk] * 4,
        out_shape=[jax.ShapeDtypeStruct((d, ns), F32)] * 4,
        compiler_params=_params("parallel"),
    )(c16, dmod16, w, m, v)


def cctx_partial(dm0, w0, dm1, w1):
    d, ns = w0.shape
    tn = 512

    def body(dm0_ref, w0_ref, dm1_ref, w1_ref, o_ref):
        @pl.when(pl.program_id(0) == 0)
        def _():
            o_ref[...] = jnp.zeros_like(o_ref)

        for dm_ref, w_ref in ((dm0_ref, w0_ref), (dm1_ref, w1_ref)):
            tot = jnp.sum(dm_ref[...], axis=0, keepdims=True)
            lhs = jnp.broadcast_to(tot, (8, tn)).astype(BF16)
            o_ref[...] += _dot_nt(lhs, w_ref[...].astype(BF16))

    dspec = pl.BlockSpec((8, tn), lambda j: (0, j))
    wspec = pl.BlockSpec((d, tn), lambda j: (0, j))
    return _pcall(
        body, name="cctx_partial", grid=(ns // tn,),
        in_specs=[dspec, wspec, dspec, wspec],
        out_specs=pl.BlockSpec((8, d), lambda j: (0, 0)),
        out_shape=jax.ShapeDtypeStruct((8, d), F32),
        compiler_params=_params("arbitrary"),
    )(dm0, w0, dm1, w1)


def cctx_update(parts, c_ctx, m, v):
    d = c_ctx.shape[1]

    def body(p_ref, c_ref, m_ref, v_ref, g_ref, d_ref, m2_ref, v2_ref):
        tot = p_ref[0:1, :] + p_ref[1:2, :]
        tot = tot + p_ref[2:3, :]
        tot = tot + p_ref[3:4, :]
        cv = c_ref[...]
        sg = _sigmoid(cv)
        g = tot * (sg * (1.0 + cv * (1.0 - sg)))
        g_ref[...] = g
        d_ref[...], m2_ref[...], v2_ref[...] = _adamw_math(cv, g, m_ref[...], v_ref[...])

    return _pcall(
        body, name="cctx_update",
        out_shape=[jax.ShapeDtypeStruct((1, d), F32)] * 4,
    )(parts, c_ctx, m, v)


WEIGHT_NAMES = (
    "c_ctx", "l0_norm1_g", "l0_w_mod", "l0_b_mod", "l0_w_in", "l0_q_norm_g", "l0_k_norm_g", "l0_conv_w", "l0_w_out",
    "l0_norm2_g", "l0_w_gate", "l0_w_up", "l0_w_down", "l1_norm1_g", "l1_w_mod", "l1_b_mod", "l1_w_in",
    "l1_q_norm_g", "l1_k_norm_g", "l1_sink", "l1_pool_w", "l1_pool_scale", "l1_w_out", "l1_norm2_g", "l1_w_gate",
    "l1_w_up", "l1_w_down", "final_norm_g")
BIG_NAMES = ("w_in", "w_out", "w_gate", "w_up", "w_down")
SMALL_SLOTS = tuple(
    [(f"l{l}_{nm}", par and f"l{l}_{nm}") for l in (0, 1)
     for nm, par in (("dmod_lat", False), ("dmod_ctx", False), ("b_mod", True), ("norm1_g", True), ("norm2_g", True),
                     ("q_norm_g", True), ("k_norm_g", True))]
    + [("l0_conv_w", None), ("l1_sink", "l1_sink"), ("l1_pool_w", "l1_pool_w"), ("l1_pool_scale", "l1_pool_scale"),
       ("final_norm_g", "final_norm_g")])


def _pack(values, sizes):
    parts = []
    for (name, _), size in zip(SMALL_SLOTS, sizes):
        v = values.get(name)
        padded = -(-size // 128) * 128
        v = jnp.zeros((padded,), F32) if v is None else jnp.pad(v.reshape(-1).astype(F32), (0, padded - size))
        parts.append(v)
    total = sum(p.shape[0] for p in parts)
    parts.append(jnp.zeros((-(-total // 1024) * 1024 - total,), F32))
    return jnp.concatenate(parts).reshape(-1, 128)


def _offsets(sizes):
    offs, o = {}, 0
    for (name, _), size in zip(SMALL_SLOTS, sizes):
        offs[name] = (o, size)
        o += -(-size // 128) * 128
    return offs


def kernel(x, c, ctx, c_ctx, l0_norm1_g, l0_w_mod, l0_b_mod, l0_w_in, l0_q_norm_g, l0_k_norm_g, l0_conv_w, l0_w_out, l0_norm2_g, l0_w_gate, l0_w_up, l0_w_down, l1_norm1_g, l1_w_mod, l1_b_mod, l1_w_in, l1_q_norm_g, l1_k_norm_g, l1_sink, l1_pool_w, l1_pool_scale, l1_w_out, l1_norm2_g, l1_w_gate, l1_w_up, l1_w_down, final_norm_g, loss_target, m_c_ctx, m_l0_norm1_g, m_l0_w_mod, m_l0_b_mod, m_l0_w_in, m_l0_q_norm_g, m_l0_k_norm_g, m_l0_conv_w, m_l0_w_out, m_l0_norm2_g, m_l0_w_gate, m_l0_w_up, m_l0_w_down, m_l1_norm1_g, m_l1_w_mod, m_l1_b_mod, m_l1_w_in, m_l1_q_norm_g, m_l1_k_norm_g, m_l1_sink, m_l1_pool_w, m_l1_pool_scale, m_l1_w_out, m_l1_norm2_g, m_l1_w_gate, m_l1_w_up, m_l1_w_down, m_final_norm_g, v_c_ctx, v_l0_norm1_g, v_l0_w_mod, v_l0_b_mod, v_l0_w_in, v_l0_q_norm_g, v_l0_k_norm_g, v_l0_conv_w, v_l0_w_out, v_l0_norm2_g, v_l0_w_gate, v_l0_w_up, v_l0_w_down, v_l1_norm1_g, v_l1_w_mod, v_l1_b_mod, v_l1_w_in, v_l1_q_norm_g, v_l1_k_norm_g, v_l1_sink, v_l1_pool_w, v_l1_pool_scale, v_l1_w_out, v_l1_norm2_g, v_l1_w_gate, v_l1_w_up, v_l1_w_down, v_final_norm_g):
    a = dict(locals())
    xi, yi, ci = lax.axis_index("x"), lax.axis_index("y"), lax.axis_index("c")
    p = 2 * xi + yi
    me = 4 * xi + 2 * yi + ci
    d = x.shape[-1]
    conv_cols = l0_conv_w.shape[1]

    row0 = jnp.concatenate([c, jnp.pad(l0_conv_w, ((0, 0), (0, d - conv_cols))), jnp.zeros((4, d), F32)], axis=0)
    gathered = all_gather8(row0, "gather_cond").reshape(N_DEV, 8, d)
    c_all = gathered[:, 0]
    conv_w = gathered[0::2, 1:4, :conv_cols].transpose(1, 0, 2).reshape(3, N_SHARD * conv_cols)
    c16_fwd = jnp.concatenate([c_all, c_ctx[None], jnp.zeros((7, d), F32)], axis=0)
    c16_bwd = jnp.concatenate([c_all, jnp.broadcast_to(c_ctx[None], (8, d))], axis=0)

    ns_mod = l0_w_mod.shape[1]
    mod_parts = [mod_fwd(c16_fwd, a[f"l{l}_w_mod"], lax.dynamic_slice(a[f"l{l}_b_mod"], (p * ns_mod,), (ns_mod,))[None],
                         f"l{l}_mod_fwd") for l in (0, 1)]
    modg = all_gather8(jnp.concatenate(mod_parts, axis=0), "gather_mod").reshape(N_DEV, 2, 16, ns_mod)[0::2]
    mod_full = modg.transpose(1, 2, 0, 3).reshape(2, 16, N_SHARD * ns_mod)
    mods = []
    for l in (0, 1):
        lat = lax.dynamic_index_in_dim(mod_full[l], me, axis=0, keepdims=False)
        cx = mod_full[l, 8]
        mods.append({nm: jnp.stack([cx[j * d:(j + 1) * d], lat[j * d:(j + 1) * d]])[:, None, :]
                     for j, nm in enumerate(MOD_NAMES)})

    placed = {(l, nm): place_shard(a[f"l{l}_{nm}"], p, f"l{l}_{nm}_place") for l in (0, 1) for nm in BIG_NAMES}
    ready = {(0, "w_in"): gather_weight_shards([placed[0, "w_in"]], mod_full)[0]}
    groups = (((0, "w_out"), (0, "w_gate"), (0, "w_up")), ((0, "w_down"),), ((1, "w_in"), (1, "w_out")),
              ((1, "w_gate"), (1, "w_up"), (1, "w_down")))
    staged = len(groups) - 1
    flights, after = [], ready[0, "w_in"]
    for gi, grp in enumerate(groups):
        first_copies = gather_half_copies if gi == staged else gather_copies
        flights.append(split_start(f"gather_start_{gi}", first_copies(len(grp)), 3 * len(grp),
                                   [placed[k] for k in grp], after))
        after = flights[-1][3]

    def forward_staged(after):
        n = len(groups[staged])
        send_sems, recv_sems, in_flight, _ = flights[staged]
        landed = split_wait(f"gather_wait_{staged}", gather_half_copies(n), send_sems, recv_sems, in_flight, after)
        flights[staged] = split_start("gather_forward_start", forward_half_copies(n), 3 * n, landed, after)
        return flights[staged][3]

    def weight_of(l):
        def weight(nm, after):
            if (l, nm) not in ready:
                gi = next(i for i, grp in enumerate(groups) if (l, nm) in grp)
                send_sems, recv_sems, in_flight, _ = flights[gi]
                name, copies = ("gather_forward_wait", forward_half_copies) if gi == staged else (
                    f"gather_wait_{gi}", gather_copies)
                landed = split_wait(name, copies(len(groups[gi])), send_sems, recv_sems, in_flight, after)
                ready.update(zip(groups[gi], landed))
            return ready[l, nm]
        return weight

    layers = [dict(norm1_g=a[f"l{l}_norm1_g"][None], norm2_g=a[f"l{l}_norm2_g"][None],
                   q_g=a[f"l{l}_q_norm_g"][None], k_g=a[f"l{l}_k_norm_g"][None]) for l in (0, 1)]
    layers[0].update(conv_w=conv_w, sink=jnp.zeros((Q_HEADS,), F32))
    layers[1].update(sink=l1_sink, pool_w=l1_pool_w, pool_scale=l1_pool_scale[None])

    ctx_len = ctx.shape[1]
    h = jnp.concatenate([ctx[0], x[0]], axis=0)
    tabs = _rope_tables(h.shape[0], ctx_len)
    h, saved0 = _layer_fwd(h, layers[0], weight_of(0), _tie(mods[0], "shift1", flights[-1][3]), tabs, ctx_len, False,
                           "l0_")
    h, saved1 = _layer_fwd(h, layers[1], weight_of(1), mods[1], tabs, ctx_len, True, "l1_", forward_staged)
    loss_tile, dh, dfg = final_loss(h, final_norm_g[None], loss_target[0])

    keys = [[f"l{l}_{nm}" for nm in BIG_NAMES] for l in (0, 1)]
    n_big = len(BIG_NAMES)
    dh, ffn_big, ffn_small, ffn_mod = _layer_bwd_ffn(dh, saved1, layers[1], mods[1], "l1_")
    dh, mix_big, mix_small, mix_mod = _layer_bwd_mixer(dh, saved1, layers[1], mods[1], tabs, ctx_len, True, "l1_")
    bigs, smalls, dmods = [None, {**ffn_big, **mix_big}], [None, {**ffn_small, **mix_small}], [None, None]
    dmods[1] = _dmod_rows(mix_mod, ffn_mod)
    grads1 = [bigs[1][nm] for nm in BIG_NAMES]
    lands = [lax.empty((g.shape[0], g.shape[1] // 2, g.shape[2]), g.dtype) for g in grads1]
    send_a, recv_a, fly_a, token_a = split_start("l1_sibling_halves_start", sibling_half_copies(n_big), n_big,
                                                 grads1 + lands, dh)
    dh, ffn_big, ffn_small, ffn_mod = _layer_bwd_ffn(dh, saved0, layers[0], _tie(mods[0], "gate2", token_a), "l0_")
    fly_a = split_wait("l1_sibling_halves_wait", sibling_half_copies(n_big), send_a, recv_a, fly_a, dh)
    chip_sums1 = [add_own_half(g, r, ci, k + "_add_halves")
                  for k, g, r in zip(keys[1], fly_a[:n_big], fly_a[n_big:])]
    lands = [lax.empty((3,) + s.shape[1:], s.dtype) for s in chip_sums1]
    send_b, recv_b, fly_b, token_b = split_start("l1_chip_quarters_start", chip_quarter_copies(n_big), 3 * n_big,
                                                 chip_sums1 + lands, dh)
    ffn_names = ("w_gate", "w_up", "w_down")
    n_ffn = len(ffn_names)
    ffn_keys = ["l0_" + nm for nm in ffn_names]
    grads0f = [ffn_big[nm] for nm in ffn_names]
    lands = [lax.empty((g.shape[0], g.shape[1] // 2, g.shape[2]), g.dtype) for g in grads0f]
    send_c, recv_c, fly_c, token_c = split_start("l0_ffn_sibling_halves_start", sibling_half_copies(n_ffn), n_ffn,
                                                 grads0f + lands, token_b)
    ffn_flight = []

    def between(dq):
        landed = split_wait("l0_ffn_sibling_halves_wait", sibling_half_copies(n_ffn), send_c, recv_c, fly_c, dq)
        sums = [add_own_half(g, r, ci, k + "_add_halves") for k, g, r in zip(ffn_keys, landed[:n_ffn], landed[n_ffn:])]
        slots = [lax.empty((3,) + s.shape[1:], s.dtype) for s in sums]
        ffn_flight.extend(split_start("l0_ffn_chip_quarters_start", chip_quarter_copies(n_ffn), 3 * n_ffn,
                                      sums + slots, dq))
        return ffn_flight[3]

    dh, mix_big, mix_small, mix_mod = _layer_bwd_mixer(dh, saved0, layers[0], _tie(mods[0], "gate1", token_c), tabs,
                                                       ctx_len, False, "l0_", between, latent_only=True)
    fly_b = split_wait("l1_chip_quarters_wait", chip_quarter_copies(n_big), send_b, recv_b, fly_b, dh)
    fly_d = split_wait("l0_ffn_chip_quarters_wait", chip_quarter_copies(n_ffn), ffn_flight[0], ffn_flight[1],
                       ffn_flight[2], dh)
    bigs[0], smalls[0], dmods[0] = {**ffn_big, **mix_big}, {**ffn_small, **mix_small}, _dmod_rows(mix_mod, ffn_mod)
    dx = dh
    chip_sums = dict(zip(keys[1] + ffn_keys, list(fly_b[:n_big]) + list(fly_d[:n_ffn])))
    quarters = dict(zip(keys[1] + ffn_keys, list(fly_b[n_big:]) + list(fly_d[n_ffn:])))

    partial = {"l0_conv_w": smalls[0]["conv_w"], "l1_sink": smalls[1]["sink"], "l1_pool_w": smalls[1]["pool_w"],
               "l1_pool_scale": smalls[1]["pool_scale"], "final_norm_g": dfg}
    for l in (0, 1):
        partial.update({f"l{l}_dmod_lat": dmods[l][1], f"l{l}_dmod_ctx": dmods[l][0],
                        f"l{l}_b_mod": dmods[l][0] + dmods[l][1], f"l{l}_norm1_g": smalls[l]["norm1_g"],
                        f"l{l}_norm2_g": smalls[l]["norm2_g"], f"l{l}_q_norm_g": smalls[l]["q_g"],
                        f"l{l}_k_norm_g": smalls[l]["k_g"]})
    sizes = [int(np.prod(partial[name].shape)) for name, _ in SMALL_SLOTS]
    offs = _offsets(sizes)

    last_keys = ["l0_w_in", "l0_w_out"]
    grads_last = [bigs[0]["w_in"], bigs[0]["w_out"]]
    lands = [lax.empty((g.shape[0], g.shape[1] // 2, g.shape[2]), g.dtype) for g in grads_last]
    send_e, recv_e, fly_e, token_e = split_start("l0_last_sibling_halves_start", sibling_half_copies(2), 2,
                                                 grads_last + lands, dh)
    gpack = _pack(partial, sizes) + token_e[0:1, 0:1]
    rows = gpack.shape[0]
    small_all = all_gather8(gpack, "gather_small").reshape(N_DEV, rows, 128)
    packs = [_pack({name: a[pre + par] for name, par in SMALL_SLOTS if par}, sizes) for pre in ("", "m_", "v_")]
    small_out = [t.reshape(-1) for t in small_update(small_all, *packs)]
    flat_all = small_all.reshape(N_DEV, rows * 128)

    def slot(flat, name, shape):
        o, size = offs[name]
        return flat[o:o + size].reshape(shape)

    results = {}
    for name, par in SMALL_SLOTS:
        if par:
            results[par] = tuple(slot(t, name, a[par].shape) for t in small_out)

    ctx_rows = []
    for l in (0, 1):
        o_lat, o_ctx = offs[f"l{l}_dmod_lat"][0], offs[f"l{l}_dmod_ctx"][0]
        lat = lax.dynamic_slice(flat_all, (0, o_lat + p * ns_mod), (N_DEV, ns_mod))
        cxr = lax.dynamic_slice(flat_all, (0, o_ctx + p * ns_mod), (N_DEV, ns_mod))
        ctx_rows.append(cxr)
        results[f"l{l}_w_mod"] = tuple(wmod_update(c16_bwd, jnp.concatenate([lat, cxr], axis=0), a[f"l{l}_w_mod"],
                                                   a[f"m_l{l}_w_mod"], a[f"v_l{l}_w_mod"], f"l{l}_w_mod_update"))
    part = cctx_partial(ctx_rows[0], l0_w_mod, ctx_rows[1], l1_w_mod)
    parts4 = all_gather8(part, "gather_cctx").reshape(N_DEV, 8, d)[0::2, 0]
    results["c_ctx"] = tuple(t[0] for t in cctx_update(parts4, c_ctx[None], m_c_ctx[None], v_c_ctx[None]))

    gconv = lax.dynamic_slice(slot(small_out[0], "l0_conv_w", (3, N_SHARD * conv_cols)), (0, p * conv_cols),
                              (3, conv_cols))
    results["l0_conv_w"] = tuple(adamw(l0_conv_w, gconv, m_l0_conv_w, v_l0_conv_w, "l0_conv_w_adamw"))

    fly_e = split_wait("l0_last_sibling_halves_wait", sibling_half_copies(2), send_e, recv_e, fly_e,
                       results["c_ctx"][0])
    sums_last = [add_own_half(g, r, ci, k + "_add_halves") for k, g, r in zip(last_keys, fly_e[:2], fly_e[2:])]
    slots = [lax.empty((3,) + s.shape[1:], s.dtype) for s in sums_last]
    send_f, recv_f, fly_f, token_f = split_start("l0_last_chip_quarters_start", chip_quarter_copies(2), 6,
                                                 sums_last + slots, sums_last[0])

    def finish(names, tag, deps):
        halves = [add_quarters(chip_sums[k], quarters[k], p, ci, k + "_add_quarters", deps) for k in names]
        for k, g in zip(names, exchange_final_halves(halves, "exchange_final_halves_" + tag)):
            results[k] = tuple(adamw(a[k], g, a["m_" + k], a["v_" + k], k + "_adamw"))

    done_keys = keys[1] + ffn_keys
    finish(done_keys, "first", (token_f,))
    fly_f = split_wait("l0_last_chip_quarters_wait", chip_quarter_copies(2), send_f, recv_f, fly_f,
                       results[done_keys[-1]][1])
    chip_sums.update(zip(last_keys, fly_f[:2]))
    quarters.update(zip(last_keys, fly_f[2:]))
    finish(last_keys, "last", ())

    loss = lax.psum(loss_tile[0, 0], ("x", "y", "c"))
    out = [loss, dx[None]]
    for j in range(4):
        out += [results[k][j] for k in WEIGHT_NAMES]
    return tuple(out)
```
